```python
import jax, jax.numpy as jnp
from jax import lax
import numpy as np

D_MODEL = 1024
BATCH = 8
SEQ = 2048
DEPTH = 1

PLE_DIM = 256
A_WIDTH = 512
A_GROUPS = 8
A_CHUNK = 128
B_HEAD_DIM = 64
B_HEADS = 8
B_CONFIGS = ((128, 1), (512, 4), (2048, 16))
B_GROUPS = len(B_CONFIGS)
B_WIDTH = B_HEADS * B_HEAD_DIM
N_EXPERT_GROUPS = 4
EXPERTS_PER_GROUP = 8
EXPERT_TOPK = 2
D_EXPERT = 256
DEEPNORM_ALPHA = (2 * DEPTH) ** 0.25
DEEPNORM_BETA = (8 * DEPTH) ** -0.25
LN_EPS = 1e-5
IN_A = 2 * A_WIDTH
IN_B = 3 * B_GROUPS * B_WIDTH
IN_G = 2 * D_MODEL
IN_TOTAL = IN_A + IN_B + IN_G

kernel_name = 'hybrid_gmlp_dilated_attn_hmoe_block'


def layer_norm(x, g, b):
    xf = x.astype(jnp.float32)
    mu = jnp.mean(xf, axis=-1, keepdims=True)
    var = jnp.mean(jnp.square(xf - mu), axis=-1, keepdims=True)
    return ((xf - mu) * lax.rsqrt(var + LN_EPS) * g + b).astype(x.dtype)


def chunked_spatial_gating(z, ln_g, ln_b, w_s, b_s):
    bsz, s, _ = z.shape
    u, v = jnp.split(z, 2, axis=-1)
    v = layer_norm(v, ln_g, ln_b)
    v = v.reshape(bsz, s // A_CHUNK, A_CHUNK, A_GROUPS, A_WIDTH // A_GROUPS)
    causal = jnp.tril(jnp.ones((A_CHUNK, A_CHUNK), dtype=bool))
    w = jnp.where(causal, w_s, jnp.zeros_like(w_s))
    mixed = jnp.einsum('gts,bcsgd->bctgd', w, v) + b_s.T[None, None, :, :, None]
    return u * mixed.reshape(bsz, s, A_WIDTH)


def dilated_causal_attention(q, k, v, window, dilation):
    bsz, s, h, dh = q.shape
    span = window // dilation
    n_sub = s // dilation
    nb = -(-n_sub // span)
    n_pad = nb * span

    def to_residue(t):
        t = t.reshape(bsz, n_sub, dilation, h, dh)
        return jnp.pad(t, ((0, 0), (0, n_pad - n_sub), (0, 0), (0, 0), (0, 0)))

    def windows(t):
        t = jnp.pad(to_residue(t), ((0, 0), (span, 0), (0, 0), (0, 0), (0, 0)))
        t = t.reshape(bsz, nb + 1, span, dilation, h, dh)
        return jnp.concatenate([t[:, :-1], t[:, 1:]], axis=2)

    qb = to_residue(q).reshape(bsz, nb, span, dilation, h, dh)
    kw, vw = windows(k), windows(v)
    scores = jnp.einsum('bnqrhd,bnkrhd->bnrhqk', qb, kw).astype(jnp.float32) * (dh ** -0.5)
    qi = jnp.arange(span)[:, None]
    ki = jnp.arange(2 * span)[None, :]
    dist = span + qi - ki
    kpos = (jnp.arange(nb)[:, None, None] - 1) * span + ki[None]
    valid = (dist >= 0) & (dist <= span) & (kpos >= 0)
    scores = jnp.where(valid[None, :, None, None], scores, -jnp.inf)
    lse = jax.nn.logsumexp(scores, axis=-1)
    probs = jnp.exp(scores - lse[..., None]).astype(v.dtype)
    out = jnp.einsum('bnrhqk,bnkrhd->bnqrhd', probs, vw)
    out = out.reshape(bsz, n_pad, dilation, h, dh)[:, :n_sub].reshape(bsz, s, h, dh)
    lse = lse.transpose(0, 1, 4, 2, 3).reshape(bsz, n_pad, dilation, h)[:, :n_sub].reshape(bsz, s, h)
    return out, lse


def token_mixer(x, w_in, a_ln_g, a_ln_b, a_ws, a_bs, w_a_proj, w_b_proj, w_o):
    bsz, s, _ = x.shape
    proj = x @ w_in
    za, zb, zg = jnp.split(proj, [IN_A, IN_A + IN_B], axis=-1)
    y_a = chunked_spatial_gating(jax.nn.gelu(za), a_ln_g, a_ln_b, a_ws, a_bs) @ w_a_proj
    qkv = zb.reshape(bsz, s, 3, B_GROUPS, B_HEADS, B_HEAD_DIM)
    outs, lses = [], []
    for gi, (window, dilation) in enumerate(B_CONFIGS):
        o, l = dilated_causal_attention(qkv[:, :, 0, gi], qkv[:, :, 1, gi], qkv[:, :, 2, gi], window, dilation)
        outs.append(o)
        lses.append(l)
    weights = jax.nn.softmax(jnp.stack(lses), axis=0)
    o_b = jnp.einsum('gbsh,gbshd->bshd', weights.astype(outs[0].dtype), jnp.stack(outs))
    y_b = o_b.reshape(bsz, s, B_WIDTH) @ w_b_proj
    gate_a, gate_b = jnp.split(jax.nn.sigmoid(zg), 2, axis=-1)
    return (gate_a * y_a + gate_b * y_b) @ w_o


def hierarchical_moe(x, w_group_router, b_group_router, w_expert_router, b_expert_router, w_gate, w_up, w_down):
    glogits = (x @ w_group_router).astype(jnp.float32) + b_group_router
    gprob = jax.nn.softmax(glogits, axis=-1)
    gmask = jax.nn.one_hot(jnp.argmax(glogits, axis=-1), N_EXPERT_GROUPS, dtype=jnp.float32)
    elogits = jnp.einsum('nd,dge->nge', x, w_expert_router).astype(jnp.float32) + b_expert_router
    top_v, top_i = lax.top_k(elogits, EXPERT_TOPK)
    top_w = jax.nn.softmax(top_v, axis=-1)
    ew = jnp.sum(jax.nn.one_hot(top_i, EXPERTS_PER_GROUP, dtype=jnp.float32) * top_w[..., None], axis=-2)
    combine = (gprob * gmask)[..., None] * ew
    y = jnp.zeros_like(x)
    for g in range(N_EXPERT_GROUPS):
        hid = jax.nn.silu(jnp.einsum('nd,edf->nef', x, w_gate[g])) * jnp.einsum('nd,edf->nef', x, w_up[g])
        hid = hid * combine[:, g, :, None].astype(x.dtype)
        y = y + jnp.einsum('nef,efd->nd', hid, w_down[g])
    return y


def setup_inputs(seed: int = 0) -> dict:
    key = jax.random.key(seed)
    ks = jax.random.split(key, 24)
    f32 = jnp.float32

    def nrm(k, shape, scale):
        return jax.random.normal(k, shape, f32) * scale

    L = DEPTH
    G, E = N_EXPERT_GROUPS, EXPERTS_PER_GROUP
    return {
        'x': nrm(ks[0], (BATCH, SEQ, D_MODEL), 1.0),
        'p': nrm(ks[1], (DEPTH, BATCH, SEQ, PLE_DIM), 1.0),
        'w_in': nrm(ks[2], (L, D_MODEL, IN_TOTAL), D_MODEL ** -0.5),
        'a_ln_g': 1.0 + nrm(ks[3], (L, A_WIDTH), 0.02),
        'a_ln_b': nrm(ks[4], (L, A_WIDTH), 0.02),
        'a_ws': nrm(ks[5], (L, A_GROUPS, A_CHUNK, A_CHUNK), A_CHUNK ** -0.5),
        'a_bs': 1.0 + nrm(ks[6], (L, A_GROUPS, A_CHUNK), 0.02),
        'w_a_proj': nrm(ks[7], (L, A_WIDTH, D_MODEL), A_WIDTH ** -0.5),
        'w_b_proj': nrm(ks[8], (L, B_WIDTH, D_MODEL), B_WIDTH ** -0.5),
        'w_o': nrm(ks[9], (L, D_MODEL, D_MODEL), D_MODEL ** -0.5 * DEEPNORM_BETA),
        'ln1_g': 1.0 + nrm(ks[10], (L, D_MODEL), 0.02),
        'ln1_b': nrm(ks[11], (L, D_MODEL), 0.02),
        'w_group_router': nrm(ks[12], (L, D_MODEL, G), D_MODEL ** -0.5),
        'b_group_router': nrm(ks[13], (L, G), 0.01),
        'w_expert_router': nrm(ks[14], (L, D_MODEL, G, E), D_MODEL ** -0.5),
        'b_expert_router': nrm(ks[15], (L, G, E), 0.01),
        'w_gate': nrm(ks[16], (L, G, E, D_MODEL, D_EXPERT), D_MODEL ** -0.5),
        'w_up': nrm(ks[17], (L, G, E, D_MODEL, D_EXPERT), D_MODEL ** -0.5),
        'w_down': nrm(ks[18], (L, G, E, D_EXPERT, D_MODEL), D_EXPERT ** -0.5 * DEEPNORM_BETA),
        'w_ple': nrm(ks[19], (L, PLE_DIM, D_MODEL), PLE_DIM ** -0.5),
        'w_ple_gate': nrm(ks[20], (L, D_MODEL, D_MODEL), D_MODEL ** -0.5),
        'ln2_g': 1.0 + nrm(ks[21], (L, D_MODEL), 0.02),
        'ln2_b': nrm(ks[22], (L, D_MODEL), 0.02),
    }


def reference(x, p, w_in, a_ln_g, a_ln_b, a_ws, a_bs, w_a_proj, w_b_proj, w_o, ln1_g, ln1_b,
              w_group_router, b_group_router, w_expert_router, b_expert_router, w_gate, w_up, w_down,
              w_ple, w_ple_gate, ln2_g, ln2_b):
    bsz, s, d = x.shape
    for i in range(DEPTH):
        mix = token_mixer(x, w_in[i], a_ln_g[i], a_ln_b[i], a_ws[i], a_bs[i], w_a_proj[i], w_b_proj[i], w_o[i])
        x = layer_norm(DEEPNORM_ALPHA * x + mix, ln1_g[i], ln1_b[i])
        ffn = hierarchical_moe(x.reshape(bsz * s, d), w_group_router[i], b_group_router[i], w_expert_router[i],
                               b_expert_router[i], w_gate[i], w_up[i], w_down[i]).reshape(bsz, s, d)
        ple = (p[i] @ w_ple[i]) * jax.nn.sigmoid(x @ w_ple_gate[i])
        x = layer_norm(DEEPNORM_ALPHA * x + ffn + ple, ln2_g[i], ln2_b[i])
    return x
```

```python
import functools

import jax
import jax.numpy as jnp
from jax import lax
from jax.experimental import pallas as pl
from jax.experimental.pallas import tpu as pltpu

F32 = jnp.float32
BF16 = jnp.bfloat16
U32 = jnp.uint32
I32 = jnp.int32

D_MODEL = 1024
PLE_DIM = 256
A_WIDTH = 512
A_CHUNK = 128
B_HEAD_DIM = 64
B_HEADS = 8
B_WIDTH = 512
B_DILATIONS = (1, 4, 16)
SPAN = 128
N_GROUPS = 4
N_EXPERTS = 8
N_EXPERTS_TOTAL = N_GROUPS * N_EXPERTS
D_EXPERT = 256
DEEPNORM_ALPHA = 2.0 ** 0.25
LN_EPS = 1e-5
COL = 512
NEG = -1e30

VMEM_LIMIT = 56 * 1024 * 1024

TM_PROJ = 512
TM_MIX = 256
TM_MOE = 256
TM_FIN = 256


def _ln(x, g, b):
    mu = jnp.mean(x, axis=-1, keepdims=True)
    xc = x - mu
    var = jnp.mean(xc * xc, axis=-1, keepdims=True)
    return xc * lax.rsqrt(var + LN_EPS) * g + b


def _gelu_tanh(x):
    return 0.5 * x * (1.0 + jnp.tanh(0.7978845608028654 * (x + 0.044715 * (x * x * x))))


def _sigmoid(x):
    return 1.0 / (1.0 + jnp.exp(-x))


def _dot(a, b):
    return jnp.dot(a, b, preferred_element_type=F32)


def _proj_nat_kernel(x_ref, wu_ref, wv_ref, wq_ref, wk_ref, wvv_ref, wg0_ref, wg1_ref, wg2_ref, wg3_ref,
                     lng_ref, lnb_ref, ws_ref, bias_ref, ga_ref, qkv_ref, gates_ref):
    tm = x_ref.shape[0]
    xb = x_ref[...].astype(BF16)

    u = _gelu_tanh(_dot(xb, wu_ref[...]))
    v = _gelu_tanh(_dot(xb, wv_ref[...]))
    vn = _ln(v, lng_ref[...], lnb_ref[...]).astype(BF16)

    row = lax.broadcasted_iota(I32, (A_CHUNK, A_CHUNK), 0)
    colm = lax.broadcasted_iota(I32, (A_CHUNK, A_CHUNK), 1)
    causal = colm <= row
    lo = colm < 64
    zero = jnp.zeros((A_CHUNK, A_CHUNK), BF16)
    wcat = []
    for j in range(4):
        w0 = jnp.where(causal, ws_ref[2 * j], 0.0).astype(BF16)
        w1 = jnp.where(causal, ws_ref[2 * j + 1], 0.0).astype(BF16)
        wcat.append(jnp.concatenate([w0, w1], axis=1))
    for c in range(tm // A_CHUNK):
        r0 = c * A_CHUNK
        for j in range(4):
            c0 = j * 128
            vt = vn[r0:r0 + A_CHUNK, c0:c0 + 128]
            rhs = jnp.concatenate([jnp.where(lo, vt, zero), jnp.where(lo, zero, vt)], axis=0)
            mixed = _dot(wcat[j], rhs) + bias_ref[:, c0:c0 + 128]
            ga_ref[r0:r0 + A_CHUNK, c0:c0 + 128] = (u[r0:r0 + A_CHUNK, c0:c0 + 128] * mixed).astype(BF16)

    qkv_ref[:, 0:COL] = _dot(xb, wq_ref[...]).astype(BF16)
    qkv_ref[:, COL:2 * COL] = _dot(xb, wk_ref[...]).astype(BF16)
    qkv_ref[:, 2 * COL:3 * COL] = _dot(xb, wvv_ref[...]).astype(BF16)
    for i, wg in enumerate((wg0_ref, wg1_ref, wg2_ref, wg3_ref)):
        gates_ref[:, i * COL:(i + 1) * COL] = _sigmoid(_dot(xb, wg[...])).astype(BF16)


def _proj_nat(x2, w_in_b, a_ln_g, a_ln_b, a_ws, a_bias):
    n = x2.shape[0]
    tm = TM_PROJ
    wspec = lambda j: pl.BlockSpec((D_MODEL, COL), lambda i, j=j: (0, j))
    full = lambda shape: pl.BlockSpec(shape, lambda i: (0,) * len(shape))
    return pl.pallas_call(
        _proj_nat_kernel,
        grid=(n // tm,),
        in_specs=[pl.BlockSpec((tm, D_MODEL), lambda i: (i, 0)),
                  wspec(0), wspec(1), wspec(2), wspec(5), wspec(8), wspec(11), wspec(12), wspec(13), wspec(14),
                  full((1, A_WIDTH)), full((1, A_WIDTH)), full((8, A_CHUNK, A_CHUNK)), full((A_CHUNK, A_WIDTH))],
        out_specs=[pl.BlockSpec((tm, A_WIDTH), lambda i: (i, 0)),
                   pl.BlockSpec((tm, 3 * COL), lambda i: (i, 0)),
                   pl.BlockSpec((tm, 4 * COL), lambda i: (i, 0))],
        out_shape=[jax.ShapeDtypeStruct((n, A_WIDTH), BF16),
                   jax.ShapeDtypeStruct((n, 3 * COL), BF16),
                   jax.ShapeDtypeStruct((n, 4 * COL), BF16)],
        compiler_params=pltpu.CompilerParams(dimension_semantics=("parallel",), vmem_limit_bytes=VMEM_LIMIT),
        name="proj_nat",
    )(x2, w_in_b, w_in_b, w_in_b, w_in_b, w_in_b, w_in_b, w_in_b, w_in_b, w_in_b, a_ln_g, a_ln_b, a_ws, a_bias)


def _proj_dil_kernel(x_ref, wq_ref, wk_ref, wv_ref, qkv_ref, *, rp):
    seq = x_ref.shape[0]
    xb = jnp.concatenate([x_ref[:, k * D_MODEL:(k + 1) * D_MODEL].astype(BF16) for k in range(rp)], axis=0)
    for j, w in enumerate((wq_ref, wk_ref, wv_ref)):
        r = _dot(xb, w[...]).astype(BF16)
        for k in range(rp):
            qkv_ref[k, :, j * COL:(j + 1) * COL] = r[k * seq:(k + 1) * seq]


def _proj_dil(x, w_in_b, gi):
    bsz, s, _ = x.shape
    dl = B_DILATIONS[gi]
    seq = s // dl
    rp = max(1, min(dl, TM_PROJ // seq))
    xv = x.reshape(bsz, seq, dl * D_MODEL)
    wspec = lambda j: pl.BlockSpec((D_MODEL, COL), lambda b, r, j=j: (0, j))
    return pl.pallas_call(
        functools.partial(_proj_dil_kernel, rp=rp),
        grid=(bsz, dl // rp),
        in_specs=[pl.BlockSpec((None, seq, rp * D_MODEL), lambda b, r: (b, 0, r)),
                  wspec(2 + gi), wspec(5 + gi), wspec(8 + gi)],
        out_specs=pl.BlockSpec((None, rp, seq, 3 * COL), lambda b, r: (b, r, 0, 0)),
        out_shape=jax.ShapeDtypeStruct((bsz, dl, seq, 3 * COL), BF16),
        compiler_params=pltpu.CompilerParams(dimension_semantics=("parallel", "parallel"),
                                             vmem_limit_bytes=VMEM_LIMIT),
        name=f"proj_dil{dl}",
    )(xv, w_in_b, w_in_b, w_in_b)


def _attn_kernel(qkv_ref, o_ref, lse_ref, *, ns, seq):
    win = min(2 * SPAN, seq)
    nb = seq // SPAN
    lane = lax.broadcasted_iota(I32, (SPAN, 128), 1)
    lo = lane < 64
    lane16 = lane // 16
    qi = lax.broadcasted_iota(I32, (SPAN, win), 0)
    ki = lax.broadcasted_iota(I32, (SPAN, win), 1)
    bias_first = jnp.where(ki <= qi, 0.0, NEG).astype(F32)
    bias_first = jnp.concatenate([bias_first, bias_first], axis=0)
    bias_main = jnp.where((ki >= qi) & (ki <= qi + SPAN), 0.0, NEG).astype(F32)
    bias_main = jnp.concatenate([bias_main, bias_main], axis=0)
    zero = jnp.zeros((SPAN, 128), BF16)

    for s in range(ns):
        def block(row0, start, bias, s=s):
            lse_tile = jnp.zeros((SPAN, 128), F32)
            for jp in range(B_HEADS // 2):
                c0 = jp * 128
                q = qkv_ref[s, pl.ds(row0, SPAN), c0:c0 + 128] * jnp.asarray(0.125, BF16)
                k = qkv_ref[s, pl.ds(start, win), COL + c0:COL + c0 + 128]
                v = qkv_ref[s, pl.ds(start, win), 2 * COL + c0:2 * COL + c0 + 128]
                qs = jnp.concatenate([jnp.where(lo, q, zero), jnp.where(lo, zero, q)], axis=0)
                sc = lax.dot_general(qs, k, (((1,), (1,)), ((), ())), preferred_element_type=F32) + bias
                m = jnp.max(sc, axis=-1, keepdims=True)
                p = jnp.exp(sc - m)
                l = jnp.sum(p, axis=-1, keepdims=True)
                ov = _dot(p.astype(BF16), v)
                inv = 1.0 / l
                o = jnp.where(lo, ov[:SPAN] * inv[:SPAN], ov[SPAN:] * inv[SPAN:])
                o_ref[pl.ds(row0, SPAN), s * B_WIDTH + c0:s * B_WIDTH + c0 + 128] = o.astype(BF16)
                lse = m + jnp.log(l)
                lse_tile = jnp.where(lane16 == 2 * jp, lse[:SPAN],
                                     jnp.where(lane16 == 2 * jp + 1, lse[SPAN:], lse_tile))
            lse_ref[pl.ds(row0, SPAN), s * 128:(s + 1) * 128] = lse_tile

        block(0, 0, bias_first)
        if nb > 1:
            def body(i, carry):
                block(pl.multiple_of(i * SPAN, SPAN), pl.multiple_of((i - 1) * SPAN, SPAN), bias_main)
                return carry
            lax.fori_loop(1, nb, body, 0)


def _attn(qkv_g, gi):
    bsz, dl, seq, _ = qkv_g.shape
    ns = max(1, min(dl, TM_PROJ // seq))
    return pl.pallas_call(
        functools.partial(_attn_kernel, ns=ns, seq=seq),
        grid=(bsz, dl // ns),
        in_specs=[pl.BlockSpec((None, ns, seq, 3 * COL), lambda b, r: (b, r, 0, 0))],
        out_specs=[pl.BlockSpec((None, seq, ns * B_WIDTH), lambda b, r: (b, 0, r)),
                   pl.BlockSpec((None, seq, ns * 128), lambda b, r: (b, 0, r))],
        out_shape=[jax.ShapeDtypeStruct((bsz, seq, dl * B_WIDTH), BF16),
                   jax.ShapeDtypeStruct((bsz, seq, dl * 128), F32)],
        compiler_params=pltpu.CompilerParams(dimension_semantics=("parallel", "parallel"),
                                             vmem_limit_bytes=VMEM_LIMIT),
        name=f"attn{dl}",
    )(qkv_g)


def _mix_kernel(ga_ref, gates_ref, o1_ref, o2_ref, o3_ref, l1_ref, l2_ref, l3_ref, x_ref, p_ref,
                wa_ref, wb_ref, wo_ref, wple_ref, wpg_ref, wr_ref, br_ref, g1_ref, b1_ref,
                base_ref, x1_ref, route_ref):
    tm = x_ref.shape[0]
    l1, l2, l3 = l1_ref[...], l2_ref[...], l3_ref[...]
    mx = jnp.maximum(l1, jnp.maximum(l2, l3))
    e1, e2, e3 = jnp.exp(l1 - mx), jnp.exp(l2 - mx), jnp.exp(l3 - mx)
    inv = 1.0 / (e1 + e2 + e3)
    er = lax.broadcasted_iota(I32, (128, B_WIDTH), 0)
    ec = lax.broadcasted_iota(I32, (128, B_WIDTH), 1)
    expand = jnp.where(er == (ec // B_HEAD_DIM) * 16, 1.0, 0.0).astype(BF16)

    def widen(w):
        hi = w.astype(BF16)
        lo = (w - hi.astype(F32)).astype(BF16)
        return _dot(hi, expand) + _dot(lo, expand)

    ob = (widen(e1 * inv) * o1_ref[...].astype(F32) + widen(e2 * inv) * o2_ref[...].astype(F32)
          + widen(e3 * inv) * o3_ref[...].astype(F32))
    yb = _dot(ob.astype(BF16), wb_ref[...])
    ya = _dot(ga_ref[...], wa_ref[...])
    pre = gates_ref[:, :D_MODEL].astype(F32) * ya + gates_ref[:, D_MODEL:].astype(F32) * yb
    mix = _dot(pre.astype(BF16), wo_ref[...])
    x1 = _ln(DEEPNORM_ALPHA * x_ref[...] + mix, g1_ref[...], b1_ref[...])
    x1b = x1.astype(BF16)

    ple = _dot(p_ref[...].astype(BF16), wple_ref[...]) * _sigmoid(_dot(x1b, wpg_ref[...]))
    base_ref[...] = DEEPNORM_ALPHA * x1 + ple
    x1_ref[...] = x1

    lg = jnp.dot(x1, wr_ref[...], preferred_element_type=F32, precision=lax.Precision.HIGHEST) + br_ref[...]
    lane = lax.broadcasted_iota(I32, (tm, 128), 1)
    big = jnp.int32(1 << 20)
    gl = jnp.where(lane < N_GROUPS, lg, NEG)
    gm = jnp.max(gl, axis=-1, keepdims=True)
    gidx = jnp.min(jnp.where(gl == gm, lane, big), axis=-1, keepdims=True)
    gsum = jnp.sum(jnp.where(lane < N_GROUPS, jnp.exp(gl - gm), 0.0), axis=-1, keepdims=True)
    gprob = 1.0 / gsum
    lo_lane = N_GROUPS + N_EXPERTS * gidx
    el = jnp.where((lane >= lo_lane) & (lane < lo_lane + N_EXPERTS), lg, NEG)
    v1 = jnp.max(el, axis=-1, keepdims=True)
    i1 = jnp.min(jnp.where(el == v1, lane, big), axis=-1, keepdims=True)
    el2 = jnp.where(lane == i1, NEG, el)
    v2 = jnp.max(el2, axis=-1, keepdims=True)
    i2 = jnp.min(jnp.where(el2 == v2, lane, big), axis=-1, keepdims=True)
    t = jnp.exp(v2 - v1)
    w1 = 1.0 / (1.0 + t)
    w2 = t * w1
    route = jnp.where(lane == 0, (i1 - N_GROUPS).astype(F32),
                      jnp.where(lane == 1, (i2 - N_GROUPS).astype(F32),
                                jnp.where(lane == 2, gprob * w1,
                                          jnp.where(lane == 3, gprob * w2, 0.0))))
    route_ref[...] = route


def _mix(ga, gates, o1, o2, o3, l1, l2, l3, x2, p2, wa, wb, wo, wple, wpg, wr, br, g1, b1):
    n = x2.shape[0]
    tm = TM_MIX
    rows = lambda w: pl.BlockSpec((tm, w), lambda i: (i, 0))
    full = lambda a: pl.BlockSpec(a.shape, lambda i: (0,) * a.ndim)
    return pl.pallas_call(
        _mix_kernel,
        grid=(n // tm,),
        in_specs=[rows(A_WIDTH), rows(2 * D_MODEL), rows(B_WIDTH), rows(B_WIDTH), rows(B_WIDTH),
                  rows(128), rows(128), rows(128), rows(D_MODEL), rows(PLE_DIM),
                  full(wa), full(wb), full(wo), full(wple), full(wpg), full(wr), full(br), full(g1), full(b1)],
        out_specs=[rows(D_MODEL), rows(D_MODEL), rows(128)],
        out_shape=[jax.ShapeDtypeStruct((n, D_MODEL), F32),
                   jax.ShapeDtypeStruct((n, D_MODEL), F32),
                   jax.ShapeDtypeStruct((n, 128), F32)],
        compiler_params=pltpu.CompilerParams(dimension_semantics=("parallel",), vmem_limit_bytes=VMEM_LIMIT),
        name="mix",
    )(ga, gates, o1, o2, o3, l1, l2, l3, x2, p2, wa, wb, wo, wple, wpg, wr, br, g1, b1)


def _row_gather_copy(src_hbm, buf, sem, slot, nrows):
    return pltpu.make_async_copy(src_hbm.at[pl.ds(0, nrows)], buf.at[slot], sem.at[slot])


def _issue_row_gather(idx_ref, src_hbm, buf, sem, slot, nrows):
    def body(r, carry):
        pltpu.make_async_copy(src_hbm.at[pl.ds(idx_ref[0, r], 1)], buf.at[slot, pl.ds(r, 1)], sem.at[slot]).start()
        return carry
    lax.fori_loop(0, nrows, body, 0, unroll=8)


def _moe_kernel(te_ref, tv_ref, idx_cur_ref, idx_nxt_ref, x_hbm, wg_ref, wu_ref, wd_ref, y_ref, xbuf, sem):
    del te_ref
    t = pl.program_id(0)
    nt = pl.num_programs(0)
    slot = t % 2
    tm = y_ref.shape[0]
    nxt = jnp.minimum(t + 1, nt - 1)

    @pl.when((t == 0) & (tv_ref[0] == 1))
    def _():
        _issue_row_gather(idx_cur_ref, x_hbm, xbuf, sem, 0, tm)

    @pl.when((t + 1 < nt) & (tv_ref[nxt] == 1))
    def _():
        _issue_row_gather(idx_nxt_ref, x_hbm, xbuf, sem, 1 - slot, tm)

    @pl.when(tv_ref[t] == 1)
    def _():
        _row_gather_copy(x_hbm, xbuf, sem, slot, tm).wait()
        xb = xbuf[slot].astype(BF16)
        g = _dot(xb, wg_ref[...])
        u = _dot(xb, wu_ref[...])
        h = (g * _sigmoid(g) * u).astype(BF16)
        y_ref[...] = _dot(h, wd_ref[...])

    @pl.when(tv_ref[t] == 0)
    def _():
        y_ref[...] = jnp.zeros(y_ref.shape, F32)


def _moe(tile_expert, tile_valid, src_tiles, x1, wg, wu, wd):
    nt, _, tm = src_tiles.shape
    return pl.pallas_call(
        _moe_kernel,
        grid_spec=pltpu.PrefetchScalarGridSpec(
            num_scalar_prefetch=2,
            grid=(nt,),
            in_specs=[pl.BlockSpec((None, 1, tm), lambda t, te, tv: (t, 0, 0), memory_space=pltpu.SMEM),
                      pl.BlockSpec((None, 1, tm), lambda t, te, tv: (jnp.minimum(t + 1, nt - 1), 0, 0),
                                   memory_space=pltpu.SMEM),
                      pl.BlockSpec(memory_space=pl.ANY),
                      pl.BlockSpec((None, D_MODEL, D_EXPERT), lambda t, te, tv: (te[t], 0, 0)),
                      pl.BlockSpec((None, D_MODEL, D_EXPERT), lambda t, te, tv: (te[t], 0, 0)),
                      pl.BlockSpec((None, D_EXPERT, D_MODEL), lambda t, te, tv: (te[t], 0, 0))],
            out_specs=pl.BlockSpec((tm, D_MODEL), lambda t, te, tv: (t, 0)),
            scratch_shapes=[pltpu.VMEM((2, tm, D_MODEL), F32), pltpu.SemaphoreType.DMA((2,))]),
        out_shape=jax.ShapeDtypeStruct((nt * tm, D_MODEL), F32),
        compiler_params=pltpu.CompilerParams(dimension_semantics=("arbitrary",), vmem_limit_bytes=VMEM_LIMIT),
        name="moe",
    )(tile_expert, tile_valid, src_tiles, src_tiles, x1, wg, wu, wd)


def _final_kernel(pos_cur_ref, pos_nxt_ref, y_hbm, base_ref, route_ref, g2_ref, b2_ref, out_ref, ybuf, sem):
    t = pl.program_id(0)
    nt = pl.num_programs(0)
    slot = t % 2
    tm = out_ref.shape[0]

    @pl.when(t == 0)
    def _():
        _issue_row_gather(pos_cur_ref, y_hbm, ybuf, sem, 0, 2 * tm)

    @pl.when(t + 1 < nt)
    def _():
        _issue_row_gather(pos_nxt_ref, y_hbm, ybuf, sem, 1 - slot, 2 * tm)

    _row_gather_copy(y_hbm, ybuf, sem, slot, 2 * tm).wait()
    ffn = route_ref[:, 2:3] * ybuf[slot, 0:tm] + route_ref[:, 3:4] * ybuf[slot, tm:2 * tm]
    out_ref[...] = _ln(base_ref[...] + ffn, g2_ref[...], b2_ref[...])


def _final(pos_tiles, ys, base, route, g2, b2):
    n = base.shape[0]
    tm = TM_FIN
    nt = n // tm
    return pl.pallas_call(
        _final_kernel,
        grid=(nt,),
        in_specs=[pl.BlockSpec((None, 1, 2 * tm), lambda t: (t, 0, 0), memory_space=pltpu.SMEM),
                  pl.BlockSpec((None, 1, 2 * tm), lambda t: (jnp.minimum(t + 1, nt - 1), 0, 0),
                               memory_space=pltpu.SMEM),
                  pl.BlockSpec(memory_space=pl.ANY),
                  pl.BlockSpec((tm, D_MODEL), lambda t: (t, 0)),
                  pl.BlockSpec((tm, 128), lambda t: (t, 0)),
                  pl.BlockSpec((1, D_MODEL), lambda t: (0, 0)),
                  pl.BlockSpec((1, D_MODEL), lambda t: (0, 0))],
        out_specs=pl.BlockSpec((tm, D_MODEL), lambda t: (t, 0)),
        out_shape=jax.ShapeDtypeStruct((n, D_MODEL), F32),
        scratch_shapes=[pltpu.VMEM((2, 2 * tm, D_MODEL), F32), pltpu.SemaphoreType.DMA((2,))],
        compiler_params=pltpu.CompilerParams(dimension_semantics=("arbitrary",), vmem_limit_bytes=VMEM_LIMIT),
        name="final",
    )(pos_tiles, pos_tiles, ys, base, route, g2, b2)


def _routing_tables(route, n):
    tm = TM_MOE
    nt = (2 * n) // tm + N_EXPERTS_TOTAL
    ef = route[:, 0:2].astype(I32).T.reshape(-1)
    onehot = (ef[:, None] == jnp.arange(N_EXPERTS_TOTAL, dtype=I32)[None, :]).astype(I32)
    csum = jnp.cumsum(onehot, axis=0)
    counts = csum[-1]
    rank = jnp.sum(csum * onehot, axis=1) - 1
    padded = ((counts + tm - 1) // tm) * tm
    ends = jnp.cumsum(padded)
    offs = ends - padded
    pos = jnp.sum(onehot * offs[None, :], axis=1) + rank
    tok = jnp.tile(jnp.arange(n, dtype=I32), 2)
    src = jnp.zeros((nt * tm,), I32).at[pos].set(tok, unique_indices=True)
    tile_start = jnp.arange(nt, dtype=I32) * tm
    tile_expert = jnp.minimum(jnp.sum((tile_start[:, None] >= ends[None, :]).astype(I32), axis=1),
                              N_EXPERTS_TOTAL - 1).astype(I32)
    tile_valid = (tile_start < ends[-1]).astype(I32)
    pos_tiles = pos.reshape(2, n // TM_FIN, TM_FIN).transpose(1, 0, 2).reshape(n // TM_FIN, 1, 2 * TM_FIN)
    return tile_expert, tile_valid, src.reshape(nt, 1, tm), pos_tiles


def kernel(x, p, w_in, a_ln_g, a_ln_b, a_ws, a_bs, w_a_proj, w_b_proj, w_o, ln1_g, ln1_b, w_group_router,
           b_group_router, w_expert_router, b_expert_router, w_gate, w_up, w_down, w_ple, w_ple_gate,
           ln2_g, ln2_b):
    bsz, s, d = x.shape
    n = bsz * s
    assert d == D_MODEL and s % (SPAN * max(B_DILATIONS)) == 0 and n % TM_PROJ == 0
    assert w_in.shape[0] == 1, "one layer"

    w_in_b = w_in[0].astype(BF16)
    x2 = x.reshape(n, d)
    a_bias = jnp.repeat(a_bs[0].T, A_WIDTH // 8, axis=1)

    ga, qkv1, gates = _proj_nat(x2, w_in_b, a_ln_g, a_ln_b, a_ws[0], a_bias)
    qkv2 = _proj_dil(x, w_in_b, 1)
    qkv3 = _proj_dil(x, w_in_b, 2)
    o1, l1 = _attn(qkv1.reshape(bsz, 1, s, 3 * COL), 0)
    o2, l2 = _attn(qkv2, 1)
    o3, l3 = _attn(qkv3, 2)

    wr = jnp.concatenate([w_group_router[0], w_expert_router[0].reshape(d, N_EXPERTS_TOTAL),
                          jnp.zeros((d, 128 - N_GROUPS - N_EXPERTS_TOTAL), F32)], axis=1)
    br = jnp.concatenate([b_group_router[0], b_expert_router[0].reshape(-1),
                          jnp.zeros((128 - N_GROUPS - N_EXPERTS_TOTAL,), F32)])[None, :]
    base, x1p, route = _mix(
        ga, gates, o1.reshape(n, B_WIDTH), o2.reshape(n, B_WIDTH), o3.reshape(n, B_WIDTH),
        l1.reshape(n, 128), l2.reshape(n, 128), l3.reshape(n, 128), x2, p[0].reshape(n, PLE_DIM),
        w_a_proj[0].astype(BF16), w_b_proj[0].astype(BF16), w_o[0].astype(BF16), w_ple[0].astype(BF16),
        w_ple_gate[0].astype(BF16), wr, br, ln1_g, ln1_b)

    tile_expert, tile_valid, src_tiles, pos_tiles = _routing_tables(route, n)
    ys = _moe(tile_expert, tile_valid, src_tiles, x1p,
              w_gate[0].reshape(N_EXPERTS_TOTAL, d, D_EXPERT).astype(BF16),
              w_up[0].reshape(N_EXPERTS_TOTAL, d, D_EXPERT).astype(BF16),
              w_down[0].reshape(N_EXPERTS_TOTAL, D_EXPERT, d).astype(BF16))
    out = _final(pos_tiles, ys, base, route, ln2_g, ln2_b)
    return out.reshape(bsz, s, d)
```

```python
import functools

import jax
import jax.numpy as jnp
from jax import lax
from jax.experimental import pallas as pl
from jax.experimental.pallas import tpu as pltpu
from jax.experimental.pallas import tpu_sc as plsc

F32 = jnp.float32
BF16 = jnp.bfloat16
U32 = jnp.uint32
I32 = jnp.int32

D_MODEL = 1024
PLE_DIM = 256
A_WIDTH = 512
A_CHUNK = 128
B_HEAD_DIM = 64
B_HEADS = 8
B_WIDTH = 512
B_DILATIONS = (1, 4, 16)
SPAN = 128
N_GROUPS = 4
N_EXPERTS = 8
N_EXPERTS_TOTAL = N_GROUPS * N_EXPERTS
D_EXPERT = 256
DEEPNORM_ALPHA = 2.0 ** 0.25
LN_EPS = 1e-5
COL = 512
NEG = -1e30

VMEM_LIMIT = 56 * 1024 * 1024

TM_PROJ = 512
TM_MIX = 256
TM_MOE = 256
TM_FIN = 256


def _ln(x, g, b):
    mu = jnp.mean(x, axis=-1, keepdims=True)
    xc = x - mu
    var = jnp.mean(xc * xc, axis=-1, keepdims=True)
    return xc * lax.rsqrt(var + LN_EPS) * g + b


def _gelu_tanh(x):
    return 0.5 * x * (1.0 + jnp.tanh(0.7978845608028654 * (x + 0.044715 * (x * x * x))))


def _sigmoid(x):
    return 1.0 / (1.0 + jnp.exp(-x))


def _dot(a, b):
    return jnp.dot(a, b, preferred_element_type=F32)


PACK_W = D_MODEL // 2
SUBROWS = PACK_W // 128


def _store_packed_rows(ref, x):
    m = x.shape[0]
    xb = x.astype(BF16).astype(F32)
    lo = pltpu.bitcast(xb[:, :PACK_W], U32) >> 16
    hi = pltpu.bitcast(xb[:, PACK_W:], U32) & jnp.uint32(0xFFFF0000)
    w = hi | lo
    for j in range(SUBROWS):
        ref[:, 8 * j:8 * (j + 1), :] = w[:, 128 * j:128 * (j + 1)].reshape(m // 8, 8, 128)


def _load_packed_rows(ref):
    m = ref.shape[0] * 8
    ws = [ref[:, 8 * j:8 * (j + 1), :].reshape(m, 128) for j in range(SUBROWS)]
    lo = [pltpu.bitcast(w << 16, F32) for w in ws]
    hi = [pltpu.bitcast(w & jnp.uint32(0xFFFF0000), F32) for w in ws]
    return jnp.concatenate(lo + hi, axis=1)


def _proj_nat_kernel(x_ref, wu_ref, wv_ref, wq_ref, wk_ref, wvv_ref, wg0_ref, wg1_ref, wg2_ref, wg3_ref,
                     lng_ref, lnb_ref, ws_ref, bias_ref, ga_ref, qkv_ref, gates_ref):
    tm = x_ref.shape[0]
    xb = x_ref[...].astype(BF16)

    u = _gelu_tanh(_dot(xb, wu_ref[...]))
    v = _gelu_tanh(_dot(xb, wv_ref[...]))
    vn = _ln(v, lng_ref[...], lnb_ref[...]).astype(BF16)

    row = lax.broadcasted_iota(I32, (A_CHUNK, A_CHUNK), 0)
    colm = lax.broadcasted_iota(I32, (A_CHUNK, A_CHUNK), 1)
    causal = colm <= row
    lo = colm < 64
    zero = jnp.zeros((A_CHUNK, A_CHUNK), BF16)
    wcat = []
    for j in range(4):
        w0 = jnp.where(causal, ws_ref[2 * j], 0.0).astype(BF16)
        w1 = jnp.where(causal, ws_ref[2 * j + 1], 0.0).astype(BF16)
        wcat.append(jnp.concatenate([w0, w1], axis=1))
    for c in range(tm // A_CHUNK):
        r0 = c * A_CHUNK
        for j in range(4):
            c0 = j * 128
            vt = vn[r0:r0 + A_CHUNK, c0:c0 + 128]
            rhs = jnp.concatenate([jnp.where(lo, vt, zero), jnp.where(lo, zero, vt)], axis=0)
            mixed = _dot(wcat[j], rhs) + bias_ref[:, c0:c0 + 128]
            ga_ref[r0:r0 + A_CHUNK, c0:c0 + 128] = (u[r0:r0 + A_CHUNK, c0:c0 + 128] * mixed).astype(BF16)

    qkv_ref[:, 0:COL] = _dot(xb, wq_ref[...]).astype(BF16)
    qkv_ref[:, COL:2 * COL] = _dot(xb, wk_ref[...]).astype(BF16)
    qkv_ref[:, 2 * COL:3 * COL] = _dot(xb, wvv_ref[...]).astype(BF16)
    for i, wg in enumerate((wg0_ref, wg1_ref, wg2_ref, wg3_ref)):
        gates_ref[:, i * COL:(i + 1) * COL] = _sigmoid(_dot(xb, wg[...])).astype(BF16)


def _proj_nat(x2, w_in_b, a_ln_g, a_ln_b, a_ws, a_bias):
    n = x2.shape[0]
    tm = TM_PROJ
    wspec = lambda j: pl.BlockSpec((D_MODEL, COL), lambda i, j=j: (0, j))
    full = lambda shape: pl.BlockSpec(shape, lambda i: (0,) * len(shape))
    return pl.pallas_call(
        _proj_nat_kernel,
        grid=(n // tm,),
        in_specs=[pl.BlockSpec((tm, D_MODEL), lambda i: (i, 0)),
                  wspec(0), wspec(1), wspec(2), wspec(5), wspec(8), wspec(11), wspec(12), wspec(13), wspec(14),
                  full((1, A_WIDTH)), full((1, A_WIDTH)), full((8, A_CHUNK, A_CHUNK)), full((A_CHUNK, A_WIDTH))],
        out_specs=[pl.BlockSpec((tm, A_WIDTH), lambda i: (i, 0)),
                   pl.BlockSpec((tm, 3 * COL), lambda i: (i, 0)),
                   pl.BlockSpec((tm, 4 * COL), lambda i: (i, 0))],
        out_shape=[jax.ShapeDtypeStruct((n, A_WIDTH), BF16),
                   jax.ShapeDtypeStruct((n, 3 * COL), BF16),
                   jax.ShapeDtypeStruct((n, 4 * COL), BF16)],
        compiler_params=pltpu.CompilerParams(dimension_semantics=("parallel",), vmem_limit_bytes=VMEM_LIMIT),
        name="proj_nat",
    )(x2, w_in_b, w_in_b, w_in_b, w_in_b, w_in_b, w_in_b, w_in_b, w_in_b, w_in_b, a_ln_g, a_ln_b, a_ws, a_bias)


def _proj_dil_kernel(x_ref, wq_ref, wk_ref, wv_ref, qkv_ref, *, rp):
    seq = x_ref.shape[0]
    xb = jnp.concatenate([x_ref[:, k * D_MODEL:(k + 1) * D_MODEL].astype(BF16) for k in range(rp)], axis=0)
    for j, w in enumerate((wq_ref, wk_ref, wv_ref)):
        r = _dot(xb, w[...]).astype(BF16)
        for k in range(rp):
            qkv_ref[k, :, j * COL:(j + 1) * COL] = r[k * seq:(k + 1) * seq]


def _proj_dil(x, w_in_b, gi):
    bsz, s, _ = x.shape
    dl = B_DILATIONS[gi]
    seq = s // dl
    rp = max(1, min(dl, TM_PROJ // seq))
    xv = x.reshape(bsz, seq, dl * D_MODEL)
    wspec = lambda j: pl.BlockSpec((D_MODEL, COL), lambda b, r, j=j: (0, j))
    return pl.pallas_call(
        functools.partial(_proj_dil_kernel, rp=rp),
        grid=(bsz, dl // rp),
        in_specs=[pl.BlockSpec((None, seq, rp * D_MODEL), lambda b, r: (b, 0, r)),
                  wspec(2 + gi), wspec(5 + gi), wspec(8 + gi)],
        out_specs=pl.BlockSpec((None, rp, seq, 3 * COL), lambda b, r: (b, r, 0, 0)),
        out_shape=jax.ShapeDtypeStruct((bsz, dl, seq, 3 * COL), BF16),
        compiler_params=pltpu.CompilerParams(dimension_semantics=("parallel", "parallel"),
                                             vmem_limit_bytes=VMEM_LIMIT),
        name=f"proj_dil{dl}",
    )(xv, w_in_b, w_in_b, w_in_b)


def _attn_kernel(qkv_ref, o_ref, lse_ref, *, ns, seq):
    win = min(2 * SPAN, seq)
    nb = seq // SPAN
    lane = lax.broadcasted_iota(I32, (SPAN, 128), 1)
    lo = lane < 64
    lane16 = lane // 16
    qi = lax.broadcasted_iota(I32, (SPAN, win), 0)
    ki = lax.broadcasted_iota(I32, (SPAN, win), 1)
    bias_first = jnp.where(ki <= qi, 0.0, NEG).astype(F32)
    bias_first = jnp.concatenate([bias_first, bias_first], axis=0)
    bias_main = jnp.where((ki >= qi) & (ki <= qi + SPAN), 0.0, NEG).astype(F32)
    bias_main = jnp.concatenate([bias_main, bias_main], axis=0)
    zero = jnp.zeros((SPAN, 128), BF16)

    for s in range(ns):
        def block(row0, start, bias, s=s):
            lse_tile = jnp.zeros((SPAN, 128), F32)
            for jp in range(B_HEADS // 2):
                c0 = jp * 128
                q = qkv_ref[s, pl.ds(row0, SPAN), c0:c0 + 128] * jnp.asarray(0.125, BF16)
                k = qkv_ref[s, pl.ds(start, win), COL + c0:COL + c0 + 128]
                v = qkv_ref[s, pl.ds(start, win), 2 * COL + c0:2 * COL + c0 + 128]
                qs = jnp.concatenate([jnp.where(lo, q, zero), jnp.where(lo, zero, q)], axis=0)
                sc = lax.dot_general(qs, k, (((1,), (1,)), ((), ())), preferred_element_type=F32) + bias
                m = jnp.max(sc, axis=-1, keepdims=True)
                p = jnp.exp(sc - m)
                l = jnp.sum(p, axis=-1, keepdims=True)
                ov = _dot(p.astype(BF16), v)
                inv = 1.0 / l
                o = jnp.where(lo, ov[:SPAN] * inv[:SPAN], ov[SPAN:] * inv[SPAN:])
                o_ref[pl.ds(row0, SPAN), s * B_WIDTH + c0:s * B_WIDTH + c0 + 128] = o.astype(BF16)
                lse = m + jnp.log(l)
                lse_tile = jnp.where(lane16 == 2 * jp, lse[:SPAN],
                                     jnp.where(lane16 == 2 * jp + 1, lse[SPAN:], lse_tile))
            lse_ref[pl.ds(row0, SPAN), s * 128:(s + 1) * 128] = lse_tile

        block(0, 0, bias_first)
        if nb > 1:
            def body(i, carry):
                block(pl.multiple_of(i * SPAN, SPAN), pl.multiple_of((i - 1) * SPAN, SPAN), bias_main)
                return carry
            lax.fori_loop(1, nb, body, 0)


def _attn(qkv_g, gi):
    bsz, dl, seq, _ = qkv_g.shape
    ns = max(1, min(dl, TM_PROJ // seq))
    return pl.pallas_call(
        functools.partial(_attn_kernel, ns=ns, seq=seq),
        grid=(bsz, dl // ns),
        in_specs=[pl.BlockSpec((None, ns, seq, 3 * COL), lambda b, r: (b, r, 0, 0))],
        out_specs=[pl.BlockSpec((None, seq, ns * B_WIDTH), lambda b, r: (b, 0, r)),
                   pl.BlockSpec((None, seq, ns * 128), lambda b, r: (b, 0, r))],
        out_shape=[jax.ShapeDtypeStruct((bsz, seq, dl * B_WIDTH), BF16),
                   jax.ShapeDtypeStruct((bsz, seq, dl * 128), F32)],
        compiler_params=pltpu.CompilerParams(dimension_semantics=("parallel", "parallel"),
                                             vmem_limit_bytes=VMEM_LIMIT),
        name=f"attn{dl}",
    )(qkv_g)


def _mix_kernel(ga_ref, gates_ref, o1_ref, o2_ref, o3_ref, l1_ref, l2_ref, l3_ref, x_ref, p_ref,
                wa_ref, wb_ref, wo_ref, wple_ref, wpg_ref, wr_ref, br_ref, g1_ref, b1_ref,
                base_ref, x1p_ref, route_ref):
    tm = x_ref.shape[0]
    l1, l2, l3 = l1_ref[...], l2_ref[...], l3_ref[...]
    mx = jnp.maximum(l1, jnp.maximum(l2, l3))
    e1, e2, e3 = jnp.exp(l1 - mx), jnp.exp(l2 - mx), jnp.exp(l3 - mx)
    inv = 1.0 / (e1 + e2 + e3)
    er = lax.broadcasted_iota(I32, (128, B_WIDTH), 0)
    ec = lax.broadcasted_iota(I32, (128, B_WIDTH), 1)
    expand = jnp.where(er == (ec // B_HEAD_DIM) * 16, 1.0, 0.0).astype(BF16)

    def widen(w):
        hi = w.astype(BF16)
        lo = (w - hi.astype(F32)).astype(BF16)
        return _dot(hi, expand) + _dot(lo, expand)

    ob = (widen(e1 * inv) * o1_ref[...].astype(F32) + widen(e2 * inv) * o2_ref[...].astype(F32)
          + widen(e3 * inv) * o3_ref[...].astype(F32))
    yb = _dot(ob.astype(BF16), wb_ref[...])
    ya = _dot(ga_ref[...], wa_ref[...])
    pre = gates_ref[:, :D_MODEL].astype(F32) * ya + gates_ref[:, D_MODEL:].astype(F32) * yb
    mix = _dot(pre.astype(BF16), wo_ref[...])
    x1 = _ln(DEEPNORM_ALPHA * x_ref[...] + mix, g1_ref[...], b1_ref[...])
    x1b = x1.astype(BF16)

    ple = _dot(p_ref[...].astype(BF16), wple_ref[...]) * _sigmoid(_dot(x1b, wpg_ref[...]))
    base_ref[...] = DEEPNORM_ALPHA * x1 + ple
    _store_packed_rows(x1p_ref, x1)

    lg = jnp.dot(x1, wr_ref[...], preferred_element_type=F32, precision=lax.Precision.HIGHEST) + br_ref[...]
    lane = lax.broadcasted_iota(I32, (tm, 128), 1)
    big = jnp.int32(1 << 20)
    gl = jnp.where(lane < N_GROUPS, lg, NEG)
    gm = jnp.max(gl, axis=-1, keepdims=True)
    gidx = jnp.min(jnp.where(gl == gm, lane, big), axis=-1, keepdims=True)
    gsum = jnp.sum(jnp.where(lane < N_GROUPS, jnp.exp(gl - gm), 0.0), axis=-1, keepdims=True)
    gprob = 1.0 / gsum
    lo_lane = N_GROUPS + N_EXPERTS * gidx
    el = jnp.where((lane >= lo_lane) & (lane < lo_lane + N_EXPERTS), lg, NEG)
    v1 = jnp.max(el, axis=-1, keepdims=True)
    i1 = jnp.min(jnp.where(el == v1, lane, big), axis=-1, keepdims=True)
    el2 = jnp.where(lane == i1, NEG, el)
    v2 = jnp.max(el2, axis=-1, keepdims=True)
    i2 = jnp.min(jnp.where(el2 == v2, lane, big), axis=-1, keepdims=True)
    t = jnp.exp(v2 - v1)
    w1 = 1.0 / (1.0 + t)
    w2 = t * w1
    route = jnp.where(lane == 0, (i1 - N_GROUPS).astype(F32),
                      jnp.where(lane == 1, (i2 - N_GROUPS).astype(F32),
                                jnp.where(lane == 2, gprob * w1,
                                          jnp.where(lane == 3, gprob * w2, 0.0))))
    route_ref[...] = route


def _mix(ga, gates, o1, o2, o3, l1, l2, l3, x2, p2, wa, wb, wo, wple, wpg, wr, br, g1, b1):
    n = x2.shape[0]
    tm = TM_MIX
    rows = lambda w: pl.BlockSpec((tm, w), lambda i: (i, 0))
    full = lambda a: pl.BlockSpec(a.shape, lambda i: (0,) * a.ndim)
    return pl.pallas_call(
        _mix_kernel,
        grid=(n // tm,),
        in_specs=[rows(A_WIDTH), rows(2 * D_MODEL), rows(B_WIDTH), rows(B_WIDTH), rows(B_WIDTH),
                  rows(128), rows(128), rows(128), rows(D_MODEL), rows(PLE_DIM),
                  full(wa), full(wb), full(wo), full(wple), full(wpg), full(wr), full(br), full(g1), full(b1)],
        out_specs=[rows(D_MODEL), pl.BlockSpec((tm // 8, 32, 128), lambda i: (i, 0, 0)), rows(128)],
        out_shape=[jax.ShapeDtypeStruct((n, D_MODEL), F32),
                   jax.ShapeDtypeStruct((n // 8, 32, 128), U32),
                   jax.ShapeDtypeStruct((n, 128), F32)],
        compiler_params=pltpu.CompilerParams(dimension_semantics=("parallel",), vmem_limit_bytes=VMEM_LIMIT),
        name="mix",
    )(ga, gates, o1, o2, o3, l1, l2, l3, x2, p2, wa, wb, wo, wple, wpg, wr, br, g1, b1)


SC_WINDOW = 128


def _sc_mesh():
    return plsc.VectorSubcoreMesh(core_axis_name="core", subcore_axis_name="subcore")


def _sc_scatter_rows(rows, dst, n_out):
    r = rows.shape[0]
    m = dst.shape[0]
    nblk = r // SC_WINDOW

    @pl.kernel(out_type=jax.ShapeDtypeStruct((n_out, 128), rows.dtype), mesh=_sc_mesh())
    def scatter(rows_hbm, dst_hbm, out_hbm):
        def body(rows_vmem, dst_vmem):
            pltpu.sync_copy(rows_vmem, out_hbm.at[dst_vmem.at[0]])

        pltpu.emit_pipeline(
            body,
            grid=(m // SC_WINDOW,),
            in_specs=[pl.BlockSpec((SC_WINDOW, 128), lambda i: (i % nblk, 0)),
                      pl.BlockSpec((1, SC_WINDOW), lambda i: (0, i))],
            out_specs=[],
            core_axis_name=("core", "subcore"),
            dimension_semantics=(pltpu.PARALLEL,),
        )(rows_hbm, dst_hbm)

    return scatter(rows, dst.reshape(1, m))


def _sc_gather_rows(table, src):
    m = src.shape[0]

    @pl.kernel(out_type=jax.ShapeDtypeStruct((m, 128), table.dtype), mesh=_sc_mesh())
    def gather(table_hbm, src_hbm, out_hbm):
        def body(src_vmem, out_vmem):
            pltpu.sync_copy(table_hbm.at[src_vmem.at[0]], out_vmem)

        pltpu.emit_pipeline(
            body,
            grid=(m // SC_WINDOW,),
            in_specs=[pl.BlockSpec((1, SC_WINDOW), lambda i: (0, i))],
            out_specs=[pl.BlockSpec((SC_WINDOW, 128), lambda i: (i, 0))],
            core_axis_name=("core", "subcore"),
            dimension_semantics=(pltpu.PARALLEL,),
        )(src_hbm, out_hbm)

    return gather(table, src.reshape(1, m))


def _moe_kernel(te_ref, tv_ref, xs_ref, wg_ref, wu_ref, wd_ref, ys_ref):
    del te_ref
    t = pl.program_id(0)

    @pl.when(tv_ref[t] == 1)
    def _():
        xb = _load_packed_rows(xs_ref).astype(BF16)
        g = _dot(xb, wg_ref[...])
        u = _dot(xb, wu_ref[...])
        h = (g * _sigmoid(g) * u).astype(BF16)
        _store_packed_rows(ys_ref, _dot(h, wd_ref[...]))

    @pl.when(tv_ref[t] == 0)
    def _():
        ys_ref[...] = jnp.zeros(ys_ref.shape, U32)


def _moe(tile_expert, tile_valid, xs, wg, wu, wd):
    tm = TM_MOE
    nt = tile_expert.shape[0]
    rows = pl.BlockSpec((tm // 8, 32, 128), lambda t, te, tv: (t, 0, 0))
    return pl.pallas_call(
        _moe_kernel,
        grid_spec=pltpu.PrefetchScalarGridSpec(
            num_scalar_prefetch=2,
            grid=(nt,),
            in_specs=[rows,
                      pl.BlockSpec((None, D_MODEL, D_EXPERT), lambda t, te, tv: (te[t], 0, 0)),
                      pl.BlockSpec((None, D_MODEL, D_EXPERT), lambda t, te, tv: (te[t], 0, 0)),
                      pl.BlockSpec((None, D_EXPERT, D_MODEL), lambda t, te, tv: (te[t], 0, 0))],
            out_specs=rows),
        out_shape=jax.ShapeDtypeStruct((nt * tm // 8, 32, 128), U32),
        compiler_params=pltpu.CompilerParams(dimension_semantics=("parallel",), vmem_limit_bytes=VMEM_LIMIT),
        name="moe",
    )(tile_expert, tile_valid, xs, wg, wu, wd)


def _final_kernel(y0_ref, y1_ref, base_ref, route_ref, g2_ref, b2_ref, out_ref):
    ffn = route_ref[:, 2:3] * _load_packed_rows(y0_ref) + route_ref[:, 3:4] * _load_packed_rows(y1_ref)
    out_ref[...] = _ln(base_ref[...] + ffn, g2_ref[...], b2_ref[...])


def _final(yg, base, route, g2, b2):
    n = base.shape[0]
    tm = TM_FIN
    nt = n // tm
    return pl.pallas_call(
        _final_kernel,
        grid=(nt,),
        in_specs=[pl.BlockSpec((tm // 8, 32, 128), lambda t: (t, 0, 0)),
                  pl.BlockSpec((tm // 8, 32, 128), lambda t: (t + nt, 0, 0)),
                  pl.BlockSpec((tm, D_MODEL), lambda t: (t, 0)),
                  pl.BlockSpec((tm, 128), lambda t: (t, 0)),
                  pl.BlockSpec((1, D_MODEL), lambda t: (0, 0)),
                  pl.BlockSpec((1, D_MODEL), lambda t: (0, 0))],
        out_specs=pl.BlockSpec((tm, D_MODEL), lambda t: (t, 0)),
        out_shape=jax.ShapeDtypeStruct((n, D_MODEL), F32),
        compiler_params=pltpu.CompilerParams(dimension_semantics=("parallel",), vmem_limit_bytes=VMEM_LIMIT),
        name="final",
    )(yg, yg, base, route, g2, b2)


def _routing_tables(route, n):
    tm = TM_MOE
    nt = (2 * n) // tm + N_EXPERTS_TOTAL
    ef = route[:, 0:2].astype(I32).T.reshape(-1)
    onehot = (ef[:, None] == jnp.arange(N_EXPERTS_TOTAL, dtype=I32)[None, :]).astype(I32)
    csum = jnp.cumsum(onehot, axis=0)
    counts = csum[-1]
    rank = jnp.sum(csum * onehot, axis=1) - 1
    padded = ((counts + tm - 1) // tm) * tm
    ends = jnp.cumsum(padded)
    offs = ends - padded
    pos = jnp.sum(onehot * offs[None, :], axis=1) + rank
    tile_start = jnp.arange(nt, dtype=I32) * tm
    tile_expert = jnp.minimum(jnp.sum((tile_start[:, None] >= ends[None, :]).astype(I32), axis=1),
                              N_EXPERTS_TOTAL - 1).astype(I32)
    tile_valid = (tile_start < ends[-1]).astype(I32)
    pg = pos.reshape(2 * n // 8, 1, 8)
    piece = (pg // 8) * (8 * SUBROWS) + pg % 8 + 8 * jnp.arange(SUBROWS, dtype=I32)[None, :, None]
    return tile_expert, tile_valid, piece.reshape(-1)


def kernel(x, p, w_in, a_ln_g, a_ln_b, a_ws, a_bs, w_a_proj, w_b_proj, w_o, ln1_g, ln1_b, w_group_router,
           b_group_router, w_expert_router, b_expert_router, w_gate, w_up, w_down, w_ple, w_ple_gate,
           ln2_g, ln2_b):
    bsz, s, d = x.shape
    n = bsz * s
    assert d == D_MODEL and s % (SPAN * max(B_DILATIONS)) == 0 and n % TM_PROJ == 0
    assert w_in.shape[0] == 1, "one layer"

    w_in_b = w_in[0].astype(BF16)
    x2 = x.reshape(n, d)
    a_bias = jnp.repeat(a_bs[0].T, A_WIDTH // 8, axis=1)

    ga, qkv1, gates = _proj_nat(x2, w_in_b, a_ln_g, a_ln_b, a_ws[0], a_bias)
    qkv2 = _proj_dil(x, w_in_b, 1)
    qkv3 = _proj_dil(x, w_in_b, 2)
    o1, l1 = _attn(qkv1.reshape(bsz, 1, s, 3 * COL), 0)
    o2, l2 = _attn(qkv2, 1)
    o3, l3 = _attn(qkv3, 2)

    wr = jnp.concatenate([w_group_router[0], w_expert_router[0].reshape(d, N_EXPERTS_TOTAL),
                          jnp.zeros((d, 128 - N_GROUPS - N_EXPERTS_TOTAL), F32)], axis=1)
    br = jnp.concatenate([b_group_router[0], b_expert_router[0].reshape(-1),
                          jnp.zeros((128 - N_GROUPS - N_EXPERTS_TOTAL,), F32)])[None, :]
    base, x1p, route = _mix(
        ga, gates, o1.reshape(n, B_WIDTH), o2.reshape(n, B_WIDTH), o3.reshape(n, B_WIDTH),
        l1.reshape(n, 128), l2.reshape(n, 128), l3.reshape(n, 128), x2, p[0].reshape(n, PLE_DIM),
        w_a_proj[0].astype(BF16), w_b_proj[0].astype(BF16), w_o[0].astype(BF16), w_ple[0].astype(BF16),
        w_ple_gate[0].astype(BF16), wr, br, ln1_g, ln1_b)

    tile_expert, tile_valid, piece = _routing_tables(route, n)
    nt = tile_expert.shape[0]
    xs = _sc_scatter_rows(x1p.reshape(n * SUBROWS, 128), piece, nt * TM_MOE * SUBROWS)
    ys = _moe(tile_expert, tile_valid, xs.reshape(nt * TM_MOE // 8, 32, 128),
              w_gate[0].reshape(N_EXPERTS_TOTAL, d, D_EXPERT).astype(BF16),
              w_up[0].reshape(N_EXPERTS_TOTAL, d, D_EXPERT).astype(BF16),
              w_down[0].reshape(N_EXPERTS_TOTAL, D_EXPERT, d).astype(BF16))
    yg = _sc_gather_rows(ys.reshape(nt * TM_MOE * SUBROWS, 128), piece)
    out = _final(yg.reshape(2 * n // 8, 32, 128), base, route, ln2_g, ln2_b)
    return out.reshape(bsz, s, d)
```

```python
import functools

import jax
import jax.numpy as jnp
from jax import lax
from jax.experimental import pallas as pl
from jax.experimental.pallas import tpu as pltpu
from jax.experimental.pallas import tpu_sc as plsc

F32 = jnp.float32
BF16 = jnp.bfloat16
U32 = jnp.uint32
I32 = jnp.int32

D_MODEL = 1024
PLE_DIM = 256
A_WIDTH = 512
A_CHUNK = 128
B_HEAD_DIM = 64
B_HEADS = 8
B_WIDTH = 512
B_DILATIONS = (1, 4, 16)
SPAN = 128
N_GROUPS = 4
N_EXPERTS = 8
N_EXPERTS_TOTAL = N_GROUPS * N_EXPERTS
D_EXPERT = 256
DEEPNORM_ALPHA = 2.0 ** 0.25
LN_EPS = 1e-5
COL = 512
NEG = -1e30

VMEM_LIMIT = 56 * 1024 * 1024

TM_PROJ = 512
TM_MIX = 256
TM_MOE = 256
TM_FIN = 256


def _ln(x, g, b):
    mu = jnp.mean(x, axis=-1, keepdims=True)
    xc = x - mu
    var = jnp.mean(xc * xc, axis=-1, keepdims=True)
    return xc * lax.rsqrt(var + LN_EPS) * g + b


def _gelu_tanh(x):
    return 0.5 * x * (1.0 + jnp.tanh(0.7978845608028654 * (x + 0.044715 * (x * x * x))))


def _sigmoid(x):
    return 1.0 / (1.0 + jnp.exp(-x))


def _dot(a, b):
    return jnp.dot(a, b, preferred_element_type=F32)


PACK_W = D_MODEL // 2
SUBROWS = PACK_W // 128


def _store_packed_rows(ref, x):
    m = x.shape[0]
    xb = x.astype(BF16).astype(F32)
    lo = pltpu.bitcast(xb[:, :PACK_W], U32) >> 16
    hi = pltpu.bitcast(xb[:, PACK_W:], U32) & jnp.uint32(0xFFFF0000)
    w = hi | lo
    for j in range(SUBROWS):
        ref[:, 8 * j:8 * (j + 1), :] = w[:, 128 * j:128 * (j + 1)].reshape(m // 8, 8, 128)


def _load_packed_rows(ref):
    m = ref.shape[0] * 8
    ws = [ref[:, 8 * j:8 * (j + 1), :].reshape(m, 128) for j in range(SUBROWS)]
    lo = [pltpu.bitcast(w << 16, F32) for w in ws]
    hi = [pltpu.bitcast(w & jnp.uint32(0xFFFF0000), F32) for w in ws]
    return jnp.concatenate(lo + hi, axis=1)


def _proj_kernel(x_ref, *refs):
    w = refs[:15]
    lng_ref, lnb_ref, ws_ref, bias_ref = refs[15:19]
    ga_ref, gates_ref, qkv1_ref, qkv2_ref, qkv3_ref = refs[19:24]
    xc_ref = refs[24]
    tm = x_ref.shape[0]
    xb = x_ref[...].astype(BF16)

    u = _gelu_tanh(_dot(xb, w[0][...]))
    v = _gelu_tanh(_dot(xb, w[1][...]))
    vn = _ln(v, lng_ref[...], lnb_ref[...]).astype(BF16)

    row = lax.broadcasted_iota(I32, (A_CHUNK, A_CHUNK), 0)
    colm = lax.broadcasted_iota(I32, (A_CHUNK, A_CHUNK), 1)
    causal = colm <= row
    lo = colm < 64
    zero = jnp.zeros((A_CHUNK, A_CHUNK), BF16)
    wcat = []
    for j in range(4):
        w0 = jnp.where(causal, ws_ref[2 * j], 0.0).astype(BF16)
        w1 = jnp.where(causal, ws_ref[2 * j + 1], 0.0).astype(BF16)
        wcat.append(jnp.concatenate([w0, w1], axis=1))
    for c in range(tm // A_CHUNK):
        r0 = c * A_CHUNK
        for j in range(4):
            c0 = j * 128
            vt = vn[r0:r0 + A_CHUNK, c0:c0 + 128]
            rhs = jnp.concatenate([jnp.where(lo, vt, zero), jnp.where(lo, zero, vt)], axis=0)
            mixed = _dot(wcat[j], rhs) + bias_ref[:, c0:c0 + 128]
            ga_ref[r0:r0 + A_CHUNK, c0:c0 + 128] = (u[r0:r0 + A_CHUNK, c0:c0 + 128] * mixed).astype(BF16)

    for i in range(4):
        gates_ref[:, i * COL:(i + 1) * COL] = _sigmoid(_dot(xb, w[11 + i][...])).astype(BF16)
    for j in range(3):
        qkv1_ref[:, j * COL:(j + 1) * COL] = _dot(xb, w[2 + 3 * j][...]).astype(BF16)

    for c in range(D_MODEL // 128):
        xc_ref[c] = x_ref[:, c * 128:(c + 1) * 128]
    for gi, out_ref in ((1, qkv2_ref), (2, qkv3_ref)):
        dl = B_DILATIONS[gi]
        per = tm // dl
        xp = jnp.concatenate(
            [jnp.concatenate([xc_ref[c, pl.ds(r, per, stride=dl), :] for c in range(D_MODEL // 128)], axis=1)
             for r in range(dl)], axis=0).astype(BF16)
        for j in range(3):
            res = _dot(xp, w[2 + 3 * j + gi][...]).astype(BF16)
            for r in range(dl):
                out_ref[r, :, j * COL:(j + 1) * COL] = res[r * per:(r + 1) * per]


def _proj(x, w_in_b, a_ln_g, a_ln_b, a_ws, a_bias):
    bsz, s, _ = x.shape
    n = bsz * s
    tm = TM_PROJ
    tiles = s // tm
    x2 = x.reshape(n, D_MODEL)
    wspec = lambda j: pl.BlockSpec((D_MODEL, COL), lambda i, j=j: (0, j), pipeline_mode=pl.Buffered(1))
    full = lambda shape: pl.BlockSpec(shape, lambda i: (0,) * len(shape))
    rows = lambda width: pl.BlockSpec((tm, width), lambda i: (i, 0))
    dil = lambda dl: pl.BlockSpec((None, dl, tm // dl, 3 * COL), lambda i: (i // tiles, 0, i % tiles, 0))
    return pl.pallas_call(
        _proj_kernel,
        grid=(n // tm,),
        in_specs=[rows(D_MODEL)] + [wspec(j) for j in range(15)]
                 + [full((1, A_WIDTH)), full((1, A_WIDTH)), full((8, A_CHUNK, A_CHUNK)), full((A_CHUNK, A_WIDTH))],
        out_specs=[rows(A_WIDTH), rows(4 * COL), rows(3 * COL), dil(4), dil(16)],
        out_shape=[jax.ShapeDtypeStruct((n, A_WIDTH), BF16),
                   jax.ShapeDtypeStruct((n, 4 * COL), BF16),
                   jax.ShapeDtypeStruct((n, 3 * COL), BF16),
                   jax.ShapeDtypeStruct((bsz, 4, s // 4, 3 * COL), BF16),
                   jax.ShapeDtypeStruct((bsz, 16, s // 16, 3 * COL), BF16)],
        scratch_shapes=[pltpu.VMEM((D_MODEL // 128, tm, 128), F32)],
        compiler_params=pltpu.CompilerParams(dimension_semantics=("parallel",), vmem_limit_bytes=VMEM_LIMIT),
        name="proj",
    )(x2, *([w_in_b] * 15), a_ln_g, a_ln_b, a_ws, a_bias)


def _attn_kernel(qkv_ref, o_ref, lse_ref, *, ns, seq):
    win = min(2 * SPAN, seq)
    nb = seq // SPAN
    lane = lax.broadcasted_iota(I32, (SPAN, 128), 1)
    lo = lane < 64
    lane16 = lane // 16
    qi = lax.broadcasted_iota(I32, (SPAN, win), 0)
    ki = lax.broadcasted_iota(I32, (SPAN, win), 1)
    bias_first = jnp.where(ki <= qi, 0.0, NEG).astype(F32)
    bias_first = jnp.concatenate([bias_first, bias_first], axis=0)
    bias_main = jnp.where((ki >= qi) & (ki <= qi + SPAN), 0.0, NEG).astype(F32)
    bias_main = jnp.concatenate([bias_main, bias_main], axis=0)
    zero = jnp.zeros((SPAN, 128), BF16)

    for s in range(ns):
        def block(row0, start, bias, s=s):
            lse_tile = jnp.zeros((SPAN, 128), F32)
            for jp in range(B_HEADS // 2):
                c0 = jp * 128
                q = qkv_ref[s, pl.ds(row0, SPAN), c0:c0 + 128] * jnp.asarray(0.125, BF16)
                k = qkv_ref[s, pl.ds(start, win), COL + c0:COL + c0 + 128]
                v = qkv_ref[s, pl.ds(start, win), 2 * COL + c0:2 * COL + c0 + 128]
                qs = jnp.concatenate([jnp.where(lo, q, zero), jnp.where(lo, zero, q)], axis=0)
                sc = lax.dot_general(qs, k, (((1,), (1,)), ((), ())), preferred_element_type=F32) + bias
                m = jnp.max(sc, axis=-1, keepdims=True)
                p = jnp.exp(sc - m)
                l = jnp.sum(p, axis=-1, keepdims=True)
                ov = _dot(p.astype(BF16), v)
                inv = 1.0 / l
                o = jnp.where(lo, ov[:SPAN] * inv[:SPAN], ov[SPAN:] * inv[SPAN:])
                o_ref[pl.ds(row0, SPAN), s * B_WIDTH + c0:s * B_WIDTH + c0 + 128] = o.astype(BF16)
                lse = m + jnp.log(l)
                lse_tile = jnp.where(lane16 == 2 * jp, lse[:SPAN],
                                     jnp.where(lane16 == 2 * jp + 1, lse[SPAN:], lse_tile))
            lse_ref[pl.ds(row0, SPAN), s * 128:(s + 1) * 128] = lse_tile

        block(0, 0, bias_first)
        if nb > 1:
            def body(i, carry):
                block(pl.multiple_of(i * SPAN, SPAN), pl.multiple_of((i - 1) * SPAN, SPAN), bias_main)
                return carry
            lax.fori_loop(1, nb, body, 0)


def _attn(qkv_g, gi):
    bsz, dl, seq, _ = qkv_g.shape
    ns = max(1, min(dl, TM_PROJ // seq))
    return pl.pallas_call(
        functools.partial(_attn_kernel, ns=ns, seq=seq),
        grid=(bsz, dl // ns),
        in_specs=[pl.BlockSpec((None, ns, seq, 3 * COL), lambda b, r: (b, r, 0, 0))],
        out_specs=[pl.BlockSpec((None, seq, ns * B_WIDTH), lambda b, r: (b, 0, r)),
                   pl.BlockSpec((None, seq, ns * 128), lambda b, r: (b, 0, r))],
        out_shape=[jax.ShapeDtypeStruct((bsz, seq, dl * B_WIDTH), BF16),
                   jax.ShapeDtypeStruct((bsz, seq, dl * 128), F32)],
        compiler_params=pltpu.CompilerParams(dimension_semantics=("parallel", "parallel"),
                                             vmem_limit_bytes=VMEM_LIMIT),
        name=f"attn{dl}",
    )(qkv_g)


def _natural_rows(ref, dl, scr):
    nchunk, tm, _ = scr.shape
    w = nchunk * 128
    per = tm // dl
    for r in range(dl):
        for c in range(nchunk):
            scr[c, pl.ds(r, per, stride=dl), :] = ref[:, r * w + c * 128:r * w + (c + 1) * 128].astype(F32)
    return jnp.concatenate([scr[c] for c in range(nchunk)], axis=1)


def _mix_kernel(ga_ref, gates_ref, o1_ref, o2_ref, o3_ref, l1_ref, l2_ref, l3_ref, x_ref, p_ref,
                wa_ref, wb_ref, wo_ref, wple_ref, wpgr_ref, br_ref, g1_ref, b1_ref,
                base_ref, x1p_ref, route_ref, o2s_ref, o3s_ref, l2s_ref, l3s_ref):
    tm = x_ref.shape[0]
    o2 = _natural_rows(o2_ref, 4, o2s_ref)
    o3 = _natural_rows(o3_ref, 16, o3s_ref)
    l2 = _natural_rows(l2_ref, 4, l2s_ref)
    l3 = _natural_rows(l3_ref, 16, l3s_ref)
    l1 = l1_ref[...]
    mx = jnp.maximum(l1, jnp.maximum(l2, l3))
    e1, e2, e3 = jnp.exp(l1 - mx), jnp.exp(l2 - mx), jnp.exp(l3 - mx)
    inv = 1.0 / (e1 + e2 + e3)
    er = lax.broadcasted_iota(I32, (128, B_WIDTH), 0)
    ec = lax.broadcasted_iota(I32, (128, B_WIDTH), 1)
    expand = jnp.where(er == (ec // B_HEAD_DIM) * 16, 1.0, 0.0).astype(BF16)

    def widen(w):
        hi = w.astype(BF16)
        lo = (w - hi.astype(F32)).astype(BF16)
        return _dot(hi, expand) + _dot(lo, expand)

    ob = widen(e1 * inv) * o1_ref[...].astype(F32) + widen(e2 * inv) * o2 + widen(e3 * inv) * o3
    yb = _dot(ob.astype(BF16), wb_ref[...])
    ya = _dot(ga_ref[...], wa_ref[...])
    pre = gates_ref[:, :D_MODEL].astype(F32) * ya + gates_ref[:, D_MODEL:].astype(F32) * yb
    mix = _dot(pre.astype(BF16), wo_ref[...])
    x1 = _ln(DEEPNORM_ALPHA * x_ref[...] + mix, g1_ref[...], b1_ref[...])
    x1b = x1.astype(BF16)

    z = _dot(x1b, wpgr_ref[...])
    ple = _dot(p_ref[...].astype(BF16), wple_ref[...]) * _sigmoid(z[:, :D_MODEL])
    base_ref[...] = DEEPNORM_ALPHA * x1 + ple
    _store_packed_rows(x1p_ref, x1)

    lg = z[:, D_MODEL:] + br_ref[...]
    lane = lax.broadcasted_iota(I32, (tm, 128), 1).astype(F32)
    big = 1e9
    gl = jnp.where(lane < N_GROUPS, lg, NEG)
    gm = jnp.max(gl, axis=-1, keepdims=True)
    gidx = jnp.min(jnp.where(gl == gm, lane, big), axis=-1, keepdims=True)
    gsum = jnp.sum(jnp.where(lane < N_GROUPS, jnp.exp(gl - gm), 0.0), axis=-1, keepdims=True)
    gprob = 1.0 / gsum
    lo_lane = N_GROUPS + N_EXPERTS * gidx
    el = jnp.where((lane >= lo_lane) & (lane < lo_lane + N_EXPERTS), lg, NEG)
    v1 = jnp.max(el, axis=-1, keepdims=True)
    i1 = jnp.min(jnp.where(el == v1, lane, big), axis=-1, keepdims=True)
    el2 = jnp.where(lane == i1, NEG, el)
    v2 = jnp.max(el2, axis=-1, keepdims=True)
    i2 = jnp.min(jnp.where(el2 == v2, lane, big), axis=-1, keepdims=True)
    t = jnp.exp(v2 - v1)
    w1 = 1.0 / (1.0 + t)
    w2 = t * w1
    route_ref[...] = jnp.where(lane == 0, i1 - N_GROUPS,
                               jnp.where(lane == 1, i2 - N_GROUPS,
                                         jnp.where(lane == 2, gprob * w1,
                                                   jnp.where(lane == 3, gprob * w2, 0.0))))


def _mix(ga, gates, o1, o2, o3, l1, l2, l3, x2, p2, wa, wb, wo, wple, wpgr, br, g1, b1):
    n = x2.shape[0]
    bsz = o2.shape[0]
    tm = TM_MIX
    tiles = n // bsz // tm
    rows = lambda w: pl.BlockSpec((tm, w), lambda i: (i, 0))
    grouped = lambda a, dl: pl.BlockSpec((None, tm // dl, a.shape[2]), lambda i: (i // tiles, i % tiles, 0))
    full = lambda a: pl.BlockSpec(a.shape, lambda i: (0,) * a.ndim)
    return pl.pallas_call(
        _mix_kernel,
        grid=(n // tm,),
        in_specs=[rows(A_WIDTH), rows(2 * D_MODEL), rows(B_WIDTH), grouped(o2, 4), grouped(o3, 16),
                  rows(128), grouped(l2, 4), grouped(l3, 16), rows(D_MODEL), rows(PLE_DIM),
                  full(wa), full(wb), full(wo), full(wple), full(wpgr), full(br), full(g1), full(b1)],
        out_specs=[rows(D_MODEL), pl.BlockSpec((tm // 8, 32, 128), lambda i: (i, 0, 0)), rows(128)],
        out_shape=[jax.ShapeDtypeStruct((n, D_MODEL), F32),
                   jax.ShapeDtypeStruct((n // 8, 32, 128), U32),
                   jax.ShapeDtypeStruct((n, 128), F32)],
        scratch_shapes=[pltpu.VMEM((B_WIDTH // 128, tm, 128), F32), pltpu.VMEM((B_WIDTH // 128, tm, 128), F32),
                        pltpu.VMEM((1, tm, 128), F32), pltpu.VMEM((1, tm, 128), F32)],
        compiler_params=pltpu.CompilerParams(dimension_semantics=("parallel",), vmem_limit_bytes=VMEM_LIMIT),
        name="mix",
    )(ga, gates, o1, o2, o3, l1, l2, l3, x2, p2, wa, wb, wo, wple, wpgr, br, g1, b1)


SC_WINDOW = 128


def _sc_mesh():
    return plsc.VectorSubcoreMesh(core_axis_name="core", subcore_axis_name="subcore")


def _sc_scatter_rows(rows, dst, n_out):
    r = rows.shape[0]
    m = dst.shape[0]
    nblk = r // SC_WINDOW

    @pl.kernel(out_type=jax.ShapeDtypeStruct((n_out, 128), rows.dtype), mesh=_sc_mesh())
    def scatter(rows_hbm, dst_hbm, out_hbm):
        def body(rows_vmem, dst_vmem):
            pltpu.sync_copy(rows_vmem, out_hbm.at[dst_vmem.at[0]])

        pltpu.emit_pipeline(
            body,
            grid=(m // SC_WINDOW,),
            in_specs=[pl.BlockSpec((SC_WINDOW, 128), lambda i: (i % nblk, 0)),
                      pl.BlockSpec((1, SC_WINDOW), lambda i: (0, i))],
            out_specs=[],
            core_axis_name=("core", "subcore"),
            dimension_semantics=(pltpu.PARALLEL,),
        )(rows_hbm, dst_hbm)

    return scatter(rows, dst.reshape(1, m))


def _sc_gather_rows(table, src):
    m = src.shape[0]

    @pl.kernel(out_type=jax.ShapeDtypeStruct((m, 128), table.dtype), mesh=_sc_mesh())
    def gather(table_hbm, src_hbm, out_hbm):
        def body(src_vmem, out_vmem):
            pltpu.sync_copy(table_hbm.at[src_vmem.at[0]], out_vmem)

        pltpu.emit_pipeline(
            body,
            grid=(m // SC_WINDOW,),
            in_specs=[pl.BlockSpec((1, SC_WINDOW), lambda i: (0, i))],
            out_specs=[pl.BlockSpec((SC_WINDOW, 128), lambda i: (i, 0))],
            core_axis_name=("core", "subcore"),
            dimension_semantics=(pltpu.PARALLEL,),
        )(src_hbm, out_hbm)

    return gather(table, src.reshape(1, m))


def _moe_kernel(te_ref, tv_ref, xs_ref, wg_ref, wu_ref, wd_ref, ys_ref):
    del te_ref
    t = pl.program_id(0)

    @pl.when(tv_ref[t] == 1)
    def _():
        xb = _load_packed_rows(xs_ref).astype(BF16)
        g = _dot(xb, wg_ref[...])
        u = _dot(xb, wu_ref[...])
        h = (g * _sigmoid(g) * u).astype(BF16)
        _store_packed_rows(ys_ref, _dot(h, wd_ref[...]))

    @pl.when(tv_ref[t] == 0)
    def _():
        ys_ref[...] = jnp.zeros(ys_ref.shape, U32)


def _moe(tile_expert, tile_valid, xs, wg, wu, wd):
    tm = TM_MOE
    nt = tile_expert.shape[0]
    rows = pl.BlockSpec((tm // 8, 32, 128), lambda t, te, tv: (t, 0, 0))
    return pl.pallas_call(
        _moe_kernel,
        grid_spec=pltpu.PrefetchScalarGridSpec(
            num_scalar_prefetch=2,
            grid=(nt,),
            in_specs=[rows,
                      pl.BlockSpec((None, D_MODEL, D_EXPERT), lambda t, te, tv: (te[t], 0, 0)),
                      pl.BlockSpec((None, D_MODEL, D_EXPERT), lambda t, te, tv: (te[t], 0, 0)),
                      pl.BlockSpec((None, D_EXPERT, D_MODEL), lambda t, te, tv: (te[t], 0, 0))],
            out_specs=rows),
        out_shape=jax.ShapeDtypeStruct((nt * tm // 8, 32, 128), U32),
        compiler_params=pltpu.CompilerParams(dimension_semantics=("parallel",), vmem_limit_bytes=VMEM_LIMIT),
        name="moe",
    )(tile_expert, tile_valid, xs, wg, wu, wd)


def _final_kernel(y0_ref, y1_ref, base_ref, route_ref, g2_ref, b2_ref, out_ref):
    ffn = route_ref[:, 2:3] * _load_packed_rows(y0_ref) + route_ref[:, 3:4] * _load_packed_rows(y1_ref)
    out_ref[...] = _ln(base_ref[...] + ffn, g2_ref[...], b2_ref[...])


def _final(yg, base, route, g2, b2):
    n = base.shape[0]
    tm = TM_FIN
    nt = n // tm
    return pl.pallas_call(
        _final_kernel,
        grid=(nt,),
        in_specs=[pl.BlockSpec((tm // 8, 32, 128), lambda t: (t, 0, 0)),
                  pl.BlockSpec((tm // 8, 32, 128), lambda t: (t + nt, 0, 0)),
                  pl.BlockSpec((tm, D_MODEL), lambda t: (t, 0)),
                  pl.BlockSpec((tm, 128), lambda t: (t, 0)),
                  pl.BlockSpec((1, D_MODEL), lambda t: (0, 0)),
                  pl.BlockSpec((1, D_MODEL), lambda t: (0, 0))],
        out_specs=pl.BlockSpec((tm, D_MODEL), lambda t: (t, 0)),
        out_shape=jax.ShapeDtypeStruct((n, D_MODEL), F32),
        compiler_params=pltpu.CompilerParams(dimension_semantics=("parallel",), vmem_limit_bytes=VMEM_LIMIT),
        name="final",
    )(yg, yg, base, route, g2, b2)


def _routing_tables(route, n):
    tm = TM_MOE
    nt = (2 * n) // tm + N_EXPERTS_TOTAL
    ef = route[:, 0:2].astype(I32).T.reshape(-1)
    onehot = (ef[:, None] == jnp.arange(N_EXPERTS_TOTAL, dtype=I32)[None, :]).astype(I32)
    csum = jnp.cumsum(onehot, axis=0)
    counts = csum[-1]
    rank = jnp.sum(csum * onehot, axis=1) - 1
    padded = ((counts + tm - 1) // tm) * tm
    ends = jnp.cumsum(padded)
    offs = ends - padded
    pos = jnp.sum(onehot * offs[None, :], axis=1) + rank
    tile_start = jnp.arange(nt, dtype=I32) * tm
    tile_expert = jnp.minimum(jnp.sum((tile_start[:, None] >= ends[None, :]).astype(I32), axis=1),
                              N_EXPERTS_TOTAL - 1).astype(I32)
    tile_valid = (tile_start < ends[-1]).astype(I32)
    pg = pos.reshape(2 * n // 8, 1, 8)
    piece = (pg // 8) * (8 * SUBROWS) + pg % 8 + 8 * jnp.arange(SUBROWS, dtype=I32)[None, :, None]
    return tile_expert, tile_valid, piece.reshape(-1)


def kernel(x, p, w_in, a_ln_g, a_ln_b, a_ws, a_bs, w_a_proj, w_b_proj, w_o, ln1_g, ln1_b, w_group_router,
           b_group_router, w_expert_router, b_expert_router, w_gate, w_up, w_down, w_ple, w_ple_gate,
           ln2_g, ln2_b):
    bsz, s, d = x.shape
    n = bsz * s
    assert d == D_MODEL and s % (SPAN * max(B_DILATIONS)) == 0 and n % TM_PROJ == 0
    assert w_in.shape[0] == 1, "one layer"

    w_in_b = w_in[0].astype(BF16)
    a_bias = jnp.repeat(a_bs[0].T, A_WIDTH // 8, axis=1)

    ga, gates, qkv1, qkv2, qkv3 = _proj(x, w_in_b, a_ln_g, a_ln_b, a_ws[0], a_bias)
    o1, l1 = _attn(qkv1.reshape(bsz, 1, s, 3 * COL), 0)
    o2, l2 = _attn(qkv2, 1)
    o3, l3 = _attn(qkv3, 2)

    pad = 128 - N_GROUPS - N_EXPERTS_TOTAL
    wpgr = jnp.concatenate([w_ple_gate[0], w_group_router[0], w_expert_router[0].reshape(d, N_EXPERTS_TOTAL),
                            jnp.zeros((d, pad), F32)], axis=1).astype(BF16)
    br = jnp.concatenate([b_group_router[0], b_expert_router[0].reshape(-1), jnp.zeros((pad,), F32)])[None, :]
    base, x1p, route = _mix(
        ga, gates, o1.reshape(n, B_WIDTH), o2, o3, l1.reshape(n, 128), l2, l3,
        x.reshape(n, d), p[0].reshape(n, PLE_DIM),
        w_a_proj[0].astype(BF16), w_b_proj[0].astype(BF16), w_o[0].astype(BF16), w_ple[0].astype(BF16),
        wpgr, br, ln1_g, ln1_b)

    tile_expert, tile_valid, piece = _routing_tables(route, n)
    nt = tile_expert.shape[0]
    xs = _sc_scatter_rows(x1p.reshape(n * SUBROWS, 128), piece, nt * TM_MOE * SUBROWS)
    ys = _moe(tile_expert, tile_valid, xs.reshape(nt * TM_MOE // 8, 32, 128),
              w_gate[0].reshape(N_EXPERTS_TOTAL, d, D_EXPERT).astype(BF16),
              w_up[0].reshape(N_EXPERTS_TOTAL, d, D_EXPERT).astype(BF16),
              w_down[0].reshape(N_EXPERTS_TOTAL, D_EXPERT, d).astype(BF16))
    yg = _sc_gather_rows(ys.reshape(nt * TM_MOE * SUBROWS, 128), piece)
    out = _final(yg.reshape(2 * n // 8, 32, 128), base, route, ln2_g, ln2_b)
    return out.reshape(bsz, s, d)
```

```python
import functools

import jax
import jax.numpy as jnp
from jax import lax
from jax.experimental import pallas as pl
from jax.experimental.pallas import tpu as pltpu
from jax.experimental.pallas import tpu_sc as plsc

F32 = jnp.float32
BF16 = jnp.bfloat16
U32 = jnp.uint32
I32 = jnp.int32

D_MODEL = 1024
PLE_DIM = 256
A_WIDTH = 512
A_CHUNK = 128
B_HEAD_DIM = 64
B_HEADS = 8
B_WIDTH = 512
B_DILATIONS = (1, 4, 16)
SPAN = 128
N_GROUPS = 4
N_EXPERTS = 8
N_EXPERTS_TOTAL = N_GROUPS * N_EXPERTS
D_EXPERT = 256
DEEPNORM_ALPHA = 2.0 ** 0.25
LN_EPS = 1e-5
COL = 512
NEG = -1e30

VMEM_LIMIT = 56 * 1024 * 1024

TM_PROJ = 512
TM_MIX = 256
TM_MOE = 256
TM_FIN = 256


def _ln(x, g, b):
    mu = jnp.mean(x, axis=-1, keepdims=True)
    xc = x - mu
    var = jnp.mean(xc * xc, axis=-1, keepdims=True)
    return xc * lax.rsqrt(var + LN_EPS) * g + b


def _gelu_tanh(x):
    return 0.5 * x * (1.0 + jnp.tanh(0.7978845608028654 * (x + 0.044715 * (x * x * x))))


def _sigmoid(x):
    return 1.0 / (1.0 + jnp.exp(-x))


def _dot(a, b):
    return jnp.dot(a, b, preferred_element_type=F32)


PACK_W = D_MODEL // 2
SUBROWS = PACK_W // 128


def _store_packed_rows(ref, x):
    m = x.shape[0]
    xb = x.astype(BF16).astype(F32)
    lo = pltpu.bitcast(xb[:, :PACK_W], U32) >> 16
    hi = pltpu.bitcast(xb[:, PACK_W:], U32) & jnp.uint32(0xFFFF0000)
    w = hi | lo
    for j in range(SUBROWS):
        ref[:, 8 * j:8 * (j + 1), :] = w[:, 128 * j:128 * (j + 1)].reshape(m // 8, 8, 128)


def _load_packed_rows(ref):
    m = ref.shape[0] * 8
    ws = [ref[:, 8 * j:8 * (j + 1), :].reshape(m, 128) for j in range(SUBROWS)]
    lo = [pltpu.bitcast(w << 16, F32) for w in ws]
    hi = [pltpu.bitcast(w & jnp.uint32(0xFFFF0000), F32) for w in ws]
    return jnp.concatenate(lo + hi, axis=1)


def _proj_kernel(x_ref, *refs):
    w = refs[:15]
    lng_ref, lnb_ref, ws_ref, bias_ref = refs[15:19]
    ga_ref, gates_ref, qkv1_ref, qkv2_ref, qkv3_ref = refs[19:24]
    xc_ref = refs[24]
    tm = x_ref.shape[0]
    xb = x_ref[...].astype(BF16)

    u = _gelu_tanh(_dot(xb, w[0][...]))
    v = _gelu_tanh(_dot(xb, w[1][...]))
    vn = _ln(v, lng_ref[...], lnb_ref[...]).astype(BF16)

    row = lax.broadcasted_iota(I32, (A_CHUNK, A_CHUNK), 0)
    colm = lax.broadcasted_iota(I32, (A_CHUNK, A_CHUNK), 1)
    causal = colm <= row
    lo = colm < 64
    zero = jnp.zeros((A_CHUNK, A_CHUNK), BF16)
    wcat = []
    for j in range(4):
        w0 = jnp.where(causal, ws_ref[2 * j], 0.0).astype(BF16)
        w1 = jnp.where(causal, ws_ref[2 * j + 1], 0.0).astype(BF16)
        wcat.append(jnp.concatenate([w0, w1], axis=1))
    for c in range(tm // A_CHUNK):
        r0 = c * A_CHUNK
        for j in range(4):
            c0 = j * 128
            vt = vn[r0:r0 + A_CHUNK, c0:c0 + 128]
            rhs = jnp.concatenate([jnp.where(lo, vt, zero), jnp.where(lo, zero, vt)], axis=0)
            mixed = _dot(wcat[j], rhs) + bias_ref[:, c0:c0 + 128]
            ga_ref[r0:r0 + A_CHUNK, c0:c0 + 128] = (u[r0:r0 + A_CHUNK, c0:c0 + 128] * mixed).astype(BF16)

    for i in range(4):
        gates_ref[:, i * COL:(i + 1) * COL] = _sigmoid(_dot(xb, w[11 + i][...])).astype(BF16)
    for j in range(3):
        qkv1_ref[:, j * COL:(j + 1) * COL] = _dot(xb, w[2 + 3 * j][...]).astype(BF16)

    for c in range(D_MODEL // 128):
        xc_ref[c] = x_ref[:, c * 128:(c + 1) * 128]
    for gi, out_ref in ((1, qkv2_ref), (2, qkv3_ref)):
        dl = B_DILATIONS[gi]
        per = tm // dl
        xp = jnp.concatenate(
            [jnp.concatenate([xc_ref[c, pl.ds(r, per, stride=dl), :] for c in range(D_MODEL // 128)], axis=1)
             for r in range(dl)], axis=0).astype(BF16)
        for j in range(3):
            res = _dot(xp, w[2 + 3 * j + gi][...]).astype(BF16)
            for r in range(dl):
                out_ref[r, :, j * COL:(j + 1) * COL] = res[r * per:(r + 1) * per]


def _proj(x, w_in_b, a_ln_g, a_ln_b, a_ws, a_bias):
    bsz, s, _ = x.shape
    n = bsz * s
    tm = TM_PROJ
    tiles = s // tm
    x2 = x.reshape(n, D_MODEL)
    wspec = lambda j: pl.BlockSpec((D_MODEL, COL), lambda i, j=j: (0, j), pipeline_mode=pl.Buffered(1))
    full = lambda shape: pl.BlockSpec(shape, lambda i: (0,) * len(shape))
    rows = lambda width: pl.BlockSpec((tm, width), lambda i: (i, 0))
    dil = lambda dl: pl.BlockSpec((None, dl, tm // dl, 3 * COL), lambda i: (i // tiles, 0, i % tiles, 0))
    return pl.pallas_call(
        _proj_kernel,
        grid=(n // tm,),
        in_specs=[rows(D_MODEL)] + [wspec(j) for j in range(15)]
                 + [full((1, A_WIDTH)), full((1, A_WIDTH)), full((8, A_CHUNK, A_CHUNK)), full((A_CHUNK, A_WIDTH))],
        out_specs=[rows(A_WIDTH), rows(4 * COL), rows(3 * COL), dil(4), dil(16)],
        out_shape=[jax.ShapeDtypeStruct((n, A_WIDTH), BF16),
                   jax.ShapeDtypeStruct((n, 4 * COL), BF16),
                   jax.ShapeDtypeStruct((n, 3 * COL), BF16),
                   jax.ShapeDtypeStruct((bsz, 4, s // 4, 3 * COL), BF16),
                   jax.ShapeDtypeStruct((bsz, 16, s // 16, 3 * COL), BF16)],
        scratch_shapes=[pltpu.VMEM((D_MODEL // 128, tm, 128), F32)],
        compiler_params=pltpu.CompilerParams(dimension_semantics=("parallel",), vmem_limit_bytes=VMEM_LIMIT),
        name="proj",
    )(x2, *([w_in_b] * 15), a_ln_g, a_ln_b, a_ws, a_bias)


def _attn_kernel(qkv_ref, o_ref, lse_ref, *, ns, seq):
    nb = seq // SPAN
    lane = lax.broadcasted_iota(I32, (SPAN, 128), 1)
    lo = lane < 64
    lane16 = lane // 16
    qi = lax.broadcasted_iota(I32, (SPAN, 2 * SPAN), 0)
    ki = lax.broadcasted_iota(I32, (SPAN, 2 * SPAN), 1)
    causal = lax.broadcasted_iota(I32, (SPAN, SPAN), 1) <= lax.broadcasted_iota(I32, (SPAN, SPAN), 0)
    bias_first = jnp.where(causal, 0.0, NEG).astype(F32)
    bias_first = jnp.concatenate([bias_first, bias_first], axis=0)
    bias_main = jnp.where((ki >= qi) & (ki <= qi + SPAN), 0.0, NEG).astype(F32)
    bias_main = jnp.concatenate([bias_main, bias_main], axis=0)
    zero = jnp.zeros((SPAN, 128), BF16)

    for s in range(ns):
        def block(row0, start, bias, s=s):
            win = bias.shape[1]
            lse_tile = jnp.zeros((SPAN, 128), F32)
            for jp in range(B_HEADS // 2):
                c0 = jp * 128
                q = qkv_ref[s, pl.ds(row0, SPAN), c0:c0 + 128] * jnp.asarray(0.125, BF16)
                k = qkv_ref[s, pl.ds(start, win), COL + c0:COL + c0 + 128]
                v = qkv_ref[s, pl.ds(start, win), 2 * COL + c0:2 * COL + c0 + 128]
                qs = jnp.concatenate([jnp.where(lo, q, zero), jnp.where(lo, zero, q)], axis=0)
                sc = lax.dot_general(qs, k, (((1,), (1,)), ((), ())), preferred_element_type=F32) + bias
                m = jnp.max(sc, axis=-1, keepdims=True)
                p = jnp.exp(sc - m)
                l = jnp.sum(p, axis=-1, keepdims=True)
                ov = _dot(p.astype(BF16), v)
                inv = 1.0 / l
                o = jnp.where(lo, ov[:SPAN] * inv[:SPAN], ov[SPAN:] * inv[SPAN:])
                o_ref[pl.ds(row0, SPAN), s * B_WIDTH + c0:s * B_WIDTH + c0 + 128] = o.astype(BF16)
                lse = m + jnp.log(l)
                lse_tile = jnp.where(lane16 == 2 * jp, lse[:SPAN],
                                     jnp.where(lane16 == 2 * jp + 1, lse[SPAN:], lse_tile))
            lse_ref[pl.ds(row0, SPAN), s * 128:(s + 1) * 128] = lse_tile

        block(0, 0, bias_first)
        if nb > 1:
            def body(i, carry):
                block(pl.multiple_of(i * SPAN, SPAN), pl.multiple_of((i - 1) * SPAN, SPAN), bias_main)
                return carry
            lax.fori_loop(1, nb, body, 0)


def _attn(qkv_g, gi):
    bsz, dl, seq, _ = qkv_g.shape
    ns = max(1, min(dl, TM_PROJ // seq))
    return pl.pallas_call(
        functools.partial(_attn_kernel, ns=ns, seq=seq),
        grid=(bsz, dl // ns),
        in_specs=[pl.BlockSpec((None, ns, seq, 3 * COL), lambda b, r: (b, r, 0, 0))],
        out_specs=[pl.BlockSpec((None, seq, ns * B_WIDTH), lambda b, r: (b, 0, r)),
                   pl.BlockSpec((None, seq, ns * 128), lambda b, r: (b, 0, r))],
        out_shape=[jax.ShapeDtypeStruct((bsz, seq, dl * B_WIDTH), BF16),
                   jax.ShapeDtypeStruct((bsz, seq, dl * 128), F32)],
        compiler_params=pltpu.CompilerParams(dimension_semantics=("parallel", "parallel"),
                                             vmem_limit_bytes=VMEM_LIMIT),
        name=f"attn{dl}",
    )(qkv_g)


def _natural_rows(ref, dl, scr):
    nchunk, tm, _ = scr.shape
    w = nchunk * 128
    per = tm // dl
    for r in range(dl):
        for c in range(nchunk):
            scr[c, pl.ds(r, per, stride=dl), :] = ref[:, r * w + c * 128:r * w + (c + 1) * 128].astype(F32)
    return jnp.concatenate([scr[c] for c in range(nchunk)], axis=1)


def _mix_kernel(ga_ref, gates_ref, o1_ref, o2_ref, o3_ref, l1_ref, l2_ref, l3_ref, x_ref, p_ref,
                wa_ref, wb_ref, wo_ref, wple_ref, wpgr_ref, br_ref, g1_ref, b1_ref,
                base_ref, x1p_ref, route_ref, o2s_ref, o3s_ref, l2s_ref, l3s_ref):
    tm = x_ref.shape[0]
    o2 = _natural_rows(o2_ref, 4, o2s_ref)
    o3 = _natural_rows(o3_ref, 16, o3s_ref)
    l2 = _natural_rows(l2_ref, 4, l2s_ref)
    l3 = _natural_rows(l3_ref, 16, l3s_ref)
    l1 = l1_ref[...]
    mx = jnp.maximum(l1, jnp.maximum(l2, l3))
    e1, e2, e3 = jnp.exp(l1 - mx), jnp.exp(l2 - mx), jnp.exp(l3 - mx)
    inv = 1.0 / (e1 + e2 + e3)
    er = lax.broadcasted_iota(I32, (256, B_WIDTH), 0)
    ec = lax.broadcasted_iota(I32, (256, B_WIDTH), 1)
    expand = jnp.where(er % 128 == (ec // B_HEAD_DIM) * 16, 1.0, 0.0).astype(BF16)

    def widen(w):
        hi = w.astype(BF16)
        lo = (w - hi.astype(F32)).astype(BF16)
        return _dot(jnp.concatenate([hi, lo], axis=1), expand)

    ob = widen(e1 * inv) * o1_ref[...].astype(F32) + widen(e2 * inv) * o2 + widen(e3 * inv) * o3
    yb = _dot(ob.astype(BF16), wb_ref[...])
    ya = _dot(ga_ref[...], wa_ref[...])
    pre = gates_ref[:, :D_MODEL].astype(F32) * ya + gates_ref[:, D_MODEL:].astype(F32) * yb
    mix = _dot(pre.astype(BF16), wo_ref[...])
    x1 = _ln(DEEPNORM_ALPHA * x_ref[...] + mix, g1_ref[...], b1_ref[...])
    x1b = x1.astype(BF16)

    z = _dot(x1b, wpgr_ref[...])
    ple = _dot(p_ref[...].astype(BF16), wple_ref[...]) * _sigmoid(z[:, :D_MODEL])
    base_ref[...] = DEEPNORM_ALPHA * x1 + ple
    _store_packed_rows(x1p_ref, x1)

    lg = z[:, D_MODEL:] + br_ref[...]
    lane = lax.broadcasted_iota(I32, (tm, 128), 1).astype(F32)
    big = 1e9
    gl = jnp.where(lane < N_GROUPS, lg, NEG)
    gm = jnp.max(gl, axis=-1, keepdims=True)
    gidx = jnp.min(jnp.where(gl == gm, lane, big), axis=-1, keepdims=True)
    gsum = jnp.sum(jnp.where(lane < N_GROUPS, jnp.exp(gl - gm), 0.0), axis=-1, keepdims=True)
    gprob = 1.0 / gsum
    lo_lane = N_GROUPS + N_EXPERTS * gidx
    el = jnp.where((lane >= lo_lane) & (lane < lo_lane + N_EXPERTS), lg, NEG)
    v1 = jnp.max(el, axis=-1, keepdims=True)
    i1 = jnp.min(jnp.where(el == v1, lane, big), axis=-1, keepdims=True)
    el2 = jnp.where(lane == i1, NEG, el)
    v2 = jnp.max(el2, axis=-1, keepdims=True)
    i2 = jnp.min(jnp.where(el2 == v2, lane, big), axis=-1, keepdims=True)
    t = jnp.exp(v2 - v1)
    w1 = 1.0 / (1.0 + t)
    w2 = t * w1
    route_ref[...] = jnp.where(lane == 0, i1 - N_GROUPS,
                               jnp.where(lane == 1, i2 - N_GROUPS,
                                         jnp.where(lane == 2, gprob * w1,
                                                   jnp.where(lane == 3, gprob * w2, 0.0))))


def _mix(ga, gates, o1, o2, o3, l1, l2, l3, x2, p2, wa, wb, wo, wple, wpgr, br, g1, b1):
    n = x2.shape[0]
    bsz = o2.shape[0]
    tm = TM_MIX
    tiles = n // bsz // tm
    rows = lambda w: pl.BlockSpec((tm, w), lambda i: (i, 0))
    grouped = lambda a, dl: pl.BlockSpec((None, tm // dl, a.shape[2]), lambda i: (i // tiles, i % tiles, 0))
    full = lambda a: pl.BlockSpec(a.shape, lambda i: (0,) * a.ndim)
    return pl.pallas_call(
        _mix_kernel,
        grid=(n // tm,),
        in_specs=[rows(A_WIDTH), rows(2 * D_MODEL), rows(B_WIDTH), grouped(o2, 4), grouped(o3, 16),
                  rows(128), grouped(l2, 4), grouped(l3, 16), rows(D_MODEL), rows(PLE_DIM),
                  full(wa), full(wb), full(wo), full(wple), full(wpgr), full(br), full(g1), full(b1)],
        out_specs=[rows(D_MODEL), pl.BlockSpec((tm // 8, 32, 128), lambda i: (i, 0, 0)), rows(128)],
        out_shape=[jax.ShapeDtypeStruct((n, D_MODEL), F32),
                   jax.ShapeDtypeStruct((n // 8, 32, 128), U32),
                   jax.ShapeDtypeStruct((n, 128), F32)],
        scratch_shapes=[pltpu.VMEM((B_WIDTH // 128, tm, 128), F32), pltpu.VMEM((B_WIDTH // 128, tm, 128), F32),
                        pltpu.VMEM((1, tm, 128), F32), pltpu.VMEM((1, tm, 128), F32)],
        compiler_params=pltpu.CompilerParams(dimension_semantics=("parallel",), vmem_limit_bytes=VMEM_LIMIT),
        name="mix",
    )(ga, gates, o1, o2, o3, l1, l2, l3, x2, p2, wa, wb, wo, wple, wpgr, br, g1, b1)


SC_WINDOW = 128


def _sc_mesh():
    return plsc.VectorSubcoreMesh(core_axis_name="core", subcore_axis_name="subcore")


def _sc_scatter_rows(rows, dst, n_out):
    r = rows.shape[0]
    m = dst.shape[0]
    nblk = r // SC_WINDOW

    @pl.kernel(out_type=jax.ShapeDtypeStruct((n_out, 128), rows.dtype), mesh=_sc_mesh())
    def scatter(rows_hbm, dst_hbm, out_hbm):
        def body(rows_vmem, dst_vmem):
            pltpu.sync_copy(rows_vmem, out_hbm.at[dst_vmem.at[0]])

        pltpu.emit_pipeline(
            body,
            grid=(m // SC_WINDOW,),
            in_specs=[pl.BlockSpec((SC_WINDOW, 128), lambda i: (i % nblk, 0)),
                      pl.BlockSpec((1, SC_WINDOW), lambda i: (0, i))],
            out_specs=[],
            core_axis_name=("core", "subcore"),
            dimension_semantics=(pltpu.PARALLEL,),
        )(rows_hbm, dst_hbm)

    return scatter(rows, dst.reshape(1, m))


def _sc_gather_rows(table, src):
    m = src.shape[0]

    @pl.kernel(out_type=jax.ShapeDtypeStruct((m, 128), table.dtype), mesh=_sc_mesh())
    def gather(table_hbm, src_hbm, out_hbm):
        def body(src_vmem, out_vmem):
            pltpu.sync_copy(table_hbm.at[src_vmem.at[0]], out_vmem)

        pltpu.emit_pipeline(
            body,
            grid=(m // SC_WINDOW,),
            in_specs=[pl.BlockSpec((1, SC_WINDOW), lambda i: (0, i))],
            out_specs=[pl.BlockSpec((SC_WINDOW, 128), lambda i: (i, 0))],
            core_axis_name=("core", "subcore"),
            dimension_semantics=(pltpu.PARALLEL,),
        )(src_hbm, out_hbm)

    return gather(table, src.reshape(1, m))


def _moe_kernel(te_ref, tv_ref, xs_ref, wg_ref, wu_ref, wd_ref, ys_ref, wgb_ref, wub_ref, wdb_ref):
    t = pl.program_id(0)

    @pl.when((t == 0) | (te_ref[t] != te_ref[jnp.maximum(t - 1, 0)]))
    def _():
        wgb_ref[...] = wg_ref[...].astype(BF16)
        wub_ref[...] = wu_ref[...].astype(BF16)
        wdb_ref[...] = wd_ref[...].astype(BF16)

    @pl.when(tv_ref[t] == 1)
    def _():
        xb = _load_packed_rows(xs_ref).astype(BF16)
        g = _dot(xb, wgb_ref[...])
        u = _dot(xb, wub_ref[...])
        h = (g * _sigmoid(g) * u).astype(BF16)
        _store_packed_rows(ys_ref, _dot(h, wdb_ref[...]))

    @pl.when(tv_ref[t] == 0)
    def _():
        ys_ref[...] = jnp.zeros(ys_ref.shape, U32)


def _moe(tile_expert, tile_valid, xs, wg, wu, wd):
    tm = TM_MOE
    nt = tile_expert.shape[0]
    rows = pl.BlockSpec((tm // 8, 32, 128), lambda t, te, tv: (t, 0, 0))
    return pl.pallas_call(
        _moe_kernel,
        grid_spec=pltpu.PrefetchScalarGridSpec(
            num_scalar_prefetch=2,
            grid=(nt,),
            in_specs=[rows,
                      pl.BlockSpec((None, D_MODEL, D_EXPERT), lambda t, te, tv: (te[t], 0, 0)),
                      pl.BlockSpec((None, D_MODEL, D_EXPERT), lambda t, te, tv: (te[t], 0, 0)),
                      pl.BlockSpec((None, D_EXPERT, D_MODEL), lambda t, te, tv: (te[t], 0, 0))],
            out_specs=rows,
            scratch_shapes=[pltpu.VMEM((D_MODEL, D_EXPERT), BF16), pltpu.VMEM((D_MODEL, D_EXPERT), BF16),
                            pltpu.VMEM((D_EXPERT, D_MODEL), BF16)]),
        out_shape=jax.ShapeDtypeStruct((nt * tm // 8, 32, 128), U32),
        compiler_params=pltpu.CompilerParams(dimension_semantics=("arbitrary",), vmem_limit_bytes=VMEM_LIMIT),
        name="moe",
    )(tile_expert, tile_valid, xs, wg, wu, wd)


def _final_kernel(y0_ref, y1_ref, base_ref, route_ref, g2_ref, b2_ref, out_ref):
    ffn = route_ref[:, 2:3] * _load_packed_rows(y0_ref) + route_ref[:, 3:4] * _load_packed_rows(y1_ref)
    out_ref[...] = _ln(base_ref[...] + ffn, g2_ref[...], b2_ref[...])


def _final(yg, base, route, g2, b2):
    n = base.shape[0]
    tm = TM_FIN
    nt = n // tm
    return pl.pallas_call(
        _final_kernel,
        grid=(nt,),
        in_specs=[pl.BlockSpec((tm // 8, 32, 128), lambda t: (t, 0, 0)),
                  pl.BlockSpec((tm // 8, 32, 128), lambda t: (t + nt, 0, 0)),
                  pl.BlockSpec((tm, D_MODEL), lambda t: (t, 0)),
                  pl.BlockSpec((tm, 128), lambda t: (t, 0)),
                  pl.BlockSpec((1, D_MODEL), lambda t: (0, 0)),
                  pl.BlockSpec((1, D_MODEL), lambda t: (0, 0))],
        out_specs=pl.BlockSpec((tm, D_MODEL), lambda t: (t, 0)),
        out_shape=jax.ShapeDtypeStruct((n, D_MODEL), F32),
        compiler_params=pltpu.CompilerParams(dimension_semantics=("parallel",), vmem_limit_bytes=VMEM_LIMIT),
        name="final",
    )(yg, yg, base, route, g2, b2)


def _routing_tables(route, n):
    tm = TM_MOE
    nt = (2 * n) // tm + N_EXPERTS_TOTAL
    ef = route[:, 0:2].astype(I32).T.reshape(-1)
    onehot = (ef[:, None] == jnp.arange(N_EXPERTS_TOTAL, dtype=I32)[None, :]).astype(I32)
    csum = jnp.cumsum(onehot, axis=0)
    counts = csum[-1]
    rank = jnp.sum(csum * onehot, axis=1) - 1
    padded = ((counts + tm - 1) // tm) * tm
    ends = jnp.cumsum(padded)
    offs = ends - padded
    pos = jnp.sum(onehot * offs[None, :], axis=1) + rank
    tile_start = jnp.arange(nt, dtype=I32) * tm
    tile_expert = jnp.minimum(jnp.sum((tile_start[:, None] >= ends[None, :]).astype(I32), axis=1),
                              N_EXPERTS_TOTAL - 1).astype(I32)
    tile_valid = (tile_start < ends[-1]).astype(I32)
    pg = pos.reshape(2 * n // 8, 1, 8)
    piece = (pg // 8) * (8 * SUBROWS) + pg % 8 + 8 * jnp.arange(SUBROWS, dtype=I32)[None, :, None]
    return tile_expert, tile_valid, piece.reshape(-1)


def kernel(x, p, w_in, a_ln_g, a_ln_b, a_ws, a_bs, w_a_proj, w_b_proj, w_o, ln1_g, ln1_b, w_group_router,
           b_group_router, w_expert_router, b_expert_router, w_gate, w_up, w_down, w_ple, w_ple_gate,
           ln2_g, ln2_b):
    bsz, s, d = x.shape
    n = bsz * s
    assert d == D_MODEL and s % (SPAN * max(B_DILATIONS)) == 0 and n % TM_PROJ == 0
    assert w_in.shape[0] == 1, "one layer"

    w_in_b = w_in[0].astype(BF16)
    a_bias = jnp.repeat(a_bs[0].T, A_WIDTH // 8, axis=1)

    ga, gates, qkv1, qkv2, qkv3 = _proj(x, w_in_b, a_ln_g, a_ln_b, a_ws[0], a_bias)
    o1, l1 = _attn(qkv1.reshape(bsz, 1, s, 3 * COL), 0)
    o2, l2 = _attn(qkv2, 1)
    o3, l3 = _attn(qkv3, 2)

    pad = 128 - N_GROUPS - N_EXPERTS_TOTAL
    wpgr = jnp.concatenate([w_ple_gate[0], w_group_router[0], w_expert_router[0].reshape(d, N_EXPERTS_TOTAL),
                            jnp.zeros((d, pad), F32)], axis=1).astype(BF16)
    br = jnp.concatenate([b_group_router[0], b_expert_router[0].reshape(-1), jnp.zeros((pad,), F32)])[None, :]
    base, x1p, route = _mix(
        ga, gates, o1.reshape(n, B_WIDTH), o2, o3, l1.reshape(n, 128), l2, l3,
        x.reshape(n, d), p[0].reshape(n, PLE_DIM),
        w_a_proj[0].astype(BF16), w_b_proj[0].astype(BF16), w_o[0].astype(BF16), w_ple[0].astype(BF16),
        wpgr, br, ln1_g, ln1_b)

    tile_expert, tile_valid, piece = _routing_tables(route, n)
    nt = tile_expert.shape[0]
    xs = _sc_scatter_rows(x1p.reshape(n * SUBROWS, 128), piece, nt * TM_MOE * SUBROWS)
    ys = _moe(tile_expert, tile_valid, xs.reshape(nt * TM_MOE // 8, 32, 128),
              w_gate[0].reshape(N_EXPERTS_TOTAL, d, D_EXPERT), w_up[0].reshape(N_EXPERTS_TOTAL, d, D_EXPERT),
              w_down[0].reshape(N_EXPERTS_TOTAL, D_EXPERT, d))
    yg = _sc_gather_rows(ys.reshape(nt * TM_MOE * SUBROWS, 128), piece)
    out = _final(yg.reshape(2 * n // 8, 32, 128), base, route, ln2_g, ln2_b)
    return out.reshape(bsz, s, d)
```

```python
import functools

import jax
import jax.numpy as jnp
from jax import lax
from jax.experimental import pallas as pl
from jax.experimental.pallas import tpu as pltpu
from jax.experimental.pallas import tpu_sc as plsc

F32 = jnp.float32
BF16 = jnp.bfloat16
U32 = jnp.uint32
I32 = jnp.int32

D_MODEL = 1024
PLE_DIM = 256
A_WIDTH = 512
A_CHUNK = 128
B_HEAD_DIM = 64
B_HEADS = 8
B_WIDTH = 512
B_DILATIONS = (1, 4, 16)
SPAN = 128
N_GROUPS = 4
N_EXPERTS = 8
N_EXPERTS_TOTAL = N_GROUPS * N_EXPERTS
D_EXPERT = 256
DEEPNORM_ALPHA = 2.0 ** 0.25
LN_EPS = 1e-5
COL = 512
NEG = -1e30

VMEM_LIMIT = 56 * 1024 * 1024

TM_PROJ = 512
TM_MIX = 512
TM_MOE = 512
TM_FIN = 512


def _ln(x, g, b):
    mu = jnp.mean(x, axis=-1, keepdims=True)
    xc = x - mu
    var = jnp.mean(xc * xc, axis=-1, keepdims=True)
    return xc * lax.rsqrt(var + LN_EPS) * g + b


def _gelu_tanh(x):
    return 0.5 * x * (1.0 + jnp.tanh(0.7978845608028654 * (x + 0.044715 * (x * x * x))))


def _sigmoid(x):
    return 1.0 / (1.0 + jnp.exp(-x))


def _dot(a, b):
    return jnp.dot(a, b, preferred_element_type=F32)


PACK_W = D_MODEL // 2
SUBROWS = PACK_W // 128


def _store_packed_rows(ref, x):
    m = x.shape[0]
    xb = x.astype(BF16).astype(F32)
    lo = pltpu.bitcast(xb[:, :PACK_W], U32) >> 16
    hi = pltpu.bitcast(xb[:, PACK_W:], U32) & jnp.uint32(0xFFFF0000)
    w = hi | lo
    for j in range(SUBROWS):
        ref[:, 8 * j:8 * (j + 1), :] = w[:, 128 * j:128 * (j + 1)].reshape(m // 8, 8, 128)


def _load_packed_rows(ref):
    m = ref.shape[0] * 8
    ws = [ref[:, 8 * j:8 * (j + 1), :].reshape(m, 128) for j in range(SUBROWS)]
    lo = [pltpu.bitcast(w << 16, F32) for w in ws]
    hi = [pltpu.bitcast(w & jnp.uint32(0xFFFF0000), F32) for w in ws]
    return jnp.concatenate(lo + hi, axis=1)


def _proj_kernel(x_ref, *refs):
    w = refs[:15]
    lng_ref, lnb_ref, ws_ref, bias_ref = refs[15:19]
    ga_ref, gates_ref, qkv1_ref, qkv2_ref, qkv3_ref = refs[19:24]
    xc_ref = refs[24]
    tm = x_ref.shape[0]
    xb = x_ref[...].astype(BF16)

    u = _gelu_tanh(_dot(xb, w[0][...]))
    v = _gelu_tanh(_dot(xb, w[1][...]))
    vn = _ln(v, lng_ref[...], lnb_ref[...]).astype(BF16)

    row = lax.broadcasted_iota(I32, (A_CHUNK, A_CHUNK), 0)
    colm = lax.broadcasted_iota(I32, (A_CHUNK, A_CHUNK), 1)
    causal = colm <= row
    lo = colm < 64
    zero = jnp.zeros((A_CHUNK, A_CHUNK), BF16)
    wcat = []
    for j in range(4):
        w0 = jnp.where(causal, ws_ref[2 * j], 0.0).astype(BF16)
        w1 = jnp.where(causal, ws_ref[2 * j + 1], 0.0).astype(BF16)
        wcat.append(jnp.concatenate([w0, w1], axis=1))
    for c in range(tm // A_CHUNK):
        r0 = c * A_CHUNK
        for j in range(4):
            c0 = j * 128
            vt = vn[r0:r0 + A_CHUNK, c0:c0 + 128]
            rhs = jnp.concatenate([jnp.where(lo, vt, zero), jnp.where(lo, zero, vt)], axis=0)
            mixed = _dot(wcat[j], rhs) + bias_ref[:, c0:c0 + 128]
            ga_ref[r0:r0 + A_CHUNK, c0:c0 + 128] = (u[r0:r0 + A_CHUNK, c0:c0 + 128] * mixed).astype(BF16)

    for i in range(4):
        gates_ref[:, i * COL:(i + 1) * COL] = _sigmoid(_dot(xb, w[11 + i][...])).astype(BF16)
    for j in range(3):
        qkv1_ref[:, j * COL:(j + 1) * COL] = _dot(xb, w[2 + 3 * j][...]).astype(BF16)

    for c in range(D_MODEL // 128):
        xc_ref[c] = x_ref[:, c * 128:(c + 1) * 128]
    for gi, out_ref in ((1, qkv2_ref), (2, qkv3_ref)):
        dl = B_DILATIONS[gi]
        per = tm // dl
        xp = jnp.concatenate(
            [jnp.concatenate([xc_ref[c, pl.ds(r, per, stride=dl), :] for c in range(D_MODEL // 128)], axis=1)
             for r in range(dl)], axis=0).astype(BF16)
        for j in range(3):
            res = _dot(xp, w[2 + 3 * j + gi][...]).astype(BF16)
            for r in range(dl):
                out_ref[r, :, j * COL:(j + 1) * COL] = res[r * per:(r + 1) * per]


def _proj(x, w_in_b, a_ln_g, a_ln_b, a_ws, a_bias):
    bsz, s, _ = x.shape
    n = bsz * s
    tm = TM_PROJ
    tiles = s // tm
    x2 = x.reshape(n, D_MODEL)
    wspec = lambda j: pl.BlockSpec((D_MODEL, COL), lambda i, j=j: (0, j), pipeline_mode=pl.Buffered(1))
    full = lambda shape: pl.BlockSpec(shape, lambda i: (0,) * len(shape))
    rows = lambda width: pl.BlockSpec((tm, width), lambda i: (i, 0))
    dil = lambda dl: pl.BlockSpec((None, dl, tm // dl, 3 * COL), lambda i: (i // tiles, 0, i % tiles, 0))
    return pl.pallas_call(
        _proj_kernel,
        grid=(n // tm,),
        in_specs=[rows(D_MODEL)] + [wspec(j) for j in range(15)]
                 + [full((1, A_WIDTH)), full((1, A_WIDTH)), full((8, A_CHUNK, A_CHUNK)), full((A_CHUNK, A_WIDTH))],
        out_specs=[rows(A_WIDTH), rows(4 * COL), rows(3 * COL), dil(4), dil(16)],
        out_shape=[jax.ShapeDtypeStruct((n, A_WIDTH), BF16),
                   jax.ShapeDtypeStruct((n, 4 * COL), BF16),
                   jax.ShapeDtypeStruct((n, 3 * COL), BF16),
                   jax.ShapeDtypeStruct((bsz, 4, s // 4, 3 * COL), BF16),
                   jax.ShapeDtypeStruct((bsz, 16, s // 16, 3 * COL), BF16)],
        scratch_shapes=[pltpu.VMEM((D_MODEL // 128, tm, 128), F32)],
        compiler_params=pltpu.CompilerParams(dimension_semantics=("parallel",), vmem_limit_bytes=VMEM_LIMIT),
        name="proj",
    )(x2, *([w_in_b] * 15), a_ln_g, a_ln_b, a_ws, a_bias)


def _attn_kernel(qkv_ref, o_ref, lse_ref, *, ns, seq):
    nb = seq // SPAN
    lane = lax.broadcasted_iota(I32, (SPAN, 128), 1)
    lo = lane < 64
    lane16 = lane // 16
    qi = lax.broadcasted_iota(I32, (SPAN, 2 * SPAN), 0)
    ki = lax.broadcasted_iota(I32, (SPAN, 2 * SPAN), 1)
    causal = lax.broadcasted_iota(I32, (SPAN, SPAN), 1) <= lax.broadcasted_iota(I32, (SPAN, SPAN), 0)
    bias_first = jnp.where(causal, 0.0, NEG).astype(F32)
    bias_first = jnp.concatenate([bias_first, bias_first], axis=0)
    bias_main = jnp.where((ki >= qi) & (ki <= qi + SPAN), 0.0, NEG).astype(F32)
    bias_main = jnp.concatenate([bias_main, bias_main], axis=0)
    zero = jnp.zeros((SPAN, 128), BF16)

    for s in range(ns):
        def block(row0, start, bias, s=s):
            win = bias.shape[1]
            lse_tile = jnp.zeros((SPAN, 128), F32)
            for jp in range(B_HEADS // 2):
                c0 = jp * 128
                q = qkv_ref[s, pl.ds(row0, SPAN), c0:c0 + 128] * jnp.asarray(0.125, BF16)
                k = qkv_ref[s, pl.ds(start, win), COL + c0:COL + c0 + 128]
                v = qkv_ref[s, pl.ds(start, win), 2 * COL + c0:2 * COL + c0 + 128]
                qs = jnp.concatenate([jnp.where(lo, q, zero), jnp.where(lo, zero, q)], axis=0)
                sc = lax.dot_general(qs, k, (((1,), (1,)), ((), ())), preferred_element_type=F32) + bias
                m = jnp.max(sc, axis=-1, keepdims=True)
                p = jnp.exp(sc - m)
                l = jnp.sum(p, axis=-1, keepdims=True)
                ov = _dot(p.astype(BF16), v)
                inv = 1.0 / l
                o = jnp.where(lo, ov[:SPAN] * inv[:SPAN], ov[SPAN:] * inv[SPAN:])
                o_ref[pl.ds(row0, SPAN), s * B_WIDTH + c0:s * B_WIDTH + c0 + 128] = o.astype(BF16)
                lse = m + jnp.log(l)
                lse_tile = jnp.where(lane16 == 2 * jp, lse[:SPAN],
                                     jnp.where(lane16 == 2 * jp + 1, lse[SPAN:], lse_tile))
            lse_ref[pl.ds(row0, SPAN), s * 128:(s + 1) * 128] = lse_tile

        block(0, 0, bias_first)
        if nb > 1:
            def body(i, carry):
                block(pl.multiple_of(i * SPAN, SPAN), pl.multiple_of((i - 1) * SPAN, SPAN), bias_main)
                return carry
            lax.fori_loop(1, nb, body, 0)


def _attn(qkv_g, gi):
    bsz, dl, seq, _ = qkv_g.shape
    ns = max(1, min(dl, TM_PROJ // seq))
    return pl.pallas_call(
        functools.partial(_attn_kernel, ns=ns, seq=seq),
        grid=(bsz, dl // ns),
        in_specs=[pl.BlockSpec((None, ns, seq, 3 * COL), lambda b, r: (b, r, 0, 0))],
        out_specs=[pl.BlockSpec((None, seq, ns * B_WIDTH), lambda b, r: (b, 0, r)),
                   pl.BlockSpec((None, seq, ns * 128), lambda b, r: (b, 0, r))],
        out_shape=[jax.ShapeDtypeStruct((bsz, seq, dl * B_WIDTH), BF16),
                   jax.ShapeDtypeStruct((bsz, seq, dl * 128), F32)],
        compiler_params=pltpu.CompilerParams(dimension_semantics=("parallel", "parallel"),
                                             vmem_limit_bytes=VMEM_LIMIT),
        name=f"attn{dl}",
    )(qkv_g)


def _natural_rows(ref, dl, scr):
    nchunk, tm, _ = scr.shape
    w = nchunk * 128
    per = tm // dl
    for r in range(dl):
        for c in range(nchunk):
            scr[c, pl.ds(r, per, stride=dl), :] = ref[:, r * w + c * 128:r * w + (c + 1) * 128].astype(F32)
    return jnp.concatenate([scr[c] for c in range(nchunk)], axis=1)


def _mix_kernel(ga_ref, gates_ref, o1_ref, o2_ref, o3_ref, l1_ref, l2_ref, l3_ref, x_ref, p_ref,
                wa_ref, wb_ref, wo_ref, wple_ref, wpgr_ref, br_ref, g1_ref, b1_ref,
                base_ref, x1p_ref, route_ref, o2s_ref, o3s_ref, l2s_ref, l3s_ref):
    tm = x_ref.shape[0]
    o2 = _natural_rows(o2_ref, 4, o2s_ref)
    o3 = _natural_rows(o3_ref, 16, o3s_ref)
    l2 = _natural_rows(l2_ref, 4, l2s_ref)
    l3 = _natural_rows(l3_ref, 16, l3s_ref)
    l1 = l1_ref[...]
    mx = jnp.maximum(l1, jnp.maximum(l2, l3))
    e1, e2, e3 = jnp.exp(l1 - mx), jnp.exp(l2 - mx), jnp.exp(l3 - mx)
    inv = 1.0 / (e1 + e2 + e3)
    er = lax.broadcasted_iota(I32, (256, B_WIDTH), 0)
    ec = lax.broadcasted_iota(I32, (256, B_WIDTH), 1)
    expand = jnp.where(er % 128 == (ec // B_HEAD_DIM) * 16, 1.0, 0.0).astype(BF16)

    def widen(w):
        hi = w.astype(BF16)
        lo = (w - hi.astype(F32)).astype(BF16)
        return _dot(jnp.concatenate([hi, lo], axis=1), expand)

    ob = widen(e1 * inv) * o1_ref[...].astype(F32) + widen(e2 * inv) * o2 + widen(e3 * inv) * o3
    yb = _dot(ob.astype(BF16), wb_ref[...])
    ya = _dot(ga_ref[...], wa_ref[...])
    pre = gates_ref[:, :D_MODEL].astype(F32) * ya + gates_ref[:, D_MODEL:].astype(F32) * yb
    mix = _dot(pre.astype(BF16), wo_ref[...])
    x1 = _ln(DEEPNORM_ALPHA * x_ref[...] + mix, g1_ref[...], b1_ref[...])
    x1b = x1.astype(BF16)

    z = _dot(x1b, wpgr_ref[...])
    ple = _dot(p_ref[...].astype(BF16), wple_ref[...]) * _sigmoid(z[:, :D_MODEL])
    base_ref[...] = DEEPNORM_ALPHA * x1 + ple
    _store_packed_rows(x1p_ref, x1)

    lg = z[:, D_MODEL:] + br_ref[...]
    lane = lax.broadcasted_iota(I32, (tm, 128), 1).astype(F32)
    big = 1e9
    gl = jnp.where(lane < N_GROUPS, lg, NEG)
    gm = jnp.max(gl, axis=-1, keepdims=True)
    gidx = jnp.min(jnp.where(gl == gm, lane, big), axis=-1, keepdims=True)
    gsum = jnp.sum(jnp.where(lane < N_GROUPS, jnp.exp(gl - gm), 0.0), axis=-1, keepdims=True)
    gprob = 1.0 / gsum
    lo_lane = N_GROUPS + N_EXPERTS * gidx
    el = jnp.where((lane >= lo_lane) & (lane < lo_lane + N_EXPERTS), lg, NEG)
    v1 = jnp.max(el, axis=-1, keepdims=True)
    i1 = jnp.min(jnp.where(el == v1, lane, big), axis=-1, keepdims=True)
    el2 = jnp.where(lane == i1, NEG, el)
    v2 = jnp.max(el2, axis=-1, keepdims=True)
    i2 = jnp.min(jnp.where(el2 == v2, lane, big), axis=-1, keepdims=True)
    t = jnp.exp(v2 - v1)
    w1 = 1.0 / (1.0 + t)
    w2 = t * w1
    route_ref[...] = jnp.where(lane == 0, i1 - N_GROUPS,
                               jnp.where(lane == 1, i2 - N_GROUPS,
                                         jnp.where(lane == 2, gprob * w1,
                                                   jnp.where(lane == 3, gprob * w2, 0.0))))


def _mix(ga, gates, o1, o2, o3, l1, l2, l3, x2, p2, wa, wb, wo, wple, wpgr, br, g1, b1):
    n = x2.shape[0]
    bsz = o2.shape[0]
    tm = TM_MIX
    tiles = n // bsz // tm
    rows = lambda w: pl.BlockSpec((tm, w), lambda i: (i, 0))
    grouped = lambda a, dl: pl.BlockSpec((None, tm // dl, a.shape[2]), lambda i: (i // tiles, i % tiles, 0))
    full = lambda a: pl.BlockSpec(a.shape, lambda i: (0,) * a.ndim)
    return pl.pallas_call(
        _mix_kernel,
        grid=(n // tm,),
        in_specs=[rows(A_WIDTH), rows(2 * D_MODEL), rows(B_WIDTH), grouped(o2, 4), grouped(o3, 16),
                  rows(128), grouped(l2, 4), grouped(l3, 16), rows(D_MODEL), rows(PLE_DIM),
                  full(wa), full(wb), full(wo), full(wple), full(wpgr), full(br), full(g1), full(b1)],
        out_specs=[rows(D_MODEL), pl.BlockSpec((tm // 8, 32, 128), lambda i: (i, 0, 0)), rows(128)],
        out_shape=[jax.ShapeDtypeStruct((n, D_MODEL), F32),
                   jax.ShapeDtypeStruct((n // 8, 32, 128), U32),
                   jax.ShapeDtypeStruct((n, 128), F32)],
        scratch_shapes=[pltpu.VMEM((B_WIDTH // 128, tm, 128), F32), pltpu.VMEM((B_WIDTH // 128, tm, 128), F32),
                        pltpu.VMEM((1, tm, 128), F32), pltpu.VMEM((1, tm, 128), F32)],
        compiler_params=pltpu.CompilerParams(dimension_semantics=("parallel",), vmem_limit_bytes=VMEM_LIMIT),
        name="mix",
    )(ga, gates, o1, o2, o3, l1, l2, l3, x2, p2, wa, wb, wo, wple, wpgr, br, g1, b1)


SC_WINDOW = 128


def _sc_mesh():
    return plsc.VectorSubcoreMesh(core_axis_name="core", subcore_axis_name="subcore")


def _sc_scatter_rows(rows, dst, n_out):
    r = rows.shape[0]
    m = dst.shape[0]
    nblk = r // SC_WINDOW

    @pl.kernel(out_type=jax.ShapeDtypeStruct((n_out, 128), rows.dtype), mesh=_sc_mesh())
    def scatter(rows_hbm, dst_hbm, out_hbm):
        def body(rows_vmem, dst_vmem):
            pltpu.sync_copy(rows_vmem, out_hbm.at[dst_vmem.at[0]])

        pltpu.emit_pipeline(
            body,
            grid=(m // SC_WINDOW,),
            in_specs=[pl.BlockSpec((SC_WINDOW, 128), lambda i: (i % nblk, 0)),
                      pl.BlockSpec((1, SC_WINDOW), lambda i: (0, i))],
            out_specs=[],
            core_axis_name=("core", "subcore"),
            dimension_semantics=(pltpu.PARALLEL,),
        )(rows_hbm, dst_hbm)

    return scatter(rows, dst.reshape(1, m))


def _sc_gather_rows(table, src):
    m = src.shape[0]

    @pl.kernel(out_type=jax.ShapeDtypeStruct((m, 128), table.dtype), mesh=_sc_mesh())
    def gather(table_hbm, src_hbm, out_hbm):
        def body(src_vmem, out_vmem):
            pltpu.sync_copy(table_hbm.at[src_vmem.at[0]], out_vmem)

        pltpu.emit_pipeline(
            body,
            grid=(m // SC_WINDOW,),
            in_specs=[pl.BlockSpec((1, SC_WINDOW), lambda i: (0, i))],
            out_specs=[pl.BlockSpec((SC_WINDOW, 128), lambda i: (i, 0))],
            core_axis_name=("core", "subcore"),
            dimension_semantics=(pltpu.PARALLEL,),
        )(src_hbm, out_hbm)

    return gather(table, src.reshape(1, m))


def _moe_kernel(te_ref, tv_ref, xs_ref, wg_ref, wu_ref, wd_ref, ys_ref, wgb_ref, wub_ref, wdb_ref):
    t = pl.program_id(0)

    @pl.when((t == 0) | (te_ref[t] != te_ref[jnp.maximum(t - 1, 0)]))
    def _():
        wgb_ref[...] = wg_ref[...].astype(BF16)
        wub_ref[...] = wu_ref[...].astype(BF16)
        wdb_ref[...] = wd_ref[...].astype(BF16)

    @pl.when(tv_ref[t] == 1)
    def _():
        xb = _load_packed_rows(xs_ref).astype(BF16)
        g = _dot(xb, wgb_ref[...])
        u = _dot(xb, wub_ref[...])
        h = (g * _sigmoid(g) * u).astype(BF16)
        _store_packed_rows(ys_ref, _dot(h, wdb_ref[...]))

    @pl.when(tv_ref[t] == 0)
    def _():
        ys_ref[...] = jnp.zeros(ys_ref.shape, U32)


def _moe(tile_expert, tile_valid, xs, wg, wu, wd):
    tm = TM_MOE
    nt = tile_expert.shape[0]
    rows = pl.BlockSpec((tm // 8, 32, 128), lambda t, te, tv: (t, 0, 0))
    return pl.pallas_call(
        _moe_kernel,
        grid_spec=pltpu.PrefetchScalarGridSpec(
            num_scalar_prefetch=2,
            grid=(nt,),
            in_specs=[rows,
                      pl.BlockSpec((None, D_MODEL, D_EXPERT), lambda t, te, tv: (te[t], 0, 0)),
                      pl.BlockSpec((None, D_MODEL, D_EXPERT), lambda t, te, tv: (te[t], 0, 0)),
                      pl.BlockSpec((None, D_EXPERT, D_MODEL), lambda t, te, tv: (te[t], 0, 0))],
            out_specs=rows,
            scratch_shapes=[pltpu.VMEM((D_MODEL, D_EXPERT), BF16), pltpu.VMEM((D_MODEL, D_EXPERT), BF16),
                            pltpu.VMEM((D_EXPERT, D_MODEL), BF16)]),
        out_shape=jax.ShapeDtypeStruct((nt * tm // 8, 32, 128), U32),
        compiler_params=pltpu.CompilerParams(dimension_semantics=("arbitrary",), vmem_limit_bytes=VMEM_LIMIT),
        name="moe",
    )(tile_expert, tile_valid, xs, wg, wu, wd)


def _final_kernel(y0_ref, y1_ref, base_ref, route_ref, g2_ref, b2_ref, out_ref):
    ffn = route_ref[:, 2:3] * _load_packed_rows(y0_ref) + route_ref[:, 3:4] * _load_packed_rows(y1_ref)
    out_ref[...] = _ln(base_ref[...] + ffn, g2_ref[...], b2_ref[...])


def _final(yg, base, route, g2, b2):
    n = base.shape[0]
    tm = TM_FIN
    nt = n // tm
    return pl.pallas_call(
        _final_kernel,
        grid=(nt,),
        in_specs=[pl.BlockSpec((tm // 8, 32, 128), lambda t: (t, 0, 0)),
                  pl.BlockSpec((tm // 8, 32, 128), lambda t: (t + nt, 0, 0)),
                  pl.BlockSpec((tm, D_MODEL), lambda t: (t, 0)),
                  pl.BlockSpec((tm, 128), lambda t: (t, 0)),
                  pl.BlockSpec((1, D_MODEL), lambda t: (0, 0)),
                  pl.BlockSpec((1, D_MODEL), lambda t: (0, 0))],
        out_specs=pl.BlockSpec((tm, D_MODEL), lambda t: (t, 0)),
        out_shape=jax.ShapeDtypeStruct((n, D_MODEL), F32),
        compiler_params=pltpu.CompilerParams(dimension_semantics=("parallel",), vmem_limit_bytes=VMEM_LIMIT),
        name="final",
    )(yg, yg, base, route, g2, b2)


def _routing_tables(route, n):
    tm = TM_MOE
    nt = (2 * n) // tm + N_EXPERTS_TOTAL
    ef = route[:, 0:2].astype(I32).T.reshape(-1)
    onehot = (ef[:, None] == jnp.arange(N_EXPERTS_TOTAL, dtype=I32)[None, :]).astype(I32)
    csum = jnp.cumsum(onehot, axis=0)
    counts = csum[-1]
    rank = jnp.sum(csum * onehot, axis=1) - 1
    padded = ((counts + tm - 1) // tm) * tm
    ends = jnp.cumsum(padded)
    offs = ends - padded
    pos = jnp.sum(onehot * offs[None, :], axis=1) + rank
    tile_start = jnp.arange(nt, dtype=I32) * tm
    tile_expert = jnp.minimum(jnp.sum((tile_start[:, None] >= ends[None, :]).astype(I32), axis=1),
                              N_EXPERTS_TOTAL - 1).astype(I32)
    tile_valid = (tile_start < ends[-1]).astype(I32)
    pg = pos.reshape(2 * n // 8, 1, 8)
    piece = (pg // 8) * (8 * SUBROWS) + pg % 8 + 8 * jnp.arange(SUBROWS, dtype=I32)[None, :, None]
    return tile_expert, tile_valid, piece.reshape(-1)


def kernel(x, p, w_in, a_ln_g, a_ln_b, a_ws, a_bs, w_a_proj, w_b_proj, w_o, ln1_g, ln1_b, w_group_router,
           b_group_router, w_expert_router, b_expert_router, w_gate, w_up, w_down, w_ple, w_ple_gate,
           ln2_g, ln2_b):
    bsz, s, d = x.shape
    n = bsz * s
    assert d == D_MODEL and s % (SPAN * max(B_DILATIONS)) == 0 and n % TM_PROJ == 0
    assert w_in.shape[0] == 1, "one layer"

    w_in_b = w_in[0].astype(BF16)
    a_bias = jnp.repeat(a_bs[0].T, A_WIDTH // 8, axis=1)

    ga, gates, qkv1, qkv2, qkv3 = _proj(x, w_in_b, a_ln_g, a_ln_b, a_ws[0], a_bias)
    o1, l1 = _attn(qkv1.reshape(bsz, 1, s, 3 * COL), 0)
    o2, l2 = _attn(qkv2, 1)
    o3, l3 = _attn(qkv3, 2)

    pad = 128 - N_GROUPS - N_EXPERTS_TOTAL
    wpgr = jnp.concatenate([w_ple_gate[0], w_group_router[0], w_expert_router[0].reshape(d, N_EXPERTS_TOTAL),
                            jnp.zeros((d, pad), F32)], axis=1).astype(BF16)
    br = jnp.concatenate([b_group_router[0], b_expert_router[0].reshape(-1), jnp.zeros((pad,), F32)])[None, :]
    base, x1p, route = _mix(
        ga, gates, o1.reshape(n, B_WIDTH), o2, o3, l1.reshape(n, 128), l2, l3,
        x.reshape(n, d), p[0].reshape(n, PLE_DIM),
        w_a_proj[0].astype(BF16), w_b_proj[0].astype(BF16), w_o[0].astype(BF16), w_ple[0].astype(BF16),
        wpgr, br, ln1_g, ln1_b)

    tile_expert, tile_valid, piece = _routing_tables(route, n)
    nt = tile_expert.shape[0]
    xs = _sc_scatter_rows(x1p.reshape(n * SUBROWS, 128), piece, nt * TM_MOE * SUBROWS)
    ys = _moe(tile_expert, tile_valid, xs.reshape(nt * TM_MOE // 8, 32, 128),
              w_gate[0].reshape(N_EXPERTS_TOTAL, d, D_EXPERT), w_up[0].reshape(N_EXPERTS_TOTAL, d, D_EXPERT),
              w_down[0].reshape(N_EXPERTS_TOTAL, D_EXPERT, d))
    yg = _sc_gather_rows(ys.reshape(nt * TM_MOE * SUBROWS, 128), piece)
    out = _final(yg.reshape(2 * n // 8, 32, 128), base, route, ln2_g, ln2_b)
    return out.reshape(bsz, s, d)
```

```python
import functools

import jax
import jax.numpy as jnp
from jax import lax
from jax.experimental import pallas as pl
from jax.experimental.pallas import tpu as pltpu
from jax.experimental.pallas import tpu_sc as plsc

F32 = jnp.float32
BF16 = jnp.bfloat16
U32 = jnp.uint32
I32 = jnp.int32

D_MODEL = 1024
PLE_DIM = 256
A_WIDTH = 512
A_CHUNK = 128
B_HEAD_DIM = 64
B_HEADS = 8
B_WIDTH = 512
B_DILATIONS = (1, 4, 16)
SPAN = 128
N_GROUPS = 4
N_EXPERTS = 8
N_EXPERTS_TOTAL = N_GROUPS * N_EXPERTS
D_EXPERT = 256
DEEPNORM_ALPHA = 2.0 ** 0.25
LN_EPS = 1e-5
COL = 512
NEG = -1e30

VMEM_LIMIT = 56 * 1024 * 1024

TM_PROJ = 512
TM_MIX = 512
TM_MOE = 512
TM_FIN = 512


def _ln(x, g, b):
    mu = jnp.mean(x, axis=-1, keepdims=True)
    xc = x - mu
    var = jnp.mean(xc * xc, axis=-1, keepdims=True)
    return xc * lax.rsqrt(var + LN_EPS) * g + b


def _gelu_tanh(x):
    return 0.5 * x * (1.0 + jnp.tanh(0.7978845608028654 * (x + 0.044715 * (x * x * x))))


def _sigmoid(x):
    return 1.0 / (1.0 + jnp.exp(-x))


def _dot(a, b):
    return jnp.dot(a, b, preferred_element_type=F32)


PACK_W = D_MODEL // 2
SUBROWS = PACK_W // 128


def _store_packed_rows(ref, x):
    m = x.shape[0]
    xb = x.astype(BF16).astype(F32)
    lo = pltpu.bitcast(xb[:, :PACK_W], U32) >> 16
    hi = pltpu.bitcast(xb[:, PACK_W:], U32) & jnp.uint32(0xFFFF0000)
    w = hi | lo
    for j in range(SUBROWS):
        ref[:, 8 * j:8 * (j + 1), :] = w[:, 128 * j:128 * (j + 1)].reshape(m // 8, 8, 128)


def _load_packed_rows(ref):
    m = ref.shape[0] * 8
    ws = [ref[:, 8 * j:8 * (j + 1), :].reshape(m, 128) for j in range(SUBROWS)]
    lo = [pltpu.bitcast(w << 16, F32) for w in ws]
    hi = [pltpu.bitcast(w & jnp.uint32(0xFFFF0000), F32) for w in ws]
    return jnp.concatenate(lo + hi, axis=1)


def _proj_kernel(x_ref, *refs):
    w = refs[:15]
    lng_ref, lnb_ref, ws_ref, bias_ref = refs[15:19]
    ga_ref, gates_ref, qkv1_ref, qkv2_ref, qkv3_ref = refs[19:24]
    xc_ref = refs[24]
    tm = x_ref.shape[0]
    xb = x_ref[...].astype(BF16)

    u = _gelu_tanh(_dot(xb, w[0][...]))
    v = _gelu_tanh(_dot(xb, w[1][...]))
    vn = _ln(v, lng_ref[...], lnb_ref[...]).astype(BF16)

    row = lax.broadcasted_iota(I32, (A_CHUNK, A_CHUNK), 0)
    colm = lax.broadcasted_iota(I32, (A_CHUNK, A_CHUNK), 1)
    causal = colm <= row
    lo = colm < 64
    zero = jnp.zeros((A_CHUNK, A_CHUNK), BF16)
    wcat = []
    for j in range(4):
        w0 = jnp.where(causal, ws_ref[2 * j], 0.0).astype(BF16)
        w1 = jnp.where(causal, ws_ref[2 * j + 1], 0.0).astype(BF16)
        wcat.append(jnp.concatenate([w0, w1], axis=1))
    for c in range(tm // A_CHUNK):
        r0 = c * A_CHUNK
        for j in range(4):
            c0 = j * 128
            vt = vn[r0:r0 + A_CHUNK, c0:c0 + 128]
            rhs = jnp.concatenate([jnp.where(lo, vt, zero), jnp.where(lo, zero, vt)], axis=0)
            mixed = _dot(wcat[j], rhs) + bias_ref[:, c0:c0 + 128]
            ga_ref[r0:r0 + A_CHUNK, c0:c0 + 128] = (u[r0:r0 + A_CHUNK, c0:c0 + 128] * mixed).astype(BF16)

    for i in range(4):
        gates_ref[:, i * COL:(i + 1) * COL] = _sigmoid(_dot(xb, w[11 + i][...])).astype(BF16)
    for j in range(3):
        qkv1_ref[:, j * COL:(j + 1) * COL] = _dot(xb, w[2 + 3 * j][...]).astype(BF16)

    for c in range(D_MODEL // 128):
        xc_ref[c] = x_ref[:, c * 128:(c + 1) * 128]
    for gi, out_ref in ((1, qkv2_ref), (2, qkv3_ref)):
        dl = B_DILATIONS[gi]
        per = tm // dl
        xp = jnp.concatenate(
            [jnp.concatenate([xc_ref[c, pl.ds(r, per, stride=dl), :] for c in range(D_MODEL // 128)], axis=1)
             for r in range(dl)], axis=0).astype(BF16)
        for j in range(3):
            res = _dot(xp, w[2 + 3 * j + gi][...]).astype(BF16)
            for r in range(dl):
                out_ref[r, :, j * COL:(j + 1) * COL] = res[r * per:(r + 1) * per]


def _proj(x, w_in_b, a_ln_g, a_ln_b, a_ws, a_bias):
    bsz, s, _ = x.shape
    n = bsz * s
    tm = TM_PROJ
    tiles = s // tm
    x2 = x.reshape(n, D_MODEL)
    wspec = lambda j: pl.BlockSpec((D_MODEL, COL), lambda i, j=j: (0, j), pipeline_mode=pl.Buffered(1))
    full = lambda shape: pl.BlockSpec(shape, lambda i: (0,) * len(shape))
    rows = lambda width: pl.BlockSpec((tm, width), lambda i: (i, 0))
    dil = lambda dl: pl.BlockSpec((None, dl, tm // dl, 3 * COL), lambda i: (i // tiles, 0, i % tiles, 0))
    return pl.pallas_call(
        _proj_kernel,
        grid=(n // tm,),
        in_specs=[rows(D_MODEL)] + [wspec(j) for j in range(15)]
                 + [full((1, A_WIDTH)), full((1, A_WIDTH)), full((8, A_CHUNK, A_CHUNK)), full((A_CHUNK, A_WIDTH))],
        out_specs=[rows(A_WIDTH), rows(4 * COL), rows(3 * COL), dil(4), dil(16)],
        out_shape=[jax.ShapeDtypeStruct((n, A_WIDTH), BF16),
                   jax.ShapeDtypeStruct((n, 4 * COL), BF16),
                   jax.ShapeDtypeStruct((n, 3 * COL), BF16),
                   jax.ShapeDtypeStruct((bsz, 4, s // 4, 3 * COL), BF16),
                   jax.ShapeDtypeStruct((bsz, 16, s // 16, 3 * COL), BF16)],
        scratch_shapes=[pltpu.VMEM((D_MODEL // 128, tm, 128), F32)],
        compiler_params=pltpu.CompilerParams(dimension_semantics=("parallel",), vmem_limit_bytes=VMEM_LIMIT),
        name="proj",
    )(x2, *([w_in_b] * 15), a_ln_g, a_ln_b, a_ws, a_bias)


def _attn_kernel(qkv_ref, o_ref, lse_ref, *, ns, seq):
    nb = seq // SPAN
    lane = lax.broadcasted_iota(I32, (SPAN, 128), 1)
    lo = lane < 64
    lane16 = lane // 16
    qi = lax.broadcasted_iota(I32, (SPAN, 2 * SPAN), 0)
    ki = lax.broadcasted_iota(I32, (SPAN, 2 * SPAN), 1)
    causal = lax.broadcasted_iota(I32, (SPAN, SPAN), 1) <= lax.broadcasted_iota(I32, (SPAN, SPAN), 0)
    bias_first = jnp.where(causal, 0.0, NEG).astype(F32)
    bias_first = jnp.concatenate([bias_first, bias_first], axis=0)
    bias_main = jnp.where((ki >= qi) & (ki <= qi + SPAN), 0.0, NEG).astype(F32)
    bias_main = jnp.concatenate([bias_main, bias_main], axis=0)
    zero = jnp.zeros((SPAN, 128), BF16)

    for s in range(ns):
        def block(row0, start, bias, s=s):
            win = bias.shape[1]
            pairs = range(B_HEADS // 2)
            scores, values = [], []
            for jp in pairs:
                c0 = jp * 128
                q = qkv_ref[s, pl.ds(row0, SPAN), c0:c0 + 128] * jnp.asarray(0.125, BF16)
                k = qkv_ref[s, pl.ds(start, win), COL + c0:COL + c0 + 128]
                values.append(qkv_ref[s, pl.ds(start, win), 2 * COL + c0:2 * COL + c0 + 128])
                qs = jnp.concatenate([jnp.where(lo, q, zero), jnp.where(lo, zero, q)], axis=0)
                scores.append(lax.dot_general(qs, k, (((1,), (1,)), ((), ())), preferred_element_type=F32) + bias)
            probs, maxes, sums = [], [], []
            for jp in pairs:
                m = jnp.max(scores[jp], axis=-1, keepdims=True)
                p = jnp.exp(scores[jp] - m)
                maxes.append(m)
                sums.append(jnp.sum(p, axis=-1, keepdims=True))
                probs.append(p.astype(BF16))
            lse_tile = jnp.zeros((SPAN, 128), F32)
            for jp in pairs:
                c0 = jp * 128
                ov = _dot(probs[jp], values[jp])
                inv = 1.0 / sums[jp]
                o = jnp.where(lo, ov[:SPAN] * inv[:SPAN], ov[SPAN:] * inv[SPAN:])
                o_ref[pl.ds(row0, SPAN), s * B_WIDTH + c0:s * B_WIDTH + c0 + 128] = o.astype(BF16)
                lse = maxes[jp] + jnp.log(sums[jp])
                lse_tile = jnp.where(lane16 == 2 * jp, lse[:SPAN],
                                     jnp.where(lane16 == 2 * jp + 1, lse[SPAN:], lse_tile))
            lse_ref[pl.ds(row0, SPAN), s * 128:(s + 1) * 128] = lse_tile

        block(0, 0, bias_first)
        if nb > 1:
            def body(i, carry):
                block(pl.multiple_of(i * SPAN, SPAN), pl.multiple_of((i - 1) * SPAN, SPAN), bias_main)
                return carry
            lax.fori_loop(1, nb, body, 0)


def _attn(qkv_g, gi):
    bsz, dl, seq, _ = qkv_g.shape
    ns = max(1, min(dl, TM_PROJ // seq))
    return pl.pallas_call(
        functools.partial(_attn_kernel, ns=ns, seq=seq),
        grid=(bsz, dl // ns),
        in_specs=[pl.BlockSpec((None, ns, seq, 3 * COL), lambda b, r: (b, r, 0, 0))],
        out_specs=[pl.BlockSpec((None, seq, ns * B_WIDTH), lambda b, r: (b, 0, r)),
                   pl.BlockSpec((None, seq, ns * 128), lambda b, r: (b, 0, r))],
        out_shape=[jax.ShapeDtypeStruct((bsz, seq, dl * B_WIDTH), BF16),
                   jax.ShapeDtypeStruct((bsz, seq, dl * 128), F32)],
        compiler_params=pltpu.CompilerParams(dimension_semantics=("parallel", "parallel"),
                                             vmem_limit_bytes=VMEM_LIMIT),
        name=f"attn{dl}",
    )(qkv_g)


def _natural_rows(ref, dl, scr):
    nchunk, tm, _ = scr.shape
    w = nchunk * 128
    per = tm // dl
    for r in range(dl):
        for c in range(nchunk):
            scr[c, pl.ds(r, per, stride=dl), :] = ref[:, r * w + c * 128:r * w + (c + 1) * 128].astype(F32)
    return jnp.concatenate([scr[c] for c in range(nchunk)], axis=1)


def _mix_kernel(ga_ref, gates_ref, o1_ref, o2_ref, o3_ref, l1_ref, l2_ref, l3_ref, x_ref, p_ref,
                wa_ref, wb_ref, wo_ref, wple_ref, wpgr_ref, br_ref, g1_ref, b1_ref,
                base_ref, x1p_ref, route_ref, o2s_ref, o3s_ref, l2s_ref, l3s_ref):
    tm = x_ref.shape[0]
    o2 = _natural_rows(o2_ref, 4, o2s_ref)
    o3 = _natural_rows(o3_ref, 16, o3s_ref)
    l2 = _natural_rows(l2_ref, 4, l2s_ref)
    l3 = _natural_rows(l3_ref, 16, l3s_ref)
    l1 = l1_ref[...]
    mx = jnp.maximum(l1, jnp.maximum(l2, l3))
    e1, e2, e3 = jnp.exp(l1 - mx), jnp.exp(l2 - mx), jnp.exp(l3 - mx)
    inv = 1.0 / (e1 + e2 + e3)
    er = lax.broadcasted_iota(I32, (256, B_WIDTH), 0)
    ec = lax.broadcasted_iota(I32, (256, B_WIDTH), 1)
    expand = jnp.where(er % 128 == (ec // B_HEAD_DIM) * 16, 1.0, 0.0).astype(BF16)

    def widen(w):
        hi = w.astype(BF16)
        lo = (w - hi.astype(F32)).astype(BF16)
        return _dot(jnp.concatenate([hi, lo], axis=1), expand)

    ob = widen(e1 * inv) * o1_ref[...].astype(F32) + widen(e2 * inv) * o2 + widen(e3 * inv) * o3
    yb = _dot(ob.astype(BF16), wb_ref[...])
    ya = _dot(ga_ref[...], wa_ref[...])
    pre = gates_ref[:, :D_MODEL].astype(F32) * ya + gates_ref[:, D_MODEL:].astype(F32) * yb
    mix = _dot(pre.astype(BF16), wo_ref[...])
    x1 = _ln(DEEPNORM_ALPHA * x_ref[...] + mix, g1_ref[...], b1_ref[...])
    x1b = x1.astype(BF16)

    z = _dot(x1b, wpgr_ref[...])
    ple = _dot(p_ref[...].astype(BF16), wple_ref[...]) * _sigmoid(z[:, :D_MODEL])
    base_ref[...] = DEEPNORM_ALPHA * x1 + ple
    _store_packed_rows(x1p_ref, x1)

    lg = z[:, D_MODEL:] + br_ref[...]
    lane = lax.broadcasted_iota(I32, (tm, 128), 1).astype(F32)
    big = 1e9
    gl = jnp.where(lane < N_GROUPS, lg, NEG)
    gm = jnp.max(gl, axis=-1, keepdims=True)
    gidx = jnp.min(jnp.where(gl == gm, lane, big), axis=-1, keepdims=True)
    gsum = jnp.sum(jnp.where(lane < N_GROUPS, jnp.exp(gl - gm), 0.0), axis=-1, keepdims=True)
    gprob = 1.0 / gsum
    lo_lane = N_GROUPS + N_EXPERTS * gidx
    el = jnp.where((lane >= lo_lane) & (lane < lo_lane + N_EXPERTS), lg, NEG)
    v1 = jnp.max(el, axis=-1, keepdims=True)
    i1 = jnp.min(jnp.where(el == v1, lane, big), axis=-1, keepdims=True)
    el2 = jnp.where(lane == i1, NEG, el)
    v2 = jnp.max(el2, axis=-1, keepdims=True)
    i2 = jnp.min(jnp.where(el2 == v2, lane, big), axis=-1, keepdims=True)
    t = jnp.exp(v2 - v1)
    w1 = 1.0 / (1.0 + t)
    w2 = t * w1
    route_ref[...] = jnp.where(lane == 0, i1 - N_GROUPS,
                               jnp.where(lane == 1, i2 - N_GROUPS,
                                         jnp.where(lane == 2, gprob * w1,
                                                   jnp.where(lane == 3, gprob * w2, 0.0))))


def _mix(ga, gates, o1, o2, o3, l1, l2, l3, x2, p2, wa, wb, wo, wple, wpgr, br, g1, b1):
    n = x2.shape[0]
    bsz = o2.shape[0]
    tm = TM_MIX
    tiles = n // bsz // tm
    rows = lambda w: pl.BlockSpec((tm, w), lambda i: (i, 0))
    grouped = lambda a, dl: pl.BlockSpec((None, tm // dl, a.shape[2]), lambda i: (i // tiles, i % tiles, 0))
    full = lambda a: pl.BlockSpec(a.shape, lambda i: (0,) * a.ndim)
    return pl.pallas_call(
        _mix_kernel,
        grid=(n // tm,),
        in_specs=[rows(A_WIDTH), rows(2 * D_MODEL), rows(B_WIDTH), grouped(o2, 4), grouped(o3, 16),
                  rows(128), grouped(l2, 4), grouped(l3, 16), rows(D_MODEL), rows(PLE_DIM),
                  full(wa), full(wb), full(wo), full(wple), full(wpgr), full(br), full(g1), full(b1)],
        out_specs=[rows(D_MODEL), pl.BlockSpec((tm // 8, 32, 128), lambda i: (i, 0, 0)), rows(128)],
        out_shape=[jax.ShapeDtypeStruct((n, D_MODEL), F32),
                   jax.ShapeDtypeStruct((n // 8, 32, 128), U32),
                   jax.ShapeDtypeStruct((n, 128), F32)],
        scratch_shapes=[pltpu.VMEM((B_WIDTH // 128, tm, 128), F32), pltpu.VMEM((B_WIDTH // 128, tm, 128), F32),
                        pltpu.VMEM((1, tm, 128), F32), pltpu.VMEM((1, tm, 128), F32)],
        compiler_params=pltpu.CompilerParams(dimension_semantics=("parallel",), vmem_limit_bytes=VMEM_LIMIT),
        name="mix",
    )(ga, gates, o1, o2, o3, l1, l2, l3, x2, p2, wa, wb, wo, wple, wpgr, br, g1, b1)


SC_WINDOW = 128


def _sc_mesh():
    return plsc.VectorSubcoreMesh(core_axis_name="core", subcore_axis_name="subcore")


def _sc_scatter_rows(rows, dst, n_out):
    r = rows.shape[0]
    m = dst.shape[0]
    nblk = r // SC_WINDOW

    @pl.kernel(out_type=jax.ShapeDtypeStruct((n_out, 128), rows.dtype), mesh=_sc_mesh())
    def scatter(rows_hbm, dst_hbm, out_hbm):
        def body(rows_vmem, dst_vmem):
            pltpu.sync_copy(rows_vmem, out_hbm.at[dst_vmem.at[0]])

        pltpu.emit_pipeline(
            body,
            grid=(m // SC_WINDOW,),
            in_specs=[pl.BlockSpec((SC_WINDOW, 128), lambda i: (i % nblk, 0)),
                      pl.BlockSpec((1, SC_WINDOW), lambda i: (0, i))],
            out_specs=[],
            core_axis_name=("core", "subcore"),
            dimension_semantics=(pltpu.PARALLEL,),
        )(rows_hbm, dst_hbm)

    return scatter(rows, dst.reshape(1, m))


def _sc_gather_rows(table, src):
    m = src.shape[0]

    @pl.kernel(out_type=jax.ShapeDtypeStruct((m, 128), table.dtype), mesh=_sc_mesh())
    def gather(table_hbm, src_hbm, out_hbm):
        def body(src_vmem, out_vmem):
            pltpu.sync_copy(table_hbm.at[src_vmem.at[0]], out_vmem)

        pltpu.emit_pipeline(
            body,
            grid=(m // SC_WINDOW,),
            in_specs=[pl.BlockSpec((1, SC_WINDOW), lambda i: (0, i))],
            out_specs=[pl.BlockSpec((SC_WINDOW, 128), lambda i: (i, 0))],
            core_axis_name=("core", "subcore"),
            dimension_semantics=(pltpu.PARALLEL,),
        )(src_hbm, out_hbm)

    return gather(table, src.reshape(1, m))


def _moe_kernel(te_ref, tv_ref, xs_ref, wg_ref, wu_ref, wd_ref, ys_ref, wgb_ref, wub_ref, wdb_ref):
    t = pl.program_id(0)

    @pl.when((t == 0) | (te_ref[t] != te_ref[jnp.maximum(t - 1, 0)]))
    def _():
        wgb_ref[...] = wg_ref[...].astype(BF16)
        wub_ref[...] = wu_ref[...].astype(BF16)
        wdb_ref[...] = wd_ref[...].astype(BF16)

    @pl.when(tv_ref[t] == 1)
    def _():
        xb = _load_packed_rows(xs_ref).astype(BF16)
        g = _dot(xb, wgb_ref[...])
        u = _dot(xb, wub_ref[...])
        h = (g * _sigmoid(g) * u).astype(BF16)
        _store_packed_rows(ys_ref, _dot(h, wdb_ref[...]))

    @pl.when(tv_ref[t] == 0)
    def _():
        ys_ref[...] = jnp.zeros(ys_ref.shape, U32)


def _moe(tile_expert, tile_valid, xs, wg, wu, wd):
    tm = TM_MOE
    nt = tile_expert.shape[0]
    rows = pl.BlockSpec((tm // 8, 32, 128), lambda t, te, tv: (t, 0, 0))
    return pl.pallas_call(
        _moe_kernel,
        grid_spec=pltpu.PrefetchScalarGridSpec(
            num_scalar_prefetch=2,
            grid=(nt,),
            in_specs=[rows,
                      pl.BlockSpec((None, D_MODEL, D_EXPERT), lambda t, te, tv: (te[t], 0, 0)),
                      pl.BlockSpec((None, D_MODEL, D_EXPERT), lambda t, te, tv: (te[t], 0, 0)),
                      pl.BlockSpec((None, D_EXPERT, D_MODEL), lambda t, te, tv: (te[t], 0, 0))],
            out_specs=rows,
            scratch_shapes=[pltpu.VMEM((D_MODEL, D_EXPERT), BF16), pltpu.VMEM((D_MODEL, D_EXPERT), BF16),
                            pltpu.VMEM((D_EXPERT, D_MODEL), BF16)]),
        out_shape=jax.ShapeDtypeStruct((nt * tm // 8, 32, 128), U32),
        compiler_params=pltpu.CompilerParams(dimension_semantics=("arbitrary",), vmem_limit_bytes=VMEM_LIMIT),
        name="moe",
    )(tile_expert, tile_valid, xs, wg, wu, wd)


def _final_kernel(y0_ref, y1_ref, base_ref, route_ref, g2_ref, b2_ref, out_ref):
    ffn = route_ref[:, 2:3] * _load_packed_rows(y0_ref) + route_ref[:, 3:4] * _load_packed_rows(y1_ref)
    out_ref[...] = _ln(base_ref[...] + ffn, g2_ref[...], b2_ref[...])


def _final(yg, base, route, g2, b2):
    n = base.shape[0]
    tm = TM_FIN
    nt = n // tm
    return pl.pallas_call(
        _final_kernel,
        grid=(nt,),
        in_specs=[pl.BlockSpec((tm // 8, 32, 128), lambda t: (t, 0, 0)),
                  pl.BlockSpec((tm // 8, 32, 128), lambda t: (t + nt, 0, 0)),
                  pl.BlockSpec((tm, D_MODEL), lambda t: (t, 0)),
                  pl.BlockSpec((tm, 128), lambda t: (t, 0)),
                  pl.BlockSpec((1, D_MODEL), lambda t: (0, 0)),
                  pl.BlockSpec((1, D_MODEL), lambda t: (0, 0))],
        out_specs=pl.BlockSpec((tm, D_MODEL), lambda t: (t, 0)),
        out_shape=jax.ShapeDtypeStruct((n, D_MODEL), F32),
        compiler_params=pltpu.CompilerParams(dimension_semantics=("parallel",), vmem_limit_bytes=VMEM_LIMIT),
        name="final",
    )(yg, yg, base, route, g2, b2)


def _routing_tables(route, n):
    tm = TM_MOE
    nt = (2 * n) // tm + N_EXPERTS_TOTAL
    ef = route[:, 0:2].astype(I32).T.reshape(-1)
    onehot = (ef[:, None] == jnp.arange(N_EXPERTS_TOTAL, dtype=I32)[None, :]).astype(I32)
    csum = jnp.cumsum(onehot, axis=0)
    counts = csum[-1]
    rank = jnp.sum(csum * onehot, axis=1) - 1
    padded = ((counts + tm - 1) // tm) * tm
    ends = jnp.cumsum(padded)
    offs = ends - padded
    pos = jnp.sum(onehot * offs[None, :], axis=1) + rank
    tile_start = jnp.arange(nt, dtype=I32) * tm
    tile_expert = jnp.minimum(jnp.sum((tile_start[:, None] >= ends[None, :]).astype(I32), axis=1),
                              N_EXPERTS_TOTAL - 1).astype(I32)
    tile_valid = (tile_start < ends[-1]).astype(I32)
    pg = pos.reshape(2 * n // 8, 1, 8)
    piece = (pg // 8) * (8 * SUBROWS) + pg % 8 + 8 * jnp.arange(SUBROWS, dtype=I32)[None, :, None]
    return tile_expert, tile_valid, piece.reshape(-1)


def kernel(x, p, w_in, a_ln_g, a_ln_b, a_ws, a_bs, w_a_proj, w_b_proj, w_o, ln1_g, ln1_b, w_group_router,
           b_group_router, w_expert_router, b_expert_router, w_gate, w_up, w_down, w_ple, w_ple_gate,
           ln2_g, ln2_b):
    bsz, s, d = x.shape
    n = bsz * s
    assert d == D_MODEL and s % (SPAN * max(B_DILATIONS)) == 0 and n % TM_PROJ == 0
    assert w_in.shape[0] == 1, "one layer"

    w_in_b = w_in[0].astype(BF16)
    a_bias = jnp.repeat(a_bs[0].T, A_WIDTH // 8, axis=1)

    ga, gates, qkv1, qkv2, qkv3 = _proj(x, w_in_b, a_ln_g, a_ln_b, a_ws[0], a_bias)
    o1, l1 = _attn(qkv1.reshape(bsz, 1, s, 3 * COL), 0)
    o2, l2 = _attn(qkv2, 1)
    o3, l3 = _attn(qkv3, 2)

    pad = 128 - N_GROUPS - N_EXPERTS_TOTAL
    wpgr = jnp.concatenate([w_ple_gate[0], w_group_router[0], w_expert_router[0].reshape(d, N_EXPERTS_TOTAL),
                            jnp.zeros((d, pad), F32)], axis=1).astype(BF16)
    br = jnp.concatenate([b_group_router[0], b_expert_router[0].reshape(-1), jnp.zeros((pad,), F32)])[None, :]
    base, x1p, route = _mix(
        ga, gates, o1.reshape(n, B_WIDTH), o2, o3, l1.reshape(n, 128), l2, l3,
        x.reshape(n, d), p[0].reshape(n, PLE_DIM),
        w_a_proj[0].astype(BF16), w_b_proj[0].astype(BF16), w_o[0].astype(BF16), w_ple[0].astype(BF16),
        wpgr, br, ln1_g, ln1_b)

    tile_expert, tile_valid, piece = _routing_tables(route, n)
    nt = tile_expert.shape[0]
    xs = _sc_scatter_rows(x1p.reshape(n * SUBROWS, 128), piece, nt * TM_MOE * SUBROWS)
    ys = _moe(tile_expert, tile_valid, xs.reshape(nt * TM_MOE // 8, 32, 128),
              w_gate[0].reshape(N_EXPERTS_TOTAL, d, D_EXPERT), w_up[0].reshape(N_EXPERTS_TOTAL, d, D_EXPERT),
              w_down[0].reshape(N_EXPERTS_TOTAL, D_EXPERT, d))
    yg = _sc_gather_rows(ys.reshape(nt * TM_MOE * SUBROWS, 128), piece)
    out = _final(yg.reshape(2 * n // 8, 32, 128), base, route, ln2_g, ln2_b)
    return out.reshape(bsz, s, d)
```

```python
import functools

import jax
import jax.numpy as jnp
from jax import lax
from jax.experimental import pallas as pl
from jax.experimental.pallas import tpu as pltpu
from jax.experimental.pallas import tpu_sc as plsc

F32 = jnp.float32
BF16 = jnp.bfloat16
U32 = jnp.uint32
I32 = jnp.int32

D_MODEL = 1024
PLE_DIM = 256
A_WIDTH = 512
A_CHUNK = 128
B_HEAD_DIM = 64
B_HEADS = 8
B_WIDTH = 512
B_DILATIONS = (1, 4, 16)
SPAN = 128
N_GROUPS = 4
N_EXPERTS = 8
N_EXPERTS_TOTAL = N_GROUPS * N_EXPERTS
D_EXPERT = 256
DEEPNORM_ALPHA = 2.0 ** 0.25
LN_EPS = 1e-5
COL = 512
NEG = -1e30

VMEM_LIMIT = 56 * 1024 * 1024

TM_PROJ = 512
TM_MIX = 512
TM_MOE = 512
TM_FIN = 512


def _ln(x, g, b):
    mu = jnp.mean(x, axis=-1, keepdims=True)
    xc = x - mu
    var = jnp.mean(xc * xc, axis=-1, keepdims=True)
    return xc * lax.rsqrt(var + LN_EPS) * g + b


def _gelu_tanh(x):
    return 0.5 * x * (1.0 + jnp.tanh(0.7978845608028654 * (x + 0.044715 * (x * x * x))))


def _sigmoid(x):
    return 0.5 * jnp.tanh(0.5 * x) + 0.5


def _dot(a, b):
    return jnp.dot(a, b, preferred_element_type=F32)


PACK_W = D_MODEL // 2
SUBROWS = PACK_W // 128


def _store_packed_rows(ref, x):
    m = x.shape[0]
    xb = x.astype(BF16).astype(F32)
    lo = pltpu.bitcast(xb[:, :PACK_W], U32) >> 16
    hi = pltpu.bitcast(xb[:, PACK_W:], U32) & jnp.uint32(0xFFFF0000)
    w = hi | lo
    for j in range(SUBROWS):
        ref[:, 8 * j:8 * (j + 1), :] = w[:, 128 * j:128 * (j + 1)].reshape(m // 8, 8, 128)


def _load_packed_rows(ref):
    m = ref.shape[0] * 8
    ws = [ref[:, 8 * j:8 * (j + 1), :].reshape(m, 128) for j in range(SUBROWS)]
    lo = [pltpu.bitcast(w << 16, F32) for w in ws]
    hi = [pltpu.bitcast(w & jnp.uint32(0xFFFF0000), F32) for w in ws]
    return jnp.concatenate(lo + hi, axis=1)


def _proj_kernel(x_ref, *refs):
    w = refs[:15]
    lng_ref, lnb_ref, ws_ref, bias_ref = refs[15:19]
    ga_ref, gates_ref, qkv1_ref, qkv2_ref, qkv3_ref = refs[19:24]
    xc_ref = refs[24]
    tm = x_ref.shape[0]
    xb = x_ref[...].astype(BF16)

    u = _gelu_tanh(_dot(xb, w[0][...]))
    v = _gelu_tanh(_dot(xb, w[1][...]))
    vn = _ln(v, lng_ref[...], lnb_ref[...]).astype(BF16)

    row = lax.broadcasted_iota(I32, (A_CHUNK, A_CHUNK), 0)
    colm = lax.broadcasted_iota(I32, (A_CHUNK, A_CHUNK), 1)
    causal = colm <= row
    lo = colm < 64
    zero = jnp.zeros((A_CHUNK, A_CHUNK), BF16)
    wcat = []
    for j in range(4):
        w0 = jnp.where(causal, ws_ref[2 * j], 0.0).astype(BF16)
        w1 = jnp.where(causal, ws_ref[2 * j + 1], 0.0).astype(BF16)
        wcat.append(jnp.concatenate([w0, w1], axis=1))
    for c in range(tm // A_CHUNK):
        r0 = c * A_CHUNK
        for j in range(4):
            c0 = j * 128
            vt = vn[r0:r0 + A_CHUNK, c0:c0 + 128]
            rhs = jnp.concatenate([jnp.where(lo, vt, zero), jnp.where(lo, zero, vt)], axis=0)
            mixed = _dot(wcat[j], rhs) + bias_ref[:, c0:c0 + 128]
            ga_ref[r0:r0 + A_CHUNK, c0:c0 + 128] = (u[r0:r0 + A_CHUNK, c0:c0 + 128] * mixed).astype(BF16)

    for i in range(4):
        gates_ref[:, i * COL:(i + 1) * COL] = _sigmoid(_dot(xb, w[11 + i][...])).astype(BF16)
    for j in range(3):
        qkv1_ref[:, j * COL:(j + 1) * COL] = _dot(xb, w[2 + 3 * j][...]).astype(BF16)

    for c in range(D_MODEL // 128):
        xc_ref[c] = x_ref[:, c * 128:(c + 1) * 128]
    for gi, out_ref in ((1, qkv2_ref), (2, qkv3_ref)):
        dl = B_DILATIONS[gi]
        per = tm // dl
        xp = jnp.concatenate(
            [jnp.concatenate([xc_ref[c, pl.ds(r, per, stride=dl), :] for c in range(D_MODEL // 128)], axis=1)
             for r in range(dl)], axis=0).astype(BF16)
        for j in range(3):
            res = _dot(xp, w[2 + 3 * j + gi][...]).astype(BF16)
            for r in range(dl):
                out_ref[r, :, j * COL:(j + 1) * COL] = res[r * per:(r + 1) * per]


def _proj(x, w_in_b, a_ln_g, a_ln_b, a_ws, a_bias):
    bsz, s, _ = x.shape
    n = bsz * s
    tm = TM_PROJ
    tiles = s // tm
    x2 = x.reshape(n, D_MODEL)
    wspec = lambda j: pl.BlockSpec((D_MODEL, COL), lambda i, j=j: (0, j), pipeline_mode=pl.Buffered(1))
    full = lambda shape: pl.BlockSpec(shape, lambda i: (0,) * len(shape))
    rows = lambda width: pl.BlockSpec((tm, width), lambda i: (i, 0))
    dil = lambda dl: pl.BlockSpec((None, dl, tm // dl, 3 * COL), lambda i: (i // tiles, 0, i % tiles, 0))
    return pl.pallas_call(
        _proj_kernel,
        grid=(n // tm,),
        in_specs=[rows(D_MODEL)] + [wspec(j) for j in range(15)]
                 + [full((1, A_WIDTH)), full((1, A_WIDTH)), full((8, A_CHUNK, A_CHUNK)), full((A_CHUNK, A_WIDTH))],
        out_specs=[rows(A_WIDTH), rows(4 * COL), rows(3 * COL), dil(4), dil(16)],
        out_shape=[jax.ShapeDtypeStruct((n, A_WIDTH), BF16),
                   jax.ShapeDtypeStruct((n, 4 * COL), BF16),
                   jax.ShapeDtypeStruct((n, 3 * COL), BF16),
                   jax.ShapeDtypeStruct((bsz, 4, s // 4, 3 * COL), BF16),
                   jax.ShapeDtypeStruct((bsz, 16, s // 16, 3 * COL), BF16)],
        scratch_shapes=[pltpu.VMEM((D_MODEL // 128, tm, 128), F32)],
        compiler_params=pltpu.CompilerParams(dimension_semantics=("parallel",), vmem_limit_bytes=VMEM_LIMIT),
        name="proj",
    )(x2, *([w_in_b] * 15), a_ln_g, a_ln_b, a_ws, a_bias)


def _attn_kernel(qkv_ref, o_ref, lse_ref, *, ns, seq):
    nb = seq // SPAN
    lane = lax.broadcasted_iota(I32, (SPAN, 128), 1)
    lo = lane < 64
    lane16 = lane // 16
    qi = lax.broadcasted_iota(I32, (SPAN, 2 * SPAN), 0)
    ki = lax.broadcasted_iota(I32, (SPAN, 2 * SPAN), 1)
    causal = lax.broadcasted_iota(I32, (SPAN, SPAN), 1) <= lax.broadcasted_iota(I32, (SPAN, SPAN), 0)
    bias_first = jnp.where(causal, 0.0, NEG).astype(F32)
    bias_first = jnp.concatenate([bias_first, bias_first], axis=0)
    bias_main = jnp.where((ki >= qi) & (ki <= qi + SPAN), 0.0, NEG).astype(F32)
    bias_main = jnp.concatenate([bias_main, bias_main], axis=0)
    zero = jnp.zeros((SPAN, 128), BF16)

    for s in range(ns):
        def block(row0, start, bias, s=s):
            win = bias.shape[1]
            pairs = range(B_HEADS // 2)
            scores, values = [], []
            for jp in pairs:
                c0 = jp * 128
                q = qkv_ref[s, pl.ds(row0, SPAN), c0:c0 + 128] * jnp.asarray(0.125, BF16)
                k = qkv_ref[s, pl.ds(start, win), COL + c0:COL + c0 + 128]
                values.append(qkv_ref[s, pl.ds(start, win), 2 * COL + c0:2 * COL + c0 + 128])
                qs = jnp.concatenate([jnp.where(lo, q, zero), jnp.where(lo, zero, q)], axis=0)
                scores.append(lax.dot_general(qs, k, (((1,), (1,)), ((), ())), preferred_element_type=F32) + bias)
            probs, maxes, sums = [], [], []
            for jp in pairs:
                m = jnp.max(scores[jp], axis=-1, keepdims=True)
                p = jnp.exp(scores[jp] - m)
                maxes.append(m)
                sums.append(jnp.sum(p, axis=-1, keepdims=True))
                probs.append(p.astype(BF16))
            lse_tile = jnp.zeros((SPAN, 128), F32)
            for jp in pairs:
                c0 = jp * 128
                ov = _dot(probs[jp], values[jp])
                inv = 1.0 / sums[jp]
                o = jnp.where(lo, ov[:SPAN] * inv[:SPAN], ov[SPAN:] * inv[SPAN:])
                o_ref[pl.ds(row0, SPAN), s * B_WIDTH + c0:s * B_WIDTH + c0 + 128] = o.astype(BF16)
                lse = maxes[jp] + jnp.log(sums[jp])
                lse_tile = jnp.where(lane16 == 2 * jp, lse[:SPAN],
                                     jnp.where(lane16 == 2 * jp + 1, lse[SPAN:], lse_tile))
            lse_ref[pl.ds(row0, SPAN), s * 128:(s + 1) * 128] = lse_tile

        block(0, 0, bias_first)
        if nb > 1:
            def body(i, carry):
                block(pl.multiple_of(i * SPAN, SPAN), pl.multiple_of((i - 1) * SPAN, SPAN), bias_main)
                return carry
            lax.fori_loop(1, nb, body, 0)


def _attn(qkv_g, gi):
    bsz, dl, seq, _ = qkv_g.shape
    ns = max(1, min(dl, TM_PROJ // seq))
    return pl.pallas_call(
        functools.partial(_attn_kernel, ns=ns, seq=seq),
        grid=(bsz, dl // ns),
        in_specs=[pl.BlockSpec((None, ns, seq, 3 * COL), lambda b, r: (b, r, 0, 0))],
        out_specs=[pl.BlockSpec((None, seq, ns * B_WIDTH), lambda b, r: (b, 0, r)),
                   pl.BlockSpec((None, seq, ns * 128), lambda b, r: (b, 0, r))],
        out_shape=[jax.ShapeDtypeStruct((bsz, seq, dl * B_WIDTH), BF16),
                   jax.ShapeDtypeStruct((bsz, seq, dl * 128), F32)],
        compiler_params=pltpu.CompilerParams(dimension_semantics=("parallel", "parallel"),
                                             vmem_limit_bytes=VMEM_LIMIT),
        name=f"attn{dl}",
    )(qkv_g)


def _natural_rows(ref, dl, scr):
    nchunk, tm, _ = scr.shape
    w = nchunk * 128
    per = tm // dl
    for r in range(dl):
        for c in range(nchunk):
            scr[c, pl.ds(r, per, stride=dl), :] = ref[:, r * w + c * 128:r * w + (c + 1) * 128].astype(F32)
    return jnp.concatenate([scr[c] for c in range(nchunk)], axis=1)


def _mix_kernel(ga_ref, gates_ref, o1_ref, o2_ref, o3_ref, l1_ref, l2_ref, l3_ref, x_ref,
                wa_ref, wb_ref, wo_ref, wr_ref, br_ref, g1_ref, b1_ref,
                x1_ref, x1p_ref, route_ref, routet_ref, o2s_ref, o3s_ref, l2s_ref, l3s_ref):
    tm = x_ref.shape[0]
    o2 = _natural_rows(o2_ref, 4, o2s_ref)
    o3 = _natural_rows(o3_ref, 16, o3s_ref)
    l2 = _natural_rows(l2_ref, 4, l2s_ref)
    l3 = _natural_rows(l3_ref, 16, l3s_ref)
    er = lax.broadcasted_iota(I32, (256, B_WIDTH), 0)
    ec = lax.broadcasted_iota(I32, (256, B_WIDTH), 1)
    expand = jnp.where(er % 128 == (ec // B_HEAD_DIM) * 16, 1.0, 0.0).astype(BF16)

    def widen(w):
        hi = w.astype(BF16)
        lo = (w - hi.astype(F32)).astype(BF16)
        return _dot(jnp.concatenate([hi, lo], axis=1), expand)

    h = tm // 2
    halves = (slice(0, h), slice(h, tm))
    obs = []
    for r in halves:
        l1 = l1_ref[r, :]
        mx = jnp.maximum(l1, jnp.maximum(l2[r], l3[r]))
        e1, e2, e3 = jnp.exp(l1 - mx), jnp.exp(l2[r] - mx), jnp.exp(l3[r] - mx)
        inv = 1.0 / (e1 + e2 + e3)
        obs.append(widen(e1 * inv) * o1_ref[r, :].astype(F32) + widen(e2 * inv) * o2[r] + widen(e3 * inv) * o3[r])
    ybs = [_dot(ob.astype(BF16), wb_ref[...]) for ob in obs]
    yas = [_dot(ga_ref[r, :], wa_ref[...]) for r in halves]
    pres = [gates_ref[r, :D_MODEL].astype(F32) * ya + gates_ref[r, D_MODEL:].astype(F32) * yb
            for r, ya, yb in zip(halves, yas, ybs)]
    mixes = [_dot(pre.astype(BF16), wo_ref[...]) for pre in pres]
    x1s = [_ln(DEEPNORM_ALPHA * x_ref[r, :] + mix, g1_ref[...], b1_ref[...]) for r, mix in zip(halves, mixes)]
    logits = [_dot(x1.astype(BF16), wr_ref[...]) + br_ref[...] for x1 in x1s]

    lane = lax.broadcasted_iota(I32, (h, 128), 1).astype(F32)
    big = 1e9
    for i, r in enumerate(halves):
        x1_ref[r, :] = x1s[i]
        _store_packed_rows(x1p_ref.at[i * h // 8:(i + 1) * h // 8], x1s[i])
        lg = logits[i]
        gl = jnp.where(lane < N_GROUPS, lg, NEG)
        gm = jnp.max(gl, axis=-1, keepdims=True)
        gidx = jnp.min(jnp.where(gl == gm, lane, big), axis=-1, keepdims=True)
        gsum = jnp.sum(jnp.where(lane < N_GROUPS, jnp.exp(gl - gm), 0.0), axis=-1, keepdims=True)
        gprob = 1.0 / gsum
        lo_lane = N_GROUPS + N_EXPERTS * gidx
        el = jnp.where((lane >= lo_lane) & (lane < lo_lane + N_EXPERTS), lg, NEG)
        v1 = jnp.max(el, axis=-1, keepdims=True)
        i1 = jnp.min(jnp.where(el == v1, lane, big), axis=-1, keepdims=True)
        el2 = jnp.where(lane == i1, NEG, el)
        v2 = jnp.max(el2, axis=-1, keepdims=True)
        i2 = jnp.min(jnp.where(el2 == v2, lane, big), axis=-1, keepdims=True)
        t = jnp.exp(v2 - v1)
        w1 = 1.0 / (1.0 + t)
        w2 = t * w1
        route = jnp.where(lane == 0, i1 - N_GROUPS,
                          jnp.where(lane == 1, i2 - N_GROUPS,
                                    jnp.where(lane == 2, gprob * w1,
                                              jnp.where(lane == 3, gprob * w2, 0.0))))
        route_ref[r, :] = route
        routet_ref[:, r] = route.T[:8, :]


def _mix(ga, gates, o1, o2, o3, l1, l2, l3, x2, wa, wb, wo, wr, br, g1, b1):
    n = x2.shape[0]
    bsz = o2.shape[0]
    tm = TM_MIX
    tiles = n // bsz // tm
    rows = lambda w: pl.BlockSpec((tm, w), lambda i: (i, 0))
    grouped = lambda a, dl: pl.BlockSpec((None, tm // dl, a.shape[2]), lambda i: (i // tiles, i % tiles, 0))
    full = lambda a: pl.BlockSpec(a.shape, lambda i: (0,) * a.ndim)
    return pl.pallas_call(
        _mix_kernel,
        grid=(n // tm,),
        in_specs=[rows(A_WIDTH), rows(2 * D_MODEL), rows(B_WIDTH), grouped(o2, 4), grouped(o3, 16),
                  rows(128), grouped(l2, 4), grouped(l3, 16), rows(D_MODEL),
                  full(wa), full(wb), full(wo), full(wr), full(br), full(g1), full(b1)],
        out_specs=[rows(D_MODEL), pl.BlockSpec((tm // 8, 32, 128), lambda i: (i, 0, 0)), rows(128),
                   pl.BlockSpec((8, tm), lambda i: (0, i))],
        out_shape=[jax.ShapeDtypeStruct((n, D_MODEL), F32),
                   jax.ShapeDtypeStruct((n // 8, 32, 128), U32),
                   jax.ShapeDtypeStruct((n, 128), F32),
                   jax.ShapeDtypeStruct((8, n), F32)],
        scratch_shapes=[pltpu.VMEM((B_WIDTH // 128, tm, 128), F32), pltpu.VMEM((B_WIDTH // 128, tm, 128), F32),
                        pltpu.VMEM((1, tm, 128), F32), pltpu.VMEM((1, tm, 128), F32)],
        compiler_params=pltpu.CompilerParams(dimension_semantics=("parallel",), vmem_limit_bytes=VMEM_LIMIT),
        name="mix",
    )(ga, gates, o1, o2, o3, l1, l2, l3, x2, wa, wb, wo, wr, br, g1, b1)


SC_WINDOW = 128


def _sc_mesh():
    return plsc.VectorSubcoreMesh(core_axis_name="core", subcore_axis_name="subcore")


def _sc_scatter_rows(rows, dst, n_out):
    r = rows.shape[0]
    m = dst.shape[0]
    nblk = r // SC_WINDOW

    @pl.kernel(out_type=jax.ShapeDtypeStruct((n_out, 128), rows.dtype), mesh=_sc_mesh())
    def scatter(rows_hbm, dst_hbm, out_hbm):
        def body(rows_vmem, dst_vmem):
            pltpu.sync_copy(rows_vmem, out_hbm.at[dst_vmem.at[0]])

        pltpu.emit_pipeline(
            body,
            grid=(m // SC_WINDOW,),
            in_specs=[pl.BlockSpec((SC_WINDOW, 128), lambda i: (i % nblk, 0)),
                      pl.BlockSpec((1, SC_WINDOW), lambda i: (0, i))],
            out_specs=[],
            core_axis_name=("core", "subcore"),
            dimension_semantics=(pltpu.PARALLEL,),
        )(rows_hbm, dst_hbm)

    return scatter(rows, dst.reshape(1, m))


def _sc_gather_rows(table, src):
    m = src.shape[0]

    @pl.kernel(out_type=jax.ShapeDtypeStruct((m, 128), table.dtype), mesh=_sc_mesh())
    def gather(table_hbm, src_hbm, out_hbm):
        def body(src_vmem, out_vmem):
            pltpu.sync_copy(table_hbm.at[src_vmem.at[0]], out_vmem)

        pltpu.emit_pipeline(
            body,
            grid=(m // SC_WINDOW,),
            in_specs=[pl.BlockSpec((1, SC_WINDOW), lambda i: (0, i))],
            out_specs=[pl.BlockSpec((SC_WINDOW, 128), lambda i: (i, 0))],
            core_axis_name=("core", "subcore"),
            dimension_semantics=(pltpu.PARALLEL,),
        )(src_hbm, out_hbm)

    return gather(table, src.reshape(1, m))


def _moe_kernel(te_ref, tv_ref, xs_ref, wg_ref, wu_ref, wd_ref, ys_ref, wgb_ref, wub_ref, wdb_ref):
    t = pl.program_id(0)

    @pl.when((t == 0) | (te_ref[t] != te_ref[jnp.maximum(t - 1, 0)]))
    def _():
        wgb_ref[...] = wg_ref[...].astype(BF16)
        wub_ref[...] = wu_ref[...].astype(BF16)
        wdb_ref[...] = wd_ref[...].astype(BF16)

    @pl.when(tv_ref[t] == 1)
    def _():
        xb = _load_packed_rows(xs_ref).astype(BF16)
        g = _dot(xb, wgb_ref[...])
        u = _dot(xb, wub_ref[...])
        h = (g * _sigmoid(g) * u).astype(BF16)
        _store_packed_rows(ys_ref, _dot(h, wdb_ref[...]))

    @pl.when(tv_ref[t] == 0)
    def _():
        ys_ref[...] = jnp.zeros(ys_ref.shape, U32)


def _moe(tile_expert, tile_valid, xs, wg, wu, wd):
    tm = TM_MOE
    nt = tile_expert.shape[0]
    rows = pl.BlockSpec((tm // 8, 32, 128), lambda t, te, tv: (t, 0, 0))
    return pl.pallas_call(
        _moe_kernel,
        grid_spec=pltpu.PrefetchScalarGridSpec(
            num_scalar_prefetch=2,
            grid=(nt,),
            in_specs=[rows,
                      pl.BlockSpec((None, D_MODEL, D_EXPERT), lambda t, te, tv: (te[t], 0, 0)),
                      pl.BlockSpec((None, D_MODEL, D_EXPERT), lambda t, te, tv: (te[t], 0, 0)),
                      pl.BlockSpec((None, D_EXPERT, D_MODEL), lambda t, te, tv: (te[t], 0, 0))],
            out_specs=rows,
            scratch_shapes=[pltpu.VMEM((D_MODEL, D_EXPERT), BF16), pltpu.VMEM((D_MODEL, D_EXPERT), BF16),
                            pltpu.VMEM((D_EXPERT, D_MODEL), BF16)]),
        out_shape=jax.ShapeDtypeStruct((nt * tm // 8, 32, 128), U32),
        compiler_params=pltpu.CompilerParams(dimension_semantics=("arbitrary",), vmem_limit_bytes=VMEM_LIMIT),
        name="moe",
    )(tile_expert, tile_valid, xs, wg, wu, wd)


def _final_kernel(y0_ref, y1_ref, x1_ref, p_ref, route_ref, wple_ref, wpg_ref, g2_ref, b2_ref, *rest):
    out_ref = rest[-1]
    tm = x1_ref.shape[0]
    h = tm // 2
    halves = (slice(0, h), slice(h, tm))
    x1s = [x1_ref[r, :] for r in halves]
    gates = [_dot(x1.astype(BF16), wpg_ref[...]) for x1 in x1s]
    plins = [_dot(p_ref[r, :].astype(BF16), wple_ref[...]) for r in halves]
    for i, r in enumerate(halves):
        g8 = slice(i * h // 8, (i + 1) * h // 8)
        ffn = (route_ref[r, 2:3] * _load_packed_rows(y0_ref.at[g8])
               + route_ref[r, 3:4] * _load_packed_rows(y1_ref.at[g8]))
        ple = plins[i] * _sigmoid(gates[i])
        out_ref[r, :] = _ln(DEEPNORM_ALPHA * x1s[i] + ffn + ple, g2_ref[...], b2_ref[...])


def _final(yg, x1, p2, route, wple, wpg, g2, b2, half, prev):
    n = x1.shape[0]
    tm = TM_FIN
    nt = n // 2 // tm
    off = half * nt
    rows = lambda w: pl.BlockSpec((tm, w), lambda t: (t + off, 0))
    full = lambda a: pl.BlockSpec(a.shape, lambda t: (0,) * a.ndim)
    in_specs = [pl.BlockSpec((tm // 8, 32, 128), lambda t: (t, 0, 0)),
                pl.BlockSpec((tm // 8, 32, 128), lambda t: (t + nt, 0, 0)),
                rows(D_MODEL), rows(PLE_DIM), rows(128), full(wple), full(wpg), full(g2), full(b2)]
    args = [yg, yg, x1, p2, route, wple, wpg, g2, b2]
    aliases = {}
    if prev is not None:
        in_specs.append(pl.BlockSpec(memory_space=pl.ANY))
        args.append(prev)
        aliases = {len(args) - 1: 0}
    return pl.pallas_call(
        _final_kernel,
        grid=(nt,),
        in_specs=in_specs,
        out_specs=rows(D_MODEL),
        out_shape=jax.ShapeDtypeStruct((n, D_MODEL), F32),
        input_output_aliases=aliases,
        compiler_params=pltpu.CompilerParams(dimension_semantics=("parallel",), vmem_limit_bytes=VMEM_LIMIT),
        name=f"final{half}",
    )(*args)


def _routing_tables(routet, n):
    tm = TM_MOE
    nt = (2 * n) // tm + N_EXPERTS_TOTAL
    ef = routet[0:2].astype(I32).reshape(-1)
    onehot = (ef[:, None] == jnp.arange(N_EXPERTS_TOTAL, dtype=I32)[None, :]).astype(I32)
    csum = jnp.cumsum(onehot, axis=0)
    counts = csum[-1]
    rank = jnp.sum(csum * onehot, axis=1) - 1
    padded = ((counts + tm - 1) // tm) * tm
    ends = jnp.cumsum(padded)
    offs = ends - padded
    pos = jnp.sum(onehot * offs[None, :], axis=1) + rank
    tile_start = jnp.arange(nt, dtype=I32) * tm
    tile_expert = jnp.minimum(jnp.sum((tile_start[:, None] >= ends[None, :]).astype(I32), axis=1),
                              N_EXPERTS_TOTAL - 1).astype(I32)
    tile_valid = (tile_start < ends[-1]).astype(I32)
    pr = pos.reshape(2 * n // 32, 32)
    pp = jnp.concatenate([jnp.tile(pr[:, 8 * a:8 * (a + 1)], (1, SUBROWS)) for a in range(4)], axis=1)
    jv = (jnp.arange(128, dtype=I32) % 32) // 8
    piece = (pp // 8) * (8 * SUBROWS) + pp % 8 + 8 * jv[None, :]
    return tile_expert, tile_valid, piece.reshape(2, n * SUBROWS)


def kernel(x, p, w_in, a_ln_g, a_ln_b, a_ws, a_bs, w_a_proj, w_b_proj, w_o, ln1_g, ln1_b, w_group_router,
           b_group_router, w_expert_router, b_expert_router, w_gate, w_up, w_down, w_ple, w_ple_gate,
           ln2_g, ln2_b):
    bsz, s, d = x.shape
    n = bsz * s
    assert d == D_MODEL and s % (SPAN * max(B_DILATIONS)) == 0 and n % TM_PROJ == 0
    assert w_in.shape[0] == 1, "one layer"

    w_in_b = w_in[0].astype(BF16)
    a_bias = jnp.repeat(a_bs[0].T, A_WIDTH // 8, axis=1)

    ga, gates, qkv1, qkv2, qkv3 = _proj(x, w_in_b, a_ln_g, a_ln_b, a_ws[0], a_bias)
    o1, l1 = _attn(qkv1.reshape(bsz, 1, s, 3 * COL), 0)
    o2, l2 = _attn(qkv2, 1)
    o3, l3 = _attn(qkv3, 2)

    pad = 128 - N_GROUPS - N_EXPERTS_TOTAL
    wr = jnp.concatenate([w_group_router[0], w_expert_router[0].reshape(d, N_EXPERTS_TOTAL),
                          jnp.zeros((d, pad), F32)], axis=1).astype(BF16)
    br = jnp.concatenate([b_group_router[0], b_expert_router[0].reshape(-1), jnp.zeros((pad,), F32)])[None, :]
    x1, x1p, route, routet = _mix(
        ga, gates, o1.reshape(n, B_WIDTH), o2, o3, l1.reshape(n, 128), l2, l3, x.reshape(n, d),
        w_a_proj[0].astype(BF16), w_b_proj[0].astype(BF16), w_o[0].astype(BF16), wr, br, ln1_g, ln1_b)

    tile_expert, tile_valid, piece = _routing_tables(routet, n)
    nt = tile_expert.shape[0]
    xs = _sc_scatter_rows(x1p.reshape(n * SUBROWS, 128), piece.reshape(-1), nt * TM_MOE * SUBROWS)
    ys = _moe(tile_expert, tile_valid, xs.reshape(nt * TM_MOE // 8, 32, 128),
              w_gate[0].reshape(N_EXPERTS_TOTAL, d, D_EXPERT), w_up[0].reshape(N_EXPERTS_TOTAL, d, D_EXPERT),
              w_down[0].reshape(N_EXPERTS_TOTAL, D_EXPERT, d)).reshape(nt * TM_MOE * SUBROWS, 128)
    out = None
    hp = n * SUBROWS // 2
    for half in range(2):
        yg = _sc_gather_rows(ys, piece[:, half * hp:(half + 1) * hp].reshape(-1))
        out = _final(yg.reshape(n // 8, 32, 128), x1, p[0].reshape(n, PLE_DIM), route,
                     w_ple[0].astype(BF16), w_ple_gate[0].astype(BF16), ln2_g, ln2_b, half, out)
    return out.reshape(bsz, s, d)
```

```python
import functools

import jax
import jax.numpy as jnp
from jax import lax
from jax.experimental import pallas as pl
from jax.experimental.pallas import tpu as pltpu
from jax.experimental.pallas import tpu_sc as plsc

F32 = jnp.float32
BF16 = jnp.bfloat16
U32 = jnp.uint32
I32 = jnp.int32

D_MODEL = 1024
PLE_DIM = 256
A_WIDTH = 512
A_CHUNK = 128
B_HEAD_DIM = 64
B_HEADS = 8
B_WIDTH = 512
B_DILATIONS = (1, 4, 16)
SPAN = 128
N_GROUPS = 4
N_EXPERTS = 8
N_EXPERTS_TOTAL = N_GROUPS * N_EXPERTS
D_EXPERT = 256
DEEPNORM_ALPHA = 2.0 ** 0.25
LN_EPS = 1e-5
COL = 512
NEG = -1e30

VMEM_LIMIT = 56 * 1024 * 1024

TM_PROJ = 512
TM_MIX = 512
TM_MOE = 512
TM_FIN = 512
COMBINE_PARTS = 2


def _ln(x, g, b):
    mu = jnp.mean(x, axis=-1, keepdims=True)
    xc = x - mu
    var = jnp.mean(xc * xc, axis=-1, keepdims=True)
    return xc * lax.rsqrt(var + LN_EPS) * g + b


def _gelu_tanh(x):
    return 0.5 * x * (1.0 + jnp.tanh(0.7978845608028654 * (x + 0.044715 * (x * x * x))))


def _sigmoid(x):
    return 0.5 * jnp.tanh(0.5 * x) + 0.5


def _dot(a, b):
    return jnp.dot(a, b, preferred_element_type=F32)


PACK_W = D_MODEL // 2
SUBROWS = PACK_W // 128


def _store_packed_rows(ref, x):
    m = x.shape[0]
    xb = x.astype(BF16).astype(F32)
    lo = pltpu.bitcast(xb[:, :PACK_W], U32) >> 16
    hi = pltpu.bitcast(xb[:, PACK_W:], U32) & jnp.uint32(0xFFFF0000)
    w = hi | lo
    for j in range(SUBROWS):
        ref[:, 8 * j:8 * (j + 1), :] = w[:, 128 * j:128 * (j + 1)].reshape(m // 8, 8, 128)


def _load_packed_rows(ref):
    m = ref.shape[0] * 8
    ws = [ref[:, 8 * j:8 * (j + 1), :].reshape(m, 128) for j in range(SUBROWS)]
    lo = [pltpu.bitcast(w << 16, F32) for w in ws]
    hi = [pltpu.bitcast(w & jnp.uint32(0xFFFF0000), F32) for w in ws]
    return jnp.concatenate(lo + hi, axis=1)


def _proj_kernel(x_ref, *refs):
    w = refs[:15]
    lng_ref, lnb_ref, ws_ref, bias_ref = refs[15:19]
    ga_ref, gates_ref, qkv1_ref, qkv2_ref, qkv3_ref = refs[19:24]
    xc_ref = refs[24]
    tm = x_ref.shape[0]
    xb = x_ref[...].astype(BF16)

    u = _gelu_tanh(_dot(xb, w[0][...]))
    v = _gelu_tanh(_dot(xb, w[1][...]))
    vn = _ln(v, lng_ref[...], lnb_ref[...]).astype(BF16)

    row = lax.broadcasted_iota(I32, (A_CHUNK, A_CHUNK), 0)
    colm = lax.broadcasted_iota(I32, (A_CHUNK, A_CHUNK), 1)
    causal = colm <= row
    lo = colm < 64
    zero = jnp.zeros((A_CHUNK, A_CHUNK), BF16)
    wcat = []
    for j in range(4):
        w0 = jnp.where(causal, ws_ref[2 * j], 0.0).astype(BF16)
        w1 = jnp.where(causal, ws_ref[2 * j + 1], 0.0).astype(BF16)
        wcat.append(jnp.concatenate([w0, w1], axis=1))
    for c in range(tm // A_CHUNK):
        r0 = c * A_CHUNK
        for j in range(4):
            c0 = j * 128
            vt = vn[r0:r0 + A_CHUNK, c0:c0 + 128]
            rhs = jnp.concatenate([jnp.where(lo, vt, zero), jnp.where(lo, zero, vt)], axis=0)
            mixed = _dot(wcat[j], rhs) + bias_ref[:, c0:c0 + 128]
            ga_ref[r0:r0 + A_CHUNK, c0:c0 + 128] = (u[r0:r0 + A_CHUNK, c0:c0 + 128] * mixed).astype(BF16)

    for i in range(4):
        gates_ref[:, i * COL:(i + 1) * COL] = _sigmoid(_dot(xb, w[11 + i][...])).astype(BF16)
    for j in range(3):
        qkv1_ref[:, j * COL:(j + 1) * COL] = _dot(xb, w[2 + 3 * j][...]).astype(BF16)

    for c in range(D_MODEL // 128):
        xc_ref[c] = x_ref[:, c * 128:(c + 1) * 128]
    for gi, out_ref in ((1, qkv2_ref), (2, qkv3_ref)):
        dl = B_DILATIONS[gi]
        per = tm // dl
        xp = jnp.concatenate(
            [jnp.concatenate([xc_ref[c, pl.ds(r, per, stride=dl), :] for c in range(D_MODEL // 128)], axis=1)
             for r in range(dl)], axis=0).astype(BF16)
        for j in range(3):
            res = _dot(xp, w[2 + 3 * j + gi][...]).astype(BF16)
            for r in range(dl):
                out_ref[r, :, j * COL:(j + 1) * COL] = res[r * per:(r + 1) * per]


def _proj(x, w_in_b, a_ln_g, a_ln_b, a_ws, a_bias):
    bsz, s, _ = x.shape
    n = bsz * s
    tm = TM_PROJ
    tiles = s // tm
    x2 = x.reshape(n, D_MODEL)
    wspec = lambda j: pl.BlockSpec((D_MODEL, COL), lambda i, j=j: (0, j), pipeline_mode=pl.Buffered(1))
    full = lambda shape: pl.BlockSpec(shape, lambda i: (0,) * len(shape))
    rows = lambda width: pl.BlockSpec((tm, width), lambda i: (i, 0))
    dil = lambda dl: pl.BlockSpec((None, dl, tm // dl, 3 * COL), lambda i: (i // tiles, 0, i % tiles, 0))
    return pl.pallas_call(
        _proj_kernel,
        grid=(n // tm,),
        in_specs=[rows(D_MODEL)] + [wspec(j) for j in range(15)]
                 + [full((1, A_WIDTH)), full((1, A_WIDTH)), full((8, A_CHUNK, A_CHUNK)), full((A_CHUNK, A_WIDTH))],
        out_specs=[rows(A_WIDTH), rows(4 * COL), rows(3 * COL), dil(4), dil(16)],
        out_shape=[jax.ShapeDtypeStruct((n, A_WIDTH), BF16),
                   jax.ShapeDtypeStruct((n, 4 * COL), BF16),
                   jax.ShapeDtypeStruct((n, 3 * COL), BF16),
                   jax.ShapeDtypeStruct((bsz, 4, s // 4, 3 * COL), BF16),
                   jax.ShapeDtypeStruct((bsz, 16, s // 16, 3 * COL), BF16)],
        scratch_shapes=[pltpu.VMEM((D_MODEL // 128, tm, 128), F32)],
        compiler_params=pltpu.CompilerParams(dimension_semantics=("parallel",), vmem_limit_bytes=VMEM_LIMIT),
        name="proj",
    )(x2, *([w_in_b] * 15), a_ln_g, a_ln_b, a_ws, a_bias)


def _attn_kernel(qkv_ref, o_ref, lse_ref, *, ns, seq):
    nb = seq // SPAN
    lane = lax.broadcasted_iota(I32, (SPAN, 128), 1)
    lo = lane < 64
    lane16 = lane // 16
    qi = lax.broadcasted_iota(I32, (SPAN, 2 * SPAN), 0)
    ki = lax.broadcasted_iota(I32, (SPAN, 2 * SPAN), 1)
    causal = lax.broadcasted_iota(I32, (SPAN, SPAN), 1) <= lax.broadcasted_iota(I32, (SPAN, SPAN), 0)
    bias_first = jnp.where(causal, 0.0, NEG).astype(F32)
    bias_first = jnp.concatenate([bias_first, bias_first], axis=0)
    bias_main = jnp.where((ki >= qi) & (ki <= qi + SPAN), 0.0, NEG).astype(F32)
    bias_main = jnp.concatenate([bias_main, bias_main], axis=0)
    zero = jnp.zeros((SPAN, 128), BF16)

    for s in range(ns):
        def block(row0, start, bias, s=s):
            win = bias.shape[1]
            pairs = range(B_HEADS // 2)
            scores, values = [], []
            for jp in pairs:
                c0 = jp * 128
                q = qkv_ref[s, pl.ds(row0, SPAN), c0:c0 + 128] * jnp.asarray(0.125, BF16)
                k = qkv_ref[s, pl.ds(start, win), COL + c0:COL + c0 + 128]
                values.append(qkv_ref[s, pl.ds(start, win), 2 * COL + c0:2 * COL + c0 + 128])
                qs = jnp.concatenate([jnp.where(lo, q, zero), jnp.where(lo, zero, q)], axis=0)
                scores.append(lax.dot_general(qs, k, (((1,), (1,)), ((), ())), preferred_element_type=F32) + bias)
            probs, maxes, sums = [], [], []
            for jp in pairs:
                m = jnp.max(scores[jp], axis=-1, keepdims=True)
                p = jnp.exp(scores[jp] - m)
                maxes.append(m)
                sums.append(jnp.sum(p, axis=-1, keepdims=True))
                probs.append(p.astype(BF16))
            lse_tile = jnp.zeros((SPAN, 128), F32)
            for jp in pairs:
                c0 = jp * 128
                ov = _dot(probs[jp], values[jp])
                inv = 1.0 / sums[jp]
                o = jnp.where(lo, ov[:SPAN] * inv[:SPAN], ov[SPAN:] * inv[SPAN:])
                o_ref[pl.ds(row0, SPAN), s * B_WIDTH + c0:s * B_WIDTH + c0 + 128] = o.astype(BF16)
                lse = maxes[jp] + jnp.log(sums[jp])
                lse_tile = jnp.where(lane16 == 2 * jp, lse[:SPAN],
                                     jnp.where(lane16 == 2 * jp + 1, lse[SPAN:], lse_tile))
            lse_ref[pl.ds(row0, SPAN), s * 128:(s + 1) * 128] = lse_tile

        block(0, 0, bias_first)
        if nb > 1:
            def body(i, carry):
                block(pl.multiple_of(i * SPAN, SPAN), pl.multiple_of((i - 1) * SPAN, SPAN), bias_main)
                return carry
            lax.fori_loop(1, nb, body, 0)


def _attn(qkv_g, gi):
    bsz, dl, seq, _ = qkv_g.shape
    ns = max(1, min(dl, TM_PROJ // seq))
    return pl.pallas_call(
        functools.partial(_attn_kernel, ns=ns, seq=seq),
        grid=(bsz, dl // ns),
        in_specs=[pl.BlockSpec((None, ns, seq, 3 * COL), lambda b, r: (b, r, 0, 0))],
        out_specs=[pl.BlockSpec((None, seq, ns * B_WIDTH), lambda b, r: (b, 0, r)),
                   pl.BlockSpec((None, seq, ns * 128), lambda b, r: (b, 0, r))],
        out_shape=[jax.ShapeDtypeStruct((bsz, seq, dl * B_WIDTH), BF16),
                   jax.ShapeDtypeStruct((bsz, seq, dl * 128), F32)],
        compiler_params=pltpu.CompilerParams(dimension_semantics=("parallel", "parallel"),
                                             vmem_limit_bytes=VMEM_LIMIT),
        name=f"attn{dl}",
    )(qkv_g)


def _natural_rows(ref, dl, scr):
    nchunk, tm, _ = scr.shape
    w = nchunk * 128
    per = tm // dl
    for r in range(dl):
        for c in range(nchunk):
            scr[c, pl.ds(r, per, stride=dl), :] = ref[:, r * w + c * 128:r * w + (c + 1) * 128].astype(F32)
    return jnp.concatenate([scr[c] for c in range(nchunk)], axis=1)


def _mix_kernel(ga_ref, gates_ref, o1_ref, o2_ref, o3_ref, l1_ref, l2_ref, l3_ref, x_ref,
                wa_ref, wb_ref, wo_ref, wr_ref, br_ref, g1_ref, b1_ref,
                x1_ref, x1p_ref, route_ref, routet_ref, o2s_ref, o3s_ref, l2s_ref, l3s_ref):
    tm = x_ref.shape[0]
    o2 = _natural_rows(o2_ref, 4, o2s_ref)
    o3 = _natural_rows(o3_ref, 16, o3s_ref)
    l2 = _natural_rows(l2_ref, 4, l2s_ref)
    l3 = _natural_rows(l3_ref, 16, l3s_ref)
    er = lax.broadcasted_iota(I32, (256, B_WIDTH), 0)
    ec = lax.broadcasted_iota(I32, (256, B_WIDTH), 1)
    expand = jnp.where(er % 128 == (ec // B_HEAD_DIM) * 16, 1.0, 0.0).astype(BF16)

    def widen(w):
        hi = w.astype(BF16)
        lo = (w - hi.astype(F32)).astype(BF16)
        return _dot(jnp.concatenate([hi, lo], axis=1), expand)

    h = tm // 2
    halves = (slice(0, h), slice(h, tm))
    obs = []
    for r in halves:
        l1 = l1_ref[r, :]
        mx = jnp.maximum(l1, jnp.maximum(l2[r], l3[r]))
        e1, e2, e3 = jnp.exp(l1 - mx), jnp.exp(l2[r] - mx), jnp.exp(l3[r] - mx)
        inv = 1.0 / (e1 + e2 + e3)
        obs.append(widen(e1 * inv) * o1_ref[r, :].astype(F32) + widen(e2 * inv) * o2[r] + widen(e3 * inv) * o3[r])
    ybs = [_dot(ob.astype(BF16), wb_ref[...]) for ob in obs]
    yas = [_dot(ga_ref[r, :], wa_ref[...]) for r in halves]
    pres = [gates_ref[r, :D_MODEL].astype(F32) * ya + gates_ref[r, D_MODEL:].astype(F32) * yb
            for r, ya, yb in zip(halves, yas, ybs)]
    mixes = [_dot(pre.astype(BF16), wo_ref[...]) for pre in pres]
    x1s = [_ln(DEEPNORM_ALPHA * x_ref[r, :] + mix, g1_ref[...], b1_ref[...]) for r, mix in zip(halves, mixes)]
    logits = [_dot(x1.astype(BF16), wr_ref[...]) + br_ref[...] for x1 in x1s]

    lane = lax.broadcasted_iota(I32, (h, 128), 1).astype(F32)
    big = 1e9
    for i, r in enumerate(halves):
        x1_ref[r, :] = x1s[i]
        _store_packed_rows(x1p_ref.at[i * h // 8:(i + 1) * h // 8], x1s[i])
        lg = logits[i]
        gl = jnp.where(lane < N_GROUPS, lg, NEG)
        gm = jnp.max(gl, axis=-1, keepdims=True)
        gidx = jnp.min(jnp.where(gl == gm, lane, big), axis=-1, keepdims=True)
        gsum = jnp.sum(jnp.where(lane < N_GROUPS, jnp.exp(gl - gm), 0.0), axis=-1, keepdims=True)
        gprob = 1.0 / gsum
        lo_lane = N_GROUPS + N_EXPERTS * gidx
        el = jnp.where((lane >= lo_lane) & (lane < lo_lane + N_EXPERTS), lg, NEG)
        v1 = jnp.max(el, axis=-1, keepdims=True)
        i1 = jnp.min(jnp.where(el == v1, lane, big), axis=-1, keepdims=True)
        el2 = jnp.where(lane == i1, NEG, el)
        v2 = jnp.max(el2, axis=-1, keepdims=True)
        i2 = jnp.min(jnp.where(el2 == v2, lane, big), axis=-1, keepdims=True)
        t = jnp.exp(v2 - v1)
        w1 = 1.0 / (1.0 + t)
        w2 = t * w1
        route = jnp.where(lane == 0, i1 - N_GROUPS,
                          jnp.where(lane == 1, i2 - N_GROUPS,
                                    jnp.where(lane == 2, gprob * w1,
                                              jnp.where(lane == 3, gprob * w2, 0.0))))
        route_ref[r, :] = route
        routet_ref[:, r] = route.T[:8, :]


def _mix(ga, gates, o1, o2, o3, l1, l2, l3, x2, wa, wb, wo, wr, br, g1, b1):
    n = x2.shape[0]
    bsz = o2.shape[0]
    tm = TM_MIX
    tiles = n // bsz // tm
    rows = lambda w: pl.BlockSpec((tm, w), lambda i: (i, 0))
    grouped = lambda a, dl: pl.BlockSpec((None, tm // dl, a.shape[2]), lambda i: (i // tiles, i % tiles, 0))
    full = lambda a: pl.BlockSpec(a.shape, lambda i: (0,) * a.ndim)
    return pl.pallas_call(
        _mix_kernel,
        grid=(n // tm,),
        in_specs=[rows(A_WIDTH), rows(2 * D_MODEL), rows(B_WIDTH), grouped(o2, 4), grouped(o3, 16),
                  rows(128), grouped(l2, 4), grouped(l3, 16), rows(D_MODEL),
                  full(wa), full(wb), full(wo), full(wr), full(br), full(g1), full(b1)],
        out_specs=[rows(D_MODEL), pl.BlockSpec((tm // 8, 32, 128), lambda i: (i, 0, 0)), rows(128),
                   pl.BlockSpec((8, tm), lambda i: (0, i))],
        out_shape=[jax.ShapeDtypeStruct((n, D_MODEL), F32),
                   jax.ShapeDtypeStruct((n // 8, 32, 128), U32),
                   jax.ShapeDtypeStruct((n, 128), F32),
                   jax.ShapeDtypeStruct((8, n), F32)],
        scratch_shapes=[pltpu.VMEM((B_WIDTH // 128, tm, 128), F32), pltpu.VMEM((B_WIDTH // 128, tm, 128), F32),
                        pltpu.VMEM((1, tm, 128), F32), pltpu.VMEM((1, tm, 128), F32)],
        compiler_params=pltpu.CompilerParams(dimension_semantics=("parallel",), vmem_limit_bytes=VMEM_LIMIT),
        name="mix",
    )(ga, gates, o1, o2, o3, l1, l2, l3, x2, wa, wb, wo, wr, br, g1, b1)


SC_WINDOW = 128


def _sc_mesh():
    return plsc.VectorSubcoreMesh(core_axis_name="core", subcore_axis_name="subcore")


def _sc_scatter_rows(rows, dst, n_out):
    r = rows.shape[0]
    m = dst.shape[0]
    nblk = r // SC_WINDOW

    @pl.kernel(out_type=jax.ShapeDtypeStruct((n_out, 128), rows.dtype), mesh=_sc_mesh())
    def scatter(rows_hbm, dst_hbm, out_hbm):
        def body(rows_vmem, dst_vmem):
            pltpu.sync_copy(rows_vmem, out_hbm.at[dst_vmem.at[0]])

        pltpu.emit_pipeline(
            body,
            grid=(m // SC_WINDOW,),
            in_specs=[pl.BlockSpec((SC_WINDOW, 128), lambda i: (i % nblk, 0)),
                      pl.BlockSpec((1, SC_WINDOW), lambda i: (0, i))],
            out_specs=[],
            core_axis_name=("core", "subcore"),
            dimension_semantics=(pltpu.PARALLEL,),
        )(rows_hbm, dst_hbm)

    return scatter(rows, dst.reshape(1, m))


def _sc_gather_rows(table, src):
    m = src.shape[0]

    @pl.kernel(out_type=jax.ShapeDtypeStruct((m, 128), table.dtype), mesh=_sc_mesh())
    def gather(table_hbm, src_hbm, out_hbm):
        def body(src_vmem, out_vmem):
            pltpu.sync_copy(table_hbm.at[src_vmem.at[0]], out_vmem)

        pltpu.emit_pipeline(
            body,
            grid=(m // SC_WINDOW,),
            in_specs=[pl.BlockSpec((1, SC_WINDOW), lambda i: (0, i))],
            out_specs=[pl.BlockSpec((SC_WINDOW, 128), lambda i: (i, 0))],
            core_axis_name=("core", "subcore"),
            dimension_semantics=(pltpu.PARALLEL,),
        )(src_hbm, out_hbm)

    return gather(table, src.reshape(1, m))


def _moe_kernel(te_ref, tv_ref, tb_ref, xs_ref, wg_ref, wu_ref, wd_ref, ys_ref, wgb_ref, wub_ref, wdb_ref):
    del tb_ref
    t = pl.program_id(0)

    @pl.when((t == 0) | (te_ref[t] != te_ref[jnp.maximum(t - 1, 0)]))
    def _():
        wgb_ref[...] = wg_ref[...].astype(BF16)
        wub_ref[...] = wu_ref[...].astype(BF16)
        wdb_ref[...] = wd_ref[...].astype(BF16)

    @pl.when(tv_ref[t] == 1)
    def _():
        xb = _load_packed_rows(xs_ref).astype(BF16)
        g = _dot(xb, wgb_ref[...])
        u = _dot(xb, wub_ref[...])
        h = (g * _sigmoid(g) * u).astype(BF16)
        _store_packed_rows(ys_ref, _dot(h, wdb_ref[...]))


def _moe(tile_expert, tile_valid, tile_block, xs, wg, wu, wd):
    tm = TM_MOE
    nt = tile_expert.shape[0]
    rows = pl.BlockSpec((tm // 8, 32, 128), lambda t, te, tv, tb: (tb[t], 0, 0))
    return pl.pallas_call(
        _moe_kernel,
        grid_spec=pltpu.PrefetchScalarGridSpec(
            num_scalar_prefetch=3,
            grid=(nt,),
            in_specs=[rows,
                      pl.BlockSpec((None, D_MODEL, D_EXPERT), lambda t, te, tv, tb: (te[t], 0, 0)),
                      pl.BlockSpec((None, D_MODEL, D_EXPERT), lambda t, te, tv, tb: (te[t], 0, 0)),
                      pl.BlockSpec((None, D_EXPERT, D_MODEL), lambda t, te, tv, tb: (te[t], 0, 0))],
            out_specs=rows,
            scratch_shapes=[pltpu.VMEM((D_MODEL, D_EXPERT), BF16), pltpu.VMEM((D_MODEL, D_EXPERT), BF16),
                            pltpu.VMEM((D_EXPERT, D_MODEL), BF16)]),
        out_shape=jax.ShapeDtypeStruct((nt * tm // 8, 32, 128), U32),
        compiler_params=pltpu.CompilerParams(dimension_semantics=("arbitrary",), vmem_limit_bytes=VMEM_LIMIT),
        name="moe",
    )(tile_expert, tile_valid, tile_block, xs, wg, wu, wd)


def _ple_kernel(x1_ref, p_ref, wple_ref, wpg_ref, ple_ref):
    tm = x1_ref.shape[0]
    h = tm // 2
    halves = (slice(0, h), slice(h, tm))
    gates = [_dot(x1_ref[r, :].astype(BF16), wpg_ref[...]) for r in halves]
    plins = [_dot(p_ref[r, :].astype(BF16), wple_ref[...]) for r in halves]
    for i, r in enumerate(halves):
        ple_ref[r, :] = (plins[i] * _sigmoid(gates[i])).astype(BF16)


def _ple(x1, p2, wple, wpg):
    n = x1.shape[0]
    tm = TM_FIN
    rows = lambda w: pl.BlockSpec((tm, w), lambda t: (t, 0))
    full = lambda a: pl.BlockSpec(a.shape, lambda t: (0,) * a.ndim)
    return pl.pallas_call(
        _ple_kernel,
        grid=(n // tm,),
        in_specs=[rows(D_MODEL), rows(PLE_DIM), full(wple), full(wpg)],
        out_specs=rows(D_MODEL),
        out_shape=jax.ShapeDtypeStruct((n, D_MODEL), BF16),
        compiler_params=pltpu.CompilerParams(dimension_semantics=("parallel",), vmem_limit_bytes=VMEM_LIMIT),
        name="ple",
    )(x1, p2, wple, wpg)


def _final_kernel(y0_ref, y1_ref, x1_ref, ple_ref, route_ref, g2_ref, b2_ref, *rest):
    out_ref = rest[-1]
    ffn = route_ref[:, 2:3] * _load_packed_rows(y0_ref) + route_ref[:, 3:4] * _load_packed_rows(y1_ref)
    out_ref[...] = _ln(DEEPNORM_ALPHA * x1_ref[...] + ffn + ple_ref[...].astype(F32), g2_ref[...], b2_ref[...])


def _final(yg, x1, ple, route, g2, b2, part, prev):
    n = x1.shape[0]
    tm = TM_FIN
    nt = n // COMBINE_PARTS // tm
    off = part * nt
    rows = lambda w: pl.BlockSpec((tm, w), lambda t: (t + off, 0))
    full = lambda a: pl.BlockSpec(a.shape, lambda t: (0,) * a.ndim)
    in_specs = [pl.BlockSpec((tm // 8, 32, 128), lambda t: (t, 0, 0)),
                pl.BlockSpec((tm // 8, 32, 128), lambda t: (t + nt, 0, 0)),
                rows(D_MODEL), rows(D_MODEL), rows(128), full(g2), full(b2)]
    args = [yg, yg, x1, ple, route, g2, b2]
    aliases = {}
    if prev is not None:
        in_specs.append(pl.BlockSpec(memory_space=pl.ANY))
        args.append(prev)
        aliases = {len(args) - 1: 0}
    return pl.pallas_call(
        _final_kernel,
        grid=(nt,),
        in_specs=in_specs,
        out_specs=rows(D_MODEL),
        out_shape=jax.ShapeDtypeStruct((n, D_MODEL), F32),
        input_output_aliases=aliases,
        compiler_params=pltpu.CompilerParams(dimension_semantics=("parallel",), vmem_limit_bytes=VMEM_LIMIT),
        name=f"final{part}",
    )(*args)


def _routing_tables(routet, n):
    tm = TM_MOE
    nt = (2 * n) // tm + N_EXPERTS_TOTAL
    ef = routet[0:2].astype(I32).reshape(-1)
    onehot = (ef[:, None] == jnp.arange(N_EXPERTS_TOTAL, dtype=I32)[None, :]).astype(I32)
    csum = jnp.cumsum(onehot, axis=0)
    counts = csum[-1]
    rank = jnp.sum(csum * onehot, axis=1) - 1
    padded = ((counts + tm - 1) // tm) * tm
    ends = jnp.cumsum(padded)
    offs = ends - padded
    pos = jnp.sum(onehot * offs[None, :], axis=1) + rank
    tile_start = jnp.arange(nt, dtype=I32) * tm
    tile_expert = jnp.minimum(jnp.sum((tile_start[:, None] >= ends[None, :]).astype(I32), axis=1),
                              N_EXPERTS_TOTAL - 1).astype(I32)
    tile_valid = (tile_start < ends[-1]).astype(I32)
    tile_block = jnp.minimum(jnp.arange(nt, dtype=I32), ends[-1] // tm - 1)
    pr = pos.reshape(2 * n // 32, 32)
    pp = jnp.concatenate([jnp.tile(pr[:, 8 * a:8 * (a + 1)], (1, SUBROWS)) for a in range(4)], axis=1)
    jv = (jnp.arange(128, dtype=I32) % 32) // 8
    piece = (pp // 8) * (8 * SUBROWS) + pp % 8 + 8 * jv[None, :]
    return tile_expert, tile_valid, tile_block, piece.reshape(2, n * SUBROWS)


def kernel(x, p, w_in, a_ln_g, a_ln_b, a_ws, a_bs, w_a_proj, w_b_proj, w_o, ln1_g, ln1_b, w_group_router,
           b_group_router, w_expert_router, b_expert_router, w_gate, w_up, w_down, w_ple, w_ple_gate,
           ln2_g, ln2_b):
    bsz, s, d = x.shape
    n = bsz * s
    assert d == D_MODEL and s % (SPAN * max(B_DILATIONS)) == 0 and n % TM_PROJ == 0
    assert w_in.shape[0] == 1, "one layer"

    w_in_b = w_in[0].astype(BF16)
    a_bias = jnp.repeat(a_bs[0].T, A_WIDTH // 8, axis=1)

    ga, gates, qkv1, qkv2, qkv3 = _proj(x, w_in_b, a_ln_g, a_ln_b, a_ws[0], a_bias)
    o1, l1 = _attn(qkv1.reshape(bsz, 1, s, 3 * COL), 0)
    o2, l2 = _attn(qkv2, 1)
    o3, l3 = _attn(qkv3, 2)

    pad = 128 - N_GROUPS - N_EXPERTS_TOTAL
    wr = jnp.concatenate([w_group_router[0], w_expert_router[0].reshape(d, N_EXPERTS_TOTAL),
                          jnp.zeros((d, pad), F32)], axis=1).astype(BF16)
    br = jnp.concatenate([b_group_router[0], b_expert_router[0].reshape(-1), jnp.zeros((pad,), F32)])[None, :]
    x1, x1p, route, routet = _mix(
        ga, gates, o1.reshape(n, B_WIDTH), o2, o3, l1.reshape(n, 128), l2, l3, x.reshape(n, d),
        w_a_proj[0].astype(BF16), w_b_proj[0].astype(BF16), w_o[0].astype(BF16), wr, br, ln1_g, ln1_b)

    tile_expert, tile_valid, tile_block, piece = _routing_tables(routet, n)
    nt = tile_expert.shape[0]
    xs = _sc_scatter_rows(x1p.reshape(n * SUBROWS, 128), piece.reshape(-1), nt * TM_MOE * SUBROWS)
    ys = _moe(tile_expert, tile_valid, tile_block, xs.reshape(nt * TM_MOE // 8, 32, 128),
              w_gate[0].reshape(N_EXPERTS_TOTAL, d, D_EXPERT), w_up[0].reshape(N_EXPERTS_TOTAL, d, D_EXPERT),
              w_down[0].reshape(N_EXPERTS_TOTAL, D_EXPERT, d)).reshape(nt * TM_MOE * SUBROWS, 128)
    ple = _ple(x1, p[0].reshape(n, PLE_DIM), w_ple[0].astype(BF16), w_ple_gate[0].astype(BF16))
    out = None
    pp = n * SUBROWS // COMBINE_PARTS
    for part in range(COMBINE_PARTS):
        yg = _sc_gather_rows(ys, piece[:, part * pp:(part + 1) * pp].reshape(-1))
        out = _final(yg.reshape(2 * n // COMBINE_PARTS // 8, 32, 128), x1, ple, route, ln2_g, ln2_b, part, out)
    return out.reshape(bsz, s, d)
```

```python
import functools

import jax
import jax.numpy as jnp
from jax import lax
from jax.experimental import pallas as pl
from jax.experimental.pallas import tpu as pltpu
from jax.experimental.pallas import tpu_sc as plsc

F32 = jnp.float32
BF16 = jnp.bfloat16
U32 = jnp.uint32
I32 = jnp.int32

D_MODEL = 1024
PLE_DIM = 256
A_WIDTH = 512
A_CHUNK = 128
B_HEAD_DIM = 64
B_HEADS = 8
B_WIDTH = 512
B_DILATIONS = (1, 4, 16)
SPAN = 128
N_GROUPS = 4
N_EXPERTS = 8
N_EXPERTS_TOTAL = N_GROUPS * N_EXPERTS
D_EXPERT = 256
DEEPNORM_ALPHA = 2.0 ** 0.25
LN_EPS = 1e-5
COL = 512
NEG = -1e30

VMEM_LIMIT = 56 * 1024 * 1024

TM_PROJ = 512
TM_MIX = 512
TM_MOE = 512
TM_FIN = 512
COMBINE_PARTS = 4


def _ln(x, g, b):
    mu = jnp.mean(x, axis=-1, keepdims=True)
    xc = x - mu
    var = jnp.mean(xc * xc, axis=-1, keepdims=True)
    return xc * lax.rsqrt(var + LN_EPS) * g + b


def _gelu_tanh(x):
    return 0.5 * x * (1.0 + jnp.tanh(0.7978845608028654 * (x + 0.044715 * (x * x * x))))


def _sigmoid(x):
    return 0.5 * jnp.tanh(0.5 * x) + 0.5


def _dot(a, b):
    return jnp.dot(a, b, preferred_element_type=F32)


PACK_W = D_MODEL // 2
SUBROWS = PACK_W // 128


def _store_packed_rows(ref, x):
    m = x.shape[0]
    xb = x.astype(BF16).astype(F32)
    lo = pltpu.bitcast(xb[:, :PACK_W], U32) >> 16
    hi = pltpu.bitcast(xb[:, PACK_W:], U32) & jnp.uint32(0xFFFF0000)
    w = hi | lo
    for j in range(SUBROWS):
        ref[:, 8 * j:8 * (j + 1), :] = w[:, 128 * j:128 * (j + 1)].reshape(m // 8, 8, 128)


def _load_packed_rows(ref):
    m = ref.shape[0] * 8
    ws = [ref[:, 8 * j:8 * (j + 1), :].reshape(m, 128) for j in range(SUBROWS)]
    lo = [pltpu.bitcast(w << 16, F32) for w in ws]
    hi = [pltpu.bitcast(w & jnp.uint32(0xFFFF0000), F32) for w in ws]
    return jnp.concatenate(lo + hi, axis=1)


def _proj_kernel(x_ref, *refs):
    w = refs[:15]
    lng_ref, lnb_ref, ws_ref, bias_ref = refs[15:19]
    ga_ref, gates_ref, qkv1_ref, qkv2_ref, qkv3_ref = refs[19:24]
    xc_ref = refs[24]
    tm = x_ref.shape[0]
    xb = x_ref[...].astype(BF16)

    u = _gelu_tanh(_dot(xb, w[0][...]))
    v = _gelu_tanh(_dot(xb, w[1][...]))
    vn = _ln(v, lng_ref[...], lnb_ref[...]).astype(BF16)

    row = lax.broadcasted_iota(I32, (A_CHUNK, A_CHUNK), 0)
    colm = lax.broadcasted_iota(I32, (A_CHUNK, A_CHUNK), 1)
    causal = colm <= row
    lo = colm < 64
    zero = jnp.zeros((A_CHUNK, A_CHUNK), BF16)
    wcat = []
    for j in range(4):
        w0 = jnp.where(causal, ws_ref[2 * j], 0.0).astype(BF16)
        w1 = jnp.where(causal, ws_ref[2 * j + 1], 0.0).astype(BF16)
        wcat.append(jnp.concatenate([w0, w1], axis=1))
    for c in range(tm // A_CHUNK):
        r0 = c * A_CHUNK
        for j in range(4):
            c0 = j * 128
            vt = vn[r0:r0 + A_CHUNK, c0:c0 + 128]
            rhs = jnp.concatenate([jnp.where(lo, vt, zero), jnp.where(lo, zero, vt)], axis=0)
            mixed = _dot(wcat[j], rhs) + bias_ref[:, c0:c0 + 128]
            ga_ref[r0:r0 + A_CHUNK, c0:c0 + 128] = (u[r0:r0 + A_CHUNK, c0:c0 + 128] * mixed).astype(BF16)

    for i in range(4):
        gates_ref[:, i * COL:(i + 1) * COL] = _sigmoid(_dot(xb, w[11 + i][...])).astype(BF16)
    for j in range(3):
        qkv1_ref[:, j * COL:(j + 1) * COL] = _dot(xb, w[2 + 3 * j][...]).astype(BF16)

    for c in range(D_MODEL // 128):
        xc_ref[c] = x_ref[:, c * 128:(c + 1) * 128]
    for gi, out_ref in ((1, qkv2_ref), (2, qkv3_ref)):
        dl = B_DILATIONS[gi]
        per = tm // dl
        xp = jnp.concatenate(
            [jnp.concatenate([xc_ref[c, pl.ds(r, per, stride=dl), :] for c in range(D_MODEL // 128)], axis=1)
             for r in range(dl)], axis=0).astype(BF16)
        for j in range(3):
            res = _dot(xp, w[2 + 3 * j + gi][...]).astype(BF16)
            for r in range(dl):
                out_ref[r, :, j * COL:(j + 1) * COL] = res[r * per:(r + 1) * per]


def _proj(x, w_in_b, a_ln_g, a_ln_b, a_ws, a_bias):
    bsz, s, _ = x.shape
    n = bsz * s
    tm = TM_PROJ
    tiles = s // tm
    x2 = x.reshape(n, D_MODEL)
    wspec = lambda j: pl.BlockSpec((D_MODEL, COL), lambda i, j=j: (0, j), pipeline_mode=pl.Buffered(1))
    full = lambda shape: pl.BlockSpec(shape, lambda i: (0,) * len(shape))
    rows = lambda width: pl.BlockSpec((tm, width), lambda i: (i, 0))
    dil = lambda dl: pl.BlockSpec((None, dl, tm // dl, 3 * COL), lambda i: (i // tiles, 0, i % tiles, 0))
    return pl.pallas_call(
        _proj_kernel,
        grid=(n // tm,),
        in_specs=[rows(D_MODEL)] + [wspec(j) for j in range(15)]
                 + [full((1, A_WIDTH)), full((1, A_WIDTH)), full((8, A_CHUNK, A_CHUNK)), full((A_CHUNK, A_WIDTH))],
        out_specs=[rows(A_WIDTH), rows(4 * COL), rows(3 * COL), dil(4), dil(16)],
        out_shape=[jax.ShapeDtypeStruct((n, A_WIDTH), BF16),
                   jax.ShapeDtypeStruct((n, 4 * COL), BF16),
                   jax.ShapeDtypeStruct((n, 3 * COL), BF16),
                   jax.ShapeDtypeStruct((bsz, 4, s // 4, 3 * COL), BF16),
                   jax.ShapeDtypeStruct((bsz, 16, s // 16, 3 * COL), BF16)],
        scratch_shapes=[pltpu.VMEM((D_MODEL // 128, tm, 128), F32)],
        compiler_params=pltpu.CompilerParams(dimension_semantics=("parallel",), vmem_limit_bytes=VMEM_LIMIT),
        name="proj",
    )(x2, *([w_in_b] * 15), a_ln_g, a_ln_b, a_ws, a_bias)


def _attn_kernel(qkv_ref, o_ref, lse_ref, *, ns, seq):
    nb = seq // SPAN
    lane = lax.broadcasted_iota(I32, (SPAN, 128), 1)
    lo = lane < 64
    lane16 = lane // 16
    qi = lax.broadcasted_iota(I32, (SPAN, 2 * SPAN), 0)
    ki = lax.broadcasted_iota(I32, (SPAN, 2 * SPAN), 1)
    causal = lax.broadcasted_iota(I32, (SPAN, SPAN), 1) <= lax.broadcasted_iota(I32, (SPAN, SPAN), 0)
    bias_first = jnp.where(causal, 0.0, NEG).astype(F32)
    bias_first = jnp.concatenate([bias_first, bias_first], axis=0)
    bias_main = jnp.where((ki >= qi) & (ki <= qi + SPAN), 0.0, NEG).astype(F32)
    bias_main = jnp.concatenate([bias_main, bias_main], axis=0)
    zero = jnp.zeros((SPAN, 128), BF16)

    for s in range(ns):
        def block(row0, start, bias, s=s):
            win = bias.shape[1]
            pairs = range(B_HEADS // 2)
            scores, values = [], []
            for jp in pairs:
                c0 = jp * 128
                q = qkv_ref[s, pl.ds(row0, SPAN), c0:c0 + 128] * jnp.asarray(0.125, BF16)
                k = qkv_ref[s, pl.ds(start, win), COL + c0:COL + c0 + 128]
                values.append(qkv_ref[s, pl.ds(start, win), 2 * COL + c0:2 * COL + c0 + 128])
                qs = jnp.concatenate([jnp.where(lo, q, zero), jnp.where(lo, zero, q)], axis=0)
                scores.append(lax.dot_general(qs, k, (((1,), (1,)), ((), ())), preferred_element_type=F32) + bias)
            probs, maxes, sums = [], [], []
            for jp in pairs:
                m = jnp.max(scores[jp], axis=-1, keepdims=True)
                p = jnp.exp(scores[jp] - m)
                maxes.append(m)
                sums.append(jnp.sum(p, axis=-1, keepdims=True))
                probs.append(p.astype(BF16))
            lse_tile = jnp.zeros((SPAN, 128), F32)
            for jp in pairs:
                c0 = jp * 128
                ov = _dot(probs[jp], values[jp])
                inv = 1.0 / sums[jp]
                o = jnp.where(lo, ov[:SPAN] * inv[:SPAN], ov[SPAN:] * inv[SPAN:])
                o_ref[pl.ds(row0, SPAN), s * B_WIDTH + c0:s * B_WIDTH + c0 + 128] = o.astype(BF16)
                lse = maxes[jp] + jnp.log(sums[jp])
                lse_tile = jnp.where(lane16 == 2 * jp, lse[:SPAN],
                                     jnp.where(lane16 == 2 * jp + 1, lse[SPAN:], lse_tile))
            lse_ref[pl.ds(row0, SPAN), s * 128:(s + 1) * 128] = lse_tile

        block(0, 0, bias_first)
        if nb > 1:
            def body(i, carry):
                block(pl.multiple_of(i * SPAN, SPAN), pl.multiple_of((i - 1) * SPAN, SPAN), bias_main)
                return carry
            lax.fori_loop(1, nb, body, 0)


def _attn(qkv_g, gi):
    bsz, dl, seq, _ = qkv_g.shape
    ns = max(1, min(dl, TM_PROJ // seq))
    return pl.pallas_call(
        functools.partial(_attn_kernel, ns=ns, seq=seq),
        grid=(bsz, dl // ns),
        in_specs=[pl.BlockSpec((None, ns, seq, 3 * COL), lambda b, r: (b, r, 0, 0))],
        out_specs=[pl.BlockSpec((None, seq, ns * B_WIDTH), lambda b, r: (b, 0, r)),
                   pl.BlockSpec((None, seq, ns * 128), lambda b, r: (b, 0, r))],
        out_shape=[jax.ShapeDtypeStruct((bsz, seq, dl * B_WIDTH), BF16),
                   jax.ShapeDtypeStruct((bsz, seq, dl * 128), F32)],
        compiler_params=pltpu.CompilerParams(dimension_semantics=("parallel", "parallel"),
                                             vmem_limit_bytes=VMEM_LIMIT),
        name=f"attn{dl}",
    )(qkv_g)


def _natural_rows(ref, dl, scr):
    nchunk, tm, _ = scr.shape
    w = nchunk * 128
    per = tm // dl
    for r in range(dl):
        for c in range(nchunk):
            scr[c, pl.ds(r, per, stride=dl), :] = ref[:, r * w + c * 128:r * w + (c + 1) * 128].astype(F32)
    return jnp.concatenate([scr[c] for c in range(nchunk)], axis=1)


def _mix_kernel(ga_ref, gates_ref, o1_ref, o2_ref, o3_ref, l1_ref, l2_ref, l3_ref, x_ref,
                wa_ref, wb_ref, wo_ref, wr_ref, br_ref, g1_ref, b1_ref,
                x1_ref, x1p_ref, route_ref, routet_ref, o2s_ref, o3s_ref, l2s_ref, l3s_ref):
    tm = x_ref.shape[0]
    o2 = _natural_rows(o2_ref, 4, o2s_ref)
    o3 = _natural_rows(o3_ref, 16, o3s_ref)
    l2 = _natural_rows(l2_ref, 4, l2s_ref)
    l3 = _natural_rows(l3_ref, 16, l3s_ref)
    er = lax.broadcasted_iota(I32, (256, B_WIDTH), 0)
    ec = lax.broadcasted_iota(I32, (256, B_WIDTH), 1)
    expand = jnp.where(er % 128 == (ec // B_HEAD_DIM) * 16, 1.0, 0.0).astype(BF16)

    def widen(w):
        hi = w.astype(BF16)
        lo = (w - hi.astype(F32)).astype(BF16)
        return _dot(jnp.concatenate([hi, lo], axis=1), expand)

    h = tm // 2
    halves = (slice(0, h), slice(h, tm))
    obs = []
    for r in halves:
        l1 = l1_ref[r, :]
        mx = jnp.maximum(l1, jnp.maximum(l2[r], l3[r]))
        e1, e2, e3 = jnp.exp(l1 - mx), jnp.exp(l2[r] - mx), jnp.exp(l3[r] - mx)
        inv = 1.0 / (e1 + e2 + e3)
        obs.append(widen(e1 * inv) * o1_ref[r, :].astype(F32) + widen(e2 * inv) * o2[r] + widen(e3 * inv) * o3[r])
    ybs = [_dot(ob.astype(BF16), wb_ref[...]) for ob in obs]
    yas = [_dot(ga_ref[r, :], wa_ref[...]) for r in halves]
    pres = [gates_ref[r, :D_MODEL].astype(F32) * ya + gates_ref[r, D_MODEL:].astype(F32) * yb
            for r, ya, yb in zip(halves, yas, ybs)]
    mixes = [_dot(pre.astype(BF16), wo_ref[...]) for pre in pres]
    x1s = [_ln(DEEPNORM_ALPHA * x_ref[r, :] + mix, g1_ref[...], b1_ref[...]) for r, mix in zip(halves, mixes)]
    logits = [_dot(x1.astype(BF16), wr_ref[...]) + br_ref[...] for x1 in x1s]

    lane = lax.broadcasted_iota(I32, (h, 128), 1).astype(F32)
    big = 1e9
    for i, r in enumerate(halves):
        x1_ref[r, :] = x1s[i]
        _store_packed_rows(x1p_ref.at[i * h // 8:(i + 1) * h // 8], x1s[i])
        lg = logits[i]
        gl = jnp.where(lane < N_GROUPS, lg, NEG)
        gm = jnp.max(gl, axis=-1, keepdims=True)
        gidx = jnp.min(jnp.where(gl == gm, lane, big), axis=-1, keepdims=True)
        gsum = jnp.sum(jnp.where(lane < N_GROUPS, jnp.exp(gl - gm), 0.0), axis=-1, keepdims=True)
        gprob = 1.0 / gsum
        lo_lane = N_GROUPS + N_EXPERTS * gidx
        el = jnp.where((lane >= lo_lane) & (lane < lo_lane + N_EXPERTS), lg, NEG)
        v1 = jnp.max(el, axis=-1, keepdims=True)
        i1 = jnp.min(jnp.where(el == v1, lane, big), axis=-1, keepdims=True)
        el2 = jnp.where(lane == i1, NEG, el)
        v2 = jnp.max(el2, axis=-1, keepdims=True)
        i2 = jnp.min(jnp.where(el2 == v2, lane, big), axis=-1, keepdims=True)
        t = jnp.exp(v2 - v1)
        w1 = 1.0 / (1.0 + t)
        w2 = t * w1
        route = jnp.where(lane == 0, i1 - N_GROUPS,
                          jnp.where(lane == 1, i2 - N_GROUPS,
                                    jnp.where(lane == 2, gprob * w1,
                                              jnp.where(lane == 3, gprob * w2, 0.0))))
        route_ref[r, :] = route
        routet_ref[:, r] = route.T[:8, :]


def _mix(ga, gates, o1, o2, o3, l1, l2, l3, x2, wa, wb, wo, wr, br, g1, b1):
    n = x2.shape[0]
    bsz = o2.shape[0]
    tm = TM_MIX
    tiles = n // bsz // tm
    rows = lambda w: pl.BlockSpec((tm, w), lambda i: (i, 0))
    grouped = lambda a, dl: pl.BlockSpec((None, tm // dl, a.shape[2]), lambda i: (i // tiles, i % tiles, 0))
    full = lambda a: pl.BlockSpec(a.shape, lambda i: (0,) * a.ndim)
    return pl.pallas_call(
        _mix_kernel,
        grid=(n // tm,),
        in_specs=[rows(A_WIDTH), rows(2 * D_MODEL), rows(B_WIDTH), grouped(o2, 4), grouped(o3, 16),
                  rows(128), grouped(l2, 4), grouped(l3, 16), rows(D_MODEL),
                  full(wa), full(wb), full(wo), full(wr), full(br), full(g1), full(b1)],
        out_specs=[rows(D_MODEL), pl.BlockSpec((tm // 8, 32, 128), lambda i: (i, 0, 0)), rows(128),
                   pl.BlockSpec((8, tm), lambda i: (0, i))],
        out_shape=[jax.ShapeDtypeStruct((n, D_MODEL), F32),
                   jax.ShapeDtypeStruct((n // 8, 32, 128), U32),
                   jax.ShapeDtypeStruct((n, 128), F32),
                   jax.ShapeDtypeStruct((8, n), F32)],
        scratch_shapes=[pltpu.VMEM((B_WIDTH // 128, tm, 128), F32), pltpu.VMEM((B_WIDTH // 128, tm, 128), F32),
                        pltpu.VMEM((1, tm, 128), F32), pltpu.VMEM((1, tm, 128), F32)],
        compiler_params=pltpu.CompilerParams(dimension_semantics=("parallel",), vmem_limit_bytes=VMEM_LIMIT),
        name="mix",
    )(ga, gates, o1, o2, o3, l1, l2, l3, x2, wa, wb, wo, wr, br, g1, b1)


SC_WINDOW = 128


def _sc_mesh():
    return plsc.VectorSubcoreMesh(core_axis_name="core", subcore_axis_name="subcore")


def _sc_scatter_rows(rows, dst, n_out):
    r = rows.shape[0]
    m = dst.shape[0]
    nblk = r // SC_WINDOW

    @pl.kernel(out_type=jax.ShapeDtypeStruct((n_out, 128), rows.dtype), mesh=_sc_mesh())
    def scatter(rows_hbm, dst_hbm, out_hbm):
        def body(rows_vmem, dst_vmem):
            pltpu.sync_copy(rows_vmem, out_hbm.at[dst_vmem.at[0]])

        pltpu.emit_pipeline(
            body,
            grid=(m // SC_WINDOW,),
            in_specs=[pl.BlockSpec((SC_WINDOW, 128), lambda i: (i % nblk, 0)),
                      pl.BlockSpec((1, SC_WINDOW), lambda i: (0, i))],
            out_specs=[],
            core_axis_name=("core", "subcore"),
            dimension_semantics=(pltpu.PARALLEL,),
        )(rows_hbm, dst_hbm)

    return scatter(rows, dst.reshape(1, m))


def _sc_gather_rows(table, src):
    m = src.shape[0]

    @pl.kernel(out_type=jax.ShapeDtypeStruct((m, 128), table.dtype), mesh=_sc_mesh())
    def gather(table_hbm, src_hbm, out_hbm):
        def body(src_vmem, out_vmem):
            pltpu.sync_copy(table_hbm.at[src_vmem.at[0]], out_vmem)

        pltpu.emit_pipeline(
            body,
            grid=(m // SC_WINDOW,),
            in_specs=[pl.BlockSpec((1, SC_WINDOW), lambda i: (0, i))],
            out_specs=[pl.BlockSpec((SC_WINDOW, 128), lambda i: (i, 0))],
            core_axis_name=("core", "subcore"),
            dimension_semantics=(pltpu.PARALLEL,),
        )(src_hbm, out_hbm)

    return gather(table, src.reshape(1, m))


def _moe_kernel(te_ref, tv_ref, tb_ref, xs_ref, wg_ref, wu_ref, wd_ref, after_ref, ys_ref,
                wgb_ref, wub_ref, wdb_ref):
    del tb_ref, after_ref
    t = pl.program_id(0)

    @pl.when((t == 0) | (te_ref[t] != te_ref[jnp.maximum(t - 1, 0)]))
    def _():
        wgb_ref[...] = wg_ref[...].astype(BF16)
        wub_ref[...] = wu_ref[...].astype(BF16)
        wdb_ref[...] = wd_ref[...].astype(BF16)

    @pl.when(tv_ref[t] == 1)
    def _():
        xb = _load_packed_rows(xs_ref).astype(BF16)
        g = _dot(xb, wgb_ref[...])
        u = _dot(xb, wub_ref[...])
        h = (g * _sigmoid(g) * u).astype(BF16)
        _store_packed_rows(ys_ref, _dot(h, wdb_ref[...]))


def _moe(tile_expert, tile_valid, tile_block, xs, wg, wu, wd, after):
    tm = TM_MOE
    nt = tile_expert.shape[0]
    rows = pl.BlockSpec((tm // 8, 32, 128), lambda t, te, tv, tb: (tb[t], 0, 0))
    return pl.pallas_call(
        _moe_kernel,
        grid_spec=pltpu.PrefetchScalarGridSpec(
            num_scalar_prefetch=3,
            grid=(nt,),
            in_specs=[rows,
                      pl.BlockSpec((None, D_MODEL, D_EXPERT), lambda t, te, tv, tb: (te[t], 0, 0)),
                      pl.BlockSpec((None, D_MODEL, D_EXPERT), lambda t, te, tv, tb: (te[t], 0, 0)),
                      pl.BlockSpec((None, D_EXPERT, D_MODEL), lambda t, te, tv, tb: (te[t], 0, 0)),
                      pl.BlockSpec(memory_space=pl.ANY)],
            out_specs=rows,
            scratch_shapes=[pltpu.VMEM((D_MODEL, D_EXPERT), BF16), pltpu.VMEM((D_MODEL, D_EXPERT), BF16),
                            pltpu.VMEM((D_EXPERT, D_MODEL), BF16)]),
        out_shape=jax.ShapeDtypeStruct((nt * tm // 8, 32, 128), U32),
        compiler_params=pltpu.CompilerParams(dimension_semantics=("arbitrary",), vmem_limit_bytes=VMEM_LIMIT),
        name="moe",
    )(tile_expert, tile_valid, tile_block, xs, wg, wu, wd, after)


def _ple_kernel(x1_ref, p_ref, wple_ref, wpg_ref, ple_ref):
    tm = x1_ref.shape[0]
    h = tm // 2
    halves = (slice(0, h), slice(h, tm))
    gates = [_dot(x1_ref[r, :].astype(BF16), wpg_ref[...]) for r in halves]
    plins = [_dot(p_ref[r, :].astype(BF16), wple_ref[...]) for r in halves]
    for i, r in enumerate(halves):
        ple_ref[r, :] = (plins[i] * _sigmoid(gates[i])).astype(BF16)


def _ple(x1, p2, wple, wpg):
    n = x1.shape[0]
    tm = TM_FIN
    rows = lambda w: pl.BlockSpec((tm, w), lambda t: (t, 0))
    full = lambda a: pl.BlockSpec(a.shape, lambda t: (0,) * a.ndim)
    return pl.pallas_call(
        _ple_kernel,
        grid=(n // tm,),
        in_specs=[rows(D_MODEL), rows(PLE_DIM), full(wple), full(wpg)],
        out_specs=rows(D_MODEL),
        out_shape=jax.ShapeDtypeStruct((n, D_MODEL), BF16),
        compiler_params=pltpu.CompilerParams(dimension_semantics=("parallel",), vmem_limit_bytes=VMEM_LIMIT),
        name="ple",
    )(x1, p2, wple, wpg)


def _final_kernel(y0_ref, y1_ref, x1_ref, ple_ref, route_ref, g2_ref, b2_ref, *rest):
    out_ref = rest[-1]
    ffn = route_ref[:, 2:3] * _load_packed_rows(y0_ref) + route_ref[:, 3:4] * _load_packed_rows(y1_ref)
    out_ref[...] = _ln(DEEPNORM_ALPHA * x1_ref[...] + ffn + ple_ref[...].astype(F32), g2_ref[...], b2_ref[...])


def _final(yg, x1, ple, route, g2, b2, part, prev):
    n = x1.shape[0]
    tm = TM_FIN
    nt = n // COMBINE_PARTS // tm
    off = part * nt
    rows = lambda w: pl.BlockSpec((tm, w), lambda t: (t + off, 0))
    full = lambda a: pl.BlockSpec(a.shape, lambda t: (0,) * a.ndim)
    in_specs = [pl.BlockSpec((tm // 8, 32, 128), lambda t: (t, 0, 0)),
                pl.BlockSpec((tm // 8, 32, 128), lambda t: (t + nt, 0, 0)),
                rows(D_MODEL), rows(D_MODEL), rows(128), full(g2), full(b2)]
    args = [yg, yg, x1, ple, route, g2, b2]
    aliases = {}
    if prev is not None:
        in_specs.append(pl.BlockSpec(memory_space=pl.ANY))
        args.append(prev)
        aliases = {len(args) - 1: 0}
    return pl.pallas_call(
        _final_kernel,
        grid=(nt,),
        in_specs=in_specs,
        out_specs=rows(D_MODEL),
        out_shape=jax.ShapeDtypeStruct((n, D_MODEL), F32),
        input_output_aliases=aliases,
        compiler_params=pltpu.CompilerParams(dimension_semantics=("parallel",), vmem_limit_bytes=VMEM_LIMIT),
        name=f"final{part}",
    )(*args)


def _routing_tables(routet, n):
    tm = TM_MOE
    nt = (2 * n) // tm + N_EXPERTS_TOTAL
    ef = routet[0:2].astype(I32).reshape(-1)
    onehot = (ef[:, None] == jnp.arange(N_EXPERTS_TOTAL, dtype=I32)[None, :]).astype(I32)
    csum = jnp.cumsum(onehot, axis=0)
    counts = csum[-1]
    rank = jnp.sum(csum * onehot, axis=1) - 1
    padded = ((counts + tm - 1) // tm) * tm
    ends = jnp.cumsum(padded)
    offs = ends - padded
    pos = jnp.sum(onehot * offs[None, :], axis=1) + rank
    tile_start = jnp.arange(nt, dtype=I32) * tm
    tile_expert = jnp.minimum(jnp.sum((tile_start[:, None] >= ends[None, :]).astype(I32), axis=1),
                              N_EXPERTS_TOTAL - 1).astype(I32)
    tile_valid = (tile_start < ends[-1]).astype(I32)
    tile_block = jnp.minimum(jnp.arange(nt, dtype=I32), ends[-1] // tm - 1)
    pr = pos.reshape(2 * n // 32, 32)
    pp = jnp.concatenate([jnp.tile(pr[:, 8 * a:8 * (a + 1)], (1, SUBROWS)) for a in range(4)], axis=1)
    jv = (jnp.arange(128, dtype=I32) % 32) // 8
    piece = (pp // 8) * (8 * SUBROWS) + pp % 8 + 8 * jv[None, :]
    return tile_expert, tile_valid, tile_block, piece.reshape(2, n * SUBROWS)


def kernel(x, p, w_in, a_ln_g, a_ln_b, a_ws, a_bs, w_a_proj, w_b_proj, w_o, ln1_g, ln1_b, w_group_router,
           b_group_router, w_expert_router, b_expert_router, w_gate, w_up, w_down, w_ple, w_ple_gate,
           ln2_g, ln2_b):
    bsz, s, d = x.shape
    n = bsz * s
    assert d == D_MODEL and s % (SPAN * max(B_DILATIONS)) == 0 and n % TM_PROJ == 0
    assert w_in.shape[0] == 1, "one layer"

    w_in_b = w_in[0].astype(BF16)
    a_bias = jnp.repeat(a_bs[0].T, A_WIDTH // 8, axis=1)

    ga, gates, qkv1, qkv2, qkv3 = _proj(x, w_in_b, a_ln_g, a_ln_b, a_ws[0], a_bias)
    o1, l1 = _attn(qkv1.reshape(bsz, 1, s, 3 * COL), 0)
    o2, l2 = _attn(qkv2, 1)
    o3, l3 = _attn(qkv3, 2)

    pad = 128 - N_GROUPS - N_EXPERTS_TOTAL
    wr = jnp.concatenate([w_group_router[0], w_expert_router[0].reshape(d, N_EXPERTS_TOTAL),
                          jnp.zeros((d, pad), F32)], axis=1).astype(BF16)
    br = jnp.concatenate([b_group_router[0], b_expert_router[0].reshape(-1), jnp.zeros((pad,), F32)])[None, :]
    x1, x1p, route, routet = _mix(
        ga, gates, o1.reshape(n, B_WIDTH), o2, o3, l1.reshape(n, 128), l2, l3, x.reshape(n, d),
        w_a_proj[0].astype(BF16), w_b_proj[0].astype(BF16), w_o[0].astype(BF16), wr, br, ln1_g, ln1_b)

    tile_expert, tile_valid, tile_block, piece = _routing_tables(routet, n)
    nt = tile_expert.shape[0]
    xs = _sc_scatter_rows(x1p.reshape(n * SUBROWS, 128), piece.reshape(-1), nt * TM_MOE * SUBROWS)
    ple = _ple(x1, p[0].reshape(n, PLE_DIM), w_ple[0].astype(BF16), w_ple_gate[0].astype(BF16))
    ys = _moe(tile_expert, tile_valid, tile_block, xs.reshape(nt * TM_MOE // 8, 32, 128),
              w_gate[0].reshape(N_EXPERTS_TOTAL, d, D_EXPERT), w_up[0].reshape(N_EXPERTS_TOTAL, d, D_EXPERT),
              w_down[0].reshape(N_EXPERTS_TOTAL, D_EXPERT, d), ple).reshape(nt * TM_MOE * SUBROWS, 128)
    out = None
    pp = n * SUBROWS // COMBINE_PARTS
    for part in range(COMBINE_PARTS):
        yg = _sc_gather_rows(ys, piece[:, part * pp:(part + 1) * pp].reshape(-1))
        out = _final(yg.reshape(2 * n // COMBINE_PARTS // 8, 32, 128), x1, ple, route, ln2_g, ln2_b, part, out)
    return out.reshape(bsz, s, d)
```

```python
import functools

import jax
import jax.numpy as jnp
from jax import lax
from jax.experimental import pallas as pl
from jax.experimental.pallas import tpu as pltpu
from jax.experimental.pallas import tpu_sc as plsc

F32 = jnp.float32
BF16 = jnp.bfloat16
U32 = jnp.uint32
I32 = jnp.int32

D_MODEL = 1024
PLE_DIM = 256
A_WIDTH = 512
A_CHUNK = 128
B_HEAD_DIM = 64
B_HEADS = 8
B_WIDTH = 512
B_DILATIONS = (1, 4, 16)
SPAN = 128
N_GROUPS = 4
N_EXPERTS = 8
N_EXPERTS_TOTAL = N_GROUPS * N_EXPERTS
D_EXPERT = 256
DEEPNORM_ALPHA = 2.0 ** 0.25
LN_EPS = 1e-5
COL = 512
NEG = -1e30

VMEM_LIMIT = 56 * 1024 * 1024

TM_PROJ = 512
TM_MIX = 512
TM_MOE = 512
TM_FIN = 512


def _ln(x, g, b):
    mu = jnp.mean(x, axis=-1, keepdims=True)
    xc = x - mu
    var = jnp.mean(xc * xc, axis=-1, keepdims=True)
    return xc * lax.rsqrt(var + LN_EPS) * g + b


def _gelu_tanh(x):
    return 0.5 * x * (1.0 + jnp.tanh(0.7978845608028654 * (x + 0.044715 * (x * x * x))))


def _sigmoid(x):
    return 0.5 * jnp.tanh(0.5 * x) + 0.5


def _dot(a, b):
    return jnp.dot(a, b, preferred_element_type=F32)


PACK_W = D_MODEL // 2
SUBROWS = PACK_W // 128


def _store_packed_rows(ref, x):
    m = x.shape[0]
    xb = x.astype(BF16).astype(F32)
    lo = pltpu.bitcast(xb[:, :PACK_W], U32) >> 16
    hi = pltpu.bitcast(xb[:, PACK_W:], U32) & jnp.uint32(0xFFFF0000)
    w = hi | lo
    for j in range(SUBROWS):
        ref[:, 8 * j:8 * (j + 1), :] = w[:, 128 * j:128 * (j + 1)].reshape(m // 8, 8, 128)


def _load_packed_rows(ref):
    m = ref.shape[0] * 8
    ws = [ref[:, 8 * j:8 * (j + 1), :].reshape(m, 128) for j in range(SUBROWS)]
    lo = [pltpu.bitcast(w << 16, F32) for w in ws]
    hi = [pltpu.bitcast(w & jnp.uint32(0xFFFF0000), F32) for w in ws]
    return jnp.concatenate(lo + hi, axis=1)


def _proj_kernel(x_ref, *refs):
    w = refs[:15]
    lng_ref, lnb_ref, ws_ref, bias_ref = refs[15:19]
    ga_ref, gates_ref, qkv1_ref, qkv2_ref, qkv3_ref = refs[19:24]
    xc_ref = refs[24]
    tm = x_ref.shape[0]
    xb = x_ref[...].astype(BF16)

    u = _gelu_tanh(_dot(xb, w[0][...]))
    v = _gelu_tanh(_dot(xb, w[1][...]))
    vn = _ln(v, lng_ref[...], lnb_ref[...]).astype(BF16)

    row = lax.broadcasted_iota(I32, (A_CHUNK, A_CHUNK), 0)
    colm = lax.broadcasted_iota(I32, (A_CHUNK, A_CHUNK), 1)
    causal = colm <= row
    lo = colm < 64
    zero = jnp.zeros((A_CHUNK, A_CHUNK), BF16)
    wcat = []
    for j in range(4):
        w0 = jnp.where(causal, ws_ref[2 * j], 0.0).astype(BF16)
        w1 = jnp.where(causal, ws_ref[2 * j + 1], 0.0).astype(BF16)
        wcat.append(jnp.concatenate([w0, w1], axis=1))
    for c in range(tm // A_CHUNK):
        r0 = c * A_CHUNK
        for j in range(4):
            c0 = j * 128
            vt = vn[r0:r0 + A_CHUNK, c0:c0 + 128]
            rhs = jnp.concatenate([jnp.where(lo, vt, zero), jnp.where(lo, zero, vt)], axis=0)
            mixed = _dot(wcat[j], rhs) + bias_ref[:, c0:c0 + 128]
            ga_ref[r0:r0 + A_CHUNK, c0:c0 + 128] = (u[r0:r0 + A_CHUNK, c0:c0 + 128] * mixed).astype(BF16)

    for i in range(4):
        gates_ref[:, i * COL:(i + 1) * COL] = _sigmoid(_dot(xb, w[11 + i][...])).astype(BF16)
    for j in range(3):
        qkv1_ref[:, j * COL:(j + 1) * COL] = _dot(xb, w[2 + 3 * j][...]).astype(BF16)

    for c in range(D_MODEL // 128):
        xc_ref[c] = x_ref[:, c * 128:(c + 1) * 128]
    for gi, out_ref in ((1, qkv2_ref), (2, qkv3_ref)):
        dl = B_DILATIONS[gi]
        per = tm // dl
        xp = jnp.concatenate(
            [jnp.concatenate([xc_ref[c, pl.ds(r, per, stride=dl), :] for c in range(D_MODEL // 128)], axis=1)
             for r in range(dl)], axis=0).astype(BF16)
        for j in range(3):
            res = _dot(xp, w[2 + 3 * j + gi][...]).astype(BF16)
            for r in range(dl):
                out_ref[r, :, j * COL:(j + 1) * COL] = res[r * per:(r + 1) * per]


def _proj(x, w_in_b, a_ln_g, a_ln_b, a_ws, a_bias):
    bsz, s, _ = x.shape
    n = bsz * s
    tm = TM_PROJ
    tiles = s // tm
    x2 = x.reshape(n, D_MODEL)
    wspec = lambda j: pl.BlockSpec((D_MODEL, COL), lambda i, j=j: (0, j), pipeline_mode=pl.Buffered(1))
    full = lambda shape: pl.BlockSpec(shape, lambda i: (0,) * len(shape))
    rows = lambda width: pl.BlockSpec((tm, width), lambda i: (i, 0))
    dil = lambda dl: pl.BlockSpec((None, dl, tm // dl, 3 * COL), lambda i: (i // tiles, 0, i % tiles, 0))
    return pl.pallas_call(
        _proj_kernel,
        grid=(n // tm,),
        in_specs=[rows(D_MODEL)] + [wspec(j) for j in range(15)]
                 + [full((1, A_WIDTH)), full((1, A_WIDTH)), full((8, A_CHUNK, A_CHUNK)), full((A_CHUNK, A_WIDTH))],
        out_specs=[rows(A_WIDTH), rows(4 * COL), rows(3 * COL), dil(4), dil(16)],
        out_shape=[jax.ShapeDtypeStruct((n, A_WIDTH), BF16),
                   jax.ShapeDtypeStruct((n, 4 * COL), BF16),
                   jax.ShapeDtypeStruct((n, 3 * COL), BF16),
                   jax.ShapeDtypeStruct((bsz, 4, s // 4, 3 * COL), BF16),
                   jax.ShapeDtypeStruct((bsz, 16, s // 16, 3 * COL), BF16)],
        scratch_shapes=[pltpu.VMEM((D_MODEL // 128, tm, 128), F32)],
        compiler_params=pltpu.CompilerParams(dimension_semantics=("parallel",), vmem_limit_bytes=VMEM_LIMIT),
        name="proj",
    )(x2, *([w_in_b] * 15), a_ln_g, a_ln_b, a_ws, a_bias)


def _attn_kernel(qkv_ref, o_ref, lse_ref, *, ns, seq):
    nb = seq // SPAN
    lane = lax.broadcasted_iota(I32, (SPAN, 128), 1)
    lo = lane < 64
    lane16 = lane // 16
    qi = lax.broadcasted_iota(I32, (SPAN, 2 * SPAN), 0)
    ki = lax.broadcasted_iota(I32, (SPAN, 2 * SPAN), 1)
    causal = lax.broadcasted_iota(I32, (SPAN, SPAN), 1) <= lax.broadcasted_iota(I32, (SPAN, SPAN), 0)
    bias_first = jnp.where(causal, 0.0, NEG).astype(F32)
    bias_first = jnp.concatenate([bias_first, bias_first], axis=0)
    bias_main = jnp.where((ki >= qi) & (ki <= qi + SPAN), 0.0, NEG).astype(F32)
    bias_main = jnp.concatenate([bias_main, bias_main], axis=0)
    zero = jnp.zeros((SPAN, 128), BF16)

    for s in range(ns):
        def block(row0, start, bias, s=s):
            win = bias.shape[1]
            pairs = range(B_HEADS // 2)
            scores, values = [], []
            for jp in pairs:
                c0 = jp * 128
                q = qkv_ref[s, pl.ds(row0, SPAN), c0:c0 + 128] * jnp.asarray(0.125, BF16)
                k = qkv_ref[s, pl.ds(start, win), COL + c0:COL + c0 + 128]
                values.append(qkv_ref[s, pl.ds(start, win), 2 * COL + c0:2 * COL + c0 + 128])
                qs = jnp.concatenate([jnp.where(lo, q, zero), jnp.where(lo, zero, q)], axis=0)
                scores.append(lax.dot_general(qs, k, (((1,), (1,)), ((), ())), preferred_element_type=F32) + bias)
            probs, maxes, sums = [], [], []
            for jp in pairs:
                m = jnp.max(scores[jp], axis=-1, keepdims=True)
                p = jnp.exp(scores[jp] - m)
                maxes.append(m)
                sums.append(jnp.sum(p, axis=-1, keepdims=True))
                probs.append(p.astype(BF16))
            lse_tile = jnp.zeros((SPAN, 128), F32)
            for jp in pairs:
                c0 = jp * 128
                ov = _dot(probs[jp], values[jp])
                inv = 1.0 / sums[jp]
                o = jnp.where(lo, ov[:SPAN] * inv[:SPAN], ov[SPAN:] * inv[SPAN:])
                o_ref[pl.ds(row0, SPAN), s * B_WIDTH + c0:s * B_WIDTH + c0 + 128] = o.astype(BF16)
                lse = maxes[jp] + jnp.log(sums[jp])
                lse_tile = jnp.where(lane16 == 2 * jp, lse[:SPAN],
                                     jnp.where(lane16 == 2 * jp + 1, lse[SPAN:], lse_tile))
            lse_ref[pl.ds(row0, SPAN), s * 128:(s + 1) * 128] = lse_tile

        block(0, 0, bias_first)
        if nb > 1:
            def body(i, carry):
                block(pl.multiple_of(i * SPAN, SPAN), pl.multiple_of((i - 1) * SPAN, SPAN), bias_main)
                return carry
            lax.fori_loop(1, nb, body, 0)


def _attn(qkv_g, gi):
    bsz, dl, seq, _ = qkv_g.shape
    ns = max(1, min(dl, TM_PROJ // seq))
    return pl.pallas_call(
        functools.partial(_attn_kernel, ns=ns, seq=seq),
        grid=(bsz, dl // ns),
        in_specs=[pl.BlockSpec((None, ns, seq, 3 * COL), lambda b, r: (b, r, 0, 0))],
        out_specs=[pl.BlockSpec((None, seq, ns * B_WIDTH), lambda b, r: (b, 0, r)),
                   pl.BlockSpec((None, seq, ns * 128), lambda b, r: (b, 0, r))],
        out_shape=[jax.ShapeDtypeStruct((bsz, seq, dl * B_WIDTH), BF16),
                   jax.ShapeDtypeStruct((bsz, seq, dl * 128), F32)],
        compiler_params=pltpu.CompilerParams(dimension_semantics=("parallel", "parallel"),
                                             vmem_limit_bytes=VMEM_LIMIT),
        name=f"attn{dl}",
    )(qkv_g)


def _natural_rows(ref, dl, scr):
    nchunk, tm, _ = scr.shape
    w = nchunk * 128
    per = tm // dl
    for r in range(dl):
        for c in range(nchunk):
            scr[c, pl.ds(r, per, stride=dl), :] = ref[:, r * w + c * 128:r * w + (c + 1) * 128].astype(F32)
    return jnp.concatenate([scr[c] for c in range(nchunk)], axis=1)


def _mix_kernel(ga_ref, gates_ref, o1_ref, o2_ref, o3_ref, l1_ref, l2_ref, l3_ref, x_ref,
                wa_ref, wb_ref, wo_ref, wr_ref, br_ref, g1_ref, b1_ref,
                x1_ref, x1p_ref, route_ref, routet_ref, o2s_ref, o3s_ref, l2s_ref, l3s_ref):
    tm = x_ref.shape[0]
    o2 = _natural_rows(o2_ref, 4, o2s_ref)
    o3 = _natural_rows(o3_ref, 16, o3s_ref)
    l2 = _natural_rows(l2_ref, 4, l2s_ref)
    l3 = _natural_rows(l3_ref, 16, l3s_ref)
    er = lax.broadcasted_iota(I32, (256, B_WIDTH), 0)
    ec = lax.broadcasted_iota(I32, (256, B_WIDTH), 1)
    expand = jnp.where(er % 128 == (ec // B_HEAD_DIM) * 16, 1.0, 0.0).astype(BF16)

    def widen(w):
        hi = w.astype(BF16)
        lo = (w - hi.astype(F32)).astype(BF16)
        return _dot(jnp.concatenate([hi, lo], axis=1), expand)

    h = tm // 2
    halves = (slice(0, h), slice(h, tm))
    obs = []
    for r in halves:
        l1 = l1_ref[r, :]
        mx = jnp.maximum(l1, jnp.maximum(l2[r], l3[r]))
        e1, e2, e3 = jnp.exp(l1 - mx), jnp.exp(l2[r] - mx), jnp.exp(l3[r] - mx)
        inv = 1.0 / (e1 + e2 + e3)
        obs.append(widen(e1 * inv) * o1_ref[r, :].astype(F32) + widen(e2 * inv) * o2[r] + widen(e3 * inv) * o3[r])
    ybs = [_dot(ob.astype(BF16), wb_ref[...]) for ob in obs]
    yas = [_dot(ga_ref[r, :], wa_ref[...]) for r in halves]
    pres = [gates_ref[r, :D_MODEL].astype(F32) * ya + gates_ref[r, D_MODEL:].astype(F32) * yb
            for r, ya, yb in zip(halves, yas, ybs)]
    mixes = [_dot(pre.astype(BF16), wo_ref[...]) for pre in pres]
    x1s = [_ln(DEEPNORM_ALPHA * x_ref[r, :] + mix, g1_ref[...], b1_ref[...]) for r, mix in zip(halves, mixes)]
    logits = [_dot(x1.astype(BF16), wr_ref[...]) + br_ref[...] for x1 in x1s]

    lane = lax.broadcasted_iota(I32, (h, 128), 1).astype(F32)
    big = 1e9
    for i, r in enumerate(halves):
        x1_ref[r, :] = x1s[i]
        _store_packed_rows(x1p_ref.at[i * h // 8:(i + 1) * h // 8], x1s[i])
        lg = logits[i]
        gl = jnp.where(lane < N_GROUPS, lg, NEG)
        gm = jnp.max(gl, axis=-1, keepdims=True)
        gidx = jnp.min(jnp.where(gl == gm, lane, big), axis=-1, keepdims=True)
        gsum = jnp.sum(jnp.where(lane < N_GROUPS, jnp.exp(gl - gm), 0.0), axis=-1, keepdims=True)
        gprob = 1.0 / gsum
        lo_lane = N_GROUPS + N_EXPERTS * gidx
        el = jnp.where((lane >= lo_lane) & (lane < lo_lane + N_EXPERTS), lg, NEG)
        v1 = jnp.max(el, axis=-1, keepdims=True)
        i1 = jnp.min(jnp.where(el == v1, lane, big), axis=-1, keepdims=True)
        el2 = jnp.where(lane == i1, NEG, el)
        v2 = jnp.max(el2, axis=-1, keepdims=True)
        i2 = jnp.min(jnp.where(el2 == v2, lane, big), axis=-1, keepdims=True)
        t = jnp.exp(v2 - v1)
        w1 = 1.0 / (1.0 + t)
        w2 = t * w1
        route = jnp.where(lane == 0, i1 - N_GROUPS,
                          jnp.where(lane == 1, i2 - N_GROUPS,
                                    jnp.where(lane == 2, gprob * w1,
                                              jnp.where(lane == 3, gprob * w2, 0.0))))
        route_ref[r, :] = route
        routet_ref[:, r] = route.T[:8, :]


def _mix(ga, gates, o1, o2, o3, l1, l2, l3, x2, wa, wb, wo, wr, br, g1, b1):
    n = x2.shape[0]
    bsz = o2.shape[0]
    tm = TM_MIX
    tiles = n // bsz // tm
    rows = lambda w: pl.BlockSpec((tm, w), lambda i: (i, 0))
    grouped = lambda a, dl: pl.BlockSpec((None, tm // dl, a.shape[2]), lambda i: (i // tiles, i % tiles, 0))
    full = lambda a: pl.BlockSpec(a.shape, lambda i: (0,) * a.ndim)
    return pl.pallas_call(
        _mix_kernel,
        grid=(n // tm,),
        in_specs=[rows(A_WIDTH), rows(2 * D_MODEL), rows(B_WIDTH), grouped(o2, 4), grouped(o3, 16),
                  rows(128), grouped(l2, 4), grouped(l3, 16), rows(D_MODEL),
                  full(wa), full(wb), full(wo), full(wr), full(br), full(g1), full(b1)],
        out_specs=[rows(D_MODEL), pl.BlockSpec((tm // 8, 32, 128), lambda i: (i, 0, 0)), rows(128),
                   pl.BlockSpec((8, tm), lambda i: (0, i))],
        out_shape=[jax.ShapeDtypeStruct((n, D_MODEL), F32),
                   jax.ShapeDtypeStruct((n // 8, 32, 128), U32),
                   jax.ShapeDtypeStruct((n, 128), F32),
                   jax.ShapeDtypeStruct((8, n), F32)],
        scratch_shapes=[pltpu.VMEM((B_WIDTH // 128, tm, 128), F32), pltpu.VMEM((B_WIDTH // 128, tm, 128), F32),
                        pltpu.VMEM((1, tm, 128), F32), pltpu.VMEM((1, tm, 128), F32)],
        compiler_params=pltpu.CompilerParams(dimension_semantics=("parallel",), vmem_limit_bytes=VMEM_LIMIT),
        name="mix",
    )(ga, gates, o1, o2, o3, l1, l2, l3, x2, wa, wb, wo, wr, br, g1, b1)


SC_WINDOW = 128


def _sc_mesh():
    return plsc.VectorSubcoreMesh(core_axis_name="core", subcore_axis_name="subcore")


def _sc_scatter_rows(rows, dst, n_out):
    r = rows.shape[0]
    m = dst.shape[0]
    nblk = r // SC_WINDOW

    @pl.kernel(out_type=jax.ShapeDtypeStruct((n_out, 128), rows.dtype), mesh=_sc_mesh())
    def scatter(rows_hbm, dst_hbm, out_hbm):
        def body(rows_vmem, dst_vmem):
            pltpu.sync_copy(rows_vmem, out_hbm.at[dst_vmem.at[0]])

        pltpu.emit_pipeline(
            body,
            grid=(m // SC_WINDOW,),
            in_specs=[pl.BlockSpec((SC_WINDOW, 128), lambda i: (i % nblk, 0)),
                      pl.BlockSpec((1, SC_WINDOW), lambda i: (0, i))],
            out_specs=[],
            core_axis_name=("core", "subcore"),
            dimension_semantics=(pltpu.PARALLEL,),
        )(rows_hbm, dst_hbm)

    return scatter(rows, dst.reshape(1, m))


def _sc_gather_rows(table, src):
    m = src.shape[0]

    @pl.kernel(out_type=jax.ShapeDtypeStruct((m, 128), table.dtype), mesh=_sc_mesh())
    def gather(table_hbm, src_hbm, out_hbm):
        def body(src_vmem, out_vmem):
            pltpu.sync_copy(table_hbm.at[src_vmem.at[0]], out_vmem)

        pltpu.emit_pipeline(
            body,
            grid=(m // SC_WINDOW,),
            in_specs=[pl.BlockSpec((1, SC_WINDOW), lambda i: (0, i))],
            out_specs=[pl.BlockSpec((SC_WINDOW, 128), lambda i: (i, 0))],
            core_axis_name=("core", "subcore"),
            dimension_semantics=(pltpu.PARALLEL,),
        )(src_hbm, out_hbm)

    return gather(table, src.reshape(1, m))


def _moe_kernel(te_ref, tv_ref, tb_ref, xs_ref, wg_ref, wu_ref, wd_ref, ys_ref, wgb_ref, wub_ref, wdb_ref):
    del tb_ref
    t = pl.program_id(0)

    @pl.when((t == 0) | (te_ref[t] != te_ref[jnp.maximum(t - 1, 0)]))
    def _():
        wgb_ref[...] = wg_ref[...].astype(BF16)
        wub_ref[...] = wu_ref[...].astype(BF16)
        wdb_ref[...] = wd_ref[...].astype(BF16)

    @pl.when(tv_ref[t] == 1)
    def _():
        xb = _load_packed_rows(xs_ref).astype(BF16)
        g = _dot(xb, wgb_ref[...])
        u = _dot(xb, wub_ref[...])
        h = (g * _sigmoid(g) * u).astype(BF16)
        _store_packed_rows(ys_ref, _dot(h, wdb_ref[...]))


def _moe(tile_expert, tile_valid, tile_block, xs, wg, wu, wd):
    tm = TM_MOE
    nt = tile_expert.shape[0]
    rows = pl.BlockSpec((tm // 8, 32, 128), lambda t, te, tv, tb: (tb[t], 0, 0))
    return pl.pallas_call(
        _moe_kernel,
        grid_spec=pltpu.PrefetchScalarGridSpec(
            num_scalar_prefetch=3,
            grid=(nt,),
            in_specs=[rows,
                      pl.BlockSpec((None, D_MODEL, D_EXPERT), lambda t, te, tv, tb: (te[t], 0, 0)),
                      pl.BlockSpec((None, D_MODEL, D_EXPERT), lambda t, te, tv, tb: (te[t], 0, 0)),
                      pl.BlockSpec((None, D_EXPERT, D_MODEL), lambda t, te, tv, tb: (te[t], 0, 0))],
            out_specs=rows,
            scratch_shapes=[pltpu.VMEM((D_MODEL, D_EXPERT), BF16), pltpu.VMEM((D_MODEL, D_EXPERT), BF16),
                            pltpu.VMEM((D_EXPERT, D_MODEL), BF16)]),
        out_shape=jax.ShapeDtypeStruct((nt * tm // 8, 32, 128), U32),
        compiler_params=pltpu.CompilerParams(dimension_semantics=("arbitrary",), vmem_limit_bytes=VMEM_LIMIT),
        name="moe",
    )(tile_expert, tile_valid, tile_block, xs, wg, wu, wd)


def _final_kernel(y0_ref, y1_ref, x1_ref, p_ref, route_ref, wple_ref, wpg_ref, g2_ref, b2_ref, *rest):
    out_ref = rest[-1]
    tm = x1_ref.shape[0]
    h = tm // 2
    halves = (slice(0, h), slice(h, tm))
    x1s = [x1_ref[r, :] for r in halves]
    gates = [_dot(x1.astype(BF16), wpg_ref[...]) for x1 in x1s]
    plins = [_dot(p_ref[r, :].astype(BF16), wple_ref[...]) for r in halves]
    for i, r in enumerate(halves):
        g8 = slice(i * h // 8, (i + 1) * h // 8)
        ffn = (route_ref[r, 2:3] * _load_packed_rows(y0_ref.at[g8])
               + route_ref[r, 3:4] * _load_packed_rows(y1_ref.at[g8]))
        ple = plins[i] * _sigmoid(gates[i])
        out_ref[r, :] = _ln(DEEPNORM_ALPHA * x1s[i] + ffn + ple, g2_ref[...], b2_ref[...])


def _final(yg, x1, p2, route, wple, wpg, g2, b2, half, prev):
    n = x1.shape[0]
    tm = TM_FIN
    nt = n // 2 // tm
    off = half * nt
    rows = lambda w: pl.BlockSpec((tm, w), lambda t: (t + off, 0))
    full = lambda a: pl.BlockSpec(a.shape, lambda t: (0,) * a.ndim)
    in_specs = [pl.BlockSpec((tm // 8, 32, 128), lambda t: (t, 0, 0)),
                pl.BlockSpec((tm // 8, 32, 128), lambda t: (t + nt, 0, 0)),
                rows(D_MODEL), rows(PLE_DIM), rows(128), full(wple), full(wpg), full(g2), full(b2)]
    args = [yg, yg, x1, p2, route, wple, wpg, g2, b2]
    aliases = {}
    if prev is not None:
        in_specs.append(pl.BlockSpec(memory_space=pl.ANY))
        args.append(prev)
        aliases = {len(args) - 1: 0}
    return pl.pallas_call(
        _final_kernel,
        grid=(nt,),
        in_specs=in_specs,
        out_specs=rows(D_MODEL),
        out_shape=jax.ShapeDtypeStruct((n, D_MODEL), F32),
        input_output_aliases=aliases,
        compiler_params=pltpu.CompilerParams(dimension_semantics=("parallel",), vmem_limit_bytes=VMEM_LIMIT),
        name=f"final{half}",
    )(*args)


def _routing_tables(routet, n):
    tm = TM_MOE
    nt = (2 * n) // tm + N_EXPERTS_TOTAL
    ef = routet[0:2].astype(I32).reshape(-1)
    onehot = (ef[:, None] == jnp.arange(N_EXPERTS_TOTAL, dtype=I32)[None, :]).astype(I32)
    csum = jnp.cumsum(onehot, axis=0)
    counts = csum[-1]
    rank = jnp.sum(csum * onehot, axis=1) - 1
    padded = ((counts + tm - 1) // tm) * tm
    ends = jnp.cumsum(padded)
    offs = ends - padded
    pos = jnp.sum(onehot * offs[None, :], axis=1) + rank
    tile_start = jnp.arange(nt, dtype=I32) * tm
    tile_expert = jnp.minimum(jnp.sum((tile_start[:, None] >= ends[None, :]).astype(I32), axis=1),
                              N_EXPERTS_TOTAL - 1).astype(I32)
    tile_valid = (tile_start < ends[-1]).astype(I32)
    tile_block = jnp.minimum(jnp.arange(nt, dtype=I32), ends[-1] // tm - 1)
    pr = pos.reshape(2 * n // 32, 32)
    pp = jnp.concatenate([jnp.tile(pr[:, 8 * a:8 * (a + 1)], (1, SUBROWS)) for a in range(4)], axis=1)
    jv = (jnp.arange(128, dtype=I32) % 32) // 8
    piece = (pp // 8) * (8 * SUBROWS) + pp % 8 + 8 * jv[None, :]
    return tile_expert, tile_valid, tile_block, piece.reshape(2, n * SUBROWS)


def kernel(x, p, w_in, a_ln_g, a_ln_b, a_ws, a_bs, w_a_proj, w_b_proj, w_o, ln1_g, ln1_b, w_group_router,
           b_group_router, w_expert_router, b_expert_router, w_gate, w_up, w_down, w_ple, w_ple_gate,
           ln2_g, ln2_b):
    bsz, s, d = x.shape
    n = bsz * s
    assert d == D_MODEL and s % (SPAN * max(B_DILATIONS)) == 0 and n % TM_PROJ == 0
    assert w_in.shape[0] == 1, "one layer"

    w_in_b = w_in[0].astype(BF16)
    a_bias = jnp.repeat(a_bs[0].T, A_WIDTH // 8, axis=1)

    ga, gates, qkv1, qkv2, qkv3 = _proj(x, w_in_b, a_ln_g, a_ln_b, a_ws[0], a_bias)
    o1, l1 = _attn(qkv1.reshape(bsz, 1, s, 3 * COL), 0)
    o2, l2 = _attn(qkv2, 1)
    o3, l3 = _attn(qkv3, 2)

    pad = 128 - N_GROUPS - N_EXPERTS_TOTAL
    wr = jnp.concatenate([w_group_router[0], w_expert_router[0].reshape(d, N_EXPERTS_TOTAL),
                          jnp.zeros((d, pad), F32)], axis=1).astype(BF16)
    br = jnp.concatenate([b_group_router[0], b_expert_router[0].reshape(-1), jnp.zeros((pad,), F32)])[None, :]
    x1, x1p, route, routet = _mix(
        ga, gates, o1.reshape(n, B_WIDTH), o2, o3, l1.reshape(n, 128), l2, l3, x.reshape(n, d),
        w_a_proj[0].astype(BF16), w_b_proj[0].astype(BF16), w_o[0].astype(BF16), wr, br, ln1_g, ln1_b)

    tile_expert, tile_valid, tile_block, piece = _routing_tables(routet, n)
    nt = tile_expert.shape[0]
    xs = _sc_scatter_rows(x1p.reshape(n * SUBROWS, 128), piece.reshape(-1), nt * TM_MOE * SUBROWS)
    ys = _moe(tile_expert, tile_valid, tile_block, xs.reshape(nt * TM_MOE // 8, 32, 128),
              w_gate[0].reshape(N_EXPERTS_TOTAL, d, D_EXPERT), w_up[0].reshape(N_EXPERTS_TOTAL, d, D_EXPERT),
              w_down[0].reshape(N_EXPERTS_TOTAL, D_EXPERT, d)).reshape(nt * TM_MOE * SUBROWS, 128)
    out = None
    hp = n * SUBROWS // 2
    for half in range(2):
        yg = _sc_gather_rows(ys, piece[:, half * hp:(half + 1) * hp].reshape(-1))
        out = _final(yg.reshape(n // 8, 32, 128), x1, p[0].reshape(n, PLE_DIM), route,
                     w_ple[0].astype(BF16), w_ple_gate[0].astype(BF16), ln2_g, ln2_b, half, out)
    return out.reshape(bsz, s, d)
```

```python
import functools

import jax
import jax.numpy as jnp
from jax import lax
from jax.experimental import pallas as pl
from jax.experimental.pallas import tpu as pltpu
from jax.experimental.pallas import tpu_sc as plsc

F32 = jnp.float32
BF16 = jnp.bfloat16
U32 = jnp.uint32
I32 = jnp.int32

D_MODEL = 1024
PLE_DIM = 256
A_WIDTH = 512
A_CHUNK = 128
B_HEAD_DIM = 64
B_HEADS = 8
B_WIDTH = 512
B_DILATIONS = (1, 4, 16)
SPAN = 128
N_GROUPS = 4
N_EXPERTS = 8
N_EXPERTS_TOTAL = N_GROUPS * N_EXPERTS
D_EXPERT = 256
DEEPNORM_ALPHA = 2.0 ** 0.25
LN_EPS = 1e-5
COL = 512
NEG = -1e30

VMEM_LIMIT = 56 * 1024 * 1024

TM_PROJ = 512
TM_MIX = 512
TM_MOE = 512
TM_FIN = 1024


def _ln(x, g, b):
    mu = jnp.mean(x, axis=-1, keepdims=True)
    xc = x - mu
    var = jnp.mean(xc * xc, axis=-1, keepdims=True)
    return xc * lax.rsqrt(var + LN_EPS) * g + b


def _gelu_tanh(x):
    return 0.5 * x * (1.0 + jnp.tanh(0.7978845608028654 * (x + 0.044715 * (x * x * x))))


def _sigmoid(x):
    return 0.5 * jnp.tanh(0.5 * x) + 0.5


def _dot(a, b):
    return jnp.dot(a, b, preferred_element_type=F32)


PACK_W = D_MODEL // 2
SUBROWS = PACK_W // 128


def _store_packed_rows(ref, x):
    m = x.shape[0]
    xb = x.astype(BF16).astype(F32)
    lo = pltpu.bitcast(xb[:, :PACK_W], U32) >> 16
    hi = pltpu.bitcast(xb[:, PACK_W:], U32) & jnp.uint32(0xFFFF0000)
    w = hi | lo
    for j in range(SUBROWS):
        ref[:, 8 * j:8 * (j + 1), :] = w[:, 128 * j:128 * (j + 1)].reshape(m // 8, 8, 128)


def _load_packed_rows(ref):
    m = ref.shape[0] * 8
    ws = [ref[:, 8 * j:8 * (j + 1), :].reshape(m, 128) for j in range(SUBROWS)]
    lo = [pltpu.bitcast(w << 16, F32) for w in ws]
    hi = [pltpu.bitcast(w & jnp.uint32(0xFFFF0000), F32) for w in ws]
    return jnp.concatenate(lo + hi, axis=1)


def _proj_kernel(x_ref, *refs):
    w = refs[:15]
    lng_ref, lnb_ref, ws_ref, bias_ref = refs[15:19]
    ga_ref, gates_ref, qkv1_ref, qkv2_ref, qkv3_ref = refs[19:24]
    xc_ref = refs[24]
    tm = x_ref.shape[0]
    xb = x_ref[...].astype(BF16)

    u = _gelu_tanh(_dot(xb, w[0][...]))
    v = _gelu_tanh(_dot(xb, w[1][...]))
    vn = _ln(v, lng_ref[...], lnb_ref[...]).astype(BF16)

    row = lax.broadcasted_iota(I32, (A_CHUNK, A_CHUNK), 0)
    colm = lax.broadcasted_iota(I32, (A_CHUNK, A_CHUNK), 1)
    causal = colm <= row
    lo = colm < 64
    zero = jnp.zeros((A_CHUNK, A_CHUNK), BF16)
    wcat = []
    for j in range(4):
        w0 = jnp.where(causal, ws_ref[2 * j], 0.0).astype(BF16)
        w1 = jnp.where(causal, ws_ref[2 * j + 1], 0.0).astype(BF16)
        wcat.append(jnp.concatenate([w0, w1], axis=1))
    for c in range(tm // A_CHUNK):
        r0 = c * A_CHUNK
        for j in range(4):
            c0 = j * 128
            vt = vn[r0:r0 + A_CHUNK, c0:c0 + 128]
            rhs = jnp.concatenate([jnp.where(lo, vt, zero), jnp.where(lo, zero, vt)], axis=0)
            mixed = _dot(wcat[j], rhs) + bias_ref[:, c0:c0 + 128]
            ga_ref[r0:r0 + A_CHUNK, c0:c0 + 128] = (u[r0:r0 + A_CHUNK, c0:c0 + 128] * mixed).astype(BF16)

    for i in range(4):
        gates_ref[:, i * COL:(i + 1) * COL] = _sigmoid(_dot(xb, w[11 + i][...])).astype(BF16)
    for j in range(3):
        qkv1_ref[:, j * COL:(j + 1) * COL] = _dot(xb, w[2 + 3 * j][...]).astype(BF16)

    for c in range(D_MODEL // 128):
        xc_ref[c] = x_ref[:, c * 128:(c + 1) * 128]
    for gi, out_ref in ((1, qkv2_ref), (2, qkv3_ref)):
        dl = B_DILATIONS[gi]
        per = tm // dl
        xp = jnp.concatenate(
            [jnp.concatenate([xc_ref[c, pl.ds(r, per, stride=dl), :] for c in range(D_MODEL // 128)], axis=1)
             for r in range(dl)], axis=0).astype(BF16)
        for j in range(3):
            res = _dot(xp, w[2 + 3 * j + gi][...]).astype(BF16)
            for r in range(dl):
                out_ref[r, :, j * COL:(j + 1) * COL] = res[r * per:(r + 1) * per]


def _proj(x, w_in_b, a_ln_g, a_ln_b, a_ws, a_bias):
    bsz, s, _ = x.shape
    n = bsz * s
    tm = TM_PROJ
    tiles = s // tm
    x2 = x.reshape(n, D_MODEL)
    wspec = lambda j: pl.BlockSpec((D_MODEL, COL), lambda i, j=j: (0, j), pipeline_mode=pl.Buffered(1))
    full = lambda shape: pl.BlockSpec(shape, lambda i: (0,) * len(shape))
    rows = lambda width: pl.BlockSpec((tm, width), lambda i: (i, 0))
    dil = lambda dl: pl.BlockSpec((None, dl, tm // dl, 3 * COL), lambda i: (i // tiles, 0, i % tiles, 0))
    return pl.pallas_call(
        _proj_kernel,
        grid=(n // tm,),
        in_specs=[rows(D_MODEL)] + [wspec(j) for j in range(15)]
                 + [full((1, A_WIDTH)), full((1, A_WIDTH)), full((8, A_CHUNK, A_CHUNK)), full((A_CHUNK, A_WIDTH))],
        out_specs=[rows(A_WIDTH), rows(4 * COL), rows(3 * COL), dil(4), dil(16)],
        out_shape=[jax.ShapeDtypeStruct((n, A_WIDTH), BF16),
                   jax.ShapeDtypeStruct((n, 4 * COL), BF16),
                   jax.ShapeDtypeStruct((n, 3 * COL), BF16),
                   jax.ShapeDtypeStruct((bsz, 4, s // 4, 3 * COL), BF16),
                   jax.ShapeDtypeStruct((bsz, 16, s // 16, 3 * COL), BF16)],
        scratch_shapes=[pltpu.VMEM((D_MODEL // 128, tm, 128), F32)],
        compiler_params=pltpu.CompilerParams(dimension_semantics=("parallel",), vmem_limit_bytes=VMEM_LIMIT),
        name="proj",
    )(x2, *([w_in_b] * 15), a_ln_g, a_ln_b, a_ws, a_bias)


def _attn_kernel(qkv_ref, o_ref, lse_ref, *, ns, seq):
    nb = seq // SPAN
    lane = lax.broadcasted_iota(I32, (SPAN, 128), 1)
    lo = lane < 64
    lane16 = lane // 16
    qi = lax.broadcasted_iota(I32, (SPAN, 2 * SPAN), 0)
    ki = lax.broadcasted_iota(I32, (SPAN, 2 * SPAN), 1)
    causal = lax.broadcasted_iota(I32, (SPAN, SPAN), 1) <= lax.broadcasted_iota(I32, (SPAN, SPAN), 0)
    bias_first = jnp.where(causal, 0.0, NEG).astype(F32)
    bias_first = jnp.concatenate([bias_first, bias_first], axis=0)
    bias_main = jnp.where((ki >= qi) & (ki <= qi + SPAN), 0.0, NEG).astype(F32)
    bias_main = jnp.concatenate([bias_main, bias_main], axis=0)
    zero = jnp.zeros((SPAN, 128), BF16)

    for s in range(ns):
        def block(row0, start, bias, s=s):
            win = bias.shape[1]
            pairs = range(B_HEADS // 2)
            scores, values = [], []
            for jp in pairs:
                c0 = jp * 128
                q = qkv_ref[s, pl.ds(row0, SPAN), c0:c0 + 128] * jnp.asarray(0.125, BF16)
                k = qkv_ref[s, pl.ds(start, win), COL + c0:COL + c0 + 128]
                values.append(qkv_ref[s, pl.ds(start, win), 2 * COL + c0:2 * COL + c0 + 128])
                qs = jnp.concatenate([jnp.where(lo, q, zero), jnp.where(lo, zero, q)], axis=0)
                scores.append(lax.dot_general(qs, k, (((1,), (1,)), ((), ())), preferred_element_type=F32) + bias)
            probs, maxes, sums = [], [], []
            for jp in pairs:
                m = jnp.max(scores[jp], axis=-1, keepdims=True)
                p = jnp.exp(scores[jp] - m)
                maxes.append(m)
                sums.append(jnp.sum(p, axis=-1, keepdims=True))
                probs.append(p.astype(BF16))
            lse_tile = jnp.zeros((SPAN, 128), F32)
            for jp in pairs:
                c0 = jp * 128
                ov = _dot(probs[jp], values[jp])
                inv = 1.0 / sums[jp]
                o = jnp.where(lo, ov[:SPAN] * inv[:SPAN], ov[SPAN:] * inv[SPAN:])
                o_ref[pl.ds(row0, SPAN), s * B_WIDTH + c0:s * B_WIDTH + c0 + 128] = o.astype(BF16)
                lse = maxes[jp] + jnp.log(sums[jp])
                lse_tile = jnp.where(lane16 == 2 * jp, lse[:SPAN],
                                     jnp.where(lane16 == 2 * jp + 1, lse[SPAN:], lse_tile))
            lse_ref[pl.ds(row0, SPAN), s * 128:(s + 1) * 128] = lse_tile

        block(0, 0, bias_first)
        if nb > 1:
            def body(i, carry):
                block(pl.multiple_of(i * SPAN, SPAN), pl.multiple_of((i - 1) * SPAN, SPAN), bias_main)
                return carry
            lax.fori_loop(1, nb, body, 0)


def _attn(qkv_g, gi):
    bsz, dl, seq, _ = qkv_g.shape
    ns = max(1, min(dl, TM_PROJ // seq))
    return pl.pallas_call(
        functools.partial(_attn_kernel, ns=ns, seq=seq),
        grid=(bsz, dl // ns),
        in_specs=[pl.BlockSpec((None, ns, seq, 3 * COL), lambda b, r: (b, r, 0, 0))],
        out_specs=[pl.BlockSpec((None, seq, ns * B_WIDTH), lambda b, r: (b, 0, r)),
                   pl.BlockSpec((None, seq, ns * 128), lambda b, r: (b, 0, r))],
        out_shape=[jax.ShapeDtypeStruct((bsz, seq, dl * B_WIDTH), BF16),
                   jax.ShapeDtypeStruct((bsz, seq, dl * 128), F32)],
        compiler_params=pltpu.CompilerParams(dimension_semantics=("parallel", "parallel"),
                                             vmem_limit_bytes=VMEM_LIMIT),
        name=f"attn{dl}",
    )(qkv_g)


def _natural_rows(ref, dl, scr):
    nchunk, tm, _ = scr.shape
    w = nchunk * 128
    per = tm // dl
    for r in range(dl):
        for c in range(nchunk):
            scr[c, pl.ds(r, per, stride=dl), :] = ref[:, r * w + c * 128:r * w + (c + 1) * 128].astype(F32)
    return jnp.concatenate([scr[c] for c in range(nchunk)], axis=1)


def _mix_kernel(ga_ref, gates_ref, o1_ref, o2_ref, o3_ref, l1_ref, l2_ref, l3_ref, x_ref,
                wa_ref, wb_ref, wo_ref, wr_ref, br_ref, g1_ref, b1_ref,
                x1_ref, x1p_ref, routet_ref, o2s_ref, o3s_ref, l2s_ref, l3s_ref):
    tm = x_ref.shape[0]
    o2 = _natural_rows(o2_ref, 4, o2s_ref)
    o3 = _natural_rows(o3_ref, 16, o3s_ref)
    l2 = _natural_rows(l2_ref, 4, l2s_ref)
    l3 = _natural_rows(l3_ref, 16, l3s_ref)
    er = lax.broadcasted_iota(I32, (256, B_WIDTH), 0)
    ec = lax.broadcasted_iota(I32, (256, B_WIDTH), 1)
    expand = jnp.where(er % 128 == (ec // B_HEAD_DIM) * 16, 1.0, 0.0).astype(BF16)

    def widen(w):
        hi = w.astype(BF16)
        lo = (w - hi.astype(F32)).astype(BF16)
        return _dot(jnp.concatenate([hi, lo], axis=1), expand)

    h = tm // 2
    halves = (slice(0, h), slice(h, tm))
    obs = []
    for r in halves:
        l1 = l1_ref[r, :]
        mx = jnp.maximum(l1, jnp.maximum(l2[r], l3[r]))
        e1, e2, e3 = jnp.exp(l1 - mx), jnp.exp(l2[r] - mx), jnp.exp(l3[r] - mx)
        inv = 1.0 / (e1 + e2 + e3)
        obs.append(widen(e1 * inv) * o1_ref[r, :].astype(F32) + widen(e2 * inv) * o2[r] + widen(e3 * inv) * o3[r])
    ybs = [_dot(ob.astype(BF16), wb_ref[...]) for ob in obs]
    yas = [_dot(ga_ref[r, :], wa_ref[...]) for r in halves]
    pres = [gates_ref[r, :D_MODEL].astype(F32) * ya + gates_ref[r, D_MODEL:].astype(F32) * yb
            for r, ya, yb in zip(halves, yas, ybs)]
    mixes = [_dot(pre.astype(BF16), wo_ref[...]) for pre in pres]
    x1s = [_ln(DEEPNORM_ALPHA * x_ref[r, :] + mix, g1_ref[...], b1_ref[...]) for r, mix in zip(halves, mixes)]
    logits = [_dot(x1.astype(BF16), wr_ref[...]) + br_ref[...] for x1 in x1s]

    lane = lax.broadcasted_iota(I32, (h, 128), 1).astype(F32)
    big = 1e9
    for i, r in enumerate(halves):
        x1_ref[r, :] = x1s[i]
        _store_packed_rows(x1p_ref.at[i * h // 8:(i + 1) * h // 8], x1s[i])
        lg = logits[i]
        gl = jnp.where(lane < N_GROUPS, lg, NEG)
        gm = jnp.max(gl, axis=-1, keepdims=True)
        gidx = jnp.min(jnp.where(gl == gm, lane, big), axis=-1, keepdims=True)
        gsum = jnp.sum(jnp.where(lane < N_GROUPS, jnp.exp(gl - gm), 0.0), axis=-1, keepdims=True)
        gprob = 1.0 / gsum
        lo_lane = N_GROUPS + N_EXPERTS * gidx
        el = jnp.where((lane >= lo_lane) & (lane < lo_lane + N_EXPERTS), lg, NEG)
        v1 = jnp.max(el, axis=-1, keepdims=True)
        i1 = jnp.min(jnp.where(el == v1, lane, big), axis=-1, keepdims=True)
        el2 = jnp.where(lane == i1, NEG, el)
        v2 = jnp.max(el2, axis=-1, keepdims=True)
        i2 = jnp.min(jnp.where(el2 == v2, lane, big), axis=-1, keepdims=True)
        t = jnp.exp(v2 - v1)
        w1 = 1.0 / (1.0 + t)
        w2 = t * w1
        route = jnp.where(lane == 0, i1 - N_GROUPS,
                          jnp.where(lane == 1, i2 - N_GROUPS,
                                    jnp.where(lane == 2, gprob * w1,
                                              jnp.where(lane == 3, gprob * w2, 0.0))))
        routet_ref[:, r] = route.T[:8, :]


def _mix(ga, gates, o1, o2, o3, l1, l2, l3, x2, wa, wb, wo, wr, br, g1, b1):
    n = x2.shape[0]
    bsz = o2.shape[0]
    tm = TM_MIX
    tiles = n // bsz // tm
    rows = lambda w: pl.BlockSpec((tm, w), lambda i: (i, 0))
    grouped = lambda a, dl: pl.BlockSpec((None, tm // dl, a.shape[2]), lambda i: (i // tiles, i % tiles, 0))
    full = lambda a: pl.BlockSpec(a.shape, lambda i: (0,) * a.ndim)
    return pl.pallas_call(
        _mix_kernel,
        grid=(n // tm,),
        in_specs=[rows(A_WIDTH), rows(2 * D_MODEL), rows(B_WIDTH), grouped(o2, 4), grouped(o3, 16),
                  rows(128), grouped(l2, 4), grouped(l3, 16), rows(D_MODEL),
                  full(wa), full(wb), full(wo), full(wr), full(br), full(g1), full(b1)],
        out_specs=[rows(D_MODEL), pl.BlockSpec((tm // 8, 32, 128), lambda i: (i, 0, 0)),
                   pl.BlockSpec((8, tm), lambda i: (0, i))],
        out_shape=[jax.ShapeDtypeStruct((n, D_MODEL), F32),
                   jax.ShapeDtypeStruct((n // 8, 32, 128), U32),
                   jax.ShapeDtypeStruct((8, n), F32)],
        scratch_shapes=[pltpu.VMEM((B_WIDTH // 128, tm, 128), F32), pltpu.VMEM((B_WIDTH // 128, tm, 128), F32),
                        pltpu.VMEM((1, tm, 128), F32), pltpu.VMEM((1, tm, 128), F32)],
        compiler_params=pltpu.CompilerParams(dimension_semantics=("parallel",), vmem_limit_bytes=VMEM_LIMIT),
        name="mix",
    )(ga, gates, o1, o2, o3, l1, l2, l3, x2, wa, wb, wo, wr, br, g1, b1)


SC_WINDOW = 128


def _sc_mesh():
    return plsc.VectorSubcoreMesh(core_axis_name="core", subcore_axis_name="subcore")


def _sc_scatter_rows(rows, dst, n_out):
    r = rows.shape[0]
    m = dst.shape[0]
    nblk = r // SC_WINDOW

    @pl.kernel(out_type=jax.ShapeDtypeStruct((n_out, 128), rows.dtype), mesh=_sc_mesh())
    def scatter(rows_hbm, dst_hbm, out_hbm):
        def body(rows_vmem, dst_vmem):
            pltpu.sync_copy(rows_vmem, out_hbm.at[dst_vmem.at[0]])

        pltpu.emit_pipeline(
            body,
            grid=(m // SC_WINDOW,),
            in_specs=[pl.BlockSpec((SC_WINDOW, 128), lambda i: (i % nblk, 0)),
                      pl.BlockSpec((1, SC_WINDOW), lambda i: (0, i))],
            out_specs=[],
            core_axis_name=("core", "subcore"),
            dimension_semantics=(pltpu.PARALLEL,),
        )(rows_hbm, dst_hbm)

    return scatter(rows, dst.reshape(1, m))


def _sc_gather_rows(table, src):
    m = src.shape[0]

    @pl.kernel(out_type=jax.ShapeDtypeStruct((m, 128), table.dtype), mesh=_sc_mesh())
    def gather(table_hbm, src_hbm, out_hbm):
        def body(src_vmem, out_vmem):
            pltpu.sync_copy(table_hbm.at[src_vmem.at[0]], out_vmem)

        pltpu.emit_pipeline(
            body,
            grid=(m // SC_WINDOW,),
            in_specs=[pl.BlockSpec((1, SC_WINDOW), lambda i: (0, i))],
            out_specs=[pl.BlockSpec((SC_WINDOW, 128), lambda i: (i, 0))],
            core_axis_name=("core", "subcore"),
            dimension_semantics=(pltpu.PARALLEL,),
        )(src_hbm, out_hbm)

    return gather(table, src.reshape(1, m))


def _moe_kernel(te_ref, tv_ref, tb_ref, xs_ref, wg_ref, wu_ref, wd_ref, ys_ref, wgb_ref, wub_ref, wdb_ref):
    del tb_ref
    t = pl.program_id(0)

    @pl.when((t == 0) | (te_ref[t] != te_ref[jnp.maximum(t - 1, 0)]))
    def _():
        wgb_ref[...] = wg_ref[...].astype(BF16)
        wub_ref[...] = wu_ref[...].astype(BF16)
        wdb_ref[...] = wd_ref[...].astype(BF16)

    @pl.when(tv_ref[t] == 1)
    def _():
        xb = _load_packed_rows(xs_ref).astype(BF16)
        g = _dot(xb, wgb_ref[...])
        u = _dot(xb, wub_ref[...])
        h = (g * _sigmoid(g) * u).astype(BF16)
        _store_packed_rows(ys_ref, _dot(h, wdb_ref[...]))


def _moe(tile_expert, tile_valid, tile_block, xs, wg, wu, wd):
    tm = TM_MOE
    nt = tile_expert.shape[0]
    rows = pl.BlockSpec((tm // 8, 32, 128), lambda t, te, tv, tb: (tb[t], 0, 0))
    return pl.pallas_call(
        _moe_kernel,
        grid_spec=pltpu.PrefetchScalarGridSpec(
            num_scalar_prefetch=3,
            grid=(nt,),
            in_specs=[rows,
                      pl.BlockSpec((None, D_MODEL, D_EXPERT), lambda t, te, tv, tb: (te[t], 0, 0)),
                      pl.BlockSpec((None, D_MODEL, D_EXPERT), lambda t, te, tv, tb: (te[t], 0, 0)),
                      pl.BlockSpec((None, D_EXPERT, D_MODEL), lambda t, te, tv, tb: (te[t], 0, 0))],
            out_specs=rows,
            scratch_shapes=[pltpu.VMEM((D_MODEL, D_EXPERT), BF16), pltpu.VMEM((D_MODEL, D_EXPERT), BF16),
                            pltpu.VMEM((D_EXPERT, D_MODEL), BF16)]),
        out_shape=jax.ShapeDtypeStruct((nt * tm // 8, 32, 128), U32),
        compiler_params=pltpu.CompilerParams(dimension_semantics=("arbitrary",), vmem_limit_bytes=VMEM_LIMIT),
        name="moe",
    )(tile_expert, tile_valid, tile_block, xs, wg, wu, wd)


def _final_kernel(y0_ref, y1_ref, x1_ref, p_ref, routet_ref, wple_ref, wpg_ref, g2_ref, b2_ref, *rest):
    out_ref = rest[-1]
    tm = x1_ref.shape[0]
    route = routet_ref[...].T
    h = tm // 2
    halves = (slice(0, h), slice(h, tm))
    x1s = [x1_ref[r, :] for r in halves]
    gates = [_dot(x1.astype(BF16), wpg_ref[...]) for x1 in x1s]
    plins = [_dot(p_ref[r, :].astype(BF16), wple_ref[...]) for r in halves]
    for i, r in enumerate(halves):
        g8 = slice(i * h // 8, (i + 1) * h // 8)
        ffn = (route[r, 2:3] * _load_packed_rows(y0_ref.at[g8])
               + route[r, 3:4] * _load_packed_rows(y1_ref.at[g8]))
        ple = plins[i] * _sigmoid(gates[i])
        out_ref[r, :] = _ln(DEEPNORM_ALPHA * x1s[i] + ffn + ple, g2_ref[...], b2_ref[...])


def _final(yg, x1, p2, routet, wple, wpg, g2, b2, half, prev):
    n = x1.shape[0]
    tm = TM_FIN
    nt = n // 2 // tm
    off = half * nt
    rows = lambda w: pl.BlockSpec((tm, w), lambda t: (t + off, 0))
    full = lambda a: pl.BlockSpec(a.shape, lambda t: (0,) * a.ndim)
    in_specs = [pl.BlockSpec((tm // 8, 32, 128), lambda t: (t, 0, 0)),
                pl.BlockSpec((tm // 8, 32, 128), lambda t: (t + nt, 0, 0)),
                rows(D_MODEL), rows(PLE_DIM), pl.BlockSpec((8, tm), lambda t: (0, t + off)),
                full(wple), full(wpg), full(g2), full(b2)]
    args = [yg, yg, x1, p2, routet, wple, wpg, g2, b2]
    aliases = {}
    if prev is not None:
        in_specs.append(pl.BlockSpec(memory_space=pl.ANY))
        args.append(prev)
        aliases = {len(args) - 1: 0}
    return pl.pallas_call(
        _final_kernel,
        grid=(nt,),
        in_specs=in_specs,
        out_specs=rows(D_MODEL),
        out_shape=jax.ShapeDtypeStruct((n, D_MODEL), F32),
        input_output_aliases=aliases,
        compiler_params=pltpu.CompilerParams(dimension_semantics=("parallel",), vmem_limit_bytes=VMEM_LIMIT),
        name=f"final{half}",
    )(*args)


def _routing_tables(routet, n):
    tm = TM_MOE
    nt = (2 * n) // tm + N_EXPERTS_TOTAL
    ef = routet[0:2].astype(I32).reshape(-1)
    onehot = (ef[:, None] == jnp.arange(N_EXPERTS_TOTAL, dtype=I32)[None, :]).astype(I32)
    csum = jnp.cumsum(onehot, axis=0)
    counts = csum[-1]
    rank = jnp.sum(csum * onehot, axis=1) - 1
    padded = ((counts + tm - 1) // tm) * tm
    ends = jnp.cumsum(padded)
    offs = ends - padded
    pos = jnp.sum(onehot * offs[None, :], axis=1) + rank
    tile_start = jnp.arange(nt, dtype=I32) * tm
    tile_expert = jnp.minimum(jnp.sum((tile_start[:, None] >= ends[None, :]).astype(I32), axis=1),
                              N_EXPERTS_TOTAL - 1).astype(I32)
    tile_valid = (tile_start < ends[-1]).astype(I32)
    tile_block = jnp.minimum(jnp.arange(nt, dtype=I32), ends[-1] // tm - 1)
    pr = pos.reshape(2 * n // 32, 32)
    pp = jnp.concatenate([jnp.tile(pr[:, 8 * a:8 * (a + 1)], (1, SUBROWS)) for a in range(4)], axis=1)
    jv = (jnp.arange(128, dtype=I32) % 32) // 8
    piece = (pp // 8) * (8 * SUBROWS) + pp % 8 + 8 * jv[None, :]
    return tile_expert, tile_valid, tile_block, piece.reshape(2, n * SUBROWS)


def kernel(x, p, w_in, a_ln_g, a_ln_b, a_ws, a_bs, w_a_proj, w_b_proj, w_o, ln1_g, ln1_b, w_group_router,
           b_group_router, w_expert_router, b_expert_router, w_gate, w_up, w_down, w_ple, w_ple_gate,
           ln2_g, ln2_b):
    bsz, s, d = x.shape
    n = bsz * s
    assert d == D_MODEL and s % (SPAN * max(B_DILATIONS)) == 0 and n % TM_PROJ == 0
    assert w_in.shape[0] == 1, "one layer"

    w_in_b = w_in[0].astype(BF16)
    a_bias = jnp.repeat(a_bs[0].T, A_WIDTH // 8, axis=1)

    ga, gates, qkv1, qkv2, qkv3 = _proj(x, w_in_b, a_ln_g, a_ln_b, a_ws[0], a_bias)
    o1, l1 = _attn(qkv1.reshape(bsz, 1, s, 3 * COL), 0)
    o2, l2 = _attn(qkv2, 1)
    o3, l3 = _attn(qkv3, 2)

    pad = 128 - N_GROUPS - N_EXPERTS_TOTAL
    wr = jnp.concatenate([w_group_router[0], w_expert_router[0].reshape(d, N_EXPERTS_TOTAL),
                          jnp.zeros((d, pad), F32)], axis=1).astype(BF16)
    br = jnp.concatenate([b_group_router[0], b_expert_router[0].reshape(-1), jnp.zeros((pad,), F32)])[None, :]
    x1, x1p, routet = _mix(
        ga, gates, o1.reshape(n, B_WIDTH), o2, o3, l1.reshape(n, 128), l2, l3, x.reshape(n, d),
        w_a_proj[0].astype(BF16), w_b_proj[0].astype(BF16), w_o[0].astype(BF16), wr, br, ln1_g, ln1_b)

    tile_expert, tile_valid, tile_block, piece = _routing_tables(routet, n)
    nt = tile_expert.shape[0]
    xs = _sc_scatter_rows(x1p.reshape(n * SUBROWS, 128), piece.reshape(-1), nt * TM_MOE * SUBROWS)
    ys = _moe(tile_expert, tile_valid, tile_block, xs.reshape(nt * TM_MOE // 8, 32, 128),
              w_gate[0].reshape(N_EXPERTS_TOTAL, d, D_EXPERT), w_up[0].reshape(N_EXPERTS_TOTAL, d, D_EXPERT),
              w_down[0].reshape(N_EXPERTS_TOTAL, D_EXPERT, d)).reshape(nt * TM_MOE * SUBROWS, 128)
    out = None
    hp = n * SUBROWS // 2
    for half in range(2):
        yg = _sc_gather_rows(ys, piece[:, half * hp:(half + 1) * hp].reshape(-1))
        out = _final(yg.reshape(n // 8, 32, 128), x1, p[0].reshape(n, PLE_DIM), routet,
                     w_ple[0].astype(BF16), w_ple_gate[0].astype(BF16), ln2_g, ln2_b, half, out)
    return out.reshape(bsz, s, d)
```

```python
import functools

import jax
import jax.numpy as jnp
from jax import lax
from jax.experimental import pallas as pl
from jax.experimental.pallas import tpu as pltpu
from jax.experimental.pallas import tpu_sc as plsc

F32 = jnp.float32
BF16 = jnp.bfloat16
U32 = jnp.uint32
I32 = jnp.int32

D_MODEL = 1024
PLE_DIM = 256
A_WIDTH = 512
A_CHUNK = 128
B_HEAD_DIM = 64
B_HEADS = 8
B_WIDTH = 512
B_DILATIONS = (1, 4, 16)
SPAN = 128
N_GROUPS = 4
N_EXPERTS = 8
N_EXPERTS_TOTAL = N_GROUPS * N_EXPERTS
D_EXPERT = 256
DEEPNORM_ALPHA = 2.0 ** 0.25
LN_EPS = 1e-5
COL = 512
NEG = -1e30

VMEM_LIMIT = 56 * 1024 * 1024

TM_PROJ = 512
TM_MIX = 512
TM_MOE = 512
TM_FIN = 1024


def _ln(x, g, b):
    mu = jnp.mean(x, axis=-1, keepdims=True)
    xc = x - mu
    var = jnp.mean(xc * xc, axis=-1, keepdims=True)
    return xc * lax.rsqrt(var + LN_EPS) * g + b


def _gelu_tanh(x):
    return 0.5 * x * (1.0 + jnp.tanh(0.7978845608028654 * (x + 0.044715 * (x * x * x))))


def _sigmoid(x):
    return 0.5 * jnp.tanh(0.5 * x) + 0.5


def _dot(a, b):
    return jnp.dot(a, b, preferred_element_type=F32)


PACK_W = D_MODEL // 2
SUBROWS = PACK_W // 128


def _store_packed_rows(ref, x):
    m = x.shape[0]
    xb = x.astype(BF16).astype(F32)
    lo = pltpu.bitcast(xb[:, :PACK_W], U32) >> 16
    hi = pltpu.bitcast(xb[:, PACK_W:], U32) & jnp.uint32(0xFFFF0000)
    w = hi | lo
    for j in range(SUBROWS):
        ref[:, 8 * j:8 * (j + 1), :] = w[:, 128 * j:128 * (j + 1)].reshape(m // 8, 8, 128)


def _load_packed_rows(ref):
    m = ref.shape[0] * 8
    ws = [ref[:, 8 * j:8 * (j + 1), :].reshape(m, 128) for j in range(SUBROWS)]
    lo = [pltpu.bitcast(w << 16, F32) for w in ws]
    hi = [pltpu.bitcast(w & jnp.uint32(0xFFFF0000), F32) for w in ws]
    return jnp.concatenate(lo + hi, axis=1)


def _proj_kernel(x_ref, *refs):
    w = refs[:15]
    lng_ref, lnb_ref, ws_ref, bias_ref = refs[15:19]
    ga_ref, gates_ref, qkv1_ref, qkv2_ref, qkv3_ref = refs[19:24]
    xc_ref = refs[24]
    tm = x_ref.shape[0]
    xb = x_ref[...].astype(BF16)

    u = _gelu_tanh(_dot(xb, w[0][...]))
    v = _gelu_tanh(_dot(xb, w[1][...]))
    vn = _ln(v, lng_ref[...], lnb_ref[...]).astype(BF16)

    row = lax.broadcasted_iota(I32, (A_CHUNK, A_CHUNK), 0)
    colm = lax.broadcasted_iota(I32, (A_CHUNK, A_CHUNK), 1)
    causal = colm <= row
    lo = colm < 64
    zero = jnp.zeros((A_CHUNK, A_CHUNK), BF16)
    wcat = []
    for j in range(4):
        w0 = jnp.where(causal, ws_ref[2 * j], 0.0).astype(BF16)
        w1 = jnp.where(causal, ws_ref[2 * j + 1], 0.0).astype(BF16)
        wcat.append(jnp.concatenate([w0, w1], axis=1))
    for c in range(tm // A_CHUNK):
        r0 = c * A_CHUNK
        for j in range(4):
            c0 = j * 128
            vt = vn[r0:r0 + A_CHUNK, c0:c0 + 128]
            rhs = jnp.concatenate([jnp.where(lo, vt, zero), jnp.where(lo, zero, vt)], axis=0)
            mixed = _dot(wcat[j], rhs) + bias_ref[:, c0:c0 + 128]
            ga_ref[r0:r0 + A_CHUNK, c0:c0 + 128] = (u[r0:r0 + A_CHUNK, c0:c0 + 128] * mixed).astype(BF16)

    for i in range(4):
        gates_ref[:, i * COL:(i + 1) * COL] = _sigmoid(_dot(xb, w[11 + i][...])).astype(BF16)
    for j in range(3):
        qkv1_ref[:, j * COL:(j + 1) * COL] = _dot(xb, w[2 + 3 * j][...]).astype(BF16)

    for c in range(D_MODEL // 128):
        xc_ref[c] = x_ref[:, c * 128:(c + 1) * 128]
    for gi, out_ref in ((1, qkv2_ref), (2, qkv3_ref)):
        dl = B_DILATIONS[gi]
        per = tm // dl
        xp = jnp.concatenate(
            [jnp.concatenate([xc_ref[c, pl.ds(r, per, stride=dl), :] for c in range(D_MODEL // 128)], axis=1)
             for r in range(dl)], axis=0).astype(BF16)
        for j in range(3):
            res = _dot(xp, w[2 + 3 * j + gi][...]).astype(BF16)
            for r in range(dl):
                out_ref[r, :, j * COL:(j + 1) * COL] = res[r * per:(r + 1) * per]


def _proj(x, w_in_b, a_ln_g, a_ln_b, a_ws, a_bias):
    bsz, s, _ = x.shape
    n = bsz * s
    tm = TM_PROJ
    tiles = s // tm
    x2 = x.reshape(n, D_MODEL)
    wspec = lambda j: pl.BlockSpec((D_MODEL, COL), lambda i, j=j: (0, j), pipeline_mode=pl.Buffered(1))
    full = lambda shape: pl.BlockSpec(shape, lambda i: (0,) * len(shape))
    rows = lambda width: pl.BlockSpec((tm, width), lambda i: (i, 0))
    dil = lambda dl: pl.BlockSpec((None, dl, tm // dl, 3 * COL), lambda i: (i // tiles, 0, i % tiles, 0))
    return pl.pallas_call(
        _proj_kernel,
        grid=(n // tm,),
        in_specs=[rows(D_MODEL)] + [wspec(j) for j in range(15)]
                 + [full((1, A_WIDTH)), full((1, A_WIDTH)), full((8, A_CHUNK, A_CHUNK)), full((A_CHUNK, A_WIDTH))],
        out_specs=[rows(A_WIDTH), rows(4 * COL), rows(3 * COL), dil(4), dil(16)],
        out_shape=[jax.ShapeDtypeStruct((n, A_WIDTH), BF16),
                   jax.ShapeDtypeStruct((n, 4 * COL), BF16),
                   jax.ShapeDtypeStruct((n, 3 * COL), BF16),
                   jax.ShapeDtypeStruct((bsz, 4, s // 4, 3 * COL), BF16),
                   jax.ShapeDtypeStruct((bsz, 16, s // 16, 3 * COL), BF16)],
        scratch_shapes=[pltpu.VMEM((D_MODEL // 128, tm, 128), F32)],
        compiler_params=pltpu.CompilerParams(dimension_semantics=("parallel",), vmem_limit_bytes=VMEM_LIMIT),
        name="proj",
    )(x2, *([w_in_b] * 15), a_ln_g, a_ln_b, a_ws, a_bias)


def _attn_kernel(qkv_ref, o_ref, lse_ref, *, ns, seq):
    nb = seq // SPAN
    lane = lax.broadcasted_iota(I32, (SPAN, 128), 1)
    lo = lane < 64
    lane16 = lane // 16
    qi = lax.broadcasted_iota(I32, (SPAN, 2 * SPAN), 0)
    ki = lax.broadcasted_iota(I32, (SPAN, 2 * SPAN), 1)
    causal = lax.broadcasted_iota(I32, (SPAN, SPAN), 1) <= lax.broadcasted_iota(I32, (SPAN, SPAN), 0)
    bias_first = jnp.where(causal, 0.0, NEG).astype(F32)
    bias_first = jnp.concatenate([bias_first, bias_first], axis=0)
    bias_main = jnp.where((ki >= qi) & (ki <= qi + SPAN), 0.0, NEG).astype(F32)
    bias_main = jnp.concatenate([bias_main, bias_main], axis=0)
    zero = jnp.zeros((SPAN, 128), BF16)

    for s in range(ns):
        def block(row0, start, bias, s=s):
            win = bias.shape[1]
            pairs = range(B_HEADS // 2)
            scores, values = [], []
            for jp in pairs:
                c0 = jp * 128
                q = qkv_ref[s, pl.ds(row0, SPAN), c0:c0 + 128] * jnp.asarray(0.125, BF16)
                k = qkv_ref[s, pl.ds(start, win), COL + c0:COL + c0 + 128]
                values.append(qkv_ref[s, pl.ds(start, win), 2 * COL + c0:2 * COL + c0 + 128])
                qs = jnp.concatenate([jnp.where(lo, q, zero), jnp.where(lo, zero, q)], axis=0)
                scores.append(lax.dot_general(qs, k, (((1,), (1,)), ((), ())), preferred_element_type=F32) + bias)
            probs, maxes, sums = [], [], []
            for jp in pairs:
                m = jnp.max(scores[jp], axis=-1, keepdims=True)
                p = jnp.exp(scores[jp] - m)
                maxes.append(m)
                sums.append(jnp.sum(p, axis=-1, keepdims=True))
                probs.append(p.astype(BF16))
            lse_tile = jnp.zeros((SPAN, 128), F32)
            for jp in pairs:
                c0 = jp * 128
                ov = _dot(probs[jp], values[jp])
                inv = 1.0 / sums[jp]
                o = jnp.where(lo, ov[:SPAN] * inv[:SPAN], ov[SPAN:] * inv[SPAN:])
                o_ref[pl.ds(row0, SPAN), s * B_WIDTH + c0:s * B_WIDTH + c0 + 128] = o.astype(BF16)
                lse = maxes[jp] + jnp.log(sums[jp])
                lse_tile = jnp.where(lane16 == 2 * jp, lse[:SPAN],
                                     jnp.where(lane16 == 2 * jp + 1, lse[SPAN:], lse_tile))
            lse_ref[pl.ds(row0, SPAN), s * 128:(s + 1) * 128] = lse_tile

        block(0, 0, bias_first)
        if nb > 1:
            def body(i, carry):
                block(pl.multiple_of(i * SPAN, SPAN), pl.multiple_of((i - 1) * SPAN, SPAN), bias_main)
                return carry
            lax.fori_loop(1, nb, body, 0)


def _attn(qkv_g, gi):
    bsz, dl, seq, _ = qkv_g.shape
    ns = max(1, min(dl, TM_PROJ // seq))
    return pl.pallas_call(
        functools.partial(_attn_kernel, ns=ns, seq=seq),
        grid=(bsz, dl // ns),
        in_specs=[pl.BlockSpec((None, ns, seq, 3 * COL), lambda b, r: (b, r, 0, 0))],
        out_specs=[pl.BlockSpec((None, seq, ns * B_WIDTH), lambda b, r: (b, 0, r)),
                   pl.BlockSpec((None, seq, ns * 128), lambda b, r: (b, 0, r))],
        out_shape=[jax.ShapeDtypeStruct((bsz, seq, dl * B_WIDTH), BF16),
                   jax.ShapeDtypeStruct((bsz, seq, dl * 128), F32)],
        compiler_params=pltpu.CompilerParams(dimension_semantics=("parallel", "parallel"),
                                             vmem_limit_bytes=VMEM_LIMIT),
        name=f"attn{dl}",
    )(qkv_g)


def _natural_rows(ref, dl, scr):
    nchunk, tm, _ = scr.shape
    w = nchunk * 128
    per = tm // dl
    for r in range(dl):
        for c in range(nchunk):
            scr[c, pl.ds(r, per, stride=dl), :] = ref[:, r * w + c * 128:r * w + (c + 1) * 128].astype(F32)
    return jnp.concatenate([scr[c] for c in range(nchunk)], axis=1)


def _mix_kernel(ga_ref, gates_ref, o1_ref, o2_ref, o3_ref, l1_ref, l2_ref, l3_ref, x_ref,
                wa_ref, wb_ref, wo_ref, wr_ref, br_ref, g1_ref, b1_ref,
                x1_ref, x1p_ref, routet_ref, o2s_ref, o3s_ref, l2s_ref, l3s_ref):
    tm = x_ref.shape[0]
    o2 = _natural_rows(o2_ref, 4, o2s_ref)
    o3 = _natural_rows(o3_ref, 16, o3s_ref)
    l2 = _natural_rows(l2_ref, 4, l2s_ref)
    l3 = _natural_rows(l3_ref, 16, l3s_ref)
    er = lax.broadcasted_iota(I32, (256, B_WIDTH), 0)
    ec = lax.broadcasted_iota(I32, (256, B_WIDTH), 1)
    expand = jnp.where(er % 128 == (ec // B_HEAD_DIM) * 16, 1.0, 0.0).astype(BF16)

    def widen(w):
        hi = w.astype(BF16)
        lo = (w - hi.astype(F32)).astype(BF16)
        return _dot(jnp.concatenate([hi, lo], axis=1), expand)

    h = tm // 2
    halves = (slice(0, h), slice(h, tm))
    obs = []
    for r in halves:
        l1 = l1_ref[r, :]
        mx = jnp.maximum(l1, jnp.maximum(l2[r], l3[r]))
        e1, e2, e3 = jnp.exp(l1 - mx), jnp.exp(l2[r] - mx), jnp.exp(l3[r] - mx)
        inv = 1.0 / (e1 + e2 + e3)
        obs.append(widen(e1 * inv) * o1_ref[r, :].astype(F32) + widen(e2 * inv) * o2[r] + widen(e3 * inv) * o3[r])
    ybs = [_dot(ob.astype(BF16), wb_ref[...]) for ob in obs]
    yas = [_dot(ga_ref[r, :], wa_ref[...]) for r in halves]
    pres = [gates_ref[r, :D_MODEL].astype(F32) * ya + gates_ref[r, D_MODEL:].astype(F32) * yb
            for r, ya, yb in zip(halves, yas, ybs)]
    mixes = [_dot(pre.astype(BF16), wo_ref[...]) for pre in pres]
    x1s = [_ln(DEEPNORM_ALPHA * x_ref[r, :] + mix, g1_ref[...], b1_ref[...]) for r, mix in zip(halves, mixes)]
    logits = [_dot(x1.astype(BF16), wr_ref[...]) + br_ref[...] for x1 in x1s]

    lane = lax.broadcasted_iota(I32, (h, 128), 1).astype(F32)
    big = 1e9
    for i, r in enumerate(halves):
        x1_ref[r, :] = x1s[i]
        _store_packed_rows(x1p_ref.at[i * h // 8:(i + 1) * h // 8], x1s[i])
        lg = logits[i]
        gl = jnp.where(lane < N_GROUPS, lg, NEG)
        gm = jnp.max(gl, axis=-1, keepdims=True)
        gidx = jnp.min(jnp.where(gl == gm, lane, big), axis=-1, keepdims=True)
        gsum = jnp.sum(jnp.where(lane < N_GROUPS, jnp.exp(gl - gm), 0.0), axis=-1, keepdims=True)
        gprob = 1.0 / gsum
        lo_lane = N_GROUPS + N_EXPERTS * gidx
        el = jnp.where((lane >= lo_lane) & (lane < lo_lane + N_EXPERTS), lg, NEG)
        v1 = jnp.max(el, axis=-1, keepdims=True)
        i1 = jnp.min(jnp.where(el == v1, lane, big), axis=-1, keepdims=True)
        el2 = jnp.where(lane == i1, NEG, el)
        v2 = jnp.max(el2, axis=-1, keepdims=True)
        i2 = jnp.min(jnp.where(el2 == v2, lane, big), axis=-1, keepdims=True)
        t = jnp.exp(v2 - v1)
        w1 = 1.0 / (1.0 + t)
        w2 = t * w1
        route = jnp.where(lane == 0, i1 - N_GROUPS,
                          jnp.where(lane == 1, i2 - N_GROUPS,
                                    jnp.where(lane == 2, gprob * w1,
                                              jnp.where(lane == 3, gprob * w2, 0.0))))
        routet_ref[:, r] = route.T[:8, :]


def _mix(ga, gates, o1, o2, o3, l1, l2, l3, x2, wa, wb, wo, wr, br, g1, b1):
    n = x2.shape[0]
    bsz = o2.shape[0]
    tm = TM_MIX
    tiles = n // bsz // tm
    rows = lambda w: pl.BlockSpec((tm, w), lambda i: (i, 0))
    grouped = lambda a, dl: pl.BlockSpec((None, tm // dl, a.shape[2]), lambda i: (i // tiles, i % tiles, 0))
    full = lambda a: pl.BlockSpec(a.shape, lambda i: (0,) * a.ndim)
    return pl.pallas_call(
        _mix_kernel,
        grid=(n // tm,),
        in_specs=[rows(A_WIDTH), rows(2 * D_MODEL), rows(B_WIDTH), grouped(o2, 4), grouped(o3, 16),
                  rows(128), grouped(l2, 4), grouped(l3, 16), rows(D_MODEL),
                  full(wa), full(wb), full(wo), full(wr), full(br), full(g1), full(b1)],
        out_specs=[rows(D_MODEL), pl.BlockSpec((tm // 8, 32, 128), lambda i: (i, 0, 0)),
                   pl.BlockSpec((8, tm), lambda i: (0, i))],
        out_shape=[jax.ShapeDtypeStruct((n, D_MODEL), F32),
                   jax.ShapeDtypeStruct((n // 8, 32, 128), U32),
                   jax.ShapeDtypeStruct((8, n), F32)],
        scratch_shapes=[pltpu.VMEM((B_WIDTH // 128, tm, 128), F32), pltpu.VMEM((B_WIDTH // 128, tm, 128), F32),
                        pltpu.VMEM((1, tm, 128), F32), pltpu.VMEM((1, tm, 128), F32)],
        compiler_params=pltpu.CompilerParams(dimension_semantics=("parallel",), vmem_limit_bytes=VMEM_LIMIT),
        name="mix",
    )(ga, gates, o1, o2, o3, l1, l2, l3, x2, wa, wb, wo, wr, br, g1, b1)


SC_WINDOW = 128


def _sc_mesh():
    return plsc.VectorSubcoreMesh(core_axis_name="core", subcore_axis_name="subcore")


def _sc_scatter_rows(rows, dst, n_out):
    r = rows.shape[0]
    m = dst.shape[0]
    nblk = r // SC_WINDOW

    @pl.kernel(out_type=jax.ShapeDtypeStruct((n_out, 128), rows.dtype), mesh=_sc_mesh())
    def scatter(rows_hbm, dst_hbm, out_hbm):
        def body(rows_vmem, dst_vmem):
            pltpu.sync_copy(rows_vmem, out_hbm.at[dst_vmem.at[0]])

        pltpu.emit_pipeline(
            body,
            grid=(m // SC_WINDOW,),
            in_specs=[pl.BlockSpec((SC_WINDOW, 128), lambda i: (i % nblk, 0)),
                      pl.BlockSpec((1, SC_WINDOW), lambda i: (0, i))],
            out_specs=[],
            core_axis_name=("core", "subcore"),
            dimension_semantics=(pltpu.PARALLEL,),
        )(rows_hbm, dst_hbm)

    return scatter(rows, dst.reshape(1, m))


def _sc_gather_rows(table, src):
    m = src.shape[0]

    @pl.kernel(out_type=jax.ShapeDtypeStruct((m, 128), table.dtype), mesh=_sc_mesh())
    def gather(table_hbm, src_hbm, out_hbm):
        def body(src_vmem, out_vmem):
            pltpu.sync_copy(table_hbm.at[src_vmem.at[0]], out_vmem)

        pltpu.emit_pipeline(
            body,
            grid=(m // SC_WINDOW,),
            in_specs=[pl.BlockSpec((1, SC_WINDOW), lambda i: (0, i))],
            out_specs=[pl.BlockSpec((SC_WINDOW, 128), lambda i: (i, 0))],
            core_axis_name=("core", "subcore"),
            dimension_semantics=(pltpu.PARALLEL,),
        )(src_hbm, out_hbm)

    return gather(table, src.reshape(1, m))


def _moe_kernel(te_ref, tv_ref, tb_ref, xs_ref, wg_ref, wu_ref, wd_ref, ys_ref, wgb_ref, wub_ref, wdb_ref):
    del tb_ref
    t = pl.program_id(0)

    @pl.when((t == 0) | (te_ref[t] != te_ref[jnp.maximum(t - 1, 0)]))
    def _():
        wgb_ref[...] = wg_ref[...].astype(BF16)
        wub_ref[...] = wu_ref[...].astype(BF16)
        wdb_ref[...] = wd_ref[...].astype(BF16)

    @pl.when(tv_ref[t] == 1)
    def _():
        xb = _load_packed_rows(xs_ref).astype(BF16)
        g = _dot(xb, wgb_ref[...])
        u = _dot(xb, wub_ref[...])
        h = (g * _sigmoid(g) * u).astype(BF16)
        _store_packed_rows(ys_ref, _dot(h, wdb_ref[...]))


def _moe(tile_expert, tile_valid, tile_block, xs, wg, wu, wd):
    tm = TM_MOE
    nt = tile_expert.shape[0]
    rows = pl.BlockSpec((tm // 8, 32, 128), lambda t, te, tv, tb: (tb[t], 0, 0))
    return pl.pallas_call(
        _moe_kernel,
        grid_spec=pltpu.PrefetchScalarGridSpec(
            num_scalar_prefetch=3,
            grid=(nt,),
            in_specs=[rows,
                      pl.BlockSpec((None, D_MODEL, D_EXPERT), lambda t, te, tv, tb: (te[t], 0, 0)),
                      pl.BlockSpec((None, D_MODEL, D_EXPERT), lambda t, te, tv, tb: (te[t], 0, 0)),
                      pl.BlockSpec((None, D_EXPERT, D_MODEL), lambda t, te, tv, tb: (te[t], 0, 0))],
            out_specs=rows,
            scratch_shapes=[pltpu.VMEM((D_MODEL, D_EXPERT), BF16), pltpu.VMEM((D_MODEL, D_EXPERT), BF16),
                            pltpu.VMEM((D_EXPERT, D_MODEL), BF16)]),
        out_shape=jax.ShapeDtypeStruct((nt * tm // 8, 32, 128), U32),
        compiler_params=pltpu.CompilerParams(dimension_semantics=("arbitrary",), vmem_limit_bytes=VMEM_LIMIT),
        name="moe",
    )(tile_expert, tile_valid, tile_block, xs, wg, wu, wd)


def _final_kernel(y0_ref, y1_ref, x1_ref, p_ref, routet_ref, wple_ref, wpg_ref, g2_ref, b2_ref, *rest):
    out_ref = rest[-1]
    tm = x1_ref.shape[0]
    route = routet_ref[...].T
    h = tm // 2
    halves = (slice(0, h), slice(h, tm))
    x1s = [x1_ref[r, :] for r in halves]
    gates = [_dot(x1.astype(BF16), wpg_ref[...]) for x1 in x1s]
    plins = [_dot(p_ref[r, :].astype(BF16), wple_ref[...]) for r in halves]
    for i, r in enumerate(halves):
        g8 = slice(i * h // 8, (i + 1) * h // 8)
        ffn = (route[r, 2:3] * _load_packed_rows(y0_ref.at[g8])
               + route[r, 3:4] * _load_packed_rows(y1_ref.at[g8]))
        ple = plins[i] * _sigmoid(gates[i])
        out_ref[r, :] = _ln(DEEPNORM_ALPHA * x1s[i] + ffn + ple, g2_ref[...], b2_ref[...])


def _final(yg, x1, p2, routet, wple, wpg, g2, b2, half, prev):
    n = x1.shape[0]
    tm = TM_FIN
    nt = n // 2 // tm
    off = half * nt
    rows = lambda w: pl.BlockSpec((tm, w), lambda t: (t + off, 0))
    full = lambda a: pl.BlockSpec(a.shape, lambda t: (0,) * a.ndim)
    in_specs = [pl.BlockSpec((tm // 8, 32, 128), lambda t: (t, 0, 0)),
                pl.BlockSpec((tm // 8, 32, 128), lambda t: (t + nt, 0, 0)),
                rows(D_MODEL), rows(PLE_DIM), pl.BlockSpec((8, tm), lambda t: (0, t + off)),
                full(wple), full(wpg), full(g2), full(b2)]
    args = [yg, yg, x1, p2, routet, wple, wpg, g2, b2]
    aliases = {}
    if prev is not None:
        in_specs.append(pl.BlockSpec(memory_space=pl.ANY))
        args.append(prev)
        aliases = {len(args) - 1: 0}
    return pl.pallas_call(
        _final_kernel,
        grid=(nt,),
        in_specs=in_specs,
        out_specs=rows(D_MODEL),
        out_shape=jax.ShapeDtypeStruct((n, D_MODEL), F32),
        input_output_aliases=aliases,
        compiler_params=pltpu.CompilerParams(dimension_semantics=("parallel",), vmem_limit_bytes=VMEM_LIMIT),
        name=f"final{half}",
    )(*args)


def _route_tables_kernel(e_ref, piece_ref, ends_ref):
    r = e_ref.shape[0]
    e = e_ref[...]
    ri = lax.broadcasted_iota(I32, (128, 128), 0)
    ci = lax.broadcasted_iota(I32, (128, 128), 1)
    upper = jnp.where(ri <= ci, 1.0, 0.0).astype(BF16)
    rr = lax.broadcasted_iota(I32, (r, r), 0)
    rc = lax.broadcasted_iota(I32, (r, r), 1)
    below = jnp.where(rc < rr, 1.0, 0.0).astype(BF16)
    lane = lax.broadcasted_iota(I32, (1, 128), 1)

    rank = jnp.zeros((r, 128), F32)
    counts = jnp.zeros((1, 128), F32)
    for x in range(N_EXPERTS_TOTAL):
        m = jnp.where(e == x, 1.0, 0.0)
        pre = _dot(m.astype(BF16), upper)
        tot = jnp.broadcast_to(pre[:, 127:128], (r, 128))
        off = _dot(below, tot.astype(BF16))
        rank = rank + m * (pre + off)
        counts = jnp.where(lane == x, off[r - 1:r, :] + tot[r - 1:r, :], counts)
    padded = jnp.floor((counts + (TM_MOE - 1)) * (1.0 / TM_MOE)) * TM_MOE
    ends = _dot(jnp.broadcast_to(padded, (8, 128)).astype(BF16), upper)[0:1, :]
    offs = ends - padded
    ends_ref[...] = jnp.broadcast_to(ends, (8, 128)).astype(I32)

    pos = rank - 1.0
    for x in range(N_EXPERTS_TOTAL):
        pos = pos + jnp.where(e == x, offs[:, x:x + 1], 0.0)

    hi = jnp.floor(pos * (1.0 / 256.0))
    lo = pos - 256.0 * hi
    jv = ((lane % 32) // 8).astype(F32)
    for c in range(4):
        sel = jnp.where(ri == 32 * c + 8 * (ci // 32) + ci % 8, 1.0, 0.0).astype(BF16)
        pc = 256.0 * _dot(hi.astype(BF16), sel) + _dot(lo.astype(BF16), sel)
        p8 = jnp.floor(pc * 0.125)
        piece = p8 * (8.0 * SUBROWS) + (pc - 8.0 * p8) + 8.0 * jv
        piece_ref[pl.ds(c, r, stride=4), :] = piece.astype(I32)


def _routing_tables(routet, n):
    tm = TM_MOE
    nt = (2 * n) // tm + N_EXPERTS_TOTAL
    r = 2 * n // 128
    piece, ends = pl.pallas_call(
        _route_tables_kernel,
        out_shape=[jax.ShapeDtypeStruct((4 * r, 128), I32), jax.ShapeDtypeStruct((8, 128), I32)],
        compiler_params=pltpu.CompilerParams(vmem_limit_bytes=VMEM_LIMIT),
        name="route_tables",
    )(routet[0:2].reshape(r, 128))
    ends = ends[0, :N_EXPERTS_TOTAL]
    tile_start = jnp.arange(nt, dtype=I32) * tm
    tile_expert = jnp.minimum(jnp.sum((tile_start[:, None] >= ends[None, :]).astype(I32), axis=1),
                              N_EXPERTS_TOTAL - 1).astype(I32)
    tile_valid = (tile_start < ends[-1]).astype(I32)
    tile_block = jnp.minimum(jnp.arange(nt, dtype=I32), ends[-1] // tm - 1)
    return tile_expert, tile_valid, tile_block, piece.reshape(2, n * SUBROWS)


def kernel(x, p, w_in, a_ln_g, a_ln_b, a_ws, a_bs, w_a_proj, w_b_proj, w_o, ln1_g, ln1_b, w_group_router,
           b_group_router, w_expert_router, b_expert_router, w_gate, w_up, w_down, w_ple, w_ple_gate,
           ln2_g, ln2_b):
    bsz, s, d = x.shape
    n = bsz * s
    assert d == D_MODEL and s % (SPAN * max(B_DILATIONS)) == 0 and n % TM_PROJ == 0
    assert w_in.shape[0] == 1, "one layer"

    w_in_b = w_in[0].astype(BF16)
    a_bias = jnp.repeat(a_bs[0].T, A_WIDTH // 8, axis=1)

    ga, gates, qkv1, qkv2, qkv3 = _proj(x, w_in_b, a_ln_g, a_ln_b, a_ws[0], a_bias)
    o1, l1 = _attn(qkv1.reshape(bsz, 1, s, 3 * COL), 0)
    o2, l2 = _attn(qkv2, 1)
    o3, l3 = _attn(qkv3, 2)

    pad = 128 - N_GROUPS - N_EXPERTS_TOTAL
    wr = jnp.concatenate([w_group_router[0], w_expert_router[0].reshape(d, N_EXPERTS_TOTAL),
                          jnp.zeros((d, pad), F32)], axis=1).astype(BF16)
    br = jnp.concatenate([b_group_router[0], b_expert_router[0].reshape(-1), jnp.zeros((pad,), F32)])[None, :]
    x1, x1p, routet = _mix(
        ga, gates, o1.reshape(n, B_WIDTH), o2, o3, l1.reshape(n, 128), l2, l3, x.reshape(n, d),
        w_a_proj[0].astype(BF16), w_b_proj[0].astype(BF16), w_o[0].astype(BF16), wr, br, ln1_g, ln1_b)

    tile_expert, tile_valid, tile_block, piece = _routing_tables(routet, n)
    nt = tile_expert.shape[0]
    xs = _sc_scatter_rows(x1p.reshape(n * SUBROWS, 128), piece.reshape(-1), nt * TM_MOE * SUBROWS)
    ys = _moe(tile_expert, tile_valid, tile_block, xs.reshape(nt * TM_MOE // 8, 32, 128),
              w_gate[0].reshape(N_EXPERTS_TOTAL, d, D_EXPERT), w_up[0].reshape(N_EXPERTS_TOTAL, d, D_EXPERT),
              w_down[0].reshape(N_EXPERTS_TOTAL, D_EXPERT, d)).reshape(nt * TM_MOE * SUBROWS, 128)
    out = None
    hp = n * SUBROWS // 2
    for half in range(2):
        yg = _sc_gather_rows(ys, piece[:, half * hp:(half + 1) * hp].reshape(-1))
        out = _final(yg.reshape(n // 8, 32, 128), x1, p[0].reshape(n, PLE_DIM), routet,
                     w_ple[0].astype(BF16), w_ple_gate[0].astype(BF16), ln2_g, ln2_b, half, out)
    return out.reshape(bsz, s, d)
```

```python
import functools

import jax
import jax.numpy as jnp
from jax import lax
from jax.experimental import pallas as pl
from jax.experimental.pallas import tpu as pltpu
from jax.experimental.pallas import tpu_sc as plsc

F32 = jnp.float32
BF16 = jnp.bfloat16
U32 = jnp.uint32
I32 = jnp.int32

D_MODEL = 1024
PLE_DIM = 256
A_WIDTH = 512
A_CHUNK = 128
B_HEAD_DIM = 64
B_HEADS = 8
B_WIDTH = 512
B_DILATIONS = (1, 4, 16)
SPAN = 128
N_GROUPS = 4
N_EXPERTS = 8
N_EXPERTS_TOTAL = N_GROUPS * N_EXPERTS
D_EXPERT = 256
DEEPNORM_ALPHA = 2.0 ** 0.25
LN_EPS = 1e-5
COL = 512
NEG = -1e30

VMEM_LIMIT = 56 * 1024 * 1024

TM_PROJ = 512
TM_MIX = 512
TM_MOE = 512
TM_FIN = 1024


def _ln(x, g, b):
    mu = jnp.mean(x, axis=-1, keepdims=True)
    xc = x - mu
    var = jnp.mean(xc * xc, axis=-1, keepdims=True)
    return xc * lax.rsqrt(var + LN_EPS) * g + b


def _gelu_tanh(x):
    return 0.5 * x * (1.0 + jnp.tanh(0.7978845608028654 * (x + 0.044715 * (x * x * x))))


def _sigmoid(x):
    return 0.5 * jnp.tanh(0.5 * x) + 0.5


def _dot(a, b):
    return jnp.dot(a, b, preferred_element_type=F32)


PACK_W = D_MODEL // 2
SUBROWS = PACK_W // 128


def _store_packed_rows(ref, x):
    m = x.shape[0]
    xb = x.astype(BF16).astype(F32)
    lo = pltpu.bitcast(xb[:, :PACK_W], U32) >> 16
    hi = pltpu.bitcast(xb[:, PACK_W:], U32) & jnp.uint32(0xFFFF0000)
    w = hi | lo
    for j in range(SUBROWS):
        ref[:, 8 * j:8 * (j + 1), :] = w[:, 128 * j:128 * (j + 1)].reshape(m // 8, 8, 128)


def _load_packed_rows(ref):
    m = ref.shape[0] * 8
    ws = [ref[:, 8 * j:8 * (j + 1), :].reshape(m, 128) for j in range(SUBROWS)]
    lo = [pltpu.bitcast(w << 16, F32) for w in ws]
    hi = [pltpu.bitcast(w & jnp.uint32(0xFFFF0000), F32) for w in ws]
    return jnp.concatenate(lo + hi, axis=1)


def _proj_kernel(x_ref, *refs):
    w = refs[:15]
    lng_ref, lnb_ref, ws_ref, bias_ref = refs[15:19]
    ga_ref, gates_ref, qkv1_ref, qkv2_ref, qkv3_ref = refs[19:24]
    xc_ref = refs[24]
    tm = x_ref.shape[0]
    xb = x_ref[...].astype(BF16)

    u = _gelu_tanh(_dot(xb, w[0][...]))
    v = _gelu_tanh(_dot(xb, w[1][...]))
    vn = _ln(v, lng_ref[...], lnb_ref[...]).astype(BF16)

    row = lax.broadcasted_iota(I32, (A_CHUNK, A_CHUNK), 0)
    colm = lax.broadcasted_iota(I32, (A_CHUNK, A_CHUNK), 1)
    causal = colm <= row
    lo = colm < 64
    zero = jnp.zeros((A_CHUNK, A_CHUNK), BF16)
    wcat = []
    for j in range(4):
        w0 = jnp.where(causal, ws_ref[2 * j], 0.0).astype(BF16)
        w1 = jnp.where(causal, ws_ref[2 * j + 1], 0.0).astype(BF16)
        wcat.append(jnp.concatenate([w0, w1], axis=1))
    for c in range(tm // A_CHUNK):
        r0 = c * A_CHUNK
        for j in range(4):
            c0 = j * 128
            vt = vn[r0:r0 + A_CHUNK, c0:c0 + 128]
            rhs = jnp.concatenate([jnp.where(lo, vt, zero), jnp.where(lo, zero, vt)], axis=0)
            mixed = _dot(wcat[j], rhs) + bias_ref[:, c0:c0 + 128]
            ga_ref[r0:r0 + A_CHUNK, c0:c0 + 128] = (u[r0:r0 + A_CHUNK, c0:c0 + 128] * mixed).astype(BF16)

    for i in range(4):
        gates_ref[:, i * COL:(i + 1) * COL] = _sigmoid(_dot(xb, w[11 + i][...])).astype(BF16)
    for j in range(3):
        qkv1_ref[:, j * COL:(j + 1) * COL] = _dot(xb, w[2 + 3 * j][...]).astype(BF16)

    for c in range(D_MODEL // 128):
        xc_ref[c] = x_ref[:, c * 128:(c + 1) * 128]
    for gi, out_ref in ((1, qkv2_ref), (2, qkv3_ref)):
        dl = B_DILATIONS[gi]
        per = tm // dl
        xp = jnp.concatenate(
            [jnp.concatenate([xc_ref[c, pl.ds(r, per, stride=dl), :] for c in range(D_MODEL // 128)], axis=1)
             for r in range(dl)], axis=0).astype(BF16)
        for j in range(3):
            res = _dot(xp, w[2 + 3 * j + gi][...]).astype(BF16)
            for r in range(dl):
                out_ref[r, :, j * COL:(j + 1) * COL] = res[r * per:(r + 1) * per]


def _proj(x, w_in_b, a_ln_g, a_ln_b, a_ws, a_bias):
    bsz, s, _ = x.shape
    n = bsz * s
    tm = TM_PROJ
    tiles = s // tm
    x2 = x.reshape(n, D_MODEL)
    wspec = lambda j: pl.BlockSpec((D_MODEL, COL), lambda i, j=j: (0, j), pipeline_mode=pl.Buffered(1))
    full = lambda shape: pl.BlockSpec(shape, lambda i: (0,) * len(shape))
    rows = lambda width: pl.BlockSpec((tm, width), lambda i: (i, 0))
    dil = lambda dl: pl.BlockSpec((None, dl, tm // dl, 3 * COL), lambda i: (i // tiles, 0, i % tiles, 0))
    return pl.pallas_call(
        _proj_kernel,
        grid=(n // tm,),
        in_specs=[rows(D_MODEL)] + [wspec(j) for j in range(15)]
                 + [full((1, A_WIDTH)), full((1, A_WIDTH)), full((8, A_CHUNK, A_CHUNK)), full((A_CHUNK, A_WIDTH))],
        out_specs=[rows(A_WIDTH), rows(4 * COL), rows(3 * COL), dil(4), dil(16)],
        out_shape=[jax.ShapeDtypeStruct((n, A_WIDTH), BF16),
                   jax.ShapeDtypeStruct((n, 4 * COL), BF16),
                   jax.ShapeDtypeStruct((n, 3 * COL), BF16),
                   jax.ShapeDtypeStruct((bsz, 4, s // 4, 3 * COL), BF16),
                   jax.ShapeDtypeStruct((bsz, 16, s // 16, 3 * COL), BF16)],
        scratch_shapes=[pltpu.VMEM((D_MODEL // 128, tm, 128), F32)],
        compiler_params=pltpu.CompilerParams(dimension_semantics=("parallel",), vmem_limit_bytes=VMEM_LIMIT),
        name="proj",
    )(x2, *([w_in_b] * 15), a_ln_g, a_ln_b, a_ws, a_bias)


def _attn_kernel(qkv_ref, o_ref, lse_ref, *, ns, seq):
    nb = seq // SPAN
    lane = lax.broadcasted_iota(I32, (SPAN, 128), 1)
    lo = lane < 64
    lane16 = lane // 16
    qi = lax.broadcasted_iota(I32, (SPAN, 2 * SPAN), 0)
    ki = lax.broadcasted_iota(I32, (SPAN, 2 * SPAN), 1)
    causal = lax.broadcasted_iota(I32, (SPAN, SPAN), 1) <= lax.broadcasted_iota(I32, (SPAN, SPAN), 0)
    bias_first = jnp.where(causal, 0.0, NEG).astype(F32)
    bias_first = jnp.concatenate([bias_first, bias_first], axis=0)
    bias_main = jnp.where((ki >= qi) & (ki <= qi + SPAN), 0.0, NEG).astype(F32)
    bias_main = jnp.concatenate([bias_main, bias_main], axis=0)
    zero = jnp.zeros((SPAN, 128), BF16)

    for s in range(ns):
        def block(row0, start, bias, s=s):
            win = bias.shape[1]
            pairs = range(B_HEADS // 2)
            scores, values = [], []
            for jp in pairs:
                c0 = jp * 128
                q = qkv_ref[s, pl.ds(row0, SPAN), c0:c0 + 128] * jnp.asarray(0.125, BF16)
                k = qkv_ref[s, pl.ds(start, win), COL + c0:COL + c0 + 128]
                values.append(qkv_ref[s, pl.ds(start, win), 2 * COL + c0:2 * COL + c0 + 128])
                qs = jnp.concatenate([jnp.where(lo, q, zero), jnp.where(lo, zero, q)], axis=0)
                scores.append(lax.dot_general(qs, k, (((1,), (1,)), ((), ())), preferred_element_type=F32) + bias)
            probs, maxes, sums = [], [], []
            for jp in pairs:
                m = jnp.max(scores[jp], axis=-1, keepdims=True)
                p = jnp.exp(scores[jp] - m)
                maxes.append(m)
                sums.append(jnp.sum(p, axis=-1, keepdims=True))
                probs.append(p.astype(BF16))
            lse_tile = jnp.zeros((SPAN, 128), F32)
            for jp in pairs:
                c0 = jp * 128
                ov = _dot(probs[jp], values[jp])
                inv = 1.0 / sums[jp]
                o = jnp.where(lo, ov[:SPAN] * inv[:SPAN], ov[SPAN:] * inv[SPAN:])
                o_ref[pl.ds(row0, SPAN), s * B_WIDTH + c0:s * B_WIDTH + c0 + 128] = o.astype(BF16)
                lse = maxes[jp] + jnp.log(sums[jp])
                lse_tile = jnp.where(lane16 == 2 * jp, lse[:SPAN],
                                     jnp.where(lane16 == 2 * jp + 1, lse[SPAN:], lse_tile))
            lse_ref[pl.ds(row0, SPAN), s * 128:(s + 1) * 128] = lse_tile

        block(0, 0, bias_first)
        if nb > 1:
            def body(i, carry):
                block(pl.multiple_of(i * SPAN, SPAN), pl.multiple_of((i - 1) * SPAN, SPAN), bias_main)
                return carry
            lax.fori_loop(1, nb, body, 0)


def _attn(qkv_g, gi):
    bsz, dl, seq, _ = qkv_g.shape
    ns = max(1, min(dl, TM_PROJ // seq))
    return pl.pallas_call(
        functools.partial(_attn_kernel, ns=ns, seq=seq),
        grid=(bsz, dl // ns),
        in_specs=[pl.BlockSpec((None, ns, seq, 3 * COL), lambda b, r: (b, r, 0, 0))],
        out_specs=[pl.BlockSpec((None, seq, ns * B_WIDTH), lambda b, r: (b, 0, r)),
                   pl.BlockSpec((None, seq, ns * 128), lambda b, r: (b, 0, r))],
        out_shape=[jax.ShapeDtypeStruct((bsz, seq, dl * B_WIDTH), BF16),
                   jax.ShapeDtypeStruct((bsz, seq, dl * 128), F32)],
        compiler_params=pltpu.CompilerParams(dimension_semantics=("parallel", "parallel"),
                                             vmem_limit_bytes=VMEM_LIMIT),
        name=f"attn{dl}",
    )(qkv_g)


def _natural_rows(ref, dl, scr):
    nchunk, tm, _ = scr.shape
    w = nchunk * 128
    per = tm // dl
    for r in range(dl):
        for c in range(nchunk):
            scr[c, pl.ds(r, per, stride=dl), :] = ref[:, r * w + c * 128:r * w + (c + 1) * 128].astype(F32)
    return jnp.concatenate([scr[c] for c in range(nchunk)], axis=1)


def _mix_kernel(ga_ref, gates_ref, o1_ref, o2_ref, o3_ref, l1_ref, l2_ref, l3_ref, x_ref,
                wa_ref, wb_ref, wo_ref, wr_ref, br_ref, g1_ref, b1_ref,
                x1_ref, x1p_ref, routet_ref, o2s_ref, o3s_ref, l2s_ref, l3s_ref):
    tm = x_ref.shape[0]
    o2 = _natural_rows(o2_ref, 4, o2s_ref)
    o3 = _natural_rows(o3_ref, 16, o3s_ref)
    l2 = _natural_rows(l2_ref, 4, l2s_ref)
    l3 = _natural_rows(l3_ref, 16, l3s_ref)
    er = lax.broadcasted_iota(I32, (256, B_WIDTH), 0)
    ec = lax.broadcasted_iota(I32, (256, B_WIDTH), 1)
    expand = jnp.where(er % 128 == (ec // B_HEAD_DIM) * 16, 1.0, 0.0).astype(BF16)

    def widen(w):
        hi = w.astype(BF16)
        lo = (w - hi.astype(F32)).astype(BF16)
        return _dot(jnp.concatenate([hi, lo], axis=1), expand)

    h = tm // 2
    halves = (slice(0, h), slice(h, tm))
    obs = []
    for r in halves:
        l1 = l1_ref[r, :]
        mx = jnp.maximum(l1, jnp.maximum(l2[r], l3[r]))
        e1, e2, e3 = jnp.exp(l1 - mx), jnp.exp(l2[r] - mx), jnp.exp(l3[r] - mx)
        inv = 1.0 / (e1 + e2 + e3)
        obs.append(widen(e1 * inv) * o1_ref[r, :].astype(F32) + widen(e2 * inv) * o2[r] + widen(e3 * inv) * o3[r])
    ybs = [_dot(ob.astype(BF16), wb_ref[...]) for ob in obs]
    yas = [_dot(ga_ref[r, :], wa_ref[...]) for r in halves]
    pres = [gates_ref[r, :D_MODEL].astype(F32) * ya + gates_ref[r, D_MODEL:].astype(F32) * yb
            for r, ya, yb in zip(halves, yas, ybs)]
    mixes = [_dot(pre.astype(BF16), wo_ref[...]) for pre in pres]
    x1s = [_ln(DEEPNORM_ALPHA * x_ref[r, :] + mix, g1_ref[...], b1_ref[...]) for r, mix in zip(halves, mixes)]
    logits = [_dot(x1.astype(BF16), wr_ref[...]) + br_ref[...] for x1 in x1s]

    lane = lax.broadcasted_iota(I32, (h, 128), 1).astype(F32)
    big = 1e9
    for i, r in enumerate(halves):
        x1_ref[r, :] = x1s[i]
        _store_packed_rows(x1p_ref.at[i * h // 8:(i + 1) * h // 8], x1s[i])
        lg = logits[i]
        gl = jnp.where(lane < N_GROUPS, lg, NEG)
        gm = jnp.max(gl, axis=-1, keepdims=True)
        gidx = jnp.min(jnp.where(gl == gm, lane, big), axis=-1, keepdims=True)
        gsum = jnp.sum(jnp.where(lane < N_GROUPS, jnp.exp(gl - gm), 0.0), axis=-1, keepdims=True)
        gprob = 1.0 / gsum
        lo_lane = N_GROUPS + N_EXPERTS * gidx
        el = jnp.where((lane >= lo_lane) & (lane < lo_lane + N_EXPERTS), lg, NEG)
        v1 = jnp.max(el, axis=-1, keepdims=True)
        i1 = jnp.min(jnp.where(el == v1, lane, big), axis=-1, keepdims=True)
        el2 = jnp.where(lane == i1, NEG, el)
        v2 = jnp.max(el2, axis=-1, keepdims=True)
        i2 = jnp.min(jnp.where(el2 == v2, lane, big), axis=-1, keepdims=True)
        t = jnp.exp(v2 - v1)
        w1 = 1.0 / (1.0 + t)
        w2 = t * w1
        route = jnp.where(lane == 0, i1 - N_GROUPS,
                          jnp.where(lane == 1, i2 - N_GROUPS,
                                    jnp.where(lane == 2, gprob * w1,
                                              jnp.where(lane == 3, gprob * w2, 0.0))))
        routet_ref[:, r] = route.T[:8, :]


def _mix(ga, gates, o1, o2, o3, l1, l2, l3, x2, wa, wb, wo, wr, br, g1, b1):
    n = x2.shape[0]
    bsz = o2.shape[0]
    tm = TM_MIX
    tiles = n // bsz // tm
    rows = lambda w: pl.BlockSpec((tm, w), lambda i: (i, 0))
    grouped = lambda a, dl: pl.BlockSpec((None, tm // dl, a.shape[2]), lambda i: (i // tiles, i % tiles, 0))
    full = lambda a: pl.BlockSpec(a.shape, lambda i: (0,) * a.ndim)
    return pl.pallas_call(
        _mix_kernel,
        grid=(n // tm,),
        in_specs=[rows(A_WIDTH), rows(2 * D_MODEL), rows(B_WIDTH), grouped(o2, 4), grouped(o3, 16),
                  rows(128), grouped(l2, 4), grouped(l3, 16), rows(D_MODEL),
                  full(wa), full(wb), full(wo), full(wr), full(br), full(g1), full(b1)],
        out_specs=[rows(D_MODEL), pl.BlockSpec((tm // 8, 32, 128), lambda i: (i, 0, 0)),
                   pl.BlockSpec((8, tm), lambda i: (0, i))],
        out_shape=[jax.ShapeDtypeStruct((n, D_MODEL), F32),
                   jax.ShapeDtypeStruct((n // 8, 32, 128), U32),
                   jax.ShapeDtypeStruct((8, n), F32)],
        scratch_shapes=[pltpu.VMEM((B_WIDTH // 128, tm, 128), F32), pltpu.VMEM((B_WIDTH // 128, tm, 128), F32),
                        pltpu.VMEM((1, tm, 128), F32), pltpu.VMEM((1, tm, 128), F32)],
        compiler_params=pltpu.CompilerParams(dimension_semantics=("parallel",), vmem_limit_bytes=VMEM_LIMIT),
        name="mix",
    )(ga, gates, o1, o2, o3, l1, l2, l3, x2, wa, wb, wo, wr, br, g1, b1)


SC_WINDOW = 128


def _sc_mesh():
    return plsc.VectorSubcoreMesh(core_axis_name="core", subcore_axis_name="subcore")


def _sc_scatter_rows(rows, dst, n_out):
    r = rows.shape[0]

    @pl.kernel(out_type=jax.ShapeDtypeStruct((n_out, 128), rows.dtype), mesh=_sc_mesh())
    def scatter(rows_hbm, dst0_hbm, dst1_hbm, out_hbm):
        def body(rows_vmem, dst0_vmem, dst1_vmem):
            pltpu.sync_copy(rows_vmem, out_hbm.at[dst0_vmem.at[0]])
            pltpu.sync_copy(rows_vmem, out_hbm.at[dst1_vmem.at[0]])

        pltpu.emit_pipeline(
            body,
            grid=(r // SC_WINDOW,),
            in_specs=[pl.BlockSpec((SC_WINDOW, 128), lambda i: (i, 0)),
                      pl.BlockSpec((1, SC_WINDOW), lambda i: (0, i)),
                      pl.BlockSpec((1, SC_WINDOW), lambda i: (0, i))],
            out_specs=[],
            core_axis_name=("core", "subcore"),
            dimension_semantics=(pltpu.PARALLEL,),
        )(rows_hbm, dst0_hbm, dst1_hbm)

    return scatter(rows, dst[0:1], dst[1:2])


def _sc_gather_rows(table, src):
    m = src.shape[0]

    @pl.kernel(out_type=jax.ShapeDtypeStruct((m, 128), table.dtype), mesh=_sc_mesh())
    def gather(table_hbm, src_hbm, out_hbm):
        def body(src_vmem, out_vmem):
            pltpu.sync_copy(table_hbm.at[src_vmem.at[0]], out_vmem)

        pltpu.emit_pipeline(
            body,
            grid=(m // SC_WINDOW,),
            in_specs=[pl.BlockSpec((1, SC_WINDOW), lambda i: (0, i))],
            out_specs=[pl.BlockSpec((SC_WINDOW, 128), lambda i: (i, 0))],
            core_axis_name=("core", "subcore"),
            dimension_semantics=(pltpu.PARALLEL,),
        )(src_hbm, out_hbm)

    return gather(table, src.reshape(1, m))


def _moe_kernel(te_ref, tv_ref, tb_ref, xs_ref, wg_ref, wu_ref, wd_ref, ys_ref, wgb_ref, wub_ref, wdb_ref):
    del tb_ref
    t = pl.program_id(0)

    @pl.when((t == 0) | (te_ref[t] != te_ref[jnp.maximum(t - 1, 0)]))
    def _():
        wgb_ref[...] = wg_ref[...].astype(BF16)
        wub_ref[...] = wu_ref[...].astype(BF16)
        wdb_ref[...] = wd_ref[...].astype(BF16)

    @pl.when(tv_ref[t] == 1)
    def _():
        xb = _load_packed_rows(xs_ref).astype(BF16)
        g = _dot(xb, wgb_ref[...])
        u = _dot(xb, wub_ref[...])
        h = (g * _sigmoid(g) * u).astype(BF16)
        _store_packed_rows(ys_ref, _dot(h, wdb_ref[...]))


def _moe(tile_expert, tile_valid, tile_block, xs, wg, wu, wd):
    tm = TM_MOE
    nt = tile_expert.shape[0]
    rows = pl.BlockSpec((tm // 8, 32, 128), lambda t, te, tv, tb: (tb[t], 0, 0))
    return pl.pallas_call(
        _moe_kernel,
        grid_spec=pltpu.PrefetchScalarGridSpec(
            num_scalar_prefetch=3,
            grid=(nt,),
            in_specs=[rows,
                      pl.BlockSpec((None, D_MODEL, D_EXPERT), lambda t, te, tv, tb: (te[t], 0, 0)),
                      pl.BlockSpec((None, D_MODEL, D_EXPERT), lambda t, te, tv, tb: (te[t], 0, 0)),
                      pl.BlockSpec((None, D_EXPERT, D_MODEL), lambda t, te, tv, tb: (te[t], 0, 0))],
            out_specs=rows,
            scratch_shapes=[pltpu.VMEM((D_MODEL, D_EXPERT), BF16), pltpu.VMEM((D_MODEL, D_EXPERT), BF16),
                            pltpu.VMEM((D_EXPERT, D_MODEL), BF16)]),
        out_shape=jax.ShapeDtypeStruct((nt * tm // 8, 32, 128), U32),
        compiler_params=pltpu.CompilerParams(dimension_semantics=("arbitrary",), vmem_limit_bytes=VMEM_LIMIT),
        name="moe",
    )(tile_expert, tile_valid, tile_block, xs, wg, wu, wd)


def _final_kernel(y0_ref, y1_ref, x1_ref, p_ref, routet_ref, wple_ref, wpg_ref, g2_ref, b2_ref, *rest):
    out_ref = rest[-1]
    tm = x1_ref.shape[0]
    route = routet_ref[...].T
    h = tm // 2
    halves = (slice(0, h), slice(h, tm))
    x1s = [x1_ref[r, :] for r in halves]
    gates = [_dot(x1.astype(BF16), wpg_ref[...]) for x1 in x1s]
    plins = [_dot(p_ref[r, :].astype(BF16), wple_ref[...]) for r in halves]
    for i, r in enumerate(halves):
        g8 = slice(i * h // 8, (i + 1) * h // 8)
        ffn = (route[r, 2:3] * _load_packed_rows(y0_ref.at[g8])
               + route[r, 3:4] * _load_packed_rows(y1_ref.at[g8]))
        ple = plins[i] * _sigmoid(gates[i])
        out_ref[r, :] = _ln(DEEPNORM_ALPHA * x1s[i] + ffn + ple, g2_ref[...], b2_ref[...])


def _final(yg, x1, p2, routet, wple, wpg, g2, b2, half, prev):
    n = x1.shape[0]
    tm = TM_FIN
    nt = n // 2 // tm
    off = half * nt
    rows = lambda w: pl.BlockSpec((tm, w), lambda t: (t + off, 0))
    full = lambda a: pl.BlockSpec(a.shape, lambda t: (0,) * a.ndim)
    in_specs = [pl.BlockSpec((tm // 8, 32, 128), lambda t: (t, 0, 0)),
                pl.BlockSpec((tm // 8, 32, 128), lambda t: (t + nt, 0, 0)),
                rows(D_MODEL), rows(PLE_DIM), pl.BlockSpec((8, tm), lambda t: (0, t + off)),
                full(wple), full(wpg), full(g2), full(b2)]
    args = [yg, yg, x1, p2, routet, wple, wpg, g2, b2]
    aliases = {}
    if prev is not None:
        in_specs.append(pl.BlockSpec(memory_space=pl.ANY))
        args.append(prev)
        aliases = {len(args) - 1: 0}
    return pl.pallas_call(
        _final_kernel,
        grid=(nt,),
        in_specs=in_specs,
        out_specs=rows(D_MODEL),
        out_shape=jax.ShapeDtypeStruct((n, D_MODEL), F32),
        input_output_aliases=aliases,
        compiler_params=pltpu.CompilerParams(dimension_semantics=("parallel",), vmem_limit_bytes=VMEM_LIMIT),
        name=f"final{half}",
    )(*args)


def _route_tables_kernel(e_ref, piece_ref, ends_ref):
    r = e_ref.shape[0]
    e = e_ref[...]
    ri = lax.broadcasted_iota(I32, (128, 128), 0)
    ci = lax.broadcasted_iota(I32, (128, 128), 1)
    upper = jnp.where(ri <= ci, 1.0, 0.0).astype(BF16)
    rr = lax.broadcasted_iota(I32, (r, r), 0)
    rc = lax.broadcasted_iota(I32, (r, r), 1)
    below = jnp.where(rc < rr, 1.0, 0.0).astype(BF16)
    lane = lax.broadcasted_iota(I32, (1, 128), 1)

    rank = jnp.zeros((r, 128), F32)
    counts = jnp.zeros((1, 128), F32)
    for x in range(N_EXPERTS_TOTAL):
        m = jnp.where(e == x, 1.0, 0.0)
        pre = _dot(m.astype(BF16), upper)
        tot = jnp.broadcast_to(pre[:, 127:128], (r, 128))
        off = _dot(below, tot.astype(BF16))
        rank = rank + m * (pre + off)
        counts = jnp.where(lane == x, off[r - 1:r, :] + tot[r - 1:r, :], counts)
    padded = jnp.floor((counts + (TM_MOE - 1)) * (1.0 / TM_MOE)) * TM_MOE
    ends = _dot(jnp.broadcast_to(padded, (8, 128)).astype(BF16), upper)[0:1, :]
    offs = ends - padded
    ends_ref[...] = jnp.broadcast_to(ends, (8, 128)).astype(I32)

    pos = rank - 1.0
    for x in range(N_EXPERTS_TOTAL):
        pos = pos + jnp.where(e == x, offs[:, x:x + 1], 0.0)

    hi = jnp.floor(pos * (1.0 / 256.0))
    lo = pos - 256.0 * hi
    jv = ((lane % 32) // 8).astype(F32)
    for c in range(4):
        sel = jnp.where(ri == 32 * c + 8 * (ci // 32) + ci % 8, 1.0, 0.0).astype(BF16)
        pc = 256.0 * _dot(hi.astype(BF16), sel) + _dot(lo.astype(BF16), sel)
        p8 = jnp.floor(pc * 0.125)
        piece = p8 * (8.0 * SUBROWS) + (pc - 8.0 * p8) + 8.0 * jv
        piece_ref[pl.ds(c, r, stride=4), :] = piece.astype(I32)


def _routing_tables(routet, n):
    tm = TM_MOE
    nt = (2 * n) // tm + N_EXPERTS_TOTAL
    r = 2 * n // 128
    piece, ends = pl.pallas_call(
        _route_tables_kernel,
        out_shape=[jax.ShapeDtypeStruct((4 * r, 128), I32), jax.ShapeDtypeStruct((8, 128), I32)],
        compiler_params=pltpu.CompilerParams(vmem_limit_bytes=VMEM_LIMIT),
        name="route_tables",
    )(routet[0:2].reshape(r, 128))
    ends = ends[0, :N_EXPERTS_TOTAL]
    tile_start = jnp.arange(nt, dtype=I32) * tm
    tile_expert = jnp.minimum(jnp.sum((tile_start[:, None] >= ends[None, :]).astype(I32), axis=1),
                              N_EXPERTS_TOTAL - 1).astype(I32)
    tile_valid = (tile_start < ends[-1]).astype(I32)
    tile_block = jnp.minimum(jnp.arange(nt, dtype=I32), ends[-1] // tm - 1)
    return tile_expert, tile_valid, tile_block, piece.reshape(2, n * SUBROWS)


def kernel(x, p, w_in, a_ln_g, a_ln_b, a_ws, a_bs, w_a_proj, w_b_proj, w_o, ln1_g, ln1_b, w_group_router,
           b_group_router, w_expert_router, b_expert_router, w_gate, w_up, w_down, w_ple, w_ple_gate,
           ln2_g, ln2_b):
    bsz, s, d = x.shape
    n = bsz * s
    assert d == D_MODEL and s % (SPAN * max(B_DILATIONS)) == 0 and n % TM_PROJ == 0
    assert w_in.shape[0] == 1, "one layer"

    w_in_b = w_in[0].astype(BF16)
    a_bias = jnp.repeat(a_bs[0].T, A_WIDTH // 8, axis=1)

    ga, gates, qkv1, qkv2, qkv3 = _proj(x, w_in_b, a_ln_g, a_ln_b, a_ws[0], a_bias)
    o1, l1 = _attn(qkv1.reshape(bsz, 1, s, 3 * COL), 0)
    o2, l2 = _attn(qkv2, 1)
    o3, l3 = _attn(qkv3, 2)

    pad = 128 - N_GROUPS - N_EXPERTS_TOTAL
    wr = jnp.concatenate([w_group_router[0], w_expert_router[0].reshape(d, N_EXPERTS_TOTAL),
                          jnp.zeros((d, pad), F32)], axis=1).astype(BF16)
    br = jnp.concatenate([b_group_router[0], b_expert_router[0].reshape(-1), jnp.zeros((pad,), F32)])[None, :]
    x1, x1p, routet = _mix(
        ga, gates, o1.reshape(n, B_WIDTH), o2, o3, l1.reshape(n, 128), l2, l3, x.reshape(n, d),
        w_a_proj[0].astype(BF16), w_b_proj[0].astype(BF16), w_o[0].astype(BF16), wr, br, ln1_g, ln1_b)

    tile_expert, tile_valid, tile_block, piece = _routing_tables(routet, n)
    nt = tile_expert.shape[0]
    xs = _sc_scatter_rows(x1p.reshape(n * SUBROWS, 128), piece, nt * TM_MOE * SUBROWS)
    ys = _moe(tile_expert, tile_valid, tile_block, xs.reshape(nt * TM_MOE // 8, 32, 128),
              w_gate[0].reshape(N_EXPERTS_TOTAL, d, D_EXPERT), w_up[0].reshape(N_EXPERTS_TOTAL, d, D_EXPERT),
              w_down[0].reshape(N_EXPERTS_TOTAL, D_EXPERT, d)).reshape(nt * TM_MOE * SUBROWS, 128)
    out = None
    hp = n * SUBROWS // 2
    for half in range(2):
        yg = _sc_gather_rows(ys, piece[:, half * hp:(half + 1) * hp].reshape(-1))
        out = _final(yg.reshape(n // 8, 32, 128), x1, p[0].reshape(n, PLE_DIM), routet,
                     w_ple[0].astype(BF16), w_ple_gate[0].astype(BF16), ln2_g, ln2_b, half, out)
    return out.reshape(bsz, s, d)
```

```python
import functools

import jax
import jax.numpy as jnp
from jax import lax
from jax.experimental import pallas as pl
from jax.experimental.pallas import tpu as pltpu
from jax.experimental.pallas import tpu_sc as plsc

F32 = jnp.float32
BF16 = jnp.bfloat16
U32 = jnp.uint32
I32 = jnp.int32

D_MODEL = 1024
PLE_DIM = 256
A_WIDTH = 512
A_CHUNK = 128
B_HEAD_DIM = 64
B_HEADS = 8
B_WIDTH = 512
B_DILATIONS = (1, 4, 16)
SPAN = 128
N_GROUPS = 4
N_EXPERTS = 8
N_EXPERTS_TOTAL = N_GROUPS * N_EXPERTS
D_EXPERT = 256
DEEPNORM_ALPHA = 2.0 ** 0.25
LN_EPS = 1e-5
COL = 512
NEG = -1e30

VMEM_LIMIT = 56 * 1024 * 1024

TM_PROJ = 512
TM_MIX = 512
TM_MOE = 512
TM_FIN = 1024


def _ln(x, g, b):
    mu = jnp.mean(x, axis=-1, keepdims=True)
    xc = x - mu
    var = jnp.mean(xc * xc, axis=-1, keepdims=True)
    return xc * lax.rsqrt(var + LN_EPS) * g + b


def _gelu_tanh(x):
    return 0.5 * x * (1.0 + jnp.tanh(0.7978845608028654 * (x + 0.044715 * (x * x * x))))


def _sigmoid(x):
    return 0.5 * jnp.tanh(0.5 * x) + 0.5


def _dot(a, b):
    return jnp.dot(a, b, preferred_element_type=F32)


PACK_W = D_MODEL // 2
SUBROWS = PACK_W // 128


def _store_packed_rows(ref, x):
    m = x.shape[0]
    xb = x.astype(BF16).astype(F32)
    lo = pltpu.bitcast(xb[:, :PACK_W], U32) >> 16
    hi = pltpu.bitcast(xb[:, PACK_W:], U32) & jnp.uint32(0xFFFF0000)
    w = hi | lo
    for j in range(SUBROWS):
        ref[:, 8 * j:8 * (j + 1), :] = w[:, 128 * j:128 * (j + 1)].reshape(m // 8, 8, 128)


def _load_packed_rows(ref):
    m = ref.shape[0] * 8
    ws = [ref[:, 8 * j:8 * (j + 1), :].reshape(m, 128) for j in range(SUBROWS)]
    lo = [pltpu.bitcast(w << 16, F32) for w in ws]
    hi = [pltpu.bitcast(w & jnp.uint32(0xFFFF0000), F32) for w in ws]
    return jnp.concatenate(lo + hi, axis=1)


def _proj_kernel(x_ref, *refs):
    w = refs[:15]
    lng_ref, lnb_ref, ws_ref, bias_ref = refs[15:19]
    ga_ref, gates_ref, qkv1_ref, qkv2_ref, qkv3_ref = refs[19:24]
    xc_ref = refs[24]
    tm = x_ref.shape[0]
    xb = x_ref[...].astype(BF16)

    u_raw = _dot(xb, w[0][...])
    v_raw = _dot(xb, w[1][...])

    for i in range(4):
        gates_ref[:, i * COL:(i + 1) * COL] = _sigmoid(_dot(xb, w[11 + i][...])).astype(BF16)
    for j in range(3):
        qkv1_ref[:, j * COL:(j + 1) * COL] = _dot(xb, w[2 + 3 * j][...]).astype(BF16)

    for c in range(D_MODEL // 128):
        xc_ref[c] = x_ref[:, c * 128:(c + 1) * 128]
    for gi, out_ref in ((1, qkv2_ref), (2, qkv3_ref)):
        dl = B_DILATIONS[gi]
        per = tm // dl
        xp = jnp.concatenate(
            [jnp.concatenate([xc_ref[c, pl.ds(r, per, stride=dl), :] for c in range(D_MODEL // 128)], axis=1)
             for r in range(dl)], axis=0).astype(BF16)
        for j in range(3):
            res = _dot(xp, w[2 + 3 * j + gi][...]).astype(BF16)
            for r in range(dl):
                out_ref[r, :, j * COL:(j + 1) * COL] = res[r * per:(r + 1) * per]

    u = _gelu_tanh(u_raw)
    v = _gelu_tanh(v_raw)
    vn = _ln(v, lng_ref[...], lnb_ref[...]).astype(BF16)

    row = lax.broadcasted_iota(I32, (A_CHUNK, A_CHUNK), 0)
    colm = lax.broadcasted_iota(I32, (A_CHUNK, A_CHUNK), 1)
    causal = colm <= row
    lo = colm < 64
    zero = jnp.zeros((A_CHUNK, A_CHUNK), BF16)
    wcat = []
    for j in range(4):
        w0 = jnp.where(causal, ws_ref[2 * j], 0.0).astype(BF16)
        w1 = jnp.where(causal, ws_ref[2 * j + 1], 0.0).astype(BF16)
        wcat.append(jnp.concatenate([w0, w1], axis=1))
    for c in range(tm // A_CHUNK):
        r0 = c * A_CHUNK
        for j in range(4):
            c0 = j * 128
            vt = vn[r0:r0 + A_CHUNK, c0:c0 + 128]
            rhs = jnp.concatenate([jnp.where(lo, vt, zero), jnp.where(lo, zero, vt)], axis=0)
            mixed = _dot(wcat[j], rhs) + bias_ref[:, c0:c0 + 128]
            ga_ref[r0:r0 + A_CHUNK, c0:c0 + 128] = (u[r0:r0 + A_CHUNK, c0:c0 + 128] * mixed).astype(BF16)


def _proj(x, w_in_b, a_ln_g, a_ln_b, a_ws, a_bias):
    bsz, s, _ = x.shape
    n = bsz * s
    tm = TM_PROJ
    tiles = s // tm
    x2 = x.reshape(n, D_MODEL)
    wspec = lambda j: pl.BlockSpec((D_MODEL, COL), lambda i, j=j: (0, j), pipeline_mode=pl.Buffered(1))
    full = lambda shape: pl.BlockSpec(shape, lambda i: (0,) * len(shape))
    rows = lambda width: pl.BlockSpec((tm, width), lambda i: (i, 0))
    dil = lambda dl: pl.BlockSpec((None, dl, tm // dl, 3 * COL), lambda i: (i // tiles, 0, i % tiles, 0))
    return pl.pallas_call(
        _proj_kernel,
        grid=(n // tm,),
        in_specs=[rows(D_MODEL)] + [wspec(j) for j in range(15)]
                 + [full((1, A_WIDTH)), full((1, A_WIDTH)), full((8, A_CHUNK, A_CHUNK)), full((A_CHUNK, A_WIDTH))],
        out_specs=[rows(A_WIDTH), rows(4 * COL), rows(3 * COL), dil(4), dil(16)],
        out_shape=[jax.ShapeDtypeStruct((n, A_WIDTH), BF16),
                   jax.ShapeDtypeStruct((n, 4 * COL), BF16),
                   jax.ShapeDtypeStruct((n, 3 * COL), BF16),
                   jax.ShapeDtypeStruct((bsz, 4, s // 4, 3 * COL), BF16),
                   jax.ShapeDtypeStruct((bsz, 16, s // 16, 3 * COL), BF16)],
        scratch_shapes=[pltpu.VMEM((D_MODEL // 128, tm, 128), F32)],
        compiler_params=pltpu.CompilerParams(dimension_semantics=("parallel",), vmem_limit_bytes=VMEM_LIMIT),
        name="proj",
    )(x2, *([w_in_b] * 15), a_ln_g, a_ln_b, a_ws, a_bias)


def _attn_kernel(qkv_ref, o_ref, lse_ref, *, ns, seq):
    nb = seq // SPAN
    lane = lax.broadcasted_iota(I32, (SPAN, 128), 1)
    lo = lane < 64
    lane16 = lane // 16
    qi = lax.broadcasted_iota(I32, (SPAN, 2 * SPAN), 0)
    ki = lax.broadcasted_iota(I32, (SPAN, 2 * SPAN), 1)
    causal = lax.broadcasted_iota(I32, (SPAN, SPAN), 1) <= lax.broadcasted_iota(I32, (SPAN, SPAN), 0)
    bias_first = jnp.where(causal, 0.0, NEG).astype(F32)
    bias_first = jnp.concatenate([bias_first, bias_first], axis=0)
    bias_main = jnp.where((ki >= qi) & (ki <= qi + SPAN), 0.0, NEG).astype(F32)
    bias_main = jnp.concatenate([bias_main, bias_main], axis=0)
    zero = jnp.zeros((SPAN, 128), BF16)

    for s in range(ns):
        def block(row0, start, bias, s=s):
            win = bias.shape[1]
            pairs = range(B_HEADS // 2)
            scores, values = [], []
            for jp in pairs:
                c0 = jp * 128
                q = qkv_ref[s, pl.ds(row0, SPAN), c0:c0 + 128] * jnp.asarray(0.125, BF16)
                k = qkv_ref[s, pl.ds(start, win), COL + c0:COL + c0 + 128]
                values.append(qkv_ref[s, pl.ds(start, win), 2 * COL + c0:2 * COL + c0 + 128])
                qs = jnp.concatenate([jnp.where(lo, q, zero), jnp.where(lo, zero, q)], axis=0)
                scores.append(lax.dot_general(qs, k, (((1,), (1,)), ((), ())), preferred_element_type=F32) + bias)
            probs, maxes, sums = [], [], []
            for jp in pairs:
                m = jnp.max(scores[jp], axis=-1, keepdims=True)
                p = jnp.exp(scores[jp] - m)
                maxes.append(m)
                sums.append(jnp.sum(p, axis=-1, keepdims=True))
                probs.append(p.astype(BF16))
            lse_tile = jnp.zeros((SPAN, 128), F32)
            for jp in pairs:
                c0 = jp * 128
                ov = _dot(probs[jp], values[jp])
                inv = 1.0 / sums[jp]
                o = jnp.where(lo, ov[:SPAN] * inv[:SPAN], ov[SPAN:] * inv[SPAN:])
                o_ref[pl.ds(row0, SPAN), s * B_WIDTH + c0:s * B_WIDTH + c0 + 128] = o.astype(BF16)
                lse = maxes[jp] + jnp.log(sums[jp])
                lse_tile = jnp.where(lane16 == 2 * jp, lse[:SPAN],
                                     jnp.where(lane16 == 2 * jp + 1, lse[SPAN:], lse_tile))
            lse_ref[pl.ds(row0, SPAN), s * 128:(s + 1) * 128] = lse_tile

        block(0, 0, bias_first)
        if nb > 1:
            def body(i, carry):
                block(pl.multiple_of(i * SPAN, SPAN), pl.multiple_of((i - 1) * SPAN, SPAN), bias_main)
                return carry
            lax.fori_loop(1, nb, body, 0)


def _attn(qkv_g, gi):
    bsz, dl, seq, _ = qkv_g.shape
    ns = max(1, min(dl, TM_PROJ // seq))
    return pl.pallas_call(
        functools.partial(_attn_kernel, ns=ns, seq=seq),
        grid=(bsz, dl // ns),
        in_specs=[pl.BlockSpec((None, ns, seq, 3 * COL), lambda b, r: (b, r, 0, 0))],
        out_specs=[pl.BlockSpec((None, seq, ns * B_WIDTH), lambda b, r: (b, 0, r)),
                   pl.BlockSpec((None, seq, ns * 128), lambda b, r: (b, 0, r))],
        out_shape=[jax.ShapeDtypeStruct((bsz, seq, dl * B_WIDTH), BF16),
                   jax.ShapeDtypeStruct((bsz, seq, dl * 128), F32)],
        compiler_params=pltpu.CompilerParams(dimension_semantics=("parallel", "parallel"),
                                             vmem_limit_bytes=VMEM_LIMIT),
        name=f"attn{dl}",
    )(qkv_g)


def _natural_rows(ref, dl, scr):
    nchunk, tm, _ = scr.shape
    w = nchunk * 128
    per = tm // dl
    for r in range(dl):
        for c in range(nchunk):
            scr[c, pl.ds(r, per, stride=dl), :] = ref[:, r * w + c * 128:r * w + (c + 1) * 128].astype(F32)
    return jnp.concatenate([scr[c] for c in range(nchunk)], axis=1)


def _mix_kernel(ga_ref, gates_ref, o1_ref, o2_ref, o3_ref, l1_ref, l2_ref, l3_ref, x_ref,
                wa_ref, wb_ref, wo_ref, wr_ref, br_ref, g1_ref, b1_ref,
                x1_ref, x1p_ref, routet_ref, o2s_ref, o3s_ref, l2s_ref, l3s_ref):
    tm = x_ref.shape[0]
    o2 = _natural_rows(o2_ref, 4, o2s_ref)
    o3 = _natural_rows(o3_ref, 16, o3s_ref)
    l2 = _natural_rows(l2_ref, 4, l2s_ref)
    l3 = _natural_rows(l3_ref, 16, l3s_ref)
    er = lax.broadcasted_iota(I32, (256, B_WIDTH), 0)
    ec = lax.broadcasted_iota(I32, (256, B_WIDTH), 1)
    expand = jnp.where(er % 128 == (ec // B_HEAD_DIM) * 16, 1.0, 0.0).astype(BF16)

    def widen(w):
        hi = w.astype(BF16)
        lo = (w - hi.astype(F32)).astype(BF16)
        return _dot(jnp.concatenate([hi, lo], axis=1), expand)

    h = tm // 2
    halves = (slice(0, h), slice(h, tm))
    obs = []
    for r in halves:
        l1 = l1_ref[r, :]
        mx = jnp.maximum(l1, jnp.maximum(l2[r], l3[r]))
        e1, e2, e3 = jnp.exp(l1 - mx), jnp.exp(l2[r] - mx), jnp.exp(l3[r] - mx)
        inv = 1.0 / (e1 + e2 + e3)
        obs.append(widen(e1 * inv) * o1_ref[r, :].astype(F32) + widen(e2 * inv) * o2[r] + widen(e3 * inv) * o3[r])
    ybs = [_dot(ob.astype(BF16), wb_ref[...]) for ob in obs]
    yas = [_dot(ga_ref[r, :], wa_ref[...]) for r in halves]
    pres = [gates_ref[r, :D_MODEL].astype(F32) * ya + gates_ref[r, D_MODEL:].astype(F32) * yb
            for r, ya, yb in zip(halves, yas, ybs)]
    mixes = [_dot(pre.astype(BF16), wo_ref[...]) for pre in pres]
    x1s = [_ln(DEEPNORM_ALPHA * x_ref[r, :] + mix, g1_ref[...], b1_ref[...]) for r, mix in zip(halves, mixes)]
    logits = [_dot(x1.astype(BF16), wr_ref[...]) + br_ref[...] for x1 in x1s]

    lane = lax.broadcasted_iota(I32, (h, 128), 1).astype(F32)
    big = 1e9
    for i, r in enumerate(halves):
        x1_ref[r, :] = x1s[i]
        _store_packed_rows(x1p_ref.at[i * h // 8:(i + 1) * h // 8], x1s[i])
        lg = logits[i]
        gl = jnp.where(lane < N_GROUPS, lg, NEG)
        gm = jnp.max(gl, axis=-1, keepdims=True)
        gidx = jnp.min(jnp.where(gl == gm, lane, big), axis=-1, keepdims=True)
        gsum = jnp.sum(jnp.where(lane < N_GROUPS, jnp.exp(gl - gm), 0.0), axis=-1, keepdims=True)
        gprob = 1.0 / gsum
        lo_lane = N_GROUPS + N_EXPERTS * gidx
        el = jnp.where((lane >= lo_lane) & (lane < lo_lane + N_EXPERTS), lg, NEG)
        v1 = jnp.max(el, axis=-1, keepdims=True)
        i1 = jnp.min(jnp.where(el == v1, lane, big), axis=-1, keepdims=True)
        el2 = jnp.where(lane == i1, NEG, el)
        v2 = jnp.max(el2, axis=-1, keepdims=True)
        i2 = jnp.min(jnp.where(el2 == v2, lane, big), axis=-1, keepdims=True)
        t = jnp.exp(v2 - v1)
        w1 = 1.0 / (1.0 + t)
        w2 = t * w1
        route = jnp.where(lane == 0, i1 - N_GROUPS,
                          jnp.where(lane == 1, i2 - N_GROUPS,
                                    jnp.where(lane == 2, gprob * w1,
                                              jnp.where(lane == 3, gprob * w2, 0.0))))
        routet_ref[:, r] = route.T[:8, :]


def _mix(ga, gates, o1, o2, o3, l1, l2, l3, x2, wa, wb, wo, wr, br, g1, b1):
    n = x2.shape[0]
    bsz = o2.shape[0]
    tm = TM_MIX
    tiles = n // bsz // tm
    rows = lambda w: pl.BlockSpec((tm, w), lambda i: (i, 0))
    grouped = lambda a, dl: pl.BlockSpec((None, tm // dl, a.shape[2]), lambda i: (i // tiles, i % tiles, 0))
    full = lambda a: pl.BlockSpec(a.shape, lambda i: (0,) * a.ndim)
    return pl.pallas_call(
        _mix_kernel,
        grid=(n // tm,),
        in_specs=[rows(A_WIDTH), rows(2 * D_MODEL), rows(B_WIDTH), grouped(o2, 4), grouped(o3, 16),
                  rows(128), grouped(l2, 4), grouped(l3, 16), rows(D_MODEL),
                  full(wa), full(wb), full(wo), full(wr), full(br), full(g1), full(b1)],
        out_specs=[rows(D_MODEL), pl.BlockSpec((tm // 8, 32, 128), lambda i: (i, 0, 0)),
                   pl.BlockSpec((8, tm), lambda i: (0, i))],
        out_shape=[jax.ShapeDtypeStruct((n, D_MODEL), F32),
                   jax.ShapeDtypeStruct((n // 8, 32, 128), U32),
                   jax.ShapeDtypeStruct((8, n), F32)],
        scratch_shapes=[pltpu.VMEM((B_WIDTH // 128, tm, 128), F32), pltpu.VMEM((B_WIDTH // 128, tm, 128), F32),
                        pltpu.VMEM((1, tm, 128), F32), pltpu.VMEM((1, tm, 128), F32)],
        compiler_params=pltpu.CompilerParams(dimension_semantics=("parallel",), vmem_limit_bytes=VMEM_LIMIT),
        name="mix",
    )(ga, gates, o1, o2, o3, l1, l2, l3, x2, wa, wb, wo, wr, br, g1, b1)


SC_WINDOW = 128


def _sc_mesh():
    return plsc.VectorSubcoreMesh(core_axis_name="core", subcore_axis_name="subcore")


def _sc_scatter_rows(rows, dst, n_out):
    r = rows.shape[0]

    @pl.kernel(out_type=jax.ShapeDtypeStruct((n_out, 128), rows.dtype), mesh=_sc_mesh())
    def scatter(rows_hbm, dst0_hbm, dst1_hbm, out_hbm):
        def body(rows_vmem, dst0_vmem, dst1_vmem):
            pltpu.sync_copy(rows_vmem, out_hbm.at[dst0_vmem.at[0]])
            pltpu.sync_copy(rows_vmem, out_hbm.at[dst1_vmem.at[0]])

        pltpu.emit_pipeline(
            body,
            grid=(r // SC_WINDOW,),
            in_specs=[pl.BlockSpec((SC_WINDOW, 128), lambda i: (i, 0)),
                      pl.BlockSpec((1, SC_WINDOW), lambda i: (0, i)),
                      pl.BlockSpec((1, SC_WINDOW), lambda i: (0, i))],
            out_specs=[],
            core_axis_name=("core", "subcore"),
            dimension_semantics=(pltpu.PARALLEL,),
        )(rows_hbm, dst0_hbm, dst1_hbm)

    return scatter(rows, dst[0:1], dst[1:2])


def _sc_gather_rows(table, src):
    m = src.shape[0]

    @pl.kernel(out_type=jax.ShapeDtypeStruct((m, 128), table.dtype), mesh=_sc_mesh())
    def gather(table_hbm, src_hbm, out_hbm):
        def body(src_vmem, out_vmem):
            pltpu.sync_copy(table_hbm.at[src_vmem.at[0]], out_vmem)

        pltpu.emit_pipeline(
            body,
            grid=(m // SC_WINDOW,),
            in_specs=[pl.BlockSpec((1, SC_WINDOW), lambda i: (0, i))],
            out_specs=[pl.BlockSpec((SC_WINDOW, 128), lambda i: (i, 0))],
            core_axis_name=("core", "subcore"),
            dimension_semantics=(pltpu.PARALLEL,),
        )(src_hbm, out_hbm)

    return gather(table, src.reshape(1, m))


def _moe_kernel(te_ref, tv_ref, tb_ref, xs_ref, wg_ref, wu_ref, wd_ref, ys_ref, wgb_ref, wub_ref, wdb_ref):
    del tb_ref
    t = pl.program_id(0)

    @pl.when((t == 0) | (te_ref[t] != te_ref[jnp.maximum(t - 1, 0)]))
    def _():
        wgb_ref[...] = wg_ref[...].astype(BF16)
        wub_ref[...] = wu_ref[...].astype(BF16)
        wdb_ref[...] = wd_ref[...].astype(BF16)

    @pl.when(tv_ref[t] == 1)
    def _():
        xb = _load_packed_rows(xs_ref).astype(BF16)
        g = _dot(xb, wgb_ref[...])
        u = _dot(xb, wub_ref[...])
        h = (g * _sigmoid(g) * u).astype(BF16)
        _store_packed_rows(ys_ref, _dot(h, wdb_ref[...]))


def _moe(tile_expert, tile_valid, tile_block, xs, wg, wu, wd):
    tm = TM_MOE
    nt = tile_expert.shape[0]
    rows = pl.BlockSpec((tm // 8, 32, 128), lambda t, te, tv, tb: (tb[t], 0, 0))
    return pl.pallas_call(
        _moe_kernel,
        grid_spec=pltpu.PrefetchScalarGridSpec(
            num_scalar_prefetch=3,
            grid=(nt,),
            in_specs=[rows,
                      pl.BlockSpec((None, D_MODEL, D_EXPERT), lambda t, te, tv, tb: (te[t], 0, 0)),
                      pl.BlockSpec((None, D_MODEL, D_EXPERT), lambda t, te, tv, tb: (te[t], 0, 0)),
                      pl.BlockSpec((None, D_EXPERT, D_MODEL), lambda t, te, tv, tb: (te[t], 0, 0))],
            out_specs=rows,
            scratch_shapes=[pltpu.VMEM((D_MODEL, D_EXPERT), BF16), pltpu.VMEM((D_MODEL, D_EXPERT), BF16),
                            pltpu.VMEM((D_EXPERT, D_MODEL), BF16)]),
        out_shape=jax.ShapeDtypeStruct((nt * tm // 8, 32, 128), U32),
        compiler_params=pltpu.CompilerParams(dimension_semantics=("arbitrary",), vmem_limit_bytes=VMEM_LIMIT),
        name="moe",
    )(tile_expert, tile_valid, tile_block, xs, wg, wu, wd)


def _final_kernel(y0_ref, y1_ref, x1_ref, p_ref, routet_ref, wple_ref, wpg_ref, g2_ref, b2_ref, *rest):
    out_ref = rest[-1]
    tm = x1_ref.shape[0]
    route = routet_ref[...].T
    h = tm // 2
    halves = (slice(0, h), slice(h, tm))
    x1s = [x1_ref[r, :] for r in halves]
    gates = [_dot(x1.astype(BF16), wpg_ref[...]) for x1 in x1s]
    plins = [_dot(p_ref[r, :].astype(BF16), wple_ref[...]) for r in halves]
    for i, r in enumerate(halves):
        g8 = slice(i * h // 8, (i + 1) * h // 8)
        ffn = (route[r, 2:3] * _load_packed_rows(y0_ref.at[g8])
               + route[r, 3:4] * _load_packed_rows(y1_ref.at[g8]))
        ple = plins[i] * _sigmoid(gates[i])
        out_ref[r, :] = _ln(DEEPNORM_ALPHA * x1s[i] + ffn + ple, g2_ref[...], b2_ref[...])


def _final(yg, x1, p2, routet, wple, wpg, g2, b2, half, prev):
    n = x1.shape[0]
    tm = TM_FIN
    nt = n // 2 // tm
    off = half * nt
    rows = lambda w: pl.BlockSpec((tm, w), lambda t: (t + off, 0))
    full = lambda a: pl.BlockSpec(a.shape, lambda t: (0,) * a.ndim)
    in_specs = [pl.BlockSpec((tm // 8, 32, 128), lambda t: (t, 0, 0)),
                pl.BlockSpec((tm // 8, 32, 128), lambda t: (t + nt, 0, 0)),
                rows(D_MODEL), rows(PLE_DIM), pl.BlockSpec((8, tm), lambda t: (0, t + off)),
                full(wple), full(wpg), full(g2), full(b2)]
    args = [yg, yg, x1, p2, routet, wple, wpg, g2, b2]
    aliases = {}
    if prev is not None:
        in_specs.append(pl.BlockSpec(memory_space=pl.ANY))
        args.append(prev)
        aliases = {len(args) - 1: 0}
    return pl.pallas_call(
        _final_kernel,
        grid=(nt,),
        in_specs=in_specs,
        out_specs=rows(D_MODEL),
        out_shape=jax.ShapeDtypeStruct((n, D_MODEL), F32),
        input_output_aliases=aliases,
        compiler_params=pltpu.CompilerParams(dimension_semantics=("parallel",), vmem_limit_bytes=VMEM_LIMIT),
        name=f"final{half}",
    )(*args)


def _route_tables_kernel(e_ref, piece_ref, ends_ref):
    r = e_ref.shape[0]
    e = e_ref[...]
    ri = lax.broadcasted_iota(I32, (128, 128), 0)
    ci = lax.broadcasted_iota(I32, (128, 128), 1)
    upper = jnp.where(ri <= ci, 1.0, 0.0).astype(BF16)
    rr = lax.broadcasted_iota(I32, (r, r), 0)
    rc = lax.broadcasted_iota(I32, (r, r), 1)
    below = jnp.where(rc < rr, 1.0, 0.0).astype(BF16)
    lane = lax.broadcasted_iota(I32, (1, 128), 1)

    rank = jnp.zeros((r, 128), F32)
    counts = jnp.zeros((1, 128), F32)
    for x in range(N_EXPERTS_TOTAL):
        m = jnp.where(e == x, 1.0, 0.0)
        pre = _dot(m.astype(BF16), upper)
        tot = jnp.broadcast_to(pre[:, 127:128], (r, 128))
        off = _dot(below, tot.astype(BF16))
        rank = rank + m * (pre + off)
        counts = jnp.where(lane == x, off[r - 1:r, :] + tot[r - 1:r, :], counts)
    padded = jnp.floor((counts + (TM_MOE - 1)) * (1.0 / TM_MOE)) * TM_MOE
    ends = _dot(jnp.broadcast_to(padded, (8, 128)).astype(BF16), upper)[0:1, :]
    offs = ends - padded
    ends_ref[...] = jnp.broadcast_to(ends, (8, 128)).astype(I32)

    pos = rank - 1.0
    for x in range(N_EXPERTS_TOTAL):
        pos = pos + jnp.where(e == x, offs[:, x:x + 1], 0.0)

    hi = jnp.floor(pos * (1.0 / 256.0))
    lo = pos - 256.0 * hi
    jv = ((lane % 32) // 8).astype(F32)
    for c in range(4):
        sel = jnp.where(ri == 32 * c + 8 * (ci // 32) + ci % 8, 1.0, 0.0).astype(BF16)
        pc = 256.0 * _dot(hi.astype(BF16), sel) + _dot(lo.astype(BF16), sel)
        p8 = jnp.floor(pc * 0.125)
        piece = p8 * (8.0 * SUBROWS) + (pc - 8.0 * p8) + 8.0 * jv
        piece_ref[pl.ds(c, r, stride=4), :] = piece.astype(I32)


def _routing_tables(routet, n):
    tm = TM_MOE
    nt = (2 * n) // tm + N_EXPERTS_TOTAL
    r = 2 * n // 128
    piece, ends = pl.pallas_call(
        _route_tables_kernel,
        out_shape=[jax.ShapeDtypeStruct((4 * r, 128), I32), jax.ShapeDtypeStruct((8, 128), I32)],
        compiler_params=pltpu.CompilerParams(vmem_limit_bytes=VMEM_LIMIT),
        name="route_tables",
    )(routet[0:2].reshape(r, 128))
    ends = ends[0, :N_EXPERTS_TOTAL]
    tile_start = jnp.arange(nt, dtype=I32) * tm
    tile_expert = jnp.minimum(jnp.sum((tile_start[:, None] >= ends[None, :]).astype(I32), axis=1),
                              N_EXPERTS_TOTAL - 1).astype(I32)
    tile_valid = (tile_start < ends[-1]).astype(I32)
    tile_block = jnp.minimum(jnp.arange(nt, dtype=I32), ends[-1] // tm - 1)
    return tile_expert, tile_valid, tile_block, piece.reshape(2, n * SUBROWS)


def kernel(x, p, w_in, a_ln_g, a_ln_b, a_ws, a_bs, w_a_proj, w_b_proj, w_o, ln1_g, ln1_b, w_group_router,
           b_group_router, w_expert_router, b_expert_router, w_gate, w_up, w_down, w_ple, w_ple_gate,
           ln2_g, ln2_b):
    bsz, s, d = x.shape
    n = bsz * s
    assert d == D_MODEL and s % (SPAN * max(B_DILATIONS)) == 0 and n % TM_PROJ == 0
    assert w_in.shape[0] == 1, "one layer"

    w_in_b = w_in[0].astype(BF16)
    a_bias = jnp.repeat(a_bs[0].T, A_WIDTH // 8, axis=1)

    ga, gates, qkv1, qkv2, qkv3 = _proj(x, w_in_b, a_ln_g, a_ln_b, a_ws[0], a_bias)
    o1, l1 = _attn(qkv1.reshape(bsz, 1, s, 3 * COL), 0)
    o2, l2 = _attn(qkv2, 1)
    o3, l3 = _attn(qkv3, 2)

    pad = 128 - N_GROUPS - N_EXPERTS_TOTAL
    wr = jnp.concatenate([w_group_router[0], w_expert_router[0].reshape(d, N_EXPERTS_TOTAL),
                          jnp.zeros((d, pad), F32)], axis=1).astype(BF16)
    br = jnp.concatenate([b_group_router[0], b_expert_router[0].reshape(-1), jnp.zeros((pad,), F32)])[None, :]
    x1, x1p, routet = _mix(
        ga, gates, o1.reshape(n, B_WIDTH), o2, o3, l1.reshape(n, 128), l2, l3, x.reshape(n, d),
        w_a_proj[0].astype(BF16), w_b_proj[0].astype(BF16), w_o[0].astype(BF16), wr, br, ln1_g, ln1_b)

    tile_expert, tile_valid, tile_block, piece = _routing_tables(routet, n)
    nt = tile_expert.shape[0]
    xs = _sc_scatter_rows(x1p.reshape(n * SUBROWS, 128), piece, nt * TM_MOE * SUBROWS)
    ys = _moe(tile_expert, tile_valid, tile_block, xs.reshape(nt * TM_MOE // 8, 32, 128),
              w_gate[0].reshape(N_EXPERTS_TOTAL, d, D_EXPERT), w_up[0].reshape(N_EXPERTS_TOTAL, d, D_EXPERT),
              w_down[0].reshape(N_EXPERTS_TOTAL, D_EXPERT, d)).reshape(nt * TM_MOE * SUBROWS, 128)
    out = None
    hp = n * SUBROWS // 2
    for half in range(2):
        yg = _sc_gather_rows(ys, piece[:, half * hp:(half + 1) * hp].reshape(-1))
        out = _final(yg.reshape(n // 8, 32, 128), x1, p[0].reshape(n, PLE_DIM), routet,
                     w_ple[0].astype(BF16), w_ple_gate[0].astype(BF16), ln2_g, ln2_b, half, out)
    return out.reshape(bsz, s, d)
```

```python
import functools

import jax
import jax.numpy as jnp
from jax import lax
from jax.experimental import pallas as pl
from jax.experimental.pallas import tpu as pltpu
from jax.experimental.pallas import tpu_sc as plsc

F32 = jnp.float32
BF16 = jnp.bfloat16
U32 = jnp.uint32
I32 = jnp.int32

D_MODEL = 1024
PLE_DIM = 256
A_WIDTH = 512
A_CHUNK = 128
B_HEAD_DIM = 64
B_HEADS = 8
B_WIDTH = 512
B_DILATIONS = (1, 4, 16)
SPAN = 128
N_GROUPS = 4
N_EXPERTS = 8
N_EXPERTS_TOTAL = N_GROUPS * N_EXPERTS
D_EXPERT = 256
DEEPNORM_ALPHA = 2.0 ** 0.25
LN_EPS = 1e-5
COL = 512
NEG = -1e30

VMEM_LIMIT = 56 * 1024 * 1024

TM_PROJ = 512
TM_MIX = 512
TM_MOE = 512
TM_FIN = 1024


def _ln(x, g, b):
    mu = jnp.mean(x, axis=-1, keepdims=True)
    xc = x - mu
    var = jnp.mean(xc * xc, axis=-1, keepdims=True)
    return xc * lax.rsqrt(var + LN_EPS) * g + b


def _gelu_tanh(x):
    return 0.5 * x * (1.0 + jnp.tanh(0.7978845608028654 * (x + 0.044715 * (x * x * x))))


def _sigmoid(x):
    return 0.5 * jnp.tanh(0.5 * x) + 0.5


def _dot(a, b):
    return jnp.dot(a, b, preferred_element_type=F32)


PACK_W = D_MODEL // 2
SUBROWS = PACK_W // 128


def _store_packed_rows(ref, x):
    m = x.shape[0]
    xb = x.astype(BF16).astype(F32)
    lo = pltpu.bitcast(xb[:, :PACK_W], U32) >> 16
    hi = pltpu.bitcast(xb[:, PACK_W:], U32) & jnp.uint32(0xFFFF0000)
    w = hi | lo
    for j in range(SUBROWS):
        ref[:, 8 * j:8 * (j + 1), :] = w[:, 128 * j:128 * (j + 1)].reshape(m // 8, 8, 128)


def _load_packed_rows(ref):
    m = ref.shape[0] * 8
    ws = [ref[:, 8 * j:8 * (j + 1), :].reshape(m, 128) for j in range(SUBROWS)]
    lo = [pltpu.bitcast(w << 16, F32) for w in ws]
    hi = [pltpu.bitcast(w & jnp.uint32(0xFFFF0000), F32) for w in ws]
    return jnp.concatenate(lo + hi, axis=1)


def _proj_kernel(x_ref, *refs):
    w = refs[:15]
    lng_ref, lnb_ref, ws_ref, bias_ref = refs[15:19]
    ga_ref, gates_ref, qkv1_ref, qkv2_ref, qkv3_ref = refs[19:24]
    xc_ref = refs[24]
    tm = x_ref.shape[0]
    xb = x_ref[...].astype(BF16)

    u_raw = _dot(xb, w[0][...])
    v_raw = _dot(xb, w[1][...])

    for i in range(4):
        gates_ref[:, i * COL:(i + 1) * COL] = _sigmoid(_dot(xb, w[11 + i][...])).astype(BF16)
    for j in range(3):
        qkv1_ref[:, j * COL:(j + 1) * COL] = _dot(xb, w[2 + 3 * j][...]).astype(BF16)

    for c in range(D_MODEL // 128):
        xc_ref[c] = x_ref[:, c * 128:(c + 1) * 128]
    for gi, out_ref in ((1, qkv2_ref), (2, qkv3_ref)):
        dl = B_DILATIONS[gi]
        per = tm // dl
        xp = jnp.concatenate(
            [jnp.concatenate([xc_ref[c, pl.ds(r, per, stride=dl), :] for c in range(D_MODEL // 128)], axis=1)
             for r in range(dl)], axis=0).astype(BF16)
        for j in range(3):
            res = _dot(xp, w[2 + 3 * j + gi][...]).astype(BF16)
            for r in range(dl):
                out_ref[r, :, j * COL:(j + 1) * COL] = res[r * per:(r + 1) * per]

    u = _gelu_tanh(u_raw)
    v = _gelu_tanh(v_raw)
    vn = _ln(v, lng_ref[...], lnb_ref[...]).astype(BF16)

    row = lax.broadcasted_iota(I32, (A_CHUNK, A_CHUNK), 0)
    colm = lax.broadcasted_iota(I32, (A_CHUNK, A_CHUNK), 1)
    causal = colm <= row
    lo = colm < 64
    zero = jnp.zeros((A_CHUNK, A_CHUNK), BF16)
    wcat = []
    for j in range(4):
        w0 = jnp.where(causal, ws_ref[2 * j], 0.0).astype(BF16)
        w1 = jnp.where(causal, ws_ref[2 * j + 1], 0.0).astype(BF16)
        wcat.append(jnp.concatenate([w0, w1], axis=1))
    for c in range(tm // A_CHUNK):
        r0 = c * A_CHUNK
        for j in range(4):
            c0 = j * 128
            vt = vn[r0:r0 + A_CHUNK, c0:c0 + 128]
            rhs = jnp.concatenate([jnp.where(lo, vt, zero), jnp.where(lo, zero, vt)], axis=0)
            mixed = _dot(wcat[j], rhs) + bias_ref[:, c0:c0 + 128]
            ga_ref[r0:r0 + A_CHUNK, c0:c0 + 128] = (u[r0:r0 + A_CHUNK, c0:c0 + 128] * mixed).astype(BF16)


def _proj(x, w_in_b, a_ln_g, a_ln_b, a_ws, a_bias):
    bsz, s, _ = x.shape
    n = bsz * s
    tm = TM_PROJ
    tiles = s // tm
    x2 = x.reshape(n, D_MODEL)
    wspec = lambda j: pl.BlockSpec((D_MODEL, COL), lambda i, j=j: (0, j), pipeline_mode=pl.Buffered(1))
    full = lambda shape: pl.BlockSpec(shape, lambda i: (0,) * len(shape))
    rows = lambda width: pl.BlockSpec((tm, width), lambda i: (i, 0))
    dil = lambda dl: pl.BlockSpec((None, dl, tm // dl, 3 * COL), lambda i: (i // tiles, 0, i % tiles, 0))
    return pl.pallas_call(
        _proj_kernel,
        grid=(n // tm,),
        in_specs=[rows(D_MODEL)] + [wspec(j) for j in range(15)]
                 + [full((1, A_WIDTH)), full((1, A_WIDTH)), full((8, A_CHUNK, A_CHUNK)), full((A_CHUNK, A_WIDTH))],
        out_specs=[rows(A_WIDTH), rows(4 * COL), rows(3 * COL), dil(4), dil(16)],
        out_shape=[jax.ShapeDtypeStruct((n, A_WIDTH), BF16),
                   jax.ShapeDtypeStruct((n, 4 * COL), BF16),
                   jax.ShapeDtypeStruct((n, 3 * COL), BF16),
                   jax.ShapeDtypeStruct((bsz, 4, s // 4, 3 * COL), BF16),
                   jax.ShapeDtypeStruct((bsz, 16, s // 16, 3 * COL), BF16)],
        scratch_shapes=[pltpu.VMEM((D_MODEL // 128, tm, 128), F32)],
        compiler_params=pltpu.CompilerParams(dimension_semantics=("parallel",), vmem_limit_bytes=VMEM_LIMIT),
        name="proj",
    )(x2, *([w_in_b] * 15), a_ln_g, a_ln_b, a_ws, a_bias)


def _attn_kernel(qkv_ref, o_ref, lse_ref, *, ns, seq):
    nb = seq // SPAN
    lane = lax.broadcasted_iota(I32, (SPAN, 128), 1)
    lo = lane < 64
    lane16 = lane // 16
    qi = lax.broadcasted_iota(I32, (SPAN, 2 * SPAN), 0)
    ki = lax.broadcasted_iota(I32, (SPAN, 2 * SPAN), 1)
    causal = lax.broadcasted_iota(I32, (SPAN, SPAN), 1) <= lax.broadcasted_iota(I32, (SPAN, SPAN), 0)
    bias_first = jnp.where(causal, 0.0, NEG).astype(F32)
    bias_first = jnp.concatenate([bias_first, bias_first], axis=0)
    bias_main = jnp.where((ki >= qi) & (ki <= qi + SPAN), 0.0, NEG).astype(F32)
    bias_main = jnp.concatenate([bias_main, bias_main], axis=0)
    zero = jnp.zeros((SPAN, 128), BF16)

    for s in range(ns):
        def block(row0, start, bias, s=s):
            win = bias.shape[1]
            pairs = range(B_HEADS // 2)
            scores, values = [], []
            for jp in pairs:
                c0 = jp * 128
                q = qkv_ref[s, pl.ds(row0, SPAN), c0:c0 + 128] * jnp.asarray(0.125, BF16)
                k = qkv_ref[s, pl.ds(start, win), COL + c0:COL + c0 + 128]
                values.append(qkv_ref[s, pl.ds(start, win), 2 * COL + c0:2 * COL + c0 + 128])
                qs = jnp.concatenate([jnp.where(lo, q, zero), jnp.where(lo, zero, q)], axis=0)
                scores.append(lax.dot_general(qs, k, (((1,), (1,)), ((), ())), preferred_element_type=F32) + bias)
            probs, maxes, sums = [], [], []
            for jp in pairs:
                m = jnp.max(scores[jp], axis=-1, keepdims=True)
                p = jnp.exp(scores[jp] - m)
                maxes.append(m)
                sums.append(jnp.sum(p, axis=-1, keepdims=True))
                probs.append(p.astype(BF16))
            lse_tile = jnp.zeros((SPAN, 128), F32)
            for jp in pairs:
                c0 = jp * 128
                ov = _dot(probs[jp], values[jp])
                inv = 1.0 / sums[jp]
                o = jnp.where(lo, ov[:SPAN] * inv[:SPAN], ov[SPAN:] * inv[SPAN:])
                o_ref[pl.ds(row0, SPAN), s * B_WIDTH + c0:s * B_WIDTH + c0 + 128] = o.astype(BF16)
                lse = maxes[jp] + jnp.log(sums[jp])
                lse_tile = jnp.where(lane16 == 2 * jp, lse[:SPAN],
                                     jnp.where(lane16 == 2 * jp + 1, lse[SPAN:], lse_tile))
            lse_ref[pl.ds(row0, SPAN), s * 128:(s + 1) * 128] = lse_tile

        block(0, 0, bias_first)
        if nb > 1:
            def body(i, carry):
                block(pl.multiple_of(i * SPAN, SPAN), pl.multiple_of((i - 1) * SPAN, SPAN), bias_main)
                return carry
            lax.fori_loop(1, nb, body, 0)


def _attn(qkv_g, gi):
    bsz, dl, seq, _ = qkv_g.shape
    ns = max(1, min(dl, TM_PROJ // seq))
    return pl.pallas_call(
        functools.partial(_attn_kernel, ns=ns, seq=seq),
        grid=(bsz, dl // ns),
        in_specs=[pl.BlockSpec((None, ns, seq, 3 * COL), lambda b, r: (b, r, 0, 0))],
        out_specs=[pl.BlockSpec((None, seq, ns * B_WIDTH), lambda b, r: (b, 0, r)),
                   pl.BlockSpec((None, seq, ns * 128), lambda b, r: (b, 0, r))],
        out_shape=[jax.ShapeDtypeStruct((bsz, seq, dl * B_WIDTH), BF16),
                   jax.ShapeDtypeStruct((bsz, seq, dl * 128), F32)],
        compiler_params=pltpu.CompilerParams(dimension_semantics=("parallel", "parallel"),
                                             vmem_limit_bytes=VMEM_LIMIT),
        name=f"attn{dl}",
    )(qkv_g)


def _natural_rows(ref, dl, scr):
    nchunk, tm, _ = scr.shape
    w = nchunk * 128
    per = tm // dl
    for r in range(dl):
        for c in range(nchunk):
            scr[c, pl.ds(r, per, stride=dl), :] = ref[:, r * w + c * 128:r * w + (c + 1) * 128].astype(F32)
    return jnp.concatenate([scr[c] for c in range(nchunk)], axis=1)


def _mix_kernel(ga_ref, gates_ref, o1_ref, o2_ref, o3_ref, l1_ref, l2_ref, l3_ref, x_ref,
                wa_ref, wb_ref, wo_ref, wr_ref, br_ref, g1_ref, b1_ref,
                x1_ref, x1p_ref, routet_ref, o2s_ref, o3s_ref, l2s_ref, l3s_ref):
    tm = x_ref.shape[0]
    o2 = _natural_rows(o2_ref, 4, o2s_ref)
    o3 = _natural_rows(o3_ref, 16, o3s_ref)
    l2 = _natural_rows(l2_ref, 4, l2s_ref)
    l3 = _natural_rows(l3_ref, 16, l3s_ref)
    er = lax.broadcasted_iota(I32, (256, B_WIDTH), 0)
    ec = lax.broadcasted_iota(I32, (256, B_WIDTH), 1)
    expand = jnp.where(er % 128 == (ec // B_HEAD_DIM) * 16, 1.0, 0.0).astype(BF16)

    def widen(w):
        hi = w.astype(BF16)
        lo = (w - hi.astype(F32)).astype(BF16)
        return _dot(jnp.concatenate([hi, lo], axis=1), expand)

    h = tm // 2
    halves = (slice(0, h), slice(h, tm))
    obs = []
    for r in halves:
        l1 = l1_ref[r, :]
        mx = jnp.maximum(l1, jnp.maximum(l2[r], l3[r]))
        e1, e2, e3 = jnp.exp(l1 - mx), jnp.exp(l2[r] - mx), jnp.exp(l3[r] - mx)
        inv = 1.0 / (e1 + e2 + e3)
        obs.append(widen(e1 * inv) * o1_ref[r, :].astype(F32) + widen(e2 * inv) * o2[r] + widen(e3 * inv) * o3[r])
    ybs = [_dot(ob.astype(BF16), wb_ref[...]) for ob in obs]
    yas = [_dot(ga_ref[r, :], wa_ref[...]) for r in halves]
    pres = [gates_ref[r, :D_MODEL].astype(F32) * ya + gates_ref[r, D_MODEL:].astype(F32) * yb
            for r, ya, yb in zip(halves, yas, ybs)]
    mixes = [_dot(pre.astype(BF16), wo_ref[...]) for pre in pres]
    x1s = [_ln(DEEPNORM_ALPHA * x_ref[r, :] + mix, g1_ref[...], b1_ref[...]) for r, mix in zip(halves, mixes)]
    logits = [_dot(x1.astype(BF16), wr_ref[...]) + br_ref[...] for x1 in x1s]

    nrow = 40
    row = lax.broadcasted_iota(I32, (nrow, h), 0).astype(F32)
    row8 = lax.broadcasted_iota(I32, (8, h), 0)
    big = 1e9
    for i, r in enumerate(halves):
        x1_ref[r, :] = x1s[i]
        _store_packed_rows(x1p_ref.at[i * h // 8:(i + 1) * h // 8], x1s[i])
        lg = logits[i].T[:nrow, :]
        gl = jnp.where(row < N_GROUPS, lg, NEG)
        gm = jnp.max(gl, axis=0, keepdims=True)
        gidx = jnp.min(jnp.where(gl == gm, row, big), axis=0, keepdims=True)
        gsum = jnp.sum(jnp.where(row < N_GROUPS, jnp.exp(gl - gm), 0.0), axis=0, keepdims=True)
        gprob = 1.0 / gsum
        lo_row = N_GROUPS + N_EXPERTS * gidx
        el = jnp.where((row >= lo_row) & (row < lo_row + N_EXPERTS), lg, NEG)
        v1 = jnp.max(el, axis=0, keepdims=True)
        i1 = jnp.min(jnp.where(el == v1, row, big), axis=0, keepdims=True)
        el2 = jnp.where(row == i1, NEG, el)
        v2 = jnp.max(el2, axis=0, keepdims=True)
        i2 = jnp.min(jnp.where(el2 == v2, row, big), axis=0, keepdims=True)
        t = jnp.exp(v2 - v1)
        w1 = 1.0 / (1.0 + t)
        w2 = t * w1
        routet_ref[:, r] = jnp.where(row8 == 0, i1 - N_GROUPS,
                                     jnp.where(row8 == 1, i2 - N_GROUPS,
                                               jnp.where(row8 == 2, gprob * w1,
                                                         jnp.where(row8 == 3, gprob * w2, 0.0))))


def _mix(ga, gates, o1, o2, o3, l1, l2, l3, x2, wa, wb, wo, wr, br, g1, b1):
    n = x2.shape[0]
    bsz = o2.shape[0]
    tm = TM_MIX
    tiles = n // bsz // tm
    rows = lambda w: pl.BlockSpec((tm, w), lambda i: (i, 0))
    grouped = lambda a, dl: pl.BlockSpec((None, tm // dl, a.shape[2]), lambda i: (i // tiles, i % tiles, 0))
    full = lambda a: pl.BlockSpec(a.shape, lambda i: (0,) * a.ndim)
    return pl.pallas_call(
        _mix_kernel,
        grid=(n // tm,),
        in_specs=[rows(A_WIDTH), rows(2 * D_MODEL), rows(B_WIDTH), grouped(o2, 4), grouped(o3, 16),
                  rows(128), grouped(l2, 4), grouped(l3, 16), rows(D_MODEL),
                  full(wa), full(wb), full(wo), full(wr), full(br), full(g1), full(b1)],
        out_specs=[rows(D_MODEL), pl.BlockSpec((tm // 8, 32, 128), lambda i: (i, 0, 0)),
                   pl.BlockSpec((8, tm), lambda i: (0, i))],
        out_shape=[jax.ShapeDtypeStruct((n, D_MODEL), F32),
                   jax.ShapeDtypeStruct((n // 8, 32, 128), U32),
                   jax.ShapeDtypeStruct((8, n), F32)],
        scratch_shapes=[pltpu.VMEM((B_WIDTH // 128, tm, 128), F32), pltpu.VMEM((B_WIDTH // 128, tm, 128), F32),
                        pltpu.VMEM((1, tm, 128), F32), pltpu.VMEM((1, tm, 128), F32)],
        compiler_params=pltpu.CompilerParams(dimension_semantics=("parallel",), vmem_limit_bytes=VMEM_LIMIT),
        name="mix",
    )(ga, gates, o1, o2, o3, l1, l2, l3, x2, wa, wb, wo, wr, br, g1, b1)


SC_WINDOW = 128


def _sc_mesh():
    return plsc.VectorSubcoreMesh(core_axis_name="core", subcore_axis_name="subcore")


def _sc_scatter_rows(rows, dst, n_out):
    r = rows.shape[0]

    @pl.kernel(out_type=jax.ShapeDtypeStruct((n_out, 128), rows.dtype), mesh=_sc_mesh())
    def scatter(rows_hbm, dst0_hbm, dst1_hbm, out_hbm):
        def body(rows_vmem, dst0_vmem, dst1_vmem):
            pltpu.sync_copy(rows_vmem, out_hbm.at[dst0_vmem.at[0]])
            pltpu.sync_copy(rows_vmem, out_hbm.at[dst1_vmem.at[0]])

        pltpu.emit_pipeline(
            body,
            grid=(r // SC_WINDOW,),
            in_specs=[pl.BlockSpec((SC_WINDOW, 128), lambda i: (i, 0)),
                      pl.BlockSpec((1, SC_WINDOW), lambda i: (0, i)),
                      pl.BlockSpec((1, SC_WINDOW), lambda i: (0, i))],
            out_specs=[],
            core_axis_name=("core", "subcore"),
            dimension_semantics=(pltpu.PARALLEL,),
        )(rows_hbm, dst0_hbm, dst1_hbm)

    return scatter(rows, dst[0:1], dst[1:2])


def _sc_gather_rows(table, src):
    m = src.shape[0]

    @pl.kernel(out_type=jax.ShapeDtypeStruct((m, 128), table.dtype), mesh=_sc_mesh())
    def gather(table_hbm, src_hbm, out_hbm):
        def body(src_vmem, out_vmem):
            pltpu.sync_copy(table_hbm.at[src_vmem.at[0]], out_vmem)

        pltpu.emit_pipeline(
            body,
            grid=(m // SC_WINDOW,),
            in_specs=[pl.BlockSpec((1, SC_WINDOW), lambda i: (0, i))],
            out_specs=[pl.BlockSpec((SC_WINDOW, 128), lambda i: (i, 0))],
            core_axis_name=("core", "subcore"),
            dimension_semantics=(pltpu.PARALLEL,),
        )(src_hbm, out_hbm)

    return gather(table, src.reshape(1, m))


def _moe_kernel(te_ref, tv_ref, tb_ref, xs_ref, wg_ref, wu_ref, wd_ref, ys_ref, wgb_ref, wub_ref, wdb_ref):
    del tb_ref
    t = pl.program_id(0)

    @pl.when((t == 0) | (te_ref[t] != te_ref[jnp.maximum(t - 1, 0)]))
    def _():
        wgb_ref[...] = wg_ref[...].astype(BF16)
        wub_ref[...] = wu_ref[...].astype(BF16)
        wdb_ref[...] = wd_ref[...].astype(BF16)

    @pl.when(tv_ref[t] == 1)
    def _():
        xb = _load_packed_rows(xs_ref).astype(BF16)
        g = _dot(xb, wgb_ref[...])
        u = _dot(xb, wub_ref[...])
        h = (g * _sigmoid(g) * u).astype(BF16)
        _store_packed_rows(ys_ref, _dot(h, wdb_ref[...]))


def _moe(tile_expert, tile_valid, tile_block, xs, wg, wu, wd):
    tm = TM_MOE
    nt = tile_expert.shape[0]
    rows = pl.BlockSpec((tm // 8, 32, 128), lambda t, te, tv, tb: (tb[t], 0, 0))
    return pl.pallas_call(
        _moe_kernel,
        grid_spec=pltpu.PrefetchScalarGridSpec(
            num_scalar_prefetch=3,
            grid=(nt,),
            in_specs=[rows,
                      pl.BlockSpec((None, D_MODEL, D_EXPERT), lambda t, te, tv, tb: (te[t], 0, 0)),
                      pl.BlockSpec((None, D_MODEL, D_EXPERT), lambda t, te, tv, tb: (te[t], 0, 0)),
                      pl.BlockSpec((None, D_EXPERT, D_MODEL), lambda t, te, tv, tb: (te[t], 0, 0))],
            out_specs=rows,
            scratch_shapes=[pltpu.VMEM((D_MODEL, D_EXPERT), BF16), pltpu.VMEM((D_MODEL, D_EXPERT), BF16),
                            pltpu.VMEM((D_EXPERT, D_MODEL), BF16)]),
        out_shape=jax.ShapeDtypeStruct((nt * tm // 8, 32, 128), U32),
        compiler_params=pltpu.CompilerParams(dimension_semantics=("arbitrary",), vmem_limit_bytes=VMEM_LIMIT),
        name="moe",
    )(tile_expert, tile_valid, tile_block, xs, wg, wu, wd)


def _final_kernel(y0_ref, y1_ref, x1_ref, p_ref, routet_ref, wple_ref, wpg_ref, g2_ref, b2_ref, *rest):
    out_ref = rest[-1]
    tm = x1_ref.shape[0]
    route = routet_ref[...].T
    h = tm // 2
    halves = (slice(0, h), slice(h, tm))
    x1s = [x1_ref[r, :] for r in halves]
    gates = [_dot(x1.astype(BF16), wpg_ref[...]) for x1 in x1s]
    plins = [_dot(p_ref[r, :].astype(BF16), wple_ref[...]) for r in halves]
    for i, r in enumerate(halves):
        g8 = slice(i * h // 8, (i + 1) * h // 8)
        ffn = (route[r, 2:3] * _load_packed_rows(y0_ref.at[g8])
               + route[r, 3:4] * _load_packed_rows(y1_ref.at[g8]))
        ple = plins[i] * _sigmoid(gates[i])
        out_ref[r, :] = _ln(DEEPNORM_ALPHA * x1s[i] + ffn + ple, g2_ref[...], b2_ref[...])


def _final(yg, x1, p2, routet, wple, wpg, g2, b2, half, prev):
    n = x1.shape[0]
    tm = TM_FIN
    nt = n // 2 // tm
    off = half * nt
    rows = lambda w: pl.BlockSpec((tm, w), lambda t: (t + off, 0))
    full = lambda a: pl.BlockSpec(a.shape, lambda t: (0,) * a.ndim)
    in_specs = [pl.BlockSpec((tm // 8, 32, 128), lambda t: (t, 0, 0)),
                pl.BlockSpec((tm // 8, 32, 128), lambda t: (t + nt, 0, 0)),
                rows(D_MODEL), rows(PLE_DIM), pl.BlockSpec((8, tm), lambda t: (0, t + off)),
                full(wple), full(wpg), full(g2), full(b2)]
    args = [yg, yg, x1, p2, routet, wple, wpg, g2, b2]
    aliases = {}
    if prev is not None:
        in_specs.append(pl.BlockSpec(memory_space=pl.ANY))
        args.append(prev)
        aliases = {len(args) - 1: 0}
    return pl.pallas_call(
        _final_kernel,
        grid=(nt,),
        in_specs=in_specs,
        out_specs=rows(D_MODEL),
        out_shape=jax.ShapeDtypeStruct((n, D_MODEL), F32),
        input_output_aliases=aliases,
        compiler_params=pltpu.CompilerParams(dimension_semantics=("parallel",), vmem_limit_bytes=VMEM_LIMIT),
        name=f"final{half}",
    )(*args)


def _route_tables_kernel(e_ref, piece_ref, ends_ref):
    r = e_ref.shape[0]
    e = e_ref[...]
    ri = lax.broadcasted_iota(I32, (128, 128), 0)
    ci = lax.broadcasted_iota(I32, (128, 128), 1)
    upper = jnp.where(ri <= ci, 1.0, 0.0).astype(BF16)
    rr = lax.broadcasted_iota(I32, (r, r), 0)
    rc = lax.broadcasted_iota(I32, (r, r), 1)
    below = jnp.where(rc < rr, 1.0, 0.0).astype(BF16)
    lane = lax.broadcasted_iota(I32, (1, 128), 1)

    rank = jnp.zeros((r, 128), F32)
    counts = jnp.zeros((1, 128), F32)
    for x in range(N_EXPERTS_TOTAL):
        m = jnp.where(e == x, 1.0, 0.0)
        pre = _dot(m.astype(BF16), upper)
        tot = jnp.broadcast_to(pre[:, 127:128], (r, 128))
        off = _dot(below, tot.astype(BF16))
        rank = rank + m * (pre + off)
        counts = jnp.where(lane == x, off[r - 1:r, :] + tot[r - 1:r, :], counts)
    padded = jnp.floor((counts + (TM_MOE - 1)) * (1.0 / TM_MOE)) * TM_MOE
    ends = _dot(jnp.broadcast_to(padded, (8, 128)).astype(BF16), upper)[0:1, :]
    offs = ends - padded
    ends_ref[...] = jnp.broadcast_to(ends, (8, 128)).astype(I32)

    pos = rank - 1.0
    for x in range(N_EXPERTS_TOTAL):
        pos = pos + jnp.where(e == x, offs[:, x:x + 1], 0.0)

    hi = jnp.floor(pos * (1.0 / 256.0))
    lo = pos - 256.0 * hi
    jv = ((lane % 32) // 8).astype(F32)
    for c in range(4):
        sel = jnp.where(ri == 32 * c + 8 * (ci // 32) + ci % 8, 1.0, 0.0).astype(BF16)
        pc = 256.0 * _dot(hi.astype(BF16), sel) + _dot(lo.astype(BF16), sel)
        p8 = jnp.floor(pc * 0.125)
        piece = p8 * (8.0 * SUBROWS) + (pc - 8.0 * p8) + 8.0 * jv
        piece_ref[pl.ds(c, r, stride=4), :] = piece.astype(I32)


def _routing_tables(routet, n):
    tm = TM_MOE
    nt = (2 * n) // tm + N_EXPERTS_TOTAL
    r = 2 * n // 128
    piece, ends = pl.pallas_call(
        _route_tables_kernel,
        out_shape=[jax.ShapeDtypeStruct((4 * r, 128), I32), jax.ShapeDtypeStruct((8, 128), I32)],
        compiler_params=pltpu.CompilerParams(vmem_limit_bytes=VMEM_LIMIT),
        name="route_tables",
    )(routet[0:2].reshape(r, 128))
    ends = ends[0, :N_EXPERTS_TOTAL]
    tile_start = jnp.arange(nt, dtype=I32) * tm
    tile_expert = jnp.minimum(jnp.sum((tile_start[:, None] >= ends[None, :]).astype(I32), axis=1),
                              N_EXPERTS_TOTAL - 1).astype(I32)
    tile_valid = (tile_start < ends[-1]).astype(I32)
    tile_block = jnp.minimum(jnp.arange(nt, dtype=I32), ends[-1] // tm - 1)
    return tile_expert, tile_valid, tile_block, piece.reshape(2, n * SUBROWS)


def kernel(x, p, w_in, a_ln_g, a_ln_b, a_ws, a_bs, w_a_proj, w_b_proj, w_o, ln1_g, ln1_b, w_group_router,
           b_group_router, w_expert_router, b_expert_router, w_gate, w_up, w_down, w_ple, w_ple_gate,
           ln2_g, ln2_b):
    bsz, s, d = x.shape
    n = bsz * s
    assert d == D_MODEL and s % (SPAN * max(B_DILATIONS)) == 0 and n % TM_PROJ == 0
    assert w_in.shape[0] == 1, "one layer"

    w_in_b = w_in[0].astype(BF16)
    a_bias = jnp.repeat(a_bs[0].T, A_WIDTH // 8, axis=1)

    ga, gates, qkv1, qkv2, qkv3 = _proj(x, w_in_b, a_ln_g, a_ln_b, a_ws[0], a_bias)
    o1, l1 = _attn(qkv1.reshape(bsz, 1, s, 3 * COL), 0)
    o2, l2 = _attn(qkv2, 1)
    o3, l3 = _attn(qkv3, 2)

    pad = 128 - N_GROUPS - N_EXPERTS_TOTAL
    wr = jnp.concatenate([w_group_router[0], w_expert_router[0].reshape(d, N_EXPERTS_TOTAL),
                          jnp.zeros((d, pad), F32)], axis=1).astype(BF16)
    br = jnp.concatenate([b_group_router[0], b_expert_router[0].reshape(-1), jnp.zeros((pad,), F32)])[None, :]
    x1, x1p, routet = _mix(
        ga, gates, o1.reshape(n, B_WIDTH), o2, o3, l1.reshape(n, 128), l2, l3, x.reshape(n, d),
        w_a_proj[0].astype(BF16), w_b_proj[0].astype(BF16), w_o[0].astype(BF16), wr, br, ln1_g, ln1_b)

    tile_expert, tile_valid, tile_block, piece = _routing_tables(routet, n)
    nt = tile_expert.shape[0]
    xs = _sc_scatter_rows(x1p.reshape(n * SUBROWS, 128), piece, nt * TM_MOE * SUBROWS)
    ys = _moe(tile_expert, tile_valid, tile_block, xs.reshape(nt * TM_MOE // 8, 32, 128),
              w_gate[0].reshape(N_EXPERTS_TOTAL, d, D_EXPERT), w_up[0].reshape(N_EXPERTS_TOTAL, d, D_EXPERT),
              w_down[0].reshape(N_EXPERTS_TOTAL, D_EXPERT, d)).reshape(nt * TM_MOE * SUBROWS, 128)
    out = None
    hp = n * SUBROWS // 2
    for half in range(2):
        yg = _sc_gather_rows(ys, piece[:, half * hp:(half + 1) * hp].reshape(-1))
        out = _final(yg.reshape(n // 8, 32, 128), x1, p[0].reshape(n, PLE_DIM), routet,
                     w_ple[0].astype(BF16), w_ple_gate[0].astype(BF16), ln2_g, ln2_b, half, out)
    return out.reshape(bsz, s, d)
```

```python
import functools

import jax
import jax.numpy as jnp
from jax import lax
from jax.experimental import pallas as pl
from jax.experimental.pallas import tpu as pltpu
from jax.experimental.pallas import tpu_sc as plsc

F32 = jnp.float32
BF16 = jnp.bfloat16
U32 = jnp.uint32
I32 = jnp.int32

D_MODEL = 1024
PLE_DIM = 256
A_WIDTH = 512
A_CHUNK = 128
B_HEAD_DIM = 64
B_HEADS = 8
B_WIDTH = 512
B_DILATIONS = (1, 4, 16)
SPAN = 128
N_GROUPS = 4
N_EXPERTS = 8
N_EXPERTS_TOTAL = N_GROUPS * N_EXPERTS
D_EXPERT = 256
DEEPNORM_ALPHA = 2.0 ** 0.25
LN_EPS = 1e-5
COL = 512
NEG = -1e30

VMEM_LIMIT = 56 * 1024 * 1024

TM_PROJ = 512
TM_MIX = 512
TM_MOE = 256
TM_FIN = 1024


def _ln(x, g, b):
    mu = jnp.mean(x, axis=-1, keepdims=True)
    xc = x - mu
    var = jnp.mean(xc * xc, axis=-1, keepdims=True)
    return xc * lax.rsqrt(var + LN_EPS) * g + b


def _gelu_tanh(x):
    return 0.5 * x * (1.0 + jnp.tanh(0.7978845608028654 * (x + 0.044715 * (x * x * x))))


def _sigmoid(x):
    return 0.5 * jnp.tanh(0.5 * x) + 0.5


def _dot(a, b):
    return jnp.dot(a, b, preferred_element_type=F32)


PACK_W = D_MODEL // 2
SUBROWS = PACK_W // 128


def _store_packed_rows(ref, x):
    m = x.shape[0]
    xb = x.astype(BF16).astype(F32)
    lo = pltpu.bitcast(xb[:, :PACK_W], U32) >> 16
    hi = pltpu.bitcast(xb[:, PACK_W:], U32) & jnp.uint32(0xFFFF0000)
    w = hi | lo
    for j in range(SUBROWS):
        ref[:, 8 * j:8 * (j + 1), :] = w[:, 128 * j:128 * (j + 1)].reshape(m // 8, 8, 128)


def _load_packed_rows(ref):
    m = ref.shape[0] * 8
    ws = [ref[:, 8 * j:8 * (j + 1), :].reshape(m, 128) for j in range(SUBROWS)]
    lo = [pltpu.bitcast(w << 16, F32) for w in ws]
    hi = [pltpu.bitcast(w & jnp.uint32(0xFFFF0000), F32) for w in ws]
    return jnp.concatenate(lo + hi, axis=1)


def _proj_kernel(x_ref, *refs):
    w = refs[:15]
    lng_ref, lnb_ref, ws_ref, bias_ref = refs[15:19]
    ga_ref, gates_ref, qkv1_ref, qkv2_ref, qkv3_ref = refs[19:24]
    xc_ref = refs[24]
    tm = x_ref.shape[0]
    xb = x_ref[...].astype(BF16)

    u_raw = _dot(xb, w[0][...])
    v_raw = _dot(xb, w[1][...])

    for i in range(4):
        gates_ref[:, i * COL:(i + 1) * COL] = _sigmoid(_dot(xb, w[11 + i][...])).astype(BF16)
    for j in range(3):
        qkv1_ref[:, j * COL:(j + 1) * COL] = _dot(xb, w[2 + 3 * j][...]).astype(BF16)

    for c in range(D_MODEL // 128):
        xc_ref[c] = x_ref[:, c * 128:(c + 1) * 128]
    for gi, out_ref in ((1, qkv2_ref), (2, qkv3_ref)):
        dl = B_DILATIONS[gi]
        per = tm // dl
        xp = jnp.concatenate(
            [jnp.concatenate([xc_ref[c, pl.ds(r, per, stride=dl), :] for c in range(D_MODEL // 128)], axis=1)
             for r in range(dl)], axis=0).astype(BF16)
        for j in range(3):
            res = _dot(xp, w[2 + 3 * j + gi][...]).astype(BF16)
            for r in range(dl):
                out_ref[r, :, j * COL:(j + 1) * COL] = res[r * per:(r + 1) * per]

    u = _gelu_tanh(u_raw)
    v = _gelu_tanh(v_raw)
    vn = _ln(v, lng_ref[...], lnb_ref[...]).astype(BF16)

    row = lax.broadcasted_iota(I32, (A_CHUNK, A_CHUNK), 0)
    colm = lax.broadcasted_iota(I32, (A_CHUNK, A_CHUNK), 1)
    causal = colm <= row
    lo = colm < 64
    zero = jnp.zeros((A_CHUNK, A_CHUNK), BF16)
    wcat = []
    for j in range(4):
        w0 = jnp.where(causal, ws_ref[2 * j], 0.0).astype(BF16)
        w1 = jnp.where(causal, ws_ref[2 * j + 1], 0.0).astype(BF16)
        wcat.append(jnp.concatenate([w0, w1], axis=1))
    for c in range(tm // A_CHUNK):
        r0 = c * A_CHUNK
        for j in range(4):
            c0 = j * 128
            vt = vn[r0:r0 + A_CHUNK, c0:c0 + 128]
            rhs = jnp.concatenate([jnp.where(lo, vt, zero), jnp.where(lo, zero, vt)], axis=0)
            mixed = _dot(wcat[j], rhs) + bias_ref[:, c0:c0 + 128]
            ga_ref[r0:r0 + A_CHUNK, c0:c0 + 128] = (u[r0:r0 + A_CHUNK, c0:c0 + 128] * mixed).astype(BF16)


def _proj(x, w_in_b, a_ln_g, a_ln_b, a_ws, a_bias):
    bsz, s, _ = x.shape
    n = bsz * s
    tm = TM_PROJ
    tiles = s // tm
    x2 = x.reshape(n, D_MODEL)
    wspec = lambda j: pl.BlockSpec((D_MODEL, COL), lambda i, j=j: (0, j), pipeline_mode=pl.Buffered(1))
    full = lambda shape: pl.BlockSpec(shape, lambda i: (0,) * len(shape))
    rows = lambda width: pl.BlockSpec((tm, width), lambda i: (i, 0))
    dil = lambda dl: pl.BlockSpec((None, dl, tm // dl, 3 * COL), lambda i: (i // tiles, 0, i % tiles, 0))
    return pl.pallas_call(
        _proj_kernel,
        grid=(n // tm,),
        in_specs=[rows(D_MODEL)] + [wspec(j) for j in range(15)]
                 + [full((1, A_WIDTH)), full((1, A_WIDTH)), full((8, A_CHUNK, A_CHUNK)), full((A_CHUNK, A_WIDTH))],
        out_specs=[rows(A_WIDTH), rows(4 * COL), rows(3 * COL), dil(4), dil(16)],
        out_shape=[jax.ShapeDtypeStruct((n, A_WIDTH), BF16),
                   jax.ShapeDtypeStruct((n, 4 * COL), BF16),
                   jax.ShapeDtypeStruct((n, 3 * COL), BF16),
                   jax.ShapeDtypeStruct((bsz, 4, s // 4, 3 * COL), BF16),
                   jax.ShapeDtypeStruct((bsz, 16, s // 16, 3 * COL), BF16)],
        scratch_shapes=[pltpu.VMEM((D_MODEL // 128, tm, 128), F32)],
        compiler_params=pltpu.CompilerParams(dimension_semantics=("parallel",), vmem_limit_bytes=VMEM_LIMIT),
        name="proj",
    )(x2, *([w_in_b] * 15), a_ln_g, a_ln_b, a_ws, a_bias)


def _attn_kernel(qkv_ref, o_ref, lse_ref, *, ns, seq):
    nb = seq // SPAN
    lane = lax.broadcasted_iota(I32, (SPAN, 128), 1)
    lo = lane < 64
    lane16 = lane // 16
    qi = lax.broadcasted_iota(I32, (SPAN, 2 * SPAN), 0)
    ki = lax.broadcasted_iota(I32, (SPAN, 2 * SPAN), 1)
    causal = lax.broadcasted_iota(I32, (SPAN, SPAN), 1) <= lax.broadcasted_iota(I32, (SPAN, SPAN), 0)
    bias_first = jnp.where(causal, 0.0, NEG).astype(F32)
    bias_first = jnp.concatenate([bias_first, bias_first], axis=0)
    bias_main = jnp.where((ki >= qi) & (ki <= qi + SPAN), 0.0, NEG).astype(F32)
    bias_main = jnp.concatenate([bias_main, bias_main], axis=0)
    zero = jnp.zeros((SPAN, 128), BF16)

    for s in range(ns):
        def block(row0, start, bias, s=s):
            win = bias.shape[1]
            pairs = range(B_HEADS // 2)
            scores, values = [], []
            for jp in pairs:
                c0 = jp * 128
                q = qkv_ref[s, pl.ds(row0, SPAN), c0:c0 + 128] * jnp.asarray(0.125, BF16)
                k = qkv_ref[s, pl.ds(start, win), COL + c0:COL + c0 + 128]
                values.append(qkv_ref[s, pl.ds(start, win), 2 * COL + c0:2 * COL + c0 + 128])
                qs = jnp.concatenate([jnp.where(lo, q, zero), jnp.where(lo, zero, q)], axis=0)
                scores.append(lax.dot_general(qs, k, (((1,), (1,)), ((), ())), preferred_element_type=F32) + bias)
            probs, maxes, sums = [], [], []
            for jp in pairs:
                m = jnp.max(scores[jp], axis=-1, keepdims=True)
                p = jnp.exp(scores[jp] - m)
                maxes.append(m)
                sums.append(jnp.sum(p, axis=-1, keepdims=True))
                probs.append(p.astype(BF16))
            lse_tile = jnp.zeros((SPAN, 128), F32)
            for jp in pairs:
                c0 = jp * 128
                ov = _dot(probs[jp], values[jp])
                inv = 1.0 / sums[jp]
                o = jnp.where(lo, ov[:SPAN] * inv[:SPAN], ov[SPAN:] * inv[SPAN:])
                o_ref[pl.ds(row0, SPAN), s * B_WIDTH + c0:s * B_WIDTH + c0 + 128] = o.astype(BF16)
                lse = maxes[jp] + jnp.log(sums[jp])
                lse_tile = jnp.where(lane16 == 2 * jp, lse[:SPAN],
                                     jnp.where(lane16 == 2 * jp + 1, lse[SPAN:], lse_tile))
            lse_ref[pl.ds(row0, SPAN), s * 128:(s + 1) * 128] = lse_tile

        block(0, 0, bias_first)
        if nb > 1:
            def body(i, carry):
                block(pl.multiple_of(i * SPAN, SPAN), pl.multiple_of((i - 1) * SPAN, SPAN), bias_main)
                return carry
            lax.fori_loop(1, nb, body, 0)


def _attn(qkv_g, gi):
    bsz, dl, seq, _ = qkv_g.shape
    ns = max(1, min(dl, TM_PROJ // seq))
    return pl.pallas_call(
        functools.partial(_attn_kernel, ns=ns, seq=seq),
        grid=(bsz, dl // ns),
        in_specs=[pl.BlockSpec((None, ns, seq, 3 * COL), lambda b, r: (b, r, 0, 0))],
        out_specs=[pl.BlockSpec((None, seq, ns * B_WIDTH), lambda b, r: (b, 0, r)),
                   pl.BlockSpec((None, seq, ns * 128), lambda b, r: (b, 0, r))],
        out_shape=[jax.ShapeDtypeStruct((bsz, seq, dl * B_WIDTH), BF16),
                   jax.ShapeDtypeStruct((bsz, seq, dl * 128), F32)],
        compiler_params=pltpu.CompilerParams(dimension_semantics=("parallel", "parallel"),
                                             vmem_limit_bytes=VMEM_LIMIT),
        name=f"attn{dl}",
    )(qkv_g)


def _natural_rows(ref, dl, scr):
    nchunk, tm, _ = scr.shape
    w = nchunk * 128
    per = tm // dl
    for r in range(dl):
        for c in range(nchunk):
            scr[c, pl.ds(r, per, stride=dl), :] = ref[:, r * w + c * 128:r * w + (c + 1) * 128].astype(F32)
    return jnp.concatenate([scr[c] for c in range(nchunk)], axis=1)


def _mix_kernel(ga_ref, gates_ref, o1_ref, o2_ref, o3_ref, l1_ref, l2_ref, l3_ref, x_ref,
                wa_ref, wb_ref, wo_ref, wr_ref, br_ref, g1_ref, b1_ref,
                x1_ref, x1p_ref, routet_ref, o2s_ref, o3s_ref, l2s_ref, l3s_ref):
    tm = x_ref.shape[0]
    o2 = _natural_rows(o2_ref, 4, o2s_ref)
    o3 = _natural_rows(o3_ref, 16, o3s_ref)
    l2 = _natural_rows(l2_ref, 4, l2s_ref)
    l3 = _natural_rows(l3_ref, 16, l3s_ref)
    er = lax.broadcasted_iota(I32, (256, B_WIDTH), 0)
    ec = lax.broadcasted_iota(I32, (256, B_WIDTH), 1)
    expand = jnp.where(er % 128 == (ec // B_HEAD_DIM) * 16, 1.0, 0.0).astype(BF16)

    def widen(w):
        hi = w.astype(BF16)
        lo = (w - hi.astype(F32)).astype(BF16)
        return _dot(jnp.concatenate([hi, lo], axis=1), expand)

    h = tm // 2
    halves = (slice(0, h), slice(h, tm))
    obs = []
    for r in halves:
        l1 = l1_ref[r, :]
        mx = jnp.maximum(l1, jnp.maximum(l2[r], l3[r]))
        e1, e2, e3 = jnp.exp(l1 - mx), jnp.exp(l2[r] - mx), jnp.exp(l3[r] - mx)
        inv = 1.0 / (e1 + e2 + e3)
        obs.append(widen(e1 * inv) * o1_ref[r, :].astype(F32) + widen(e2 * inv) * o2[r] + widen(e3 * inv) * o3[r])
    ybs = [_dot(ob.astype(BF16), wb_ref[...]) for ob in obs]
    yas = [_dot(ga_ref[r, :], wa_ref[...]) for r in halves]
    pres = [gates_ref[r, :D_MODEL].astype(F32) * ya + gates_ref[r, D_MODEL:].astype(F32) * yb
            for r, ya, yb in zip(halves, yas, ybs)]
    mixes = [_dot(pre.astype(BF16), wo_ref[...]) for pre in pres]
    x1s = [_ln(DEEPNORM_ALPHA * x_ref[r, :] + mix, g1_ref[...], b1_ref[...]) for r, mix in zip(halves, mixes)]
    logits = [_dot(x1.astype(BF16), wr_ref[...]) + br_ref[...] for x1 in x1s]

    nrow = 40
    row = lax.broadcasted_iota(I32, (nrow, h), 0).astype(F32)
    row8 = lax.broadcasted_iota(I32, (8, h), 0)
    big = 1e9
    for i, r in enumerate(halves):
        x1_ref[r, :] = x1s[i]
        _store_packed_rows(x1p_ref.at[i * h // 8:(i + 1) * h // 8], x1s[i])
        lg = logits[i].T[:nrow, :]
        gl = jnp.where(row < N_GROUPS, lg, NEG)
        gm = jnp.max(gl, axis=0, keepdims=True)
        gidx = jnp.min(jnp.where(gl == gm, row, big), axis=0, keepdims=True)
        gsum = jnp.sum(jnp.where(row < N_GROUPS, jnp.exp(gl - gm), 0.0), axis=0, keepdims=True)
        gprob = 1.0 / gsum
        lo_row = N_GROUPS + N_EXPERTS * gidx
        el = jnp.where((row >= lo_row) & (row < lo_row + N_EXPERTS), lg, NEG)
        v1 = jnp.max(el, axis=0, keepdims=True)
        i1 = jnp.min(jnp.where(el == v1, row, big), axis=0, keepdims=True)
        el2 = jnp.where(row == i1, NEG, el)
        v2 = jnp.max(el2, axis=0, keepdims=True)
        i2 = jnp.min(jnp.where(el2 == v2, row, big), axis=0, keepdims=True)
        t = jnp.exp(v2 - v1)
        w1 = 1.0 / (1.0 + t)
        w2 = t * w1
        routet_ref[:, r] = jnp.where(row8 == 0, i1 - N_GROUPS,
                                     jnp.where(row8 == 1, i2 - N_GROUPS,
                                               jnp.where(row8 == 2, gprob * w1,
                                                         jnp.where(row8 == 3, gprob * w2, 0.0))))


def _mix(ga, gates, o1, o2, o3, l1, l2, l3, x2, wa, wb, wo, wr, br, g1, b1):
    n = x2.shape[0]
    bsz = o2.shape[0]
    tm = TM_MIX
    tiles = n // bsz // tm
    rows = lambda w: pl.BlockSpec((tm, w), lambda i: (i, 0))
    grouped = lambda a, dl: pl.BlockSpec((None, tm // dl, a.shape[2]), lambda i: (i // tiles, i % tiles, 0))
    full = lambda a: pl.BlockSpec(a.shape, lambda i: (0,) * a.ndim)
    return pl.pallas_call(
        _mix_kernel,
        grid=(n // tm,),
        in_specs=[rows(A_WIDTH), rows(2 * D_MODEL), rows(B_WIDTH), grouped(o2, 4), grouped(o3, 16),
                  rows(128), grouped(l2, 4), grouped(l3, 16), rows(D_MODEL),
                  full(wa), full(wb), full(wo), full(wr), full(br), full(g1), full(b1)],
        out_specs=[rows(D_MODEL), pl.BlockSpec((tm // 8, 32, 128), lambda i: (i, 0, 0)),
                   pl.BlockSpec((8, tm), lambda i: (0, i))],
        out_shape=[jax.ShapeDtypeStruct((n, D_MODEL), F32),
                   jax.ShapeDtypeStruct((n // 8, 32, 128), U32),
                   jax.ShapeDtypeStruct((8, n), F32)],
        scratch_shapes=[pltpu.VMEM((B_WIDTH // 128, tm, 128), F32), pltpu.VMEM((B_WIDTH // 128, tm, 128), F32),
                        pltpu.VMEM((1, tm, 128), F32), pltpu.VMEM((1, tm, 128), F32)],
        compiler_params=pltpu.CompilerParams(dimension_semantics=("parallel",), vmem_limit_bytes=VMEM_LIMIT),
        name="mix",
    )(ga, gates, o1, o2, o3, l1, l2, l3, x2, wa, wb, wo, wr, br, g1, b1)


SC_WINDOW = 128


def _sc_mesh():
    return plsc.VectorSubcoreMesh(core_axis_name="core", subcore_axis_name="subcore")


def _sc_scatter_rows(rows, dst, n_out):
    r = rows.shape[0]

    @pl.kernel(out_type=jax.ShapeDtypeStruct((n_out, 128), rows.dtype), mesh=_sc_mesh())
    def scatter(rows_hbm, dst0_hbm, dst1_hbm, out_hbm):
        def body(rows_vmem, dst0_vmem, dst1_vmem):
            pltpu.sync_copy(rows_vmem, out_hbm.at[dst0_vmem.at[0]])
            pltpu.sync_copy(rows_vmem, out_hbm.at[dst1_vmem.at[0]])

        pltpu.emit_pipeline(
            body,
            grid=(r // SC_WINDOW,),
            in_specs=[pl.BlockSpec((SC_WINDOW, 128), lambda i: (i, 0)),
                      pl.BlockSpec((1, SC_WINDOW), lambda i: (0, i)),
                      pl.BlockSpec((1, SC_WINDOW), lambda i: (0, i))],
            out_specs=[],
            core_axis_name=("core", "subcore"),
            dimension_semantics=(pltpu.PARALLEL,),
        )(rows_hbm, dst0_hbm, dst1_hbm)

    return scatter(rows, dst[0:1], dst[1:2])


def _sc_gather_rows(table, src):
    m = src.shape[0]

    @pl.kernel(out_type=jax.ShapeDtypeStruct((m, 128), table.dtype), mesh=_sc_mesh())
    def gather(table_hbm, src_hbm, out_hbm):
        def body(src_vmem, out_vmem):
            pltpu.sync_copy(table_hbm.at[src_vmem.at[0]], out_vmem)

        pltpu.emit_pipeline(
            body,
            grid=(m // SC_WINDOW,),
            in_specs=[pl.BlockSpec((1, SC_WINDOW), lambda i: (0, i))],
            out_specs=[pl.BlockSpec((SC_WINDOW, 128), lambda i: (i, 0))],
            core_axis_name=("core", "subcore"),
            dimension_semantics=(pltpu.PARALLEL,),
        )(src_hbm, out_hbm)

    return gather(table, src.reshape(1, m))


def _moe_kernel(off_ref, cnt_ref, xs_hbm, wg_ref, wu_ref, wd_ref, ys_hbm,
                xbuf, ybuf, wgb_ref, wub_ref, wdb_ref, sem_in, sem_out):
    e = pl.program_id(0)
    ntile = cnt_ref[e]
    base = off_ref[e]
    grp = TM_MOE // 8

    def in_copy(g0, slot):
        return pltpu.make_async_copy(xs_hbm.at[pl.ds(g0, grp)], xbuf.at[slot], sem_in.at[slot])

    def out_copy(g0, slot):
        return pltpu.make_async_copy(ybuf.at[slot], ys_hbm.at[pl.ds(g0, grp)], sem_out.at[slot])

    @pl.when((e == 0) & (ntile > 0))
    def _():
        in_copy(base, 0).start()

    wgb_ref[...] = wg_ref[...].astype(BF16)
    wub_ref[...] = wu_ref[...].astype(BF16)
    wdb_ref[...] = wd_ref[...].astype(BF16)

    def tile(i, carry):
        slot = i % 2
        g0 = base + i * grp
        in_copy(g0, slot).wait()

        @pl.when(i + 1 < ntile)
        def _():
            in_copy(g0 + grp, 1 - slot).start()

        xb = _load_packed_rows(xbuf.at[slot]).astype(BF16)
        g = _dot(xb, wgb_ref[...])
        u = _dot(xb, wub_ref[...])
        h = (g * _sigmoid(g) * u).astype(BF16)
        y = _dot(h, wdb_ref[...])

        @pl.when(i >= 2)
        def _():
            out_copy(g0, slot).wait()

        _store_packed_rows(ybuf.at[slot], y)
        out_copy(g0, slot).start()
        return carry

    lax.fori_loop(0, ntile, tile, 0)

    @pl.when(ntile >= 1)
    def _():
        out_copy(base, (ntile - 1) % 2).wait()

    @pl.when(ntile >= 2)
    def _():
        out_copy(base, ntile % 2).wait()

    nxt = jnp.minimum(e + 1, pl.num_programs(0) - 1)

    @pl.when((e + 1 < pl.num_programs(0)) & (cnt_ref[nxt] > 0))
    def _():
        in_copy(off_ref[nxt], 0).start()


def _moe(tile_off, tile_cnt, xs, wg, wu, wd):
    grp = TM_MOE // 8
    wspec = lambda a: pl.BlockSpec((None,) + a.shape[1:], lambda e, off, cnt: (e, 0, 0))
    return pl.pallas_call(
        _moe_kernel,
        grid_spec=pltpu.PrefetchScalarGridSpec(
            num_scalar_prefetch=2,
            grid=(N_EXPERTS_TOTAL,),
            in_specs=[pl.BlockSpec(memory_space=pl.ANY), wspec(wg), wspec(wu), wspec(wd)],
            out_specs=pl.BlockSpec(memory_space=pl.ANY),
            scratch_shapes=[pltpu.VMEM((2, grp, 32, 128), U32), pltpu.VMEM((2, grp, 32, 128), U32),
                            pltpu.VMEM((D_MODEL, D_EXPERT), BF16), pltpu.VMEM((D_MODEL, D_EXPERT), BF16),
                            pltpu.VMEM((D_EXPERT, D_MODEL), BF16),
                            pltpu.SemaphoreType.DMA((2,)), pltpu.SemaphoreType.DMA((2,))]),
        out_shape=jax.ShapeDtypeStruct(xs.shape, U32),
        compiler_params=pltpu.CompilerParams(dimension_semantics=("arbitrary",), vmem_limit_bytes=VMEM_LIMIT),
        name="moe",
    )(tile_off, tile_cnt, xs, wg, wu, wd)


def _final_kernel(y0_ref, y1_ref, x1_ref, p_ref, routet_ref, wple_ref, wpg_ref, g2_ref, b2_ref, *rest):
    out_ref = rest[-1]
    tm = x1_ref.shape[0]
    route = routet_ref[...].T
    h = tm // 2
    halves = (slice(0, h), slice(h, tm))
    x1s = [x1_ref[r, :] for r in halves]
    gates = [_dot(x1.astype(BF16), wpg_ref[...]) for x1 in x1s]
    plins = [_dot(p_ref[r, :].astype(BF16), wple_ref[...]) for r in halves]
    for i, r in enumerate(halves):
        g8 = slice(i * h // 8, (i + 1) * h // 8)
        ffn = (route[r, 2:3] * _load_packed_rows(y0_ref.at[g8])
               + route[r, 3:4] * _load_packed_rows(y1_ref.at[g8]))
        ple = plins[i] * _sigmoid(gates[i])
        out_ref[r, :] = _ln(DEEPNORM_ALPHA * x1s[i] + ffn + ple, g2_ref[...], b2_ref[...])


def _final(yg, x1, p2, routet, wple, wpg, g2, b2, half, prev):
    n = x1.shape[0]
    tm = TM_FIN
    nt = n // 2 // tm
    off = half * nt
    rows = lambda w: pl.BlockSpec((tm, w), lambda t: (t + off, 0))
    full = lambda a: pl.BlockSpec(a.shape, lambda t: (0,) * a.ndim)
    in_specs = [pl.BlockSpec((tm // 8, 32, 128), lambda t: (t, 0, 0)),
                pl.BlockSpec((tm // 8, 32, 128), lambda t: (t + nt, 0, 0)),
                rows(D_MODEL), rows(PLE_DIM), pl.BlockSpec((8, tm), lambda t: (0, t + off)),
                full(wple), full(wpg), full(g2), full(b2)]
    args = [yg, yg, x1, p2, routet, wple, wpg, g2, b2]
    aliases = {}
    if prev is not None:
        in_specs.append(pl.BlockSpec(memory_space=pl.ANY))
        args.append(prev)
        aliases = {len(args) - 1: 0}
    return pl.pallas_call(
        _final_kernel,
        grid=(nt,),
        in_specs=in_specs,
        out_specs=rows(D_MODEL),
        out_shape=jax.ShapeDtypeStruct((n, D_MODEL), F32),
        input_output_aliases=aliases,
        compiler_params=pltpu.CompilerParams(dimension_semantics=("parallel",), vmem_limit_bytes=VMEM_LIMIT),
        name=f"final{half}",
    )(*args)


def _route_tables_kernel(e_ref, piece_ref, ends_ref):
    r = e_ref.shape[0]
    e = e_ref[...]
    ri = lax.broadcasted_iota(I32, (128, 128), 0)
    ci = lax.broadcasted_iota(I32, (128, 128), 1)
    upper = jnp.where(ri <= ci, 1.0, 0.0).astype(BF16)
    rr = lax.broadcasted_iota(I32, (r, r), 0)
    rc = lax.broadcasted_iota(I32, (r, r), 1)
    below = jnp.where(rc < rr, 1.0, 0.0).astype(BF16)
    lane = lax.broadcasted_iota(I32, (1, 128), 1)

    rank = jnp.zeros((r, 128), F32)
    counts = jnp.zeros((1, 128), F32)
    for x in range(N_EXPERTS_TOTAL):
        m = jnp.where(e == x, 1.0, 0.0)
        pre = _dot(m.astype(BF16), upper)
        tot = jnp.broadcast_to(pre[:, 127:128], (r, 128))
        off = _dot(below, tot.astype(BF16))
        rank = rank + m * (pre + off)
        counts = jnp.where(lane == x, off[r - 1:r, :] + tot[r - 1:r, :], counts)
    padded = jnp.floor((counts + (TM_MOE - 1)) * (1.0 / TM_MOE)) * TM_MOE
    ends = _dot(jnp.broadcast_to(padded, (8, 128)).astype(BF16), upper)[0:1, :]
    offs = ends - padded
    ends_ref[...] = jnp.broadcast_to(ends, (8, 128)).astype(I32)

    pos = rank - 1.0
    for x in range(N_EXPERTS_TOTAL):
        pos = pos + jnp.where(e == x, offs[:, x:x + 1], 0.0)

    hi = jnp.floor(pos * (1.0 / 256.0))
    lo = pos - 256.0 * hi
    jv = ((lane % 32) // 8).astype(F32)
    for c in range(4):
        sel = jnp.where(ri == 32 * c + 8 * (ci // 32) + ci % 8, 1.0, 0.0).astype(BF16)
        pc = 256.0 * _dot(hi.astype(BF16), sel) + _dot(lo.astype(BF16), sel)
        p8 = jnp.floor(pc * 0.125)
        piece = p8 * (8.0 * SUBROWS) + (pc - 8.0 * p8) + 8.0 * jv
        piece_ref[pl.ds(c, r, stride=4), :] = piece.astype(I32)


def _routing_tables(routet, n):
    tm = TM_MOE
    r = 2 * n // 128
    piece, ends = pl.pallas_call(
        _route_tables_kernel,
        out_shape=[jax.ShapeDtypeStruct((4 * r, 128), I32), jax.ShapeDtypeStruct((8, 128), I32)],
        compiler_params=pltpu.CompilerParams(vmem_limit_bytes=VMEM_LIMIT),
        name="route_tables",
    )(routet[0:2].reshape(r, 128))
    ends = ends[0, :N_EXPERTS_TOTAL]
    starts = jnp.concatenate([jnp.zeros((1,), I32), ends[:-1]])
    return starts // 8, (ends - starts) // tm, piece.reshape(2, n * SUBROWS)


def kernel(x, p, w_in, a_ln_g, a_ln_b, a_ws, a_bs, w_a_proj, w_b_proj, w_o, ln1_g, ln1_b, w_group_router,
           b_group_router, w_expert_router, b_expert_router, w_gate, w_up, w_down, w_ple, w_ple_gate,
           ln2_g, ln2_b):
    bsz, s, d = x.shape
    n = bsz * s
    assert d == D_MODEL and s % (SPAN * max(B_DILATIONS)) == 0 and n % TM_PROJ == 0
    assert w_in.shape[0] == 1, "one layer"

    w_in_b = w_in[0].astype(BF16)
    a_bias = jnp.repeat(a_bs[0].T, A_WIDTH // 8, axis=1)

    ga, gates, qkv1, qkv2, qkv3 = _proj(x, w_in_b, a_ln_g, a_ln_b, a_ws[0], a_bias)
    o1, l1 = _attn(qkv1.reshape(bsz, 1, s, 3 * COL), 0)
    o2, l2 = _attn(qkv2, 1)
    o3, l3 = _attn(qkv3, 2)

    pad = 128 - N_GROUPS - N_EXPERTS_TOTAL
    wr = jnp.concatenate([w_group_router[0], w_expert_router[0].reshape(d, N_EXPERTS_TOTAL),
                          jnp.zeros((d, pad), F32)], axis=1).astype(BF16)
    br = jnp.concatenate([b_group_router[0], b_expert_router[0].reshape(-1), jnp.zeros((pad,), F32)])[None, :]
    x1, x1p, routet = _mix(
        ga, gates, o1.reshape(n, B_WIDTH), o2, o3, l1.reshape(n, 128), l2, l3, x.reshape(n, d),
        w_a_proj[0].astype(BF16), w_b_proj[0].astype(BF16), w_o[0].astype(BF16), wr, br, ln1_g, ln1_b)

    tile_off, tile_cnt, piece = _routing_tables(routet, n)
    cap = 2 * n + N_EXPERTS_TOTAL * TM_MOE
    xs = _sc_scatter_rows(x1p.reshape(n * SUBROWS, 128), piece, cap * SUBROWS)
    ys = _moe(tile_off, tile_cnt, xs.reshape(cap // 8, 32, 128),
              w_gate[0].reshape(N_EXPERTS_TOTAL, d, D_EXPERT), w_up[0].reshape(N_EXPERTS_TOTAL, d, D_EXPERT),
              w_down[0].reshape(N_EXPERTS_TOTAL, D_EXPERT, d)).reshape(cap * SUBROWS, 128)
    out = None
    hp = n * SUBROWS // 2
    for half in range(2):
        yg = _sc_gather_rows(ys, piece[:, half * hp:(half + 1) * hp].reshape(-1))
        out = _final(yg.reshape(n // 8, 32, 128), x1, p[0].reshape(n, PLE_DIM), routet,
                     w_ple[0].astype(BF16), w_ple_gate[0].astype(BF16), ln2_g, ln2_b, half, out)
    return out.reshape(bsz, s, d)
```

```python
import functools

import jax
import jax.numpy as jnp
from jax import lax
from jax.experimental import pallas as pl
from jax.experimental.pallas import tpu as pltpu
from jax.experimental.pallas import tpu_sc as plsc

F32 = jnp.float32
BF16 = jnp.bfloat16
U32 = jnp.uint32
I32 = jnp.int32

D_MODEL = 1024
PLE_DIM = 256
A_WIDTH = 512
A_CHUNK = 128
B_HEAD_DIM = 64
B_HEADS = 8
B_WIDTH = 512
B_DILATIONS = (1, 4, 16)
SPAN = 128
N_GROUPS = 4
N_EXPERTS = 8
N_EXPERTS_TOTAL = N_GROUPS * N_EXPERTS
D_EXPERT = 256
DEEPNORM_ALPHA = 2.0 ** 0.25
LN_EPS = 1e-5
COL = 512
NEG = -1e30

VMEM_LIMIT = 56 * 1024 * 1024

TM_PROJ = 512
TM_ATTN = 1024
TM_MIX = 512
TM_MOE = 512
TM_FIN = 1024


def _ln(x, g, b):
    mu = jnp.mean(x, axis=-1, keepdims=True)
    xc = x - mu
    var = jnp.mean(xc * xc, axis=-1, keepdims=True)
    return xc * lax.rsqrt(var + LN_EPS) * g + b


def _gelu_tanh(x):
    return 0.5 * x * (1.0 + jnp.tanh(0.7978845608028654 * (x + 0.044715 * (x * x * x))))


def _sigmoid(x):
    return 0.5 * jnp.tanh(0.5 * x) + 0.5


def _dot(a, b):
    return jnp.dot(a, b, preferred_element_type=F32)


PACK_W = D_MODEL // 2
SUBROWS = PACK_W // 128


def _store_packed_rows(ref, x):
    m = x.shape[0]
    xb = x.astype(BF16).astype(F32)
    lo = pltpu.bitcast(xb[:, :PACK_W], U32) >> 16
    hi = pltpu.bitcast(xb[:, PACK_W:], U32) & jnp.uint32(0xFFFF0000)
    w = hi | lo
    for j in range(SUBROWS):
        ref[:, 8 * j:8 * (j + 1), :] = w[:, 128 * j:128 * (j + 1)].reshape(m // 8, 8, 128)


def _load_packed_rows(ref):
    m = ref.shape[0] * 8
    ws = [ref[:, 8 * j:8 * (j + 1), :].reshape(m, 128) for j in range(SUBROWS)]
    lo = [pltpu.bitcast(w << 16, F32) for w in ws]
    hi = [pltpu.bitcast(w & jnp.uint32(0xFFFF0000), F32) for w in ws]
    return jnp.concatenate(lo + hi, axis=1)


def _proj_kernel(x_ref, *refs):
    w = refs[:15]
    lng_ref, lnb_ref, ws_ref, bias_ref = refs[15:19]
    ga_ref, gates_ref, qkv1_ref, qkv2_ref, qkv3_ref = refs[19:24]
    xc_ref = refs[24]
    tm = x_ref.shape[0]
    xb = x_ref[...].astype(BF16)

    u_raw = _dot(xb, w[0][...])
    v_raw = _dot(xb, w[1][...])

    for i in range(4):
        gates_ref[:, i * COL:(i + 1) * COL] = _sigmoid(_dot(xb, w[11 + i][...])).astype(BF16)
    for j in range(3):
        qkv1_ref[:, j * COL:(j + 1) * COL] = _dot(xb, w[2 + 3 * j][...]).astype(BF16)

    for c in range(D_MODEL // 128):
        xc_ref[c] = x_ref[:, c * 128:(c + 1) * 128]
    for gi, out_ref in ((1, qkv2_ref), (2, qkv3_ref)):
        dl = B_DILATIONS[gi]
        per = tm // dl
        xp = jnp.concatenate(
            [jnp.concatenate([xc_ref[c, pl.ds(r, per, stride=dl), :] for c in range(D_MODEL // 128)], axis=1)
             for r in range(dl)], axis=0).astype(BF16)
        for j in range(3):
            res = _dot(xp, w[2 + 3 * j + gi][...]).astype(BF16)
            for r in range(dl):
                out_ref[r, :, j * COL:(j + 1) * COL] = res[r * per:(r + 1) * per]

    u = _gelu_tanh(u_raw)
    v = _gelu_tanh(v_raw)
    vn = _ln(v, lng_ref[...], lnb_ref[...]).astype(BF16)

    row = lax.broadcasted_iota(I32, (A_CHUNK, A_CHUNK), 0)
    colm = lax.broadcasted_iota(I32, (A_CHUNK, A_CHUNK), 1)
    causal = colm <= row
    lo = colm < 64
    zero = jnp.zeros((A_CHUNK, A_CHUNK), BF16)
    wcat = []
    for j in range(4):
        w0 = jnp.where(causal, ws_ref[2 * j], 0.0).astype(BF16)
        w1 = jnp.where(causal, ws_ref[2 * j + 1], 0.0).astype(BF16)
        wcat.append(jnp.concatenate([w0, w1], axis=1))
    for c in range(tm // A_CHUNK):
        r0 = c * A_CHUNK
        for j in range(4):
            c0 = j * 128
            vt = vn[r0:r0 + A_CHUNK, c0:c0 + 128]
            rhs = jnp.concatenate([jnp.where(lo, vt, zero), jnp.where(lo, zero, vt)], axis=0)
            mixed = _dot(wcat[j], rhs) + bias_ref[:, c0:c0 + 128]
            ga_ref[r0:r0 + A_CHUNK, c0:c0 + 128] = (u[r0:r0 + A_CHUNK, c0:c0 + 128] * mixed).astype(BF16)


def _proj(x, w_in_b, a_ln_g, a_ln_b, a_ws, a_bias):
    bsz, s, _ = x.shape
    n = bsz * s
    tm = TM_PROJ
    tiles = s // tm
    x2 = x.reshape(n, D_MODEL)
    wspec = lambda j: pl.BlockSpec((D_MODEL, COL), lambda i, j=j: (0, j), pipeline_mode=pl.Buffered(1))
    full = lambda shape: pl.BlockSpec(shape, lambda i: (0,) * len(shape))
    rows = lambda width: pl.BlockSpec((tm, width), lambda i: (i, 0))
    dil = lambda dl: pl.BlockSpec((None, dl, tm // dl, 3 * COL), lambda i: (i // tiles, 0, i % tiles, 0))
    return pl.pallas_call(
        _proj_kernel,
        grid=(n // tm,),
        in_specs=[rows(D_MODEL)] + [wspec(j) for j in range(15)]
                 + [full((1, A_WIDTH)), full((1, A_WIDTH)), full((8, A_CHUNK, A_CHUNK)), full((A_CHUNK, A_WIDTH))],
        out_specs=[rows(A_WIDTH), rows(4 * COL), rows(3 * COL), dil(4), dil(16)],
        out_shape=[jax.ShapeDtypeStruct((n, A_WIDTH), BF16),
                   jax.ShapeDtypeStruct((n, 4 * COL), BF16),
                   jax.ShapeDtypeStruct((n, 3 * COL), BF16),
                   jax.ShapeDtypeStruct((bsz, 4, s // 4, 3 * COL), BF16),
                   jax.ShapeDtypeStruct((bsz, 16, s // 16, 3 * COL), BF16)],
        scratch_shapes=[pltpu.VMEM((D_MODEL // 128, tm, 128), F32)],
        compiler_params=pltpu.CompilerParams(dimension_semantics=("parallel",), vmem_limit_bytes=VMEM_LIMIT),
        name="proj",
    )(x2, *([w_in_b] * 15), a_ln_g, a_ln_b, a_ws, a_bias)


def _attn_kernel(qkv_ref, o_ref, lse_ref, *, ns, seq):
    nb = seq // SPAN
    lane = lax.broadcasted_iota(I32, (SPAN, 128), 1)
    lo = lane < 64
    lane16 = lane // 16
    qi = lax.broadcasted_iota(I32, (SPAN, 2 * SPAN), 0)
    ki = lax.broadcasted_iota(I32, (SPAN, 2 * SPAN), 1)
    causal = lax.broadcasted_iota(I32, (SPAN, SPAN), 1) <= lax.broadcasted_iota(I32, (SPAN, SPAN), 0)
    bias_first = jnp.where(causal, 0.0, NEG).astype(F32)
    bias_first = jnp.concatenate([bias_first, bias_first], axis=0)
    bias_main = jnp.where((ki >= qi) & (ki <= qi + SPAN), 0.0, NEG).astype(F32)
    bias_main = jnp.concatenate([bias_main, bias_main], axis=0)
    zero = jnp.zeros((SPAN, 128), BF16)

    for s in range(ns):
        def block(row0, start, bias, s=s):
            win = bias.shape[1]
            pairs = range(B_HEADS // 2)
            scores, values = [], []
            for jp in pairs:
                c0 = jp * 128
                q = qkv_ref[s, pl.ds(row0, SPAN), c0:c0 + 128] * jnp.asarray(0.125, BF16)
                k = qkv_ref[s, pl.ds(start, win), COL + c0:COL + c0 + 128]
                values.append(qkv_ref[s, pl.ds(start, win), 2 * COL + c0:2 * COL + c0 + 128])
                qs = jnp.concatenate([jnp.where(lo, q, zero), jnp.where(lo, zero, q)], axis=0)
                scores.append(lax.dot_general(qs, k, (((1,), (1,)), ((), ())), preferred_element_type=F32) + bias)
            probs, maxes, sums = [], [], []
            for jp in pairs:
                m = jnp.max(scores[jp], axis=-1, keepdims=True)
                p = jnp.exp(scores[jp] - m)
                maxes.append(m)
                sums.append(jnp.sum(p, axis=-1, keepdims=True))
                probs.append(p.astype(BF16))
            lse_tile = jnp.zeros((SPAN, 128), F32)
            for jp in pairs:
                c0 = jp * 128
                ov = _dot(probs[jp], values[jp])
                inv = 1.0 / sums[jp]
                o = jnp.where(lo, ov[:SPAN] * inv[:SPAN], ov[SPAN:] * inv[SPAN:])
                o_ref[pl.ds(row0, SPAN), s * B_WIDTH + c0:s * B_WIDTH + c0 + 128] = o.astype(BF16)
                lse = maxes[jp] + jnp.log(sums[jp])
                lse_tile = jnp.where(lane16 == 2 * jp, lse[:SPAN],
                                     jnp.where(lane16 == 2 * jp + 1, lse[SPAN:], lse_tile))
            lse_ref[pl.ds(row0, SPAN), s * 128:(s + 1) * 128] = lse_tile

        block(0, 0, bias_first)
        if nb > 1:
            def body(i, carry):
                block(pl.multiple_of(i * SPAN, SPAN), pl.multiple_of((i - 1) * SPAN, SPAN), bias_main)
                return carry
            lax.fori_loop(1, nb, body, 0)


def _attn(qkv_g, gi):
    bsz, dl, seq, _ = qkv_g.shape
    ns = max(1, min(dl, TM_ATTN // seq))
    return pl.pallas_call(
        functools.partial(_attn_kernel, ns=ns, seq=seq),
        grid=(bsz, dl // ns),
        in_specs=[pl.BlockSpec((None, ns, seq, 3 * COL), lambda b, r: (b, r, 0, 0))],
        out_specs=[pl.BlockSpec((None, seq, ns * B_WIDTH), lambda b, r: (b, 0, r)),
                   pl.BlockSpec((None, seq, ns * 128), lambda b, r: (b, 0, r))],
        out_shape=[jax.ShapeDtypeStruct((bsz, seq, dl * B_WIDTH), BF16),
                   jax.ShapeDtypeStruct((bsz, seq, dl * 128), F32)],
        compiler_params=pltpu.CompilerParams(dimension_semantics=("parallel", "parallel"),
                                             vmem_limit_bytes=VMEM_LIMIT),
        name=f"attn{dl}",
    )(qkv_g)


def _natural_rows(ref, dl, scr):
    nchunk, tm, _ = scr.shape
    w = nchunk * 128
    per = tm // dl
    for r in range(dl):
        for c in range(nchunk):
            scr[c, pl.ds(r, per, stride=dl), :] = ref[:, r * w + c * 128:r * w + (c + 1) * 128].astype(F32)
    return jnp.concatenate([scr[c] for c in range(nchunk)], axis=1)


def _mix_kernel(ga_ref, gates_ref, o1_ref, o2_ref, o3_ref, l1_ref, l2_ref, l3_ref, x_ref,
                wa_ref, wb_ref, wo_ref, wr_ref, br_ref, g1_ref, b1_ref,
                x1_ref, x1p_ref, routet_ref, o2s_ref, o3s_ref, l2s_ref, l3s_ref):
    tm = x_ref.shape[0]
    o2 = _natural_rows(o2_ref, 4, o2s_ref)
    o3 = _natural_rows(o3_ref, 16, o3s_ref)
    l2 = _natural_rows(l2_ref, 4, l2s_ref)
    l3 = _natural_rows(l3_ref, 16, l3s_ref)
    er = lax.broadcasted_iota(I32, (256, B_WIDTH), 0)
    ec = lax.broadcasted_iota(I32, (256, B_WIDTH), 1)
    expand = jnp.where(er % 128 == (ec // B_HEAD_DIM) * 16, 1.0, 0.0).astype(BF16)

    def widen(w):
        hi = w.astype(BF16)
        lo = (w - hi.astype(F32)).astype(BF16)
        return _dot(jnp.concatenate([hi, lo], axis=1), expand)

    h = tm // 2
    halves = (slice(0, h), slice(h, tm))
    obs = []
    for r in halves:
        l1 = l1_ref[r, :]
        mx = jnp.maximum(l1, jnp.maximum(l2[r], l3[r]))
        e1, e2, e3 = jnp.exp(l1 - mx), jnp.exp(l2[r] - mx), jnp.exp(l3[r] - mx)
        inv = 1.0 / (e1 + e2 + e3)
        obs.append(widen(e1 * inv) * o1_ref[r, :].astype(F32) + widen(e2 * inv) * o2[r] + widen(e3 * inv) * o3[r])
    ybs = [_dot(ob.astype(BF16), wb_ref[...]) for ob in obs]
    yas = [_dot(ga_ref[r, :], wa_ref[...]) for r in halves]
    pres = [gates_ref[r, :D_MODEL].astype(F32) * ya + gates_ref[r, D_MODEL:].astype(F32) * yb
            for r, ya, yb in zip(halves, yas, ybs)]
    mixes = [_dot(pre.astype(BF16), wo_ref[...]) for pre in pres]
    x1s = [_ln(DEEPNORM_ALPHA * x_ref[r, :] + mix, g1_ref[...], b1_ref[...]) for r, mix in zip(halves, mixes)]
    logits = [_dot(x1.astype(BF16), wr_ref[...]) + br_ref[...] for x1 in x1s]

    nrow = 40
    row = lax.broadcasted_iota(I32, (nrow, h), 0).astype(F32)
    row8 = lax.broadcasted_iota(I32, (8, h), 0)
    big = 1e9
    for i, r in enumerate(halves):
        x1_ref[r, :] = x1s[i]
        _store_packed_rows(x1p_ref.at[i * h // 8:(i + 1) * h // 8], x1s[i])
        lg = logits[i].T[:nrow, :]
        gl = jnp.where(row < N_GROUPS, lg, NEG)
        gm = jnp.max(gl, axis=0, keepdims=True)
        gidx = jnp.min(jnp.where(gl == gm, row, big), axis=0, keepdims=True)
        gsum = jnp.sum(jnp.where(row < N_GROUPS, jnp.exp(gl - gm), 0.0), axis=0, keepdims=True)
        gprob = 1.0 / gsum
        lo_row = N_GROUPS + N_EXPERTS * gidx
        el = jnp.where((row >= lo_row) & (row < lo_row + N_EXPERTS), lg, NEG)
        v1 = jnp.max(el, axis=0, keepdims=True)
        i1 = jnp.min(jnp.where(el == v1, row, big), axis=0, keepdims=True)
        el2 = jnp.where(row == i1, NEG, el)
        v2 = jnp.max(el2, axis=0, keepdims=True)
        i2 = jnp.min(jnp.where(el2 == v2, row, big), axis=0, keepdims=True)
        t = jnp.exp(v2 - v1)
        w1 = 1.0 / (1.0 + t)
        w2 = t * w1
        routet_ref[:, r] = jnp.where(row8 == 0, i1 - N_GROUPS,
                                     jnp.where(row8 == 1, i2 - N_GROUPS,
                                               jnp.where(row8 == 2, gprob * w1,
                                                         jnp.where(row8 == 3, gprob * w2, 0.0))))


def _mix(ga, gates, o1, o2, o3, l1, l2, l3, x2, wa, wb, wo, wr, br, g1, b1):
    n = x2.shape[0]
    bsz = o2.shape[0]
    tm = TM_MIX
    tiles = n // bsz // tm
    rows = lambda w: pl.BlockSpec((tm, w), lambda i: (i, 0))
    grouped = lambda a, dl: pl.BlockSpec((None, tm // dl, a.shape[2]), lambda i: (i // tiles, i % tiles, 0))
    full = lambda a: pl.BlockSpec(a.shape, lambda i: (0,) * a.ndim)
    return pl.pallas_call(
        _mix_kernel,
        grid=(n // tm,),
        in_specs=[rows(A_WIDTH), rows(2 * D_MODEL), rows(B_WIDTH), grouped(o2, 4), grouped(o3, 16),
                  rows(128), grouped(l2, 4), grouped(l3, 16), rows(D_MODEL),
                  full(wa), full(wb), full(wo), full(wr), full(br), full(g1), full(b1)],
        out_specs=[rows(D_MODEL), pl.BlockSpec((tm // 8, 32, 128), lambda i: (i, 0, 0)),
                   pl.BlockSpec((8, tm), lambda i: (0, i))],
        out_shape=[jax.ShapeDtypeStruct((n, D_MODEL), F32),
                   jax.ShapeDtypeStruct((n // 8, 32, 128), U32),
                   jax.ShapeDtypeStruct((8, n), F32)],
        scratch_shapes=[pltpu.VMEM((B_WIDTH // 128, tm, 128), F32), pltpu.VMEM((B_WIDTH // 128, tm, 128), F32),
                        pltpu.VMEM((1, tm, 128), F32), pltpu.VMEM((1, tm, 128), F32)],
        compiler_params=pltpu.CompilerParams(dimension_semantics=("parallel",), vmem_limit_bytes=VMEM_LIMIT),
        name="mix",
    )(ga, gates, o1, o2, o3, l1, l2, l3, x2, wa, wb, wo, wr, br, g1, b1)


SC_WINDOW = 128


def _sc_mesh():
    return plsc.VectorSubcoreMesh(core_axis_name="core", subcore_axis_name="subcore")


def _sc_scatter_rows(rows, dst, n_out):
    r = rows.shape[0]

    @pl.kernel(out_type=jax.ShapeDtypeStruct((n_out, 128), rows.dtype), mesh=_sc_mesh())
    def scatter(rows_hbm, dst0_hbm, dst1_hbm, out_hbm):
        def body(rows_vmem, dst0_vmem, dst1_vmem):
            pltpu.sync_copy(rows_vmem, out_hbm.at[dst0_vmem.at[0]])
            pltpu.sync_copy(rows_vmem, out_hbm.at[dst1_vmem.at[0]])

        pltpu.emit_pipeline(
            body,
            grid=(r // SC_WINDOW,),
            in_specs=[pl.BlockSpec((SC_WINDOW, 128), lambda i: (i, 0)),
                      pl.BlockSpec((1, SC_WINDOW), lambda i: (0, i)),
                      pl.BlockSpec((1, SC_WINDOW), lambda i: (0, i))],
            out_specs=[],
            core_axis_name=("core", "subcore"),
            dimension_semantics=(pltpu.PARALLEL,),
        )(rows_hbm, dst0_hbm, dst1_hbm)

    return scatter(rows, dst[0:1], dst[1:2])


def _sc_gather_rows(table, src):
    m = src.shape[0]
    k = 2

    @pl.kernel(out_type=jax.ShapeDtypeStruct((m, 128), table.dtype), mesh=_sc_mesh(),
               scratch_types=[pltpu.SemaphoreType.DMA((k,))])
    def gather(table_hbm, src_hbm, out_hbm, sems):
        def body(src_vmem, out_vmem):
            copies = [pltpu.async_copy(table_hbm.at[src_vmem.at[j]], out_vmem.at[pl.ds(j * SC_WINDOW, SC_WINDOW)],
                                       sems.at[j]) for j in range(k)]
            for c in copies:
                c.wait()

        pltpu.emit_pipeline(
            body,
            grid=(m // (k * SC_WINDOW),),
            in_specs=[pl.BlockSpec((k, SC_WINDOW), lambda i: (i, 0))],
            out_specs=[pl.BlockSpec((k * SC_WINDOW, 128), lambda i: (i, 0))],
            core_axis_name=("core", "subcore"),
            dimension_semantics=(pltpu.PARALLEL,),
        )(src_hbm, out_hbm)

    return gather(table, src.reshape(m // SC_WINDOW, SC_WINDOW))


def _moe_kernel(te_ref, tv_ref, tb_ref, xs_ref, wg_ref, wu_ref, wd_ref, ys_ref, wgb_ref, wub_ref, wdb_ref):
    del tb_ref
    t = pl.program_id(0)

    @pl.when((t == 0) | (te_ref[t] != te_ref[jnp.maximum(t - 1, 0)]))
    def _():
        wgb_ref[...] = wg_ref[...].astype(BF16)
        wub_ref[...] = wu_ref[...].astype(BF16)
        wdb_ref[...] = wd_ref[...].astype(BF16)

    @pl.when(tv_ref[t] == 1)
    def _():
        xb = _load_packed_rows(xs_ref).astype(BF16)
        g = _dot(xb, wgb_ref[...])
        u = _dot(xb, wub_ref[...])
        h = (g * _sigmoid(g) * u).astype(BF16)
        _store_packed_rows(ys_ref, _dot(h, wdb_ref[...]))


def _moe(tile_expert, tile_valid, tile_block, xs, wg, wu, wd):
    tm = TM_MOE
    nt = tile_expert.shape[0]
    rows = pl.BlockSpec((tm // 8, 32, 128), lambda t, te, tv, tb: (tb[t], 0, 0))
    return pl.pallas_call(
        _moe_kernel,
        grid_spec=pltpu.PrefetchScalarGridSpec(
            num_scalar_prefetch=3,
            grid=(nt,),
            in_specs=[rows,
                      pl.BlockSpec((None, D_MODEL, D_EXPERT), lambda t, te, tv, tb: (te[t], 0, 0)),
                      pl.BlockSpec((None, D_MODEL, D_EXPERT), lambda t, te, tv, tb: (te[t], 0, 0)),
                      pl.BlockSpec((None, D_EXPERT, D_MODEL), lambda t, te, tv, tb: (te[t], 0, 0))],
            out_specs=rows,
            scratch_shapes=[pltpu.VMEM((D_MODEL, D_EXPERT), BF16), pltpu.VMEM((D_MODEL, D_EXPERT), BF16),
                            pltpu.VMEM((D_EXPERT, D_MODEL), BF16)]),
        out_shape=jax.ShapeDtypeStruct((nt * tm // 8, 32, 128), U32),
        compiler_params=pltpu.CompilerParams(dimension_semantics=("arbitrary",), vmem_limit_bytes=VMEM_LIMIT),
        name="moe",
    )(tile_expert, tile_valid, tile_block, xs, wg, wu, wd)


def _final_kernel(y0_ref, y1_ref, x1_ref, p_ref, routet_ref, wple_ref, wpg_ref, g2_ref, b2_ref, *rest):
    out_ref = rest[-1]
    tm = x1_ref.shape[0]
    route = routet_ref[...].T
    h = tm // 2
    halves = (slice(0, h), slice(h, tm))
    x1s = [x1_ref[r, :] for r in halves]
    gates = [_dot(x1.astype(BF16), wpg_ref[...]) for x1 in x1s]
    plins = [_dot(p_ref[r, :].astype(BF16), wple_ref[...]) for r in halves]
    for i, r in enumerate(halves):
        g8 = slice(i * h // 8, (i + 1) * h // 8)
        ffn = (route[r, 2:3] * _load_packed_rows(y0_ref.at[g8])
               + route[r, 3:4] * _load_packed_rows(y1_ref.at[g8]))
        ple = plins[i] * _sigmoid(gates[i])
        out_ref[r, :] = _ln(DEEPNORM_ALPHA * x1s[i] + ffn + ple, g2_ref[...], b2_ref[...])


def _final(yg, x1, p2, routet, wple, wpg, g2, b2, half, prev):
    n = x1.shape[0]
    tm = TM_FIN
    nt = n // 2 // tm
    off = half * nt
    rows = lambda w: pl.BlockSpec((tm, w), lambda t: (t + off, 0))
    full = lambda a: pl.BlockSpec(a.shape, lambda t: (0,) * a.ndim)
    in_specs = [pl.BlockSpec((tm // 8, 32, 128), lambda t: (t, 0, 0)),
                pl.BlockSpec((tm // 8, 32, 128), lambda t: (t + nt, 0, 0)),
                rows(D_MODEL), rows(PLE_DIM), pl.BlockSpec((8, tm), lambda t: (0, t + off)),
                full(wple), full(wpg), full(g2), full(b2)]
    args = [yg, yg, x1, p2, routet, wple, wpg, g2, b2]
    aliases = {}
    if prev is not None:
        in_specs.append(pl.BlockSpec(memory_space=pl.ANY))
        args.append(prev)
        aliases = {len(args) - 1: 0}
    return pl.pallas_call(
        _final_kernel,
        grid=(nt,),
        in_specs=in_specs,
        out_specs=rows(D_MODEL),
        out_shape=jax.ShapeDtypeStruct((n, D_MODEL), F32),
        input_output_aliases=aliases,
        compiler_params=pltpu.CompilerParams(dimension_semantics=("parallel",), vmem_limit_bytes=VMEM_LIMIT),
        name=f"final{half}",
    )(*args)


def _route_tables_kernel(e_ref, piece_ref, ends_ref):
    r = e_ref.shape[0]
    e = e_ref[...]
    ri = lax.broadcasted_iota(I32, (128, 128), 0)
    ci = lax.broadcasted_iota(I32, (128, 128), 1)
    upper = jnp.where(ri <= ci, 1.0, 0.0).astype(BF16)
    rr = lax.broadcasted_iota(I32, (r, r), 0)
    rc = lax.broadcasted_iota(I32, (r, r), 1)
    below = jnp.where(rc < rr, 1.0, 0.0).astype(BF16)
    lane = lax.broadcasted_iota(I32, (1, 128), 1)

    rank = jnp.zeros((r, 128), F32)
    counts = jnp.zeros((1, 128), F32)
    for x in range(N_EXPERTS_TOTAL):
        m = jnp.where(e == x, 1.0, 0.0)
        pre = _dot(m.astype(BF16), upper)
        tot = jnp.broadcast_to(pre[:, 127:128], (r, 128))
        off = _dot(below, tot.astype(BF16))
        rank = rank + m * (pre + off)
        counts = jnp.where(lane == x, off[r - 1:r, :] + tot[r - 1:r, :], counts)
    padded = jnp.floor((counts + (TM_MOE - 1)) * (1.0 / TM_MOE)) * TM_MOE
    ends = _dot(jnp.broadcast_to(padded, (8, 128)).astype(BF16), upper)[0:1, :]
    offs = ends - padded
    ends_ref[...] = jnp.broadcast_to(ends, (8, 128)).astype(I32)

    pos = rank - 1.0
    for x in range(N_EXPERTS_TOTAL):
        pos = pos + jnp.where(e == x, offs[:, x:x + 1], 0.0)

    hi = jnp.floor(pos * (1.0 / 256.0))
    lo = pos - 256.0 * hi
    jv = ((lane % 32) // 8).astype(F32)
    for c in range(4):
        sel = jnp.where(ri == 32 * c + 8 * (ci // 32) + ci % 8, 1.0, 0.0).astype(BF16)
        pc = 256.0 * _dot(hi.astype(BF16), sel) + _dot(lo.astype(BF16), sel)
        p8 = jnp.floor(pc * 0.125)
        piece = p8 * (8.0 * SUBROWS) + (pc - 8.0 * p8) + 8.0 * jv
        piece_ref[pl.ds(c, r, stride=4), :] = piece.astype(I32)


def _routing_tables(routet, n):
    tm = TM_MOE
    nt = (2 * n) // tm + N_EXPERTS_TOTAL
    r = 2 * n // 128
    piece, ends = pl.pallas_call(
        _route_tables_kernel,
        out_shape=[jax.ShapeDtypeStruct((4 * r, 128), I32), jax.ShapeDtypeStruct((8, 128), I32)],
        compiler_params=pltpu.CompilerParams(vmem_limit_bytes=VMEM_LIMIT),
        name="route_tables",
    )(routet[0:2].reshape(r, 128))
    ends = ends[0, :N_EXPERTS_TOTAL]
    tile_start = jnp.arange(nt, dtype=I32) * tm
    tile_expert = jnp.minimum(jnp.sum((tile_start[:, None] >= ends[None, :]).astype(I32), axis=1),
                              N_EXPERTS_TOTAL - 1).astype(I32)
    tile_valid = (tile_start < ends[-1]).astype(I32)
    tile_block = jnp.minimum(jnp.arange(nt, dtype=I32), ends[-1] // tm - 1)
    return tile_expert, tile_valid, tile_block, piece.reshape(2, n * SUBROWS)


def kernel(x, p, w_in, a_ln_g, a_ln_b, a_ws, a_bs, w_a_proj, w_b_proj, w_o, ln1_g, ln1_b, w_group_router,
           b_group_router, w_expert_router, b_expert_router, w_gate, w_up, w_down, w_ple, w_ple_gate,
           ln2_g, ln2_b):
    bsz, s, d = x.shape
    n = bsz * s
    assert d == D_MODEL and s % (SPAN * max(B_DILATIONS)) == 0 and n % TM_PROJ == 0
    assert w_in.shape[0] == 1, "one layer"

    w_in_b = w_in[0].astype(BF16)
    a_bias = jnp.repeat(a_bs[0].T, A_WIDTH // 8, axis=1)

    ga, gates, qkv1, qkv2, qkv3 = _proj(x, w_in_b, a_ln_g, a_ln_b, a_ws[0], a_bias)
    o1, l1 = _attn(qkv1.reshape(bsz, 1, s, 3 * COL), 0)
    o2, l2 = _attn(qkv2, 1)
    o3, l3 = _attn(qkv3, 2)

    pad = 128 - N_GROUPS - N_EXPERTS_TOTAL
    wr = jnp.concatenate([w_group_router[0], w_expert_router[0].reshape(d, N_EXPERTS_TOTAL),
                          jnp.zeros((d, pad), F32)], axis=1).astype(BF16)
    br = jnp.concatenate([b_group_router[0], b_expert_router[0].reshape(-1), jnp.zeros((pad,), F32)])[None, :]
    x1, x1p, routet = _mix(
        ga, gates, o1.reshape(n, B_WIDTH), o2, o3, l1.reshape(n, 128), l2, l3, x.reshape(n, d),
        w_a_proj[0].astype(BF16), w_b_proj[0].astype(BF16), w_o[0].astype(BF16), wr, br, ln1_g, ln1_b)

    tile_expert, tile_valid, tile_block, piece = _routing_tables(routet, n)
    nt = tile_expert.shape[0]
    xs = _sc_scatter_rows(x1p.reshape(n * SUBROWS, 128), piece, nt * TM_MOE * SUBROWS)
    ys = _moe(tile_expert, tile_valid, tile_block, xs.reshape(nt * TM_MOE // 8, 32, 128),
              w_gate[0].reshape(N_EXPERTS_TOTAL, d, D_EXPERT), w_up[0].reshape(N_EXPERTS_TOTAL, d, D_EXPERT),
              w_down[0].reshape(N_EXPERTS_TOTAL, D_EXPERT, d)).reshape(nt * TM_MOE * SUBROWS, 128)
    out = None
    hp = n * SUBROWS // 2
    for half in range(2):
        yg = _sc_gather_rows(ys, piece[:, half * hp:(half + 1) * hp].reshape(-1))
        out = _final(yg.reshape(n // 8, 32, 128), x1, p[0].reshape(n, PLE_DIM), routet,
                     w_ple[0].astype(BF16), w_ple_gate[0].astype(BF16), ln2_g, ln2_b, half, out)
    return out.reshape(bsz, s, d)
```

```python
import functools

import jax
import jax.numpy as jnp
from jax import lax
from jax.experimental import pallas as pl
from jax.experimental.pallas import tpu as pltpu
from jax.experimental.pallas import tpu_sc as plsc

F32 = jnp.float32
BF16 = jnp.bfloat16
U32 = jnp.uint32
I32 = jnp.int32

D_MODEL = 1024
PLE_DIM = 256
A_WIDTH = 512
A_CHUNK = 128
B_HEAD_DIM = 64
B_HEADS = 8
B_WIDTH = 512
B_DILATIONS = (1, 4, 16)
SPAN = 128
N_GROUPS = 4
N_EXPERTS = 8
N_EXPERTS_TOTAL = N_GROUPS * N_EXPERTS
D_EXPERT = 256
DEEPNORM_ALPHA = 2.0 ** 0.25
LN_EPS = 1e-5
COL = 512
NEG = -1e30

VMEM_LIMIT = 56 * 1024 * 1024

TM_PROJ = 512
TM_ATTN = 1024
TM_MIX = 512
TM_MOE = 512
TM_FIN = 1024


def _ln(x, g, b):
    mu = jnp.mean(x, axis=-1, keepdims=True)
    xc = x - mu
    var = jnp.mean(xc * xc, axis=-1, keepdims=True)
    return xc * lax.rsqrt(var + LN_EPS) * g + b


def _gelu_tanh(x):
    return 0.5 * x * (1.0 + jnp.tanh(0.7978845608028654 * (x + 0.044715 * (x * x * x))))


def _sigmoid(x):
    return 0.5 * jnp.tanh(0.5 * x) + 0.5


def _dot(a, b):
    return jnp.dot(a, b, preferred_element_type=F32)


PACK_W = D_MODEL // 2
SUBROWS = PACK_W // 128


def _store_packed_rows(ref, x):
    m = x.shape[0]
    xb = x.astype(BF16).astype(F32)
    lo = pltpu.bitcast(xb[:, :PACK_W], U32) >> 16
    hi = pltpu.bitcast(xb[:, PACK_W:], U32) & jnp.uint32(0xFFFF0000)
    w = hi | lo
    for j in range(SUBROWS):
        ref[:, 8 * j:8 * (j + 1), :] = w[:, 128 * j:128 * (j + 1)].reshape(m // 8, 8, 128)


def _load_packed_rows(ref):
    m = ref.shape[0] * 8
    ws = [ref[:, 8 * j:8 * (j + 1), :].reshape(m, 128) for j in range(SUBROWS)]
    lo = [pltpu.bitcast(w << 16, F32) for w in ws]
    hi = [pltpu.bitcast(w & jnp.uint32(0xFFFF0000), F32) for w in ws]
    return jnp.concatenate(lo + hi, axis=1)


def _proj_kernel(x_ref, *refs):
    w = refs[:15]
    lng_ref, lnb_ref, ws_ref, bias_ref = refs[15:19]
    ga_ref, gates_ref, qkv1_ref, qkv2_ref, qkv3_ref = refs[19:24]
    xc_ref = refs[24]
    tm = x_ref.shape[0]
    xb = x_ref[...].astype(BF16)

    u_raw = _dot(xb, w[0][...])
    v_raw = _dot(xb, w[1][...])

    for i in range(4):
        gates_ref[:, i * COL:(i + 1) * COL] = _sigmoid(_dot(xb, w[11 + i][...])).astype(BF16)
    for j in range(3):
        qkv1_ref[:, j * COL:(j + 1) * COL] = _dot(xb, w[2 + 3 * j][...]).astype(BF16)

    for c in range(D_MODEL // 128):
        xc_ref[c] = x_ref[:, c * 128:(c + 1) * 128]
    for gi, out_ref in ((1, qkv2_ref), (2, qkv3_ref)):
        dl = B_DILATIONS[gi]
        per = tm // dl
        xp = jnp.concatenate(
            [jnp.concatenate([xc_ref[c, pl.ds(r, per, stride=dl), :] for c in range(D_MODEL // 128)], axis=1)
             for r in range(dl)], axis=0).astype(BF16)
        for j in range(3):
            res = _dot(xp, w[2 + 3 * j + gi][...]).astype(BF16)
            for r in range(dl):
                out_ref[r, :, j * COL:(j + 1) * COL] = res[r * per:(r + 1) * per]

    u = _gelu_tanh(u_raw)
    v = _gelu_tanh(v_raw)
    vn = _ln(v, lng_ref[...], lnb_ref[...]).astype(BF16)

    row = lax.broadcasted_iota(I32, (A_CHUNK, A_CHUNK), 0)
    colm = lax.broadcasted_iota(I32, (A_CHUNK, A_CHUNK), 1)
    causal = colm <= row
    lo = colm < 64
    zero = jnp.zeros((A_CHUNK, A_CHUNK), BF16)
    wcat = []
    for j in range(4):
        w0 = jnp.where(causal, ws_ref[2 * j], 0.0).astype(BF16)
        w1 = jnp.where(causal, ws_ref[2 * j + 1], 0.0).astype(BF16)
        wcat.append(jnp.concatenate([w0, w1], axis=1))
    for c in range(tm // A_CHUNK):
        r0 = c * A_CHUNK
        for j in range(4):
            c0 = j * 128
            vt = vn[r0:r0 + A_CHUNK, c0:c0 + 128]
            rhs = jnp.concatenate([jnp.where(lo, vt, zero), jnp.where(lo, zero, vt)], axis=0)
            mixed = _dot(wcat[j], rhs) + bias_ref[:, c0:c0 + 128]
            ga_ref[r0:r0 + A_CHUNK, c0:c0 + 128] = (u[r0:r0 + A_CHUNK, c0:c0 + 128] * mixed).astype(BF16)


def _proj(x, w_in_b, a_ln_g, a_ln_b, a_ws, a_bias):
    bsz, s, _ = x.shape
    n = bsz * s
    tm = TM_PROJ
    tiles = s // tm
    x2 = x.reshape(n, D_MODEL)
    wspec = lambda j: pl.BlockSpec((D_MODEL, COL), lambda i, j=j: (0, j), pipeline_mode=pl.Buffered(1))
    full = lambda shape: pl.BlockSpec(shape, lambda i: (0,) * len(shape))
    rows = lambda width: pl.BlockSpec((tm, width), lambda i: (i, 0))
    dil = lambda dl: pl.BlockSpec((None, dl, tm // dl, 3 * COL), lambda i: (i // tiles, 0, i % tiles, 0))
    return pl.pallas_call(
        _proj_kernel,
        grid=(n // tm,),
        in_specs=[rows(D_MODEL)] + [wspec(j) for j in range(15)]
                 + [full((1, A_WIDTH)), full((1, A_WIDTH)), full((8, A_CHUNK, A_CHUNK)), full((A_CHUNK, A_WIDTH))],
        out_specs=[rows(A_WIDTH), rows(4 * COL), rows(3 * COL), dil(4), dil(16)],
        out_shape=[jax.ShapeDtypeStruct((n, A_WIDTH), BF16),
                   jax.ShapeDtypeStruct((n, 4 * COL), BF16),
                   jax.ShapeDtypeStruct((n, 3 * COL), BF16),
                   jax.ShapeDtypeStruct((bsz, 4, s // 4, 3 * COL), BF16),
                   jax.ShapeDtypeStruct((bsz, 16, s // 16, 3 * COL), BF16)],
        scratch_shapes=[pltpu.VMEM((D_MODEL // 128, tm, 128), F32)],
        compiler_params=pltpu.CompilerParams(dimension_semantics=("parallel",), vmem_limit_bytes=VMEM_LIMIT),
        name="proj",
    )(x2, *([w_in_b] * 15), a_ln_g, a_ln_b, a_ws, a_bias)


def _attn_kernel(qkv_ref, o_ref, lse_ref, *, ns, seq):
    nb = seq // SPAN
    lane = lax.broadcasted_iota(I32, (SPAN, 128), 1)
    lo = lane < 64
    lane16 = lane // 16
    qi = lax.broadcasted_iota(I32, (SPAN, 2 * SPAN), 0)
    ki = lax.broadcasted_iota(I32, (SPAN, 2 * SPAN), 1)
    causal = lax.broadcasted_iota(I32, (SPAN, SPAN), 1) <= lax.broadcasted_iota(I32, (SPAN, SPAN), 0)
    bias_first = jnp.where(causal, 0.0, NEG).astype(F32)
    bias_first = jnp.concatenate([bias_first, bias_first], axis=0)
    bias_main = jnp.where((ki >= qi) & (ki <= qi + SPAN), 0.0, NEG).astype(F32)
    bias_main = jnp.concatenate([bias_main, bias_main], axis=0)
    zero = jnp.zeros((SPAN, 128), BF16)

    for s in range(ns):
        def block(row0, start, bias, s=s):
            win = bias.shape[1]
            pairs = range(B_HEADS // 2)
            scores, values = [], []
            for jp in pairs:
                c0 = jp * 128
                q = qkv_ref[s, pl.ds(row0, SPAN), c0:c0 + 128] * jnp.asarray(0.125, BF16)
                k = qkv_ref[s, pl.ds(start, win), COL + c0:COL + c0 + 128]
                values.append(qkv_ref[s, pl.ds(start, win), 2 * COL + c0:2 * COL + c0 + 128])
                qs = jnp.concatenate([jnp.where(lo, q, zero), jnp.where(lo, zero, q)], axis=0)
                scores.append(lax.dot_general(qs, k, (((1,), (1,)), ((), ())), preferred_element_type=F32) + bias)
            probs, maxes, sums = [], [], []
            for jp in pairs:
                m = jnp.max(scores[jp], axis=-1, keepdims=True)
                p = jnp.exp(scores[jp] - m)
                maxes.append(m)
                sums.append(jnp.sum(p, axis=-1, keepdims=True))
                probs.append(p.astype(BF16))
            lse_tile = jnp.zeros((SPAN, 128), F32)
            for jp in pairs:
                c0 = jp * 128
                ov = _dot(probs[jp], values[jp])
                inv = 1.0 / sums[jp]
                o = jnp.where(lo, ov[:SPAN] * inv[:SPAN], ov[SPAN:] * inv[SPAN:])
                o_ref[pl.ds(row0, SPAN), s * B_WIDTH + c0:s * B_WIDTH + c0 + 128] = o.astype(BF16)
                lse = maxes[jp] + jnp.log(sums[jp])
                lse_tile = jnp.where(lane16 == 2 * jp, lse[:SPAN],
                                     jnp.where(lane16 == 2 * jp + 1, lse[SPAN:], lse_tile))
            lse_ref[pl.ds(row0, SPAN), s * 128:(s + 1) * 128] = lse_tile

        block(0, 0, bias_first)
        if nb > 1:
            def body(i, carry):
                block(pl.multiple_of(i * SPAN, SPAN), pl.multiple_of((i - 1) * SPAN, SPAN), bias_main)
                return carry
            lax.fori_loop(1, nb, body, 0, unroll=min(5, nb - 1))


def _attn(qkv_g):
    bsz, dl, seq, _ = qkv_g.shape
    ns = max(1, min(dl, TM_ATTN // seq))
    return pl.pallas_call(
        functools.partial(_attn_kernel, ns=ns, seq=seq),
        grid=(bsz, dl // ns),
        in_specs=[pl.BlockSpec((None, ns, seq, 3 * COL), lambda b, r: (b, r, 0, 0))],
        out_specs=[pl.BlockSpec((None, seq, ns * B_WIDTH), lambda b, r: (b, 0, r)),
                   pl.BlockSpec((None, seq, ns * 128), lambda b, r: (b, 0, r))],
        out_shape=[jax.ShapeDtypeStruct((bsz, seq, dl * B_WIDTH), BF16),
                   jax.ShapeDtypeStruct((bsz, seq, dl * 128), F32)],
        compiler_params=pltpu.CompilerParams(dimension_semantics=("parallel", "parallel"),
                                             vmem_limit_bytes=VMEM_LIMIT),
        name=f"attn{dl}",
    )(qkv_g)


def _natural_rows(ref, dl, scr):
    nchunk, tm, _ = scr.shape
    w = nchunk * 128
    per = tm // dl
    for r in range(dl):
        for c in range(nchunk):
            scr[c, pl.ds(r, per, stride=dl), :] = ref[:, r * w + c * 128:r * w + (c + 1) * 128].astype(F32)
    return jnp.concatenate([scr[c] for c in range(nchunk)], axis=1)


def _mix_kernel(ga_ref, gates_ref, o1_ref, o2_ref, o3_ref, l1_ref, l2_ref, l3_ref, x_ref,
                wa_ref, wb_ref, wo_ref, wr_ref, br_ref, g1_ref, b1_ref,
                x1_ref, x1p_ref, routet_ref, o2s_ref, o3s_ref, l2s_ref, l3s_ref):
    tm = x_ref.shape[0]
    o2 = _natural_rows(o2_ref, 4, o2s_ref)
    o3 = _natural_rows(o3_ref, 16, o3s_ref)
    l2 = _natural_rows(l2_ref, 4, l2s_ref)
    l3 = _natural_rows(l3_ref, 16, l3s_ref)
    er = lax.broadcasted_iota(I32, (256, B_WIDTH), 0)
    ec = lax.broadcasted_iota(I32, (256, B_WIDTH), 1)
    expand = jnp.where(er % 128 == (ec // B_HEAD_DIM) * 16, 1.0, 0.0).astype(BF16)

    def widen(w):
        hi = w.astype(BF16)
        lo = (w - hi.astype(F32)).astype(BF16)
        return _dot(jnp.concatenate([hi, lo], axis=1), expand)

    h = tm // 2
    halves = (slice(0, h), slice(h, tm))
    obs = []
    for r in halves:
        l1 = l1_ref[r, :]
        mx = jnp.maximum(l1, jnp.maximum(l2[r], l3[r]))
        e1, e2, e3 = jnp.exp(l1 - mx), jnp.exp(l2[r] - mx), jnp.exp(l3[r] - mx)
        inv = 1.0 / (e1 + e2 + e3)
        obs.append(widen(e1 * inv) * o1_ref[r, :].astype(F32) + widen(e2 * inv) * o2[r] + widen(e3 * inv) * o3[r])
    ybs = [_dot(ob.astype(BF16), wb_ref[...]) for ob in obs]
    yas = [_dot(ga_ref[r, :], wa_ref[...]) for r in halves]
    pres = [gates_ref[r, :D_MODEL].astype(F32) * ya + gates_ref[r, D_MODEL:].astype(F32) * yb
            for r, ya, yb in zip(halves, yas, ybs)]
    mixes = [_dot(pre.astype(BF16), wo_ref[...]) for pre in pres]
    x1s = [_ln(DEEPNORM_ALPHA * x_ref[r, :] + mix, g1_ref[...], b1_ref[...]) for r, mix in zip(halves, mixes)]
    logits = [_dot(x1.astype(BF16), wr_ref[...]) + br_ref[...] for x1 in x1s]

    nrow = 40
    row = lax.broadcasted_iota(I32, (nrow, h), 0).astype(F32)
    row8 = lax.broadcasted_iota(I32, (8, h), 0)
    big = 1e9
    for i, r in enumerate(halves):
        x1_ref[r, :] = x1s[i]
        _store_packed_rows(x1p_ref.at[i * h // 8:(i + 1) * h // 8], x1s[i])
        lg = logits[i].T[:nrow, :]
        gl = jnp.where(row < N_GROUPS, lg, NEG)
        gm = jnp.max(gl, axis=0, keepdims=True)
        gidx = jnp.min(jnp.where(gl == gm, row, big), axis=0, keepdims=True)
        gsum = jnp.sum(jnp.where(row < N_GROUPS, jnp.exp(gl - gm), 0.0), axis=0, keepdims=True)
        gprob = 1.0 / gsum
        lo_row = N_GROUPS + N_EXPERTS * gidx
        el = jnp.where((row >= lo_row) & (row < lo_row + N_EXPERTS), lg, NEG)
        v1 = jnp.max(el, axis=0, keepdims=True)
        i1 = jnp.min(jnp.where(el == v1, row, big), axis=0, keepdims=True)
        el2 = jnp.where(row == i1, NEG, el)
        v2 = jnp.max(el2, axis=0, keepdims=True)
        i2 = jnp.min(jnp.where(el2 == v2, row, big), axis=0, keepdims=True)
        t = jnp.exp(v2 - v1)
        w1 = 1.0 / (1.0 + t)
        w2 = t * w1
        routet_ref[:, r] = jnp.where(row8 == 0, i1 - N_GROUPS,
                                     jnp.where(row8 == 1, i2 - N_GROUPS,
                                               jnp.where(row8 == 2, gprob * w1,
                                                         jnp.where(row8 == 3, gprob * w2, 0.0))))


def _mix(ga, gates, o1, o2, o3, l1, l2, l3, x2, wa, wb, wo, wr, br, g1, b1):
    n = x2.shape[0]
    bsz = o2.shape[0]
    tm = TM_MIX
    tiles = n // bsz // tm
    rows = lambda w: pl.BlockSpec((tm, w), lambda i: (i, 0))
    grouped = lambda a, dl: pl.BlockSpec((None, tm // dl, a.shape[2]), lambda i: (i // tiles, i % tiles, 0))
    full = lambda a: pl.BlockSpec(a.shape, lambda i: (0,) * a.ndim)
    return pl.pallas_call(
        _mix_kernel,
        grid=(n // tm,),
        in_specs=[rows(A_WIDTH), rows(2 * D_MODEL), rows(B_WIDTH), grouped(o2, 4), grouped(o3, 16),
                  rows(128), grouped(l2, 4), grouped(l3, 16), rows(D_MODEL),
                  full(wa), full(wb), full(wo), full(wr), full(br), full(g1), full(b1)],
        out_specs=[rows(D_MODEL), pl.BlockSpec((tm // 8, 32, 128), lambda i: (i, 0, 0)),
                   pl.BlockSpec((8, tm), lambda i: (0, i))],
        out_shape=[jax.ShapeDtypeStruct((n, D_MODEL), F32),
                   jax.ShapeDtypeStruct((n // 8, 32, 128), U32),
                   jax.ShapeDtypeStruct((8, n), F32)],
        scratch_shapes=[pltpu.VMEM((B_WIDTH // 128, tm, 128), F32), pltpu.VMEM((B_WIDTH // 128, tm, 128), F32),
                        pltpu.VMEM((1, tm, 128), F32), pltpu.VMEM((1, tm, 128), F32)],
        compiler_params=pltpu.CompilerParams(dimension_semantics=("parallel",), vmem_limit_bytes=VMEM_LIMIT),
        name="mix",
    )(ga, gates, o1, o2, o3, l1, l2, l3, x2, wa, wb, wo, wr, br, g1, b1)


SC_WINDOW = 128


def _sc_mesh():
    return plsc.VectorSubcoreMesh(core_axis_name="core", subcore_axis_name="subcore")


def _sc_scatter_rows(rows, dst, n_out):
    r = rows.shape[0]

    @pl.kernel(out_type=jax.ShapeDtypeStruct((n_out, 128), rows.dtype), mesh=_sc_mesh())
    def scatter(rows_hbm, dst0_hbm, dst1_hbm, out_hbm):
        def body(rows_vmem, dst0_vmem, dst1_vmem):
            pltpu.sync_copy(rows_vmem, out_hbm.at[dst0_vmem.at[0]])
            pltpu.sync_copy(rows_vmem, out_hbm.at[dst1_vmem.at[0]])

        pltpu.emit_pipeline(
            body,
            grid=(r // SC_WINDOW,),
            in_specs=[pl.BlockSpec((SC_WINDOW, 128), lambda i: (i, 0)),
                      pl.BlockSpec((1, SC_WINDOW), lambda i: (0, i)),
                      pl.BlockSpec((1, SC_WINDOW), lambda i: (0, i))],
            out_specs=[],
            core_axis_name=("core", "subcore"),
            dimension_semantics=(pltpu.PARALLEL,),
        )(rows_hbm, dst0_hbm, dst1_hbm)

    return scatter(rows, dst[0:1], dst[1:2])


def _sc_gather_rows(table, src):
    m = src.shape[0]
    k = 2

    @pl.kernel(out_type=jax.ShapeDtypeStruct((m, 128), table.dtype), mesh=_sc_mesh(),
               scratch_types=[pltpu.SemaphoreType.DMA((k,))])
    def gather(table_hbm, src_hbm, out_hbm, sems):
        def body(src_vmem, out_vmem):
            copies = [pltpu.async_copy(table_hbm.at[src_vmem.at[j]], out_vmem.at[pl.ds(j * SC_WINDOW, SC_WINDOW)],
                                       sems.at[j]) for j in range(k)]
            for c in copies:
                c.wait()

        pltpu.emit_pipeline(
            body,
            grid=(m // (k * SC_WINDOW),),
            in_specs=[pl.BlockSpec((k, SC_WINDOW), lambda i: (i, 0))],
            out_specs=[pl.BlockSpec((k * SC_WINDOW, 128), lambda i: (i, 0))],
            core_axis_name=("core", "subcore"),
            dimension_semantics=(pltpu.PARALLEL,),
        )(src_hbm, out_hbm)

    return gather(table, src.reshape(m // SC_WINDOW, SC_WINDOW))


def _moe_kernel(te_ref, tv_ref, tb_ref, xs_ref, wg_ref, wu_ref, wd_ref, ys_ref, wgb_ref, wub_ref, wdb_ref):
    del tb_ref
    t = pl.program_id(0)

    @pl.when((t == 0) | (te_ref[t] != te_ref[jnp.maximum(t - 1, 0)]))
    def _():
        wgb_ref[...] = wg_ref[...].astype(BF16)
        wub_ref[...] = wu_ref[...].astype(BF16)
        wdb_ref[...] = wd_ref[...].astype(BF16)

    @pl.when(tv_ref[t] == 1)
    def _():
        xb = _load_packed_rows(xs_ref).astype(BF16)
        g = _dot(xb, wgb_ref[...])
        u = _dot(xb, wub_ref[...])
        h = (g * _sigmoid(g) * u).astype(BF16)
        _store_packed_rows(ys_ref, _dot(h, wdb_ref[...]))


def _moe(tile_expert, tile_valid, tile_block, xs, wg, wu, wd):
    tm = TM_MOE
    nt = tile_expert.shape[0]
    rows = pl.BlockSpec((tm // 8, 32, 128), lambda t, te, tv, tb: (tb[t], 0, 0))
    return pl.pallas_call(
        _moe_kernel,
        grid_spec=pltpu.PrefetchScalarGridSpec(
            num_scalar_prefetch=3,
            grid=(nt,),
            in_specs=[rows,
                      pl.BlockSpec((None, D_MODEL, D_EXPERT), lambda t, te, tv, tb: (te[t], 0, 0)),
                      pl.BlockSpec((None, D_MODEL, D_EXPERT), lambda t, te, tv, tb: (te[t], 0, 0)),
                      pl.BlockSpec((None, D_EXPERT, D_MODEL), lambda t, te, tv, tb: (te[t], 0, 0))],
            out_specs=rows,
            scratch_shapes=[pltpu.VMEM((D_MODEL, D_EXPERT), BF16), pltpu.VMEM((D_MODEL, D_EXPERT), BF16),
                            pltpu.VMEM((D_EXPERT, D_MODEL), BF16)]),
        out_shape=jax.ShapeDtypeStruct((nt * tm // 8, 32, 128), U32),
        compiler_params=pltpu.CompilerParams(dimension_semantics=("arbitrary",), vmem_limit_bytes=VMEM_LIMIT),
        name="moe",
    )(tile_expert, tile_valid, tile_block, xs, wg, wu, wd)


def _final_kernel(y0_ref, y1_ref, x1_ref, p_ref, routet_ref, wple_ref, wpg_ref, g2_ref, b2_ref, *rest):
    out_ref = rest[-1]
    tm = x1_ref.shape[0]
    route = routet_ref[...].T
    h = tm // 4
    parts = tuple(slice(i * h, (i + 1) * h) for i in range(4))
    x1s = [x1_ref[r, :] for r in parts]
    plins = [_dot(p_ref[r, :].astype(BF16), wple_ref[...]) for r in parts]
    gates = [_dot(x1.astype(BF16), wpg_ref[...]) for x1 in x1s]
    sums = []
    for i, r in enumerate(parts):
        g8 = slice(i * h // 8, (i + 1) * h // 8)
        sums.append(DEEPNORM_ALPHA * x1s[i] + route[r, 2:3] * _load_packed_rows(y0_ref.at[g8])
                    + route[r, 3:4] * _load_packed_rows(y1_ref.at[g8]))
    for i, r in enumerate(parts):
        ple = plins[i] * _sigmoid(gates[i])
        out_ref[r, :] = _ln(sums[i] + ple, g2_ref[...], b2_ref[...])


def _final(yg, x1, p2, routet, wple, wpg, g2, b2, half, prev):
    n = x1.shape[0]
    tm = TM_FIN
    nt = n // 2 // tm
    off = half * nt
    rows = lambda w: pl.BlockSpec((tm, w), lambda t: (t + off, 0))
    full = lambda a: pl.BlockSpec(a.shape, lambda t: (0,) * a.ndim)
    in_specs = [pl.BlockSpec((tm // 8, 32, 128), lambda t: (t, 0, 0)),
                pl.BlockSpec((tm // 8, 32, 128), lambda t: (t + nt, 0, 0)),
                rows(D_MODEL), rows(PLE_DIM), pl.BlockSpec((8, tm), lambda t: (0, t + off)),
                full(wple), full(wpg), full(g2), full(b2)]
    args = [yg, yg, x1, p2, routet, wple, wpg, g2, b2]
    aliases = {}
    if prev is not None:
        in_specs.append(pl.BlockSpec(memory_space=pl.ANY))
        args.append(prev)
        aliases = {len(args) - 1: 0}
    return pl.pallas_call(
        _final_kernel,
        grid=(nt,),
        in_specs=in_specs,
        out_specs=rows(D_MODEL),
        out_shape=jax.ShapeDtypeStruct((n, D_MODEL), F32),
        input_output_aliases=aliases,
        compiler_params=pltpu.CompilerParams(dimension_semantics=("parallel",), vmem_limit_bytes=VMEM_LIMIT),
        name=f"final{half}",
    )(*args)


def _route_tables_kernel(e_ref, piece_ref, ends_ref):
    r = e_ref.shape[0]
    e = e_ref[...]
    ri = lax.broadcasted_iota(I32, (128, 128), 0)
    ci = lax.broadcasted_iota(I32, (128, 128), 1)
    upper = jnp.where(ri <= ci, 1.0, 0.0).astype(BF16)
    rr = lax.broadcasted_iota(I32, (r, r), 0)
    rc = lax.broadcasted_iota(I32, (r, r), 1)
    below = jnp.where(rc < rr, 1.0, 0.0).astype(BF16)
    lane = lax.broadcasted_iota(I32, (1, 128), 1)

    rank = jnp.zeros((r, 128), F32)
    counts = jnp.zeros((1, 128), F32)
    for x in range(N_EXPERTS_TOTAL):
        m = jnp.where(e == x, 1.0, 0.0)
        pre = _dot(m.astype(BF16), upper)
        tot = jnp.broadcast_to(pre[:, 127:128], (r, 128))
        off = _dot(below, tot.astype(BF16))
        rank = rank + m * (pre + off)
        counts = jnp.where(lane == x, off[r - 1:r, :] + tot[r - 1:r, :], counts)
    padded = jnp.floor((counts + (TM_MOE - 1)) * (1.0 / TM_MOE)) * TM_MOE
    ends = _dot(jnp.broadcast_to(padded, (8, 128)).astype(BF16), upper)[0:1, :]
    offs = ends - padded
    ends_ref[...] = jnp.broadcast_to(ends, (8, 128)).astype(I32)

    pos = rank - 1.0
    for x in range(N_EXPERTS_TOTAL):
        pos = pos + jnp.where(e == x, offs[:, x:x + 1], 0.0)

    hi = jnp.floor(pos * (1.0 / 256.0))
    lo = pos - 256.0 * hi
    jv = ((lane % 32) // 8).astype(F32)
    for c in range(4):
        sel = jnp.where(ri == 32 * c + 8 * (ci // 32) + ci % 8, 1.0, 0.0).astype(BF16)
        pc = 256.0 * _dot(hi.astype(BF16), sel) + _dot(lo.astype(BF16), sel)
        p8 = jnp.floor(pc * 0.125)
        piece = p8 * (8.0 * SUBROWS) + (pc - 8.0 * p8) + 8.0 * jv
        piece_ref[pl.ds(c, r, stride=4), :] = piece.astype(I32)


def _routing_tables(routet, n):
    tm = TM_MOE
    nt = (2 * n) // tm + N_EXPERTS_TOTAL
    r = 2 * n // 128
    piece, ends = pl.pallas_call(
        _route_tables_kernel,
        out_shape=[jax.ShapeDtypeStruct((4 * r, 128), I32), jax.ShapeDtypeStruct((8, 128), I32)],
        compiler_params=pltpu.CompilerParams(vmem_limit_bytes=VMEM_LIMIT),
        name="route_tables",
    )(routet[0:2].reshape(r, 128))
    ends = ends[0, :N_EXPERTS_TOTAL]
    tile_start = jnp.arange(nt, dtype=I32) * tm
    tile_expert = jnp.minimum(jnp.sum((tile_start[:, None] >= ends[None, :]).astype(I32), axis=1),
                              N_EXPERTS_TOTAL - 1).astype(I32)
    tile_valid = (tile_start < ends[-1]).astype(I32)
    tile_block = jnp.minimum(jnp.arange(nt, dtype=I32), ends[-1] // tm - 1)
    return tile_expert, tile_valid, tile_block, piece.reshape(2, n * SUBROWS)


def kernel(x, p, w_in, a_ln_g, a_ln_b, a_ws, a_bs, w_a_proj, w_b_proj, w_o, ln1_g, ln1_b, w_group_router,
           b_group_router, w_expert_router, b_expert_router, w_gate, w_up, w_down, w_ple, w_ple_gate,
           ln2_g, ln2_b):
    bsz, s, d = x.shape
    n = bsz * s
    assert d == D_MODEL and s % (SPAN * max(B_DILATIONS)) == 0 and n % TM_PROJ == 0
    assert w_in.shape[0] == 1, "one layer"

    w_in_b = w_in[0].astype(BF16)
    a_bias = jnp.repeat(a_bs[0].T, A_WIDTH // 8, axis=1)

    ga, gates, qkv1, qkv2, qkv3 = _proj(x, w_in_b, a_ln_g, a_ln_b, a_ws[0], a_bias)
    o1, l1 = _attn(qkv1.reshape(bsz, 1, s, 3 * COL))
    o2, l2 = _attn(qkv2)
    o3, l3 = _attn(qkv3)

    pad = 128 - N_GROUPS - N_EXPERTS_TOTAL
    wr = jnp.concatenate([w_group_router[0], w_expert_router[0].reshape(d, N_EXPERTS_TOTAL),
                          jnp.zeros((d, pad), F32)], axis=1).astype(BF16)
    br = jnp.concatenate([b_group_router[0], b_expert_router[0].reshape(-1), jnp.zeros((pad,), F32)])[None, :]
    x1, x1p, routet = _mix(
        ga, gates, o1.reshape(n, B_WIDTH), o2, o3, l1.reshape(n, 128), l2, l3, x.reshape(n, d),
        w_a_proj[0].astype(BF16), w_b_proj[0].astype(BF16), w_o[0].astype(BF16), wr, br, ln1_g, ln1_b)

    tile_expert, tile_valid, tile_block, piece = _routing_tables(routet, n)
    nt = tile_expert.shape[0]
    xs = _sc_scatter_rows(x1p.reshape(n * SUBROWS, 128), piece, nt * TM_MOE * SUBROWS)
    ys = _moe(tile_expert, tile_valid, tile_block, xs.reshape(nt * TM_MOE // 8, 32, 128),
              w_gate[0].reshape(N_EXPERTS_TOTAL, d, D_EXPERT), w_up[0].reshape(N_EXPERTS_TOTAL, d, D_EXPERT),
              w_down[0].reshape(N_EXPERTS_TOTAL, D_EXPERT, d)).reshape(nt * TM_MOE * SUBROWS, 128)
    out = None
    hp = n * SUBROWS // 2
    for half in range(2):
        yg = _sc_gather_rows(ys, piece[:, half * hp:(half + 1) * hp].reshape(-1))
        out = _final(yg.reshape(n // 8, 32, 128), x1, p[0].reshape(n, PLE_DIM), routet,
                     w_ple[0].astype(BF16), w_ple_gate[0].astype(BF16), ln2_g, ln2_b, half, out)
    return out.reshape(bsz, s, d)
```

```python
import functools

import jax
import jax.numpy as jnp
from jax import lax
from jax.experimental import pallas as pl
from jax.experimental.pallas import tpu as pltpu
from jax.experimental.pallas import tpu_sc as plsc

F32 = jnp.float32
BF16 = jnp.bfloat16
U32 = jnp.uint32
I32 = jnp.int32

D_MODEL = 1024
PLE_DIM = 256
A_WIDTH = 512
A_CHUNK = 128
B_HEAD_DIM = 64
B_HEADS = 8
B_WIDTH = 512
B_DILATIONS = (1, 4, 16)
SPAN = 128
N_GROUPS = 4
N_EXPERTS = 8
N_EXPERTS_TOTAL = N_GROUPS * N_EXPERTS
D_EXPERT = 256
DEEPNORM_ALPHA = 2.0 ** 0.25
LN_EPS = 1e-5
COL = 512
NEG = -1e30

VMEM_LIMIT = 56 * 1024 * 1024

TM_PROJ = 512
TM_ATTN = 2048
TM_MIX = 512
TM_MOE = 512
TM_FIN = 1024


def _ln(x, g, b):
    mu = jnp.mean(x, axis=-1, keepdims=True)
    xc = x - mu
    var = jnp.mean(xc * xc, axis=-1, keepdims=True)
    return xc * lax.rsqrt(var + LN_EPS) * g + b


def _gelu_tanh(x):
    return 0.5 * x * (1.0 + jnp.tanh(0.7978845608028654 * (x + 0.044715 * (x * x * x))))


def _sigmoid(x):
    return 0.5 * jnp.tanh(0.5 * x) + 0.5


def _dot(a, b):
    return jnp.dot(a, b, preferred_element_type=F32)


PACK_W = D_MODEL // 2
SUBROWS = PACK_W // 128


def _store_packed_rows(ref, x):
    m = x.shape[0]
    xb = x.astype(BF16).astype(F32)
    lo = pltpu.bitcast(xb[:, :PACK_W], U32) >> 16
    hi = pltpu.bitcast(xb[:, PACK_W:], U32) & jnp.uint32(0xFFFF0000)
    w = hi | lo
    for j in range(SUBROWS):
        ref[:, 8 * j:8 * (j + 1), :] = w[:, 128 * j:128 * (j + 1)].reshape(m // 8, 8, 128)


def _load_packed_rows(ref):
    m = ref.shape[0] * 8
    ws = [ref[:, 8 * j:8 * (j + 1), :].reshape(m, 128) for j in range(SUBROWS)]
    lo = [pltpu.bitcast(w << 16, F32) for w in ws]
    hi = [pltpu.bitcast(w & jnp.uint32(0xFFFF0000), F32) for w in ws]
    return jnp.concatenate(lo + hi, axis=1)


def _proj_kernel(x_ref, *refs):
    w = refs[:15]
    lng_ref, lnb_ref, ws_ref, bias_ref = refs[15:19]
    ga_ref, gates_ref, qkv1_ref, qkv2_ref, qkv3_ref = refs[19:24]
    xc_ref = refs[24]
    tm = x_ref.shape[0]
    xb = x_ref[...].astype(BF16)

    u_raw = _dot(xb, w[0][...])
    v_raw = _dot(xb, w[1][...])

    for i in range(4):
        gates_ref[:, i * COL:(i + 1) * COL] = _sigmoid(_dot(xb, w[11 + i][...])).astype(BF16)
    for j in range(3):
        qkv1_ref[:, j * COL:(j + 1) * COL] = _dot(xb, w[2 + 3 * j][...]).astype(BF16)

    for c in range(D_MODEL // 128):
        xc_ref[c] = x_ref[:, c * 128:(c + 1) * 128]
    for gi, out_ref in ((1, qkv2_ref), (2, qkv3_ref)):
        dl = B_DILATIONS[gi]
        per = tm // dl
        xp = jnp.concatenate(
            [jnp.concatenate([xc_ref[c, pl.ds(r, per, stride=dl), :] for c in range(D_MODEL // 128)], axis=1)
             for r in range(dl)], axis=0).astype(BF16)
        for j in range(3):
            res = _dot(xp, w[2 + 3 * j + gi][...]).astype(BF16)
            for r in range(dl):
                out_ref[r, :, j * COL:(j + 1) * COL] = res[r * per:(r + 1) * per]

    u = _gelu_tanh(u_raw)
    v = _gelu_tanh(v_raw)
    vn = _ln(v, lng_ref[...], lnb_ref[...]).astype(BF16)

    row = lax.broadcasted_iota(I32, (A_CHUNK, A_CHUNK), 0)
    colm = lax.broadcasted_iota(I32, (A_CHUNK, A_CHUNK), 1)
    causal = colm <= row
    lo = colm < 64
    zero = jnp.zeros((A_CHUNK, A_CHUNK), BF16)
    wcat = []
    for j in range(4):
        w0 = jnp.where(causal, ws_ref[2 * j], 0.0).astype(BF16)
        w1 = jnp.where(causal, ws_ref[2 * j + 1], 0.0).astype(BF16)
        wcat.append(jnp.concatenate([w0, w1], axis=1))
    for c in range(tm // A_CHUNK):
        r0 = c * A_CHUNK
        for j in range(4):
            c0 = j * 128
            vt = vn[r0:r0 + A_CHUNK, c0:c0 + 128]
            rhs = jnp.concatenate([jnp.where(lo, vt, zero), jnp.where(lo, zero, vt)], axis=0)
            mixed = _dot(wcat[j], rhs) + bias_ref[:, c0:c0 + 128]
            ga_ref[r0:r0 + A_CHUNK, c0:c0 + 128] = (u[r0:r0 + A_CHUNK, c0:c0 + 128] * mixed).astype(BF16)


def _proj(x, w_in_b, a_ln_g, a_ln_b, a_ws, a_bias):
    bsz, s, _ = x.shape
    n = bsz * s
    tm = TM_PROJ
    tiles = s // tm
    x2 = x.reshape(n, D_MODEL)
    wspec = lambda j: pl.BlockSpec((D_MODEL, COL), lambda i, j=j: (0, j), pipeline_mode=pl.Buffered(1))
    full = lambda shape: pl.BlockSpec(shape, lambda i: (0,) * len(shape))
    rows = lambda width: pl.BlockSpec((tm, width), lambda i: (i, 0))
    dil = lambda dl: pl.BlockSpec((None, dl, tm // dl, 3 * COL), lambda i: (i // tiles, 0, i % tiles, 0))
    return pl.pallas_call(
        _proj_kernel,
        grid=(n // tm,),
        in_specs=[rows(D_MODEL)] + [wspec(j) for j in range(15)]
                 + [full((1, A_WIDTH)), full((1, A_WIDTH)), full((8, A_CHUNK, A_CHUNK)), full((A_CHUNK, A_WIDTH))],
        out_specs=[rows(A_WIDTH), rows(4 * COL), rows(3 * COL), dil(4), dil(16)],
        out_shape=[jax.ShapeDtypeStruct((n, A_WIDTH), BF16),
                   jax.ShapeDtypeStruct((n, 4 * COL), BF16),
                   jax.ShapeDtypeStruct((n, 3 * COL), BF16),
                   jax.ShapeDtypeStruct((bsz, 4, s // 4, 3 * COL), BF16),
                   jax.ShapeDtypeStruct((bsz, 16, s // 16, 3 * COL), BF16)],
        scratch_shapes=[pltpu.VMEM((D_MODEL // 128, tm, 128), F32)],
        compiler_params=pltpu.CompilerParams(dimension_semantics=("parallel",), vmem_limit_bytes=VMEM_LIMIT),
        name="proj",
    )(x2, *([w_in_b] * 15), a_ln_g, a_ln_b, a_ws, a_bias)


def _attn_kernel(qkv_ref, o_ref, lse_ref, *, ns, seq):
    nb = seq // SPAN
    lane = lax.broadcasted_iota(I32, (SPAN, 128), 1)
    lo = lane < 64
    lane16 = lane // 16
    qi = lax.broadcasted_iota(I32, (SPAN, 2 * SPAN), 0)
    ki = lax.broadcasted_iota(I32, (SPAN, 2 * SPAN), 1)
    causal = lax.broadcasted_iota(I32, (SPAN, SPAN), 1) <= lax.broadcasted_iota(I32, (SPAN, SPAN), 0)
    bias_first = jnp.where(causal, 0.0, NEG).astype(F32)
    bias_first = jnp.concatenate([bias_first, bias_first], axis=0)
    bias_main = jnp.where((ki >= qi) & (ki <= qi + SPAN), 0.0, NEG).astype(F32)
    bias_main = jnp.concatenate([bias_main, bias_main], axis=0)
    zero = jnp.zeros((SPAN, 128), BF16)

    for s in range(ns):
        def block(row0, start, bias, s=s):
            win = bias.shape[1]
            pairs = range(B_HEADS // 2)
            scores, values = [], []
            for jp in pairs:
                c0 = jp * 128
                q = qkv_ref[s, pl.ds(row0, SPAN), c0:c0 + 128] * jnp.asarray(0.125, BF16)
                k = qkv_ref[s, pl.ds(start, win), COL + c0:COL + c0 + 128]
                values.append(qkv_ref[s, pl.ds(start, win), 2 * COL + c0:2 * COL + c0 + 128])
                qs = jnp.concatenate([jnp.where(lo, q, zero), jnp.where(lo, zero, q)], axis=0)
                scores.append(lax.dot_general(qs, k, (((1,), (1,)), ((), ())), preferred_element_type=F32) + bias)
            probs, maxes, sums = [], [], []
            for jp in pairs:
                m = jnp.max(scores[jp], axis=-1, keepdims=True)
                p = jnp.exp(scores[jp] - m)
                maxes.append(m)
                sums.append(jnp.sum(p, axis=-1, keepdims=True))
                probs.append(p.astype(BF16))
            lse_tile = jnp.zeros((SPAN, 128), F32)
            for jp in pairs:
                c0 = jp * 128
                ov = _dot(probs[jp], values[jp])
                inv = 1.0 / sums[jp]
                o = jnp.where(lo, ov[:SPAN] * inv[:SPAN], ov[SPAN:] * inv[SPAN:])
                o_ref[pl.ds(row0, SPAN), s * B_WIDTH + c0:s * B_WIDTH + c0 + 128] = o.astype(BF16)
                lse = maxes[jp] + jnp.log(sums[jp])
                lse_tile = jnp.where(lane16 == 2 * jp, lse[:SPAN],
                                     jnp.where(lane16 == 2 * jp + 1, lse[SPAN:], lse_tile))
            lse_ref[pl.ds(row0, SPAN), s * 128:(s + 1) * 128] = lse_tile

        block(0, 0, bias_first)
        if nb > 1:
            def body(i, carry):
                block(pl.multiple_of(i * SPAN, SPAN), pl.multiple_of((i - 1) * SPAN, SPAN), bias_main)
                return carry
            lax.fori_loop(1, nb, body, 0, unroll=min(5, nb - 1))


def _attn(qkv_g):
    bsz, dl, seq, _ = qkv_g.shape
    ns = max(1, min(dl, TM_ATTN // seq))
    return pl.pallas_call(
        functools.partial(_attn_kernel, ns=ns, seq=seq),
        grid=(bsz, dl // ns),
        in_specs=[pl.BlockSpec((None, ns, seq, 3 * COL), lambda b, r: (b, r, 0, 0))],
        out_specs=[pl.BlockSpec((None, seq, ns * B_WIDTH), lambda b, r: (b, 0, r)),
                   pl.BlockSpec((None, seq, ns * 128), lambda b, r: (b, 0, r))],
        out_shape=[jax.ShapeDtypeStruct((bsz, seq, dl * B_WIDTH), BF16),
                   jax.ShapeDtypeStruct((bsz, seq, dl * 128), F32)],
        compiler_params=pltpu.CompilerParams(dimension_semantics=("parallel", "parallel"),
                                             vmem_limit_bytes=VMEM_LIMIT),
        name=f"attn{dl}",
    )(qkv_g)


def _natural_rows(ref, dl, scr):
    nchunk, tm, _ = scr.shape
    w = nchunk * 128
    per = tm // dl
    for r in range(dl):
        for c in range(nchunk):
            scr[c, pl.ds(r, per, stride=dl), :] = ref[:, r * w + c * 128:r * w + (c + 1) * 128].astype(F32)
    return jnp.concatenate([scr[c] for c in range(nchunk)], axis=1)


def _mix_kernel(ga_ref, gates_ref, o1_ref, o2_ref, o3_ref, l1_ref, l2_ref, l3_ref, x_ref,
                wa_ref, wb_ref, wo_ref, wr_ref, br_ref, g1_ref, b1_ref,
                x1_ref, x1p_ref, routet_ref, o2s_ref, o3s_ref, l2s_ref, l3s_ref):
    tm = x_ref.shape[0]
    o2 = _natural_rows(o2_ref, 4, o2s_ref)
    o3 = _natural_rows(o3_ref, 16, o3s_ref)
    l2 = _natural_rows(l2_ref, 4, l2s_ref)
    l3 = _natural_rows(l3_ref, 16, l3s_ref)
    er = lax.broadcasted_iota(I32, (256, B_WIDTH), 0)
    ec = lax.broadcasted_iota(I32, (256, B_WIDTH), 1)
    expand = jnp.where(er % 128 == (ec // B_HEAD_DIM) * 16, 1.0, 0.0).astype(BF16)

    def widen(w):
        hi = w.astype(BF16)
        lo = (w - hi.astype(F32)).astype(BF16)
        return _dot(jnp.concatenate([hi, lo], axis=1), expand)

    h = tm // 2
    halves = (slice(0, h), slice(h, tm))
    obs = []
    for r in halves:
        l1 = l1_ref[r, :]
        mx = jnp.maximum(l1, jnp.maximum(l2[r], l3[r]))
        e1, e2, e3 = jnp.exp(l1 - mx), jnp.exp(l2[r] - mx), jnp.exp(l3[r] - mx)
        inv = 1.0 / (e1 + e2 + e3)
        obs.append(widen(e1 * inv) * o1_ref[r, :].astype(F32) + widen(e2 * inv) * o2[r] + widen(e3 * inv) * o3[r])
    ybs = [_dot(ob.astype(BF16), wb_ref[...]) for ob in obs]
    yas = [_dot(ga_ref[r, :], wa_ref[...]) for r in halves]
    pres = [gates_ref[r, :D_MODEL].astype(F32) * ya + gates_ref[r, D_MODEL:].astype(F32) * yb
            for r, ya, yb in zip(halves, yas, ybs)]
    mixes = [_dot(pre.astype(BF16), wo_ref[...]) for pre in pres]
    x1s = [_ln(DEEPNORM_ALPHA * x_ref[r, :] + mix, g1_ref[...], b1_ref[...]) for r, mix in zip(halves, mixes)]
    logits = [_dot(x1.astype(BF16), wr_ref[...]) + br_ref[...] for x1 in x1s]

    nrow = 40
    row = lax.broadcasted_iota(I32, (nrow, h), 0).astype(F32)
    row8 = lax.broadcasted_iota(I32, (8, h), 0)
    big = 1e9
    for i, r in enumerate(halves):
        x1_ref[r, :] = x1s[i]
        _store_packed_rows(x1p_ref.at[i * h // 8:(i + 1) * h // 8], x1s[i])
        lg = logits[i].T[:nrow, :]
        gl = jnp.where(row < N_GROUPS, lg, NEG)
        gm = jnp.max(gl, axis=0, keepdims=True)
        gidx = jnp.min(jnp.where(gl == gm, row, big), axis=0, keepdims=True)
        gsum = jnp.sum(jnp.where(row < N_GROUPS, jnp.exp(gl - gm), 0.0), axis=0, keepdims=True)
        gprob = 1.0 / gsum
        lo_row = N_GROUPS + N_EXPERTS * gidx
        el = jnp.where((row >= lo_row) & (row < lo_row + N_EXPERTS), lg, NEG)
        v1 = jnp.max(el, axis=0, keepdims=True)
        i1 = jnp.min(jnp.where(el == v1, row, big), axis=0, keepdims=True)
        el2 = jnp.where(row == i1, NEG, el)
        v2 = jnp.max(el2, axis=0, keepdims=True)
        i2 = jnp.min(jnp.where(el2 == v2, row, big), axis=0, keepdims=True)
        t = jnp.exp(v2 - v1)
        w1 = 1.0 / (1.0 + t)
        w2 = t * w1
        routet_ref[:, r] = jnp.where(row8 == 0, i1 - N_GROUPS,
                                     jnp.where(row8 == 1, i2 - N_GROUPS,
                                               jnp.where(row8 == 2, gprob * w1,
                                                         jnp.where(row8 == 3, gprob * w2, 0.0))))


def _mix(ga, gates, o1, o2, o3, l1, l2, l3, x2, wa, wb, wo, wr, br, g1, b1):
    n = x2.shape[0]
    bsz = o2.shape[0]
    tm = TM_MIX
    tiles = n // bsz // tm
    rows = lambda w: pl.BlockSpec((tm, w), lambda i: (i, 0))
    grouped = lambda a, dl: pl.BlockSpec((None, tm // dl, a.shape[2]), lambda i: (i // tiles, i % tiles, 0))
    full = lambda a: pl.BlockSpec(a.shape, lambda i: (0,) * a.ndim)
    return pl.pallas_call(
        _mix_kernel,
        grid=(n // tm,),
        in_specs=[rows(A_WIDTH), rows(2 * D_MODEL), rows(B_WIDTH), grouped(o2, 4), grouped(o3, 16),
                  rows(128), grouped(l2, 4), grouped(l3, 16), rows(D_MODEL),
                  full(wa), full(wb), full(wo), full(wr), full(br), full(g1), full(b1)],
        out_specs=[rows(D_MODEL), pl.BlockSpec((tm // 8, 32, 128), lambda i: (i, 0, 0)),
                   pl.BlockSpec((8, tm), lambda i: (0, i))],
        out_shape=[jax.ShapeDtypeStruct((n, D_MODEL), F32),
                   jax.ShapeDtypeStruct((n // 8, 32, 128), U32),
                   jax.ShapeDtypeStruct((8, n), F32)],
        scratch_shapes=[pltpu.VMEM((B_WIDTH // 128, tm, 128), F32), pltpu.VMEM((B_WIDTH // 128, tm, 128), F32),
                        pltpu.VMEM((1, tm, 128), F32), pltpu.VMEM((1, tm, 128), F32)],
        compiler_params=pltpu.CompilerParams(dimension_semantics=("parallel",), vmem_limit_bytes=VMEM_LIMIT),
        name="mix",
    )(ga, gates, o1, o2, o3, l1, l2, l3, x2, wa, wb, wo, wr, br, g1, b1)


SC_WINDOW = 128


def _sc_mesh():
    return plsc.VectorSubcoreMesh(core_axis_name="core", subcore_axis_name="subcore")


def _sc_scatter_rows(rows, dst, n_out):
    r = rows.shape[0]

    @pl.kernel(out_type=jax.ShapeDtypeStruct((n_out, 128), rows.dtype), mesh=_sc_mesh())
    def scatter(rows_hbm, dst0_hbm, dst1_hbm, out_hbm):
        def body(rows_vmem, dst0_vmem, dst1_vmem):
            pltpu.sync_copy(rows_vmem, out_hbm.at[dst0_vmem.at[0]])
            pltpu.sync_copy(rows_vmem, out_hbm.at[dst1_vmem.at[0]])

        pltpu.emit_pipeline(
            body,
            grid=(r // SC_WINDOW,),
            in_specs=[pl.BlockSpec((SC_WINDOW, 128), lambda i: (i, 0)),
                      pl.BlockSpec((1, SC_WINDOW), lambda i: (0, i)),
                      pl.BlockSpec((1, SC_WINDOW), lambda i: (0, i))],
            out_specs=[],
            core_axis_name=("core", "subcore"),
            dimension_semantics=(pltpu.PARALLEL,),
        )(rows_hbm, dst0_hbm, dst1_hbm)

    return scatter(rows, dst[0:1], dst[1:2])


def _sc_gather_rows(table, src):
    m = src.shape[0]
    k = 2

    @pl.kernel(out_type=jax.ShapeDtypeStruct((m, 128), table.dtype), mesh=_sc_mesh(),
               scratch_types=[pltpu.SemaphoreType.DMA((k,))])
    def gather(table_hbm, src_hbm, out_hbm, sems):
        def body(src_vmem, out_vmem):
            copies = [pltpu.async_copy(table_hbm.at[src_vmem.at[j]], out_vmem.at[pl.ds(j * SC_WINDOW, SC_WINDOW)],
                                       sems.at[j]) for j in range(k)]
            for c in copies:
                c.wait()

        pltpu.emit_pipeline(
            body,
            grid=(m // (k * SC_WINDOW),),
            in_specs=[pl.BlockSpec((k, SC_WINDOW), lambda i: (i, 0))],
            out_specs=[pl.BlockSpec((k * SC_WINDOW, 128), lambda i: (i, 0))],
            core_axis_name=("core", "subcore"),
            dimension_semantics=(pltpu.PARALLEL,),
        )(src_hbm, out_hbm)

    return gather(table, src.reshape(m // SC_WINDOW, SC_WINDOW))


def _moe_kernel(te_ref, tv_ref, tb_ref, xs_ref, wg_ref, wu_ref, wd_ref, ys_ref, wgb_ref, wub_ref, wdb_ref):
    del tb_ref
    t = pl.program_id(0)

    @pl.when((t == 0) | (te_ref[t] != te_ref[jnp.maximum(t - 1, 0)]))
    def _():
        wgb_ref[...] = wg_ref[...].astype(BF16)
        wub_ref[...] = wu_ref[...].astype(BF16)
        wdb_ref[...] = wd_ref[...].astype(BF16)

    @pl.when(tv_ref[t] == 1)
    def _():
        xb = _load_packed_rows(xs_ref).astype(BF16)
        g = _dot(xb, wgb_ref[...])
        u = _dot(xb, wub_ref[...])
        h = (g * _sigmoid(g) * u).astype(BF16)
        _store_packed_rows(ys_ref, _dot(h, wdb_ref[...]))


def _moe(tile_expert, tile_valid, tile_block, xs, wg, wu, wd):
    tm = TM_MOE
    nt = tile_expert.shape[0]
    rows = pl.BlockSpec((tm // 8, 32, 128), lambda t, te, tv, tb: (tb[t], 0, 0))
    return pl.pallas_call(
        _moe_kernel,
        grid_spec=pltpu.PrefetchScalarGridSpec(
            num_scalar_prefetch=3,
            grid=(nt,),
            in_specs=[rows,
                      pl.BlockSpec((None, D_MODEL, D_EXPERT), lambda t, te, tv, tb: (te[t], 0, 0)),
                      pl.BlockSpec((None, D_MODEL, D_EXPERT), lambda t, te, tv, tb: (te[t], 0, 0)),
                      pl.BlockSpec((None, D_EXPERT, D_MODEL), lambda t, te, tv, tb: (te[t], 0, 0))],
            out_specs=rows,
            scratch_shapes=[pltpu.VMEM((D_MODEL, D_EXPERT), BF16), pltpu.VMEM((D_MODEL, D_EXPERT), BF16),
                            pltpu.VMEM((D_EXPERT, D_MODEL), BF16)]),
        out_shape=jax.ShapeDtypeStruct((nt * tm // 8, 32, 128), U32),
        compiler_params=pltpu.CompilerParams(dimension_semantics=("arbitrary",), vmem_limit_bytes=VMEM_LIMIT),
        name="moe",
    )(tile_expert, tile_valid, tile_block, xs, wg, wu, wd)


def _final_kernel(y0_ref, y1_ref, x1_ref, p_ref, routet_ref, wple_ref, wpg_ref, g2_ref, b2_ref, *rest):
    out_ref = rest[-1]
    tm = x1_ref.shape[0]
    route = routet_ref[...].T
    h = tm // 4
    parts = tuple(slice(i * h, (i + 1) * h) for i in range(4))
    x1s = [x1_ref[r, :] for r in parts]
    plins = [_dot(p_ref[r, :].astype(BF16), wple_ref[...]) for r in parts]
    gates = [_dot(x1.astype(BF16), wpg_ref[...]) for x1 in x1s]
    sums = []
    for i, r in enumerate(parts):
        g8 = slice(i * h // 8, (i + 1) * h // 8)
        sums.append(DEEPNORM_ALPHA * x1s[i] + route[r, 2:3] * _load_packed_rows(y0_ref.at[g8])
                    + route[r, 3:4] * _load_packed_rows(y1_ref.at[g8]))
    for i, r in enumerate(parts):
        ple = plins[i] * _sigmoid(gates[i])
        out_ref[r, :] = _ln(sums[i] + ple, g2_ref[...], b2_ref[...])


def _final(yg, x1, p2, routet, wple, wpg, g2, b2, half, prev):
    n = x1.shape[0]
    tm = TM_FIN
    nt = n // 2 // tm
    off = half * nt
    rows = lambda w: pl.BlockSpec((tm, w), lambda t: (t + off, 0))
    full = lambda a: pl.BlockSpec(a.shape, lambda t: (0,) * a.ndim)
    in_specs = [pl.BlockSpec((tm // 8, 32, 128), lambda t: (t, 0, 0)),
                pl.BlockSpec((tm // 8, 32, 128), lambda t: (t + nt, 0, 0)),
                rows(D_MODEL), rows(PLE_DIM), pl.BlockSpec((8, tm), lambda t: (0, t + off)),
                full(wple), full(wpg), full(g2), full(b2)]
    args = [yg, yg, x1, p2, routet, wple, wpg, g2, b2]
    aliases = {}
    if prev is not None:
        in_specs.append(pl.BlockSpec(memory_space=pl.ANY))
        args.append(prev)
        aliases = {len(args) - 1: 0}
    return pl.pallas_call(
        _final_kernel,
        grid=(nt,),
        in_specs=in_specs,
        out_specs=rows(D_MODEL),
        out_shape=jax.ShapeDtypeStruct((n, D_MODEL), F32),
        input_output_aliases=aliases,
        compiler_params=pltpu.CompilerParams(dimension_semantics=("parallel",), vmem_limit_bytes=VMEM_LIMIT),
        name=f"final{half}",
    )(*args)


def _route_tables_kernel(e_ref, piece_ref, ends_ref):
    r = e_ref.shape[0]
    e = e_ref[...]
    ri = lax.broadcasted_iota(I32, (128, 128), 0)
    ci = lax.broadcasted_iota(I32, (128, 128), 1)
    upper = jnp.where(ri <= ci, 1.0, 0.0).astype(BF16)
    rr = lax.broadcasted_iota(I32, (r, r), 0)
    rc = lax.broadcasted_iota(I32, (r, r), 1)
    below = jnp.where(rc < rr, 1.0, 0.0).astype(BF16)
    lane = lax.broadcasted_iota(I32, (1, 128), 1)

    rank = jnp.zeros((r, 128), F32)
    counts = jnp.zeros((1, 128), F32)
    for x in range(N_EXPERTS_TOTAL):
        m = jnp.where(e == x, 1.0, 0.0)
        pre = _dot(m.astype(BF16), upper)
        tot = jnp.broadcast_to(pre[:, 127:128], (r, 128))
        off = _dot(below, tot.astype(BF16))
        rank = rank + m * (pre + off)
        counts = jnp.where(lane == x, off[r - 1:r, :] + tot[r - 1:r, :], counts)
    padded = jnp.floor((counts + (TM_MOE - 1)) * (1.0 / TM_MOE)) * TM_MOE
    ends = _dot(jnp.broadcast_to(padded, (8, 128)).astype(BF16), upper)[0:1, :]
    offs = ends - padded
    ends_ref[...] = jnp.broadcast_to(ends, (8, 128)).astype(I32)

    pos = rank - 1.0
    for x in range(N_EXPERTS_TOTAL):
        pos = pos + jnp.where(e == x, offs[:, x:x + 1], 0.0)

    hi = jnp.floor(pos * (1.0 / 256.0))
    lo = pos - 256.0 * hi
    jv = ((lane % 32) // 8).astype(F32)
    for c in range(4):
        sel = jnp.where(ri == 32 * c + 8 * (ci // 32) + ci % 8, 1.0, 0.0).astype(BF16)
        pc = 256.0 * _dot(hi.astype(BF16), sel) + _dot(lo.astype(BF16), sel)
        p8 = jnp.floor(pc * 0.125)
        piece = p8 * (8.0 * SUBROWS) + (pc - 8.0 * p8) + 8.0 * jv
        piece_ref[pl.ds(c, r, stride=4), :] = piece.astype(I32)


def _routing_tables(routet, n):
    tm = TM_MOE
    nt = (2 * n) // tm + N_EXPERTS_TOTAL
    r = 2 * n // 128
    piece, ends = pl.pallas_call(
        _route_tables_kernel,
        out_shape=[jax.ShapeDtypeStruct((4 * r, 128), I32), jax.ShapeDtypeStruct((8, 128), I32)],
        compiler_params=pltpu.CompilerParams(vmem_limit_bytes=VMEM_LIMIT),
        name="route_tables",
    )(routet[0:2].reshape(r, 128))
    ends = ends[0, :N_EXPERTS_TOTAL]
    tile_start = jnp.arange(nt, dtype=I32) * tm
    tile_expert = jnp.minimum(jnp.sum((tile_start[:, None] >= ends[None, :]).astype(I32), axis=1),
                              N_EXPERTS_TOTAL - 1).astype(I32)
    tile_valid = (tile_start < ends[-1]).astype(I32)
    tile_block = jnp.minimum(jnp.arange(nt, dtype=I32), ends[-1] // tm - 1)
    return tile_expert, tile_valid, tile_block, piece.reshape(2, n * SUBROWS)


def kernel(x, p, w_in, a_ln_g, a_ln_b, a_ws, a_bs, w_a_proj, w_b_proj, w_o, ln1_g, ln1_b, w_group_router,
           b_group_router, w_expert_router, b_expert_router, w_gate, w_up, w_down, w_ple, w_ple_gate,
           ln2_g, ln2_b):
    bsz, s, d = x.shape
    n = bsz * s
    assert d == D_MODEL and s % (SPAN * max(B_DILATIONS)) == 0 and n % TM_PROJ == 0
    assert w_in.shape[0] == 1, "one layer"

    w_in_b = w_in[0].astype(BF16)
    a_bias = jnp.repeat(a_bs[0].T, A_WIDTH // 8, axis=1)

    ga, gates, qkv1, qkv2, qkv3 = _proj(x, w_in_b, a_ln_g, a_ln_b, a_ws[0], a_bias)
    o1, l1 = _attn(qkv1.reshape(bsz, 1, s, 3 * COL))
    o2, l2 = _attn(qkv2)
    o3, l3 = _attn(qkv3)

    pad = 128 - N_GROUPS - N_EXPERTS_TOTAL
    wr = jnp.concatenate([w_group_router[0], w_expert_router[0].reshape(d, N_EXPERTS_TOTAL),
                          jnp.zeros((d, pad), F32)], axis=1).astype(BF16)
    br = jnp.concatenate([b_group_router[0], b_expert_router[0].reshape(-1), jnp.zeros((pad,), F32)])[None, :]
    x1, x1p, routet = _mix(
        ga, gates, o1.reshape(n, B_WIDTH), o2, o3, l1.reshape(n, 128), l2, l3, x.reshape(n, d),
        w_a_proj[0].astype(BF16), w_b_proj[0].astype(BF16), w_o[0].astype(BF16), wr, br, ln1_g, ln1_b)

    tile_expert, tile_valid, tile_block, piece = _routing_tables(routet, n)
    nt = tile_expert.shape[0]
    xs = _sc_scatter_rows(x1p.reshape(n * SUBROWS, 128), piece, nt * TM_MOE * SUBROWS)
    ys = _moe(tile_expert, tile_valid, tile_block, xs.reshape(nt * TM_MOE // 8, 32, 128),
              w_gate[0].reshape(N_EXPERTS_TOTAL, d, D_EXPERT), w_up[0].reshape(N_EXPERTS_TOTAL, d, D_EXPERT),
              w_down[0].reshape(N_EXPERTS_TOTAL, D_EXPERT, d)).reshape(nt * TM_MOE * SUBROWS, 128)
    out = None
    hp = n * SUBROWS // 2
    for half in range(2):
        yg = _sc_gather_rows(ys, piece[:, half * hp:(half + 1) * hp].reshape(-1))
        out = _final(yg.reshape(n // 8, 32, 128), x1, p[0].reshape(n, PLE_DIM), routet,
                     w_ple[0].astype(BF16), w_ple_gate[0].astype(BF16), ln2_g, ln2_b, half, out)
    return out.reshape(bsz, s, d)
```

```python
import functools

import jax
import jax.numpy as jnp
from jax import lax
from jax.experimental import pallas as pl
from jax.experimental.pallas import tpu as pltpu
from jax.experimental.pallas import tpu_sc as plsc

F32 = jnp.float32
BF16 = jnp.bfloat16
U32 = jnp.uint32
I32 = jnp.int32

D_MODEL = 1024
PLE_DIM = 256
A_WIDTH = 512
A_CHUNK = 128
B_HEAD_DIM = 64
B_HEADS = 8
B_WIDTH = 512
B_DILATIONS = (1, 4, 16)
SPAN = 128
N_GROUPS = 4
N_EXPERTS = 8
N_EXPERTS_TOTAL = N_GROUPS * N_EXPERTS
D_EXPERT = 256
DEEPNORM_ALPHA = 2.0 ** 0.25
LN_EPS = 1e-5
COL = 512
NEG = -1e30

VMEM_LIMIT = 56 * 1024 * 1024

TM_PROJ = 512
TM_ATTN = 1024
TM_MIX = 512
TM_MOE = 512
TM_FIN = 1024


def _ln(x, g, b):
    mu = jnp.mean(x, axis=-1, keepdims=True)
    xc = x - mu
    var = jnp.mean(xc * xc, axis=-1, keepdims=True)
    return xc * lax.rsqrt(var + LN_EPS) * g + b


def _gelu_tanh(x):
    return 0.5 * x * (1.0 + jnp.tanh(0.7978845608028654 * (x + 0.044715 * (x * x * x))))


def _sigmoid(x):
    return 0.5 * jnp.tanh(0.5 * x) + 0.5


def _dot(a, b):
    return jnp.dot(a, b, preferred_element_type=F32)


PACK_W = D_MODEL // 2
SUBROWS = PACK_W // 128


def _store_packed_rows(ref, x):
    m = x.shape[0]
    xb = x.astype(BF16).astype(F32)
    lo = pltpu.bitcast(xb[:, :PACK_W], U32) >> 16
    hi = pltpu.bitcast(xb[:, PACK_W:], U32) & jnp.uint32(0xFFFF0000)
    w = hi | lo
    for j in range(SUBROWS):
        ref[:, 8 * j:8 * (j + 1), :] = w[:, 128 * j:128 * (j + 1)].reshape(m // 8, 8, 128)


def _load_packed_rows(ref):
    m = ref.shape[0] * 8
    ws = [ref[:, 8 * j:8 * (j + 1), :].reshape(m, 128) for j in range(SUBROWS)]
    lo = [pltpu.bitcast(w << 16, F32) for w in ws]
    hi = [pltpu.bitcast(w & jnp.uint32(0xFFFF0000), F32) for w in ws]
    return jnp.concatenate(lo + hi, axis=1)


def _proj_kernel(x_ref, *refs):
    w = refs[:15]
    lng_ref, lnb_ref, ws_ref, bias_ref = refs[15:19]
    ga_ref, gates_ref, qkv1_ref, qkv2_ref, qkv3_ref = refs[19:24]
    xc_ref = refs[24]
    tm = x_ref.shape[0]
    xb = x_ref[...].astype(BF16)

    u_raw = _dot(xb, w[0][...])
    v_raw = _dot(xb, w[1][...])

    for i in range(4):
        gates_ref[:, i * COL:(i + 1) * COL] = _sigmoid(_dot(xb, w[11 + i][...])).astype(BF16)
    for j in range(3):
        qkv1_ref[:, j * COL:(j + 1) * COL] = _dot(xb, w[2 + 3 * j][...]).astype(BF16)

    for c in range(D_MODEL // 128):
        xc_ref[c] = x_ref[:, c * 128:(c + 1) * 128]
    for gi, out_ref in ((1, qkv2_ref), (2, qkv3_ref)):
        dl = B_DILATIONS[gi]
        per = tm // dl
        xp = jnp.concatenate(
            [jnp.concatenate([xc_ref[c, pl.ds(r, per, stride=dl), :] for c in range(D_MODEL // 128)], axis=1)
             for r in range(dl)], axis=0).astype(BF16)
        for j in range(3):
            res = _dot(xp, w[2 + 3 * j + gi][...]).astype(BF16)
            for r in range(dl):
                out_ref[r, :, j * COL:(j + 1) * COL] = res[r * per:(r + 1) * per]

    u = _gelu_tanh(u_raw)
    v = _gelu_tanh(v_raw)
    vn = _ln(v, lng_ref[...], lnb_ref[...]).astype(BF16)

    row = lax.broadcasted_iota(I32, (A_CHUNK, A_CHUNK), 0)
    colm = lax.broadcasted_iota(I32, (A_CHUNK, A_CHUNK), 1)
    causal = colm <= row
    lo = colm < 64
    zero = jnp.zeros((A_CHUNK, A_CHUNK), BF16)
    wcat = []
    for j in range(4):
        w0 = jnp.where(causal, ws_ref[2 * j], 0.0).astype(BF16)
        w1 = jnp.where(causal, ws_ref[2 * j + 1], 0.0).astype(BF16)
        wcat.append(jnp.concatenate([w0, w1], axis=1))
    for c in range(tm // A_CHUNK):
        r0 = c * A_CHUNK
        for j in range(4):
            c0 = j * 128
            vt = vn[r0:r0 + A_CHUNK, c0:c0 + 128]
            rhs = jnp.concatenate([jnp.where(lo, vt, zero), jnp.where(lo, zero, vt)], axis=0)
            mixed = _dot(wcat[j], rhs) + bias_ref[:, c0:c0 + 128]
            ga_ref[r0:r0 + A_CHUNK, c0:c0 + 128] = (u[r0:r0 + A_CHUNK, c0:c0 + 128] * mixed).astype(BF16)


def _proj(x, w_in_b, a_ln_g, a_ln_b, a_ws, a_bias):
    bsz, s, _ = x.shape
    n = bsz * s
    tm = TM_PROJ
    tiles = s // tm
    x2 = x.reshape(n, D_MODEL)
    wspec = lambda j: pl.BlockSpec((D_MODEL, COL), lambda i, j=j: (0, j), pipeline_mode=pl.Buffered(1))
    full = lambda shape: pl.BlockSpec(shape, lambda i: (0,) * len(shape))
    rows = lambda width: pl.BlockSpec((tm, width), lambda i: (i, 0))
    dil = lambda dl: pl.BlockSpec((None, dl, tm // dl, 3 * COL), lambda i: (i // tiles, 0, i % tiles, 0))
    return pl.pallas_call(
        _proj_kernel,
        grid=(n // tm,),
        in_specs=[rows(D_MODEL)] + [wspec(j) for j in range(15)]
                 + [full((1, A_WIDTH)), full((1, A_WIDTH)), full((8, A_CHUNK, A_CHUNK)), full((A_CHUNK, A_WIDTH))],
        out_specs=[rows(A_WIDTH), rows(4 * COL), rows(3 * COL), dil(4), dil(16)],
        out_shape=[jax.ShapeDtypeStruct((n, A_WIDTH), BF16),
                   jax.ShapeDtypeStruct((n, 4 * COL), BF16),
                   jax.ShapeDtypeStruct((n, 3 * COL), BF16),
                   jax.ShapeDtypeStruct((bsz, 4, s // 4, 3 * COL), BF16),
                   jax.ShapeDtypeStruct((bsz, 16, s // 16, 3 * COL), BF16)],
        scratch_shapes=[pltpu.VMEM((D_MODEL // 128, tm, 128), F32)],
        compiler_params=pltpu.CompilerParams(dimension_semantics=("parallel",), vmem_limit_bytes=VMEM_LIMIT),
        name="proj",
    )(x2, *([w_in_b] * 15), a_ln_g, a_ln_b, a_ws, a_bias)


def _attn_kernel(qkv_ref, o_ref, lse_ref, *, ns, seq):
    nb = seq // SPAN
    lane = lax.broadcasted_iota(I32, (SPAN, 128), 1)
    lo = lane < 64
    lane16 = lane // 16
    qi = lax.broadcasted_iota(I32, (SPAN, 2 * SPAN), 0)
    ki = lax.broadcasted_iota(I32, (SPAN, 2 * SPAN), 1)
    causal = lax.broadcasted_iota(I32, (SPAN, SPAN), 1) <= lax.broadcasted_iota(I32, (SPAN, SPAN), 0)
    bias_first = jnp.where(causal, 0.0, NEG).astype(F32)
    bias_first = jnp.concatenate([bias_first, bias_first], axis=0)
    bias_main = jnp.where((ki >= qi) & (ki <= qi + SPAN), 0.0, NEG).astype(F32)
    bias_main = jnp.concatenate([bias_main, bias_main], axis=0)
    zero = jnp.zeros((SPAN, 128), BF16)

    for s in range(ns):
        def block(row0, start, bias, s=s):
            win = bias.shape[1]
            pairs = range(B_HEADS // 2)
            scores, values = [], []
            for jp in pairs:
                c0 = jp * 128
                q = qkv_ref[s, pl.ds(row0, SPAN), c0:c0 + 128] * jnp.asarray(0.125, BF16)
                k = qkv_ref[s, pl.ds(start, win), COL + c0:COL + c0 + 128]
                values.append(qkv_ref[s, pl.ds(start, win), 2 * COL + c0:2 * COL + c0 + 128])
                qs = jnp.concatenate([jnp.where(lo, q, zero), jnp.where(lo, zero, q)], axis=0)
                scores.append(lax.dot_general(qs, k, (((1,), (1,)), ((), ())), preferred_element_type=F32) + bias)
            probs, maxes, sums = [], [], []
            for jp in pairs:
                m = jnp.max(scores[jp], axis=-1, keepdims=True)
                p = jnp.exp(scores[jp] - m)
                maxes.append(m)
                sums.append(jnp.sum(p, axis=-1, keepdims=True))
                probs.append(p.astype(BF16))
            lse_tile = jnp.zeros((SPAN, 128), F32)
            for jp in pairs:
                c0 = jp * 128
                ov = _dot(probs[jp], values[jp])
                inv = 1.0 / sums[jp]
                o = jnp.where(lo, ov[:SPAN] * inv[:SPAN], ov[SPAN:] * inv[SPAN:])
                o_ref[pl.ds(row0, SPAN), s * B_WIDTH + c0:s * B_WIDTH + c0 + 128] = o.astype(BF16)
                lse = maxes[jp] + jnp.log(sums[jp])
                lse_tile = jnp.where(lane16 == 2 * jp, lse[:SPAN],
                                     jnp.where(lane16 == 2 * jp + 1, lse[SPAN:], lse_tile))
            lse_ref[pl.ds(row0, SPAN), s * 128:(s + 1) * 128] = lse_tile

        block(0, 0, bias_first)
        if nb > 1:
            def body(i, carry):
                block(pl.multiple_of(i * SPAN, SPAN), pl.multiple_of((i - 1) * SPAN, SPAN), bias_main)
                return carry
            lax.fori_loop(1, nb, body, 0, unroll=min(5, nb - 1))


def _attn(qkv_g):
    bsz, dl, seq, _ = qkv_g.shape
    ns = max(1, min(dl, TM_ATTN // seq))
    return pl.pallas_call(
        functools.partial(_attn_kernel, ns=ns, seq=seq),
        grid=(bsz, dl // ns),
        in_specs=[pl.BlockSpec((None, ns, seq, 3 * COL), lambda b, r: (b, r, 0, 0))],
        out_specs=[pl.BlockSpec((None, seq, ns * B_WIDTH), lambda b, r: (b, 0, r)),
                   pl.BlockSpec((None, seq, ns * 128), lambda b, r: (b, 0, r))],
        out_shape=[jax.ShapeDtypeStruct((bsz, seq, dl * B_WIDTH), BF16),
                   jax.ShapeDtypeStruct((bsz, seq, dl * 128), F32)],
        compiler_params=pltpu.CompilerParams(dimension_semantics=("parallel", "parallel"),
                                             vmem_limit_bytes=VMEM_LIMIT),
        name=f"attn{dl}",
    )(qkv_g)


def _natural_rows(ref, dl, scr):
    nchunk, tm, _ = scr.shape
    w = nchunk * 128
    per = tm // dl
    for r in range(dl):
        for c in range(nchunk):
            scr[c, pl.ds(r, per, stride=dl), :] = ref[:, r * w + c * 128:r * w + (c + 1) * 128].astype(F32)
    return jnp.concatenate([scr[c] for c in range(nchunk)], axis=1)


def _mix_kernel(ga_ref, gates_ref, o1_ref, o2_ref, o3_ref, l1_ref, l2_ref, l3_ref, x_ref,
                wa_ref, wb_ref, wo_ref, wr_ref, br_ref, g1_ref, b1_ref,
                x1_ref, x1p_ref, routet_ref, o2s_ref, o3s_ref, l2s_ref, l3s_ref):
    tm = x_ref.shape[0]
    o2 = _natural_rows(o2_ref, 4, o2s_ref)
    o3 = _natural_rows(o3_ref, 16, o3s_ref)
    l2 = _natural_rows(l2_ref, 4, l2s_ref)
    l3 = _natural_rows(l3_ref, 16, l3s_ref)
    er = lax.broadcasted_iota(I32, (256, B_WIDTH), 0)
    ec = lax.broadcasted_iota(I32, (256, B_WIDTH), 1)
    expand = jnp.where(er % 128 == (ec // B_HEAD_DIM) * 16, 1.0, 0.0).astype(BF16)

    def widen(w):
        hi = w.astype(BF16)
        lo = (w - hi.astype(F32)).astype(BF16)
        return _dot(jnp.concatenate([hi, lo], axis=1), expand)

    h = tm // 2
    halves = (slice(0, h), slice(h, tm))
    obs = []
    for r in halves:
        l1 = l1_ref[r, :]
        mx = jnp.maximum(l1, jnp.maximum(l2[r], l3[r]))
        e1, e2, e3 = jnp.exp(l1 - mx), jnp.exp(l2[r] - mx), jnp.exp(l3[r] - mx)
        inv = 1.0 / (e1 + e2 + e3)
        obs.append(widen(e1 * inv) * o1_ref[r, :].astype(F32) + widen(e2 * inv) * o2[r] + widen(e3 * inv) * o3[r])
    ybs = [_dot(ob.astype(BF16), wb_ref[...]) for ob in obs]
    yas = [_dot(ga_ref[r, :], wa_ref[...]) for r in halves]
    pres = [gates_ref[r, :D_MODEL].astype(F32) * ya + gates_ref[r, D_MODEL:].astype(F32) * yb
            for r, ya, yb in zip(halves, yas, ybs)]
    mixes = [_dot(pre.astype(BF16), wo_ref[...]) for pre in pres]
    x1s = [_ln(DEEPNORM_ALPHA * x_ref[r, :] + mix, g1_ref[...], b1_ref[...]) for r, mix in zip(halves, mixes)]
    logits = [_dot(x1.astype(BF16), wr_ref[...]) + br_ref[...] for x1 in x1s]

    nrow = 40
    row = lax.broadcasted_iota(I32, (nrow, h), 0).astype(F32)
    row8 = lax.broadcasted_iota(I32, (8, h), 0)
    big = 1e9
    for i, r in enumerate(halves):
        x1_ref[r, :] = x1s[i]
        _store_packed_rows(x1p_ref.at[i * h // 8:(i + 1) * h // 8], x1s[i])
        lg = logits[i].T[:nrow, :]
        gl = jnp.where(row < N_GROUPS, lg, NEG)
        gm = jnp.max(gl, axis=0, keepdims=True)
        gidx = jnp.min(jnp.where(gl == gm, row, big), axis=0, keepdims=True)
        gsum = jnp.sum(jnp.where(row < N_GROUPS, jnp.exp(gl - gm), 0.0), axis=0, keepdims=True)
        gprob = 1.0 / gsum
        lo_row = N_GROUPS + N_EXPERTS * gidx
        el = jnp.where((row >= lo_row) & (row < lo_row + N_EXPERTS), lg, NEG)
        v1 = jnp.max(el, axis=0, keepdims=True)
        i1 = jnp.min(jnp.where(el == v1, row, big), axis=0, keepdims=True)
        el2 = jnp.where(row == i1, NEG, el)
        v2 = jnp.max(el2, axis=0, keepdims=True)
        i2 = jnp.min(jnp.where(el2 == v2, row, big), axis=0, keepdims=True)
        t = jnp.exp(v2 - v1)
        w1 = 1.0 / (1.0 + t)
        w2 = t * w1
        routet_ref[:, r] = jnp.where(row8 == 0, i1 - N_GROUPS,
                                     jnp.where(row8 == 1, i2 - N_GROUPS,
                                               jnp.where(row8 == 2, gprob * w1,
                                                         jnp.where(row8 == 3, gprob * w2, 0.0))))


def _mix(ga, gates, o1, o2, o3, l1, l2, l3, x2, wa, wb, wo, wr, br, g1, b1):
    n = x2.shape[0]
    bsz = o2.shape[0]
    tm = TM_MIX
    tiles = n // bsz // tm
    rows = lambda w: pl.BlockSpec((tm, w), lambda i: (i, 0))
    grouped = lambda a, dl: pl.BlockSpec((None, tm // dl, a.shape[2]), lambda i: (i // tiles, i % tiles, 0))
    full = lambda a: pl.BlockSpec(a.shape, lambda i: (0,) * a.ndim)
    return pl.pallas_call(
        _mix_kernel,
        grid=(n // tm,),
        in_specs=[rows(A_WIDTH), rows(2 * D_MODEL), rows(B_WIDTH), grouped(o2, 4), grouped(o3, 16),
                  rows(128), grouped(l2, 4), grouped(l3, 16), rows(D_MODEL),
                  full(wa), full(wb), full(wo), full(wr), full(br), full(g1), full(b1)],
        out_specs=[rows(D_MODEL), pl.BlockSpec((tm // 8, 32, 128), lambda i: (i, 0, 0)),
                   pl.BlockSpec((8, tm), lambda i: (0, i))],
        out_shape=[jax.ShapeDtypeStruct((n, D_MODEL), F32),
                   jax.ShapeDtypeStruct((n // 8, 32, 128), U32),
                   jax.ShapeDtypeStruct((8, n), F32)],
        scratch_shapes=[pltpu.VMEM((B_WIDTH // 128, tm, 128), F32), pltpu.VMEM((B_WIDTH // 128, tm, 128), F32),
                        pltpu.VMEM((1, tm, 128), F32), pltpu.VMEM((1, tm, 128), F32)],
        compiler_params=pltpu.CompilerParams(dimension_semantics=("parallel",), vmem_limit_bytes=VMEM_LIMIT),
        name="mix",
    )(ga, gates, o1, o2, o3, l1, l2, l3, x2, wa, wb, wo, wr, br, g1, b1)


SC_WINDOW = 128


def _sc_mesh():
    return plsc.VectorSubcoreMesh(core_axis_name="core", subcore_axis_name="subcore")


def _sc_scatter_rows(rows, dst, n_out):
    r = rows.shape[0]

    @pl.kernel(out_type=jax.ShapeDtypeStruct((n_out, 128), rows.dtype), mesh=_sc_mesh())
    def scatter(rows_hbm, dst0_hbm, dst1_hbm, out_hbm):
        def body(rows_vmem, dst0_vmem, dst1_vmem):
            pltpu.sync_copy(rows_vmem, out_hbm.at[dst0_vmem.at[0]])
            pltpu.sync_copy(rows_vmem, out_hbm.at[dst1_vmem.at[0]])

        pltpu.emit_pipeline(
            body,
            grid=(r // SC_WINDOW,),
            in_specs=[pl.BlockSpec((SC_WINDOW, 128), lambda i: (i, 0)),
                      pl.BlockSpec((1, SC_WINDOW), lambda i: (0, i)),
                      pl.BlockSpec((1, SC_WINDOW), lambda i: (0, i))],
            out_specs=[],
            core_axis_name=("core", "subcore"),
            dimension_semantics=(pltpu.PARALLEL,),
        )(rows_hbm, dst0_hbm, dst1_hbm)

    return scatter(rows, dst[0:1], dst[1:2])


def _sc_gather_rows(table, src):
    m = src.shape[0]
    k = 2

    @pl.kernel(out_type=jax.ShapeDtypeStruct((m, 128), table.dtype), mesh=_sc_mesh(),
               scratch_types=[pltpu.SemaphoreType.DMA((k,))])
    def gather(table_hbm, src_hbm, out_hbm, sems):
        def body(src_vmem, out_vmem):
            copies = [pltpu.async_copy(table_hbm.at[src_vmem.at[j]], out_vmem.at[pl.ds(j * SC_WINDOW, SC_WINDOW)],
                                       sems.at[j]) for j in range(k)]
            for c in copies:
                c.wait()

        pltpu.emit_pipeline(
            body,
            grid=(m // (k * SC_WINDOW),),
            in_specs=[pl.BlockSpec((k, SC_WINDOW), lambda i: (i, 0))],
            out_specs=[pl.BlockSpec((k * SC_WINDOW, 128), lambda i: (i, 0))],
            core_axis_name=("core", "subcore"),
            dimension_semantics=(pltpu.PARALLEL,),
        )(src_hbm, out_hbm)

    return gather(table, src.reshape(m // SC_WINDOW, SC_WINDOW))


def _moe_kernel(te_ref, tv_ref, tb_ref, xs_hbm, wg_ref, wu_ref, wd_ref, ys_ref,
                xbuf, wgb_ref, wub_ref, wdb_ref, sem):
    del tb_ref
    t = pl.program_id(0)
    nt = pl.num_programs(0)
    grp = TM_MOE // 8

    def fetch(j):
        return pltpu.make_async_copy(xs_hbm.at[pl.ds(j * grp, grp)], xbuf.at[j % 3], sem.at[j % 3])

    @pl.when(t == 0)
    def _():
        @pl.when(tv_ref[0] == 1)
        def _():
            fetch(0).start()

        @pl.when(tv_ref[1] == 1)
        def _():
            fetch(1).start()

    ahead = jnp.minimum(t + 2, nt - 1)

    @pl.when((t + 2 < nt) & (tv_ref[ahead] == 1))
    def _():
        fetch(t + 2).start()

    @pl.when((t == 0) | (te_ref[t] != te_ref[jnp.maximum(t - 1, 0)]))
    def _():
        wgb_ref[...] = wg_ref[...].astype(BF16)
        wub_ref[...] = wu_ref[...].astype(BF16)
        wdb_ref[...] = wd_ref[...].astype(BF16)

    @pl.when(tv_ref[t] == 1)
    def _():
        fetch(t).wait()
        xb = _load_packed_rows(xbuf.at[t % 3]).astype(BF16)
        g = _dot(xb, wgb_ref[...])
        u = _dot(xb, wub_ref[...])
        h = (g * _sigmoid(g) * u).astype(BF16)
        _store_packed_rows(ys_ref, _dot(h, wdb_ref[...]))


def _moe(tile_expert, tile_valid, tile_block, xs, wg, wu, wd):
    tm = TM_MOE
    nt = tile_expert.shape[0]
    return pl.pallas_call(
        _moe_kernel,
        grid_spec=pltpu.PrefetchScalarGridSpec(
            num_scalar_prefetch=3,
            grid=(nt,),
            in_specs=[pl.BlockSpec(memory_space=pl.ANY),
                      pl.BlockSpec((None, D_MODEL, D_EXPERT), lambda t, te, tv, tb: (te[t], 0, 0)),
                      pl.BlockSpec((None, D_MODEL, D_EXPERT), lambda t, te, tv, tb: (te[t], 0, 0)),
                      pl.BlockSpec((None, D_EXPERT, D_MODEL), lambda t, te, tv, tb: (te[t], 0, 0))],
            out_specs=pl.BlockSpec((tm // 8, 32, 128), lambda t, te, tv, tb: (tb[t], 0, 0)),
            scratch_shapes=[pltpu.VMEM((3, tm // 8, 32, 128), U32),
                            pltpu.VMEM((D_MODEL, D_EXPERT), BF16), pltpu.VMEM((D_MODEL, D_EXPERT), BF16),
                            pltpu.VMEM((D_EXPERT, D_MODEL), BF16), pltpu.SemaphoreType.DMA((3,))]),
        out_shape=jax.ShapeDtypeStruct((nt * tm // 8, 32, 128), U32),
        compiler_params=pltpu.CompilerParams(dimension_semantics=("arbitrary",), vmem_limit_bytes=VMEM_LIMIT),
        name="moe",
    )(tile_expert, tile_valid, tile_block, xs, wg, wu, wd)


def _final_kernel(y0_ref, y1_ref, x1_ref, p_ref, routet_ref, wple_ref, wpg_ref, g2_ref, b2_ref, *rest):
    out_ref = rest[-1]
    tm = x1_ref.shape[0]
    route = routet_ref[...].T
    h = tm // 4
    parts = tuple(slice(i * h, (i + 1) * h) for i in range(4))
    x1s = [x1_ref[r, :] for r in parts]
    plins = [_dot(p_ref[r, :].astype(BF16), wple_ref[...]) for r in parts]
    gates = [_dot(x1.astype(BF16), wpg_ref[...]) for x1 in x1s]
    sums = []
    for i, r in enumerate(parts):
        g8 = slice(i * h // 8, (i + 1) * h // 8)
        sums.append(DEEPNORM_ALPHA * x1s[i] + route[r, 2:3] * _load_packed_rows(y0_ref.at[g8])
                    + route[r, 3:4] * _load_packed_rows(y1_ref.at[g8]))
    for i, r in enumerate(parts):
        ple = plins[i] * _sigmoid(gates[i])
        out_ref[r, :] = _ln(sums[i] + ple, g2_ref[...], b2_ref[...])


def _final(yg, x1, p2, routet, wple, wpg, g2, b2, half, prev):
    n = x1.shape[0]
    tm = TM_FIN
    nt = n // 2 // tm
    off = half * nt
    rows = lambda w: pl.BlockSpec((tm, w), lambda t: (t + off, 0))
    full = lambda a: pl.BlockSpec(a.shape, lambda t: (0,) * a.ndim)
    in_specs = [pl.BlockSpec((tm // 8, 32, 128), lambda t: (t, 0, 0)),
                pl.BlockSpec((tm // 8, 32, 128), lambda t: (t + nt, 0, 0)),
                rows(D_MODEL), rows(PLE_DIM), pl.BlockSpec((8, tm), lambda t: (0, t + off)),
                full(wple), full(wpg), full(g2), full(b2)]
    args = [yg, yg, x1, p2, routet, wple, wpg, g2, b2]
    aliases = {}
    if prev is not None:
        in_specs.append(pl.BlockSpec(memory_space=pl.ANY))
        args.append(prev)
        aliases = {len(args) - 1: 0}
    return pl.pallas_call(
        _final_kernel,
        grid=(nt,),
        in_specs=in_specs,
        out_specs=rows(D_MODEL),
        out_shape=jax.ShapeDtypeStruct((n, D_MODEL), F32),
        input_output_aliases=aliases,
        compiler_params=pltpu.CompilerParams(dimension_semantics=("parallel",), vmem_limit_bytes=VMEM_LIMIT),
        name=f"final{half}",
    )(*args)


def _route_tables_kernel(e_ref, piece_ref, ends_ref):
    r = e_ref.shape[0]
    e = e_ref[...]
    ri = lax.broadcasted_iota(I32, (128, 128), 0)
    ci = lax.broadcasted_iota(I32, (128, 128), 1)
    upper = jnp.where(ri <= ci, 1.0, 0.0).astype(BF16)
    rr = lax.broadcasted_iota(I32, (r, r), 0)
    rc = lax.broadcasted_iota(I32, (r, r), 1)
    below = jnp.where(rc < rr, 1.0, 0.0).astype(BF16)
    lane = lax.broadcasted_iota(I32, (1, 128), 1)

    rank = jnp.zeros((r, 128), F32)
    counts = jnp.zeros((1, 128), F32)
    for x in range(N_EXPERTS_TOTAL):
        m = jnp.where(e == x, 1.0, 0.0)
        pre = _dot(m.astype(BF16), upper)
        tot = jnp.broadcast_to(pre[:, 127:128], (r, 128))
        off = _dot(below, tot.astype(BF16))
        rank = rank + m * (pre + off)
        counts = jnp.where(lane == x, off[r - 1:r, :] + tot[r - 1:r, :], counts)
    padded = jnp.floor((counts + (TM_MOE - 1)) * (1.0 / TM_MOE)) * TM_MOE
    ends = _dot(jnp.broadcast_to(padded, (8, 128)).astype(BF16), upper)[0:1, :]
    offs = ends - padded
    ends_ref[...] = jnp.broadcast_to(ends, (8, 128)).astype(I32)

    pos = rank - 1.0
    for x in range(N_EXPERTS_TOTAL):
        pos = pos + jnp.where(e == x, offs[:, x:x + 1], 0.0)

    hi = jnp.floor(pos * (1.0 / 256.0))
    lo = pos - 256.0 * hi
    jv = ((lane % 32) // 8).astype(F32)
    for c in range(4):
        sel = jnp.where(ri == 32 * c + 8 * (ci // 32) + ci % 8, 1.0, 0.0).astype(BF16)
        pc = 256.0 * _dot(hi.astype(BF16), sel) + _dot(lo.astype(BF16), sel)
        p8 = jnp.floor(pc * 0.125)
        piece = p8 * (8.0 * SUBROWS) + (pc - 8.0 * p8) + 8.0 * jv
        piece_ref[pl.ds(c, r, stride=4), :] = piece.astype(I32)


def _routing_tables(routet, n):
    tm = TM_MOE
    nt = (2 * n) // tm + N_EXPERTS_TOTAL
    r = 2 * n // 128
    piece, ends = pl.pallas_call(
        _route_tables_kernel,
        out_shape=[jax.ShapeDtypeStruct((4 * r, 128), I32), jax.ShapeDtypeStruct((8, 128), I32)],
        compiler_params=pltpu.CompilerParams(vmem_limit_bytes=VMEM_LIMIT),
        name="route_tables",
    )(routet[0:2].reshape(r, 128))
    ends = ends[0, :N_EXPERTS_TOTAL]
    tile_start = jnp.arange(nt, dtype=I32) * tm
    tile_expert = jnp.minimum(jnp.sum((tile_start[:, None] >= ends[None, :]).astype(I32), axis=1),
                              N_EXPERTS_TOTAL - 1).astype(I32)
    tile_valid = (tile_start < ends[-1]).astype(I32)
    tile_block = jnp.minimum(jnp.arange(nt, dtype=I32), ends[-1] // tm - 1)
    return tile_expert, tile_valid, tile_block, piece.reshape(2, n * SUBROWS)


def kernel(x, p, w_in, a_ln_g, a_ln_b, a_ws, a_bs, w_a_proj, w_b_proj, w_o, ln1_g, ln1_b, w_group_router,
           b_group_router, w_expert_router, b_expert_router, w_gate, w_up, w_down, w_ple, w_ple_gate,
           ln2_g, ln2_b):
    bsz, s, d = x.shape
    n = bsz * s
    assert d == D_MODEL and s % (SPAN * max(B_DILATIONS)) == 0 and n % TM_PROJ == 0
    assert w_in.shape[0] == 1, "one layer"

    w_in_b = w_in[0].astype(BF16)
    a_bias = jnp.repeat(a_bs[0].T, A_WIDTH // 8, axis=1)

    ga, gates, qkv1, qkv2, qkv3 = _proj(x, w_in_b, a_ln_g, a_ln_b, a_ws[0], a_bias)
    o1, l1 = _attn(qkv1.reshape(bsz, 1, s, 3 * COL))
    o2, l2 = _attn(qkv2)
    o3, l3 = _attn(qkv3)

    pad = 128 - N_GROUPS - N_EXPERTS_TOTAL
    wr = jnp.concatenate([w_group_router[0], w_expert_router[0].reshape(d, N_EXPERTS_TOTAL),
                          jnp.zeros((d, pad), F32)], axis=1).astype(BF16)
    br = jnp.concatenate([b_group_router[0], b_expert_router[0].reshape(-1), jnp.zeros((pad,), F32)])[None, :]
    x1, x1p, routet = _mix(
        ga, gates, o1.reshape(n, B_WIDTH), o2, o3, l1.reshape(n, 128), l2, l3, x.reshape(n, d),
        w_a_proj[0].astype(BF16), w_b_proj[0].astype(BF16), w_o[0].astype(BF16), wr, br, ln1_g, ln1_b)

    tile_expert, tile_valid, tile_block, piece = _routing_tables(routet, n)
    nt = tile_expert.shape[0]
    xs = _sc_scatter_rows(x1p.reshape(n * SUBROWS, 128), piece, nt * TM_MOE * SUBROWS)
    ys = _moe(tile_expert, tile_valid, tile_block, xs.reshape(nt * TM_MOE // 8, 32, 128),
              w_gate[0].reshape(N_EXPERTS_TOTAL, d, D_EXPERT), w_up[0].reshape(N_EXPERTS_TOTAL, d, D_EXPERT),
              w_down[0].reshape(N_EXPERTS_TOTAL, D_EXPERT, d)).reshape(nt * TM_MOE * SUBROWS, 128)
    out = None
    hp = n * SUBROWS // 2
    for half in range(2):
        yg = _sc_gather_rows(ys, piece[:, half * hp:(half + 1) * hp].reshape(-1))
        out = _final(yg.reshape(n // 8, 32, 128), x1, p[0].reshape(n, PLE_DIM), routet,
                     w_ple[0].astype(BF16), w_ple_gate[0].astype(BF16), ln2_g, ln2_b, half, out)
    return out.reshape(bsz, s, d)
```

```python
import functools

import jax
import jax.numpy as jnp
from jax import lax
from jax.experimental import pallas as pl
from jax.experimental.pallas import tpu as pltpu
from jax.experimental.pallas import tpu_sc as plsc

F32 = jnp.float32
BF16 = jnp.bfloat16
U32 = jnp.uint32
I32 = jnp.int32

D_MODEL = 1024
PLE_DIM = 256
A_WIDTH = 512
A_CHUNK = 128
B_HEAD_DIM = 64
B_HEADS = 8
B_WIDTH = 512
B_DILATIONS = (1, 4, 16)
SPAN = 128
N_GROUPS = 4
N_EXPERTS = 8
N_EXPERTS_TOTAL = N_GROUPS * N_EXPERTS
D_EXPERT = 256
DEEPNORM_ALPHA = 2.0 ** 0.25
LN_EPS = 1e-5
COL = 512
NEG = -1e30

VMEM_LIMIT = 56 * 1024 * 1024

TM_PROJ = 512
TM_ATTN = 1024
TM_MIX = 512
TM_MOE = 512
TM_FIN = 1024


def _ln(x, g, b):
    mu = jnp.mean(x, axis=-1, keepdims=True)
    xc = x - mu
    var = jnp.mean(xc * xc, axis=-1, keepdims=True)
    return xc * lax.rsqrt(var + LN_EPS) * g + b


def _gelu_tanh(x):
    return 0.5 * x * (1.0 + jnp.tanh(0.7978845608028654 * (x + 0.044715 * (x * x * x))))


def _sigmoid(x):
    return 0.5 * jnp.tanh(0.5 * x) + 0.5


def _dot(a, b):
    return jnp.dot(a, b, preferred_element_type=F32)


PACK_W = D_MODEL // 2
SUBROWS = PACK_W // 128


def _store_packed_rows(ref, x):
    m = x.shape[0]
    xb = x.astype(BF16).astype(F32)
    lo = pltpu.bitcast(xb[:, :PACK_W], U32) >> 16
    hi = pltpu.bitcast(xb[:, PACK_W:], U32) & jnp.uint32(0xFFFF0000)
    w = hi | lo
    for j in range(SUBROWS):
        ref[:, 8 * j:8 * (j + 1), :] = w[:, 128 * j:128 * (j + 1)].reshape(m // 8, 8, 128)


def _load_packed_rows(ref):
    m = ref.shape[0] * 8
    ws = [ref[:, 8 * j:8 * (j + 1), :].reshape(m, 128) for j in range(SUBROWS)]
    lo = [pltpu.bitcast(w << 16, F32) for w in ws]
    hi = [pltpu.bitcast(w & jnp.uint32(0xFFFF0000), F32) for w in ws]
    return jnp.concatenate(lo + hi, axis=1)


def _proj_kernel(x_ref, *refs):
    w = refs[:15]
    lng_ref, lnb_ref, ws_ref, bias_ref = refs[15:19]
    ga_ref, gates_ref, qkv1_ref, qkv2_ref, qkv3_ref = refs[19:24]
    xc_ref = refs[24]
    tm = x_ref.shape[0]
    xb = x_ref[...].astype(BF16)

    u_raw = _dot(xb, w[0][...])
    v_raw = _dot(xb, w[1][...])

    for i in range(4):
        gates_ref[:, i * COL:(i + 1) * COL] = _sigmoid(_dot(xb, w[11 + i][...])).astype(BF16)
    for j in range(3):
        qkv1_ref[:, j * COL:(j + 1) * COL] = _dot(xb, w[2 + 3 * j][...]).astype(BF16)

    for c in range(D_MODEL // 128):
        xc_ref[c] = x_ref[:, c * 128:(c + 1) * 128]
    for gi, out_ref in ((1, qkv2_ref), (2, qkv3_ref)):
        dl = B_DILATIONS[gi]
        per = tm // dl
        xp = jnp.concatenate(
            [jnp.concatenate([xc_ref[c, pl.ds(r, per, stride=dl), :] for c in range(D_MODEL // 128)], axis=1)
             for r in range(dl)], axis=0).astype(BF16)
        for j in range(3):
            res = _dot(xp, w[2 + 3 * j + gi][...]).astype(BF16)
            for r in range(dl):
                out_ref[r, :, j * COL:(j + 1) * COL] = res[r * per:(r + 1) * per]

    u = _gelu_tanh(u_raw)
    v = _gelu_tanh(v_raw)
    vn = _ln(v, lng_ref[...], lnb_ref[...]).astype(BF16)

    row = lax.broadcasted_iota(I32, (A_CHUNK, A_CHUNK), 0)
    colm = lax.broadcasted_iota(I32, (A_CHUNK, A_CHUNK), 1)
    causal = colm <= row
    lo = colm < 64
    zero = jnp.zeros((A_CHUNK, A_CHUNK), BF16)
    wcat = []
    for j in range(4):
        w0 = jnp.where(causal, ws_ref[2 * j], 0.0).astype(BF16)
        w1 = jnp.where(causal, ws_ref[2 * j + 1], 0.0).astype(BF16)
        wcat.append(jnp.concatenate([w0, w1], axis=1))
    for c in range(tm // A_CHUNK):
        r0 = c * A_CHUNK
        for j in range(4):
            c0 = j * 128
            vt = vn[r0:r0 + A_CHUNK, c0:c0 + 128]
            rhs = jnp.concatenate([jnp.where(lo, vt, zero), jnp.where(lo, zero, vt)], axis=0)
            mixed = _dot(wcat[j], rhs) + bias_ref[:, c0:c0 + 128]
            ga_ref[r0:r0 + A_CHUNK, c0:c0 + 128] = (u[r0:r0 + A_CHUNK, c0:c0 + 128] * mixed).astype(BF16)


def _proj(x, w_in_b, a_ln_g, a_ln_b, a_ws, a_bias):
    bsz, s, _ = x.shape
    n = bsz * s
    tm = TM_PROJ
    tiles = s // tm
    x2 = x.reshape(n, D_MODEL)
    wspec = lambda j: pl.BlockSpec((D_MODEL, COL), lambda i, j=j: (0, j), pipeline_mode=pl.Buffered(1))
    full = lambda shape: pl.BlockSpec(shape, lambda i: (0,) * len(shape))
    rows = lambda width: pl.BlockSpec((tm, width), lambda i: (i, 0))
    dil = lambda dl: pl.BlockSpec((None, dl, tm // dl, 3 * COL), lambda i: (i // tiles, 0, i % tiles, 0))
    return pl.pallas_call(
        _proj_kernel,
        grid=(n // tm,),
        in_specs=[rows(D_MODEL)] + [wspec(j) for j in range(15)]
                 + [full((1, A_WIDTH)), full((1, A_WIDTH)), full((8, A_CHUNK, A_CHUNK)), full((A_CHUNK, A_WIDTH))],
        out_specs=[rows(A_WIDTH), rows(4 * COL), rows(3 * COL), dil(4), dil(16)],
        out_shape=[jax.ShapeDtypeStruct((n, A_WIDTH), BF16),
                   jax.ShapeDtypeStruct((n, 4 * COL), BF16),
                   jax.ShapeDtypeStruct((n, 3 * COL), BF16),
                   jax.ShapeDtypeStruct((bsz, 4, s // 4, 3 * COL), BF16),
                   jax.ShapeDtypeStruct((bsz, 16, s // 16, 3 * COL), BF16)],
        scratch_shapes=[pltpu.VMEM((D_MODEL // 128, tm, 128), F32)],
        compiler_params=pltpu.CompilerParams(dimension_semantics=("parallel",), vmem_limit_bytes=VMEM_LIMIT),
        name="proj",
    )(x2, *([w_in_b] * 15), a_ln_g, a_ln_b, a_ws, a_bias)


def _attn_kernel(qkv_ref, o_ref, lse_ref, *, ns, seq):
    nb = seq // SPAN
    lane = lax.broadcasted_iota(I32, (SPAN, 128), 1)
    lo = lane < 64
    lane16 = lane // 16
    qi = lax.broadcasted_iota(I32, (SPAN, 2 * SPAN), 0)
    ki = lax.broadcasted_iota(I32, (SPAN, 2 * SPAN), 1)
    causal = lax.broadcasted_iota(I32, (SPAN, SPAN), 1) <= lax.broadcasted_iota(I32, (SPAN, SPAN), 0)
    bias_first = jnp.where(causal, 0.0, NEG).astype(F32)
    bias_first = jnp.concatenate([bias_first, bias_first], axis=0)
    bias_main = jnp.where((ki >= qi) & (ki <= qi + SPAN), 0.0, NEG).astype(F32)
    bias_main = jnp.concatenate([bias_main, bias_main], axis=0)
    zero = jnp.zeros((SPAN, 128), BF16)

    for s in range(ns):
        def block(row0, start, bias, s=s):
            win = bias.shape[1]
            pairs = range(B_HEADS // 2)
            scores, values = [], []
            for jp in pairs:
                c0 = jp * 128
                q = qkv_ref[s, pl.ds(row0, SPAN), c0:c0 + 128] * jnp.asarray(0.125, BF16)
                k = qkv_ref[s, pl.ds(start, win), COL + c0:COL + c0 + 128]
                values.append(qkv_ref[s, pl.ds(start, win), 2 * COL + c0:2 * COL + c0 + 128])
                qs = jnp.concatenate([jnp.where(lo, q, zero), jnp.where(lo, zero, q)], axis=0)
                scores.append(lax.dot_general(qs, k, (((1,), (1,)), ((), ())), preferred_element_type=F32) + bias)
            probs, maxes, sums = [], [], []
            for jp in pairs:
                m = jnp.max(scores[jp], axis=-1, keepdims=True)
                p = jnp.exp(scores[jp] - m)
                maxes.append(m)
                sums.append(jnp.sum(p, axis=-1, keepdims=True))
                probs.append(p.astype(BF16))
            lse_tile = jnp.zeros((SPAN, 128), F32)
            for jp in pairs:
                c0 = jp * 128
                ov = _dot(probs[jp], values[jp])
                inv = 1.0 / sums[jp]
                o = jnp.where(lo, ov[:SPAN] * inv[:SPAN], ov[SPAN:] * inv[SPAN:])
                o_ref[pl.ds(row0, SPAN), s * B_WIDTH + c0:s * B_WIDTH + c0 + 128] = o.astype(BF16)
                lse = maxes[jp] + jnp.log(sums[jp])
                lse_tile = jnp.where(lane16 == 2 * jp, lse[:SPAN],
                                     jnp.where(lane16 == 2 * jp + 1, lse[SPAN:], lse_tile))
            lse_ref[pl.ds(row0, SPAN), s * 128:(s + 1) * 128] = lse_tile

        block(0, 0, bias_first)
        if nb > 1:
            def body(i, carry):
                block(pl.multiple_of(i * SPAN, SPAN), pl.multiple_of((i - 1) * SPAN, SPAN), bias_main)
                return carry
            lax.fori_loop(1, nb, body, 0, unroll=min(5, nb - 1))


def _attn(qkv_g):
    bsz, dl, seq, _ = qkv_g.shape
    ns = max(1, min(dl, TM_ATTN // seq))
    return pl.pallas_call(
        functools.partial(_attn_kernel, ns=ns, seq=seq),
        grid=(bsz, dl // ns),
        in_specs=[pl.BlockSpec((None, ns, seq, 3 * COL), lambda b, r: (b, r, 0, 0))],
        out_specs=[pl.BlockSpec((None, seq, ns * B_WIDTH), lambda b, r: (b, 0, r)),
                   pl.BlockSpec((None, seq, ns * 128), lambda b, r: (b, 0, r))],
        out_shape=[jax.ShapeDtypeStruct((bsz, seq, dl * B_WIDTH), BF16),
                   jax.ShapeDtypeStruct((bsz, seq, dl * 128), F32)],
        compiler_params=pltpu.CompilerParams(dimension_semantics=("parallel", "parallel"),
                                             vmem_limit_bytes=VMEM_LIMIT),
        name=f"attn{dl}",
    )(qkv_g)


def _natural_rows(ref, dl, scr):
    nchunk, tm, _ = scr.shape
    w = nchunk * 128
    per = tm // dl
    for r in range(dl):
        for c in range(nchunk):
            scr[c, pl.ds(r, per, stride=dl), :] = ref[:, r * w + c * 128:r * w + (c + 1) * 128].astype(F32)
    return jnp.concatenate([scr[c] for c in range(nchunk)], axis=1)


def _mix_kernel(ga_ref, gates_ref, o1_ref, o2_ref, o3_ref, l1_ref, l2_ref, l3_ref, x_ref,
                wa_ref, wb_ref, wo_ref, wr_ref, br_ref, g1_ref, b1_ref,
                x1_ref, x1p_ref, routet_ref, o2s_ref, o3s_ref, l2s_ref, l3s_ref):
    tm = x_ref.shape[0]
    o2 = _natural_rows(o2_ref, 4, o2s_ref)
    o3 = _natural_rows(o3_ref, 16, o3s_ref)
    l2 = _natural_rows(l2_ref, 4, l2s_ref)
    l3 = _natural_rows(l3_ref, 16, l3s_ref)
    er = lax.broadcasted_iota(I32, (256, B_WIDTH), 0)
    ec = lax.broadcasted_iota(I32, (256, B_WIDTH), 1)
    expand = jnp.where(er % 128 == (ec // B_HEAD_DIM) * 16, 1.0, 0.0).astype(BF16)

    def widen(w):
        hi = w.astype(BF16)
        lo = (w - hi.astype(F32)).astype(BF16)
        return _dot(jnp.concatenate([hi, lo], axis=1), expand)

    h = tm // 2
    halves = (slice(0, h), slice(h, tm))
    obs = []
    for r in halves:
        l1 = l1_ref[r, :]
        mx = jnp.maximum(l1, jnp.maximum(l2[r], l3[r]))
        e1, e2, e3 = jnp.exp(l1 - mx), jnp.exp(l2[r] - mx), jnp.exp(l3[r] - mx)
        inv = 1.0 / (e1 + e2 + e3)
        obs.append(widen(e1 * inv) * o1_ref[r, :].astype(F32) + widen(e2 * inv) * o2[r] + widen(e3 * inv) * o3[r])
    ybs = [_dot(ob.astype(BF16), wb_ref[...]) for ob in obs]
    yas = [_dot(ga_ref[r, :], wa_ref[...]) for r in halves]
    pres = [gates_ref[r, :D_MODEL].astype(F32) * ya + gates_ref[r, D_MODEL:].astype(F32) * yb
            for r, ya, yb in zip(halves, yas, ybs)]
    mixes = [_dot(pre.astype(BF16), wo_ref[...]) for pre in pres]
    x1s = [_ln(DEEPNORM_ALPHA * x_ref[r, :] + mix, g1_ref[...], b1_ref[...]) for r, mix in zip(halves, mixes)]
    logits = [_dot(x1.astype(BF16), wr_ref[...]) + br_ref[...] for x1 in x1s]

    nrow = 40
    row = lax.broadcasted_iota(I32, (nrow, h), 0).astype(F32)
    row8 = lax.broadcasted_iota(I32, (8, h), 0)
    big = 1e9
    for i, r in enumerate(halves):
        x1_ref[r, :] = x1s[i]
        _store_packed_rows(x1p_ref.at[i * h // 8:(i + 1) * h // 8], x1s[i])
        lg = logits[i].T[:nrow, :]
        gl = jnp.where(row < N_GROUPS, lg, NEG)
        gm = jnp.max(gl, axis=0, keepdims=True)
        gidx = jnp.min(jnp.where(gl == gm, row, big), axis=0, keepdims=True)
        gsum = jnp.sum(jnp.where(row < N_GROUPS, jnp.exp(gl - gm), 0.0), axis=0, keepdims=True)
        gprob = 1.0 / gsum
        lo_row = N_GROUPS + N_EXPERTS * gidx
        el = jnp.where((row >= lo_row) & (row < lo_row + N_EXPERTS), lg, NEG)
        v1 = jnp.max(el, axis=0, keepdims=True)
        i1 = jnp.min(jnp.where(el == v1, row, big), axis=0, keepdims=True)
        el2 = jnp.where(row == i1, NEG, el)
        v2 = jnp.max(el2, axis=0, keepdims=True)
        i2 = jnp.min(jnp.where(el2 == v2, row, big), axis=0, keepdims=True)
        t = jnp.exp(v2 - v1)
        w1 = 1.0 / (1.0 + t)
        w2 = t * w1
        routet_ref[:, r] = jnp.where(row8 == 0, i1 - N_GROUPS,
                                     jnp.where(row8 == 1, i2 - N_GROUPS,
                                               jnp.where(row8 == 2, gprob * w1,
                                                         jnp.where(row8 == 3, gprob * w2, 0.0))))


def _mix(ga, gates, o1, o2, o3, l1, l2, l3, x2, wa, wb, wo, wr, br, g1, b1):
    n = x2.shape[0]
    bsz = o2.shape[0]
    tm = TM_MIX
    tiles = n // bsz // tm
    rows = lambda w: pl.BlockSpec((tm, w), lambda i: (i, 0))
    grouped = lambda a, dl: pl.BlockSpec((None, tm // dl, a.shape[2]), lambda i: (i // tiles, i % tiles, 0))
    full = lambda a: pl.BlockSpec(a.shape, lambda i: (0,) * a.ndim)
    return pl.pallas_call(
        _mix_kernel,
        grid=(n // tm,),
        in_specs=[rows(A_WIDTH), rows(2 * D_MODEL), rows(B_WIDTH), grouped(o2, 4), grouped(o3, 16),
                  rows(128), grouped(l2, 4), grouped(l3, 16), rows(D_MODEL),
                  full(wa), full(wb), full(wo), full(wr), full(br), full(g1), full(b1)],
        out_specs=[rows(D_MODEL), pl.BlockSpec((tm // 8, 32, 128), lambda i: (i, 0, 0)),
                   pl.BlockSpec((8, tm), lambda i: (0, i))],
        out_shape=[jax.ShapeDtypeStruct((n, D_MODEL), F32),
                   jax.ShapeDtypeStruct((n // 8, 32, 128), U32),
                   jax.ShapeDtypeStruct((8, n), F32)],
        scratch_shapes=[pltpu.VMEM((B_WIDTH // 128, tm, 128), F32), pltpu.VMEM((B_WIDTH // 128, tm, 128), F32),
                        pltpu.VMEM((1, tm, 128), F32), pltpu.VMEM((1, tm, 128), F32)],
        compiler_params=pltpu.CompilerParams(dimension_semantics=("parallel",), vmem_limit_bytes=VMEM_LIMIT),
        name="mix",
    )(ga, gates, o1, o2, o3, l1, l2, l3, x2, wa, wb, wo, wr, br, g1, b1)


SC_WINDOW = 128


def _sc_mesh():
    return plsc.VectorSubcoreMesh(core_axis_name="core", subcore_axis_name="subcore")


def _sc_scatter_rows(rows, dst, n_out):
    r = rows.shape[0]

    @pl.kernel(out_type=jax.ShapeDtypeStruct((n_out, 128), rows.dtype), mesh=_sc_mesh())
    def scatter(rows_hbm, dst0_hbm, dst1_hbm, out_hbm):
        def body(rows_vmem, dst0_vmem, dst1_vmem):
            pltpu.sync_copy(rows_vmem, out_hbm.at[dst0_vmem.at[0]])
            pltpu.sync_copy(rows_vmem, out_hbm.at[dst1_vmem.at[0]])

        pltpu.emit_pipeline(
            body,
            grid=(r // SC_WINDOW,),
            in_specs=[pl.BlockSpec((SC_WINDOW, 128), lambda i: (i, 0)),
                      pl.BlockSpec((1, SC_WINDOW), lambda i: (0, i)),
                      pl.BlockSpec((1, SC_WINDOW), lambda i: (0, i))],
            out_specs=[],
            core_axis_name=("core", "subcore"),
            dimension_semantics=(pltpu.PARALLEL,),
        )(rows_hbm, dst0_hbm, dst1_hbm)

    return scatter(rows, dst[0:1], dst[1:2])


def _sc_gather_rows(table, src):
    m = src.shape[0]
    k = 2

    @pl.kernel(out_type=jax.ShapeDtypeStruct((m, 128), table.dtype), mesh=_sc_mesh(),
               scratch_types=[pltpu.SemaphoreType.DMA((k,))])
    def gather(table_hbm, src_hbm, out_hbm, sems):
        def body(src_vmem, out_vmem):
            copies = [pltpu.async_copy(table_hbm.at[src_vmem.at[j]], out_vmem.at[pl.ds(j * SC_WINDOW, SC_WINDOW)],
                                       sems.at[j]) for j in range(k)]
            for c in copies:
                c.wait()

        pltpu.emit_pipeline(
            body,
            grid=(m // (k * SC_WINDOW),),
            in_specs=[pl.BlockSpec((k, SC_WINDOW), lambda i: (i, 0))],
            out_specs=[pl.BlockSpec((k * SC_WINDOW, 128), lambda i: (i, 0))],
            core_axis_name=("core", "subcore"),
            dimension_semantics=(pltpu.PARALLEL,),
        )(src_hbm, out_hbm)

    return gather(table, src.reshape(m // SC_WINDOW, SC_WINDOW))


MOE_X_SLOTS = 3
MOE_Y_SLOTS = 2


def _moe_kernel(te_ref, tv_ref, ws_ref, nx_ref, xs_hbm, wg_hbm, wu_hbm, wd_hbm, ys_hbm,
                xbuf, ybuf, wgf_ref, wuf_ref, wdf_ref, wgb_ref, wub_ref, wdb_ref, xsem, ysem, wsem):
    t = pl.program_id(0)
    nt = te_ref.shape[0]
    grp = TM_MOE // 8

    def fetch_x(j):
        return pltpu.make_async_copy(xs_hbm.at[pl.ds(j * grp, grp)], xbuf.at[j % MOE_X_SLOTS],
                                     xsem.at[j % MOE_X_SLOTS])

    def store_y(j):
        return pltpu.make_async_copy(ybuf.at[j % MOE_Y_SLOTS], ys_hbm.at[pl.ds(j * grp, grp)],
                                     ysem.at[j % MOE_Y_SLOTS])

    def fetch_w(e, slot):
        return [pltpu.make_async_copy(w.at[e], buf.at[slot], wsem.at[slot, i])
                for i, (w, buf) in enumerate(((wg_hbm, wgf_ref), (wu_hbm, wuf_ref), (wd_hbm, wdf_ref)))]

    @pl.when(t == 0)
    def _():
        for c in fetch_w(te_ref[0], 0):
            c.start()
        for j in range(MOE_X_SLOTS - 1):
            @pl.when(tv_ref[j] == 1)
            def _():
                fetch_x(j).start()

    ahead = t + (MOE_X_SLOTS - 1)

    @pl.when((ahead < nt) & (tv_ref[jnp.minimum(ahead, nt - 1)] == 1))
    def _():
        fetch_x(ahead).start()

    slot = ws_ref[t]

    @pl.when((t == 0) | (te_ref[t] != te_ref[jnp.maximum(t - 1, 0)]))
    def _():
        @pl.when(nx_ref[t] >= 0)
        def _():
            for c in fetch_w(nx_ref[t], 1 - slot):
                c.start()

        for c in fetch_w(te_ref[t], slot):
            c.wait()
        wgb_ref[...] = wgf_ref[slot].astype(BF16)
        wub_ref[...] = wuf_ref[slot].astype(BF16)
        wdb_ref[...] = wdf_ref[slot].astype(BF16)

    @pl.when((t >= MOE_Y_SLOTS) & (tv_ref[jnp.maximum(t - MOE_Y_SLOTS, 0)] == 1))
    def _():
        store_y(t - MOE_Y_SLOTS).wait()

    @pl.when(tv_ref[t] == 1)
    def _():
        fetch_x(t).wait()
        xb = _load_packed_rows(xbuf.at[t % MOE_X_SLOTS]).astype(BF16)
        g = _dot(xb, wgb_ref[...])
        u = _dot(xb, wub_ref[...])
        h = (g * _sigmoid(g) * u).astype(BF16)
        _store_packed_rows(ybuf.at[t % MOE_Y_SLOTS], _dot(h, wdb_ref[...]))
        store_y(t).start()

    @pl.when(t == nt - 1)
    def _():
        for j in range(nt - MOE_Y_SLOTS, nt):
            @pl.when(tv_ref[j] == 1)
            def _():
                store_y(j).wait()


def _moe(tile_expert, tile_valid, tile_wslot, tile_next, xs, wg, wu, wd):
    tm = TM_MOE
    nt = tile_expert.shape[0]
    any_spec = pl.BlockSpec(memory_space=pl.ANY)
    return pl.pallas_call(
        _moe_kernel,
        grid_spec=pltpu.PrefetchScalarGridSpec(
            num_scalar_prefetch=4,
            grid=(nt,),
            in_specs=[any_spec] * 4,
            out_specs=any_spec,
            scratch_shapes=[pltpu.VMEM((MOE_X_SLOTS, tm // 8, 32, 128), U32),
                            pltpu.VMEM((MOE_Y_SLOTS, tm // 8, 32, 128), U32),
                            pltpu.VMEM((2, D_MODEL, D_EXPERT), F32), pltpu.VMEM((2, D_MODEL, D_EXPERT), F32),
                            pltpu.VMEM((2, D_EXPERT, D_MODEL), F32),
                            pltpu.VMEM((D_MODEL, D_EXPERT), BF16), pltpu.VMEM((D_MODEL, D_EXPERT), BF16),
                            pltpu.VMEM((D_EXPERT, D_MODEL), BF16),
                            pltpu.SemaphoreType.DMA((MOE_X_SLOTS,)), pltpu.SemaphoreType.DMA((MOE_Y_SLOTS,)),
                            pltpu.SemaphoreType.DMA((2, 3))]),
        out_shape=jax.ShapeDtypeStruct((nt * tm // 8, 32, 128), U32),
        compiler_params=pltpu.CompilerParams(dimension_semantics=("arbitrary",), vmem_limit_bytes=VMEM_LIMIT),
        name="moe",
    )(tile_expert, tile_valid, tile_wslot, tile_next, xs, wg, wu, wd)


def _final_kernel(y0_ref, y1_ref, x1_ref, p_ref, routet_ref, wple_ref, wpg_ref, g2_ref, b2_ref, *rest):
    out_ref = rest[-1]
    tm = x1_ref.shape[0]
    route = routet_ref[...].T
    h = tm // 4
    parts = tuple(slice(i * h, (i + 1) * h) for i in range(4))
    x1s = [x1_ref[r, :] for r in parts]
    plins = [_dot(p_ref[r, :].astype(BF16), wple_ref[...]) for r in parts]
    gates = [_dot(x1.astype(BF16), wpg_ref[...]) for x1 in x1s]
    sums = []
    for i, r in enumerate(parts):
        g8 = slice(i * h // 8, (i + 1) * h // 8)
        sums.append(DEEPNORM_ALPHA * x1s[i] + route[r, 2:3] * _load_packed_rows(y0_ref.at[g8])
                    + route[r, 3:4] * _load_packed_rows(y1_ref.at[g8]))
    for i, r in enumerate(parts):
        ple = plins[i] * _sigmoid(gates[i])
        out_ref[r, :] = _ln(sums[i] + ple, g2_ref[...], b2_ref[...])


def _final(yg, x1, p2, routet, wple, wpg, g2, b2, half, prev):
    n = x1.shape[0]
    tm = TM_FIN
    nt = n // 2 // tm
    off = half * nt
    rows = lambda w: pl.BlockSpec((tm, w), lambda t: (t + off, 0))
    full = lambda a: pl.BlockSpec(a.shape, lambda t: (0,) * a.ndim)
    in_specs = [pl.BlockSpec((tm // 8, 32, 128), lambda t: (t, 0, 0)),
                pl.BlockSpec((tm // 8, 32, 128), lambda t: (t + nt, 0, 0)),
                rows(D_MODEL), rows(PLE_DIM), pl.BlockSpec((8, tm), lambda t: (0, t + off)),
                full(wple), full(wpg), full(g2), full(b2)]
    args = [yg, yg, x1, p2, routet, wple, wpg, g2, b2]
    aliases = {}
    if prev is not None:
        in_specs.append(pl.BlockSpec(memory_space=pl.ANY))
        args.append(prev)
        aliases = {len(args) - 1: 0}
    return pl.pallas_call(
        _final_kernel,
        grid=(nt,),
        in_specs=in_specs,
        out_specs=rows(D_MODEL),
        out_shape=jax.ShapeDtypeStruct((n, D_MODEL), F32),
        input_output_aliases=aliases,
        compiler_params=pltpu.CompilerParams(dimension_semantics=("parallel",), vmem_limit_bytes=VMEM_LIMIT),
        name=f"final{half}",
    )(*args)


def _route_tables_kernel(e_ref, piece_ref, ends_ref):
    r = e_ref.shape[0]
    e = e_ref[...]
    ri = lax.broadcasted_iota(I32, (128, 128), 0)
    ci = lax.broadcasted_iota(I32, (128, 128), 1)
    upper = jnp.where(ri <= ci, 1.0, 0.0).astype(BF16)
    rr = lax.broadcasted_iota(I32, (r, r), 0)
    rc = lax.broadcasted_iota(I32, (r, r), 1)
    below = jnp.where(rc < rr, 1.0, 0.0).astype(BF16)
    lane = lax.broadcasted_iota(I32, (1, 128), 1)

    rank = jnp.zeros((r, 128), F32)
    counts = jnp.zeros((1, 128), F32)
    for x in range(N_EXPERTS_TOTAL):
        m = jnp.where(e == x, 1.0, 0.0)
        pre = _dot(m.astype(BF16), upper)
        tot = jnp.broadcast_to(pre[:, 127:128], (r, 128))
        off = _dot(below, tot.astype(BF16))
        rank = rank + m * (pre + off)
        counts = jnp.where(lane == x, off[r - 1:r, :] + tot[r - 1:r, :], counts)
    padded = jnp.floor((counts + (TM_MOE - 1)) * (1.0 / TM_MOE)) * TM_MOE
    ends = _dot(jnp.broadcast_to(padded, (8, 128)).astype(BF16), upper)[0:1, :]
    offs = ends - padded
    ends_ref[...] = jnp.broadcast_to(ends, (8, 128)).astype(I32)

    pos = rank - 1.0
    for x in range(N_EXPERTS_TOTAL):
        pos = pos + jnp.where(e == x, offs[:, x:x + 1], 0.0)

    hi = jnp.floor(pos * (1.0 / 256.0))
    lo = pos - 256.0 * hi
    jv = ((lane % 32) // 8).astype(F32)
    for c in range(4):
        sel = jnp.where(ri == 32 * c + 8 * (ci // 32) + ci % 8, 1.0, 0.0).astype(BF16)
        pc = 256.0 * _dot(hi.astype(BF16), sel) + _dot(lo.astype(BF16), sel)
        p8 = jnp.floor(pc * 0.125)
        piece = p8 * (8.0 * SUBROWS) + (pc - 8.0 * p8) + 8.0 * jv
        piece_ref[pl.ds(c, r, stride=4), :] = piece.astype(I32)


def _routing_tables(routet, n):
    tm = TM_MOE
    nt = (2 * n) // tm + N_EXPERTS_TOTAL
    r = 2 * n // 128
    piece, ends = pl.pallas_call(
        _route_tables_kernel,
        out_shape=[jax.ShapeDtypeStruct((4 * r, 128), I32), jax.ShapeDtypeStruct((8, 128), I32)],
        compiler_params=pltpu.CompilerParams(vmem_limit_bytes=VMEM_LIMIT),
        name="route_tables",
    )(routet[0:2].reshape(r, 128))
    ends = ends[0, :N_EXPERTS_TOTAL]
    tile_start = jnp.arange(nt, dtype=I32) * tm
    tile_expert = jnp.minimum(jnp.sum((tile_start[:, None] >= ends[None, :]).astype(I32), axis=1),
                              N_EXPERTS_TOTAL - 1).astype(I32)
    tile_valid = (tile_start < ends[-1]).astype(I32)
    first = jnp.concatenate([jnp.ones((1,), I32), (tile_expert[1:] != tile_expert[:-1]).astype(I32)])
    tile_wslot = (jnp.cumsum(first) - 1) % 2
    later = jnp.where(tile_expert[None, :] > tile_expert[:, None], tile_expert[None, :], N_EXPERTS_TOTAL)
    tile_next = jnp.min(later, axis=1)
    tile_next = jnp.where(tile_next == N_EXPERTS_TOTAL, -1, tile_next).astype(I32)
    return tile_expert, tile_valid, tile_wslot.astype(I32), tile_next, piece.reshape(2, n * SUBROWS)


def kernel(x, p, w_in, a_ln_g, a_ln_b, a_ws, a_bs, w_a_proj, w_b_proj, w_o, ln1_g, ln1_b, w_group_router,
           b_group_router, w_expert_router, b_expert_router, w_gate, w_up, w_down, w_ple, w_ple_gate,
           ln2_g, ln2_b):
    bsz, s, d = x.shape
    n = bsz * s
    assert d == D_MODEL and s % (SPAN * max(B_DILATIONS)) == 0 and n % TM_PROJ == 0
    assert w_in.shape[0] == 1, "one layer"

    w_in_b = w_in[0].astype(BF16)
    a_bias = jnp.repeat(a_bs[0].T, A_WIDTH // 8, axis=1)

    ga, gates, qkv1, qkv2, qkv3 = _proj(x, w_in_b, a_ln_g, a_ln_b, a_ws[0], a_bias)
    o1, l1 = _attn(qkv1.reshape(bsz, 1, s, 3 * COL))
    o2, l2 = _attn(qkv2)
    o3, l3 = _attn(qkv3)

    pad = 128 - N_GROUPS - N_EXPERTS_TOTAL
    wr = jnp.concatenate([w_group_router[0], w_expert_router[0].reshape(d, N_EXPERTS_TOTAL),
                          jnp.zeros((d, pad), F32)], axis=1).astype(BF16)
    br = jnp.concatenate([b_group_router[0], b_expert_router[0].reshape(-1), jnp.zeros((pad,), F32)])[None, :]
    x1, x1p, routet = _mix(
        ga, gates, o1.reshape(n, B_WIDTH), o2, o3, l1.reshape(n, 128), l2, l3, x.reshape(n, d),
        w_a_proj[0].astype(BF16), w_b_proj[0].astype(BF16), w_o[0].astype(BF16), wr, br, ln1_g, ln1_b)

    tile_expert, tile_valid, tile_wslot, tile_next, piece = _routing_tables(routet, n)
    nt = tile_expert.shape[0]
    xs = _sc_scatter_rows(x1p.reshape(n * SUBROWS, 128), piece, nt * TM_MOE * SUBROWS)
    ys = _moe(tile_expert, tile_valid, tile_wslot, tile_next, xs.reshape(nt * TM_MOE // 8, 32, 128),
              w_gate[0].reshape(N_EXPERTS_TOTAL, d, D_EXPERT), w_up[0].reshape(N_EXPERTS_TOTAL, d, D_EXPERT),
              w_down[0].reshape(N_EXPERTS_TOTAL, D_EXPERT, d)).reshape(nt * TM_MOE * SUBROWS, 128)
    out = None
    hp = n * SUBROWS // 2
    for half in range(2):
        yg = _sc_gather_rows(ys, piece[:, half * hp:(half + 1) * hp].reshape(-1))
        out = _final(yg.reshape(n // 8, 32, 128), x1, p[0].reshape(n, PLE_DIM), routet,
                     w_ple[0].astype(BF16), w_ple_gate[0].astype(BF16), ln2_g, ln2_b, half, out)
    return out.reshape(bsz, s, d)
```

```python
import functools

import jax
import jax.numpy as jnp
from jax import lax
from jax.experimental import pallas as pl
from jax.experimental.pallas import tpu as pltpu
from jax.experimental.pallas import tpu_sc as plsc

F32 = jnp.float32
BF16 = jnp.bfloat16
U32 = jnp.uint32
I32 = jnp.int32

D_MODEL = 1024
PLE_DIM = 256
A_WIDTH = 512
A_CHUNK = 128
B_HEAD_DIM = 64
B_HEADS = 8
B_WIDTH = 512
B_DILATIONS = (1, 4, 16)
SPAN = 128
N_GROUPS = 4
N_EXPERTS = 8
N_EXPERTS_TOTAL = N_GROUPS * N_EXPERTS
D_EXPERT = 256
DEEPNORM_ALPHA = 2.0 ** 0.25
LN_EPS = 1e-5
COL = 512
NEG = -1e30

VMEM_LIMIT = 56 * 1024 * 1024

TM_PROJ = 512
TM_ATTN = 1024
TM_MIX = 512
TM_MOE = 512
TM_FIN = 1024


def _ln(x, g, b):
    mu = jnp.mean(x, axis=-1, keepdims=True)
    xc = x - mu
    var = jnp.mean(xc * xc, axis=-1, keepdims=True)
    return xc * lax.rsqrt(var + LN_EPS) * g + b


def _gelu_tanh(x):
    return 0.5 * x * (1.0 + jnp.tanh(0.7978845608028654 * (x + 0.044715 * (x * x * x))))


def _sigmoid(x):
    return 0.5 * jnp.tanh(0.5 * x) + 0.5


def _dot(a, b):
    return jnp.dot(a, b, preferred_element_type=F32)


PACK_W = D_MODEL // 2
SUBROWS = PACK_W // 128


def _store_packed_rows(ref, x):
    m = x.shape[0]
    xb = x.astype(BF16).astype(F32)
    lo = pltpu.bitcast(xb[:, :PACK_W], U32) >> 16
    hi = pltpu.bitcast(xb[:, PACK_W:], U32) & jnp.uint32(0xFFFF0000)
    w = hi | lo
    for j in range(SUBROWS):
        ref[:, 8 * j:8 * (j + 1), :] = w[:, 128 * j:128 * (j + 1)].reshape(m // 8, 8, 128)


def _load_packed_rows(ref):
    m = ref.shape[0] * 8
    ws = [ref[:, 8 * j:8 * (j + 1), :].reshape(m, 128) for j in range(SUBROWS)]
    lo = [pltpu.bitcast(w << 16, F32) for w in ws]
    hi = [pltpu.bitcast(w & jnp.uint32(0xFFFF0000), F32) for w in ws]
    return jnp.concatenate(lo + hi, axis=1)


W_IN_BLOCKS = 15
W_IN_SLOTS = 4


def _proj_kernel(x_ref, w_hbm, lng_ref, lnb_ref, ws_ref, bias_ref,
                 ga_ref, gates_ref, qkv1_ref, qkv2_ref, qkv3_ref, xc_ref, w, wstage_ref, wsem):
    tm = x_ref.shape[0]

    @pl.when(pl.program_id(0) == 0)
    def _():
        def fetch(j):
            return pltpu.make_async_copy(w_hbm.at[:, pl.ds(j * COL, COL)], wstage_ref.at[j % W_IN_SLOTS],
                                         wsem.at[j % W_IN_SLOTS])

        for j in range(W_IN_SLOTS):
            fetch(j).start()
        for j in range(W_IN_BLOCKS):
            fetch(j).wait()
            w[j] = wstage_ref[j % W_IN_SLOTS].astype(BF16)
            if j + W_IN_SLOTS < W_IN_BLOCKS:
                fetch(j + W_IN_SLOTS).start()

    xb = x_ref[...].astype(BF16)

    u_raw = _dot(xb, w[0])
    v_raw = _dot(xb, w[1])

    for i in range(4):
        gates_ref[:, i * COL:(i + 1) * COL] = _sigmoid(_dot(xb, w[11 + i])).astype(BF16)
    for j in range(3):
        qkv1_ref[:, j * COL:(j + 1) * COL] = _dot(xb, w[2 + 3 * j]).astype(BF16)

    for c in range(D_MODEL // 128):
        xc_ref[c] = x_ref[:, c * 128:(c + 1) * 128]
    for gi, out_ref in ((1, qkv2_ref), (2, qkv3_ref)):
        dl = B_DILATIONS[gi]
        per = tm // dl
        xp = jnp.concatenate(
            [jnp.concatenate([xc_ref[c, pl.ds(r, per, stride=dl), :] for c in range(D_MODEL // 128)], axis=1)
             for r in range(dl)], axis=0).astype(BF16)
        for j in range(3):
            res = _dot(xp, w[2 + 3 * j + gi]).astype(BF16)
            for r in range(dl):
                out_ref[r, :, j * COL:(j + 1) * COL] = res[r * per:(r + 1) * per]

    u = _gelu_tanh(u_raw)
    v = _gelu_tanh(v_raw)
    vn = _ln(v, lng_ref[...], lnb_ref[...]).astype(BF16)

    row = lax.broadcasted_iota(I32, (A_CHUNK, A_CHUNK), 0)
    colm = lax.broadcasted_iota(I32, (A_CHUNK, A_CHUNK), 1)
    causal = colm <= row
    lo = colm < 64
    zero = jnp.zeros((A_CHUNK, A_CHUNK), BF16)
    wcat = []
    for j in range(4):
        w0 = jnp.where(causal, ws_ref[2 * j], 0.0).astype(BF16)
        w1 = jnp.where(causal, ws_ref[2 * j + 1], 0.0).astype(BF16)
        wcat.append(jnp.concatenate([w0, w1], axis=1))
    for c in range(tm // A_CHUNK):
        r0 = c * A_CHUNK
        for j in range(4):
            c0 = j * 128
            vt = vn[r0:r0 + A_CHUNK, c0:c0 + 128]
            rhs = jnp.concatenate([jnp.where(lo, vt, zero), jnp.where(lo, zero, vt)], axis=0)
            mixed = _dot(wcat[j], rhs) + bias_ref[:, c0:c0 + 128]
            ga_ref[r0:r0 + A_CHUNK, c0:c0 + 128] = (u[r0:r0 + A_CHUNK, c0:c0 + 128] * mixed).astype(BF16)


def _proj(x, w_in, a_ln_g, a_ln_b, a_ws, a_bias):
    bsz, s, _ = x.shape
    n = bsz * s
    tm = TM_PROJ
    tiles = s // tm
    x2 = x.reshape(n, D_MODEL)
    full = lambda shape: pl.BlockSpec(shape, lambda i: (0,) * len(shape))
    rows = lambda width: pl.BlockSpec((tm, width), lambda i: (i, 0))
    dil = lambda dl: pl.BlockSpec((None, dl, tm // dl, 3 * COL), lambda i: (i // tiles, 0, i % tiles, 0))
    return pl.pallas_call(
        _proj_kernel,
        grid=(n // tm,),
        in_specs=[rows(D_MODEL), pl.BlockSpec(memory_space=pl.ANY)]
                 + [full((1, A_WIDTH)), full((1, A_WIDTH)), full((8, A_CHUNK, A_CHUNK)), full((A_CHUNK, A_WIDTH))],
        out_specs=[rows(A_WIDTH), rows(4 * COL), rows(3 * COL), dil(4), dil(16)],
        out_shape=[jax.ShapeDtypeStruct((n, A_WIDTH), BF16),
                   jax.ShapeDtypeStruct((n, 4 * COL), BF16),
                   jax.ShapeDtypeStruct((n, 3 * COL), BF16),
                   jax.ShapeDtypeStruct((bsz, 4, s // 4, 3 * COL), BF16),
                   jax.ShapeDtypeStruct((bsz, 16, s // 16, 3 * COL), BF16)],
        scratch_shapes=[pltpu.VMEM((D_MODEL // 128, tm, 128), F32),
                        pltpu.VMEM((W_IN_BLOCKS, D_MODEL, COL), BF16),
                        pltpu.VMEM((W_IN_SLOTS, D_MODEL, COL), F32),
                        pltpu.SemaphoreType.DMA((W_IN_SLOTS,))],
        compiler_params=pltpu.CompilerParams(dimension_semantics=("arbitrary",), vmem_limit_bytes=VMEM_LIMIT),
        name="proj",
    )(x2, w_in, a_ln_g, a_ln_b, a_ws, a_bias)


def _attn_kernel(qkv_ref, o_ref, lse_ref, *, ns, seq):
    nb = seq // SPAN
    lane = lax.broadcasted_iota(I32, (SPAN, 128), 1)
    lo = lane < 64
    lane16 = lane // 16
    qi = lax.broadcasted_iota(I32, (SPAN, 2 * SPAN), 0)
    ki = lax.broadcasted_iota(I32, (SPAN, 2 * SPAN), 1)
    causal = lax.broadcasted_iota(I32, (SPAN, SPAN), 1) <= lax.broadcasted_iota(I32, (SPAN, SPAN), 0)
    bias_first = jnp.where(causal, 0.0, NEG).astype(F32)
    bias_first = jnp.concatenate([bias_first, bias_first], axis=0)
    bias_main = jnp.where((ki >= qi) & (ki <= qi + SPAN), 0.0, NEG).astype(F32)
    bias_main = jnp.concatenate([bias_main, bias_main], axis=0)
    zero = jnp.zeros((SPAN, 128), BF16)

    for s in range(ns):
        def block(row0, start, bias, s=s):
            win = bias.shape[1]
            pairs = range(B_HEADS // 2)
            scores, values = [], []
            for jp in pairs:
                c0 = jp * 128
                q = qkv_ref[s, pl.ds(row0, SPAN), c0:c0 + 128] * jnp.asarray(0.125, BF16)
                k = qkv_ref[s, pl.ds(start, win), COL + c0:COL + c0 + 128]
                values.append(qkv_ref[s, pl.ds(start, win), 2 * COL + c0:2 * COL + c0 + 128])
                qs = jnp.concatenate([jnp.where(lo, q, zero), jnp.where(lo, zero, q)], axis=0)
                scores.append(lax.dot_general(qs, k, (((1,), (1,)), ((), ())), preferred_element_type=F32) + bias)
            probs, maxes, sums = [], [], []
            for jp in pairs:
                m = jnp.max(scores[jp], axis=-1, keepdims=True)
                p = jnp.exp(scores[jp] - m)
                maxes.append(m)
                sums.append(jnp.sum(p, axis=-1, keepdims=True))
                probs.append(p.astype(BF16))
            lse_tile = jnp.zeros((SPAN, 128), F32)
            for jp in pairs:
                c0 = jp * 128
                ov = _dot(probs[jp], values[jp])
                inv = 1.0 / sums[jp]
                o = jnp.where(lo, ov[:SPAN] * inv[:SPAN], ov[SPAN:] * inv[SPAN:])
                o_ref[pl.ds(row0, SPAN), s * B_WIDTH + c0:s * B_WIDTH + c0 + 128] = o.astype(BF16)
                lse = maxes[jp] + jnp.log(sums[jp])
                lse_tile = jnp.where(lane16 == 2 * jp, lse[:SPAN],
                                     jnp.where(lane16 == 2 * jp + 1, lse[SPAN:], lse_tile))
            lse_ref[pl.ds(row0, SPAN), s * 128:(s + 1) * 128] = lse_tile

        block(0, 0, bias_first)
        if nb > 1:
            def body(i, carry):
                block(pl.multiple_of(i * SPAN, SPAN), pl.multiple_of((i - 1) * SPAN, SPAN), bias_main)
                return carry
            lax.fori_loop(1, nb, body, 0, unroll=min(5, nb - 1))


def _attn(qkv_g):
    bsz, dl, seq, _ = qkv_g.shape
    ns = max(1, min(dl, TM_ATTN // seq))
    return pl.pallas_call(
        functools.partial(_attn_kernel, ns=ns, seq=seq),
        grid=(bsz, dl // ns),
        in_specs=[pl.BlockSpec((None, ns, seq, 3 * COL), lambda b, r: (b, r, 0, 0))],
        out_specs=[pl.BlockSpec((None, seq, ns * B_WIDTH), lambda b, r: (b, 0, r)),
                   pl.BlockSpec((None, seq, ns * 128), lambda b, r: (b, 0, r))],
        out_shape=[jax.ShapeDtypeStruct((bsz, seq, dl * B_WIDTH), BF16),
                   jax.ShapeDtypeStruct((bsz, seq, dl * 128), F32)],
        compiler_params=pltpu.CompilerParams(dimension_semantics=("parallel", "parallel"),
                                             vmem_limit_bytes=VMEM_LIMIT),
        name=f"attn{dl}",
    )(qkv_g)


def _natural_rows(ref, dl, scr):
    nchunk, tm, _ = scr.shape
    w = nchunk * 128
    per = tm // dl
    for r in range(dl):
        for c in range(nchunk):
            scr[c, pl.ds(r, per, stride=dl), :] = ref[:, r * w + c * 128:r * w + (c + 1) * 128].astype(F32)
    return jnp.concatenate([scr[c] for c in range(nchunk)], axis=1)


def _mix_kernel(ga_ref, gates_ref, o1_ref, o2_ref, o3_ref, l1_ref, l2_ref, l3_ref, x_ref,
                wa_ref, wb_ref, wo_ref, wr_ref, br_ref, g1_ref, b1_ref,
                x1_ref, x1p_ref, routet_ref, o2s_ref, o3s_ref, l2s_ref, l3s_ref):
    tm = x_ref.shape[0]
    o2 = _natural_rows(o2_ref, 4, o2s_ref)
    o3 = _natural_rows(o3_ref, 16, o3s_ref)
    l2 = _natural_rows(l2_ref, 4, l2s_ref)
    l3 = _natural_rows(l3_ref, 16, l3s_ref)
    er = lax.broadcasted_iota(I32, (256, B_WIDTH), 0)
    ec = lax.broadcasted_iota(I32, (256, B_WIDTH), 1)
    expand = jnp.where(er % 128 == (ec // B_HEAD_DIM) * 16, 1.0, 0.0).astype(BF16)

    def widen(w):
        hi = w.astype(BF16)
        lo = (w - hi.astype(F32)).astype(BF16)
        return _dot(jnp.concatenate([hi, lo], axis=1), expand)

    h = tm // 2
    halves = (slice(0, h), slice(h, tm))
    obs = []
    for r in halves:
        l1 = l1_ref[r, :]
        mx = jnp.maximum(l1, jnp.maximum(l2[r], l3[r]))
        e1, e2, e3 = jnp.exp(l1 - mx), jnp.exp(l2[r] - mx), jnp.exp(l3[r] - mx)
        inv = 1.0 / (e1 + e2 + e3)
        obs.append(widen(e1 * inv) * o1_ref[r, :].astype(F32) + widen(e2 * inv) * o2[r] + widen(e3 * inv) * o3[r])
    ybs = [_dot(ob.astype(BF16), wb_ref[...]) for ob in obs]
    yas = [_dot(ga_ref[r, :], wa_ref[...]) for r in halves]
    pres = [gates_ref[r, :D_MODEL].astype(F32) * ya + gates_ref[r, D_MODEL:].astype(F32) * yb
            for r, ya, yb in zip(halves, yas, ybs)]
    mixes = [_dot(pre.astype(BF16), wo_ref[...]) for pre in pres]
    x1s = [_ln(DEEPNORM_ALPHA * x_ref[r, :] + mix, g1_ref[...], b1_ref[...]) for r, mix in zip(halves, mixes)]
    logits = [_dot(x1.astype(BF16), wr_ref[...]) + br_ref[...] for x1 in x1s]

    nrow = 40
    row = lax.broadcasted_iota(I32, (nrow, h), 0).astype(F32)
    row8 = lax.broadcasted_iota(I32, (8, h), 0)
    big = 1e9
    for i, r in enumerate(halves):
        x1_ref[r, :] = x1s[i]
        _store_packed_rows(x1p_ref.at[i * h // 8:(i + 1) * h // 8], x1s[i])
        lg = logits[i].T[:nrow, :]
        gl = jnp.where(row < N_GROUPS, lg, NEG)
        gm = jnp.max(gl, axis=0, keepdims=True)
        gidx = jnp.min(jnp.where(gl == gm, row, big), axis=0, keepdims=True)
        gsum = jnp.sum(jnp.where(row < N_GROUPS, jnp.exp(gl - gm), 0.0), axis=0, keepdims=True)
        gprob = 1.0 / gsum
        lo_row = N_GROUPS + N_EXPERTS * gidx
        el = jnp.where((row >= lo_row) & (row < lo_row + N_EXPERTS), lg, NEG)
        v1 = jnp.max(el, axis=0, keepdims=True)
        i1 = jnp.min(jnp.where(el == v1, row, big), axis=0, keepdims=True)
        el2 = jnp.where(row == i1, NEG, el)
        v2 = jnp.max(el2, axis=0, keepdims=True)
        i2 = jnp.min(jnp.where(el2 == v2, row, big), axis=0, keepdims=True)
        t = jnp.exp(v2 - v1)
        w1 = 1.0 / (1.0 + t)
        w2 = t * w1
        routet_ref[:, r] = jnp.where(row8 == 0, i1 - N_GROUPS,
                                     jnp.where(row8 == 1, i2 - N_GROUPS,
                                               jnp.where(row8 == 2, gprob * w1,
                                                         jnp.where(row8 == 3, gprob * w2, 0.0))))


def _mix(ga, gates, o1, o2, o3, l1, l2, l3, x2, wa, wb, wo, wr, br, g1, b1):
    n = x2.shape[0]
    bsz = o2.shape[0]
    tm = TM_MIX
    tiles = n // bsz // tm
    rows = lambda w: pl.BlockSpec((tm, w), lambda i: (i, 0))
    grouped = lambda a, dl: pl.BlockSpec((None, tm // dl, a.shape[2]), lambda i: (i // tiles, i % tiles, 0))
    full = lambda a: pl.BlockSpec(a.shape, lambda i: (0,) * a.ndim)
    return pl.pallas_call(
        _mix_kernel,
        grid=(n // tm,),
        in_specs=[rows(A_WIDTH), rows(2 * D_MODEL), rows(B_WIDTH), grouped(o2, 4), grouped(o3, 16),
                  rows(128), grouped(l2, 4), grouped(l3, 16), rows(D_MODEL),
                  full(wa), full(wb), full(wo), full(wr), full(br), full(g1), full(b1)],
        out_specs=[rows(D_MODEL), pl.BlockSpec((tm // 8, 32, 128), lambda i: (i, 0, 0)),
                   pl.BlockSpec((8, tm), lambda i: (0, i))],
        out_shape=[jax.ShapeDtypeStruct((n, D_MODEL), F32),
                   jax.ShapeDtypeStruct((n // 8, 32, 128), U32),
                   jax.ShapeDtypeStruct((8, n), F32)],
        scratch_shapes=[pltpu.VMEM((B_WIDTH // 128, tm, 128), F32), pltpu.VMEM((B_WIDTH // 128, tm, 128), F32),
                        pltpu.VMEM((1, tm, 128), F32), pltpu.VMEM((1, tm, 128), F32)],
        compiler_params=pltpu.CompilerParams(dimension_semantics=("parallel",), vmem_limit_bytes=VMEM_LIMIT),
        name="mix",
    )(ga, gates, o1, o2, o3, l1, l2, l3, x2, wa, wb, wo, wr, br, g1, b1)


SC_WINDOW = 128


def _sc_mesh():
    return plsc.VectorSubcoreMesh(core_axis_name="core", subcore_axis_name="subcore")


def _sc_scatter_rows(rows, dst, n_out):
    r = rows.shape[0]

    @pl.kernel(out_type=jax.ShapeDtypeStruct((n_out, 128), rows.dtype), mesh=_sc_mesh())
    def scatter(rows_hbm, dst0_hbm, dst1_hbm, out_hbm):
        def body(rows_vmem, dst0_vmem, dst1_vmem):
            pltpu.sync_copy(rows_vmem, out_hbm.at[dst0_vmem.at[0]])
            pltpu.sync_copy(rows_vmem, out_hbm.at[dst1_vmem.at[0]])

        pltpu.emit_pipeline(
            body,
            grid=(r // SC_WINDOW,),
            in_specs=[pl.BlockSpec((SC_WINDOW, 128), lambda i: (i, 0)),
                      pl.BlockSpec((1, SC_WINDOW), lambda i: (0, i)),
                      pl.BlockSpec((1, SC_WINDOW), lambda i: (0, i))],
            out_specs=[],
            core_axis_name=("core", "subcore"),
            dimension_semantics=(pltpu.PARALLEL,),
        )(rows_hbm, dst0_hbm, dst1_hbm)

    return scatter(rows, dst[0:1], dst[1:2])


def _sc_gather_rows(table, src):
    m = src.shape[0]
    k = 2

    @pl.kernel(out_type=jax.ShapeDtypeStruct((m, 128), table.dtype), mesh=_sc_mesh(),
               scratch_types=[pltpu.SemaphoreType.DMA((k,))])
    def gather(table_hbm, src_hbm, out_hbm, sems):
        def body(src_vmem, out_vmem):
            copies = [pltpu.async_copy(table_hbm.at[src_vmem.at[j]], out_vmem.at[pl.ds(j * SC_WINDOW, SC_WINDOW)],
                                       sems.at[j]) for j in range(k)]
            for c in copies:
                c.wait()

        pltpu.emit_pipeline(
            body,
            grid=(m // (k * SC_WINDOW),),
            in_specs=[pl.BlockSpec((k, SC_WINDOW), lambda i: (i, 0))],
            out_specs=[pl.BlockSpec((k * SC_WINDOW, 128), lambda i: (i, 0))],
            core_axis_name=("core", "subcore"),
            dimension_semantics=(pltpu.PARALLEL,),
        )(src_hbm, out_hbm)

    return gather(table, src.reshape(m // SC_WINDOW, SC_WINDOW))


MOE_X_SLOTS = 3
MOE_Y_SLOTS = 2


def _moe_kernel(te_ref, tv_ref, ws_ref, nx_ref, xs_hbm, wg_hbm, wu_hbm, wd_hbm, ys_hbm,
                xbuf, ybuf, wgf_ref, wuf_ref, wdf_ref, wgb_ref, wub_ref, wdb_ref, xsem, ysem, wsem):
    t = pl.program_id(0)
    nt = te_ref.shape[0]
    grp = TM_MOE // 8

    def fetch_x(j):
        return pltpu.make_async_copy(xs_hbm.at[pl.ds(j * grp, grp)], xbuf.at[j % MOE_X_SLOTS],
                                     xsem.at[j % MOE_X_SLOTS])

    def store_y(j):
        return pltpu.make_async_copy(ybuf.at[j % MOE_Y_SLOTS], ys_hbm.at[pl.ds(j * grp, grp)],
                                     ysem.at[j % MOE_Y_SLOTS])

    def fetch_w(e, slot):
        return [pltpu.make_async_copy(w.at[e], buf.at[slot], wsem.at[slot, i])
                for i, (w, buf) in enumerate(((wg_hbm, wgf_ref), (wu_hbm, wuf_ref), (wd_hbm, wdf_ref)))]

    @pl.when(t == 0)
    def _():
        for c in fetch_w(te_ref[0], 0):
            c.start()
        for j in range(MOE_X_SLOTS - 1):
            @pl.when(tv_ref[j] == 1)
            def _():
                fetch_x(j).start()

    ahead = t + (MOE_X_SLOTS - 1)

    @pl.when((ahead < nt) & (tv_ref[jnp.minimum(ahead, nt - 1)] == 1))
    def _():
        fetch_x(ahead).start()

    slot = ws_ref[t]

    @pl.when((t == 0) | (te_ref[t] != te_ref[jnp.maximum(t - 1, 0)]))
    def _():
        @pl.when(nx_ref[t] >= 0)
        def _():
            for c in fetch_w(nx_ref[t], 1 - slot):
                c.start()

        for c in fetch_w(te_ref[t], slot):
            c.wait()
        wgb_ref[...] = wgf_ref[slot].astype(BF16)
        wub_ref[...] = wuf_ref[slot].astype(BF16)
        wdb_ref[...] = wdf_ref[slot].astype(BF16)

    @pl.when((t >= MOE_Y_SLOTS) & (tv_ref[jnp.maximum(t - MOE_Y_SLOTS, 0)] == 1))
    def _():
        store_y(t - MOE_Y_SLOTS).wait()

    @pl.when(tv_ref[t] == 1)
    def _():
        fetch_x(t).wait()
        xb = _load_packed_rows(xbuf.at[t % MOE_X_SLOTS]).astype(BF16)
        g = _dot(xb, wgb_ref[...])
        u = _dot(xb, wub_ref[...])
        h = (g * _sigmoid(g) * u).astype(BF16)
        _store_packed_rows(ybuf.at[t % MOE_Y_SLOTS], _dot(h, wdb_ref[...]))
        store_y(t).start()

    @pl.when(t == nt - 1)
    def _():
        for j in range(nt - MOE_Y_SLOTS, nt):
            @pl.when(tv_ref[j] == 1)
            def _():
                store_y(j).wait()


def _moe(tile_expert, tile_valid, tile_wslot, tile_next, xs, wg, wu, wd):
    tm = TM_MOE
    nt = tile_expert.shape[0]
    any_spec = pl.BlockSpec(memory_space=pl.ANY)
    return pl.pallas_call(
        _moe_kernel,
        grid_spec=pltpu.PrefetchScalarGridSpec(
            num_scalar_prefetch=4,
            grid=(nt,),
            in_specs=[any_spec] * 4,
            out_specs=any_spec,
            scratch_shapes=[pltpu.VMEM((MOE_X_SLOTS, tm // 8, 32, 128), U32),
                            pltpu.VMEM((MOE_Y_SLOTS, tm // 8, 32, 128), U32),
                            pltpu.VMEM((2, D_MODEL, D_EXPERT), F32), pltpu.VMEM((2, D_MODEL, D_EXPERT), F32),
                            pltpu.VMEM((2, D_EXPERT, D_MODEL), F32),
                            pltpu.VMEM((D_MODEL, D_EXPERT), BF16), pltpu.VMEM((D_MODEL, D_EXPERT), BF16),
                            pltpu.VMEM((D_EXPERT, D_MODEL), BF16),
                            pltpu.SemaphoreType.DMA((MOE_X_SLOTS,)), pltpu.SemaphoreType.DMA((MOE_Y_SLOTS,)),
                            pltpu.SemaphoreType.DMA((2, 3))]),
        out_shape=jax.ShapeDtypeStruct((nt * tm // 8, 32, 128), U32),
        compiler_params=pltpu.CompilerParams(dimension_semantics=("arbitrary",), vmem_limit_bytes=VMEM_LIMIT),
        name="moe",
    )(tile_expert, tile_valid, tile_wslot, tile_next, xs, wg, wu, wd)


def _final_kernel(y0_ref, y1_ref, x1_ref, p_ref, routet_ref, wple_ref, wpg_ref, g2_ref, b2_ref, *rest):
    out_ref = rest[-1]
    tm = x1_ref.shape[0]
    route = routet_ref[...].T
    h = tm // 4
    parts = tuple(slice(i * h, (i + 1) * h) for i in range(4))
    x1s = [x1_ref[r, :] for r in parts]
    plins = [_dot(p_ref[r, :].astype(BF16), wple_ref[...]) for r in parts]
    gates = [_dot(x1.astype(BF16), wpg_ref[...]) for x1 in x1s]
    sums = []
    for i, r in enumerate(parts):
        g8 = slice(i * h // 8, (i + 1) * h // 8)
        sums.append(DEEPNORM_ALPHA * x1s[i] + route[r, 2:3] * _load_packed_rows(y0_ref.at[g8])
                    + route[r, 3:4] * _load_packed_rows(y1_ref.at[g8]))
    for i, r in enumerate(parts):
        ple = plins[i] * _sigmoid(gates[i])
        out_ref[r, :] = _ln(sums[i] + ple, g2_ref[...], b2_ref[...])


def _final(yg, x1, p2, routet, wple, wpg, g2, b2, half, prev):
    n = x1.shape[0]
    tm = TM_FIN
    nt = n // 2 // tm
    off = half * nt
    rows = lambda w: pl.BlockSpec((tm, w), lambda t: (t + off, 0))
    full = lambda a: pl.BlockSpec(a.shape, lambda t: (0,) * a.ndim)
    in_specs = [pl.BlockSpec((tm // 8, 32, 128), lambda t: (t, 0, 0)),
                pl.BlockSpec((tm // 8, 32, 128), lambda t: (t + nt, 0, 0)),
                rows(D_MODEL), rows(PLE_DIM), pl.BlockSpec((8, tm), lambda t: (0, t + off)),
                full(wple), full(wpg), full(g2), full(b2)]
    args = [yg, yg, x1, p2, routet, wple, wpg, g2, b2]
    aliases = {}
    if prev is not None:
        in_specs.append(pl.BlockSpec(memory_space=pl.ANY))
        args.append(prev)
        aliases = {len(args) - 1: 0}
    return pl.pallas_call(
        _final_kernel,
        grid=(nt,),
        in_specs=in_specs,
        out_specs=rows(D_MODEL),
        out_shape=jax.ShapeDtypeStruct((n, D_MODEL), F32),
        input_output_aliases=aliases,
        compiler_params=pltpu.CompilerParams(dimension_semantics=("parallel",), vmem_limit_bytes=VMEM_LIMIT),
        name=f"final{half}",
    )(*args)


def _route_tables_kernel(e_ref, piece_ref, ends_ref):
    r = e_ref.shape[0]
    e = e_ref[...]
    ri = lax.broadcasted_iota(I32, (128, 128), 0)
    ci = lax.broadcasted_iota(I32, (128, 128), 1)
    upper = jnp.where(ri <= ci, 1.0, 0.0).astype(BF16)
    rr = lax.broadcasted_iota(I32, (r, r), 0)
    rc = lax.broadcasted_iota(I32, (r, r), 1)
    below = jnp.where(rc < rr, 1.0, 0.0).astype(BF16)
    lane = lax.broadcasted_iota(I32, (1, 128), 1)

    rank = jnp.zeros((r, 128), F32)
    counts = jnp.zeros((1, 128), F32)
    for x in range(N_EXPERTS_TOTAL):
        m = jnp.where(e == x, 1.0, 0.0)
        pre = _dot(m.astype(BF16), upper)
        tot = jnp.broadcast_to(pre[:, 127:128], (r, 128))
        off = _dot(below, tot.astype(BF16))
        rank = rank + m * (pre + off)
        counts = jnp.where(lane == x, off[r - 1:r, :] + tot[r - 1:r, :], counts)
    padded = jnp.floor((counts + (TM_MOE - 1)) * (1.0 / TM_MOE)) * TM_MOE
    ends = _dot(jnp.broadcast_to(padded, (8, 128)).astype(BF16), upper)[0:1, :]
    offs = ends - padded
    ends_ref[...] = jnp.broadcast_to(ends, (8, 128)).astype(I32)

    pos = rank - 1.0
    for x in range(N_EXPERTS_TOTAL):
        pos = pos + jnp.where(e == x, offs[:, x:x + 1], 0.0)

    hi = jnp.floor(pos * (1.0 / 256.0))
    lo = pos - 256.0 * hi
    jv = ((lane % 32) // 8).astype(F32)
    for c in range(4):
        sel = jnp.where(ri == 32 * c + 8 * (ci // 32) + ci % 8, 1.0, 0.0).astype(BF16)
        pc = 256.0 * _dot(hi.astype(BF16), sel) + _dot(lo.astype(BF16), sel)
        p8 = jnp.floor(pc * 0.125)
        piece = p8 * (8.0 * SUBROWS) + (pc - 8.0 * p8) + 8.0 * jv
        piece_ref[pl.ds(c, r, stride=4), :] = piece.astype(I32)


def _routing_tables(routet, n):
    tm = TM_MOE
    nt = (2 * n) // tm + N_EXPERTS_TOTAL
    r = 2 * n // 128
    piece, ends = pl.pallas_call(
        _route_tables_kernel,
        out_shape=[jax.ShapeDtypeStruct((4 * r, 128), I32), jax.ShapeDtypeStruct((8, 128), I32)],
        compiler_params=pltpu.CompilerParams(vmem_limit_bytes=VMEM_LIMIT),
        name="route_tables",
    )(routet[0:2].reshape(r, 128))
    ends = ends[0, :N_EXPERTS_TOTAL]
    tile_start = jnp.arange(nt, dtype=I32) * tm
    tile_expert = jnp.minimum(jnp.sum((tile_start[:, None] >= ends[None, :]).astype(I32), axis=1),
                              N_EXPERTS_TOTAL - 1).astype(I32)
    tile_valid = (tile_start < ends[-1]).astype(I32)
    first = jnp.concatenate([jnp.ones((1,), I32), (tile_expert[1:] != tile_expert[:-1]).astype(I32)])
    tile_wslot = (jnp.cumsum(first) - 1) % 2
    later = jnp.where(tile_expert[None, :] > tile_expert[:, None], tile_expert[None, :], N_EXPERTS_TOTAL)
    tile_next = jnp.min(later, axis=1)
    tile_next = jnp.where(tile_next == N_EXPERTS_TOTAL, -1, tile_next).astype(I32)
    return tile_expert, tile_valid, tile_wslot.astype(I32), tile_next, piece.reshape(2, n * SUBROWS)


def kernel(x, p, w_in, a_ln_g, a_ln_b, a_ws, a_bs, w_a_proj, w_b_proj, w_o, ln1_g, ln1_b, w_group_router,
           b_group_router, w_expert_router, b_expert_router, w_gate, w_up, w_down, w_ple, w_ple_gate,
           ln2_g, ln2_b):
    bsz, s, d = x.shape
    n = bsz * s
    assert d == D_MODEL and s % (SPAN * max(B_DILATIONS)) == 0 and n % TM_PROJ == 0
    assert w_in.shape[0] == 1, "one layer"

    a_bias = jnp.repeat(a_bs[0].T, A_WIDTH // 8, axis=1)

    ga, gates, qkv1, qkv2, qkv3 = _proj(x, w_in[0], a_ln_g, a_ln_b, a_ws[0], a_bias)
    o1, l1 = _attn(qkv1.reshape(bsz, 1, s, 3 * COL))
    o2, l2 = _attn(qkv2)
    o3, l3 = _attn(qkv3)

    pad = 128 - N_GROUPS - N_EXPERTS_TOTAL
    wr = jnp.concatenate([w_group_router[0], w_expert_router[0].reshape(d, N_EXPERTS_TOTAL),
                          jnp.zeros((d, pad), F32)], axis=1).astype(BF16)
    br = jnp.concatenate([b_group_router[0], b_expert_router[0].reshape(-1), jnp.zeros((pad,), F32)])[None, :]
    x1, x1p, routet = _mix(
        ga, gates, o1.reshape(n, B_WIDTH), o2, o3, l1.reshape(n, 128), l2, l3, x.reshape(n, d),
        w_a_proj[0].astype(BF16), w_b_proj[0].astype(BF16), w_o[0].astype(BF16), wr, br, ln1_g, ln1_b)

    tile_expert, tile_valid, tile_wslot, tile_next, piece = _routing_tables(routet, n)
    nt = tile_expert.shape[0]
    xs = _sc_scatter_rows(x1p.reshape(n * SUBROWS, 128), piece, nt * TM_MOE * SUBROWS)
    ys = _moe(tile_expert, tile_valid, tile_wslot, tile_next, xs.reshape(nt * TM_MOE // 8, 32, 128),
              w_gate[0].reshape(N_EXPERTS_TOTAL, d, D_EXPERT), w_up[0].reshape(N_EXPERTS_TOTAL, d, D_EXPERT),
              w_down[0].reshape(N_EXPERTS_TOTAL, D_EXPERT, d)).reshape(nt * TM_MOE * SUBROWS, 128)
    out = None
    hp = n * SUBROWS // 2
    for half in range(2):
        yg = _sc_gather_rows(ys, piece[:, half * hp:(half + 1) * hp].reshape(-1))
        out = _final(yg.reshape(n // 8, 32, 128), x1, p[0].reshape(n, PLE_DIM), routet,
                     w_ple[0].astype(BF16), w_ple_gate[0].astype(BF16), ln2_g, ln2_b, half, out)
    return out.reshape(bsz, s, d)
```

```python
import functools

import jax
import jax.numpy as jnp
from jax import lax
from jax.experimental import pallas as pl
from jax.experimental.pallas import tpu as pltpu
from jax.experimental.pallas import tpu_sc as plsc

F32 = jnp.float32
BF16 = jnp.bfloat16
U32 = jnp.uint32
I32 = jnp.int32

D_MODEL = 1024
PLE_DIM = 256
A_WIDTH = 512
A_CHUNK = 128
B_HEAD_DIM = 64
B_HEADS = 8
B_WIDTH = 512
B_DILATIONS = (1, 4, 16)
SPAN = 128
N_GROUPS = 4
N_EXPERTS = 8
N_EXPERTS_TOTAL = N_GROUPS * N_EXPERTS
D_EXPERT = 256
DEEPNORM_ALPHA = 2.0 ** 0.25
LN_EPS = 1e-5
COL = 512
NEG = -1e30

VMEM_LIMIT = 56 * 1024 * 1024

TM_PROJ = 512
TM_ATTN = 1024
TM_MIX = 512
TM_MOE = 512
TM_FIN = 1024


def _ln(x, g, b):
    mu = jnp.mean(x, axis=-1, keepdims=True)
    xc = x - mu
    var = jnp.mean(xc * xc, axis=-1, keepdims=True)
    return xc * lax.rsqrt(var + LN_EPS) * g + b


def _gelu_tanh(x):
    return 0.5 * x * (1.0 + jnp.tanh(0.7978845608028654 * (x + 0.044715 * (x * x * x))))


def _sigmoid(x):
    return 0.5 * jnp.tanh(0.5 * x) + 0.5


def _dot(a, b):
    return jnp.dot(a, b, preferred_element_type=F32)


PACK_W = D_MODEL // 2
SUBROWS = PACK_W // 128


def _store_packed_rows(ref, x):
    m = x.shape[0]
    xb = x.astype(BF16).astype(F32)
    lo = pltpu.bitcast(xb[:, :PACK_W], U32) >> 16
    hi = pltpu.bitcast(xb[:, PACK_W:], U32) & jnp.uint32(0xFFFF0000)
    w = hi | lo
    for j in range(SUBROWS):
        ref[:, 8 * j:8 * (j + 1), :] = w[:, 128 * j:128 * (j + 1)].reshape(m // 8, 8, 128)


def _load_packed_rows(ref):
    m = ref.shape[0] * 8
    ws = [ref[:, 8 * j:8 * (j + 1), :].reshape(m, 128) for j in range(SUBROWS)]
    lo = [pltpu.bitcast(w << 16, F32) for w in ws]
    hi = [pltpu.bitcast(w & jnp.uint32(0xFFFF0000), F32) for w in ws]
    return jnp.concatenate(lo + hi, axis=1)


W_IN_BLOCKS = 15
W_IN_SLOTS = 4


def _proj_kernel(x_ref, w_hbm, lng_ref, lnb_ref, ws_ref, bias_ref,
                 ga_ref, gates_ref, qkv1_ref, qkv2_ref, qkv3_ref, xc_ref, w, wstage_ref, wsem):
    tm = x_ref.shape[0]

    @pl.when(pl.program_id(0) == 0)
    def _():
        def fetch(j):
            return pltpu.make_async_copy(w_hbm.at[:, pl.ds(j * COL, COL)], wstage_ref.at[j % W_IN_SLOTS],
                                         wsem.at[j % W_IN_SLOTS])

        for j in range(W_IN_SLOTS):
            fetch(j).start()
        for j in range(W_IN_BLOCKS):
            fetch(j).wait()
            w[j] = wstage_ref[j % W_IN_SLOTS].astype(BF16)
            if j + W_IN_SLOTS < W_IN_BLOCKS:
                fetch(j + W_IN_SLOTS).start()

    xb = x_ref[...].astype(BF16)

    u_raw = _dot(xb, w[0])
    v_raw = _dot(xb, w[1])

    for i in range(4):
        gates_ref[:, i * COL:(i + 1) * COL] = _sigmoid(_dot(xb, w[11 + i])).astype(BF16)
    for j in range(3):
        qkv1_ref[:, j * COL:(j + 1) * COL] = _dot(xb, w[2 + 3 * j]).astype(BF16)

    for c in range(D_MODEL // 128):
        xc_ref[c] = x_ref[:, c * 128:(c + 1) * 128]
    for gi, out_ref in ((1, qkv2_ref), (2, qkv3_ref)):
        dl = B_DILATIONS[gi]
        per = tm // dl
        xp = jnp.concatenate(
            [jnp.concatenate([xc_ref[c, pl.ds(r, per, stride=dl), :] for c in range(D_MODEL // 128)], axis=1)
             for r in range(dl)], axis=0).astype(BF16)
        for j in range(3):
            res = _dot(xp, w[2 + 3 * j + gi]).astype(BF16)
            for r in range(dl):
                out_ref[r, :, j * COL:(j + 1) * COL] = res[r * per:(r + 1) * per]

    u = _gelu_tanh(u_raw)
    v = _gelu_tanh(v_raw)
    vn = _ln(v, lng_ref[...], lnb_ref[...]).astype(BF16)

    row = lax.broadcasted_iota(I32, (A_CHUNK, A_CHUNK), 0)
    colm = lax.broadcasted_iota(I32, (A_CHUNK, A_CHUNK), 1)
    causal = colm <= row
    lo = colm < 64
    zero = jnp.zeros((A_CHUNK, A_CHUNK), BF16)
    wcat = []
    for j in range(4):
        w0 = jnp.where(causal, ws_ref[2 * j], 0.0).astype(BF16)
        w1 = jnp.where(causal, ws_ref[2 * j + 1], 0.0).astype(BF16)
        wcat.append(jnp.concatenate([w0, w1], axis=1))
    for c in range(tm // A_CHUNK):
        r0 = c * A_CHUNK
        for j in range(4):
            c0 = j * 128
            vt = vn[r0:r0 + A_CHUNK, c0:c0 + 128]
            rhs = jnp.concatenate([jnp.where(lo, vt, zero), jnp.where(lo, zero, vt)], axis=0)
            mixed = _dot(wcat[j], rhs) + bias_ref[:, c0:c0 + 128]
            ga_ref[r0:r0 + A_CHUNK, c0:c0 + 128] = (u[r0:r0 + A_CHUNK, c0:c0 + 128] * mixed).astype(BF16)


def _proj(x, w_in, a_ln_g, a_ln_b, a_ws, a_bias):
    bsz, s, _ = x.shape
    n = bsz * s
    tm = TM_PROJ
    tiles = s // tm
    x2 = x.reshape(n, D_MODEL)
    full = lambda shape: pl.BlockSpec(shape, lambda i: (0,) * len(shape))
    rows = lambda width: pl.BlockSpec((tm, width), lambda i: (i, 0))
    dil = lambda dl: pl.BlockSpec((None, dl, tm // dl, 3 * COL), lambda i: (i // tiles, 0, i % tiles, 0))
    return pl.pallas_call(
        _proj_kernel,
        grid=(n // tm,),
        in_specs=[rows(D_MODEL), pl.BlockSpec(memory_space=pl.ANY)]
                 + [full((1, A_WIDTH)), full((1, A_WIDTH)), full((8, A_CHUNK, A_CHUNK)), full((A_CHUNK, A_WIDTH))],
        out_specs=[rows(A_WIDTH), rows(4 * COL), rows(3 * COL), dil(4), dil(16)],
        out_shape=[jax.ShapeDtypeStruct((n, A_WIDTH), BF16),
                   jax.ShapeDtypeStruct((n, 4 * COL), BF16),
                   jax.ShapeDtypeStruct((n, 3 * COL), BF16),
                   jax.ShapeDtypeStruct((bsz, 4, s // 4, 3 * COL), BF16),
                   jax.ShapeDtypeStruct((bsz, 16, s // 16, 3 * COL), BF16)],
        scratch_shapes=[pltpu.VMEM((D_MODEL // 128, tm, 128), F32),
                        pltpu.VMEM((W_IN_BLOCKS, D_MODEL, COL), BF16),
                        pltpu.VMEM((W_IN_SLOTS, D_MODEL, COL), F32),
                        pltpu.SemaphoreType.DMA((W_IN_SLOTS,))],
        compiler_params=pltpu.CompilerParams(dimension_semantics=("arbitrary",), vmem_limit_bytes=VMEM_LIMIT),
        name="proj",
    )(x2, w_in, a_ln_g, a_ln_b, a_ws, a_bias)


def _attn_kernel(qkv_ref, o_ref, lse_ref, *, ns, seq):
    nb = seq // SPAN
    lane = lax.broadcasted_iota(I32, (SPAN, 128), 1)
    lo = lane < 64
    lane16 = lane // 16
    qi = lax.broadcasted_iota(I32, (SPAN, 2 * SPAN), 0)
    ki = lax.broadcasted_iota(I32, (SPAN, 2 * SPAN), 1)
    causal = lax.broadcasted_iota(I32, (SPAN, SPAN), 1) <= lax.broadcasted_iota(I32, (SPAN, SPAN), 0)
    bias_first = jnp.where(causal, 0.0, NEG).astype(F32)
    bias_first = jnp.concatenate([bias_first, bias_first], axis=0)
    bias_main = jnp.where((ki >= qi) & (ki <= qi + SPAN), 0.0, NEG).astype(F32)
    bias_main = jnp.concatenate([bias_main, bias_main], axis=0)
    zero = jnp.zeros((SPAN, 128), BF16)

    for s in range(ns):
        def block(row0, start, bias, s=s):
            win = bias.shape[1]
            pairs = range(B_HEADS // 2)
            scores, values = [], []
            for jp in pairs:
                c0 = jp * 128
                q = qkv_ref[s, pl.ds(row0, SPAN), c0:c0 + 128] * jnp.asarray(0.125, BF16)
                k = qkv_ref[s, pl.ds(start, win), COL + c0:COL + c0 + 128]
                values.append(qkv_ref[s, pl.ds(start, win), 2 * COL + c0:2 * COL + c0 + 128])
                qs = jnp.concatenate([jnp.where(lo, q, zero), jnp.where(lo, zero, q)], axis=0)
                scores.append(lax.dot_general(qs, k, (((1,), (1,)), ((), ())), preferred_element_type=F32) + bias)
            probs, maxes, sums = [], [], []
            for jp in pairs:
                m = jnp.max(scores[jp], axis=-1, keepdims=True)
                p = jnp.exp(scores[jp] - m)
                maxes.append(m)
                sums.append(jnp.sum(p, axis=-1, keepdims=True))
                probs.append(p.astype(BF16))
            lse_tile = jnp.zeros((SPAN, 128), F32)
            for jp in pairs:
                c0 = jp * 128
                ov = _dot(probs[jp], values[jp])
                inv = 1.0 / sums[jp]
                o = jnp.where(lo, ov[:SPAN] * inv[:SPAN], ov[SPAN:] * inv[SPAN:])
                o_ref[pl.ds(row0, SPAN), s * B_WIDTH + c0:s * B_WIDTH + c0 + 128] = o.astype(BF16)
                lse = maxes[jp] + jnp.log(sums[jp])
                lse_tile = jnp.where(lane16 == 2 * jp, lse[:SPAN],
                                     jnp.where(lane16 == 2 * jp + 1, lse[SPAN:], lse_tile))
            lse_ref[pl.ds(row0, SPAN), s * 128:(s + 1) * 128] = lse_tile

        block(0, 0, bias_first)
        if nb > 1:
            def body(i, carry):
                block(pl.multiple_of(i * SPAN, SPAN), pl.multiple_of((i - 1) * SPAN, SPAN), bias_main)
                return carry
            lax.fori_loop(1, nb, body, 0, unroll=min(5, nb - 1))


def _attn(qkv_g):
    bsz, dl, seq, _ = qkv_g.shape
    ns = max(1, min(dl, TM_ATTN // seq))
    return pl.pallas_call(
        functools.partial(_attn_kernel, ns=ns, seq=seq),
        grid=(bsz, dl // ns),
        in_specs=[pl.BlockSpec((None, ns, seq, 3 * COL), lambda b, r: (b, r, 0, 0))],
        out_specs=[pl.BlockSpec((None, seq, ns * B_WIDTH), lambda b, r: (b, 0, r)),
                   pl.BlockSpec((None, seq, ns * 128), lambda b, r: (b, 0, r))],
        out_shape=[jax.ShapeDtypeStruct((bsz, seq, dl * B_WIDTH), BF16),
                   jax.ShapeDtypeStruct((bsz, seq, dl * 128), F32)],
        compiler_params=pltpu.CompilerParams(dimension_semantics=("parallel", "parallel"),
                                             vmem_limit_bytes=VMEM_LIMIT),
        name=f"attn{dl}",
    )(qkv_g)


def _natural_rows(ref, dl, scr):
    nchunk, tm, _ = scr.shape
    w = nchunk * 128
    per = tm // dl
    for r in range(dl):
        for c in range(nchunk):
            scr[c, pl.ds(r, per, stride=dl), :] = ref[:, r * w + c * 128:r * w + (c + 1) * 128].astype(F32)
    return jnp.concatenate([scr[c] for c in range(nchunk)], axis=1)


def _mix_kernel(ga_ref, gates_ref, o1_ref, o2_ref, o3_ref, l1_ref, l2_ref, l3_ref, x_ref,
                wa_ref, wb_ref, wo_ref, wr_ref, br_ref, g1_ref, b1_ref,
                x1_ref, x1p_ref, routet_ref, o2s_ref, o3s_ref, l2s_ref, l3s_ref):
    tm = x_ref.shape[0]
    o2 = _natural_rows(o2_ref, 4, o2s_ref)
    o3 = _natural_rows(o3_ref, 16, o3s_ref)
    l2 = _natural_rows(l2_ref, 4, l2s_ref)
    l3 = _natural_rows(l3_ref, 16, l3s_ref)
    er = lax.broadcasted_iota(I32, (256, B_WIDTH), 0)
    ec = lax.broadcasted_iota(I32, (256, B_WIDTH), 1)
    expand = jnp.where(er % 128 == (ec // B_HEAD_DIM) * 16, 1.0, 0.0).astype(BF16)

    def widen(w):
        hi = w.astype(BF16)
        lo = (w - hi.astype(F32)).astype(BF16)
        return _dot(jnp.concatenate([hi, lo], axis=1), expand)

    h = tm // 2
    halves = (slice(0, h), slice(h, tm))
    obs = []
    for r in halves:
        l1 = l1_ref[r, :]
        mx = jnp.maximum(l1, jnp.maximum(l2[r], l3[r]))
        e1, e2, e3 = jnp.exp(l1 - mx), jnp.exp(l2[r] - mx), jnp.exp(l3[r] - mx)
        inv = 1.0 / (e1 + e2 + e3)
        obs.append(widen(e1 * inv) * o1_ref[r, :].astype(F32) + widen(e2 * inv) * o2[r] + widen(e3 * inv) * o3[r])
    ybs = [_dot(ob.astype(BF16), wb_ref[...]) for ob in obs]
    yas = [_dot(ga_ref[r, :], wa_ref[...]) for r in halves]
    pres = [gates_ref[r, :D_MODEL].astype(F32) * ya + gates_ref[r, D_MODEL:].astype(F32) * yb
            for r, ya, yb in zip(halves, yas, ybs)]
    mixes = [_dot(pre.astype(BF16), wo_ref[...]) for pre in pres]
    x1s = [_ln(DEEPNORM_ALPHA * x_ref[r, :] + mix, g1_ref[...], b1_ref[...]) for r, mix in zip(halves, mixes)]
    logits = [_dot(x1.astype(BF16), wr_ref[...]) + br_ref[...] for x1 in x1s]

    nrow = 40
    row = lax.broadcasted_iota(I32, (nrow, h), 0).astype(F32)
    row8 = lax.broadcasted_iota(I32, (8, h), 0)
    big = 1e9
    for i, r in enumerate(halves):
        x1_ref[r, :] = x1s[i]
        _store_packed_rows(x1p_ref.at[i * h // 8:(i + 1) * h // 8], x1s[i])
        lg = logits[i].T[:nrow, :]
        gl = jnp.where(row < N_GROUPS, lg, NEG)
        gm = jnp.max(gl, axis=0, keepdims=True)
        gidx = jnp.min(jnp.where(gl == gm, row, big), axis=0, keepdims=True)
        gsum = jnp.sum(jnp.where(row < N_GROUPS, jnp.exp(gl - gm), 0.0), axis=0, keepdims=True)
        gprob = 1.0 / gsum
        lo_row = N_GROUPS + N_EXPERTS * gidx
        el = jnp.where((row >= lo_row) & (row < lo_row + N_EXPERTS), lg, NEG)
        v1 = jnp.max(el, axis=0, keepdims=True)
        i1 = jnp.min(jnp.where(el == v1, row, big), axis=0, keepdims=True)
        el2 = jnp.where(row == i1, NEG, el)
        v2 = jnp.max(el2, axis=0, keepdims=True)
        i2 = jnp.min(jnp.where(el2 == v2, row, big), axis=0, keepdims=True)
        t = jnp.exp(v2 - v1)
        w1 = 1.0 / (1.0 + t)
        w2 = t * w1
        routet_ref[:, r] = jnp.where(row8 == 0, i1 - N_GROUPS,
                                     jnp.where(row8 == 1, i2 - N_GROUPS,
                                               jnp.where(row8 == 2, gprob * w1,
                                                         jnp.where(row8 == 3, gprob * w2, 0.0))))


def _mix(ga, gates, o1, o2, o3, l1, l2, l3, x2, wa, wb, wo, wr, br, g1, b1):
    n = x2.shape[0]
    bsz = o2.shape[0]
    tm = TM_MIX
    tiles = n // bsz // tm
    rows = lambda w: pl.BlockSpec((tm, w), lambda i: (i, 0))
    grouped = lambda a, dl: pl.BlockSpec((None, tm // dl, a.shape[2]), lambda i: (i // tiles, i % tiles, 0))
    full = lambda a: pl.BlockSpec(a.shape, lambda i: (0,) * a.ndim)
    return pl.pallas_call(
        _mix_kernel,
        grid=(n // tm,),
        in_specs=[rows(A_WIDTH), rows(2 * D_MODEL), rows(B_WIDTH), grouped(o2, 4), grouped(o3, 16),
                  rows(128), grouped(l2, 4), grouped(l3, 16), rows(D_MODEL),
                  full(wa), full(wb), full(wo), full(wr), full(br), full(g1), full(b1)],
        out_specs=[rows(D_MODEL), pl.BlockSpec((tm // 8, 32, 128), lambda i: (i, 0, 0)),
                   pl.BlockSpec((8, tm), lambda i: (0, i))],
        out_shape=[jax.ShapeDtypeStruct((n, D_MODEL), F32),
                   jax.ShapeDtypeStruct((n // 8, 32, 128), U32),
                   jax.ShapeDtypeStruct((8, n), F32)],
        scratch_shapes=[pltpu.VMEM((B_WIDTH // 128, tm, 128), F32), pltpu.VMEM((B_WIDTH // 128, tm, 128), F32),
                        pltpu.VMEM((1, tm, 128), F32), pltpu.VMEM((1, tm, 128), F32)],
        compiler_params=pltpu.CompilerParams(dimension_semantics=("parallel",), vmem_limit_bytes=VMEM_LIMIT),
        name="mix",
    )(ga, gates, o1, o2, o3, l1, l2, l3, x2, wa, wb, wo, wr, br, g1, b1)


SC_WINDOW = 128


def _sc_mesh():
    return plsc.VectorSubcoreMesh(core_axis_name="core", subcore_axis_name="subcore")


def _sc_scatter_rows(rows, dst, n_out):
    r = rows.shape[0]

    @pl.kernel(out_type=jax.ShapeDtypeStruct((n_out, 128), rows.dtype), mesh=_sc_mesh())
    def scatter(rows_hbm, dst0_hbm, dst1_hbm, out_hbm):
        def body(rows_vmem, dst0_vmem, dst1_vmem):
            pltpu.sync_copy(rows_vmem, out_hbm.at[dst0_vmem.at[0]])
            pltpu.sync_copy(rows_vmem, out_hbm.at[dst1_vmem.at[0]])

        pltpu.emit_pipeline(
            body,
            grid=(r // SC_WINDOW,),
            in_specs=[pl.BlockSpec((SC_WINDOW, 128), lambda i: (i, 0)),
                      pl.BlockSpec((1, SC_WINDOW), lambda i: (0, i)),
                      pl.BlockSpec((1, SC_WINDOW), lambda i: (0, i))],
            out_specs=[],
            core_axis_name=("core", "subcore"),
            dimension_semantics=(pltpu.PARALLEL,),
        )(rows_hbm, dst0_hbm, dst1_hbm)

    return scatter(rows, dst[0:1], dst[1:2])


def _sc_gather_rows(table, src):
    m = src.shape[0]
    k = 2

    @pl.kernel(out_type=jax.ShapeDtypeStruct((m, 128), table.dtype), mesh=_sc_mesh(),
               scratch_types=[pltpu.SemaphoreType.DMA((k,))])
    def gather(table_hbm, src_hbm, out_hbm, sems):
        def body(src_vmem, out_vmem):
            copies = [pltpu.async_copy(table_hbm.at[src_vmem.at[j]], out_vmem.at[pl.ds(j * SC_WINDOW, SC_WINDOW)],
                                       sems.at[j]) for j in range(k)]
            for c in copies:
                c.wait()

        pltpu.emit_pipeline(
            body,
            grid=(m // (k * SC_WINDOW),),
            in_specs=[pl.BlockSpec((k, SC_WINDOW), lambda i: (i, 0))],
            out_specs=[pl.BlockSpec((k * SC_WINDOW, 128), lambda i: (i, 0))],
            core_axis_name=("core", "subcore"),
            dimension_semantics=(pltpu.PARALLEL,),
        )(src_hbm, out_hbm)

    return gather(table, src.reshape(m // SC_WINDOW, SC_WINDOW))


MOE_X_SLOTS = 3
MOE_Y_SLOTS = 2


def _moe_kernel(tiles_ref, xs_hbm, wg_hbm, wu_hbm, wd_hbm, ys_hbm,
                xbuf, ybuf, wgf_ref, wuf_ref, wdf_ref, wgb_ref, wub_ref, wdb_ref, xsem, ysem, wsem, *, nt):
    te_ref, tv_ref, ws_ref, nx_ref = (tiles_ref.at[i] for i in range(4))
    t = pl.program_id(0)
    grp = TM_MOE // 8

    def fetch_x(j):
        return pltpu.make_async_copy(xs_hbm.at[pl.ds(j * grp, grp)], xbuf.at[j % MOE_X_SLOTS],
                                     xsem.at[j % MOE_X_SLOTS])

    def store_y(j):
        return pltpu.make_async_copy(ybuf.at[j % MOE_Y_SLOTS], ys_hbm.at[pl.ds(j * grp, grp)],
                                     ysem.at[j % MOE_Y_SLOTS])

    def fetch_w(e, slot):
        return [pltpu.make_async_copy(w.at[e], buf.at[slot], wsem.at[slot, i])
                for i, (w, buf) in enumerate(((wg_hbm, wgf_ref), (wu_hbm, wuf_ref), (wd_hbm, wdf_ref)))]

    @pl.when(t == 0)
    def _():
        for c in fetch_w(te_ref[0], 0):
            c.start()
        for j in range(MOE_X_SLOTS - 1):
            @pl.when(tv_ref[j] == 1)
            def _():
                fetch_x(j).start()

    ahead = t + (MOE_X_SLOTS - 1)

    @pl.when((ahead < nt) & (tv_ref[jnp.minimum(ahead, nt - 1)] == 1))
    def _():
        fetch_x(ahead).start()

    slot = ws_ref[t]

    @pl.when((t == 0) | (te_ref[t] != te_ref[jnp.maximum(t - 1, 0)]))
    def _():
        @pl.when(nx_ref[t] >= 0)
        def _():
            for c in fetch_w(nx_ref[t], 1 - slot):
                c.start()

        for c in fetch_w(te_ref[t], slot):
            c.wait()
        wgb_ref[...] = wgf_ref[slot].astype(BF16)
        wub_ref[...] = wuf_ref[slot].astype(BF16)
        wdb_ref[...] = wdf_ref[slot].astype(BF16)

    @pl.when((t >= MOE_Y_SLOTS) & (tv_ref[jnp.maximum(t - MOE_Y_SLOTS, 0)] == 1))
    def _():
        store_y(t - MOE_Y_SLOTS).wait()

    @pl.when(tv_ref[t] == 1)
    def _():
        fetch_x(t).wait()
        xb = _load_packed_rows(xbuf.at[t % MOE_X_SLOTS]).astype(BF16)
        g = _dot(xb, wgb_ref[...])
        u = _dot(xb, wub_ref[...])
        h = (g * _sigmoid(g) * u).astype(BF16)
        _store_packed_rows(ybuf.at[t % MOE_Y_SLOTS], _dot(h, wdb_ref[...]))
        store_y(t).start()

    @pl.when(t == nt - 1)
    def _():
        for j in range(nt - MOE_Y_SLOTS, nt):
            @pl.when(tv_ref[j] == 1)
            def _():
                store_y(j).wait()


def _moe(tiles, nt, xs, wg, wu, wd):
    tm = TM_MOE
    any_spec = pl.BlockSpec(memory_space=pl.ANY)
    return pl.pallas_call(
        functools.partial(_moe_kernel, nt=nt),
        grid_spec=pltpu.PrefetchScalarGridSpec(
            num_scalar_prefetch=1,
            grid=(nt,),
            in_specs=[any_spec] * 4,
            out_specs=any_spec,
            scratch_shapes=[pltpu.VMEM((MOE_X_SLOTS, tm // 8, 32, 128), U32),
                            pltpu.VMEM((MOE_Y_SLOTS, tm // 8, 32, 128), U32),
                            pltpu.VMEM((2, D_MODEL, D_EXPERT), F32), pltpu.VMEM((2, D_MODEL, D_EXPERT), F32),
                            pltpu.VMEM((2, D_EXPERT, D_MODEL), F32),
                            pltpu.VMEM((D_MODEL, D_EXPERT), BF16), pltpu.VMEM((D_MODEL, D_EXPERT), BF16),
                            pltpu.VMEM((D_EXPERT, D_MODEL), BF16),
                            pltpu.SemaphoreType.DMA((MOE_X_SLOTS,)), pltpu.SemaphoreType.DMA((MOE_Y_SLOTS,)),
                            pltpu.SemaphoreType.DMA((2, 3))]),
        out_shape=jax.ShapeDtypeStruct((nt * tm // 8, 32, 128), U32),
        compiler_params=pltpu.CompilerParams(dimension_semantics=("arbitrary",), vmem_limit_bytes=VMEM_LIMIT),
        name="moe",
    )(tiles, xs, wg, wu, wd)


def _final_kernel(y0_ref, y1_ref, x1_ref, p_ref, routet_ref, wple_ref, wpg_ref, g2_ref, b2_ref, *rest):
    out_ref = rest[-1]
    tm = x1_ref.shape[0]
    route = routet_ref[...].T
    h = tm // 4
    parts = tuple(slice(i * h, (i + 1) * h) for i in range(4))
    x1s = [x1_ref[r, :] for r in parts]
    plins = [_dot(p_ref[r, :].astype(BF16), wple_ref[...]) for r in parts]
    gates = [_dot(x1.astype(BF16), wpg_ref[...]) for x1 in x1s]
    sums = []
    for i, r in enumerate(parts):
        g8 = slice(i * h // 8, (i + 1) * h // 8)
        sums.append(DEEPNORM_ALPHA * x1s[i] + route[r, 2:3] * _load_packed_rows(y0_ref.at[g8])
                    + route[r, 3:4] * _load_packed_rows(y1_ref.at[g8]))
    for i, r in enumerate(parts):
        ple = plins[i] * _sigmoid(gates[i])
        out_ref[r, :] = _ln(sums[i] + ple, g2_ref[...], b2_ref[...])


def _final(yg, x1, p2, routet, wple, wpg, g2, b2, half, prev):
    n = x1.shape[0]
    tm = TM_FIN
    nt = n // 2 // tm
    off = half * nt
    rows = lambda w: pl.BlockSpec((tm, w), lambda t: (t + off, 0))
    full = lambda a: pl.BlockSpec(a.shape, lambda t: (0,) * a.ndim)
    in_specs = [pl.BlockSpec((tm // 8, 32, 128), lambda t: (t, 0, 0)),
                pl.BlockSpec((tm // 8, 32, 128), lambda t: (t + nt, 0, 0)),
                rows(D_MODEL), rows(PLE_DIM), pl.BlockSpec((8, tm), lambda t: (0, t + off)),
                full(wple), full(wpg), full(g2), full(b2)]
    args = [yg, yg, x1, p2, routet, wple, wpg, g2, b2]
    aliases = {}
    if prev is not None:
        in_specs.append(pl.BlockSpec(memory_space=pl.ANY))
        args.append(prev)
        aliases = {len(args) - 1: 0}
    return pl.pallas_call(
        _final_kernel,
        grid=(nt,),
        in_specs=in_specs,
        out_specs=rows(D_MODEL),
        out_shape=jax.ShapeDtypeStruct((n, D_MODEL), F32),
        input_output_aliases=aliases,
        compiler_params=pltpu.CompilerParams(dimension_semantics=("parallel",), vmem_limit_bytes=VMEM_LIMIT),
        name=f"final{half}",
    )(*args)


def _route_tables_kernel(e_ref, piece_ref, tab_ref):
    r = e_ref.shape[0]
    e = e_ref[...]
    ri = lax.broadcasted_iota(I32, (128, 128), 0)
    ci = lax.broadcasted_iota(I32, (128, 128), 1)
    upper = jnp.where(ri <= ci, 1.0, 0.0).astype(BF16)
    rr = lax.broadcasted_iota(I32, (r, r), 0)
    rc = lax.broadcasted_iota(I32, (r, r), 1)
    below = jnp.where(rc < rr, 1.0, 0.0).astype(BF16)
    lane = lax.broadcasted_iota(I32, (1, 128), 1)

    rank = jnp.zeros((r, 128), F32)
    counts = jnp.zeros((1, 128), F32)
    for x in range(N_EXPERTS_TOTAL):
        m = jnp.where(e == x, 1.0, 0.0)
        pre = _dot(m.astype(BF16), upper)
        tot = jnp.broadcast_to(pre[:, 127:128], (r, 128))
        off = _dot(below, tot.astype(BF16))
        rank = rank + m * (pre + off)
        counts = jnp.where(lane == x, off[r - 1:r, :] + tot[r - 1:r, :], counts)
    padded = jnp.floor((counts + (TM_MOE - 1)) * (1.0 / TM_MOE)) * TM_MOE
    ends = _dot(jnp.broadcast_to(padded, (8, 128)).astype(BF16), upper)[0:1, :]
    offs = ends - padded

    nt = r * 128 // TM_MOE + N_EXPERTS_TOTAL
    ones = jnp.ones((128, 128), BF16)
    lanef = lane.astype(F32)
    diag = ri == ci
    ends_t = ends * (1.0 / TM_MOE)
    ecol = _dot(jnp.where(diag, jnp.broadcast_to(ends_t, (128, 128)), 0.0).astype(BF16), ones)
    passed = jnp.where((ecol <= ci.astype(F32)) & (ri < N_EXPERTS_TOTAL), 1.0, 0.0)
    te = jnp.minimum(jnp.sum(passed, axis=0, keepdims=True), N_EXPERTS_TOTAL - 1.0)
    tv = jnp.where(lanef < ecol[N_EXPERTS_TOTAL - 1:N_EXPERTS_TOTAL, :], 1.0, 0.0)
    shift = jnp.where(ri + 1 == ci, 1.0, 0.0).astype(BF16)
    te_prev = _dot(jnp.broadcast_to(te, (8, 128)).astype(BF16), shift)[0:1, :]
    first = jnp.where((lane == 0) | (te != te_prev), 1.0, 0.0)
    run = _dot(jnp.broadcast_to(first, (8, 128)).astype(BF16), upper)[0:1, :] - 1.0
    ws = run - 2.0 * jnp.floor(run * 0.5)
    tcol = _dot(jnp.where(diag, jnp.broadcast_to(te, (128, 128)), 0.0).astype(BF16), ones)
    later = jnp.where((tcol > te) & (ri < nt), tcol, float(N_EXPERTS_TOTAL))
    nx = jnp.min(later, axis=0, keepdims=True)
    nx = jnp.where(nx == N_EXPERTS_TOTAL, -1.0, nx)
    row8 = lax.broadcasted_iota(I32, (8, 128), 0)
    tab = jnp.where(row8 == 0, te, jnp.where(row8 == 1, tv, jnp.where(row8 == 2, ws, jnp.where(row8 == 3, nx, 0.0))))
    tab_ref[...] = tab.astype(I32)

    pos = rank - 1.0
    for x in range(N_EXPERTS_TOTAL):
        pos = pos + jnp.where(e == x, offs[:, x:x + 1], 0.0)

    hi = jnp.floor(pos * (1.0 / 256.0))
    lo = pos - 256.0 * hi
    jv = ((lane % 32) // 8).astype(F32)
    for c in range(4):
        sel = jnp.where(ri == 32 * c + 8 * (ci // 32) + ci % 8, 1.0, 0.0).astype(BF16)
        pc = 256.0 * _dot(hi.astype(BF16), sel) + _dot(lo.astype(BF16), sel)
        p8 = jnp.floor(pc * 0.125)
        piece = p8 * (8.0 * SUBROWS) + (pc - 8.0 * p8) + 8.0 * jv
        piece_ref[pl.ds(c, r, stride=4), :] = piece.astype(I32)


def _routing_tables(routet, n):
    nt = (2 * n) // TM_MOE + N_EXPERTS_TOTAL
    assert nt <= 128, "the tile tables hold one tile per lane"
    r = 2 * n // 128
    piece, tiles = pl.pallas_call(
        _route_tables_kernel,
        out_shape=[jax.ShapeDtypeStruct((4 * r, 128), I32), jax.ShapeDtypeStruct((8, 128), I32)],
        compiler_params=pltpu.CompilerParams(vmem_limit_bytes=VMEM_LIMIT),
        name="route_tables",
    )(routet[0:2].reshape(r, 128))
    return tiles, nt, piece.reshape(2, n * SUBROWS)


def kernel(x, p, w_in, a_ln_g, a_ln_b, a_ws, a_bs, w_a_proj, w_b_proj, w_o, ln1_g, ln1_b, w_group_router,
           b_group_router, w_expert_router, b_expert_router, w_gate, w_up, w_down, w_ple, w_ple_gate,
           ln2_g, ln2_b):
    bsz, s, d = x.shape
    n = bsz * s
    assert d == D_MODEL and s % (SPAN * max(B_DILATIONS)) == 0 and n % TM_PROJ == 0
    assert w_in.shape[0] == 1, "one layer"

    a_bias = jnp.repeat(a_bs[0].T, A_WIDTH // 8, axis=1)

    ga, gates, qkv1, qkv2, qkv3 = _proj(x, w_in[0], a_ln_g, a_ln_b, a_ws[0], a_bias)
    o1, l1 = _attn(qkv1.reshape(bsz, 1, s, 3 * COL))
    o2, l2 = _attn(qkv2)
    o3, l3 = _attn(qkv3)

    pad = 128 - N_GROUPS - N_EXPERTS_TOTAL
    wr = jnp.concatenate([w_group_router[0], w_expert_router[0].reshape(d, N_EXPERTS_TOTAL),
                          jnp.zeros((d, pad), F32)], axis=1).astype(BF16)
    br = jnp.concatenate([b_group_router[0], b_expert_router[0].reshape(-1), jnp.zeros((pad,), F32)])[None, :]
    x1, x1p, routet = _mix(
        ga, gates, o1.reshape(n, B_WIDTH), o2, o3, l1.reshape(n, 128), l2, l3, x.reshape(n, d),
        w_a_proj[0].astype(BF16), w_b_proj[0].astype(BF16), w_o[0].astype(BF16), wr, br, ln1_g, ln1_b)

    tiles, nt, piece = _routing_tables(routet, n)
    xs = _sc_scatter_rows(x1p.reshape(n * SUBROWS, 128), piece, nt * TM_MOE * SUBROWS)
    ys = _moe(tiles, nt, xs.reshape(nt * TM_MOE // 8, 32, 128),
              w_gate[0].reshape(N_EXPERTS_TOTAL, d, D_EXPERT), w_up[0].reshape(N_EXPERTS_TOTAL, d, D_EXPERT),
              w_down[0].reshape(N_EXPERTS_TOTAL, D_EXPERT, d)).reshape(nt * TM_MOE * SUBROWS, 128)
    out = None
    hp = n * SUBROWS // 2
    for half in range(2):
        yg = _sc_gather_rows(ys, piece[:, half * hp:(half + 1) * hp].reshape(-1))
        out = _final(yg.reshape(n // 8, 32, 128), x1, p[0].reshape(n, PLE_DIM), routet,
                     w_ple[0].astype(BF16), w_ple_gate[0].astype(BF16), ln2_g, ln2_b, half, out)
    return out.reshape(bsz, s, d)
```

```python
import functools

import jax
import jax.numpy as jnp
from jax import lax
from jax.experimental import pallas as pl
from jax.experimental.pallas import tpu as pltpu
from jax.experimental.pallas import tpu_sc as plsc

F32 = jnp.float32
BF16 = jnp.bfloat16
U32 = jnp.uint32
I32 = jnp.int32

D_MODEL = 1024
PLE_DIM = 256
A_WIDTH = 512
A_CHUNK = 128
B_HEAD_DIM = 64
B_HEADS = 8
B_WIDTH = 512
B_DILATIONS = (1, 4, 16)
SPAN = 128
N_GROUPS = 4
N_EXPERTS = 8
N_EXPERTS_TOTAL = N_GROUPS * N_EXPERTS
D_EXPERT = 256
DEEPNORM_ALPHA = 2.0 ** 0.25
LN_EPS = 1e-5
COL = 512
NEG = -1e30

VMEM_LIMIT = 56 * 1024 * 1024

TM_PROJ = 512
TM_ATTN = 1024
TM_MIX = 512
TM_MOE = 512
TM_FIN = 1024


def _ln(x, g, b):
    mu = jnp.mean(x, axis=-1, keepdims=True)
    xc = x - mu
    var = jnp.mean(xc * xc, axis=-1, keepdims=True)
    return xc * lax.rsqrt(var + LN_EPS) * g + b


def _gelu_tanh(x):
    return 0.5 * x * (1.0 + jnp.tanh(0.7978845608028654 * (x + 0.044715 * (x * x * x))))


def _sigmoid(x):
    return 0.5 * jnp.tanh(0.5 * x) + 0.5


def _dot(a, b):
    return jnp.dot(a, b, preferred_element_type=F32)


PACK_W = D_MODEL // 2
SUBROWS = PACK_W // 128


def _store_packed_rows(ref, x):
    m = x.shape[0]
    xb = x.astype(BF16).astype(F32)
    lo = pltpu.bitcast(xb[:, :PACK_W], U32) >> 16
    hi = pltpu.bitcast(xb[:, PACK_W:], U32) & jnp.uint32(0xFFFF0000)
    w = hi | lo
    for j in range(SUBROWS):
        ref[:, 8 * j:8 * (j + 1), :] = w[:, 128 * j:128 * (j + 1)].reshape(m // 8, 8, 128)


def _load_packed_rows(ref):
    m = ref.shape[0] * 8
    ws = [ref[:, 8 * j:8 * (j + 1), :].reshape(m, 128) for j in range(SUBROWS)]
    lo = [pltpu.bitcast(w << 16, F32) for w in ws]
    hi = [pltpu.bitcast(w & jnp.uint32(0xFFFF0000), F32) for w in ws]
    return jnp.concatenate(lo + hi, axis=1)


W_IN_BLOCKS = 15
W_IN_SLOTS = 4


def _proj_kernel(x_ref, w_hbm, lng_ref, lnb_ref, ws_ref, bias_ref,
                 ga_ref, gates_ref, qkv1_ref, qkv2_ref, qkv3_ref, xc_ref, w, wstage_ref, wsem):
    tm = x_ref.shape[0]

    @pl.when(pl.program_id(0) == 0)
    def _():
        def fetch(j):
            return pltpu.make_async_copy(w_hbm.at[:, pl.ds(j * COL, COL)], wstage_ref.at[j % W_IN_SLOTS],
                                         wsem.at[j % W_IN_SLOTS])

        for j in range(W_IN_SLOTS):
            fetch(j).start()
        for j in range(W_IN_BLOCKS):
            fetch(j).wait()
            w[j] = wstage_ref[j % W_IN_SLOTS].astype(BF16)
            if j + W_IN_SLOTS < W_IN_BLOCKS:
                fetch(j + W_IN_SLOTS).start()

    xb = x_ref[...].astype(BF16)

    u_raw = _dot(xb, w[0])
    v_raw = _dot(xb, w[1])

    for i in range(4):
        gates_ref[:, i * COL:(i + 1) * COL] = _sigmoid(_dot(xb, w[11 + i])).astype(BF16)
    for j in range(3):
        qkv1_ref[:, j * COL:(j + 1) * COL] = _dot(xb, w[2 + 3 * j]).astype(BF16)

    for c in range(D_MODEL // 128):
        xc_ref[c] = x_ref[:, c * 128:(c + 1) * 128]
    for gi, out_ref in ((1, qkv2_ref), (2, qkv3_ref)):
        dl = B_DILATIONS[gi]
        per = tm // dl
        xp = jnp.concatenate(
            [jnp.concatenate([xc_ref[c, pl.ds(r, per, stride=dl), :] for c in range(D_MODEL // 128)], axis=1)
             for r in range(dl)], axis=0).astype(BF16)
        for j in range(3):
            res = _dot(xp, w[2 + 3 * j + gi]).astype(BF16)
            for r in range(dl):
                out_ref[r, :, j * COL:(j + 1) * COL] = res[r * per:(r + 1) * per]

    u = _gelu_tanh(u_raw)
    v = _gelu_tanh(v_raw)
    vn = _ln(v, lng_ref[...], lnb_ref[...]).astype(BF16)

    row = lax.broadcasted_iota(I32, (A_CHUNK, A_CHUNK), 0)
    colm = lax.broadcasted_iota(I32, (A_CHUNK, A_CHUNK), 1)
    causal = colm <= row
    lo = colm < 64
    zero = jnp.zeros((A_CHUNK, A_CHUNK), BF16)
    wcat = []
    for j in range(4):
        w0 = jnp.where(causal, ws_ref[2 * j], 0.0).astype(BF16)
        w1 = jnp.where(causal, ws_ref[2 * j + 1], 0.0).astype(BF16)
        wcat.append(jnp.concatenate([w0, w1], axis=1))
    for c in range(tm // A_CHUNK):
        r0 = c * A_CHUNK
        for j in range(4):
            c0 = j * 128
            vt = vn[r0:r0 + A_CHUNK, c0:c0 + 128]
            rhs = jnp.concatenate([jnp.where(lo, vt, zero), jnp.where(lo, zero, vt)], axis=0)
            mixed = _dot(wcat[j], rhs) + bias_ref[:, c0:c0 + 128]
            ga_ref[r0:r0 + A_CHUNK, c0:c0 + 128] = (u[r0:r0 + A_CHUNK, c0:c0 + 128] * mixed).astype(BF16)


def _proj(x, w_in, a_ln_g, a_ln_b, a_ws, a_bias):
    bsz, s, _ = x.shape
    n = bsz * s
    tm = TM_PROJ
    tiles = s // tm
    x2 = x.reshape(n, D_MODEL)
    full = lambda shape: pl.BlockSpec(shape, lambda i: (0,) * len(shape))
    rows = lambda width: pl.BlockSpec((tm, width), lambda i: (i, 0))
    dil = lambda dl: pl.BlockSpec((None, dl, tm // dl, 3 * COL), lambda i: (i // tiles, 0, i % tiles, 0))
    return pl.pallas_call(
        _proj_kernel,
        grid=(n // tm,),
        in_specs=[rows(D_MODEL), pl.BlockSpec(memory_space=pl.ANY)]
                 + [full((1, A_WIDTH)), full((1, A_WIDTH)), full((8, A_CHUNK, A_CHUNK)), full((A_CHUNK, A_WIDTH))],
        out_specs=[rows(A_WIDTH), rows(4 * COL), rows(3 * COL), dil(4), dil(16)],
        out_shape=[jax.ShapeDtypeStruct((n, A_WIDTH), BF16),
                   jax.ShapeDtypeStruct((n, 4 * COL), BF16),
                   jax.ShapeDtypeStruct((n, 3 * COL), BF16),
                   jax.ShapeDtypeStruct((bsz, 4, s // 4, 3 * COL), BF16),
                   jax.ShapeDtypeStruct((bsz, 16, s // 16, 3 * COL), BF16)],
        scratch_shapes=[pltpu.VMEM((D_MODEL // 128, tm, 128), F32),
                        pltpu.VMEM((W_IN_BLOCKS, D_MODEL, COL), BF16),
                        pltpu.VMEM((W_IN_SLOTS, D_MODEL, COL), F32),
                        pltpu.SemaphoreType.DMA((W_IN_SLOTS,))],
        compiler_params=pltpu.CompilerParams(dimension_semantics=("arbitrary",), vmem_limit_bytes=VMEM_LIMIT),
        name="proj",
    )(x2, w_in, a_ln_g, a_ln_b, a_ws, a_bias)


def _attn_kernel(qkv_ref, o_ref, lse_ref, *, ns, seq):
    nb = seq // SPAN
    lane = lax.broadcasted_iota(I32, (SPAN, 128), 1)
    lo = lane < 64
    lane16 = lane // 16
    qi = lax.broadcasted_iota(I32, (SPAN, 2 * SPAN), 0)
    ki = lax.broadcasted_iota(I32, (SPAN, 2 * SPAN), 1)
    causal = lax.broadcasted_iota(I32, (SPAN, SPAN), 1) <= lax.broadcasted_iota(I32, (SPAN, SPAN), 0)
    bias_first = jnp.where(causal, 0.0, NEG).astype(F32)
    bias_first = jnp.concatenate([bias_first, bias_first], axis=0)
    bias_main = jnp.where((ki >= qi) & (ki <= qi + SPAN), 0.0, NEG).astype(F32)
    bias_main = jnp.concatenate([bias_main, bias_main], axis=0)
    zero = jnp.zeros((SPAN, 128), BF16)

    for s in range(ns):
        def block(row0, start, bias, s=s):
            win = bias.shape[1]
            pairs = range(B_HEADS // 2)
            scores, values = [], []
            for jp in pairs:
                c0 = jp * 128
                q = qkv_ref[s, pl.ds(row0, SPAN), c0:c0 + 128] * jnp.asarray(0.125, BF16)
                k = qkv_ref[s, pl.ds(start, win), COL + c0:COL + c0 + 128]
                values.append(qkv_ref[s, pl.ds(start, win), 2 * COL + c0:2 * COL + c0 + 128])
                qs = jnp.concatenate([jnp.where(lo, q, zero), jnp.where(lo, zero, q)], axis=0)
                scores.append(lax.dot_general(qs, k, (((1,), (1,)), ((), ())), preferred_element_type=F32) + bias)
            probs, maxes, sums = [], [], []
            for jp in pairs:
                m = jnp.max(scores[jp], axis=-1, keepdims=True)
                p = jnp.exp(scores[jp] - m)
                maxes.append(m)
                sums.append(jnp.sum(p, axis=-1, keepdims=True))
                probs.append(p.astype(BF16))
            lse_tile = jnp.zeros((SPAN, 128), F32)
            for jp in pairs:
                c0 = jp * 128
                ov = _dot(probs[jp], values[jp])
                inv = 1.0 / sums[jp]
                o = jnp.where(lo, ov[:SPAN] * inv[:SPAN], ov[SPAN:] * inv[SPAN:])
                o_ref[pl.ds(row0, SPAN), s * B_WIDTH + c0:s * B_WIDTH + c0 + 128] = o.astype(BF16)
                lse = maxes[jp] + jnp.log(sums[jp])
                lse_tile = jnp.where(lane16 == 2 * jp, lse[:SPAN],
                                     jnp.where(lane16 == 2 * jp + 1, lse[SPAN:], lse_tile))
            lse_ref[pl.ds(row0, SPAN), s * 128:(s + 1) * 128] = lse_tile

        block(0, 0, bias_first)
        if nb > 1:
            def body(i, carry):
                block(pl.multiple_of(i * SPAN, SPAN), pl.multiple_of((i - 1) * SPAN, SPAN), bias_main)
                return carry
            lax.fori_loop(1, nb, body, 0, unroll=min(5, nb - 1))


def _attn(qkv_g):
    bsz, dl, seq, _ = qkv_g.shape
    ns = max(1, min(dl, TM_ATTN // seq))
    return pl.pallas_call(
        functools.partial(_attn_kernel, ns=ns, seq=seq),
        grid=(bsz, dl // ns),
        in_specs=[pl.BlockSpec((None, ns, seq, 3 * COL), lambda b, r: (b, r, 0, 0))],
        out_specs=[pl.BlockSpec((None, seq, ns * B_WIDTH), lambda b, r: (b, 0, r)),
                   pl.BlockSpec((None, seq, ns * 128), lambda b, r: (b, 0, r))],
        out_shape=[jax.ShapeDtypeStruct((bsz, seq, dl * B_WIDTH), BF16),
                   jax.ShapeDtypeStruct((bsz, seq, dl * 128), F32)],
        compiler_params=pltpu.CompilerParams(dimension_semantics=("parallel", "parallel"),
                                             vmem_limit_bytes=VMEM_LIMIT),
        name=f"attn{dl}",
    )(qkv_g)


def _natural_rows(ref, dl, scr):
    nchunk, tm, _ = scr.shape
    w = nchunk * 128
    per = tm // dl
    for r in range(dl):
        for c in range(nchunk):
            scr[c, pl.ds(r, per, stride=dl), :] = ref[:, r * w + c * 128:r * w + (c + 1) * 128].astype(F32)
    return jnp.concatenate([scr[c] for c in range(nchunk)], axis=1)


def _mix_kernel(ga_ref, gates_ref, o1_ref, o2_ref, o3_ref, l1_ref, l2_ref, l3_ref, x_ref,
                wa_ref, wb_ref, wo_ref, wr_ref, br_ref, g1_ref, b1_ref,
                x1_ref, x1p_ref, routet_ref, o2s_ref, o3s_ref, l2s_ref, l3s_ref):
    tm = x_ref.shape[0]
    o2 = _natural_rows(o2_ref, 4, o2s_ref)
    o3 = _natural_rows(o3_ref, 16, o3s_ref)
    l2 = _natural_rows(l2_ref, 4, l2s_ref)
    l3 = _natural_rows(l3_ref, 16, l3s_ref)
    er = lax.broadcasted_iota(I32, (256, B_WIDTH), 0)
    ec = lax.broadcasted_iota(I32, (256, B_WIDTH), 1)
    expand = jnp.where(er % 128 == (ec // B_HEAD_DIM) * 16, 1.0, 0.0).astype(BF16)

    def widen(w):
        hi = w.astype(BF16)
        lo = (w - hi.astype(F32)).astype(BF16)
        return _dot(jnp.concatenate([hi, lo], axis=1), expand)

    h = tm // 2
    halves = (slice(0, h), slice(h, tm))
    obs = []
    for r in halves:
        l1 = l1_ref[r, :]
        mx = jnp.maximum(l1, jnp.maximum(l2[r], l3[r]))
        e1, e2, e3 = jnp.exp(l1 - mx), jnp.exp(l2[r] - mx), jnp.exp(l3[r] - mx)
        inv = 1.0 / (e1 + e2 + e3)
        obs.append(widen(e1 * inv) * o1_ref[r, :].astype(F32) + widen(e2 * inv) * o2[r] + widen(e3 * inv) * o3[r])
    ybs = [_dot(ob.astype(BF16), wb_ref[...]) for ob in obs]
    yas = [_dot(ga_ref[r, :], wa_ref[...]) for r in halves]
    pres = [gates_ref[r, :D_MODEL].astype(F32) * ya + gates_ref[r, D_MODEL:].astype(F32) * yb
            for r, ya, yb in zip(halves, yas, ybs)]
    mixes = [_dot(pre.astype(BF16), wo_ref[...]) for pre in pres]
    x1s = [_ln(DEEPNORM_ALPHA * x_ref[r, :] + mix, g1_ref[...], b1_ref[...]) for r, mix in zip(halves, mixes)]
    logits = [_dot(x1.astype(BF16), wr_ref[...]) + br_ref[...] for x1 in x1s]

    nrow = 40
    row = lax.broadcasted_iota(I32, (nrow, h), 0).astype(F32)
    row8 = lax.broadcasted_iota(I32, (8, h), 0)
    big = 1e9
    for i, r in enumerate(halves):
        x1_ref[r, :] = x1s[i]
        _store_packed_rows(x1p_ref.at[i * h // 8:(i + 1) * h // 8], x1s[i])
        lg = logits[i].T[:nrow, :]
        gl = jnp.where(row < N_GROUPS, lg, NEG)
        gm = jnp.max(gl, axis=0, keepdims=True)
        gidx = jnp.min(jnp.where(gl == gm, row, big), axis=0, keepdims=True)
        gsum = jnp.sum(jnp.where(row < N_GROUPS, jnp.exp(gl - gm), 0.0), axis=0, keepdims=True)
        gprob = 1.0 / gsum
        lo_row = N_GROUPS + N_EXPERTS * gidx
        el = jnp.where((row >= lo_row) & (row < lo_row + N_EXPERTS), lg, NEG)
        v1 = jnp.max(el, axis=0, keepdims=True)
        i1 = jnp.min(jnp.where(el == v1, row, big), axis=0, keepdims=True)
        el2 = jnp.where(row == i1, NEG, el)
        v2 = jnp.max(el2, axis=0, keepdims=True)
        i2 = jnp.min(jnp.where(el2 == v2, row, big), axis=0, keepdims=True)
        t = jnp.exp(v2 - v1)
        w1 = 1.0 / (1.0 + t)
        w2 = t * w1
        routet_ref[:, r] = jnp.where(row8 == 0, i1 - N_GROUPS,
                                     jnp.where(row8 == 1, i2 - N_GROUPS,
                                               jnp.where(row8 == 2, gprob * w1,
                                                         jnp.where(row8 == 3, gprob * w2, 0.0))))


def _mix(ga, gates, o1, o2, o3, l1, l2, l3, x2, wa, wb, wo, wr, br, g1, b1):
    n = x2.shape[0]
    bsz = o2.shape[0]
    tm = TM_MIX
    tiles = n // bsz // tm
    rows = lambda w: pl.BlockSpec((tm, w), lambda i: (i, 0))
    grouped = lambda a, dl: pl.BlockSpec((None, tm // dl, a.shape[2]), lambda i: (i // tiles, i % tiles, 0))
    full = lambda a: pl.BlockSpec(a.shape, lambda i: (0,) * a.ndim)
    return pl.pallas_call(
        _mix_kernel,
        grid=(n // tm,),
        in_specs=[rows(A_WIDTH), rows(2 * D_MODEL), rows(B_WIDTH), grouped(o2, 4), grouped(o3, 16),
                  rows(128), grouped(l2, 4), grouped(l3, 16), rows(D_MODEL),
                  full(wa), full(wb), full(wo), full(wr), full(br), full(g1), full(b1)],
        out_specs=[rows(D_MODEL), pl.BlockSpec((tm // 8, 32, 128), lambda i: (i, 0, 0)),
                   pl.BlockSpec((8, tm), lambda i: (0, i))],
        out_shape=[jax.ShapeDtypeStruct((n, D_MODEL), F32),
                   jax.ShapeDtypeStruct((n // 8, 32, 128), U32),
                   jax.ShapeDtypeStruct((8, n), F32)],
        scratch_shapes=[pltpu.VMEM((B_WIDTH // 128, tm, 128), F32), pltpu.VMEM((B_WIDTH // 128, tm, 128), F32),
                        pltpu.VMEM((1, tm, 128), F32), pltpu.VMEM((1, tm, 128), F32)],
        compiler_params=pltpu.CompilerParams(dimension_semantics=("parallel",), vmem_limit_bytes=VMEM_LIMIT),
        name="mix",
    )(ga, gates, o1, o2, o3, l1, l2, l3, x2, wa, wb, wo, wr, br, g1, b1)


SC_WINDOW = 128


def _sc_mesh():
    return plsc.VectorSubcoreMesh(core_axis_name="core", subcore_axis_name="subcore")


def _sc_scatter_rows(rows, dst, n_out):
    r = rows.shape[0]

    @pl.kernel(out_type=jax.ShapeDtypeStruct((n_out, 128), rows.dtype), mesh=_sc_mesh())
    def scatter(rows_hbm, dst0_hbm, dst1_hbm, out_hbm):
        def body(rows_vmem, dst0_vmem, dst1_vmem):
            pltpu.sync_copy(rows_vmem, out_hbm.at[dst0_vmem.at[0]])
            pltpu.sync_copy(rows_vmem, out_hbm.at[dst1_vmem.at[0]])

        pltpu.emit_pipeline(
            body,
            grid=(r // SC_WINDOW,),
            in_specs=[pl.BlockSpec((SC_WINDOW, 128), lambda i: (i, 0)),
                      pl.BlockSpec((1, SC_WINDOW), lambda i: (0, i)),
                      pl.BlockSpec((1, SC_WINDOW), lambda i: (0, i))],
            out_specs=[],
            core_axis_name=("core", "subcore"),
            dimension_semantics=(pltpu.PARALLEL,),
        )(rows_hbm, dst0_hbm, dst1_hbm)

    return scatter(rows, dst[0:1], dst[1:2])


def _sc_gather_rows(table, src):
    m = src.shape[0]
    k = 2

    @pl.kernel(out_type=jax.ShapeDtypeStruct((m, 128), table.dtype), mesh=_sc_mesh(),
               scratch_types=[pltpu.SemaphoreType.DMA((k,))])
    def gather(table_hbm, src_hbm, out_hbm, sems):
        def body(src_vmem, out_vmem):
            copies = [pltpu.async_copy(table_hbm.at[src_vmem.at[j]], out_vmem.at[pl.ds(j * SC_WINDOW, SC_WINDOW)],
                                       sems.at[j]) for j in range(k)]
            for c in copies:
                c.wait()

        pltpu.emit_pipeline(
            body,
            grid=(m // (k * SC_WINDOW),),
            in_specs=[pl.BlockSpec((k, SC_WINDOW), lambda i: (i, 0))],
            out_specs=[pl.BlockSpec((k * SC_WINDOW, 128), lambda i: (i, 0))],
            core_axis_name=("core", "subcore"),
            dimension_semantics=(pltpu.PARALLEL,),
        )(src_hbm, out_hbm)

    return gather(table, src.reshape(m // SC_WINDOW, SC_WINDOW))


MOE_X_SLOTS = 3
MOE_Y_SLOTS = 2


def _moe_kernel(tiles_ref, xs_hbm, wg_hbm, wu_hbm, wd_hbm, ys_hbm,
                xbuf, ybuf, wgf_ref, wuf_ref, wdf_ref, wgb_ref, wub_ref, wdb_ref, xsem, ysem, wsem, *, nt):
    te_ref, tv_ref, ws_ref, nx_ref = (tiles_ref.at[i] for i in range(4))
    t = pl.program_id(0)
    grp = TM_MOE // 8

    def fetch_x(j):
        return pltpu.make_async_copy(xs_hbm.at[pl.ds(j * grp, grp)], xbuf.at[j % MOE_X_SLOTS],
                                     xsem.at[j % MOE_X_SLOTS])

    def store_y(j):
        return pltpu.make_async_copy(ybuf.at[j % MOE_Y_SLOTS], ys_hbm.at[pl.ds(j * grp, grp)],
                                     ysem.at[j % MOE_Y_SLOTS])

    def fetch_w(e, slot):
        return [pltpu.make_async_copy(w.at[e], buf.at[slot], wsem.at[slot, i])
                for i, (w, buf) in enumerate(((wg_hbm, wgf_ref), (wu_hbm, wuf_ref), (wd_hbm, wdf_ref)))]

    @pl.when(t == 0)
    def _():
        for c in fetch_w(te_ref[0], 0):
            c.start()
        for j in range(MOE_X_SLOTS - 1):
            @pl.when(tv_ref[j] == 1)
            def _():
                fetch_x(j).start()

    ahead = t + (MOE_X_SLOTS - 1)

    @pl.when((ahead < nt) & (tv_ref[jnp.minimum(ahead, nt - 1)] == 1))
    def _():
        fetch_x(ahead).start()

    slot = ws_ref[t]

    @pl.when((t == 0) | (te_ref[t] != te_ref[jnp.maximum(t - 1, 0)]))
    def _():
        @pl.when(nx_ref[t] >= 0)
        def _():
            for c in fetch_w(nx_ref[t], 1 - slot):
                c.start()

        for c in fetch_w(te_ref[t], slot):
            c.wait()
        wgb_ref[...] = wgf_ref[slot].astype(BF16)
        wub_ref[...] = wuf_ref[slot].astype(BF16)
        wdb_ref[...] = wdf_ref[slot].astype(BF16)

    @pl.when((t >= MOE_Y_SLOTS) & (tv_ref[jnp.maximum(t - MOE_Y_SLOTS, 0)] == 1))
    def _():
        store_y(t - MOE_Y_SLOTS).wait()

    @pl.when(tv_ref[t] == 1)
    def _():
        fetch_x(t).wait()
        xb = _load_packed_rows(xbuf.at[t % MOE_X_SLOTS]).astype(BF16)
        g = _dot(xb, wgb_ref[...])
        u = _dot(xb, wub_ref[...])
        h = (g * _sigmoid(g) * u).astype(BF16)
        _store_packed_rows(ybuf.at[t % MOE_Y_SLOTS], _dot(h, wdb_ref[...]))
        store_y(t).start()

    @pl.when(t == nt - 1)
    def _():
        for j in range(nt - MOE_Y_SLOTS, nt):
            @pl.when(tv_ref[j] == 1)
            def _():
                store_y(j).wait()


def _moe(tiles, nt, xs, wg, wu, wd):
    tm = TM_MOE
    any_spec = pl.BlockSpec(memory_space=pl.ANY)
    return pl.pallas_call(
        functools.partial(_moe_kernel, nt=nt),
        grid_spec=pltpu.PrefetchScalarGridSpec(
            num_scalar_prefetch=1,
            grid=(nt,),
            in_specs=[any_spec] * 4,
            out_specs=any_spec,
            scratch_shapes=[pltpu.VMEM((MOE_X_SLOTS, tm // 8, 32, 128), U32),
                            pltpu.VMEM((MOE_Y_SLOTS, tm // 8, 32, 128), U32),
                            pltpu.VMEM((2, D_MODEL, D_EXPERT), F32), pltpu.VMEM((2, D_MODEL, D_EXPERT), F32),
                            pltpu.VMEM((2, D_EXPERT, D_MODEL), F32),
                            pltpu.VMEM((D_MODEL, D_EXPERT), BF16), pltpu.VMEM((D_MODEL, D_EXPERT), BF16),
                            pltpu.VMEM((D_EXPERT, D_MODEL), BF16),
                            pltpu.SemaphoreType.DMA((MOE_X_SLOTS,)), pltpu.SemaphoreType.DMA((MOE_Y_SLOTS,)),
                            pltpu.SemaphoreType.DMA((2, 3))]),
        out_shape=jax.ShapeDtypeStruct((nt * tm // 8, 32, 128), U32),
        compiler_params=pltpu.CompilerParams(dimension_semantics=("arbitrary",), vmem_limit_bytes=VMEM_LIMIT),
        name="moe",
    )(tiles, xs, wg, wu, wd)


def _final_kernel(y0_ref, y1_ref, x1_ref, p_ref, routet_ref, wple_ref, wpg_ref, g2_ref, b2_ref, *rest):
    out_ref = rest[-1]
    tm = x1_ref.shape[0]
    route = routet_ref[...].T
    h = tm // 4
    parts = tuple(slice(i * h, (i + 1) * h) for i in range(4))
    x1s = [x1_ref[r, :] for r in parts]
    plins = [_dot(p_ref[r, :].astype(BF16), wple_ref[...]) for r in parts]
    gates = [_dot(x1.astype(BF16), wpg_ref[...]) for x1 in x1s]
    sums = []
    for i, r in enumerate(parts):
        g8 = slice(i * h // 8, (i + 1) * h // 8)
        sums.append(DEEPNORM_ALPHA * x1s[i] + route[r, 2:3] * _load_packed_rows(y0_ref.at[g8])
                    + route[r, 3:4] * _load_packed_rows(y1_ref.at[g8]))
    for i, r in enumerate(parts):
        ple = plins[i] * _sigmoid(gates[i])
        out_ref[r, :] = _ln(sums[i] + ple, g2_ref[...], b2_ref[...])


COMBINE_PARTS = 1


def _final(yg, x1, p2, routet, wple, wpg, g2, b2, half, prev):
    n = x1.shape[0]
    tm = TM_FIN
    nt = n // COMBINE_PARTS // tm
    off = half * nt
    rows = lambda w: pl.BlockSpec((tm, w), lambda t: (t + off, 0))
    full = lambda a: pl.BlockSpec(a.shape, lambda t: (0,) * a.ndim)
    in_specs = [pl.BlockSpec((tm // 8, 32, 128), lambda t: (t, 0, 0)),
                pl.BlockSpec((tm // 8, 32, 128), lambda t: (t + nt, 0, 0)),
                rows(D_MODEL), rows(PLE_DIM), pl.BlockSpec((8, tm), lambda t: (0, t + off)),
                full(wple), full(wpg), full(g2), full(b2)]
    args = [yg, yg, x1, p2, routet, wple, wpg, g2, b2]
    aliases = {}
    if prev is not None:
        in_specs.append(pl.BlockSpec(memory_space=pl.ANY))
        args.append(prev)
        aliases = {len(args) - 1: 0}
    return pl.pallas_call(
        _final_kernel,
        grid=(nt,),
        in_specs=in_specs,
        out_specs=rows(D_MODEL),
        out_shape=jax.ShapeDtypeStruct((n, D_MODEL), F32),
        input_output_aliases=aliases,
        compiler_params=pltpu.CompilerParams(dimension_semantics=("parallel",), vmem_limit_bytes=VMEM_LIMIT),
        name=f"final{half}",
    )(*args)


def _route_tables_kernel(e_ref, piece_ref, tab_ref):
    r = e_ref.shape[0]
    e = e_ref[...]
    ri = lax.broadcasted_iota(I32, (128, 128), 0)
    ci = lax.broadcasted_iota(I32, (128, 128), 1)
    upper = jnp.where(ri <= ci, 1.0, 0.0).astype(BF16)
    rr = lax.broadcasted_iota(I32, (r, r), 0)
    rc = lax.broadcasted_iota(I32, (r, r), 1)
    below = jnp.where(rc < rr, 1.0, 0.0).astype(BF16)
    lane = lax.broadcasted_iota(I32, (1, 128), 1)

    rank = jnp.zeros((r, 128), F32)
    counts = jnp.zeros((1, 128), F32)
    for x in range(N_EXPERTS_TOTAL):
        m = jnp.where(e == x, 1.0, 0.0)
        pre = _dot(m.astype(BF16), upper)
        tot = jnp.broadcast_to(pre[:, 127:128], (r, 128))
        off = _dot(below, tot.astype(BF16))
        rank = rank + m * (pre + off)
        counts = jnp.where(lane == x, off[r - 1:r, :] + tot[r - 1:r, :], counts)
    padded = jnp.floor((counts + (TM_MOE - 1)) * (1.0 / TM_MOE)) * TM_MOE
    ends = _dot(jnp.broadcast_to(padded, (8, 128)).astype(BF16), upper)[0:1, :]
    offs = ends - padded

    nt = r * 128 // TM_MOE + N_EXPERTS_TOTAL
    ones = jnp.ones((128, 128), BF16)
    lanef = lane.astype(F32)
    diag = ri == ci
    ends_t = ends * (1.0 / TM_MOE)
    ecol = _dot(jnp.where(diag, jnp.broadcast_to(ends_t, (128, 128)), 0.0).astype(BF16), ones)
    passed = jnp.where((ecol <= ci.astype(F32)) & (ri < N_EXPERTS_TOTAL), 1.0, 0.0)
    te = jnp.minimum(jnp.sum(passed, axis=0, keepdims=True), N_EXPERTS_TOTAL - 1.0)
    tv = jnp.where(lanef < ecol[N_EXPERTS_TOTAL - 1:N_EXPERTS_TOTAL, :], 1.0, 0.0)
    shift = jnp.where(ri + 1 == ci, 1.0, 0.0).astype(BF16)
    te_prev = _dot(jnp.broadcast_to(te, (8, 128)).astype(BF16), shift)[0:1, :]
    first = jnp.where((lane == 0) | (te != te_prev), 1.0, 0.0)
    run = _dot(jnp.broadcast_to(first, (8, 128)).astype(BF16), upper)[0:1, :] - 1.0
    ws = run - 2.0 * jnp.floor(run * 0.5)
    tcol = _dot(jnp.where(diag, jnp.broadcast_to(te, (128, 128)), 0.0).astype(BF16), ones)
    later = jnp.where((tcol > te) & (ri < nt), tcol, float(N_EXPERTS_TOTAL))
    nx = jnp.min(later, axis=0, keepdims=True)
    nx = jnp.where(nx == N_EXPERTS_TOTAL, -1.0, nx)
    row8 = lax.broadcasted_iota(I32, (8, 128), 0)
    tab = jnp.where(row8 == 0, te, jnp.where(row8 == 1, tv, jnp.where(row8 == 2, ws, jnp.where(row8 == 3, nx, 0.0))))
    tab_ref[...] = tab.astype(I32)

    pos = rank - 1.0
    for x in range(N_EXPERTS_TOTAL):
        pos = pos + jnp.where(e == x, offs[:, x:x + 1], 0.0)

    hi = jnp.floor(pos * (1.0 / 256.0))
    lo = pos - 256.0 * hi
    jv = ((lane % 32) // 8).astype(F32)
    for c in range(4):
        sel = jnp.where(ri == 32 * c + 8 * (ci // 32) + ci % 8, 1.0, 0.0).astype(BF16)
        pc = 256.0 * _dot(hi.astype(BF16), sel) + _dot(lo.astype(BF16), sel)
        p8 = jnp.floor(pc * 0.125)
        piece = p8 * (8.0 * SUBROWS) + (pc - 8.0 * p8) + 8.0 * jv
        piece_ref[pl.ds(c, r, stride=4), :] = piece.astype(I32)


def _routing_tables(routet, n):
    nt = (2 * n) // TM_MOE + N_EXPERTS_TOTAL
    assert nt <= 128, "the tile tables hold one tile per lane"
    r = 2 * n // 128
    piece, tiles = pl.pallas_call(
        _route_tables_kernel,
        out_shape=[jax.ShapeDtypeStruct((4 * r, 128), I32), jax.ShapeDtypeStruct((8, 128), I32)],
        compiler_params=pltpu.CompilerParams(vmem_limit_bytes=VMEM_LIMIT),
        name="route_tables",
    )(routet[0:2].reshape(r, 128))
    return tiles, nt, piece.reshape(2, n * SUBROWS)


def kernel(x, p, w_in, a_ln_g, a_ln_b, a_ws, a_bs, w_a_proj, w_b_proj, w_o, ln1_g, ln1_b, w_group_router,
           b_group_router, w_expert_router, b_expert_router, w_gate, w_up, w_down, w_ple, w_ple_gate,
           ln2_g, ln2_b):
    bsz, s, d = x.shape
    n = bsz * s
    assert d == D_MODEL and s % (SPAN * max(B_DILATIONS)) == 0 and n % TM_PROJ == 0
    assert w_in.shape[0] == 1, "one layer"

    a_bias = jnp.repeat(a_bs[0].T, A_WIDTH // 8, axis=1)

    ga, gates, qkv1, qkv2, qkv3 = _proj(x, w_in[0], a_ln_g, a_ln_b, a_ws[0], a_bias)
    o1, l1 = _attn(qkv1.reshape(bsz, 1, s, 3 * COL))
    o2, l2 = _attn(qkv2)
    o3, l3 = _attn(qkv3)

    pad = 128 - N_GROUPS - N_EXPERTS_TOTAL
    wr = jnp.concatenate([w_group_router[0], w_expert_router[0].reshape(d, N_EXPERTS_TOTAL),
                          jnp.zeros((d, pad), F32)], axis=1).astype(BF16)
    br = jnp.concatenate([b_group_router[0], b_expert_router[0].reshape(-1), jnp.zeros((pad,), F32)])[None, :]
    x1, x1p, routet = _mix(
        ga, gates, o1.reshape(n, B_WIDTH), o2, o3, l1.reshape(n, 128), l2, l3, x.reshape(n, d),
        w_a_proj[0].astype(BF16), w_b_proj[0].astype(BF16), w_o[0].astype(BF16), wr, br, ln1_g, ln1_b)

    tiles, nt, piece = _routing_tables(routet, n)
    xs = _sc_scatter_rows(x1p.reshape(n * SUBROWS, 128), piece, nt * TM_MOE * SUBROWS)
    ys = _moe(tiles, nt, xs.reshape(nt * TM_MOE // 8, 32, 128),
              w_gate[0].reshape(N_EXPERTS_TOTAL, d, D_EXPERT), w_up[0].reshape(N_EXPERTS_TOTAL, d, D_EXPERT),
              w_down[0].reshape(N_EXPERTS_TOTAL, D_EXPERT, d)).reshape(nt * TM_MOE * SUBROWS, 128)
    out = None
    hp = n * SUBROWS // COMBINE_PARTS
    for half in range(COMBINE_PARTS):
        yg = _sc_gather_rows(ys, piece[:, half * hp:(half + 1) * hp].reshape(-1))
        out = _final(yg.reshape(2 * n // COMBINE_PARTS // 8, 32, 128), x1, p[0].reshape(n, PLE_DIM), routet,
                     w_ple[0].astype(BF16), w_ple_gate[0].astype(BF16), ln2_g, ln2_b, half, out)
    return out.reshape(bsz, s, d)
```

```python
import functools

import jax
import jax.numpy as jnp
from jax import lax
from jax.experimental import pallas as pl
from jax.experimental.pallas import tpu as pltpu
from jax.experimental.pallas import tpu_sc as plsc

F32 = jnp.float32
BF16 = jnp.bfloat16
U32 = jnp.uint32
I32 = jnp.int32

D_MODEL = 1024
PLE_DIM = 256
A_WIDTH = 512
A_CHUNK = 128
B_HEAD_DIM = 64
B_HEADS = 8
B_WIDTH = 512
B_DILATIONS = (1, 4, 16)
SPAN = 128
N_GROUPS = 4
N_EXPERTS = 8
N_EXPERTS_TOTAL = N_GROUPS * N_EXPERTS
D_EXPERT = 256
DEEPNORM_ALPHA = 2.0 ** 0.25
LN_EPS = 1e-5
COL = 512
NEG = -1e30

VMEM_LIMIT = 56 * 1024 * 1024

TM_PROJ = 512
TM_ATTN = 1024
TM_MIX = 512
TM_MOE = 512
TM_FIN = 1024


def _ln(x, g, b):
    mu = jnp.mean(x, axis=-1, keepdims=True)
    xc = x - mu
    var = jnp.mean(xc * xc, axis=-1, keepdims=True)
    return xc * lax.rsqrt(var + LN_EPS) * g + b


def _gelu_tanh(x):
    return 0.5 * x * (1.0 + jnp.tanh(0.7978845608028654 * (x + 0.044715 * (x * x * x))))


def _sigmoid(x):
    return 0.5 * jnp.tanh(0.5 * x) + 0.5


def _dot(a, b):
    return jnp.dot(a, b, preferred_element_type=F32)


PACK_W = D_MODEL // 2
SUBROWS = PACK_W // 128


def _store_packed_rows(ref, x):
    m = x.shape[0]
    xb = x.astype(BF16).astype(F32)
    lo = pltpu.bitcast(xb[:, :PACK_W], U32) >> 16
    hi = pltpu.bitcast(xb[:, PACK_W:], U32) & jnp.uint32(0xFFFF0000)
    w = hi | lo
    for j in range(SUBROWS):
        ref[:, 8 * j:8 * (j + 1), :] = w[:, 128 * j:128 * (j + 1)].reshape(m // 8, 8, 128)


def _load_packed_rows(ref):
    m = ref.shape[0] * 8
    ws = [ref[:, 8 * j:8 * (j + 1), :].reshape(m, 128) for j in range(SUBROWS)]
    lo = [pltpu.bitcast(w << 16, F32) for w in ws]
    hi = [pltpu.bitcast(w & jnp.uint32(0xFFFF0000), F32) for w in ws]
    return jnp.concatenate(lo + hi, axis=1)


W_IN_BLOCKS = 15
W_IN_SLOTS = 4


def _proj_kernel(x_ref, w_hbm, lng_ref, lnb_ref, ws_ref, bias_ref,
                 ga_ref, gates_ref, qkv1_ref, qkv2_ref, qkv3_ref, xc_ref, w, wstage_ref, wsem):
    tm = x_ref.shape[0]

    @pl.when(pl.program_id(0) == 0)
    def _():
        def fetch(j):
            return pltpu.make_async_copy(w_hbm.at[:, pl.ds(j * COL, COL)], wstage_ref.at[j % W_IN_SLOTS],
                                         wsem.at[j % W_IN_SLOTS])

        for j in range(W_IN_SLOTS):
            fetch(j).start()
        for j in range(W_IN_BLOCKS):
            fetch(j).wait()
            w[j] = wstage_ref[j % W_IN_SLOTS].astype(BF16)
            if j + W_IN_SLOTS < W_IN_BLOCKS:
                fetch(j + W_IN_SLOTS).start()

    xb = x_ref[...].astype(BF16)

    u_raw = _dot(xb, w[0])
    v_raw = _dot(xb, w[1])

    for i in range(4):
        gates_ref[:, i * COL:(i + 1) * COL] = _sigmoid(_dot(xb, w[11 + i])).astype(BF16)
    for j in range(3):
        qkv1_ref[:, j * COL:(j + 1) * COL] = _dot(xb, w[2 + 3 * j]).astype(BF16)

    for c in range(D_MODEL // 128):
        xc_ref[c] = x_ref[:, c * 128:(c + 1) * 128]
    for gi, out_ref in ((1, qkv2_ref), (2, qkv3_ref)):
        dl = B_DILATIONS[gi]
        per = tm // dl
        xp = jnp.concatenate(
            [jnp.concatenate([xc_ref[c, pl.ds(r, per, stride=dl), :] for c in range(D_MODEL // 128)], axis=1)
             for r in range(dl)], axis=0).astype(BF16)
        for j in range(3):
            res = _dot(xp, w[2 + 3 * j + gi]).astype(BF16)
            for r in range(dl):
                out_ref[r, :, j * COL:(j + 1) * COL] = res[r * per:(r + 1) * per]

    u = _gelu_tanh(u_raw)
    v = _gelu_tanh(v_raw)
    vn = _ln(v, lng_ref[...], lnb_ref[...]).astype(BF16)

    row = lax.broadcasted_iota(I32, (A_CHUNK, A_CHUNK), 0)
    colm = lax.broadcasted_iota(I32, (A_CHUNK, A_CHUNK), 1)
    causal = colm <= row
    lo = colm < 64
    zero = jnp.zeros((A_CHUNK, A_CHUNK), BF16)
    wcat = []
    for j in range(4):
        w0 = jnp.where(causal, ws_ref[2 * j], 0.0).astype(BF16)
        w1 = jnp.where(causal, ws_ref[2 * j + 1], 0.0).astype(BF16)
        wcat.append(jnp.concatenate([w0, w1], axis=1))
    for c in range(tm // A_CHUNK):
        r0 = c * A_CHUNK
        for j in range(4):
            c0 = j * 128
            vt = vn[r0:r0 + A_CHUNK, c0:c0 + 128]
            rhs = jnp.concatenate([jnp.where(lo, vt, zero), jnp.where(lo, zero, vt)], axis=0)
            mixed = _dot(wcat[j], rhs) + bias_ref[:, c0:c0 + 128]
            ga_ref[r0:r0 + A_CHUNK, c0:c0 + 128] = (u[r0:r0 + A_CHUNK, c0:c0 + 128] * mixed).astype(BF16)


def _proj(x, w_in, a_ln_g, a_ln_b, a_ws, a_bias):
    bsz, s, _ = x.shape
    n = bsz * s
    tm = TM_PROJ
    tiles = s // tm
    x2 = x.reshape(n, D_MODEL)
    full = lambda shape: pl.BlockSpec(shape, lambda i: (0,) * len(shape))
    rows = lambda width: pl.BlockSpec((tm, width), lambda i: (i, 0))
    dil = lambda dl: pl.BlockSpec((None, dl, tm // dl, 3 * COL), lambda i: (i // tiles, 0, i % tiles, 0))
    return pl.pallas_call(
        _proj_kernel,
        grid=(n // tm,),
        in_specs=[rows(D_MODEL), pl.BlockSpec(memory_space=pl.ANY)]
                 + [full((1, A_WIDTH)), full((1, A_WIDTH)), full((8, A_CHUNK, A_CHUNK)), full((A_CHUNK, A_WIDTH))],
        out_specs=[rows(A_WIDTH), rows(4 * COL), rows(3 * COL), dil(4), dil(16)],
        out_shape=[jax.ShapeDtypeStruct((n, A_WIDTH), BF16),
                   jax.ShapeDtypeStruct((n, 4 * COL), BF16),
                   jax.ShapeDtypeStruct((n, 3 * COL), BF16),
                   jax.ShapeDtypeStruct((bsz, 4, s // 4, 3 * COL), BF16),
                   jax.ShapeDtypeStruct((bsz, 16, s // 16, 3 * COL), BF16)],
        scratch_shapes=[pltpu.VMEM((D_MODEL // 128, tm, 128), F32),
                        pltpu.VMEM((W_IN_BLOCKS, D_MODEL, COL), BF16),
                        pltpu.VMEM((W_IN_SLOTS, D_MODEL, COL), F32),
                        pltpu.SemaphoreType.DMA((W_IN_SLOTS,))],
        compiler_params=pltpu.CompilerParams(dimension_semantics=("arbitrary",), vmem_limit_bytes=VMEM_LIMIT),
        name="proj",
    )(x2, w_in, a_ln_g, a_ln_b, a_ws, a_bias)


def _attn_kernel(qkv_ref, o_ref, lse_ref, *, ns, seq):
    nb = seq // SPAN
    lane = lax.broadcasted_iota(I32, (SPAN, 128), 1)
    lo = lane < 64
    lane16 = lane // 16
    qi = lax.broadcasted_iota(I32, (SPAN, 2 * SPAN), 0)
    ki = lax.broadcasted_iota(I32, (SPAN, 2 * SPAN), 1)
    causal = lax.broadcasted_iota(I32, (SPAN, SPAN), 1) <= lax.broadcasted_iota(I32, (SPAN, SPAN), 0)
    bias_first = jnp.where(causal, 0.0, NEG).astype(F32)
    bias_first = jnp.concatenate([bias_first, bias_first], axis=0)
    bias_main = jnp.where((ki >= qi) & (ki <= qi + SPAN), 0.0, NEG).astype(F32)
    bias_main = jnp.concatenate([bias_main, bias_main], axis=0)
    zero = jnp.zeros((SPAN, 128), BF16)

    for s in range(ns):
        def block(row0, start, bias, s=s):
            win = bias.shape[1]
            pairs = range(B_HEADS // 2)
            scores, values = [], []
            for jp in pairs:
                c0 = jp * 128
                q = qkv_ref[s, pl.ds(row0, SPAN), c0:c0 + 128] * jnp.asarray(0.125, BF16)
                k = qkv_ref[s, pl.ds(start, win), COL + c0:COL + c0 + 128]
                values.append(qkv_ref[s, pl.ds(start, win), 2 * COL + c0:2 * COL + c0 + 128])
                qs = jnp.concatenate([jnp.where(lo, q, zero), jnp.where(lo, zero, q)], axis=0)
                scores.append(lax.dot_general(qs, k, (((1,), (1,)), ((), ())), preferred_element_type=F32) + bias)
            probs, maxes, sums = [], [], []
            for jp in pairs:
                m = jnp.max(scores[jp], axis=-1, keepdims=True)
                p = jnp.exp(scores[jp] - m)
                maxes.append(m)
                sums.append(jnp.sum(p, axis=-1, keepdims=True))
                probs.append(p.astype(BF16))
            lse_tile = jnp.zeros((SPAN, 128), F32)
            for jp in pairs:
                c0 = jp * 128
                ov = _dot(probs[jp], values[jp])
                inv = 1.0 / sums[jp]
                o = jnp.where(lo, ov[:SPAN] * inv[:SPAN], ov[SPAN:] * inv[SPAN:])
                o_ref[pl.ds(row0, SPAN), s * B_WIDTH + c0:s * B_WIDTH + c0 + 128] = o.astype(BF16)
                lse = maxes[jp] + jnp.log(sums[jp])
                lse_tile = jnp.where(lane16 == 2 * jp, lse[:SPAN],
                                     jnp.where(lane16 == 2 * jp + 1, lse[SPAN:], lse_tile))
            lse_ref[pl.ds(row0, SPAN), s * 128:(s + 1) * 128] = lse_tile

        block(0, 0, bias_first)
        if nb > 1:
            def body(i, carry):
                block(pl.multiple_of(i * SPAN, SPAN), pl.multiple_of((i - 1) * SPAN, SPAN), bias_main)
                return carry
            lax.fori_loop(1, nb, body, 0, unroll=min(5, nb - 1))


def _attn(qkv_g):
    bsz, dl, seq, _ = qkv_g.shape
    ns = max(1, min(dl, TM_ATTN // seq))
    return pl.pallas_call(
        functools.partial(_attn_kernel, ns=ns, seq=seq),
        grid=(bsz, dl // ns),
        in_specs=[pl.BlockSpec((None, ns, seq, 3 * COL), lambda b, r: (b, r, 0, 0))],
        out_specs=[pl.BlockSpec((None, seq, ns * B_WIDTH), lambda b, r: (b, 0, r)),
                   pl.BlockSpec((None, seq, ns * 128), lambda b, r: (b, 0, r))],
        out_shape=[jax.ShapeDtypeStruct((bsz, seq, dl * B_WIDTH), BF16),
                   jax.ShapeDtypeStruct((bsz, seq, dl * 128), F32)],
        compiler_params=pltpu.CompilerParams(dimension_semantics=("parallel", "parallel"),
                                             vmem_limit_bytes=VMEM_LIMIT),
        name=f"attn{dl}",
    )(qkv_g)


def _natural_rows(ref, dl, scr):
    nchunk, tm, _ = scr.shape
    w = nchunk * 128
    per = tm // dl
    for r in range(dl):
        for c in range(nchunk):
            scr[c, pl.ds(r, per, stride=dl), :] = ref[:, r * w + c * 128:r * w + (c + 1) * 128].astype(F32)
    return jnp.concatenate([scr[c] for c in range(nchunk)], axis=1)


def _mix_kernel(ga_ref, gates_ref, o1_ref, o2_ref, o3_ref, l1_ref, l2_ref, l3_ref, x_ref,
                wa_ref, wb_ref, wo_ref, wr_ref, br_ref, g1_ref, b1_ref,
                x1_ref, x1p_ref, routet_ref, o2s_ref, o3s_ref, l2s_ref, l3s_ref):
    tm = x_ref.shape[0]
    o2 = _natural_rows(o2_ref, 4, o2s_ref)
    o3 = _natural_rows(o3_ref, 16, o3s_ref)
    l2 = _natural_rows(l2_ref, 4, l2s_ref)
    l3 = _natural_rows(l3_ref, 16, l3s_ref)
    er = lax.broadcasted_iota(I32, (256, B_WIDTH), 0)
    ec = lax.broadcasted_iota(I32, (256, B_WIDTH), 1)
    expand = jnp.where(er % 128 == (ec // B_HEAD_DIM) * 16, 1.0, 0.0).astype(BF16)

    def widen(w):
        hi = w.astype(BF16)
        lo = (w - hi.astype(F32)).astype(BF16)
        return _dot(jnp.concatenate([hi, lo], axis=1), expand)

    h = tm // 2
    halves = (slice(0, h), slice(h, tm))
    obs = []
    for r in halves:
        l1 = l1_ref[r, :]
        mx = jnp.maximum(l1, jnp.maximum(l2[r], l3[r]))
        e1, e2, e3 = jnp.exp(l1 - mx), jnp.exp(l2[r] - mx), jnp.exp(l3[r] - mx)
        inv = 1.0 / (e1 + e2 + e3)
        obs.append(widen(e1 * inv) * o1_ref[r, :].astype(F32) + widen(e2 * inv) * o2[r] + widen(e3 * inv) * o3[r])
    ybs = [_dot(ob.astype(BF16), wb_ref[...]) for ob in obs]
    yas = [_dot(ga_ref[r, :], wa_ref[...]) for r in halves]
    pres = [gates_ref[r, :D_MODEL].astype(F32) * ya + gates_ref[r, D_MODEL:].astype(F32) * yb
            for r, ya, yb in zip(halves, yas, ybs)]
    mixes = [_dot(pre.astype(BF16), wo_ref[...]) for pre in pres]
    x1s = [_ln(DEEPNORM_ALPHA * x_ref[r, :] + mix, g1_ref[...], b1_ref[...]) for r, mix in zip(halves, mixes)]
    logits = [_dot(x1.astype(BF16), wr_ref[...]) + br_ref[...] for x1 in x1s]

    nrow = 40
    row = lax.broadcasted_iota(I32, (nrow, h), 0).astype(F32)
    row8 = lax.broadcasted_iota(I32, (8, h), 0)
    big = 1e9
    for i, r in enumerate(halves):
        x1_ref[r, :] = x1s[i]
        _store_packed_rows(x1p_ref.at[i * h // 8:(i + 1) * h // 8], x1s[i])
        lg = logits[i].T[:nrow, :]
        gl = jnp.where(row < N_GROUPS, lg, NEG)
        gm = jnp.max(gl, axis=0, keepdims=True)
        gidx = jnp.min(jnp.where(gl == gm, row, big), axis=0, keepdims=True)
        gsum = jnp.sum(jnp.where(row < N_GROUPS, jnp.exp(gl - gm), 0.0), axis=0, keepdims=True)
        gprob = 1.0 / gsum
        lo_row = N_GROUPS + N_EXPERTS * gidx
        el = jnp.where((row >= lo_row) & (row < lo_row + N_EXPERTS), lg, NEG)
        v1 = jnp.max(el, axis=0, keepdims=True)
        i1 = jnp.min(jnp.where(el == v1, row, big), axis=0, keepdims=True)
        el2 = jnp.where(row == i1, NEG, el)
        v2 = jnp.max(el2, axis=0, keepdims=True)
        i2 = jnp.min(jnp.where(el2 == v2, row, big), axis=0, keepdims=True)
        t = jnp.exp(v2 - v1)
        w1 = 1.0 / (1.0 + t)
        w2 = t * w1
        routet_ref[:, r] = jnp.where(row8 == 0, i1 - N_GROUPS,
                                     jnp.where(row8 == 1, i2 - N_GROUPS,
                                               jnp.where(row8 == 2, gprob * w1,
                                                         jnp.where(row8 == 3, gprob * w2, 0.0))))


def _mix(ga, gates, o1, o2, o3, l1, l2, l3, x2, wa, wb, wo, wr, br, g1, b1):
    n = x2.shape[0]
    bsz = o2.shape[0]
    tm = TM_MIX
    tiles = n // bsz // tm
    rows = lambda w: pl.BlockSpec((tm, w), lambda i: (i, 0))
    grouped = lambda a, dl: pl.BlockSpec((None, tm // dl, a.shape[2]), lambda i: (i // tiles, i % tiles, 0))
    full = lambda a: pl.BlockSpec(a.shape, lambda i: (0,) * a.ndim)
    return pl.pallas_call(
        _mix_kernel,
        grid=(n // tm,),
        in_specs=[rows(A_WIDTH), rows(2 * D_MODEL), rows(B_WIDTH), grouped(o2, 4), grouped(o3, 16),
                  rows(128), grouped(l2, 4), grouped(l3, 16), rows(D_MODEL),
                  full(wa), full(wb), full(wo), full(wr), full(br), full(g1), full(b1)],
        out_specs=[rows(D_MODEL), pl.BlockSpec((tm // 8, 32, 128), lambda i: (i, 0, 0)),
                   pl.BlockSpec((8, tm), lambda i: (0, i))],
        out_shape=[jax.ShapeDtypeStruct((n, D_MODEL), F32),
                   jax.ShapeDtypeStruct((n // 8, 32, 128), U32),
                   jax.ShapeDtypeStruct((8, n), F32)],
        scratch_shapes=[pltpu.VMEM((B_WIDTH // 128, tm, 128), F32), pltpu.VMEM((B_WIDTH // 128, tm, 128), F32),
                        pltpu.VMEM((1, tm, 128), F32), pltpu.VMEM((1, tm, 128), F32)],
        compiler_params=pltpu.CompilerParams(dimension_semantics=("parallel",), vmem_limit_bytes=VMEM_LIMIT),
        name="mix",
    )(ga, gates, o1, o2, o3, l1, l2, l3, x2, wa, wb, wo, wr, br, g1, b1)


SC_WINDOW = 128


def _sc_mesh():
    return plsc.VectorSubcoreMesh(core_axis_name="core", subcore_axis_name="subcore")


def _sc_scatter_rows(rows, dst, n_out):
    r = rows.shape[0]

    @pl.kernel(out_type=jax.ShapeDtypeStruct((n_out, 128), rows.dtype), mesh=_sc_mesh())
    def scatter(rows_hbm, dst0_hbm, dst1_hbm, out_hbm):
        def body(rows_vmem, dst0_vmem, dst1_vmem):
            pltpu.sync_copy(rows_vmem, out_hbm.at[dst0_vmem.at[0]])
            pltpu.sync_copy(rows_vmem, out_hbm.at[dst1_vmem.at[0]])

        pltpu.emit_pipeline(
            body,
            grid=(r // SC_WINDOW,),
            in_specs=[pl.BlockSpec((SC_WINDOW, 128), lambda i: (i, 0)),
                      pl.BlockSpec((1, SC_WINDOW), lambda i: (0, i)),
                      pl.BlockSpec((1, SC_WINDOW), lambda i: (0, i))],
            out_specs=[],
            core_axis_name=("core", "subcore"),
            dimension_semantics=(pltpu.PARALLEL,),
        )(rows_hbm, dst0_hbm, dst1_hbm)

    return scatter(rows, dst[0:1], dst[1:2])


def _sc_gather_rows(table, src):
    m = src.shape[0]
    k = 2

    @pl.kernel(out_type=jax.ShapeDtypeStruct((m, 128), table.dtype), mesh=_sc_mesh(),
               scratch_types=[pltpu.SemaphoreType.DMA((k,))])
    def gather(table_hbm, src_hbm, out_hbm, sems):
        def body(src_vmem, out_vmem):
            copies = [pltpu.async_copy(table_hbm.at[src_vmem.at[j]], out_vmem.at[pl.ds(j * SC_WINDOW, SC_WINDOW)],
                                       sems.at[j]) for j in range(k)]
            for c in copies:
                c.wait()

        pltpu.emit_pipeline(
            body,
            grid=(m // (k * SC_WINDOW),),
            in_specs=[pl.BlockSpec((k, SC_WINDOW), lambda i: (i, 0))],
            out_specs=[pl.BlockSpec((k * SC_WINDOW, 128), lambda i: (i, 0))],
            core_axis_name=("core", "subcore"),
            dimension_semantics=(pltpu.PARALLEL,),
        )(src_hbm, out_hbm)

    return gather(table, src.reshape(m // SC_WINDOW, SC_WINDOW))


MOE_X_SLOTS = 3
MOE_Y_SLOTS = 2


def _moe_kernel(tiles_ref, xs_hbm, wg_hbm, wu_hbm, wd_hbm, ys_hbm,
                xbuf, ybuf, wgf_ref, wuf_ref, wdf_ref, wgb_ref, wub_ref, wdb_ref, xsem, ysem, wsem, *, nt):
    te_ref, tv_ref, ws_ref, nx_ref = (tiles_ref.at[i] for i in range(4))
    t = pl.program_id(0)
    grp = TM_MOE // 8

    def fetch_x(j):
        return pltpu.make_async_copy(xs_hbm.at[pl.ds(j * grp, grp)], xbuf.at[j % MOE_X_SLOTS],
                                     xsem.at[j % MOE_X_SLOTS])

    def store_y(j):
        return pltpu.make_async_copy(ybuf.at[j % MOE_Y_SLOTS], ys_hbm.at[pl.ds(j * grp, grp)],
                                     ysem.at[j % MOE_Y_SLOTS])

    def fetch_w(e, slot):
        return [pltpu.make_async_copy(w.at[e], buf.at[slot], wsem.at[slot, i])
                for i, (w, buf) in enumerate(((wg_hbm, wgf_ref), (wu_hbm, wuf_ref), (wd_hbm, wdf_ref)))]

    @pl.when(t == 0)
    def _():
        for c in fetch_w(te_ref[0], 0):
            c.start()
        for j in range(MOE_X_SLOTS - 1):
            @pl.when(tv_ref[j] == 1)
            def _():
                fetch_x(j).start()

    ahead = t + (MOE_X_SLOTS - 1)

    @pl.when((ahead < nt) & (tv_ref[jnp.minimum(ahead, nt - 1)] == 1))
    def _():
        fetch_x(ahead).start()

    slot = ws_ref[t]

    @pl.when((t == 0) | (te_ref[t] != te_ref[jnp.maximum(t - 1, 0)]))
    def _():
        @pl.when(nx_ref[t] >= 0)
        def _():
            for c in fetch_w(nx_ref[t], 1 - slot):
                c.start()

        for c in fetch_w(te_ref[t], slot):
            c.wait()
        wgb_ref[...] = wgf_ref[slot].astype(BF16)
        wub_ref[...] = wuf_ref[slot].astype(BF16)
        wdb_ref[...] = wdf_ref[slot].astype(BF16)

    @pl.when((t >= MOE_Y_SLOTS) & (tv_ref[jnp.maximum(t - MOE_Y_SLOTS, 0)] == 1))
    def _():
        store_y(t - MOE_Y_SLOTS).wait()

    @pl.when(tv_ref[t] == 1)
    def _():
        fetch_x(t).wait()
        xb = _load_packed_rows(xbuf.at[t % MOE_X_SLOTS]).astype(BF16)
        g = _dot(xb, wgb_ref[...])
        u = _dot(xb, wub_ref[...])
        h = (g * _sigmoid(g) * u).astype(BF16)
        _store_packed_rows(ybuf.at[t % MOE_Y_SLOTS], _dot(h, wdb_ref[...]))
        store_y(t).start()

    @pl.when(t == nt - 1)
    def _():
        for j in range(nt - MOE_Y_SLOTS, nt):
            @pl.when(tv_ref[j] == 1)
            def _():
                store_y(j).wait()


def _moe(tiles, nt, xs, wg, wu, wd):
    tm = TM_MOE
    any_spec = pl.BlockSpec(memory_space=pl.ANY)
    return pl.pallas_call(
        functools.partial(_moe_kernel, nt=nt),
        grid_spec=pltpu.PrefetchScalarGridSpec(
            num_scalar_prefetch=1,
            grid=(nt,),
            in_specs=[any_spec] * 4,
            out_specs=any_spec,
            scratch_shapes=[pltpu.VMEM((MOE_X_SLOTS, tm // 8, 32, 128), U32),
                            pltpu.VMEM((MOE_Y_SLOTS, tm // 8, 32, 128), U32),
                            pltpu.VMEM((2, D_MODEL, D_EXPERT), F32), pltpu.VMEM((2, D_MODEL, D_EXPERT), F32),
                            pltpu.VMEM((2, D_EXPERT, D_MODEL), F32),
                            pltpu.VMEM((D_MODEL, D_EXPERT), BF16), pltpu.VMEM((D_MODEL, D_EXPERT), BF16),
                            pltpu.VMEM((D_EXPERT, D_MODEL), BF16),
                            pltpu.SemaphoreType.DMA((MOE_X_SLOTS,)), pltpu.SemaphoreType.DMA((MOE_Y_SLOTS,)),
                            pltpu.SemaphoreType.DMA((2, 3))]),
        out_shape=jax.ShapeDtypeStruct((nt * tm // 8, 32, 128), U32),
        compiler_params=pltpu.CompilerParams(dimension_semantics=("arbitrary",), vmem_limit_bytes=VMEM_LIMIT),
        name="moe",
    )(tiles, xs, wg, wu, wd)


def _final_kernel(y0_ref, y1_ref, x1_ref, p_ref, routet_ref, wple_ref, wpg_ref, g2_ref, b2_ref, *rest):
    out_ref = rest[-1]
    tm = x1_ref.shape[0]
    route = routet_ref[...].T
    h = tm // 4
    parts = tuple(slice(i * h, (i + 1) * h) for i in range(4))
    x1s = [x1_ref[r, :] for r in parts]
    plins = [_dot(p_ref[r, :].astype(BF16), wple_ref[...]) for r in parts]
    gates = [_dot(x1.astype(BF16), wpg_ref[...]) for x1 in x1s]
    sums = []
    for i, r in enumerate(parts):
        g8 = slice(i * h // 8, (i + 1) * h // 8)
        sums.append(DEEPNORM_ALPHA * x1s[i] + route[r, 2:3] * _load_packed_rows(y0_ref.at[g8])
                    + route[r, 3:4] * _load_packed_rows(y1_ref.at[g8]))
    for i, r in enumerate(parts):
        out_ref[r, :] = _ln((sums[i] + plins[i]) + plins[i] * jnp.tanh(gates[i]), g2_ref[...], b2_ref[...])


COMBINE_PARTS = 1


def _final(yg, x1, p2, routet, wple, wpg, g2, b2, half, prev):
    n = x1.shape[0]
    tm = TM_FIN
    nt = n // COMBINE_PARTS // tm
    off = half * nt
    rows = lambda w: pl.BlockSpec((tm, w), lambda t: (t + off, 0))
    full = lambda a: pl.BlockSpec(a.shape, lambda t: (0,) * a.ndim)
    in_specs = [pl.BlockSpec((tm // 8, 32, 128), lambda t: (t, 0, 0)),
                pl.BlockSpec((tm // 8, 32, 128), lambda t: (t + nt, 0, 0)),
                rows(D_MODEL), rows(PLE_DIM), pl.BlockSpec((8, tm), lambda t: (0, t + off)),
                full(wple), full(wpg), full(g2), full(b2)]
    args = [yg, yg, x1, p2, routet, wple, wpg, g2, b2]
    aliases = {}
    if prev is not None:
        in_specs.append(pl.BlockSpec(memory_space=pl.ANY))
        args.append(prev)
        aliases = {len(args) - 1: 0}
    return pl.pallas_call(
        _final_kernel,
        grid=(nt,),
        in_specs=in_specs,
        out_specs=rows(D_MODEL),
        out_shape=jax.ShapeDtypeStruct((n, D_MODEL), F32),
        input_output_aliases=aliases,
        compiler_params=pltpu.CompilerParams(dimension_semantics=("parallel",), vmem_limit_bytes=VMEM_LIMIT),
        name=f"final{half}",
    )(*args)


def _route_tables_kernel(e_ref, piece_ref, tab_ref):
    r = e_ref.shape[0]
    e = e_ref[...]
    ri = lax.broadcasted_iota(I32, (128, 128), 0)
    ci = lax.broadcasted_iota(I32, (128, 128), 1)
    upper = jnp.where(ri <= ci, 1.0, 0.0).astype(BF16)
    rr = lax.broadcasted_iota(I32, (r, r), 0)
    rc = lax.broadcasted_iota(I32, (r, r), 1)
    below = jnp.where(rc < rr, 1.0, 0.0).astype(BF16)
    lane = lax.broadcasted_iota(I32, (1, 128), 1)

    rank = jnp.zeros((r, 128), F32)
    counts = jnp.zeros((1, 128), F32)
    for x in range(N_EXPERTS_TOTAL):
        m = jnp.where(e == x, 1.0, 0.0)
        pre = _dot(m.astype(BF16), upper)
        tot = jnp.broadcast_to(pre[:, 127:128], (r, 128))
        off = _dot(below, tot.astype(BF16))
        rank = rank + m * (pre + off)
        counts = jnp.where(lane == x, off[r - 1:r, :] + tot[r - 1:r, :], counts)
    padded = jnp.floor((counts + (TM_MOE - 1)) * (1.0 / TM_MOE)) * TM_MOE
    ends = _dot(jnp.broadcast_to(padded, (8, 128)).astype(BF16), upper)[0:1, :]
    offs = ends - padded

    nt = r * 128 // TM_MOE + N_EXPERTS_TOTAL
    ones = jnp.ones((128, 128), BF16)
    lanef = lane.astype(F32)
    diag = ri == ci
    ends_t = ends * (1.0 / TM_MOE)
    ecol = _dot(jnp.where(diag, jnp.broadcast_to(ends_t, (128, 128)), 0.0).astype(BF16), ones)
    passed = jnp.where((ecol <= ci.astype(F32)) & (ri < N_EXPERTS_TOTAL), 1.0, 0.0)
    te = jnp.minimum(jnp.sum(passed, axis=0, keepdims=True), N_EXPERTS_TOTAL - 1.0)
    tv = jnp.where(lanef < ecol[N_EXPERTS_TOTAL - 1:N_EXPERTS_TOTAL, :], 1.0, 0.0)
    shift = jnp.where(ri + 1 == ci, 1.0, 0.0).astype(BF16)
    te_prev = _dot(jnp.broadcast_to(te, (8, 128)).astype(BF16), shift)[0:1, :]
    first = jnp.where((lane == 0) | (te != te_prev), 1.0, 0.0)
    run = _dot(jnp.broadcast_to(first, (8, 128)).astype(BF16), upper)[0:1, :] - 1.0
    ws = run - 2.0 * jnp.floor(run * 0.5)
    tcol = _dot(jnp.where(diag, jnp.broadcast_to(te, (128, 128)), 0.0).astype(BF16), ones)
    later = jnp.where((tcol > te) & (ri < nt), tcol, float(N_EXPERTS_TOTAL))
    nx = jnp.min(later, axis=0, keepdims=True)
    nx = jnp.where(nx == N_EXPERTS_TOTAL, -1.0, nx)
    row8 = lax.broadcasted_iota(I32, (8, 128), 0)
    tab = jnp.where(row8 == 0, te, jnp.where(row8 == 1, tv, jnp.where(row8 == 2, ws, jnp.where(row8 == 3, nx, 0.0))))
    tab_ref[...] = tab.astype(I32)

    pos = rank - 1.0
    for x in range(N_EXPERTS_TOTAL):
        pos = pos + jnp.where(e == x, offs[:, x:x + 1], 0.0)

    hi = jnp.floor(pos * (1.0 / 256.0))
    lo = pos - 256.0 * hi
    jv = ((lane % 32) // 8).astype(F32)
    for c in range(4):
        sel = jnp.where(ri == 32 * c + 8 * (ci // 32) + ci % 8, 1.0, 0.0).astype(BF16)
        pc = 256.0 * _dot(hi.astype(BF16), sel) + _dot(lo.astype(BF16), sel)
        p8 = jnp.floor(pc * 0.125)
        piece = p8 * (8.0 * SUBROWS) + (pc - 8.0 * p8) + 8.0 * jv
        piece_ref[pl.ds(c, r, stride=4), :] = piece.astype(I32)


def _routing_tables(routet, n):
    nt = (2 * n) // TM_MOE + N_EXPERTS_TOTAL
    assert nt <= 128, "the tile tables hold one tile per lane"
    r = 2 * n // 128
    piece, tiles = pl.pallas_call(
        _route_tables_kernel,
        out_shape=[jax.ShapeDtypeStruct((4 * r, 128), I32), jax.ShapeDtypeStruct((8, 128), I32)],
        compiler_params=pltpu.CompilerParams(vmem_limit_bytes=VMEM_LIMIT),
        name="route_tables",
    )(routet[0:2].reshape(r, 128))
    return tiles, nt, piece.reshape(2, n * SUBROWS)


def kernel(x, p, w_in, a_ln_g, a_ln_b, a_ws, a_bs, w_a_proj, w_b_proj, w_o, ln1_g, ln1_b, w_group_router,
           b_group_router, w_expert_router, b_expert_router, w_gate, w_up, w_down, w_ple, w_ple_gate,
           ln2_g, ln2_b):
    bsz, s, d = x.shape
    n = bsz * s
    assert d == D_MODEL and s % (SPAN * max(B_DILATIONS)) == 0 and n % TM_PROJ == 0
    assert w_in.shape[0] == 1, "one layer"

    a_bias = jnp.repeat(a_bs[0].T, A_WIDTH // 8, axis=1)

    ga, gates, qkv1, qkv2, qkv3 = _proj(x, w_in[0], a_ln_g, a_ln_b, a_ws[0], a_bias)
    o1, l1 = _attn(qkv1.reshape(bsz, 1, s, 3 * COL))
    o2, l2 = _attn(qkv2)
    o3, l3 = _attn(qkv3)

    pad = 128 - N_GROUPS - N_EXPERTS_TOTAL
    wr = jnp.concatenate([w_group_router[0], w_expert_router[0].reshape(d, N_EXPERTS_TOTAL),
                          jnp.zeros((d, pad), F32)], axis=1).astype(BF16)
    br = jnp.concatenate([b_group_router[0], b_expert_router[0].reshape(-1), jnp.zeros((pad,), F32)])[None, :]
    x1, x1p, routet = _mix(
        ga, gates, o1.reshape(n, B_WIDTH), o2, o3, l1.reshape(n, 128), l2, l3, x.reshape(n, d),
        w_a_proj[0].astype(BF16), w_b_proj[0].astype(BF16), w_o[0].astype(BF16), wr, br, ln1_g, ln1_b)

    tiles, nt, piece = _routing_tables(routet, n)
    xs = _sc_scatter_rows(x1p.reshape(n * SUBROWS, 128), piece, nt * TM_MOE * SUBROWS)
    ys = _moe(tiles, nt, xs.reshape(nt * TM_MOE // 8, 32, 128),
              w_gate[0].reshape(N_EXPERTS_TOTAL, d, D_EXPERT), w_up[0].reshape(N_EXPERTS_TOTAL, d, D_EXPERT),
              w_down[0].reshape(N_EXPERTS_TOTAL, D_EXPERT, d)).reshape(nt * TM_MOE * SUBROWS, 128)
    out = None
    hp = n * SUBROWS // COMBINE_PARTS
    for half in range(COMBINE_PARTS):
        yg = _sc_gather_rows(ys, piece[:, half * hp:(half + 1) * hp].reshape(-1))
        out = _final(yg.reshape(2 * n // COMBINE_PARTS // 8, 32, 128), x1, p[0].reshape(n, PLE_DIM), routet,
                     (0.5 * w_ple[0]).astype(BF16), (0.5 * w_ple_gate[0]).astype(BF16), ln2_g, ln2_b, half, out)
    return out.reshape(bsz, s, d)
```

```python
import functools

import jax
import jax.numpy as jnp
from jax import lax
from jax.experimental import pallas as pl
from jax.experimental.pallas import tpu as pltpu
from jax.experimental.pallas import tpu_sc as plsc

F32 = jnp.float32
BF16 = jnp.bfloat16
U32 = jnp.uint32
I32 = jnp.int32

D_MODEL = 1024
PLE_DIM = 256
A_WIDTH = 512
A_CHUNK = 128
B_HEAD_DIM = 64
B_HEADS = 8
B_WIDTH = 512
B_DILATIONS = (1, 4, 16)
SPAN = 128
N_GROUPS = 4
N_EXPERTS = 8
N_EXPERTS_TOTAL = N_GROUPS * N_EXPERTS
D_EXPERT = 256
DEEPNORM_ALPHA = 2.0 ** 0.25
LN_EPS = 1e-5
COL = 512
NEG = -1e30

VMEM_LIMIT = 56 * 1024 * 1024

TM_PROJ = 512
TM_ATTN = 1024
TM_MIX = 1024
TM_MOE = 512
TM_FIN = 1024


def _ln(x, g, b):
    mu = jnp.mean(x, axis=-1, keepdims=True)
    xc = x - mu
    var = jnp.mean(xc * xc, axis=-1, keepdims=True)
    return xc * lax.rsqrt(var + LN_EPS) * g + b


def _gelu_tanh(x):
    return 0.5 * x * (1.0 + jnp.tanh(0.7978845608028654 * (x + 0.044715 * (x * x * x))))


def _sigmoid(x):
    return 0.5 * jnp.tanh(0.5 * x) + 0.5


def _dot(a, b):
    return jnp.dot(a, b, preferred_element_type=F32)


PACK_W = D_MODEL // 2
SUBROWS = PACK_W // 128


def _store_packed_rows(ref, x):
    m = x.shape[0]
    xb = x.astype(BF16).astype(F32)
    lo = pltpu.bitcast(xb[:, :PACK_W], U32) >> 16
    hi = pltpu.bitcast(xb[:, PACK_W:], U32) & jnp.uint32(0xFFFF0000)
    w = hi | lo
    for j in range(SUBROWS):
        ref[:, 8 * j:8 * (j + 1), :] = w[:, 128 * j:128 * (j + 1)].reshape(m // 8, 8, 128)


def _load_packed_rows(ref):
    m = ref.shape[0] * 8
    ws = [ref[:, 8 * j:8 * (j + 1), :].reshape(m, 128) for j in range(SUBROWS)]
    lo = [pltpu.bitcast(w << 16, F32) for w in ws]
    hi = [pltpu.bitcast(w & jnp.uint32(0xFFFF0000), F32) for w in ws]
    return jnp.concatenate(lo + hi, axis=1)


W_IN_BLOCKS = 15
W_IN_SLOTS = 4


def _proj_kernel(x_ref, w_hbm, lng_ref, lnb_ref, ws_ref, bias_ref,
                 ga_ref, gates_ref, qkv1_ref, qkv2_ref, qkv3_ref, xc_ref, w, wstage_ref, wsem):
    tm = x_ref.shape[0]

    @pl.when(pl.program_id(0) == 0)
    def _():
        def fetch(j):
            return pltpu.make_async_copy(w_hbm.at[:, pl.ds(j * COL, COL)], wstage_ref.at[j % W_IN_SLOTS],
                                         wsem.at[j % W_IN_SLOTS])

        for j in range(W_IN_SLOTS):
            fetch(j).start()
        for j in range(W_IN_BLOCKS):
            fetch(j).wait()
            w[j] = wstage_ref[j % W_IN_SLOTS].astype(BF16)
            if j + W_IN_SLOTS < W_IN_BLOCKS:
                fetch(j + W_IN_SLOTS).start()

    xb = x_ref[...].astype(BF16)

    u_raw = _dot(xb, w[0])
    v_raw = _dot(xb, w[1])

    for i in range(4):
        gates_ref[:, i * COL:(i + 1) * COL] = _sigmoid(_dot(xb, w[11 + i])).astype(BF16)
    for j in range(3):
        qkv1_ref[:, j * COL:(j + 1) * COL] = _dot(xb, w[2 + 3 * j]).astype(BF16)

    for c in range(D_MODEL // 128):
        xc_ref[c] = x_ref[:, c * 128:(c + 1) * 128]
    for gi, out_ref in ((1, qkv2_ref), (2, qkv3_ref)):
        dl = B_DILATIONS[gi]
        per = tm // dl
        xp = jnp.concatenate(
            [jnp.concatenate([xc_ref[c, pl.ds(r, per, stride=dl), :] for c in range(D_MODEL // 128)], axis=1)
             for r in range(dl)], axis=0).astype(BF16)
        for j in range(3):
            res = _dot(xp, w[2 + 3 * j + gi]).astype(BF16)
            for r in range(dl):
                out_ref[r, :, j * COL:(j + 1) * COL] = res[r * per:(r + 1) * per]

    u = _gelu_tanh(u_raw)
    v = _gelu_tanh(v_raw)
    vn = _ln(v, lng_ref[...], lnb_ref[...]).astype(BF16)

    row = lax.broadcasted_iota(I32, (A_CHUNK, A_CHUNK), 0)
    colm = lax.broadcasted_iota(I32, (A_CHUNK, A_CHUNK), 1)
    causal = colm <= row
    lo = colm < 64
    zero = jnp.zeros((A_CHUNK, A_CHUNK), BF16)
    wcat = []
    for j in range(4):
        w0 = jnp.where(causal, ws_ref[2 * j], 0.0).astype(BF16)
        w1 = jnp.where(causal, ws_ref[2 * j + 1], 0.0).astype(BF16)
        wcat.append(jnp.concatenate([w0, w1], axis=1))
    for c in range(tm // A_CHUNK):
        r0 = c * A_CHUNK
        for j in range(4):
            c0 = j * 128
            vt = vn[r0:r0 + A_CHUNK, c0:c0 + 128]
            rhs = jnp.concatenate([jnp.where(lo, vt, zero), jnp.where(lo, zero, vt)], axis=0)
            mixed = _dot(wcat[j], rhs) + bias_ref[:, c0:c0 + 128]
            ga_ref[r0:r0 + A_CHUNK, c0:c0 + 128] = (u[r0:r0 + A_CHUNK, c0:c0 + 128] * mixed).astype(BF16)


def _proj(x, w_in, a_ln_g, a_ln_b, a_ws, a_bias):
    bsz, s, _ = x.shape
    n = bsz * s
    tm = TM_PROJ
    tiles = s // tm
    x2 = x.reshape(n, D_MODEL)
    full = lambda shape: pl.BlockSpec(shape, lambda i: (0,) * len(shape))
    rows = lambda width: pl.BlockSpec((tm, width), lambda i: (i, 0))
    dil = lambda dl: pl.BlockSpec((None, dl, tm // dl, 3 * COL), lambda i: (i // tiles, 0, i % tiles, 0))
    return pl.pallas_call(
        _proj_kernel,
        grid=(n // tm,),
        in_specs=[rows(D_MODEL), pl.BlockSpec(memory_space=pl.ANY)]
                 + [full((1, A_WIDTH)), full((1, A_WIDTH)), full((8, A_CHUNK, A_CHUNK)), full((A_CHUNK, A_WIDTH))],
        out_specs=[rows(A_WIDTH), rows(4 * COL), rows(3 * COL), dil(4), dil(16)],
        out_shape=[jax.ShapeDtypeStruct((n, A_WIDTH), BF16),
                   jax.ShapeDtypeStruct((n, 4 * COL), BF16),
                   jax.ShapeDtypeStruct((n, 3 * COL), BF16),
                   jax.ShapeDtypeStruct((bsz, 4, s // 4, 3 * COL), BF16),
                   jax.ShapeDtypeStruct((bsz, 16, s // 16, 3 * COL), BF16)],
        scratch_shapes=[pltpu.VMEM((D_MODEL // 128, tm, 128), F32),
                        pltpu.VMEM((W_IN_BLOCKS, D_MODEL, COL), BF16),
                        pltpu.VMEM((W_IN_SLOTS, D_MODEL, COL), F32),
                        pltpu.SemaphoreType.DMA((W_IN_SLOTS,))],
        compiler_params=pltpu.CompilerParams(dimension_semantics=("arbitrary",), vmem_limit_bytes=VMEM_LIMIT),
        name="proj",
    )(x2, w_in, a_ln_g, a_ln_b, a_ws, a_bias)


def _attn_kernel(qkv_ref, o_ref, lse_ref, *, ns, seq):
    nb = seq // SPAN
    lane = lax.broadcasted_iota(I32, (SPAN, 128), 1)
    lo = lane < 64
    lane16 = lane // 16
    qi = lax.broadcasted_iota(I32, (SPAN, 2 * SPAN), 0)
    ki = lax.broadcasted_iota(I32, (SPAN, 2 * SPAN), 1)
    causal = lax.broadcasted_iota(I32, (SPAN, SPAN), 1) <= lax.broadcasted_iota(I32, (SPAN, SPAN), 0)
    bias_first = jnp.where(causal, 0.0, NEG).astype(F32)
    bias_first = jnp.concatenate([bias_first, bias_first], axis=0)
    bias_main = jnp.where((ki >= qi) & (ki <= qi + SPAN), 0.0, NEG).astype(F32)
    bias_main = jnp.concatenate([bias_main, bias_main], axis=0)
    zero = jnp.zeros((SPAN, 128), BF16)

    for s in range(ns):
        def block(row0, start, bias, s=s):
            win = bias.shape[1]
            pairs = range(B_HEADS // 2)
            scores, values = [], []
            for jp in pairs:
                c0 = jp * 128
                q = qkv_ref[s, pl.ds(row0, SPAN), c0:c0 + 128] * jnp.asarray(0.125, BF16)
                k = qkv_ref[s, pl.ds(start, win), COL + c0:COL + c0 + 128]
                values.append(qkv_ref[s, pl.ds(start, win), 2 * COL + c0:2 * COL + c0 + 128])
                qs = jnp.concatenate([jnp.where(lo, q, zero), jnp.where(lo, zero, q)], axis=0)
                scores.append(lax.dot_general(qs, k, (((1,), (1,)), ((), ())), preferred_element_type=F32) + bias)
            probs, maxes, sums = [], [], []
            for jp in pairs:
                m = jnp.max(scores[jp], axis=-1, keepdims=True)
                p = jnp.exp(scores[jp] - m)
                maxes.append(m)
                sums.append(jnp.sum(p, axis=-1, keepdims=True))
                probs.append(p.astype(BF16))
            lse_tile = jnp.zeros((SPAN, 128), F32)
            for jp in pairs:
                c0 = jp * 128
                ov = _dot(probs[jp], values[jp])
                inv = 1.0 / sums[jp]
                o = jnp.where(lo, ov[:SPAN] * inv[:SPAN], ov[SPAN:] * inv[SPAN:])
                o_ref[pl.ds(row0, SPAN), s * B_WIDTH + c0:s * B_WIDTH + c0 + 128] = o.astype(BF16)
                lse = maxes[jp] + jnp.log(sums[jp])
                lse_tile = jnp.where(lane16 == 2 * jp, lse[:SPAN],
                                     jnp.where(lane16 == 2 * jp + 1, lse[SPAN:], lse_tile))
            lse_ref[pl.ds(row0, SPAN), s * 128:(s + 1) * 128] = lse_tile

        block(0, 0, bias_first)
        if nb > 1:
            def body(i, carry):
                block(pl.multiple_of(i * SPAN, SPAN), pl.multiple_of((i - 1) * SPAN, SPAN), bias_main)
                return carry
            lax.fori_loop(1, nb, body, 0, unroll=min(5, nb - 1))


def _attn(qkv_g):
    bsz, dl, seq, _ = qkv_g.shape
    ns = max(1, min(dl, TM_ATTN // seq))
    return pl.pallas_call(
        functools.partial(_attn_kernel, ns=ns, seq=seq),
        grid=(bsz, dl // ns),
        in_specs=[pl.BlockSpec((None, ns, seq, 3 * COL), lambda b, r: (b, r, 0, 0))],
        out_specs=[pl.BlockSpec((None, seq, ns * B_WIDTH), lambda b, r: (b, 0, r)),
                   pl.BlockSpec((None, seq, ns * 128), lambda b, r: (b, 0, r))],
        out_shape=[jax.ShapeDtypeStruct((bsz, seq, dl * B_WIDTH), BF16),
                   jax.ShapeDtypeStruct((bsz, seq, dl * 128), F32)],
        compiler_params=pltpu.CompilerParams(dimension_semantics=("parallel", "parallel"),
                                             vmem_limit_bytes=VMEM_LIMIT),
        name=f"attn{dl}",
    )(qkv_g)


def _natural_rows(ref, dl, scr):
    nchunk, tm, _ = scr.shape
    w = nchunk * 128
    per = tm // dl
    for r in range(dl):
        for c in range(nchunk):
            scr[c, pl.ds(r, per, stride=dl), :] = ref[:, r * w + c * 128:r * w + (c + 1) * 128].astype(F32)
    return jnp.concatenate([scr[c] for c in range(nchunk)], axis=1)


def _mix_kernel(ga_ref, gates_ref, o1_ref, o2_ref, o3_ref, l1_ref, l2_ref, l3_ref, x_ref,
                wa_ref, wb_ref, wo_ref, wr_ref, br_ref, g1_ref, b1_ref,
                x1_ref, x1p_ref, routet_ref, o2s_ref, o3s_ref, l2s_ref, l3s_ref):
    tm = x_ref.shape[0]
    o2 = _natural_rows(o2_ref, 4, o2s_ref)
    o3 = _natural_rows(o3_ref, 16, o3s_ref)
    l2 = _natural_rows(l2_ref, 4, l2s_ref)
    l3 = _natural_rows(l3_ref, 16, l3s_ref)
    er = lax.broadcasted_iota(I32, (256, B_WIDTH), 0)
    ec = lax.broadcasted_iota(I32, (256, B_WIDTH), 1)
    expand = jnp.where(er % 128 == (ec // B_HEAD_DIM) * 16, 1.0, 0.0).astype(BF16)

    def widen(w):
        hi = w.astype(BF16)
        lo = (w - hi.astype(F32)).astype(BF16)
        return _dot(jnp.concatenate([hi, lo], axis=1), expand)

    h = tm // 2
    halves = (slice(0, h), slice(h, tm))
    obs = []
    for r in halves:
        l1 = l1_ref[r, :]
        mx = jnp.maximum(l1, jnp.maximum(l2[r], l3[r]))
        e1, e2, e3 = jnp.exp(l1 - mx), jnp.exp(l2[r] - mx), jnp.exp(l3[r] - mx)
        inv = 1.0 / (e1 + e2 + e3)
        obs.append(widen(e1 * inv) * o1_ref[r, :].astype(F32) + widen(e2 * inv) * o2[r] + widen(e3 * inv) * o3[r])
    ybs = [_dot(ob.astype(BF16), wb_ref[...]) for ob in obs]
    yas = [_dot(ga_ref[r, :], wa_ref[...]) for r in halves]
    pres = [gates_ref[r, :D_MODEL].astype(F32) * ya + gates_ref[r, D_MODEL:].astype(F32) * yb
            for r, ya, yb in zip(halves, yas, ybs)]
    mixes = [_dot(pre.astype(BF16), wo_ref[...]) for pre in pres]
    x1s = [_ln(DEEPNORM_ALPHA * x_ref[r, :] + mix, g1_ref[...], b1_ref[...]) for r, mix in zip(halves, mixes)]
    logits = [_dot(x1.astype(BF16), wr_ref[...]) + br_ref[...] for x1 in x1s]

    nrow = 40
    row = lax.broadcasted_iota(I32, (nrow, h), 0).astype(F32)
    row8 = lax.broadcasted_iota(I32, (8, h), 0)
    big = 1e9
    for i, r in enumerate(halves):
        x1_ref[r, :] = x1s[i]
        _store_packed_rows(x1p_ref.at[i * h // 8:(i + 1) * h // 8], x1s[i])
        lg = logits[i].T[:nrow, :]
        gl = jnp.where(row < N_GROUPS, lg, NEG)
        gm = jnp.max(gl, axis=0, keepdims=True)
        gidx = jnp.min(jnp.where(gl == gm, row, big), axis=0, keepdims=True)
        gsum = jnp.sum(jnp.where(row < N_GROUPS, jnp.exp(gl - gm), 0.0), axis=0, keepdims=True)
        gprob = 1.0 / gsum
        lo_row = N_GROUPS + N_EXPERTS * gidx
        el = jnp.where((row >= lo_row) & (row < lo_row + N_EXPERTS), lg, NEG)
        v1 = jnp.max(el, axis=0, keepdims=True)
        i1 = jnp.min(jnp.where(el == v1, row, big), axis=0, keepdims=True)
        el2 = jnp.where(row == i1, NEG, el)
        v2 = jnp.max(el2, axis=0, keepdims=True)
        i2 = jnp.min(jnp.where(el2 == v2, row, big), axis=0, keepdims=True)
        t = jnp.exp(v2 - v1)
        w1 = 1.0 / (1.0 + t)
        w2 = t * w1
        routet_ref[:, r] = jnp.where(row8 == 0, i1 - N_GROUPS,
                                     jnp.where(row8 == 1, i2 - N_GROUPS,
                                               jnp.where(row8 == 2, gprob * w1,
                                                         jnp.where(row8 == 3, gprob * w2, 0.0))))


def _mix(ga, gates, o1, o2, o3, l1, l2, l3, x2, wa, wb, wo, wr, br, g1, b1):
    n = x2.shape[0]
    bsz = o2.shape[0]
    tm = TM_MIX
    tiles = n // bsz // tm
    rows = lambda w: pl.BlockSpec((tm, w), lambda i: (i, 0))
    grouped = lambda a, dl: pl.BlockSpec((None, tm // dl, a.shape[2]), lambda i: (i // tiles, i % tiles, 0))
    full = lambda a: pl.BlockSpec(a.shape, lambda i: (0,) * a.ndim)
    return pl.pallas_call(
        _mix_kernel,
        grid=(n // tm,),
        in_specs=[rows(A_WIDTH), rows(2 * D_MODEL), rows(B_WIDTH), grouped(o2, 4), grouped(o3, 16),
                  rows(128), grouped(l2, 4), grouped(l3, 16), rows(D_MODEL),
                  full(wa), full(wb), full(wo), full(wr), full(br), full(g1), full(b1)],
        out_specs=[rows(D_MODEL), pl.BlockSpec((tm // 8, 32, 128), lambda i: (i, 0, 0)),
                   pl.BlockSpec((8, tm), lambda i: (0, i))],
        out_shape=[jax.ShapeDtypeStruct((n, D_MODEL), F32),
                   jax.ShapeDtypeStruct((n // 8, 32, 128), U32),
                   jax.ShapeDtypeStruct((8, n), F32)],
        scratch_shapes=[pltpu.VMEM((B_WIDTH // 128, tm, 128), F32), pltpu.VMEM((B_WIDTH // 128, tm, 128), F32),
                        pltpu.VMEM((1, tm, 128), F32), pltpu.VMEM((1, tm, 128), F32)],
        compiler_params=pltpu.CompilerParams(dimension_semantics=("parallel",), vmem_limit_bytes=VMEM_LIMIT),
        name="mix",
    )(ga, gates, o1, o2, o3, l1, l2, l3, x2, wa, wb, wo, wr, br, g1, b1)


SC_WINDOW = 128


def _sc_mesh():
    return plsc.VectorSubcoreMesh(core_axis_name="core", subcore_axis_name="subcore")


def _sc_scatter_rows(rows, dst, n_out):
    r = rows.shape[0]

    @pl.kernel(out_type=jax.ShapeDtypeStruct((n_out, 128), rows.dtype), mesh=_sc_mesh())
    def scatter(rows_hbm, dst0_hbm, dst1_hbm, out_hbm):
        def body(rows_vmem, dst0_vmem, dst1_vmem):
            pltpu.sync_copy(rows_vmem, out_hbm.at[dst0_vmem.at[0]])
            pltpu.sync_copy(rows_vmem, out_hbm.at[dst1_vmem.at[0]])

        pltpu.emit_pipeline(
            body,
            grid=(r // SC_WINDOW,),
            in_specs=[pl.BlockSpec((SC_WINDOW, 128), lambda i: (i, 0)),
                      pl.BlockSpec((1, SC_WINDOW), lambda i: (0, i)),
                      pl.BlockSpec((1, SC_WINDOW), lambda i: (0, i))],
            out_specs=[],
            core_axis_name=("core", "subcore"),
            dimension_semantics=(pltpu.PARALLEL,),
        )(rows_hbm, dst0_hbm, dst1_hbm)

    return scatter(rows, dst[0:1], dst[1:2])


def _sc_gather_rows(table, src):
    m = src.shape[0]
    k = 2

    @pl.kernel(out_type=jax.ShapeDtypeStruct((m, 128), table.dtype), mesh=_sc_mesh(),
               scratch_types=[pltpu.SemaphoreType.DMA((k,))])
    def gather(table_hbm, src_hbm, out_hbm, sems):
        def body(src_vmem, out_vmem):
            copies = [pltpu.async_copy(table_hbm.at[src_vmem.at[j]], out_vmem.at[pl.ds(j * SC_WINDOW, SC_WINDOW)],
                                       sems.at[j]) for j in range(k)]
            for c in copies:
                c.wait()

        pltpu.emit_pipeline(
            body,
            grid=(m // (k * SC_WINDOW),),
            in_specs=[pl.BlockSpec((k, SC_WINDOW), lambda i: (i, 0))],
            out_specs=[pl.BlockSpec((k * SC_WINDOW, 128), lambda i: (i, 0))],
            core_axis_name=("core", "subcore"),
            dimension_semantics=(pltpu.PARALLEL,),
        )(src_hbm, out_hbm)

    return gather(table, src.reshape(m // SC_WINDOW, SC_WINDOW))


MOE_X_SLOTS = 3
MOE_Y_SLOTS = 2


def _moe_kernel(tiles_ref, xs_hbm, wg_hbm, wu_hbm, wd_hbm, ys_hbm,
                xbuf, ybuf, wgf_ref, wuf_ref, wdf_ref, wgb_ref, wub_ref, wdb_ref, xsem, ysem, wsem, *, nt):
    te_ref, tv_ref, ws_ref, nx_ref = (tiles_ref.at[i] for i in range(4))
    t = pl.program_id(0)
    grp = TM_MOE // 8

    def fetch_x(j):
        return pltpu.make_async_copy(xs_hbm.at[pl.ds(j * grp, grp)], xbuf.at[j % MOE_X_SLOTS],
                                     xsem.at[j % MOE_X_SLOTS])

    def store_y(j):
        return pltpu.make_async_copy(ybuf.at[j % MOE_Y_SLOTS], ys_hbm.at[pl.ds(j * grp, grp)],
                                     ysem.at[j % MOE_Y_SLOTS])

    def fetch_w(e, slot):
        return [pltpu.make_async_copy(w.at[e], buf.at[slot], wsem.at[slot, i])
                for i, (w, buf) in enumerate(((wg_hbm, wgf_ref), (wu_hbm, wuf_ref), (wd_hbm, wdf_ref)))]

    @pl.when(t == 0)
    def _():
        for c in fetch_w(te_ref[0], 0):
            c.start()
        for j in range(MOE_X_SLOTS - 1):
            @pl.when(tv_ref[j] == 1)
            def _():
                fetch_x(j).start()

    ahead = t + (MOE_X_SLOTS - 1)

    @pl.when((ahead < nt) & (tv_ref[jnp.minimum(ahead, nt - 1)] == 1))
    def _():
        fetch_x(ahead).start()

    slot = ws_ref[t]

    @pl.when((t == 0) | (te_ref[t] != te_ref[jnp.maximum(t - 1, 0)]))
    def _():
        @pl.when(nx_ref[t] >= 0)
        def _():
            for c in fetch_w(nx_ref[t], 1 - slot):
                c.start()

        for c in fetch_w(te_ref[t], slot):
            c.wait()
        wgb_ref[...] = wgf_ref[slot].astype(BF16)
        wub_ref[...] = wuf_ref[slot].astype(BF16)
        wdb_ref[...] = wdf_ref[slot].astype(BF16)

    @pl.when((t >= MOE_Y_SLOTS) & (tv_ref[jnp.maximum(t - MOE_Y_SLOTS, 0)] == 1))
    def _():
        store_y(t - MOE_Y_SLOTS).wait()

    @pl.when(tv_ref[t] == 1)
    def _():
        fetch_x(t).wait()
        xb = _load_packed_rows(xbuf.at[t % MOE_X_SLOTS]).astype(BF16)
        g = _dot(xb, wgb_ref[...])
        u = _dot(xb, wub_ref[...])
        h = (g * _sigmoid(g) * u).astype(BF16)
        _store_packed_rows(ybuf.at[t % MOE_Y_SLOTS], _dot(h, wdb_ref[...]))
        store_y(t).start()

    @pl.when(t == nt - 1)
    def _():
        for j in range(nt - MOE_Y_SLOTS, nt):
            @pl.when(tv_ref[j] == 1)
            def _():
                store_y(j).wait()


def _moe(tiles, nt, xs, wg, wu, wd):
    tm = TM_MOE
    any_spec = pl.BlockSpec(memory_space=pl.ANY)
    return pl.pallas_call(
        functools.partial(_moe_kernel, nt=nt),
        grid_spec=pltpu.PrefetchScalarGridSpec(
            num_scalar_prefetch=1,
            grid=(nt,),
            in_specs=[any_spec] * 4,
            out_specs=any_spec,
            scratch_shapes=[pltpu.VMEM((MOE_X_SLOTS, tm // 8, 32, 128), U32),
                            pltpu.VMEM((MOE_Y_SLOTS, tm // 8, 32, 128), U32),
                            pltpu.VMEM((2, D_MODEL, D_EXPERT), F32), pltpu.VMEM((2, D_MODEL, D_EXPERT), F32),
                            pltpu.VMEM((2, D_EXPERT, D_MODEL), F32),
                            pltpu.VMEM((D_MODEL, D_EXPERT), BF16), pltpu.VMEM((D_MODEL, D_EXPERT), BF16),
                            pltpu.VMEM((D_EXPERT, D_MODEL), BF16),
                            pltpu.SemaphoreType.DMA((MOE_X_SLOTS,)), pltpu.SemaphoreType.DMA((MOE_Y_SLOTS,)),
                            pltpu.SemaphoreType.DMA((2, 3))]),
        out_shape=jax.ShapeDtypeStruct((nt * tm // 8, 32, 128), U32),
        compiler_params=pltpu.CompilerParams(dimension_semantics=("arbitrary",), vmem_limit_bytes=VMEM_LIMIT),
        name="moe",
    )(tiles, xs, wg, wu, wd)


def _final_kernel(y0_ref, y1_ref, x1_ref, p_ref, routet_ref, wple_ref, wpg_ref, g2_ref, b2_ref, *rest):
    out_ref = rest[-1]
    tm = x1_ref.shape[0]
    route = routet_ref[...].T
    h = tm // 4
    parts = tuple(slice(i * h, (i + 1) * h) for i in range(4))
    x1s = [x1_ref[r, :] for r in parts]
    plins = [_dot(p_ref[r, :].astype(BF16), wple_ref[...]) for r in parts]
    gates = [_dot(x1.astype(BF16), wpg_ref[...]) for x1 in x1s]
    sums = []
    for i, r in enumerate(parts):
        g8 = slice(i * h // 8, (i + 1) * h // 8)
        sums.append(DEEPNORM_ALPHA * x1s[i] + route[r, 2:3] * _load_packed_rows(y0_ref.at[g8])
                    + route[r, 3:4] * _load_packed_rows(y1_ref.at[g8]))
    for i, r in enumerate(parts):
        out_ref[r, :] = _ln((sums[i] + plins[i]) + plins[i] * jnp.tanh(gates[i]), g2_ref[...], b2_ref[...])


COMBINE_PARTS = 1


def _final(yg, x1, p2, routet, wple, wpg, g2, b2, half, prev):
    n = x1.shape[0]
    tm = TM_FIN
    nt = n // COMBINE_PARTS // tm
    off = half * nt
    rows = lambda w: pl.BlockSpec((tm, w), lambda t: (t + off, 0))
    full = lambda a: pl.BlockSpec(a.shape, lambda t: (0,) * a.ndim)
    in_specs = [pl.BlockSpec((tm // 8, 32, 128), lambda t: (t, 0, 0)),
                pl.BlockSpec((tm // 8, 32, 128), lambda t: (t + nt, 0, 0)),
                rows(D_MODEL), rows(PLE_DIM), pl.BlockSpec((8, tm), lambda t: (0, t + off)),
                full(wple), full(wpg), full(g2), full(b2)]
    args = [yg, yg, x1, p2, routet, wple, wpg, g2, b2]
    aliases = {}
    if prev is not None:
        in_specs.append(pl.BlockSpec(memory_space=pl.ANY))
        args.append(prev)
        aliases = {len(args) - 1: 0}
    return pl.pallas_call(
        _final_kernel,
        grid=(nt,),
        in_specs=in_specs,
        out_specs=rows(D_MODEL),
        out_shape=jax.ShapeDtypeStruct((n, D_MODEL), F32),
        input_output_aliases=aliases,
        compiler_params=pltpu.CompilerParams(dimension_semantics=("parallel",), vmem_limit_bytes=VMEM_LIMIT),
        name=f"final{half}",
    )(*args)


def _route_tables_kernel(e_ref, piece_ref, tab_ref):
    r = e_ref.shape[0]
    e = e_ref[...]
    ri = lax.broadcasted_iota(I32, (128, 128), 0)
    ci = lax.broadcasted_iota(I32, (128, 128), 1)
    upper = jnp.where(ri <= ci, 1.0, 0.0).astype(BF16)
    rr = lax.broadcasted_iota(I32, (r, r), 0)
    rc = lax.broadcasted_iota(I32, (r, r), 1)
    below = jnp.where(rc < rr, 1.0, 0.0).astype(BF16)
    lane = lax.broadcasted_iota(I32, (1, 128), 1)

    rank = jnp.zeros((r, 128), F32)
    counts = jnp.zeros((1, 128), F32)
    for x in range(N_EXPERTS_TOTAL):
        m = jnp.where(e == x, 1.0, 0.0)
        pre = _dot(m.astype(BF16), upper)
        tot = jnp.broadcast_to(pre[:, 127:128], (r, 128))
        off = _dot(below, tot.astype(BF16))
        rank = rank + m * (pre + off)
        counts = jnp.where(lane == x, off[r - 1:r, :] + tot[r - 1:r, :], counts)
    padded = jnp.floor((counts + (TM_MOE - 1)) * (1.0 / TM_MOE)) * TM_MOE
    ends = _dot(jnp.broadcast_to(padded, (8, 128)).astype(BF16), upper)[0:1, :]
    offs = ends - padded

    nt = r * 128 // TM_MOE + N_EXPERTS_TOTAL
    ones = jnp.ones((128, 128), BF16)
    lanef = lane.astype(F32)
    diag = ri == ci
    ends_t = ends * (1.0 / TM_MOE)
    ecol = _dot(jnp.where(diag, jnp.broadcast_to(ends_t, (128, 128)), 0.0).astype(BF16), ones)
    passed = jnp.where((ecol <= ci.astype(F32)) & (ri < N_EXPERTS_TOTAL), 1.0, 0.0)
    te = jnp.minimum(jnp.sum(passed, axis=0, keepdims=True), N_EXPERTS_TOTAL - 1.0)
    tv = jnp.where(lanef < ecol[N_EXPERTS_TOTAL - 1:N_EXPERTS_TOTAL, :], 1.0, 0.0)
    shift = jnp.where(ri + 1 == ci, 1.0, 0.0).astype(BF16)
    te_prev = _dot(jnp.broadcast_to(te, (8, 128)).astype(BF16), shift)[0:1, :]
    first = jnp.where((lane == 0) | (te != te_prev), 1.0, 0.0)
    run = _dot(jnp.broadcast_to(first, (8, 128)).astype(BF16), upper)[0:1, :] - 1.0
    ws = run - 2.0 * jnp.floor(run * 0.5)
    tcol = _dot(jnp.where(diag, jnp.broadcast_to(te, (128, 128)), 0.0).astype(BF16), ones)
    later = jnp.where((tcol > te) & (ri < nt), tcol, float(N_EXPERTS_TOTAL))
    nx = jnp.min(later, axis=0, keepdims=True)
    nx = jnp.where(nx == N_EXPERTS_TOTAL, -1.0, nx)
    row8 = lax.broadcasted_iota(I32, (8, 128), 0)
    tab = jnp.where(row8 == 0, te, jnp.where(row8 == 1, tv, jnp.where(row8 == 2, ws, jnp.where(row8 == 3, nx, 0.0))))
    tab_ref[...] = tab.astype(I32)

    pos = rank - 1.0
    for x in range(N_EXPERTS_TOTAL):
        pos = pos + jnp.where(e == x, offs[:, x:x + 1], 0.0)

    hi = jnp.floor(pos * (1.0 / 256.0))
    lo = pos - 256.0 * hi
    jv = ((lane % 32) // 8).astype(F32)
    for c in range(4):
        sel = jnp.where(ri == 32 * c + 8 * (ci // 32) + ci % 8, 1.0, 0.0).astype(BF16)
        pc = 256.0 * _dot(hi.astype(BF16), sel) + _dot(lo.astype(BF16), sel)
        p8 = jnp.floor(pc * 0.125)
        piece = p8 * (8.0 * SUBROWS) + (pc - 8.0 * p8) + 8.0 * jv
        piece_ref[pl.ds(c, r, stride=4), :] = piece.astype(I32)


def _routing_tables(routet, n):
    nt = (2 * n) // TM_MOE + N_EXPERTS_TOTAL
    assert nt <= 128, "the tile tables hold one tile per lane"
    r = 2 * n // 128
    piece, tiles = pl.pallas_call(
        _route_tables_kernel,
        out_shape=[jax.ShapeDtypeStruct((4 * r, 128), I32), jax.ShapeDtypeStruct((8, 128), I32)],
        compiler_params=pltpu.CompilerParams(vmem_limit_bytes=VMEM_LIMIT),
        name="route_tables",
    )(routet[0:2].reshape(r, 128))
    return tiles, nt, piece.reshape(2, n * SUBROWS)


def kernel(x, p, w_in, a_ln_g, a_ln_b, a_ws, a_bs, w_a_proj, w_b_proj, w_o, ln1_g, ln1_b, w_group_router,
           b_group_router, w_expert_router, b_expert_router, w_gate, w_up, w_down, w_ple, w_ple_gate,
           ln2_g, ln2_b):
    bsz, s, d = x.shape
    n = bsz * s
    assert d == D_MODEL and s % (SPAN * max(B_DILATIONS)) == 0 and n % TM_PROJ == 0
    assert w_in.shape[0] == 1, "one layer"

    a_bias = jnp.repeat(a_bs[0].T, A_WIDTH // 8, axis=1)

    ga, gates, qkv1, qkv2, qkv3 = _proj(x, w_in[0], a_ln_g, a_ln_b, a_ws[0], a_bias)
    o1, l1 = _attn(qkv1.reshape(bsz, 1, s, 3 * COL))
    o2, l2 = _attn(qkv2)
    o3, l3 = _attn(qkv3)

    pad = 128 - N_GROUPS - N_EXPERTS_TOTAL
    wr = jnp.concatenate([w_group_router[0], w_expert_router[0].reshape(d, N_EXPERTS_TOTAL),
                          jnp.zeros((d, pad), F32)], axis=1).astype(BF16)
    br = jnp.concatenate([b_group_router[0], b_expert_router[0].reshape(-1), jnp.zeros((pad,), F32)])[None, :]
    x1, x1p, routet = _mix(
        ga, gates, o1.reshape(n, B_WIDTH), o2, o3, l1.reshape(n, 128), l2, l3, x.reshape(n, d),
        w_a_proj[0].astype(BF16), w_b_proj[0].astype(BF16), w_o[0].astype(BF16), wr, br, ln1_g, ln1_b)

    tiles, nt, piece = _routing_tables(routet, n)
    xs = _sc_scatter_rows(x1p.reshape(n * SUBROWS, 128), piece, nt * TM_MOE * SUBROWS)
    ys = _moe(tiles, nt, xs.reshape(nt * TM_MOE // 8, 32, 128),
              w_gate[0].reshape(N_EXPERTS_TOTAL, d, D_EXPERT), w_up[0].reshape(N_EXPERTS_TOTAL, d, D_EXPERT),
              w_down[0].reshape(N_EXPERTS_TOTAL, D_EXPERT, d)).reshape(nt * TM_MOE * SUBROWS, 128)
    out = None
    hp = n * SUBROWS // COMBINE_PARTS
    for half in range(COMBINE_PARTS):
        yg = _sc_gather_rows(ys, piece[:, half * hp:(half + 1) * hp].reshape(-1))
        out = _final(yg.reshape(2 * n // COMBINE_PARTS // 8, 32, 128), x1, p[0].reshape(n, PLE_DIM), routet,
                     (0.5 * w_ple[0]).astype(BF16), (0.5 * w_ple_gate[0]).astype(BF16), ln2_g, ln2_b, half, out)
    return out.reshape(bsz, s, d)
```

```python
import functools

import jax
import jax.numpy as jnp
from jax import lax
from jax.experimental import pallas as pl
from jax.experimental.pallas import tpu as pltpu
from jax.experimental.pallas import tpu_sc as plsc

F32 = jnp.float32
BF16 = jnp.bfloat16
U32 = jnp.uint32
I32 = jnp.int32

D_MODEL = 1024
PLE_DIM = 256
A_WIDTH = 512
A_CHUNK = 128
B_HEAD_DIM = 64
B_HEADS = 8
B_WIDTH = 512
B_DILATIONS = (1, 4, 16)
SPAN = 128
N_GROUPS = 4
N_EXPERTS = 8
N_EXPERTS_TOTAL = N_GROUPS * N_EXPERTS
D_EXPERT = 256
DEEPNORM_ALPHA = 2.0 ** 0.25
LN_EPS = 1e-5
COL = 512
NEG = -1e30

VMEM_LIMIT = 56 * 1024 * 1024

TM_PROJ = 512
TM_ATTN = 1024
TM_MIX = 512
TM_MOE = 512
TM_FIN = 1024


def _ln(x, g, b):
    mu = jnp.mean(x, axis=-1, keepdims=True)
    xc = x - mu
    var = jnp.mean(xc * xc, axis=-1, keepdims=True)
    return xc * lax.rsqrt(var + LN_EPS) * g + b


def _gelu_tanh(x):
    return 0.5 * x * (1.0 + jnp.tanh(0.7978845608028654 * (x + 0.044715 * (x * x * x))))


def _sigmoid(x):
    return 0.5 * jnp.tanh(0.5 * x) + 0.5


def _dot(a, b):
    return jnp.dot(a, b, preferred_element_type=F32)


PACK_W = D_MODEL // 2
SUBROWS = PACK_W // 128


def _store_packed_rows(ref, x):
    m = x.shape[0]
    xb = x.astype(BF16).astype(F32)
    lo = pltpu.bitcast(xb[:, :PACK_W], U32) >> 16
    hi = pltpu.bitcast(xb[:, PACK_W:], U32) & jnp.uint32(0xFFFF0000)
    w = hi | lo
    for j in range(SUBROWS):
        ref[:, 8 * j:8 * (j + 1), :] = w[:, 128 * j:128 * (j + 1)].reshape(m // 8, 8, 128)


def _load_packed_rows(ref):
    m = ref.shape[0] * 8
    ws = [ref[:, 8 * j:8 * (j + 1), :].reshape(m, 128) for j in range(SUBROWS)]
    lo = [pltpu.bitcast(w << 16, F32) for w in ws]
    hi = [pltpu.bitcast(w & jnp.uint32(0xFFFF0000), F32) for w in ws]
    return jnp.concatenate(lo + hi, axis=1)


W_IN_BLOCKS = 15
W_IN_SLOTS = 4


def _proj_kernel(x_ref, w_hbm, lng_ref, lnb_ref, ws_ref, bias_ref,
                 ga_ref, gates_ref, qkv1_ref, qkv2_ref, qkv3_ref, xc_ref, w, wstage_ref, wsem):
    tm = x_ref.shape[0]

    @pl.when(pl.program_id(0) == 0)
    def _():
        def fetch(j):
            return pltpu.make_async_copy(w_hbm.at[:, pl.ds(j * COL, COL)], wstage_ref.at[j % W_IN_SLOTS],
                                         wsem.at[j % W_IN_SLOTS])

        for j in range(W_IN_SLOTS):
            fetch(j).start()
        for j in range(W_IN_BLOCKS):
            fetch(j).wait()
            w[j] = wstage_ref[j % W_IN_SLOTS].astype(BF16)
            if j + W_IN_SLOTS < W_IN_BLOCKS:
                fetch(j + W_IN_SLOTS).start()

    xb = x_ref[...].astype(BF16)

    u_raw = _dot(xb, w[0])
    v_raw = _dot(xb, w[1])

    for i in range(4):
        gates_ref[:, i * COL:(i + 1) * COL] = _sigmoid(_dot(xb, w[11 + i])).astype(BF16)
    for j in range(3):
        qkv1_ref[:, j * COL:(j + 1) * COL] = _dot(xb, w[2 + 3 * j]).astype(BF16)

    for c in range(D_MODEL // 128):
        xc_ref[c] = x_ref[:, c * 128:(c + 1) * 128]
    for gi, out_ref in ((1, qkv2_ref), (2, qkv3_ref)):
        dl = B_DILATIONS[gi]
        per = tm // dl
        xp = jnp.concatenate(
            [jnp.concatenate([xc_ref[c, pl.ds(r, per, stride=dl), :] for c in range(D_MODEL // 128)], axis=1)
             for r in range(dl)], axis=0).astype(BF16)
        for j in range(3):
            res = _dot(xp, w[2 + 3 * j + gi]).astype(BF16)
            for r in range(dl):
                out_ref[r, :, j * COL:(j + 1) * COL] = res[r * per:(r + 1) * per]

    u = _gelu_tanh(u_raw)
    v = _gelu_tanh(v_raw)
    vn = _ln(v, lng_ref[...], lnb_ref[...]).astype(BF16)

    row = lax.broadcasted_iota(I32, (A_CHUNK, A_CHUNK), 0)
    colm = lax.broadcasted_iota(I32, (A_CHUNK, A_CHUNK), 1)
    causal = colm <= row
    lo = colm < 64
    zero = jnp.zeros((A_CHUNK, A_CHUNK), BF16)
    wcat = []
    for j in range(4):
        w0 = jnp.where(causal, ws_ref[2 * j], 0.0).astype(BF16)
        w1 = jnp.where(causal, ws_ref[2 * j + 1], 0.0).astype(BF16)
        wcat.append(jnp.concatenate([w0, w1], axis=1))
    for c in range(tm // A_CHUNK):
        r0 = c * A_CHUNK
        for j in range(4):
            c0 = j * 128
            vt = vn[r0:r0 + A_CHUNK, c0:c0 + 128]
            rhs = jnp.concatenate([jnp.where(lo, vt, zero), jnp.where(lo, zero, vt)], axis=0)
            mixed = _dot(wcat[j], rhs) + bias_ref[:, c0:c0 + 128]
            ga_ref[r0:r0 + A_CHUNK, c0:c0 + 128] = (u[r0:r0 + A_CHUNK, c0:c0 + 128] * mixed).astype(BF16)


def _proj(x, w_in, a_ln_g, a_ln_b, a_ws, a_bias):
    bsz, s, _ = x.shape
    n = bsz * s
    tm = TM_PROJ
    tiles = s // tm
    x2 = x.reshape(n, D_MODEL)
    full = lambda shape: pl.BlockSpec(shape, lambda i: (0,) * len(shape))
    rows = lambda width: pl.BlockSpec((tm, width), lambda i: (i, 0))
    dil = lambda dl: pl.BlockSpec((None, dl, tm // dl, 3 * COL), lambda i: (i // tiles, 0, i % tiles, 0))
    return pl.pallas_call(
        _proj_kernel,
        grid=(n // tm,),
        in_specs=[rows(D_MODEL), pl.BlockSpec(memory_space=pl.ANY)]
                 + [full((1, A_WIDTH)), full((1, A_WIDTH)), full((8, A_CHUNK, A_CHUNK)), full((A_CHUNK, A_WIDTH))],
        out_specs=[rows(A_WIDTH), rows(4 * COL), rows(3 * COL), dil(4), dil(16)],
        out_shape=[jax.ShapeDtypeStruct((n, A_WIDTH), BF16),
                   jax.ShapeDtypeStruct((n, 4 * COL), BF16),
                   jax.ShapeDtypeStruct((n, 3 * COL), BF16),
                   jax.ShapeDtypeStruct((bsz, 4, s // 4, 3 * COL), BF16),
                   jax.ShapeDtypeStruct((bsz, 16, s // 16, 3 * COL), BF16)],
        scratch_shapes=[pltpu.VMEM((D_MODEL // 128, tm, 128), F32),
                        pltpu.VMEM((W_IN_BLOCKS, D_MODEL, COL), BF16),
                        pltpu.VMEM((W_IN_SLOTS, D_MODEL, COL), F32),
                        pltpu.SemaphoreType.DMA((W_IN_SLOTS,))],
        compiler_params=pltpu.CompilerParams(dimension_semantics=("arbitrary",), vmem_limit_bytes=VMEM_LIMIT),
        name="proj",
    )(x2, w_in, a_ln_g, a_ln_b, a_ws, a_bias)


def _attn_kernel(qkv_ref, o_ref, lse_ref, *, ns, seq):
    nb = seq // SPAN
    lane = lax.broadcasted_iota(I32, (SPAN, 128), 1)
    lo = lane < 64
    lane16 = lane // 16
    qi = lax.broadcasted_iota(I32, (SPAN, 2 * SPAN), 0)
    ki = lax.broadcasted_iota(I32, (SPAN, 2 * SPAN), 1)
    causal = lax.broadcasted_iota(I32, (SPAN, SPAN), 1) <= lax.broadcasted_iota(I32, (SPAN, SPAN), 0)
    bias_first = jnp.where(causal, 0.0, NEG).astype(F32)
    bias_first = jnp.concatenate([bias_first, bias_first], axis=0)
    bias_main = jnp.where((ki >= qi) & (ki <= qi + SPAN), 0.0, NEG).astype(F32)
    bias_main = jnp.concatenate([bias_main, bias_main], axis=0)
    zero = jnp.zeros((SPAN, 128), BF16)

    for s in range(ns):
        def block(row0, start, bias, s=s):
            win = bias.shape[1]
            pairs = range(B_HEADS // 2)
            scores, values = [], []
            for jp in pairs:
                c0 = jp * 128
                q = qkv_ref[s, pl.ds(row0, SPAN), c0:c0 + 128] * jnp.asarray(0.125, BF16)
                k = qkv_ref[s, pl.ds(start, win), COL + c0:COL + c0 + 128]
                values.append(qkv_ref[s, pl.ds(start, win), 2 * COL + c0:2 * COL + c0 + 128])
                qs = jnp.concatenate([jnp.where(lo, q, zero), jnp.where(lo, zero, q)], axis=0)
                scores.append(lax.dot_general(qs, k, (((1,), (1,)), ((), ())), preferred_element_type=F32) + bias)
            probs, maxes, sums = [], [], []
            for jp in pairs:
                m = jnp.max(scores[jp], axis=-1, keepdims=True)
                p = jnp.exp(scores[jp] - m)
                maxes.append(m)
                sums.append(jnp.sum(p, axis=-1, keepdims=True))
                probs.append(p.astype(BF16))
            lse_tile = jnp.zeros((SPAN, 128), F32)
            for jp in pairs:
                c0 = jp * 128
                ov = _dot(probs[jp], values[jp])
                inv = 1.0 / sums[jp]
                o = jnp.where(lo, ov[:SPAN] * inv[:SPAN], ov[SPAN:] * inv[SPAN:])
                o_ref[pl.ds(row0, SPAN), s * B_WIDTH + c0:s * B_WIDTH + c0 + 128] = o.astype(BF16)
                lse = maxes[jp] + jnp.log(sums[jp])
                lse_tile = jnp.where(lane16 == 2 * jp, lse[:SPAN],
                                     jnp.where(lane16 == 2 * jp + 1, lse[SPAN:], lse_tile))
            lse_ref[pl.ds(row0, SPAN), s * 128:(s + 1) * 128] = lse_tile

        block(0, 0, bias_first)
        if nb > 1:
            def body(i, carry):
                block(pl.multiple_of(i * SPAN, SPAN), pl.multiple_of((i - 1) * SPAN, SPAN), bias_main)
                return carry
            lax.fori_loop(1, nb, body, 0, unroll=min(5, nb - 1))


def _attn(qkv_g):
    bsz, dl, seq, _ = qkv_g.shape
    ns = max(1, min(dl, TM_ATTN // seq))
    return pl.pallas_call(
        functools.partial(_attn_kernel, ns=ns, seq=seq),
        grid=(bsz, dl // ns),
        in_specs=[pl.BlockSpec((None, ns, seq, 3 * COL), lambda b, r: (b, r, 0, 0))],
        out_specs=[pl.BlockSpec((None, seq, ns * B_WIDTH), lambda b, r: (b, 0, r)),
                   pl.BlockSpec((None, seq, ns * 128), lambda b, r: (b, 0, r))],
        out_shape=[jax.ShapeDtypeStruct((bsz, seq, dl * B_WIDTH), BF16),
                   jax.ShapeDtypeStruct((bsz, seq, dl * 128), F32)],
        compiler_params=pltpu.CompilerParams(dimension_semantics=("parallel", "parallel"),
                                             vmem_limit_bytes=VMEM_LIMIT),
        name=f"attn{dl}",
    )(qkv_g)


def _natural_rows(ref, dl, scr):
    nchunk, tm, _ = scr.shape
    w = nchunk * 128
    per = tm // dl
    for r in range(dl):
        for c in range(nchunk):
            scr[c, pl.ds(r, per, stride=dl), :] = ref[:, r * w + c * 128:r * w + (c + 1) * 128].astype(F32)
    return jnp.concatenate([scr[c] for c in range(nchunk)], axis=1)


def _mix_kernel(ga_ref, gates_ref, o1_ref, o2_ref, o3_ref, l1_ref, l2_ref, l3_ref, x_ref,
                wa_ref, wb_ref, wo_ref, wr_ref, br_ref, g1_ref, b1_ref,
                x1_ref, x1p_ref, routet_ref, o2s_ref, o3s_ref, l2s_ref, l3s_ref):
    tm = x_ref.shape[0]
    o2 = _natural_rows(o2_ref, 4, o2s_ref)
    o3 = _natural_rows(o3_ref, 16, o3s_ref)
    l2 = _natural_rows(l2_ref, 4, l2s_ref)
    l3 = _natural_rows(l3_ref, 16, l3s_ref)
    er = lax.broadcasted_iota(I32, (256, B_WIDTH), 0)
    ec = lax.broadcasted_iota(I32, (256, B_WIDTH), 1)
    expand = jnp.where(er % 128 == (ec // B_HEAD_DIM) * 16, 1.0, 0.0).astype(BF16)

    def widen(w):
        hi = w.astype(BF16)
        lo = (w - hi.astype(F32)).astype(BF16)
        return _dot(jnp.concatenate([hi, lo], axis=1), expand)

    h = tm // 2
    halves = (slice(0, h), slice(h, tm))
    obs = []
    for r in halves:
        l1 = l1_ref[r, :]
        mx = jnp.maximum(l1, jnp.maximum(l2[r], l3[r]))
        e1, e2, e3 = jnp.exp(l1 - mx), jnp.exp(l2[r] - mx), jnp.exp(l3[r] - mx)
        inv = 1.0 / (e1 + e2 + e3)
        obs.append(widen(e1 * inv) * o1_ref[r, :].astype(F32) + widen(e2 * inv) * o2[r] + widen(e3 * inv) * o3[r])
    ybs = [_dot(ob.astype(BF16), wb_ref[...]) for ob in obs]
    yas = [_dot(ga_ref[r, :], wa_ref[...]) for r in halves]
    pres = [gates_ref[r, :D_MODEL].astype(F32) * ya + gates_ref[r, D_MODEL:].astype(F32) * yb
            for r, ya, yb in zip(halves, yas, ybs)]
    mixes = [_dot(pre.astype(BF16), wo_ref[...]) for pre in pres]
    x1s = [_ln(DEEPNORM_ALPHA * x_ref[r, :] + mix, g1_ref[...], b1_ref[...]) for r, mix in zip(halves, mixes)]
    logits = [_dot(x1.astype(BF16), wr_ref[...]) + br_ref[...] for x1 in x1s]

    nrow = 40
    row = lax.broadcasted_iota(I32, (nrow, h), 0).astype(F32)
    row8 = lax.broadcasted_iota(I32, (8, h), 0)
    big = 1e9
    for i, r in enumerate(halves):
        x1_ref[r, :] = x1s[i]
        _store_packed_rows(x1p_ref.at[i * h // 8:(i + 1) * h // 8], x1s[i])
        lg = logits[i].T[:nrow, :]
        gl = jnp.where(row < N_GROUPS, lg, NEG)
        gm = jnp.max(gl, axis=0, keepdims=True)
        gidx = jnp.min(jnp.where(gl == gm, row, big), axis=0, keepdims=True)
        gsum = jnp.sum(jnp.where(row < N_GROUPS, jnp.exp(gl - gm), 0.0), axis=0, keepdims=True)
        gprob = 1.0 / gsum
        lo_row = N_GROUPS + N_EXPERTS * gidx
        el = jnp.where((row >= lo_row) & (row < lo_row + N_EXPERTS), lg, NEG)
        v1 = jnp.max(el, axis=0, keepdims=True)
        i1 = jnp.min(jnp.where(el == v1, row, big), axis=0, keepdims=True)
        el2 = jnp.where(row == i1, NEG, el)
        v2 = jnp.max(el2, axis=0, keepdims=True)
        i2 = jnp.min(jnp.where(el2 == v2, row, big), axis=0, keepdims=True)
        t = jnp.exp(v2 - v1)
        w1 = 1.0 / (1.0 + t)
        w2 = t * w1
        routet_ref[:, r] = jnp.where(row8 == 0, i1 - N_GROUPS,
                                     jnp.where(row8 == 1, i2 - N_GROUPS,
                                               jnp.where(row8 == 2, gprob * w1,
                                                         jnp.where(row8 == 3, gprob * w2, 0.0))))


def _mix(ga, gates, o1, o2, o3, l1, l2, l3, x2, wa, wb, wo, wr, br, g1, b1):
    n = x2.shape[0]
    bsz = o2.shape[0]
    tm = TM_MIX
    tiles = n // bsz // tm
    rows = lambda w: pl.BlockSpec((tm, w), lambda i: (i, 0))
    grouped = lambda a, dl: pl.BlockSpec((None, tm // dl, a.shape[2]), lambda i: (i // tiles, i % tiles, 0))
    full = lambda a: pl.BlockSpec(a.shape, lambda i: (0,) * a.ndim)
    return pl.pallas_call(
        _mix_kernel,
        grid=(n // tm,),
        in_specs=[rows(A_WIDTH), rows(2 * D_MODEL), rows(B_WIDTH), grouped(o2, 4), grouped(o3, 16),
                  rows(128), grouped(l2, 4), grouped(l3, 16), rows(D_MODEL),
                  full(wa), full(wb), full(wo), full(wr), full(br), full(g1), full(b1)],
        out_specs=[rows(D_MODEL), pl.BlockSpec((tm // 8, 32, 128), lambda i: (i, 0, 0)),
                   pl.BlockSpec((8, tm), lambda i: (0, i))],
        out_shape=[jax.ShapeDtypeStruct((n, D_MODEL), F32),
                   jax.ShapeDtypeStruct((n // 8, 32, 128), U32),
                   jax.ShapeDtypeStruct((8, n), F32)],
        scratch_shapes=[pltpu.VMEM((B_WIDTH // 128, tm, 128), F32), pltpu.VMEM((B_WIDTH // 128, tm, 128), F32),
                        pltpu.VMEM((1, tm, 128), F32), pltpu.VMEM((1, tm, 128), F32)],
        compiler_params=pltpu.CompilerParams(dimension_semantics=("parallel",), vmem_limit_bytes=VMEM_LIMIT),
        name="mix",
    )(ga, gates, o1, o2, o3, l1, l2, l3, x2, wa, wb, wo, wr, br, g1, b1)


SC_WINDOW = 128


def _sc_mesh():
    return plsc.VectorSubcoreMesh(core_axis_name="core", subcore_axis_name="subcore")


def _sc_scatter_rows(rows, dst, n_out):
    r = rows.shape[0]

    @pl.kernel(out_type=jax.ShapeDtypeStruct((n_out, 128), rows.dtype), mesh=_sc_mesh())
    def scatter(rows_hbm, dst0_hbm, dst1_hbm, out_hbm):
        def body(rows_vmem, dst0_vmem, dst1_vmem):
            pltpu.sync_copy(rows_vmem, out_hbm.at[dst0_vmem.at[0]])
            pltpu.sync_copy(rows_vmem, out_hbm.at[dst1_vmem.at[0]])

        pltpu.emit_pipeline(
            body,
            grid=(r // SC_WINDOW,),
            in_specs=[pl.BlockSpec((SC_WINDOW, 128), lambda i: (i, 0)),
                      pl.BlockSpec((1, SC_WINDOW), lambda i: (0, i)),
                      pl.BlockSpec((1, SC_WINDOW), lambda i: (0, i))],
            out_specs=[],
            core_axis_name=("core", "subcore"),
            dimension_semantics=(pltpu.PARALLEL,),
        )(rows_hbm, dst0_hbm, dst1_hbm)

    return scatter(rows, dst[0:1], dst[1:2])


def _sc_gather_rows(table, src):
    m = src.shape[0]
    k = 2

    @pl.kernel(out_type=jax.ShapeDtypeStruct((m, 128), table.dtype), mesh=_sc_mesh(),
               scratch_types=[pltpu.SemaphoreType.DMA((k,))])
    def gather(table_hbm, src_hbm, out_hbm, sems):
        def body(src_vmem, out_vmem):
            copies = [pltpu.async_copy(table_hbm.at[src_vmem.at[j]], out_vmem.at[pl.ds(j * SC_WINDOW, SC_WINDOW)],
                                       sems.at[j]) for j in range(k)]
            for c in copies:
                c.wait()

        pltpu.emit_pipeline(
            body,
            grid=(m // (k * SC_WINDOW),),
            in_specs=[pl.BlockSpec((k, SC_WINDOW), lambda i: (i, 0))],
            out_specs=[pl.BlockSpec((k * SC_WINDOW, 128), lambda i: (i, 0))],
            core_axis_name=("core", "subcore"),
            dimension_semantics=(pltpu.PARALLEL,),
        )(src_hbm, out_hbm)

    return gather(table, src.reshape(m // SC_WINDOW, SC_WINDOW))


MOE_X_SLOTS = 3
MOE_Y_SLOTS = 2


def _moe_kernel(tiles_ref, xs_hbm, wg_hbm, wu_hbm, wd_hbm, ys_hbm,
                xbuf, ybuf, wgf_ref, wuf_ref, wdf_ref, wgb_ref, wub_ref, wdb_ref, xsem, ysem, wsem, *, nt):
    te_ref, tv_ref, ws_ref, nx_ref = (tiles_ref.at[i] for i in range(4))
    t = pl.program_id(0)
    grp = TM_MOE // 8

    def fetch_x(j):
        return pltpu.make_async_copy(xs_hbm.at[pl.ds(j * grp, grp)], xbuf.at[j % MOE_X_SLOTS],
                                     xsem.at[j % MOE_X_SLOTS])

    def store_y(j):
        return pltpu.make_async_copy(ybuf.at[j % MOE_Y_SLOTS], ys_hbm.at[pl.ds(j * grp, grp)],
                                     ysem.at[j % MOE_Y_SLOTS])

    def fetch_w(e, slot):
        return [pltpu.make_async_copy(w.at[e], buf.at[slot], wsem.at[slot, i])
                for i, (w, buf) in enumerate(((wg_hbm, wgf_ref), (wu_hbm, wuf_ref), (wd_hbm, wdf_ref)))]

    @pl.when(t == 0)
    def _():
        for c in fetch_w(te_ref[0], 0):
            c.start()
        for j in range(MOE_X_SLOTS - 1):
            @pl.when(tv_ref[j] == 1)
            def _():
                fetch_x(j).start()

    ahead = t + (MOE_X_SLOTS - 1)

    @pl.when((ahead < nt) & (tv_ref[jnp.minimum(ahead, nt - 1)] == 1))
    def _():
        fetch_x(ahead).start()

    slot = ws_ref[t]

    @pl.when((t == 0) | (te_ref[t] != te_ref[jnp.maximum(t - 1, 0)]))
    def _():
        @pl.when(nx_ref[t] >= 0)
        def _():
            for c in fetch_w(nx_ref[t], 1 - slot):
                c.start()

        for c in fetch_w(te_ref[t], slot):
            c.wait()
        wgb_ref[...] = wgf_ref[slot].astype(BF16)
        wub_ref[...] = wuf_ref[slot].astype(BF16)
        wdb_ref[...] = wdf_ref[slot].astype(BF16)

    @pl.when((t >= MOE_Y_SLOTS) & (tv_ref[jnp.maximum(t - MOE_Y_SLOTS, 0)] == 1))
    def _():
        store_y(t - MOE_Y_SLOTS).wait()

    @pl.when(tv_ref[t] == 1)
    def _():
        fetch_x(t).wait()
        xb = _load_packed_rows(xbuf.at[t % MOE_X_SLOTS]).astype(BF16)
        g = _dot(xb, wgb_ref[...])
        u = _dot(xb, wub_ref[...])
        h = (g * _sigmoid(g) * u).astype(BF16)
        _store_packed_rows(ybuf.at[t % MOE_Y_SLOTS], _dot(h, wdb_ref[...]))
        store_y(t).start()

    @pl.when(t == nt - 1)
    def _():
        for j in range(nt - MOE_Y_SLOTS, nt):
            @pl.when(tv_ref[j] == 1)
            def _():
                store_y(j).wait()


def _moe(tiles, nt, xs, wg, wu, wd):
    tm = TM_MOE
    any_spec = pl.BlockSpec(memory_space=pl.ANY)
    return pl.pallas_call(
        functools.partial(_moe_kernel, nt=nt),
        grid_spec=pltpu.PrefetchScalarGridSpec(
            num_scalar_prefetch=1,
            grid=(nt,),
            in_specs=[any_spec] * 4,
            out_specs=any_spec,
            scratch_shapes=[pltpu.VMEM((MOE_X_SLOTS, tm // 8, 32, 128), U32),
                            pltpu.VMEM((MOE_Y_SLOTS, tm // 8, 32, 128), U32),
                            pltpu.VMEM((2, D_MODEL, D_EXPERT), F32), pltpu.VMEM((2, D_MODEL, D_EXPERT), F32),
                            pltpu.VMEM((2, D_EXPERT, D_MODEL), F32),
                            pltpu.VMEM((D_MODEL, D_EXPERT), BF16), pltpu.VMEM((D_MODEL, D_EXPERT), BF16),
                            pltpu.VMEM((D_EXPERT, D_MODEL), BF16),
                            pltpu.SemaphoreType.DMA((MOE_X_SLOTS,)), pltpu.SemaphoreType.DMA((MOE_Y_SLOTS,)),
                            pltpu.SemaphoreType.DMA((2, 3))]),
        out_shape=jax.ShapeDtypeStruct((nt * tm // 8, 32, 128), U32),
        compiler_params=pltpu.CompilerParams(dimension_semantics=("arbitrary",), vmem_limit_bytes=VMEM_LIMIT),
        name="moe",
    )(tiles, xs, wg, wu, wd)


def _final_kernel(y0_ref, y1_ref, x1_ref, p_ref, routet_ref, wple_ref, wpg_ref, g2_ref, b2_ref, out_ref):
    tm = x1_ref.shape[0]
    route = routet_ref[...].T
    h = tm // 4
    parts = tuple(slice(i * h, (i + 1) * h) for i in range(4))
    x1s = [x1_ref[r, :] for r in parts]
    plins = [_dot(p_ref[r, :].astype(BF16), wple_ref[...]) for r in parts]
    gates = [_dot(x1.astype(BF16), wpg_ref[...]) for x1 in x1s]
    sums = []
    for i, r in enumerate(parts):
        g8 = slice(i * h // 8, (i + 1) * h // 8)
        sums.append(DEEPNORM_ALPHA * x1s[i] + route[r, 2:3] * _load_packed_rows(y0_ref.at[g8])
                    + route[r, 3:4] * _load_packed_rows(y1_ref.at[g8]))
    for i, r in enumerate(parts):
        out_ref[r, :] = _ln((sums[i] + plins[i]) + plins[i] * jnp.tanh(gates[i]), g2_ref[...], b2_ref[...])


def _final(yg, x1, p2, routet, wple, wpg, g2, b2):
    n = x1.shape[0]
    tm = TM_FIN
    nt = n // tm
    rows = lambda w: pl.BlockSpec((tm, w), lambda t: (t, 0))
    full = lambda a: pl.BlockSpec(a.shape, lambda t: (0,) * a.ndim)
    return pl.pallas_call(
        _final_kernel,
        grid=(nt,),
        in_specs=[pl.BlockSpec((tm // 8, 32, 128), lambda t: (t, 0, 0)),
                  pl.BlockSpec((tm // 8, 32, 128), lambda t: (t + nt, 0, 0)),
                  rows(D_MODEL), rows(PLE_DIM), pl.BlockSpec((8, tm), lambda t: (0, t)),
                  full(wple), full(wpg), full(g2), full(b2)],
        out_specs=rows(D_MODEL),
        out_shape=jax.ShapeDtypeStruct((n, D_MODEL), F32),
        compiler_params=pltpu.CompilerParams(dimension_semantics=("parallel",), vmem_limit_bytes=VMEM_LIMIT),
        name="final",
    )(yg, yg, x1, p2, routet, wple, wpg, g2, b2)


def _route_tables_kernel(e_ref, piece_ref, tab_ref):
    r = e_ref.shape[0]
    e = e_ref[...]
    ri = lax.broadcasted_iota(I32, (128, 128), 0)
    ci = lax.broadcasted_iota(I32, (128, 128), 1)
    upper = jnp.where(ri <= ci, 1.0, 0.0).astype(BF16)
    rr = lax.broadcasted_iota(I32, (r, r), 0)
    rc = lax.broadcasted_iota(I32, (r, r), 1)
    below = jnp.where(rc < rr, 1.0, 0.0).astype(BF16)
    lane = lax.broadcasted_iota(I32, (1, 128), 1)

    rank = jnp.zeros((r, 128), F32)
    counts = jnp.zeros((1, 128), F32)
    for x in range(N_EXPERTS_TOTAL):
        m = jnp.where(e == x, 1.0, 0.0)
        pre = _dot(m.astype(BF16), upper)
        tot = jnp.broadcast_to(pre[:, 127:128], (r, 128))
        off = _dot(below, tot.astype(BF16))
        rank = rank + m * (pre + off)
        counts = jnp.where(lane == x, off[r - 1:r, :] + tot[r - 1:r, :], counts)
    padded = jnp.floor((counts + (TM_MOE - 1)) * (1.0 / TM_MOE)) * TM_MOE
    ends = _dot(jnp.broadcast_to(padded, (8, 128)).astype(BF16), upper)[0:1, :]
    offs = ends - padded

    nt = r * 128 // TM_MOE + N_EXPERTS_TOTAL
    ones = jnp.ones((128, 128), BF16)
    lanef = lane.astype(F32)
    diag = ri == ci
    ends_t = ends * (1.0 / TM_MOE)
    ecol = _dot(jnp.where(diag, jnp.broadcast_to(ends_t, (128, 128)), 0.0).astype(BF16), ones)
    passed = jnp.where((ecol <= ci.astype(F32)) & (ri < N_EXPERTS_TOTAL), 1.0, 0.0)
    te = jnp.minimum(jnp.sum(passed, axis=0, keepdims=True), N_EXPERTS_TOTAL - 1.0)
    tv = jnp.where(lanef < ecol[N_EXPERTS_TOTAL - 1:N_EXPERTS_TOTAL, :], 1.0, 0.0)
    shift = jnp.where(ri + 1 == ci, 1.0, 0.0).astype(BF16)
    te_prev = _dot(jnp.broadcast_to(te, (8, 128)).astype(BF16), shift)[0:1, :]
    first = jnp.where((lane == 0) | (te != te_prev), 1.0, 0.0)
    run = _dot(jnp.broadcast_to(first, (8, 128)).astype(BF16), upper)[0:1, :] - 1.0
    ws = run - 2.0 * jnp.floor(run * 0.5)
    tcol = _dot(jnp.where(diag, jnp.broadcast_to(te, (128, 128)), 0.0).astype(BF16), ones)
    later = jnp.where((tcol > te) & (ri < nt), tcol, float(N_EXPERTS_TOTAL))
    nx = jnp.min(later, axis=0, keepdims=True)
    nx = jnp.where(nx == N_EXPERTS_TOTAL, -1.0, nx)
    row8 = lax.broadcasted_iota(I32, (8, 128), 0)
    tab = jnp.where(row8 == 0, te, jnp.where(row8 == 1, tv, jnp.where(row8 == 2, ws, jnp.where(row8 == 3, nx, 0.0))))
    tab_ref[...] = tab.astype(I32)

    pos = rank - 1.0
    for x in range(N_EXPERTS_TOTAL):
        pos = pos + jnp.where(e == x, offs[:, x:x + 1], 0.0)

    hi = jnp.floor(pos * (1.0 / 256.0))
    lo = pos - 256.0 * hi
    jv = ((lane % 32) // 8).astype(F32)
    for c in range(4):
        sel = jnp.where(ri == 32 * c + 8 * (ci // 32) + ci % 8, 1.0, 0.0).astype(BF16)
        pc = 256.0 * _dot(hi.astype(BF16), sel) + _dot(lo.astype(BF16), sel)
        p8 = jnp.floor(pc * 0.125)
        piece = p8 * (8.0 * SUBROWS) + (pc - 8.0 * p8) + 8.0 * jv
        piece_ref[pl.ds(c, r, stride=4), :] = piece.astype(I32)


def _routing_tables(routet, n):
    nt = (2 * n) // TM_MOE + N_EXPERTS_TOTAL
    assert nt <= 128, "the tile tables hold one tile per lane"
    r = 2 * n // 128
    piece, tiles = pl.pallas_call(
        _route_tables_kernel,
        out_shape=[jax.ShapeDtypeStruct((4 * r, 128), I32), jax.ShapeDtypeStruct((8, 128), I32)],
        compiler_params=pltpu.CompilerParams(vmem_limit_bytes=VMEM_LIMIT),
        name="route_tables",
    )(routet[0:2].reshape(r, 128))
    return tiles, nt, piece.reshape(2, n * SUBROWS)


def kernel(x, p, w_in, a_ln_g, a_ln_b, a_ws, a_bs, w_a_proj, w_b_proj, w_o, ln1_g, ln1_b, w_group_router,
           b_group_router, w_expert_router, b_expert_router, w_gate, w_up, w_down, w_ple, w_ple_gate,
           ln2_g, ln2_b):
    bsz, s, d = x.shape
    n = bsz * s
    assert d == D_MODEL and s % (SPAN * max(B_DILATIONS)) == 0 and n % TM_PROJ == 0
    assert w_in.shape[0] == 1, "one layer"

    a_bias = jnp.repeat(a_bs[0].T, A_WIDTH // 8, axis=1)

    ga, gates, qkv1, qkv2, qkv3 = _proj(x, w_in[0], a_ln_g, a_ln_b, a_ws[0], a_bias)
    o1, l1 = _attn(qkv1.reshape(bsz, 1, s, 3 * COL))
    o2, l2 = _attn(qkv2)
    o3, l3 = _attn(qkv3)

    pad = 128 - N_GROUPS - N_EXPERTS_TOTAL
    wr = jnp.concatenate([w_group_router[0], w_expert_router[0].reshape(d, N_EXPERTS_TOTAL),
                          jnp.zeros((d, pad), F32)], axis=1).astype(BF16)
    br = jnp.concatenate([b_group_router[0], b_expert_router[0].reshape(-1), jnp.zeros((pad,), F32)])[None, :]
    x1, x1p, routet = _mix(
        ga, gates, o1.reshape(n, B_WIDTH), o2, o3, l1.reshape(n, 128), l2, l3, x.reshape(n, d),
        w_a_proj[0].astype(BF16), w_b_proj[0].astype(BF16), w_o[0].astype(BF16), wr, br, ln1_g, ln1_b)

    tiles, nt, piece = _routing_tables(routet, n)
    xs = _sc_scatter_rows(x1p.reshape(n * SUBROWS, 128), piece, nt * TM_MOE * SUBROWS)
    ys = _moe(tiles, nt, xs.reshape(nt * TM_MOE // 8, 32, 128),
              w_gate[0].reshape(N_EXPERTS_TOTAL, d, D_EXPERT), w_up[0].reshape(N_EXPERTS_TOTAL, d, D_EXPERT),
              w_down[0].reshape(N_EXPERTS_TOTAL, D_EXPERT, d)).reshape(nt * TM_MOE * SUBROWS, 128)
    yg = _sc_gather_rows(ys, piece.reshape(-1))
    out = _final(yg.reshape(n // 4, 32, 128), x1, p[0].reshape(n, PLE_DIM), routet,
                 (0.5 * w_ple[0]).astype(BF16), (0.5 * w_ple_gate[0]).astype(BF16), ln2_g, ln2_b)
    return out.reshape(bsz, s, d)
```

```python
import functools

import jax
import jax.numpy as jnp
from jax import lax
from jax.experimental import pallas as pl
from jax.experimental.pallas import tpu as pltpu
from jax.experimental.pallas import tpu_sc as plsc

F32 = jnp.float32
BF16 = jnp.bfloat16
U32 = jnp.uint32
I32 = jnp.int32

D_MODEL = 1024
PLE_DIM = 256
A_WIDTH = 512
A_CHUNK = 128
B_HEAD_DIM = 64
B_HEADS = 8
B_WIDTH = 512
B_DILATIONS = (1, 4, 16)
SPAN = 128
N_GROUPS = 4
N_EXPERTS = 8
N_EXPERTS_TOTAL = N_GROUPS * N_EXPERTS
D_EXPERT = 256
DEEPNORM_ALPHA = 2.0 ** 0.25
LN_EPS = 1e-5
COL = 512
NEG = -1e30

VMEM_LIMIT = 56 * 1024 * 1024

TM_PROJ = 512
TM_ATTN = 1024
TM_MIX = 512
TM_MOE = 512
TM_FIN = 1024


def _ln(x, g, b):
    mu = jnp.mean(x, axis=-1, keepdims=True)
    xc = x - mu
    var = jnp.mean(xc * xc, axis=-1, keepdims=True)
    return xc * lax.rsqrt(var + LN_EPS) * g + b


def _gelu_tanh(x):
    return 0.5 * x * (1.0 + jnp.tanh(0.7978845608028654 * (x + 0.044715 * (x * x * x))))


def _sigmoid(x):
    return 0.5 * jnp.tanh(0.5 * x) + 0.5


def _dot(a, b):
    return jnp.dot(a, b, preferred_element_type=F32)


PACK_W = D_MODEL // 2
SUBROWS = PACK_W // 128


def _store_packed_rows(ref, x):
    m = x.shape[0]
    xb = x.astype(BF16).astype(F32)
    lo = pltpu.bitcast(xb[:, :PACK_W], U32) >> 16
    hi = pltpu.bitcast(xb[:, PACK_W:], U32) & jnp.uint32(0xFFFF0000)
    w = hi | lo
    for j in range(SUBROWS):
        ref[:, 8 * j:8 * (j + 1), :] = w[:, 128 * j:128 * (j + 1)].reshape(m // 8, 8, 128)


def _load_packed_rows(ref):
    m = ref.shape[0] * 8
    ws = [ref[:, 8 * j:8 * (j + 1), :].reshape(m, 128) for j in range(SUBROWS)]
    lo = [pltpu.bitcast(w << 16, F32) for w in ws]
    hi = [pltpu.bitcast(w & jnp.uint32(0xFFFF0000), F32) for w in ws]
    return jnp.concatenate(lo + hi, axis=1)


W_IN_BLOCKS = 15
W_IN_SLOTS = 4


def _proj_kernel(x_ref, w_hbm, lng_ref, lnb_ref, ws_ref, bias_ref,
                 ga_ref, gates_ref, qkv1_ref, qkv2_ref, qkv3_ref, xc_ref, w, wstage_ref, wsem):
    tm = x_ref.shape[0]

    @pl.when(pl.program_id(0) == 0)
    def _():
        def fetch(j):
            return pltpu.make_async_copy(w_hbm.at[:, pl.ds(j * COL, COL)], wstage_ref.at[j % W_IN_SLOTS],
                                         wsem.at[j % W_IN_SLOTS])

        for j in range(W_IN_SLOTS):
            fetch(j).start()
        for j in range(W_IN_BLOCKS):
            fetch(j).wait()
            w[j] = wstage_ref[j % W_IN_SLOTS].astype(BF16)
            if j + W_IN_SLOTS < W_IN_BLOCKS:
                fetch(j + W_IN_SLOTS).start()

    xb = x_ref[...].astype(BF16)

    u_raw = _dot(xb, w[0])
    v_raw = _dot(xb, w[1])

    for i in range(4):
        gates_ref[:, i * COL:(i + 1) * COL] = _sigmoid(_dot(xb, w[11 + i])).astype(BF16)
    for j in range(3):
        qkv1_ref[:, j * COL:(j + 1) * COL] = _dot(xb, w[2 + 3 * j]).astype(BF16)

    for c in range(D_MODEL // 128):
        xc_ref[c] = x_ref[:, c * 128:(c + 1) * 128]
    for gi, out_ref in ((1, qkv2_ref), (2, qkv3_ref)):
        dl = B_DILATIONS[gi]
        per = tm // dl
        xp = jnp.concatenate(
            [jnp.concatenate([xc_ref[c, pl.ds(r, per, stride=dl), :] for c in range(D_MODEL // 128)], axis=1)
             for r in range(dl)], axis=0).astype(BF16)
        for j in range(3):
            res = _dot(xp, w[2 + 3 * j + gi]).astype(BF16)
            for r in range(dl):
                out_ref[r, :, j * COL:(j + 1) * COL] = res[r * per:(r + 1) * per]

    u = _gelu_tanh(u_raw)
    v = _gelu_tanh(v_raw)
    vn = _ln(v, lng_ref[...], lnb_ref[...]).astype(BF16)

    row = lax.broadcasted_iota(I32, (A_CHUNK, A_CHUNK), 0)
    colm = lax.broadcasted_iota(I32, (A_CHUNK, A_CHUNK), 1)
    causal = colm <= row
    lo = colm < 64
    zero = jnp.zeros((A_CHUNK, A_CHUNK), BF16)
    wcat = []
    for j in range(4):
        w0 = jnp.where(causal, ws_ref[2 * j], 0.0).astype(BF16)
        w1 = jnp.where(causal, ws_ref[2 * j + 1], 0.0).astype(BF16)
        wcat.append(jnp.concatenate([w0, w1], axis=1))
    for c in range(tm // A_CHUNK):
        r0 = c * A_CHUNK
        for j in range(4):
            c0 = j * 128
            vt = vn[r0:r0 + A_CHUNK, c0:c0 + 128]
            rhs = jnp.concatenate([jnp.where(lo, vt, zero), jnp.where(lo, zero, vt)], axis=0)
            mixed = _dot(wcat[j], rhs) + bias_ref[:, c0:c0 + 128]
            ga_ref[r0:r0 + A_CHUNK, c0:c0 + 128] = (u[r0:r0 + A_CHUNK, c0:c0 + 128] * mixed).astype(BF16)


def _proj(x, w_in, a_ln_g, a_ln_b, a_ws, a_bias):
    bsz, s, _ = x.shape
    n = bsz * s
    tm = TM_PROJ
    tiles = s // tm
    x2 = x.reshape(n, D_MODEL)
    full = lambda shape: pl.BlockSpec(shape, lambda i: (0,) * len(shape))
    rows = lambda width: pl.BlockSpec((tm, width), lambda i: (i, 0))
    dil = lambda dl: pl.BlockSpec((None, dl, tm // dl, 3 * COL), lambda i: (i // tiles, 0, i % tiles, 0))
    return pl.pallas_call(
        _proj_kernel,
        grid=(n // tm,),
        in_specs=[rows(D_MODEL), pl.BlockSpec(memory_space=pl.ANY)]
                 + [full((1, A_WIDTH)), full((1, A_WIDTH)), full((8, A_CHUNK, A_CHUNK)), full((A_CHUNK, A_WIDTH))],
        out_specs=[rows(A_WIDTH), rows(4 * COL), rows(3 * COL), dil(4), dil(16)],
        out_shape=[jax.ShapeDtypeStruct((n, A_WIDTH), BF16),
                   jax.ShapeDtypeStruct((n, 4 * COL), BF16),
                   jax.ShapeDtypeStruct((n, 3 * COL), BF16),
                   jax.ShapeDtypeStruct((bsz, 4, s // 4, 3 * COL), BF16),
                   jax.ShapeDtypeStruct((bsz, 16, s // 16, 3 * COL), BF16)],
        scratch_shapes=[pltpu.VMEM((D_MODEL // 128, tm, 128), F32),
                        pltpu.VMEM((W_IN_BLOCKS, D_MODEL, COL), BF16),
                        pltpu.VMEM((W_IN_SLOTS, D_MODEL, COL), F32),
                        pltpu.SemaphoreType.DMA((W_IN_SLOTS,))],
        compiler_params=pltpu.CompilerParams(dimension_semantics=("arbitrary",), vmem_limit_bytes=VMEM_LIMIT),
        name="proj",
    )(x2, w_in, a_ln_g, a_ln_b, a_ws, a_bias)


def _attn_kernel(qkv_ref, o_ref, lse_ref, *, ns, seq):
    nb = seq // SPAN
    lane = lax.broadcasted_iota(I32, (SPAN, 128), 1)
    lo = lane < 64
    lane16 = lane // 16
    qi = lax.broadcasted_iota(I32, (SPAN, 2 * SPAN), 0)
    ki = lax.broadcasted_iota(I32, (SPAN, 2 * SPAN), 1)
    causal = lax.broadcasted_iota(I32, (SPAN, SPAN), 1) <= lax.broadcasted_iota(I32, (SPAN, SPAN), 0)
    bias_first = jnp.where(causal, 0.0, NEG).astype(F32)
    bias_first = jnp.concatenate([bias_first, bias_first], axis=0)
    bias_main = jnp.where((ki >= qi) & (ki <= qi + SPAN), 0.0, NEG).astype(F32)
    bias_main = jnp.concatenate([bias_main, bias_main], axis=0)
    zero = jnp.zeros((SPAN, 128), BF16)

    for s in range(ns):
        def block(row0, start, bias, s=s):
            win = bias.shape[1]
            pairs = range(B_HEADS // 2)
            scores, values = [], []
            for jp in pairs:
                c0 = jp * 128
                q = qkv_ref[s, pl.ds(row0, SPAN), c0:c0 + 128] * jnp.asarray(0.125, BF16)
                k = qkv_ref[s, pl.ds(start, win), COL + c0:COL + c0 + 128]
                values.append(qkv_ref[s, pl.ds(start, win), 2 * COL + c0:2 * COL + c0 + 128])
                qs = jnp.concatenate([jnp.where(lo, q, zero), jnp.where(lo, zero, q)], axis=0)
                scores.append(lax.dot_general(qs, k, (((1,), (1,)), ((), ())), preferred_element_type=F32) + bias)
            probs, maxes, sums = [], [], []
            for jp in pairs:
                m = jnp.max(scores[jp], axis=-1, keepdims=True)
                p = jnp.exp(scores[jp] - m)
                maxes.append(m)
                sums.append(jnp.sum(p, axis=-1, keepdims=True))
                probs.append(p.astype(BF16))
            lse_tile = jnp.zeros((SPAN, 128), F32)
            for jp in pairs:
                c0 = jp * 128
                ov = _dot(probs[jp], values[jp])
                inv = 1.0 / sums[jp]
                o = jnp.where(lo, ov[:SPAN] * inv[:SPAN], ov[SPAN:] * inv[SPAN:])
                o_ref[pl.ds(row0, SPAN), s * B_WIDTH + c0:s * B_WIDTH + c0 + 128] = o.astype(BF16)
                lse = maxes[jp] + jnp.log(sums[jp])
                lse_tile = jnp.where(lane16 == 2 * jp, lse[:SPAN],
                                     jnp.where(lane16 == 2 * jp + 1, lse[SPAN:], lse_tile))
            lse_ref[pl.ds(row0, SPAN), s * 128:(s + 1) * 128] = lse_tile

        block(0, 0, bias_first)
        if nb > 1:
            def body(i, carry):
                block(pl.multiple_of(i * SPAN, SPAN), pl.multiple_of((i - 1) * SPAN, SPAN), bias_main)
                return carry
            lax.fori_loop(1, nb, body, 0, unroll=min(5, nb - 1))


def _attn(qkv_g):
    bsz, dl, seq, _ = qkv_g.shape
    ns = max(1, min(dl, TM_ATTN // seq))
    return pl.pallas_call(
        functools.partial(_attn_kernel, ns=ns, seq=seq),
        grid=(bsz, dl // ns),
        in_specs=[pl.BlockSpec((None, ns, seq, 3 * COL), lambda b, r: (b, r, 0, 0))],
        out_specs=[pl.BlockSpec((None, seq, ns * B_WIDTH), lambda b, r: (b, 0, r)),
                   pl.BlockSpec((None, seq, ns * 128), lambda b, r: (b, 0, r))],
        out_shape=[jax.ShapeDtypeStruct((bsz, seq, dl * B_WIDTH), BF16),
                   jax.ShapeDtypeStruct((bsz, seq, dl * 128), F32)],
        compiler_params=pltpu.CompilerParams(dimension_semantics=("parallel", "parallel"),
                                             vmem_limit_bytes=VMEM_LIMIT),
        name=f"attn{dl}",
    )(qkv_g)


def _natural_rows(ref, dl, scr):
    nchunk, tm, _ = scr.shape
    w = nchunk * 128
    per = tm // dl
    for r in range(dl):
        for c in range(nchunk):
            scr[c, pl.ds(r, per, stride=dl), :] = ref[:, r * w + c * 128:r * w + (c + 1) * 128].astype(F32)
    return jnp.concatenate([scr[c] for c in range(nchunk)], axis=1)


def _mix_kernel(ga_ref, gates_ref, o1_ref, o2_ref, o3_ref, l1_ref, l2_ref, l3_ref, x_ref,
                wa_ref, wb_ref, wo_ref, wr_ref, br_ref, g1_ref, b1_ref,
                x1_ref, x1p_ref, routet_ref, o2s_ref, o3s_ref, l2s_ref, l3s_ref):
    tm = x_ref.shape[0]
    o2 = _natural_rows(o2_ref, 4, o2s_ref)
    o3 = _natural_rows(o3_ref, 16, o3s_ref)
    l2 = _natural_rows(l2_ref, 4, l2s_ref)
    l3 = _natural_rows(l3_ref, 16, l3s_ref)
    er = lax.broadcasted_iota(I32, (256, B_WIDTH), 0)
    ec = lax.broadcasted_iota(I32, (256, B_WIDTH), 1)
    expand = jnp.where(er % 128 == (ec // B_HEAD_DIM) * 16, 1.0, 0.0).astype(BF16)

    def widen(w):
        hi = w.astype(BF16)
        lo = (w - hi.astype(F32)).astype(BF16)
        return _dot(jnp.concatenate([hi, lo], axis=1), expand)

    h = tm // 2
    halves = (slice(0, h), slice(h, tm))
    obs = []
    for r in halves:
        l1 = l1_ref[r, :]
        mx = jnp.maximum(l1, jnp.maximum(l2[r], l3[r]))
        e1, e2, e3 = jnp.exp(l1 - mx), jnp.exp(l2[r] - mx), jnp.exp(l3[r] - mx)
        inv = 1.0 / (e1 + e2 + e3)
        obs.append(widen(e1 * inv) * o1_ref[r, :].astype(F32) + widen(e2 * inv) * o2[r] + widen(e3 * inv) * o3[r])
    ybs = [_dot(ob.astype(BF16), wb_ref[...]) for ob in obs]
    yas = [_dot(ga_ref[r, :], wa_ref[...]) for r in halves]
    pres = [gates_ref[r, :D_MODEL].astype(F32) * ya + gates_ref[r, D_MODEL:].astype(F32) * yb
            for r, ya, yb in zip(halves, yas, ybs)]
    mixes = [_dot(pre.astype(BF16), wo_ref[...]) for pre in pres]
    x1s = [_ln(DEEPNORM_ALPHA * x_ref[r, :] + mix, g1_ref[...], b1_ref[...]) for r, mix in zip(halves, mixes)]
    logits = [_dot(x1.astype(BF16), wr_ref[...]) + br_ref[...] for x1 in x1s]

    nrow = 40
    row = lax.broadcasted_iota(I32, (nrow, h), 0).astype(F32)
    row8 = lax.broadcasted_iota(I32, (8, h), 0)
    big = 1e9
    for i, r in enumerate(halves):
        x1_ref[r, :] = x1s[i]
        _store_packed_rows(x1p_ref.at[i * h // 8:(i + 1) * h // 8], x1s[i])
        lg = logits[i].T[:nrow, :]
        gl = jnp.where(row < N_GROUPS, lg, NEG)
        gm = jnp.max(gl, axis=0, keepdims=True)
        gidx = jnp.min(jnp.where(gl == gm, row, big), axis=0, keepdims=True)
        gsum = jnp.sum(jnp.where(row < N_GROUPS, jnp.exp(gl - gm), 0.0), axis=0, keepdims=True)
        gprob = 1.0 / gsum
        lo_row = N_GROUPS + N_EXPERTS * gidx
        el = jnp.where((row >= lo_row) & (row < lo_row + N_EXPERTS), lg, NEG)
        v1 = jnp.max(el, axis=0, keepdims=True)
        i1 = jnp.min(jnp.where(el == v1, row, big), axis=0, keepdims=True)
        el2 = jnp.where(row == i1, NEG, el)
        v2 = jnp.max(el2, axis=0, keepdims=True)
        i2 = jnp.min(jnp.where(el2 == v2, row, big), axis=0, keepdims=True)
        t = jnp.exp(v2 - v1)
        w1 = 1.0 / (1.0 + t)
        w2 = t * w1
        routet_ref[:, r] = jnp.where(row8 == 0, i1 - N_GROUPS,
                                     jnp.where(row8 == 1, i2 - N_GROUPS,
                                               jnp.where(row8 == 2, gprob * w1,
                                                         jnp.where(row8 == 3, gprob * w2, 0.0))))


def _mix(ga, gates, o1, o2, o3, l1, l2, l3, x2, wa, wb, wo, wr, br, g1, b1):
    n = x2.shape[0]
    bsz = o2.shape[0]
    tm = TM_MIX
    tiles = n // bsz // tm
    rows = lambda w: pl.BlockSpec((tm, w), lambda i: (i, 0))
    grouped = lambda a, dl: pl.BlockSpec((None, tm // dl, a.shape[2]), lambda i: (i // tiles, i % tiles, 0))
    full = lambda a: pl.BlockSpec(a.shape, lambda i: (0,) * a.ndim)
    return pl.pallas_call(
        _mix_kernel,
        grid=(n // tm,),
        in_specs=[rows(A_WIDTH), rows(2 * D_MODEL), rows(B_WIDTH), grouped(o2, 4), grouped(o3, 16),
                  rows(128), grouped(l2, 4), grouped(l3, 16), rows(D_MODEL),
                  full(wa), full(wb), full(wo), full(wr), full(br), full(g1), full(b1)],
        out_specs=[rows(D_MODEL), pl.BlockSpec((tm // 8, 32, 128), lambda i: (i, 0, 0)),
                   pl.BlockSpec((8, tm), lambda i: (0, i))],
        out_shape=[jax.ShapeDtypeStruct((n, D_MODEL), F32),
                   jax.ShapeDtypeStruct((n // 8, 32, 128), U32),
                   jax.ShapeDtypeStruct((8, n), F32)],
        scratch_shapes=[pltpu.VMEM((B_WIDTH // 128, tm, 128), F32), pltpu.VMEM((B_WIDTH // 128, tm, 128), F32),
                        pltpu.VMEM((1, tm, 128), F32), pltpu.VMEM((1, tm, 128), F32)],
        compiler_params=pltpu.CompilerParams(dimension_semantics=("parallel",), vmem_limit_bytes=VMEM_LIMIT),
        name="mix",
    )(ga, gates, o1, o2, o3, l1, l2, l3, x2, wa, wb, wo, wr, br, g1, b1)


SC_WINDOW = 128


def _sc_mesh():
    return plsc.VectorSubcoreMesh(core_axis_name="core", subcore_axis_name="subcore")


def _sc_scatter_rows(rows, dst, n_out):
    r = rows.shape[0]

    @pl.kernel(out_type=jax.ShapeDtypeStruct((n_out, 128), rows.dtype), mesh=_sc_mesh())
    def scatter(rows_hbm, dst0_hbm, dst1_hbm, out_hbm):
        def body(rows_vmem, dst0_vmem, dst1_vmem):
            pltpu.sync_copy(rows_vmem, out_hbm.at[dst0_vmem.at[0]])
            pltpu.sync_copy(rows_vmem, out_hbm.at[dst1_vmem.at[0]])

        pltpu.emit_pipeline(
            body,
            grid=(r // SC_WINDOW,),
            in_specs=[pl.BlockSpec((SC_WINDOW, 128), lambda i: (i, 0)),
                      pl.BlockSpec((1, SC_WINDOW), lambda i: (0, i)),
                      pl.BlockSpec((1, SC_WINDOW), lambda i: (0, i))],
            out_specs=[],
            core_axis_name=("core", "subcore"),
            dimension_semantics=(pltpu.PARALLEL,),
        )(rows_hbm, dst0_hbm, dst1_hbm)

    return scatter(rows, dst[0:1], dst[1:2])


def _sc_gather_rows(table, src):
    m = src.shape[0]
    k = 2

    @pl.kernel(out_type=jax.ShapeDtypeStruct((m, 128), table.dtype), mesh=_sc_mesh(),
               scratch_types=[pltpu.SemaphoreType.DMA((k,))])
    def gather(table_hbm, src_hbm, out_hbm, sems):
        def body(src_vmem, out_vmem):
            copies = [pltpu.async_copy(table_hbm.at[src_vmem.at[j]], out_vmem.at[pl.ds(j * SC_WINDOW, SC_WINDOW)],
                                       sems.at[j]) for j in range(k)]
            for c in copies:
                c.wait()

        pltpu.emit_pipeline(
            body,
            grid=(m // (k * SC_WINDOW),),
            in_specs=[pl.BlockSpec((k, SC_WINDOW), lambda i: (i, 0))],
            out_specs=[pl.BlockSpec((k * SC_WINDOW, 128), lambda i: (i, 0))],
            core_axis_name=("core", "subcore"),
            dimension_semantics=(pltpu.PARALLEL,),
        )(src_hbm, out_hbm)

    return gather(table, src.reshape(m // SC_WINDOW, SC_WINDOW))


MOE_X_SLOTS = 3
MOE_Y_SLOTS = 2


def _moe_kernel(tiles_ref, xs_hbm, wg_hbm, wu_hbm, wd_hbm, ys_hbm,
                xbuf, ybuf, wgf_ref, wuf_ref, wdf_ref, wgb_ref, wub_ref, wdb_ref, xsem, ysem, wsem, *, nt):
    te_ref, tv_ref, ws_ref, nx_ref = (tiles_ref.at[i] for i in range(4))
    t = pl.program_id(0)
    grp = TM_MOE // 8

    def fetch_x(j):
        return pltpu.make_async_copy(xs_hbm.at[pl.ds(j * grp, grp)], xbuf.at[j % MOE_X_SLOTS],
                                     xsem.at[j % MOE_X_SLOTS])

    def store_y(j):
        return pltpu.make_async_copy(ybuf.at[j % MOE_Y_SLOTS], ys_hbm.at[pl.ds(j * grp, grp)],
                                     ysem.at[j % MOE_Y_SLOTS])

    def fetch_w(e, slot):
        return [pltpu.make_async_copy(w.at[e], buf.at[slot], wsem.at[slot, i])
                for i, (w, buf) in enumerate(((wg_hbm, wgf_ref), (wu_hbm, wuf_ref), (wd_hbm, wdf_ref)))]

    @pl.when(t == 0)
    def _():
        for c in fetch_w(te_ref[0], 0):
            c.start()
        for j in range(MOE_X_SLOTS - 1):
            @pl.when(tv_ref[j] == 1)
            def _():
                fetch_x(j).start()

    ahead = t + (MOE_X_SLOTS - 1)

    @pl.when((ahead < nt) & (tv_ref[jnp.minimum(ahead, nt - 1)] == 1))
    def _():
        fetch_x(ahead).start()

    slot = ws_ref[t]

    @pl.when((t == 0) | (te_ref[t] != te_ref[jnp.maximum(t - 1, 0)]))
    def _():
        @pl.when(nx_ref[t] >= 0)
        def _():
            for c in fetch_w(nx_ref[t], 1 - slot):
                c.start()

        for c in fetch_w(te_ref[t], slot):
            c.wait()
        wgb_ref[...] = wgf_ref[slot].astype(BF16)
        wub_ref[...] = wuf_ref[slot].astype(BF16)
        wdb_ref[...] = wdf_ref[slot].astype(BF16)

    @pl.when((t >= MOE_Y_SLOTS) & (tv_ref[jnp.maximum(t - MOE_Y_SLOTS, 0)] == 1))
    def _():
        store_y(t - MOE_Y_SLOTS).wait()

    @pl.when(tv_ref[t] == 1)
    def _():
        fetch_x(t).wait()
        xb = _load_packed_rows(xbuf.at[t % MOE_X_SLOTS]).astype(BF16)
        g = _dot(xb, wgb_ref[...])
        u = _dot(xb, wub_ref[...])
        h = (g * _sigmoid(g) * u).astype(BF16)
        _store_packed_rows(ybuf.at[t % MOE_Y_SLOTS], _dot(h, wdb_ref[...]))
        store_y(t).start()

    @pl.when(t == nt - 1)
    def _():
        for j in range(nt - MOE_Y_SLOTS, nt):
            @pl.when(tv_ref[j] == 1)
            def _():
                store_y(j).wait()


def _moe(tiles, nt, xs, wg, wu, wd):
    tm = TM_MOE
    any_spec = pl.BlockSpec(memory_space=pl.ANY)
    return pl.pallas_call(
        functools.partial(_moe_kernel, nt=nt),
        grid_spec=pltpu.PrefetchScalarGridSpec(
            num_scalar_prefetch=1,
            grid=(nt,),
            in_specs=[any_spec] * 4,
            out_specs=any_spec,
            scratch_shapes=[pltpu.VMEM((MOE_X_SLOTS, tm // 8, 32, 128), U32),
                            pltpu.VMEM((MOE_Y_SLOTS, tm // 8, 32, 128), U32),
                            pltpu.VMEM((2, D_MODEL, D_EXPERT), F32), pltpu.VMEM((2, D_MODEL, D_EXPERT), F32),
                            pltpu.VMEM((2, D_EXPERT, D_MODEL), F32),
                            pltpu.VMEM((D_MODEL, D_EXPERT), BF16), pltpu.VMEM((D_MODEL, D_EXPERT), BF16),
                            pltpu.VMEM((D_EXPERT, D_MODEL), BF16),
                            pltpu.SemaphoreType.DMA((MOE_X_SLOTS,)), pltpu.SemaphoreType.DMA((MOE_Y_SLOTS,)),
                            pltpu.SemaphoreType.DMA((2, 3))]),
        out_shape=jax.ShapeDtypeStruct((nt * tm // 8, 32, 128), U32),
        compiler_params=pltpu.CompilerParams(dimension_semantics=("arbitrary",), vmem_limit_bytes=VMEM_LIMIT),
        name="moe",
    )(tiles, xs, wg, wu, wd)


def _final_kernel(y0_ref, y1_ref, x1_ref, p_ref, routet_ref, wple_ref, wpg_ref, g2_ref, b2_ref, out_ref):
    tm = x1_ref.shape[0]
    route = routet_ref[...].T
    h = tm // 4
    parts = tuple(slice(i * h, (i + 1) * h) for i in range(4))
    x1s = [x1_ref[r, :] for r in parts]
    plins = [_dot(p_ref[r, :].astype(BF16), wple_ref[...]) for r in parts]
    gates = [_dot(x1.astype(BF16), wpg_ref[...]) for x1 in x1s]
    sums = []
    for i, r in enumerate(parts):
        g8 = slice(i * h // 8, (i + 1) * h // 8)
        sums.append(DEEPNORM_ALPHA * x1s[i] + route[r, 2:3] * _load_packed_rows(y0_ref.at[g8])
                    + route[r, 3:4] * _load_packed_rows(y1_ref.at[g8]))
    for i, r in enumerate(parts):
        out_ref[r, :] = _ln((sums[i] + plins[i]) + plins[i] * jnp.tanh(gates[i]), g2_ref[...], b2_ref[...])


def _final(yg, x1, p2, routet, wple, wpg, g2, b2):
    n = x1.shape[0]
    tm = TM_FIN
    nt = n // tm
    rows = lambda w: pl.BlockSpec((tm, w), lambda t: (t, 0))
    full = lambda a: pl.BlockSpec(a.shape, lambda t: (0,) * a.ndim)
    return pl.pallas_call(
        _final_kernel,
        grid=(nt,),
        in_specs=[pl.BlockSpec((tm // 8, 32, 128), lambda t: (t, 0, 0)),
                  pl.BlockSpec((tm // 8, 32, 128), lambda t: (t + nt, 0, 0)),
                  rows(D_MODEL), rows(PLE_DIM), pl.BlockSpec((8, tm), lambda t: (0, t)),
                  full(wple), full(wpg), full(g2), full(b2)],
        out_specs=rows(D_MODEL),
        out_shape=jax.ShapeDtypeStruct((n, D_MODEL), F32),
        compiler_params=pltpu.CompilerParams(dimension_semantics=("parallel",), vmem_limit_bytes=VMEM_LIMIT),
        name="final",
    )(yg, yg, x1, p2, routet, wple, wpg, g2, b2)


def _route_tables_kernel(e_ref, piece_ref, tab_ref):
    r = e_ref.shape[0]
    e = e_ref[...]
    ri = lax.broadcasted_iota(I32, (128, 128), 0)
    ci = lax.broadcasted_iota(I32, (128, 128), 1)
    upper = jnp.where(ri <= ci, 1.0, 0.0).astype(BF16)
    rr = lax.broadcasted_iota(I32, (r, r), 0)
    rc = lax.broadcasted_iota(I32, (r, r), 1)
    below = jnp.where(rc < rr, 1.0, 0.0).astype(BF16)
    lane = lax.broadcasted_iota(I32, (1, 128), 1)

    experts = range(N_EXPERTS_TOTAL)
    ms = [jnp.where(e == x, 1.0, 0.0) for x in experts]
    pres = [_dot(m.astype(BF16), upper) for m in ms]
    tots = [jnp.broadcast_to(pre[:, 127:128], (r, 128)) for pre in pres]
    offs_rows = [_dot(below, tot.astype(BF16)) for tot in tots]
    rank = jnp.zeros((r, 128), F32)
    counts = jnp.zeros((1, 128), F32)
    for x in experts:
        rank = rank + ms[x] * (pres[x] + offs_rows[x])
        counts = jnp.where(lane == x, offs_rows[x][r - 1:r, :] + tots[x][r - 1:r, :], counts)
    padded = jnp.floor((counts + (TM_MOE - 1)) * (1.0 / TM_MOE)) * TM_MOE
    ends = _dot(jnp.broadcast_to(padded, (8, 128)).astype(BF16), upper)[0:1, :]
    offs = ends - padded

    nt = r * 128 // TM_MOE + N_EXPERTS_TOTAL
    ones = jnp.ones((128, 128), BF16)
    lanef = lane.astype(F32)
    diag = ri == ci
    ends_t = ends * (1.0 / TM_MOE)
    ecol = _dot(jnp.where(diag, jnp.broadcast_to(ends_t, (128, 128)), 0.0).astype(BF16), ones)
    passed = jnp.where((ecol <= ci.astype(F32)) & (ri < N_EXPERTS_TOTAL), 1.0, 0.0)
    te = jnp.minimum(jnp.sum(passed, axis=0, keepdims=True), N_EXPERTS_TOTAL - 1.0)
    tv = jnp.where(lanef < ecol[N_EXPERTS_TOTAL - 1:N_EXPERTS_TOTAL, :], 1.0, 0.0)
    shift = jnp.where(ri + 1 == ci, 1.0, 0.0).astype(BF16)
    te_prev = _dot(jnp.broadcast_to(te, (8, 128)).astype(BF16), shift)[0:1, :]
    first = jnp.where((lane == 0) | (te != te_prev), 1.0, 0.0)
    run = _dot(jnp.broadcast_to(first, (8, 128)).astype(BF16), upper)[0:1, :] - 1.0
    ws = run - 2.0 * jnp.floor(run * 0.5)
    tcol = _dot(jnp.where(diag, jnp.broadcast_to(te, (128, 128)), 0.0).astype(BF16), ones)
    later = jnp.where((tcol > te) & (ri < nt), tcol, float(N_EXPERTS_TOTAL))
    nx = jnp.min(later, axis=0, keepdims=True)
    nx = jnp.where(nx == N_EXPERTS_TOTAL, -1.0, nx)
    row8 = lax.broadcasted_iota(I32, (8, 128), 0)
    tab = jnp.where(row8 == 0, te, jnp.where(row8 == 1, tv, jnp.where(row8 == 2, ws, jnp.where(row8 == 3, nx, 0.0))))
    tab_ref[...] = tab.astype(I32)

    pos = rank - 1.0
    for x in range(N_EXPERTS_TOTAL):
        pos = pos + jnp.where(e == x, offs[:, x:x + 1], 0.0)

    hi = jnp.floor(pos * (1.0 / 256.0))
    lo = pos - 256.0 * hi
    jv = ((lane % 32) // 8).astype(F32)
    for c in range(4):
        sel = jnp.where(ri == 32 * c + 8 * (ci // 32) + ci % 8, 1.0, 0.0).astype(BF16)
        pc = 256.0 * _dot(hi.astype(BF16), sel) + _dot(lo.astype(BF16), sel)
        p8 = jnp.floor(pc * 0.125)
        piece = p8 * (8.0 * SUBROWS) + (pc - 8.0 * p8) + 8.0 * jv
        piece_ref[pl.ds(c, r, stride=4), :] = piece.astype(I32)


def _routing_tables(routet, n):
    nt = (2 * n) // TM_MOE + N_EXPERTS_TOTAL
    assert nt <= 128, "the tile tables hold one tile per lane"
    r = 2 * n // 128
    piece, tiles = pl.pallas_call(
        _route_tables_kernel,
        out_shape=[jax.ShapeDtypeStruct((4 * r, 128), I32), jax.ShapeDtypeStruct((8, 128), I32)],
        compiler_params=pltpu.CompilerParams(vmem_limit_bytes=VMEM_LIMIT),
        name="route_tables",
    )(routet[0:2].reshape(r, 128))
    return tiles, nt, piece.reshape(2, n * SUBROWS)


def kernel(x, p, w_in, a_ln_g, a_ln_b, a_ws, a_bs, w_a_proj, w_b_proj, w_o, ln1_g, ln1_b, w_group_router,
           b_group_router, w_expert_router, b_expert_router, w_gate, w_up, w_down, w_ple, w_ple_gate,
           ln2_g, ln2_b):
    bsz, s, d = x.shape
    n = bsz * s
    assert d == D_MODEL and s % (SPAN * max(B_DILATIONS)) == 0 and n % TM_PROJ == 0
    assert w_in.shape[0] == 1, "one layer"

    a_bias = jnp.repeat(a_bs[0].T, A_WIDTH // 8, axis=1)

    ga, gates, qkv1, qkv2, qkv3 = _proj(x, w_in[0], a_ln_g, a_ln_b, a_ws[0], a_bias)
    o1, l1 = _attn(qkv1.reshape(bsz, 1, s, 3 * COL))
    o2, l2 = _attn(qkv2)
    o3, l3 = _attn(qkv3)

    pad = 128 - N_GROUPS - N_EXPERTS_TOTAL
    wr = jnp.concatenate([w_group_router[0], w_expert_router[0].reshape(d, N_EXPERTS_TOTAL),
                          jnp.zeros((d, pad), F32)], axis=1).astype(BF16)
    br = jnp.concatenate([b_group_router[0], b_expert_router[0].reshape(-1), jnp.zeros((pad,), F32)])[None, :]
    x1, x1p, routet = _mix(
        ga, gates, o1.reshape(n, B_WIDTH), o2, o3, l1.reshape(n, 128), l2, l3, x.reshape(n, d),
        w_a_proj[0].astype(BF16), w_b_proj[0].astype(BF16), w_o[0].astype(BF16), wr, br, ln1_g, ln1_b)

    tiles, nt, piece = _routing_tables(routet, n)
    xs = _sc_scatter_rows(x1p.reshape(n * SUBROWS, 128), piece, nt * TM_MOE * SUBROWS)
    ys = _moe(tiles, nt, xs.reshape(nt * TM_MOE // 8, 32, 128),
              w_gate[0].reshape(N_EXPERTS_TOTAL, d, D_EXPERT), w_up[0].reshape(N_EXPERTS_TOTAL, d, D_EXPERT),
              w_down[0].reshape(N_EXPERTS_TOTAL, D_EXPERT, d)).reshape(nt * TM_MOE * SUBROWS, 128)
    yg = _sc_gather_rows(ys, piece.reshape(-1))
    out = _final(yg.reshape(n // 4, 32, 128), x1, p[0].reshape(n, PLE_DIM), routet,
                 (0.5 * w_ple[0]).astype(BF16), (0.5 * w_ple_gate[0]).astype(BF16), ln2_g, ln2_b)
    return out.reshape(bsz, s, d)
```

```python
import functools

import jax
import jax.numpy as jnp
from jax import lax
from jax.experimental import pallas as pl
from jax.experimental.pallas import tpu as pltpu
from jax.experimental.pallas import tpu_sc as plsc

F32 = jnp.float32
BF16 = jnp.bfloat16
U32 = jnp.uint32
I32 = jnp.int32

D_MODEL = 1024
PLE_DIM = 256
A_WIDTH = 512
A_CHUNK = 128
B_HEAD_DIM = 64
B_HEADS = 8
B_WIDTH = 512
B_DILATIONS = (1, 4, 16)
SPAN = 128
N_GROUPS = 4
N_EXPERTS = 8
N_EXPERTS_TOTAL = N_GROUPS * N_EXPERTS
D_EXPERT = 256
DEEPNORM_ALPHA = 2.0 ** 0.25
LN_EPS = 1e-5
COL = 512
NEG = -1e30

VMEM_LIMIT = 56 * 1024 * 1024

TM_PROJ = 512
TM_ATTN = 1024
TM_MIX = 512
TM_MOE = 512
TM_FIN = 1024


def _ln(x, g, b):
    mu = jnp.mean(x, axis=-1, keepdims=True)
    xc = x - mu
    var = jnp.mean(xc * xc, axis=-1, keepdims=True)
    return xc * lax.rsqrt(var + LN_EPS) * g + b


def _gelu_tanh(x):
    return 0.5 * x * (1.0 + jnp.tanh(0.7978845608028654 * (x + 0.044715 * (x * x * x))))


def _sigmoid(x):
    return 0.5 * jnp.tanh(0.5 * x) + 0.5


def _dot(a, b):
    return jnp.dot(a, b, preferred_element_type=F32)


PACK_W = D_MODEL // 2
SUBROWS = PACK_W // 128


def _store_packed_rows(ref, x):
    m = x.shape[0]
    xb = x.astype(BF16).astype(F32)
    lo = pltpu.bitcast(xb[:, :PACK_W], U32) >> 16
    hi = pltpu.bitcast(xb[:, PACK_W:], U32) & jnp.uint32(0xFFFF0000)
    w = hi | lo
    for j in range(SUBROWS):
        ref[:, 8 * j:8 * (j + 1), :] = w[:, 128 * j:128 * (j + 1)].reshape(m // 8, 8, 128)


def _load_packed_rows(ref):
    m = ref.shape[0] * 8
    ws = [ref[:, 8 * j:8 * (j + 1), :].reshape(m, 128) for j in range(SUBROWS)]
    lo = [pltpu.bitcast(w << 16, F32) for w in ws]
    hi = [pltpu.bitcast(w & jnp.uint32(0xFFFF0000), F32) for w in ws]
    return jnp.concatenate(lo + hi, axis=1)


W_IN_BLOCKS = 15
W_IN_SLOTS = 4


def _proj_kernel(x_ref, w_hbm, lng_ref, lnb_ref, ws_ref, bias_ref,
                 ga_ref, gates_ref, qkv1_ref, qkv2_ref, qkv3_ref, xc_ref, w, wstage_ref, wsem):
    tm = x_ref.shape[0]

    @pl.when(pl.program_id(0) == 0)
    def _():
        def fetch(j):
            return pltpu.make_async_copy(w_hbm.at[:, pl.ds(j * COL, COL)], wstage_ref.at[j % W_IN_SLOTS],
                                         wsem.at[j % W_IN_SLOTS])

        for j in range(W_IN_SLOTS):
            fetch(j).start()
        for j in range(W_IN_BLOCKS):
            fetch(j).wait()
            w[j] = wstage_ref[j % W_IN_SLOTS].astype(BF16)
            if j + W_IN_SLOTS < W_IN_BLOCKS:
                fetch(j + W_IN_SLOTS).start()

    xb = x_ref[...].astype(BF16)

    u_raw = _dot(xb, w[0])
    v_raw = _dot(xb, w[1])

    for i in range(4):
        gates_ref[:, i * COL:(i + 1) * COL] = _sigmoid(_dot(xb, w[11 + i])).astype(BF16)
    for j in range(3):
        qkv1_ref[:, j * COL:(j + 1) * COL] = _dot(xb, w[2 + 3 * j]).astype(BF16)

    for c in range(D_MODEL // 128):
        xc_ref[c] = x_ref[:, c * 128:(c + 1) * 128]
    for gi, out_ref in ((1, qkv2_ref), (2, qkv3_ref)):
        dl = B_DILATIONS[gi]
        per = tm // dl
        xp = jnp.concatenate(
            [jnp.concatenate([xc_ref[c, pl.ds(r, per, stride=dl), :] for c in range(D_MODEL // 128)], axis=1)
             for r in range(dl)], axis=0).astype(BF16)
        for j in range(3):
            res = _dot(xp, w[2 + 3 * j + gi]).astype(BF16)
            for r in range(dl):
                out_ref[r, :, j * COL:(j + 1) * COL] = res[r * per:(r + 1) * per]

    u = _gelu_tanh(u_raw)
    v = _gelu_tanh(v_raw)
    vn = _ln(v, lng_ref[...], lnb_ref[...]).astype(BF16)

    row = lax.broadcasted_iota(I32, (A_CHUNK, A_CHUNK), 0)
    colm = lax.broadcasted_iota(I32, (A_CHUNK, A_CHUNK), 1)
    causal = colm <= row
    lo = colm < 64
    zero = jnp.zeros((A_CHUNK, A_CHUNK), BF16)
    wcat = []
    for j in range(4):
        w0 = jnp.where(causal, ws_ref[2 * j], 0.0).astype(BF16)
        w1 = jnp.where(causal, ws_ref[2 * j + 1], 0.0).astype(BF16)
        wcat.append(jnp.concatenate([w0, w1], axis=1))
    for c in range(tm // A_CHUNK):
        r0 = c * A_CHUNK
        for j in range(4):
            c0 = j * 128
            vt = vn[r0:r0 + A_CHUNK, c0:c0 + 128]
            rhs = jnp.concatenate([jnp.where(lo, vt, zero), jnp.where(lo, zero, vt)], axis=0)
            mixed = _dot(wcat[j], rhs) + bias_ref[:, c0:c0 + 128]
            ga_ref[r0:r0 + A_CHUNK, c0:c0 + 128] = (u[r0:r0 + A_CHUNK, c0:c0 + 128] * mixed).astype(BF16)


def _proj(x, w_in, a_ln_g, a_ln_b, a_ws, a_bias):
    bsz, s, _ = x.shape
    n = bsz * s
    tm = TM_PROJ
    tiles = s // tm
    x2 = x.reshape(n, D_MODEL)
    full = lambda shape: pl.BlockSpec(shape, lambda i: (0,) * len(shape))
    rows = lambda width: pl.BlockSpec((tm, width), lambda i: (i, 0))
    dil = lambda dl: pl.BlockSpec((None, dl, tm // dl, 3 * COL), lambda i: (i // tiles, 0, i % tiles, 0))
    return pl.pallas_call(
        _proj_kernel,
        grid=(n // tm,),
        in_specs=[rows(D_MODEL), pl.BlockSpec(memory_space=pl.ANY)]
                 + [full((1, A_WIDTH)), full((1, A_WIDTH)), full((8, A_CHUNK, A_CHUNK)), full((A_CHUNK, A_WIDTH))],
        out_specs=[rows(A_WIDTH), rows(4 * COL), rows(3 * COL), dil(4), dil(16)],
        out_shape=[jax.ShapeDtypeStruct((n, A_WIDTH), BF16),
                   jax.ShapeDtypeStruct((n, 4 * COL), BF16),
                   jax.ShapeDtypeStruct((n, 3 * COL), BF16),
                   jax.ShapeDtypeStruct((bsz, 4, s // 4, 3 * COL), BF16),
                   jax.ShapeDtypeStruct((bsz, 16, s // 16, 3 * COL), BF16)],
        scratch_shapes=[pltpu.VMEM((D_MODEL // 128, tm, 128), F32),
                        pltpu.VMEM((W_IN_BLOCKS, D_MODEL, COL), BF16),
                        pltpu.VMEM((W_IN_SLOTS, D_MODEL, COL), F32),
                        pltpu.SemaphoreType.DMA((W_IN_SLOTS,))],
        compiler_params=pltpu.CompilerParams(dimension_semantics=("arbitrary",), vmem_limit_bytes=VMEM_LIMIT),
        name="proj",
    )(x2, w_in, a_ln_g, a_ln_b, a_ws, a_bias)


def _attn_kernel(qkv_ref, o_ref, lse_ref, *, ns, seq):
    nb = seq // SPAN
    lane = lax.broadcasted_iota(I32, (SPAN, 128), 1)
    lo = lane < 64
    lane16 = lane // 16
    qi = lax.broadcasted_iota(I32, (SPAN, 2 * SPAN), 0)
    ki = lax.broadcasted_iota(I32, (SPAN, 2 * SPAN), 1)
    causal = lax.broadcasted_iota(I32, (SPAN, SPAN), 1) <= lax.broadcasted_iota(I32, (SPAN, SPAN), 0)
    bias_first = jnp.where(causal, 0.0, NEG).astype(F32)
    bias_first = jnp.concatenate([bias_first, bias_first], axis=0)
    bias_main = jnp.where((ki >= qi) & (ki <= qi + SPAN), 0.0, NEG).astype(F32)
    bias_main = jnp.concatenate([bias_main, bias_main], axis=0)
    zero = jnp.zeros((SPAN, 128), BF16)

    for s in range(ns):
        def block(row0, start, bias, s=s):
            win = bias.shape[1]
            pairs = range(B_HEADS // 2)
            scores, values = [], []
            for jp in pairs:
                c0 = jp * 128
                q = qkv_ref[s, pl.ds(row0, SPAN), c0:c0 + 128] * jnp.asarray(0.125, BF16)
                k = qkv_ref[s, pl.ds(start, win), COL + c0:COL + c0 + 128]
                values.append(qkv_ref[s, pl.ds(start, win), 2 * COL + c0:2 * COL + c0 + 128])
                qs = jnp.concatenate([jnp.where(lo, q, zero), jnp.where(lo, zero, q)], axis=0)
                scores.append(lax.dot_general(qs, k, (((1,), (1,)), ((), ())), preferred_element_type=F32) + bias)
            probs, maxes, sums = [], [], []
            for jp in pairs:
                m = jnp.max(scores[jp], axis=-1, keepdims=True)
                p = jnp.exp(scores[jp] - m)
                maxes.append(m)
                sums.append(jnp.sum(p, axis=-1, keepdims=True))
                probs.append(p.astype(BF16))
            lse_tile = jnp.zeros((SPAN, 128), F32)
            for jp in pairs:
                c0 = jp * 128
                ov = _dot(probs[jp], values[jp])
                inv = 1.0 / sums[jp]
                o = jnp.where(lo, ov[:SPAN] * inv[:SPAN], ov[SPAN:] * inv[SPAN:])
                o_ref[pl.ds(row0, SPAN), s * B_WIDTH + c0:s * B_WIDTH + c0 + 128] = o.astype(BF16)
                lse = maxes[jp] + jnp.log(sums[jp])
                lse_tile = jnp.where(lane16 == 2 * jp, lse[:SPAN],
                                     jnp.where(lane16 == 2 * jp + 1, lse[SPAN:], lse_tile))
            lse_ref[pl.ds(row0, SPAN), s * 128:(s + 1) * 128] = lse_tile

        block(0, 0, bias_first)
        if nb > 1:
            def body(i, carry):
                block(pl.multiple_of(i * SPAN, SPAN), pl.multiple_of((i - 1) * SPAN, SPAN), bias_main)
                return carry
            lax.fori_loop(1, nb, body, 0, unroll=min(5, nb - 1))


def _attn(qkv_g):
    bsz, dl, seq, _ = qkv_g.shape
    ns = max(1, min(dl, TM_ATTN // seq))
    return pl.pallas_call(
        functools.partial(_attn_kernel, ns=ns, seq=seq),
        grid=(bsz, dl // ns),
        in_specs=[pl.BlockSpec((None, ns, seq, 3 * COL), lambda b, r: (b, r, 0, 0))],
        out_specs=[pl.BlockSpec((None, seq, ns * B_WIDTH), lambda b, r: (b, 0, r)),
                   pl.BlockSpec((None, seq, ns * 128), lambda b, r: (b, 0, r))],
        out_shape=[jax.ShapeDtypeStruct((bsz, seq, dl * B_WIDTH), BF16),
                   jax.ShapeDtypeStruct((bsz, seq, dl * 128), F32)],
        compiler_params=pltpu.CompilerParams(dimension_semantics=("parallel", "parallel"),
                                             vmem_limit_bytes=VMEM_LIMIT),
        name=f"attn{dl}",
    )(qkv_g)


def _natural_rows(ref, dl, scr):
    nchunk, tm, _ = scr.shape
    w = nchunk * 128
    per = tm // dl
    for r in range(dl):
        for c in range(nchunk):
            scr[c, pl.ds(r, per, stride=dl), :] = ref[:, r * w + c * 128:r * w + (c + 1) * 128].astype(F32)
    return jnp.concatenate([scr[c] for c in range(nchunk)], axis=1)


def _mix_kernel(ga_ref, gates_ref, o1_ref, o2_ref, o3_ref, l1_ref, l2_ref, l3_ref, x_ref,
                wa_ref, wb_ref, wo_ref, wr_ref, br_ref, g1_ref, b1_ref,
                x1_ref, x1p_ref, routet_ref, o2s_ref, o3s_ref, l2s_ref, l3s_ref):
    tm = x_ref.shape[0]
    o2 = _natural_rows(o2_ref, 4, o2s_ref)
    o3 = _natural_rows(o3_ref, 16, o3s_ref)
    l2 = _natural_rows(l2_ref, 4, l2s_ref)
    l3 = _natural_rows(l3_ref, 16, l3s_ref)
    er = lax.broadcasted_iota(I32, (256, B_WIDTH), 0)
    ec = lax.broadcasted_iota(I32, (256, B_WIDTH), 1)
    expand = jnp.where(er % 128 == (ec // B_HEAD_DIM) * 16, 1.0, 0.0).astype(BF16)

    def widen(w):
        hi = w.astype(BF16)
        lo = (w - hi.astype(F32)).astype(BF16)
        return _dot(jnp.concatenate([hi, lo], axis=1), expand)

    h = tm // 2
    halves = (slice(0, h), slice(h, tm))
    obs = []
    for r in halves:
        l1 = l1_ref[r, :]
        mx = jnp.maximum(l1, jnp.maximum(l2[r], l3[r]))
        e1, e2, e3 = jnp.exp(l1 - mx), jnp.exp(l2[r] - mx), jnp.exp(l3[r] - mx)
        inv = 1.0 / (e1 + e2 + e3)
        obs.append(widen(e1 * inv) * o1_ref[r, :].astype(F32) + widen(e2 * inv) * o2[r] + widen(e3 * inv) * o3[r])
    ybs = [_dot(ob.astype(BF16), wb_ref[...]) for ob in obs]
    yas = [_dot(ga_ref[r, :], wa_ref[...]) for r in halves]
    pres = [gates_ref[r, :D_MODEL].astype(F32) * ya + gates_ref[r, D_MODEL:].astype(F32) * yb
            for r, ya, yb in zip(halves, yas, ybs)]
    mixes = [_dot(pre.astype(BF16), wo_ref[...]) for pre in pres]
    x1s = [_ln(DEEPNORM_ALPHA * x_ref[r, :] + mix, g1_ref[...], b1_ref[...]) for r, mix in zip(halves, mixes)]
    logits = [_dot(x1.astype(BF16), wr_ref[...]) + br_ref[...] for x1 in x1s]

    nrow = 40
    row = lax.broadcasted_iota(I32, (nrow, h), 0).astype(F32)
    row8 = lax.broadcasted_iota(I32, (8, h), 0)
    big = 1e9
    for i, r in enumerate(halves):
        x1_ref[r, :] = x1s[i]
        _store_packed_rows(x1p_ref.at[i * h // 8:(i + 1) * h // 8], x1s[i])
        lg = logits[i].T[:nrow, :]
        gl = jnp.where(row < N_GROUPS, lg, NEG)
        gm = jnp.max(gl, axis=0, keepdims=True)
        gidx = jnp.min(jnp.where(gl == gm, row, big), axis=0, keepdims=True)
        gsum = jnp.sum(jnp.where(row < N_GROUPS, jnp.exp(gl - gm), 0.0), axis=0, keepdims=True)
        gprob = 1.0 / gsum
        lo_row = N_GROUPS + N_EXPERTS * gidx
        el = jnp.where((row >= lo_row) & (row < lo_row + N_EXPERTS), lg, NEG)
        v1 = jnp.max(el, axis=0, keepdims=True)
        i1 = jnp.min(jnp.where(el == v1, row, big), axis=0, keepdims=True)
        el2 = jnp.where(row == i1, NEG, el)
        v2 = jnp.max(el2, axis=0, keepdims=True)
        i2 = jnp.min(jnp.where(el2 == v2, row, big), axis=0, keepdims=True)
        t = jnp.exp(v2 - v1)
        w1 = 1.0 / (1.0 + t)
        w2 = t * w1
        routet_ref[:, r] = jnp.where(row8 == 0, i1 - N_GROUPS,
                                     jnp.where(row8 == 1, i2 - N_GROUPS,
                                               jnp.where(row8 == 2, gprob * w1,
                                                         jnp.where(row8 == 3, gprob * w2, 0.0))))


def _mix(ga, gates, o1, o2, o3, l1, l2, l3, x2, wa, wb, wo, wr, br, g1, b1):
    n = x2.shape[0]
    bsz = o2.shape[0]
    tm = TM_MIX
    tiles = n // bsz // tm
    rows = lambda w: pl.BlockSpec((tm, w), lambda i: (i, 0))
    grouped = lambda a, dl: pl.BlockSpec((None, tm // dl, a.shape[2]), lambda i: (i // tiles, i % tiles, 0))
    full = lambda a: pl.BlockSpec(a.shape, lambda i: (0,) * a.ndim)
    return pl.pallas_call(
        _mix_kernel,
        grid=(n // tm,),
        in_specs=[rows(A_WIDTH), rows(2 * D_MODEL), rows(B_WIDTH), grouped(o2, 4), grouped(o3, 16),
                  rows(128), grouped(l2, 4), grouped(l3, 16), rows(D_MODEL),
                  full(wa), full(wb), full(wo), full(wr), full(br), full(g1), full(b1)],
        out_specs=[rows(D_MODEL), pl.BlockSpec((tm // 8, 32, 128), lambda i: (i, 0, 0)),
                   pl.BlockSpec((8, tm), lambda i: (0, i))],
        out_shape=[jax.ShapeDtypeStruct((n, D_MODEL), F32),
                   jax.ShapeDtypeStruct((n // 8, 32, 128), U32),
                   jax.ShapeDtypeStruct((8, n), F32)],
        scratch_shapes=[pltpu.VMEM((B_WIDTH // 128, tm, 128), F32), pltpu.VMEM((B_WIDTH // 128, tm, 128), F32),
                        pltpu.VMEM((1, tm, 128), F32), pltpu.VMEM((1, tm, 128), F32)],
        compiler_params=pltpu.CompilerParams(dimension_semantics=("parallel",), vmem_limit_bytes=VMEM_LIMIT),
        name="mix",
    )(ga, gates, o1, o2, o3, l1, l2, l3, x2, wa, wb, wo, wr, br, g1, b1)


SC_WINDOW = 128


def _sc_mesh():
    return plsc.VectorSubcoreMesh(core_axis_name="core", subcore_axis_name="subcore")


def _sc_scatter_rows(rows, dst, n_out):
    r = rows.shape[0]

    @pl.kernel(out_type=jax.ShapeDtypeStruct((n_out, 128), rows.dtype), mesh=_sc_mesh())
    def scatter(rows_hbm, dst0_hbm, dst1_hbm, out_hbm):
        def body(rows_vmem, dst0_vmem, dst1_vmem):
            pltpu.sync_copy(rows_vmem, out_hbm.at[dst0_vmem.at[0]])
            pltpu.sync_copy(rows_vmem, out_hbm.at[dst1_vmem.at[0]])

        pltpu.emit_pipeline(
            body,
            grid=(r // SC_WINDOW,),
            in_specs=[pl.BlockSpec((SC_WINDOW, 128), lambda i: (i, 0)),
                      pl.BlockSpec((1, SC_WINDOW), lambda i: (0, i)),
                      pl.BlockSpec((1, SC_WINDOW), lambda i: (0, i))],
            out_specs=[],
            core_axis_name=("core", "subcore"),
            dimension_semantics=(pltpu.PARALLEL,),
        )(rows_hbm, dst0_hbm, dst1_hbm)

    return scatter(rows, dst[0:1], dst[1:2])


def _sc_gather_rows(table, src):
    m = src.shape[0]
    k = 2

    @pl.kernel(out_type=jax.ShapeDtypeStruct((m, 128), table.dtype), mesh=_sc_mesh(),
               scratch_types=[pltpu.SemaphoreType.DMA((k,))])
    def gather(table_hbm, src_hbm, out_hbm, sems):
        def body(src_vmem, out_vmem):
            copies = [pltpu.async_copy(table_hbm.at[src_vmem.at[j]], out_vmem.at[pl.ds(j * SC_WINDOW, SC_WINDOW)],
                                       sems.at[j]) for j in range(k)]
            for c in copies:
                c.wait()

        pltpu.emit_pipeline(
            body,
            grid=(m // (k * SC_WINDOW),),
            in_specs=[pl.BlockSpec((k, SC_WINDOW), lambda i: (i, 0))],
            out_specs=[pl.BlockSpec((k * SC_WINDOW, 128), lambda i: (i, 0))],
            core_axis_name=("core", "subcore"),
            dimension_semantics=(pltpu.PARALLEL,),
        )(src_hbm, out_hbm)

    return gather(table, src.reshape(m // SC_WINDOW, SC_WINDOW))


MOE_X_SLOTS = 3
MOE_Y_SLOTS = 2


def _moe_kernel(tiles_ref, xs_hbm, wg_hbm, wu_hbm, wd_hbm, ys_hbm,
                xbuf, ybuf, wgf_ref, wuf_ref, wdf_ref, wgb_ref, wub_ref, wdb_ref, xsem, ysem, wsem, *, nt):
    te_ref, tv_ref, ws_ref, nx_ref = (tiles_ref.at[i] for i in range(4))
    t = pl.program_id(0)
    grp = TM_MOE // 8

    def fetch_x(j):
        return pltpu.make_async_copy(xs_hbm.at[pl.ds(j * grp, grp)], xbuf.at[j % MOE_X_SLOTS],
                                     xsem.at[j % MOE_X_SLOTS])

    def store_y(j):
        return pltpu.make_async_copy(ybuf.at[j % MOE_Y_SLOTS], ys_hbm.at[pl.ds(j * grp, grp)],
                                     ysem.at[j % MOE_Y_SLOTS])

    def fetch_w(e, slot):
        return [pltpu.make_async_copy(w.at[e], buf.at[slot], wsem.at[slot, i])
                for i, (w, buf) in enumerate(((wg_hbm, wgf_ref), (wu_hbm, wuf_ref), (wd_hbm, wdf_ref)))]

    @pl.when(t == 0)
    def _():
        for c in fetch_w(te_ref[0], 0):
            c.start()
        for j in range(MOE_X_SLOTS - 1):
            @pl.when(tv_ref[j] == 1)
            def _():
                fetch_x(j).start()

    ahead = t + (MOE_X_SLOTS - 1)

    @pl.when((ahead < nt) & (tv_ref[jnp.minimum(ahead, nt - 1)] == 1))
    def _():
        fetch_x(ahead).start()

    slot = ws_ref[t]

    @pl.when((t == 0) | (te_ref[t] != te_ref[jnp.maximum(t - 1, 0)]))
    def _():
        @pl.when(nx_ref[t] >= 0)
        def _():
            for c in fetch_w(nx_ref[t], 1 - slot):
                c.start()

        for c in fetch_w(te_ref[t], slot):
            c.wait()
        wgb_ref[...] = wgf_ref[slot].astype(BF16)
        wub_ref[...] = wuf_ref[slot].astype(BF16)
        wdb_ref[...] = wdf_ref[slot].astype(BF16)

    @pl.when((t >= MOE_Y_SLOTS) & (tv_ref[jnp.maximum(t - MOE_Y_SLOTS, 0)] == 1))
    def _():
        store_y(t - MOE_Y_SLOTS).wait()

    @pl.when(tv_ref[t] == 1)
    def _():
        fetch_x(t).wait()
        xb = _load_packed_rows(xbuf.at[t % MOE_X_SLOTS]).astype(BF16)
        g = _dot(xb, wgb_ref[...])
        u = _dot(xb, wub_ref[...])
        h = (g * _sigmoid(g) * u).astype(BF16)
        _store_packed_rows(ybuf.at[t % MOE_Y_SLOTS], _dot(h, wdb_ref[...]))
        store_y(t).start()

    @pl.when(t == nt - 1)
    def _():
        for j in range(nt - MOE_Y_SLOTS, nt):
            @pl.when(tv_ref[j] == 1)
            def _():
                store_y(j).wait()


def _moe(tiles, nt, xs, wg, wu, wd):
    tm = TM_MOE
    any_spec = pl.BlockSpec(memory_space=pl.ANY)
    return pl.pallas_call(
        functools.partial(_moe_kernel, nt=nt),
        grid_spec=pltpu.PrefetchScalarGridSpec(
            num_scalar_prefetch=1,
            grid=(nt,),
            in_specs=[any_spec] * 4,
            out_specs=any_spec,
            scratch_shapes=[pltpu.VMEM((MOE_X_SLOTS, tm // 8, 32, 128), U32),
                            pltpu.VMEM((MOE_Y_SLOTS, tm // 8, 32, 128), U32),
                            pltpu.VMEM((2, D_MODEL, D_EXPERT), F32), pltpu.VMEM((2, D_MODEL, D_EXPERT), F32),
                            pltpu.VMEM((2, D_EXPERT, D_MODEL), F32),
                            pltpu.VMEM((D_MODEL, D_EXPERT), BF16), pltpu.VMEM((D_MODEL, D_EXPERT), BF16),
                            pltpu.VMEM((D_EXPERT, D_MODEL), BF16),
                            pltpu.SemaphoreType.DMA((MOE_X_SLOTS,)), pltpu.SemaphoreType.DMA((MOE_Y_SLOTS,)),
                            pltpu.SemaphoreType.DMA((2, 3))]),
        out_shape=jax.ShapeDtypeStruct((nt * tm // 8, 32, 128), U32),
        compiler_params=pltpu.CompilerParams(dimension_semantics=("arbitrary",), vmem_limit_bytes=VMEM_LIMIT),
        name="moe",
    )(tiles, xs, wg, wu, wd)


def _final_kernel(y0_ref, y1_ref, x1_ref, p_ref, routet_ref, wple_ref, wpg_ref, g2_ref, b2_ref, out_ref):
    tm = x1_ref.shape[0]
    route = routet_ref[...].T
    h = tm // 4
    parts = tuple(slice(i * h, (i + 1) * h) for i in range(4))
    x1s = [x1_ref[r, :] for r in parts]
    plins = [_dot(p_ref[r, :].astype(BF16), wple_ref[...]) for r in parts]
    gates = [_dot(x1.astype(BF16), wpg_ref[...]) for x1 in x1s]
    sums = []
    for i, r in enumerate(parts):
        g8 = slice(i * h // 8, (i + 1) * h // 8)
        sums.append(DEEPNORM_ALPHA * x1s[i] + route[r, 2:3] * _load_packed_rows(y0_ref.at[g8])
                    + route[r, 3:4] * _load_packed_rows(y1_ref.at[g8]))
    for i, r in enumerate(parts):
        out_ref[r, :] = _ln((sums[i] + plins[i]) + plins[i] * jnp.tanh(gates[i]), g2_ref[...], b2_ref[...])


FIN_BUFFERS = 3


def _final(yg, x1, p2, routet, wple, wpg, g2, b2):
    n = x1.shape[0]
    tm = TM_FIN
    nt = n // tm
    deep = pl.Buffered(FIN_BUFFERS)
    rows = lambda w, **kw: pl.BlockSpec((tm, w), lambda t: (t, 0), **kw)

    def outer(yg_hbm, x1_hbm, p_hbm, routet_hbm, wple_ref, wpg_ref, g2_ref, b2_ref, out_hbm):
        def body(y0_ref, y1_ref, x1_ref, p_ref, routet_ref, out_ref):
            _final_kernel(y0_ref, y1_ref, x1_ref, p_ref, routet_ref, wple_ref, wpg_ref, g2_ref, b2_ref, out_ref)

        pltpu.emit_pipeline(
            body,
            grid=(nt,),
            in_specs=[pl.BlockSpec((tm // 8, 32, 128), lambda t: (t, 0, 0), pipeline_mode=deep),
                      pl.BlockSpec((tm // 8, 32, 128), lambda t: (t + nt, 0, 0), pipeline_mode=deep),
                      rows(D_MODEL, pipeline_mode=deep), rows(PLE_DIM), pl.BlockSpec((8, tm), lambda t: (0, t))],
            out_specs=[rows(D_MODEL)],
        )(yg_hbm, yg_hbm, x1_hbm, p_hbm, routet_hbm, out_hbm)

    any_spec = pl.BlockSpec(memory_space=pl.ANY)
    vmem = pl.BlockSpec(memory_space=pltpu.VMEM)
    return pl.pallas_call(
        outer,
        in_specs=[any_spec] * 4 + [vmem] * 4,
        out_specs=any_spec,
        out_shape=jax.ShapeDtypeStruct((n, D_MODEL), F32),
        compiler_params=pltpu.CompilerParams(vmem_limit_bytes=VMEM_LIMIT),
        name="final",
    )(yg, x1, p2, routet, wple, wpg, g2, b2)


def _route_tables_kernel(e_ref, piece_ref, tab_ref):
    r = e_ref.shape[0]
    e = e_ref[...]
    ri = lax.broadcasted_iota(I32, (128, 128), 0)
    ci = lax.broadcasted_iota(I32, (128, 128), 1)
    upper = jnp.where(ri <= ci, 1.0, 0.0).astype(BF16)
    rr = lax.broadcasted_iota(I32, (r, r), 0)
    rc = lax.broadcasted_iota(I32, (r, r), 1)
    below = jnp.where(rc < rr, 1.0, 0.0).astype(BF16)
    lane = lax.broadcasted_iota(I32, (1, 128), 1)

    experts = range(N_EXPERTS_TOTAL)
    ms = [jnp.where(e == x, 1.0, 0.0) for x in experts]
    pres = [_dot(m.astype(BF16), upper) for m in ms]
    tots = [jnp.broadcast_to(pre[:, 127:128], (r, 128)) for pre in pres]
    offs_rows = [_dot(below, tot.astype(BF16)) for tot in tots]
    rank = jnp.zeros((r, 128), F32)
    counts = jnp.zeros((1, 128), F32)
    for x in experts:
        rank = rank + ms[x] * (pres[x] + offs_rows[x])
        counts = jnp.where(lane == x, offs_rows[x][r - 1:r, :] + tots[x][r - 1:r, :], counts)
    padded = jnp.floor((counts + (TM_MOE - 1)) * (1.0 / TM_MOE)) * TM_MOE
    ends = _dot(jnp.broadcast_to(padded, (8, 128)).astype(BF16), upper)[0:1, :]
    offs = ends - padded

    nt = r * 128 // TM_MOE + N_EXPERTS_TOTAL
    ones = jnp.ones((128, 128), BF16)
    lanef = lane.astype(F32)
    diag = ri == ci
    ends_t = ends * (1.0 / TM_MOE)
    ecol = _dot(jnp.where(diag, jnp.broadcast_to(ends_t, (128, 128)), 0.0).astype(BF16), ones)
    passed = jnp.where((ecol <= ci.astype(F32)) & (ri < N_EXPERTS_TOTAL), 1.0, 0.0)
    te = jnp.minimum(jnp.sum(passed, axis=0, keepdims=True), N_EXPERTS_TOTAL - 1.0)
    tv = jnp.where(lanef < ecol[N_EXPERTS_TOTAL - 1:N_EXPERTS_TOTAL, :], 1.0, 0.0)
    shift = jnp.where(ri + 1 == ci, 1.0, 0.0).astype(BF16)
    te_prev = _dot(jnp.broadcast_to(te, (8, 128)).astype(BF16), shift)[0:1, :]
    first = jnp.where((lane == 0) | (te != te_prev), 1.0, 0.0)
    run = _dot(jnp.broadcast_to(first, (8, 128)).astype(BF16), upper)[0:1, :] - 1.0
    ws = run - 2.0 * jnp.floor(run * 0.5)
    tcol = _dot(jnp.where(diag, jnp.broadcast_to(te, (128, 128)), 0.0).astype(BF16), ones)
    later = jnp.where((tcol > te) & (ri < nt), tcol, float(N_EXPERTS_TOTAL))
    nx = jnp.min(later, axis=0, keepdims=True)
    nx = jnp.where(nx == N_EXPERTS_TOTAL, -1.0, nx)
    row8 = lax.broadcasted_iota(I32, (8, 128), 0)
    tab = jnp.where(row8 == 0, te, jnp.where(row8 == 1, tv, jnp.where(row8 == 2, ws, jnp.where(row8 == 3, nx, 0.0))))
    tab_ref[...] = tab.astype(I32)

    pos = rank - 1.0
    for x in range(N_EXPERTS_TOTAL):
        pos = pos + jnp.where(e == x, offs[:, x:x + 1], 0.0)

    hi = jnp.floor(pos * (1.0 / 256.0))
    lo = pos - 256.0 * hi
    jv = ((lane % 32) // 8).astype(F32)
    for c in range(4):
        sel = jnp.where(ri == 32 * c + 8 * (ci // 32) + ci % 8, 1.0, 0.0).astype(BF16)
        pc = 256.0 * _dot(hi.astype(BF16), sel) + _dot(lo.astype(BF16), sel)
        p8 = jnp.floor(pc * 0.125)
        piece = p8 * (8.0 * SUBROWS) + (pc - 8.0 * p8) + 8.0 * jv
        piece_ref[pl.ds(c, r, stride=4), :] = piece.astype(I32)


def _routing_tables(routet, n):
    nt = (2 * n) // TM_MOE + N_EXPERTS_TOTAL
    assert nt <= 128, "the tile tables hold one tile per lane"
    r = 2 * n // 128
    piece, tiles = pl.pallas_call(
        _route_tables_kernel,
        out_shape=[jax.ShapeDtypeStruct((4 * r, 128), I32), jax.ShapeDtypeStruct((8, 128), I32)],
        compiler_params=pltpu.CompilerParams(vmem_limit_bytes=VMEM_LIMIT),
        name="route_tables",
    )(routet[0:2].reshape(r, 128))
    return tiles, nt, piece.reshape(2, n * SUBROWS)


def kernel(x, p, w_in, a_ln_g, a_ln_b, a_ws, a_bs, w_a_proj, w_b_proj, w_o, ln1_g, ln1_b, w_group_router,
           b_group_router, w_expert_router, b_expert_router, w_gate, w_up, w_down, w_ple, w_ple_gate,
           ln2_g, ln2_b):
    bsz, s, d = x.shape
    n = bsz * s
    assert d == D_MODEL and s % (SPAN * max(B_DILATIONS)) == 0 and n % TM_PROJ == 0
    assert w_in.shape[0] == 1, "one layer"

    a_bias = jnp.repeat(a_bs[0].T, A_WIDTH // 8, axis=1)

    ga, gates, qkv1, qkv2, qkv3 = _proj(x, w_in[0], a_ln_g, a_ln_b, a_ws[0], a_bias)
    o1, l1 = _attn(qkv1.reshape(bsz, 1, s, 3 * COL))
    o2, l2 = _attn(qkv2)
    o3, l3 = _attn(qkv3)

    pad = 128 - N_GROUPS - N_EXPERTS_TOTAL
    wr = jnp.concatenate([w_group_router[0], w_expert_router[0].reshape(d, N_EXPERTS_TOTAL),
                          jnp.zeros((d, pad), F32)], axis=1).astype(BF16)
    br = jnp.concatenate([b_group_router[0], b_expert_router[0].reshape(-1), jnp.zeros((pad,), F32)])[None, :]
    x1, x1p, routet = _mix(
        ga, gates, o1.reshape(n, B_WIDTH), o2, o3, l1.reshape(n, 128), l2, l3, x.reshape(n, d),
        w_a_proj[0].astype(BF16), w_b_proj[0].astype(BF16), w_o[0].astype(BF16), wr, br, ln1_g, ln1_b)

    tiles, nt, piece = _routing_tables(routet, n)
    xs = _sc_scatter_rows(x1p.reshape(n * SUBROWS, 128), piece, nt * TM_MOE * SUBROWS)
    ys = _moe(tiles, nt, xs.reshape(nt * TM_MOE // 8, 32, 128),
              w_gate[0].reshape(N_EXPERTS_TOTAL, d, D_EXPERT), w_up[0].reshape(N_EXPERTS_TOTAL, d, D_EXPERT),
              w_down[0].reshape(N_EXPERTS_TOTAL, D_EXPERT, d)).reshape(nt * TM_MOE * SUBROWS, 128)
    yg = _sc_gather_rows(ys, piece.reshape(-1))
    out = _final(yg.reshape(n // 4, 32, 128), x1, p[0].reshape(n, PLE_DIM), routet,
                 (0.5 * w_ple[0]).astype(BF16), (0.5 * w_ple_gate[0]).astype(BF16), ln2_g, ln2_b)
    return out.reshape(bsz, s, d)
```

```python
import functools

import jax
import jax.numpy as jnp
from jax import lax
from jax.experimental import pallas as pl
from jax.experimental.pallas import tpu as pltpu
from jax.experimental.pallas import tpu_sc as plsc

F32 = jnp.float32
BF16 = jnp.bfloat16
U32 = jnp.uint32
I32 = jnp.int32

D_MODEL = 1024
PLE_DIM = 256
A_WIDTH = 512
A_CHUNK = 128
B_HEAD_DIM = 64
B_HEADS = 8
B_WIDTH = 512
B_DILATIONS = (1, 4, 16)
SPAN = 128
N_GROUPS = 4
N_EXPERTS = 8
N_EXPERTS_TOTAL = N_GROUPS * N_EXPERTS
D_EXPERT = 256
DEEPNORM_ALPHA = 2.0 ** 0.25
LN_EPS = 1e-5
COL = 512
NEG = -1e30

VMEM_LIMIT = 56 * 1024 * 1024

TM_PROJ = 512
TM_ATTN = 1024
TM_MIX = 512
TM_MOE = 512
TM_FIN = 1024


def _ln(x, g, b):
    mu = jnp.mean(x, axis=-1, keepdims=True)
    xc = x - mu
    var = jnp.mean(xc * xc, axis=-1, keepdims=True)
    return xc * lax.rsqrt(var + LN_EPS) * g + b


def _gelu_tanh(x):
    return 0.5 * x * (1.0 + jnp.tanh(0.7978845608028654 * (x + 0.044715 * (x * x * x))))


def _sigmoid(x):
    return 0.5 * jnp.tanh(0.5 * x) + 0.5


def _dot(a, b):
    return jnp.dot(a, b, preferred_element_type=F32)


PACK_W = D_MODEL // 2
SUBROWS = PACK_W // 128


def _store_packed_rows(ref, x):
    m = x.shape[0]
    xb = x.astype(BF16).astype(F32)
    lo = pltpu.bitcast(xb[:, :PACK_W], U32) >> 16
    hi = pltpu.bitcast(xb[:, PACK_W:], U32) & jnp.uint32(0xFFFF0000)
    w = hi | lo
    for j in range(SUBROWS):
        ref[:, 8 * j:8 * (j + 1), :] = w[:, 128 * j:128 * (j + 1)].reshape(m // 8, 8, 128)


def _load_packed_rows(ref):
    m = ref.shape[0] * 8
    ws = [ref[:, 8 * j:8 * (j + 1), :].reshape(m, 128) for j in range(SUBROWS)]
    lo = [pltpu.bitcast(w << 16, F32) for w in ws]
    hi = [pltpu.bitcast(w & jnp.uint32(0xFFFF0000), F32) for w in ws]
    return jnp.concatenate(lo + hi, axis=1)


W_IN_BLOCKS = 15
W_IN_SLOTS = 4


def _proj_kernel(x_ref, w_hbm, lng_ref, lnb_ref, ws_ref, bias_ref,
                 ga_ref, gates_ref, qkv1_ref, qkv2_ref, qkv3_ref, xc_ref, w, wstage_ref, wsem):
    tm = x_ref.shape[0]

    @pl.when(pl.program_id(0) == 0)
    def _():
        def fetch(j):
            return pltpu.make_async_copy(w_hbm.at[:, pl.ds(j * COL, COL)], wstage_ref.at[j % W_IN_SLOTS],
                                         wsem.at[j % W_IN_SLOTS])

        for j in range(W_IN_SLOTS):
            fetch(j).start()
        for j in range(W_IN_BLOCKS):
            fetch(j).wait()
            w[j] = wstage_ref[j % W_IN_SLOTS].astype(BF16)
            if j + W_IN_SLOTS < W_IN_BLOCKS:
                fetch(j + W_IN_SLOTS).start()

    xb = x_ref[...].astype(BF16)

    u_raw = _dot(xb, w[0])
    v_raw = _dot(xb, w[1])

    for i in range(4):
        gates_ref[:, i * COL:(i + 1) * COL] = _sigmoid(_dot(xb, w[11 + i])).astype(BF16)
    for j in range(3):
        qkv1_ref[:, j * COL:(j + 1) * COL] = _dot(xb, w[2 + 3 * j]).astype(BF16)

    for c in range(D_MODEL // 128):
        xc_ref[c] = x_ref[:, c * 128:(c + 1) * 128]
    for gi, out_ref in ((1, qkv2_ref), (2, qkv3_ref)):
        dl = B_DILATIONS[gi]
        per = tm // dl
        xp = jnp.concatenate(
            [jnp.concatenate([xc_ref[c, pl.ds(r, per, stride=dl), :] for c in range(D_MODEL // 128)], axis=1)
             for r in range(dl)], axis=0).astype(BF16)
        for j in range(3):
            res = _dot(xp, w[2 + 3 * j + gi]).astype(BF16)
            for r in range(dl):
                out_ref[r, :, j * COL:(j + 1) * COL] = res[r * per:(r + 1) * per]

    u = _gelu_tanh(u_raw)
    v = _gelu_tanh(v_raw)
    vn = _ln(v, lng_ref[...], lnb_ref[...]).astype(BF16)

    row = lax.broadcasted_iota(I32, (A_CHUNK, A_CHUNK), 0)
    colm = lax.broadcasted_iota(I32, (A_CHUNK, A_CHUNK), 1)
    causal = colm <= row
    lo = colm < 64
    zero = jnp.zeros((A_CHUNK, A_CHUNK), BF16)
    wcat = []
    for j in range(4):
        w0 = jnp.where(causal, ws_ref[2 * j], 0.0).astype(BF16)
        w1 = jnp.where(causal, ws_ref[2 * j + 1], 0.0).astype(BF16)
        wcat.append(jnp.concatenate([w0, w1], axis=1))
    for c in range(tm // A_CHUNK):
        r0 = c * A_CHUNK
        for j in range(4):
            c0 = j * 128
            vt = vn[r0:r0 + A_CHUNK, c0:c0 + 128]
            rhs = jnp.concatenate([jnp.where(lo, vt, zero), jnp.where(lo, zero, vt)], axis=0)
            mixed = _dot(wcat[j], rhs) + bias_ref[:, c0:c0 + 128]
            ga_ref[r0:r0 + A_CHUNK, c0:c0 + 128] = (u[r0:r0 + A_CHUNK, c0:c0 + 128] * mixed).astype(BF16)


def _proj(x, w_in, a_ln_g, a_ln_b, a_ws, a_bias):
    bsz, s, _ = x.shape
    n = bsz * s
    tm = TM_PROJ
    tiles = s // tm
    x2 = x.reshape(n, D_MODEL)
    full = lambda shape: pl.BlockSpec(shape, lambda i: (0,) * len(shape))
    rows = lambda width: pl.BlockSpec((tm, width), lambda i: (i, 0))
    dil = lambda dl: pl.BlockSpec((None, dl, tm // dl, 3 * COL), lambda i: (i // tiles, 0, i % tiles, 0))
    return pl.pallas_call(
        _proj_kernel,
        grid=(n // tm,),
        in_specs=[rows(D_MODEL), pl.BlockSpec(memory_space=pl.ANY)]
                 + [full((1, A_WIDTH)), full((1, A_WIDTH)), full((8, A_CHUNK, A_CHUNK)), full((A_CHUNK, A_WIDTH))],
        out_specs=[rows(A_WIDTH), rows(4 * COL), rows(3 * COL), dil(4), dil(16)],
        out_shape=[jax.ShapeDtypeStruct((n, A_WIDTH), BF16),
                   jax.ShapeDtypeStruct((n, 4 * COL), BF16),
                   jax.ShapeDtypeStruct((n, 3 * COL), BF16),
                   jax.ShapeDtypeStruct((bsz, 4, s // 4, 3 * COL), BF16),
                   jax.ShapeDtypeStruct((bsz, 16, s // 16, 3 * COL), BF16)],
        scratch_shapes=[pltpu.VMEM((D_MODEL // 128, tm, 128), F32),
                        pltpu.VMEM((W_IN_BLOCKS, D_MODEL, COL), BF16),
                        pltpu.VMEM((W_IN_SLOTS, D_MODEL, COL), F32),
                        pltpu.SemaphoreType.DMA((W_IN_SLOTS,))],
        compiler_params=pltpu.CompilerParams(dimension_semantics=("arbitrary",), vmem_limit_bytes=VMEM_LIMIT),
        name="proj",
    )(x2, w_in, a_ln_g, a_ln_b, a_ws, a_bias)


def _attn_kernel(qkv_ref, o_ref, lse_ref, *, ns, seq):
    nb = seq // SPAN
    lane = lax.broadcasted_iota(I32, (SPAN, 128), 1)
    lo = lane < 64
    lane16 = lane // 16
    qi = lax.broadcasted_iota(I32, (SPAN, 2 * SPAN), 0)
    ki = lax.broadcasted_iota(I32, (SPAN, 2 * SPAN), 1)
    causal = lax.broadcasted_iota(I32, (SPAN, SPAN), 1) <= lax.broadcasted_iota(I32, (SPAN, SPAN), 0)
    bias_first = jnp.where(causal, 0.0, NEG).astype(F32)
    bias_first = jnp.concatenate([bias_first, bias_first], axis=0)
    bias_main = jnp.where((ki >= qi) & (ki <= qi + SPAN), 0.0, NEG).astype(F32)
    bias_main = jnp.concatenate([bias_main, bias_main], axis=0)
    zero = jnp.zeros((SPAN, 128), BF16)

    for s in range(ns):
        def block(row0, start, bias, s=s):
            win = bias.shape[1]
            pairs = range(B_HEADS // 2)
            scores, values = [], []
            for jp in pairs:
                c0 = jp * 128
                q = qkv_ref[s, pl.ds(row0, SPAN), c0:c0 + 128] * jnp.asarray(0.125, BF16)
                k = qkv_ref[s, pl.ds(start, win), COL + c0:COL + c0 + 128]
                values.append(qkv_ref[s, pl.ds(start, win), 2 * COL + c0:2 * COL + c0 + 128])
                qs = jnp.concatenate([jnp.where(lo, q, zero), jnp.where(lo, zero, q)], axis=0)
                scores.append(lax.dot_general(qs, k, (((1,), (1,)), ((), ())), preferred_element_type=F32) + bias)
            probs, maxes, sums = [], [], []
            for jp in pairs:
                m = jnp.max(scores[jp], axis=-1, keepdims=True)
                p = jnp.exp(scores[jp] - m)
                maxes.append(m)
                sums.append(jnp.sum(p, axis=-1, keepdims=True))
                probs.append(p.astype(BF16))
            lse_tile = jnp.zeros((SPAN, 128), F32)
            for jp in pairs:
                c0 = jp * 128
                ov = _dot(probs[jp], values[jp])
                inv = 1.0 / sums[jp]
                o = jnp.where(lo, ov[:SPAN] * inv[:SPAN], ov[SPAN:] * inv[SPAN:])
                o_ref[pl.ds(row0, SPAN), s * B_WIDTH + c0:s * B_WIDTH + c0 + 128] = o.astype(BF16)
                lse = maxes[jp] + jnp.log(sums[jp])
                lse_tile = jnp.where(lane16 == 2 * jp, lse[:SPAN],
                                     jnp.where(lane16 == 2 * jp + 1, lse[SPAN:], lse_tile))
            lse_ref[pl.ds(row0, SPAN), s * 128:(s + 1) * 128] = lse_tile

        block(0, 0, bias_first)
        if nb > 1:
            def body(i, carry):
                block(pl.multiple_of(i * SPAN, SPAN), pl.multiple_of((i - 1) * SPAN, SPAN), bias_main)
                return carry
            lax.fori_loop(1, nb, body, 0, unroll=min(5, nb - 1))


def _attn(qkv_g):
    bsz, dl, seq, _ = qkv_g.shape
    ns = max(1, min(dl, TM_ATTN // seq))
    return pl.pallas_call(
        functools.partial(_attn_kernel, ns=ns, seq=seq),
        grid=(bsz, dl // ns),
        in_specs=[pl.BlockSpec((None, ns, seq, 3 * COL), lambda b, r: (b, r, 0, 0))],
        out_specs=[pl.BlockSpec((None, seq, ns * B_WIDTH), lambda b, r: (b, 0, r)),
                   pl.BlockSpec((None, seq, ns * 128), lambda b, r: (b, 0, r))],
        out_shape=[jax.ShapeDtypeStruct((bsz, seq, dl * B_WIDTH), BF16),
                   jax.ShapeDtypeStruct((bsz, seq, dl * 128), F32)],
        compiler_params=pltpu.CompilerParams(dimension_semantics=("parallel", "parallel"),
                                             vmem_limit_bytes=VMEM_LIMIT),
        name=f"attn{dl}",
    )(qkv_g)


def _natural_rows(ref, dl, scr):
    nchunk, tm, _ = scr.shape
    w = nchunk * 128
    per = tm // dl
    for r in range(dl):
        for c in range(nchunk):
            scr[c, pl.ds(r, per, stride=dl), :] = ref[:, r * w + c * 128:r * w + (c + 1) * 128].astype(F32)
    return jnp.concatenate([scr[c] for c in range(nchunk)], axis=1)


def _mix_kernel(ga_ref, gates_ref, o1_ref, o2_ref, o3_ref, l1_ref, l2_ref, l3_ref, x_ref,
                wa_ref, wb_ref, wo_ref, wr_ref, br_ref, g1_ref, b1_ref,
                x1_ref, x1p_ref, routet_ref, o2s_ref, o3s_ref, l2s_ref, l3s_ref):
    tm = x_ref.shape[0]
    o2 = _natural_rows(o2_ref, 4, o2s_ref)
    o3 = _natural_rows(o3_ref, 16, o3s_ref)
    l2 = _natural_rows(l2_ref, 4, l2s_ref)
    l3 = _natural_rows(l3_ref, 16, l3s_ref)
    er = lax.broadcasted_iota(I32, (256, B_WIDTH), 0)
    ec = lax.broadcasted_iota(I32, (256, B_WIDTH), 1)
    expand = jnp.where(er % 128 == (ec // B_HEAD_DIM) * 16, 1.0, 0.0).astype(BF16)

    def widen(w):
        hi = w.astype(BF16)
        lo = (w - hi.astype(F32)).astype(BF16)
        return _dot(jnp.concatenate([hi, lo], axis=1), expand)

    h = tm // 2
    halves = (slice(0, h), slice(h, tm))
    obs = []
    for r in halves:
        l1 = l1_ref[r, :]
        mx = jnp.maximum(l1, jnp.maximum(l2[r], l3[r]))
        e1, e2, e3 = jnp.exp(l1 - mx), jnp.exp(l2[r] - mx), jnp.exp(l3[r] - mx)
        inv = 1.0 / (e1 + e2 + e3)
        obs.append(widen(e1 * inv) * o1_ref[r, :].astype(F32) + widen(e2 * inv) * o2[r] + widen(e3 * inv) * o3[r])
    ybs = [_dot(ob.astype(BF16), wb_ref[...]) for ob in obs]
    yas = [_dot(ga_ref[r, :], wa_ref[...]) for r in halves]
    pres = [gates_ref[r, :D_MODEL].astype(F32) * ya + gates_ref[r, D_MODEL:].astype(F32) * yb
            for r, ya, yb in zip(halves, yas, ybs)]
    mixes = [_dot(pre.astype(BF16), wo_ref[...]) for pre in pres]
    x1s = [_ln(DEEPNORM_ALPHA * x_ref[r, :] + mix, g1_ref[...], b1_ref[...]) for r, mix in zip(halves, mixes)]
    logits = [_dot(x1.astype(BF16), wr_ref[...]) + br_ref[...] for x1 in x1s]

    nrow = 40
    row = lax.broadcasted_iota(I32, (nrow, h), 0).astype(F32)
    row8 = lax.broadcasted_iota(I32, (8, h), 0)
    big = 1e9
    for i, r in enumerate(halves):
        x1_ref[r, :] = x1s[i]
        _store_packed_rows(x1p_ref.at[i * h // 8:(i + 1) * h // 8], x1s[i])
        lg = logits[i].T[:nrow, :]
        gl = jnp.where(row < N_GROUPS, lg, NEG)
        gm = jnp.max(gl, axis=0, keepdims=True)
        gidx = jnp.min(jnp.where(gl == gm, row, big), axis=0, keepdims=True)
        gsum = jnp.sum(jnp.where(row < N_GROUPS, jnp.exp(gl - gm), 0.0), axis=0, keepdims=True)
        gprob = 1.0 / gsum
        lo_row = N_GROUPS + N_EXPERTS * gidx
        el = jnp.where((row >= lo_row) & (row < lo_row + N_EXPERTS), lg, NEG)
        v1 = jnp.max(el, axis=0, keepdims=True)
        i1 = jnp.min(jnp.where(el == v1, row, big), axis=0, keepdims=True)
        el2 = jnp.where(row == i1, NEG, el)
        v2 = jnp.max(el2, axis=0, keepdims=True)
        i2 = jnp.min(jnp.where(el2 == v2, row, big), axis=0, keepdims=True)
        t = jnp.exp(v2 - v1)
        w1 = 1.0 / (1.0 + t)
        w2 = t * w1
        routet_ref[:, r] = jnp.where(row8 == 0, i1 - N_GROUPS,
                                     jnp.where(row8 == 1, i2 - N_GROUPS,
                                               jnp.where(row8 == 2, gprob * w1,
                                                         jnp.where(row8 == 3, gprob * w2, 0.0))))


def _mix(ga, gates, o1, o2, o3, l1, l2, l3, x2, wa, wb, wo, wr, br, g1, b1):
    n = x2.shape[0]
    bsz = o2.shape[0]
    tm = TM_MIX
    tiles = n // bsz // tm
    rows = lambda w: pl.BlockSpec((tm, w), lambda i: (i, 0))
    grouped = lambda a, dl: pl.BlockSpec((None, tm // dl, a.shape[2]), lambda i: (i // tiles, i % tiles, 0))
    full = lambda a: pl.BlockSpec(a.shape, lambda i: (0,) * a.ndim)
    return pl.pallas_call(
        _mix_kernel,
        grid=(n // tm,),
        in_specs=[rows(A_WIDTH), rows(2 * D_MODEL), rows(B_WIDTH), grouped(o2, 4), grouped(o3, 16),
                  rows(128), grouped(l2, 4), grouped(l3, 16), rows(D_MODEL),
                  full(wa), full(wb), full(wo), full(wr), full(br), full(g1), full(b1)],
        out_specs=[rows(D_MODEL), pl.BlockSpec((tm // 8, 32, 128), lambda i: (i, 0, 0)),
                   pl.BlockSpec((8, tm), lambda i: (0, i))],
        out_shape=[jax.ShapeDtypeStruct((n, D_MODEL), F32),
                   jax.ShapeDtypeStruct((n // 8, 32, 128), U32),
                   jax.ShapeDtypeStruct((8, n), F32)],
        scratch_shapes=[pltpu.VMEM((B_WIDTH // 128, tm, 128), F32), pltpu.VMEM((B_WIDTH // 128, tm, 128), F32),
                        pltpu.VMEM((1, tm, 128), F32), pltpu.VMEM((1, tm, 128), F32)],
        compiler_params=pltpu.CompilerParams(dimension_semantics=("parallel",), vmem_limit_bytes=VMEM_LIMIT),
        name="mix",
    )(ga, gates, o1, o2, o3, l1, l2, l3, x2, wa, wb, wo, wr, br, g1, b1)


SC_WINDOW = 128


def _sc_mesh():
    return plsc.VectorSubcoreMesh(core_axis_name="core", subcore_axis_name="subcore")


def _sc_scatter_rows(rows, dst, n_out):
    r = rows.shape[0]
    k = 2

    @pl.kernel(out_type=jax.ShapeDtypeStruct((n_out, 128), rows.dtype), mesh=_sc_mesh(),
               scratch_types=[pltpu.SemaphoreType.DMA((2 * k,))])
    def scatter(rows_hbm, dst0_hbm, dst1_hbm, out_hbm, sems):
        def body(rows_vmem, dst0_vmem, dst1_vmem):
            copies = []
            for j in range(k):
                window = rows_vmem.at[pl.ds(j * SC_WINDOW, SC_WINDOW)]
                copies.append(pltpu.async_copy(window, out_hbm.at[dst0_vmem.at[j]], sems.at[2 * j]))
                copies.append(pltpu.async_copy(window, out_hbm.at[dst1_vmem.at[j]], sems.at[2 * j + 1]))
            for c in copies:
                c.wait()

        pltpu.emit_pipeline(
            body,
            grid=(r // (k * SC_WINDOW),),
            in_specs=[pl.BlockSpec((k * SC_WINDOW, 128), lambda i: (i, 0)),
                      pl.BlockSpec((k, SC_WINDOW), lambda i: (i, 0)),
                      pl.BlockSpec((k, SC_WINDOW), lambda i: (i, 0))],
            out_specs=[],
            core_axis_name=("core", "subcore"),
            dimension_semantics=(pltpu.PARALLEL,),
        )(rows_hbm, dst0_hbm, dst1_hbm)

    return scatter(rows, dst[0].reshape(r // SC_WINDOW, SC_WINDOW), dst[1].reshape(r // SC_WINDOW, SC_WINDOW))


def _sc_gather_rows(table, src):
    m = src.shape[0]
    k = 2

    @pl.kernel(out_type=jax.ShapeDtypeStruct((m, 128), table.dtype), mesh=_sc_mesh(),
               scratch_types=[pltpu.SemaphoreType.DMA((k,))])
    def gather(table_hbm, src_hbm, out_hbm, sems):
        def body(src_vmem, out_vmem):
            copies = [pltpu.async_copy(table_hbm.at[src_vmem.at[j]], out_vmem.at[pl.ds(j * SC_WINDOW, SC_WINDOW)],
                                       sems.at[j]) for j in range(k)]
            for c in copies:
                c.wait()

        pltpu.emit_pipeline(
            body,
            grid=(m // (k * SC_WINDOW),),
            in_specs=[pl.BlockSpec((k, SC_WINDOW), lambda i: (i, 0))],
            out_specs=[pl.BlockSpec((k * SC_WINDOW, 128), lambda i: (i, 0))],
            core_axis_name=("core", "subcore"),
            dimension_semantics=(pltpu.PARALLEL,),
        )(src_hbm, out_hbm)

    return gather(table, src.reshape(m // SC_WINDOW, SC_WINDOW))


MOE_X_SLOTS = 3
MOE_Y_SLOTS = 2


def _moe_kernel(tiles_ref, xs_hbm, wg_hbm, wu_hbm, wd_hbm, ys_hbm,
                xbuf, ybuf, wgf_ref, wuf_ref, wdf_ref, wgb_ref, wub_ref, wdb_ref, xsem, ysem, wsem, *, nt):
    te_ref, tv_ref, ws_ref, nx_ref = (tiles_ref.at[i] for i in range(4))
    t = pl.program_id(0)
    grp = TM_MOE // 8

    def fetch_x(j):
        return pltpu.make_async_copy(xs_hbm.at[pl.ds(j * grp, grp)], xbuf.at[j % MOE_X_SLOTS],
                                     xsem.at[j % MOE_X_SLOTS])

    def store_y(j):
        return pltpu.make_async_copy(ybuf.at[j % MOE_Y_SLOTS], ys_hbm.at[pl.ds(j * grp, grp)],
                                     ysem.at[j % MOE_Y_SLOTS])

    def fetch_w(e, slot):
        return [pltpu.make_async_copy(w.at[e], buf.at[slot], wsem.at[slot, i])
                for i, (w, buf) in enumerate(((wg_hbm, wgf_ref), (wu_hbm, wuf_ref), (wd_hbm, wdf_ref)))]

    @pl.when(t == 0)
    def _():
        for c in fetch_w(te_ref[0], 0):
            c.start()
        for j in range(MOE_X_SLOTS - 1):
            @pl.when(tv_ref[j] == 1)
            def _():
                fetch_x(j).start()

    ahead = t + (MOE_X_SLOTS - 1)

    @pl.when((ahead < nt) & (tv_ref[jnp.minimum(ahead, nt - 1)] == 1))
    def _():
        fetch_x(ahead).start()

    slot = ws_ref[t]

    @pl.when((t == 0) | (te_ref[t] != te_ref[jnp.maximum(t - 1, 0)]))
    def _():
        @pl.when(nx_ref[t] >= 0)
        def _():
            for c in fetch_w(nx_ref[t], 1 - slot):
                c.start()

        for c in fetch_w(te_ref[t], slot):
            c.wait()
        wgb_ref[...] = wgf_ref[slot].astype(BF16)
        wub_ref[...] = wuf_ref[slot].astype(BF16)
        wdb_ref[...] = wdf_ref[slot].astype(BF16)

    @pl.when((t >= MOE_Y_SLOTS) & (tv_ref[jnp.maximum(t - MOE_Y_SLOTS, 0)] == 1))
    def _():
        store_y(t - MOE_Y_SLOTS).wait()

    @pl.when(tv_ref[t] == 1)
    def _():
        fetch_x(t).wait()
        xb = _load_packed_rows(xbuf.at[t % MOE_X_SLOTS]).astype(BF16)
        g = _dot(xb, wgb_ref[...])
        u = _dot(xb, wub_ref[...])
        h = (g * _sigmoid(g) * u).astype(BF16)
        _store_packed_rows(ybuf.at[t % MOE_Y_SLOTS], _dot(h, wdb_ref[...]))
        store_y(t).start()

    @pl.when(t == nt - 1)
    def _():
        for j in range(nt - MOE_Y_SLOTS, nt):
            @pl.when(tv_ref[j] == 1)
            def _():
                store_y(j).wait()


def _moe(tiles, nt, xs, wg, wu, wd):
    tm = TM_MOE
    any_spec = pl.BlockSpec(memory_space=pl.ANY)
    return pl.pallas_call(
        functools.partial(_moe_kernel, nt=nt),
        grid_spec=pltpu.PrefetchScalarGridSpec(
            num_scalar_prefetch=1,
            grid=(nt,),
            in_specs=[any_spec] * 4,
            out_specs=any_spec,
            scratch_shapes=[pltpu.VMEM((MOE_X_SLOTS, tm // 8, 32, 128), U32),
                            pltpu.VMEM((MOE_Y_SLOTS, tm // 8, 32, 128), U32),
                            pltpu.VMEM((2, D_MODEL, D_EXPERT), F32), pltpu.VMEM((2, D_MODEL, D_EXPERT), F32),
                            pltpu.VMEM((2, D_EXPERT, D_MODEL), F32),
                            pltpu.VMEM((D_MODEL, D_EXPERT), BF16), pltpu.VMEM((D_MODEL, D_EXPERT), BF16),
                            pltpu.VMEM((D_EXPERT, D_MODEL), BF16),
                            pltpu.SemaphoreType.DMA((MOE_X_SLOTS,)), pltpu.SemaphoreType.DMA((MOE_Y_SLOTS,)),
                            pltpu.SemaphoreType.DMA((2, 3))]),
        out_shape=jax.ShapeDtypeStruct((nt * tm // 8, 32, 128), U32),
        compiler_params=pltpu.CompilerParams(dimension_semantics=("arbitrary",), vmem_limit_bytes=VMEM_LIMIT),
        name="moe",
    )(tiles, xs, wg, wu, wd)


def _final_kernel(y0_ref, y1_ref, x1_ref, p_ref, routet_ref, wple_ref, wpg_ref, g2_ref, b2_ref, out_ref):
    tm = x1_ref.shape[0]
    route = routet_ref[...].T
    h = tm // 4
    parts = tuple(slice(i * h, (i + 1) * h) for i in range(4))
    x1s = [x1_ref[r, :] for r in parts]
    plins = [_dot(p_ref[r, :].astype(BF16), wple_ref[...]) for r in parts]
    gates = [_dot(x1.astype(BF16), wpg_ref[...]) for x1 in x1s]
    sums = []
    for i, r in enumerate(parts):
        g8 = slice(i * h // 8, (i + 1) * h // 8)
        sums.append(DEEPNORM_ALPHA * x1s[i] + route[r, 2:3] * _load_packed_rows(y0_ref.at[g8])
                    + route[r, 3:4] * _load_packed_rows(y1_ref.at[g8]))
    for i, r in enumerate(parts):
        out_ref[r, :] = _ln((sums[i] + plins[i]) + plins[i] * jnp.tanh(gates[i]), g2_ref[...], b2_ref[...])


def _final(yg, x1, p2, routet, wple, wpg, g2, b2):
    n = x1.shape[0]
    tm = TM_FIN
    nt = n // tm
    rows = lambda w: pl.BlockSpec((tm, w), lambda t: (t, 0))
    full = lambda a: pl.BlockSpec(a.shape, lambda t: (0,) * a.ndim)
    return pl.pallas_call(
        _final_kernel,
        grid=(nt,),
        in_specs=[pl.BlockSpec((tm // 8, 32, 128), lambda t: (t, 0, 0)),
                  pl.BlockSpec((tm // 8, 32, 128), lambda t: (t + nt, 0, 0)),
                  rows(D_MODEL), rows(PLE_DIM), pl.BlockSpec((8, tm), lambda t: (0, t)),
                  full(wple), full(wpg), full(g2), full(b2)],
        out_specs=rows(D_MODEL),
        out_shape=jax.ShapeDtypeStruct((n, D_MODEL), F32),
        compiler_params=pltpu.CompilerParams(dimension_semantics=("parallel",), vmem_limit_bytes=VMEM_LIMIT),
        name="final",
    )(yg, yg, x1, p2, routet, wple, wpg, g2, b2)


def _route_tables_kernel(e_ref, piece_ref, tab_ref):
    r = e_ref.shape[0]
    e = e_ref[...]
    ri = lax.broadcasted_iota(I32, (128, 128), 0)
    ci = lax.broadcasted_iota(I32, (128, 128), 1)
    upper = jnp.where(ri <= ci, 1.0, 0.0).astype(BF16)
    rr = lax.broadcasted_iota(I32, (r, r), 0)
    rc = lax.broadcasted_iota(I32, (r, r), 1)
    below = jnp.where(rc < rr, 1.0, 0.0).astype(BF16)
    lane = lax.broadcasted_iota(I32, (1, 128), 1)

    experts = range(N_EXPERTS_TOTAL)
    ms = [jnp.where(e == x, 1.0, 0.0) for x in experts]
    pres = [_dot(m.astype(BF16), upper) for m in ms]
    tots = [jnp.broadcast_to(pre[:, 127:128], (r, 128)) for pre in pres]
    offs_rows = [_dot(below, tot.astype(BF16)) for tot in tots]
    rank = jnp.zeros((r, 128), F32)
    counts = jnp.zeros((1, 128), F32)
    for x in experts:
        rank = rank + ms[x] * (pres[x] + offs_rows[x])
        counts = jnp.where(lane == x, offs_rows[x][r - 1:r, :] + tots[x][r - 1:r, :], counts)
    padded = jnp.floor((counts + (TM_MOE - 1)) * (1.0 / TM_MOE)) * TM_MOE
    ends = _dot(jnp.broadcast_to(padded, (8, 128)).astype(BF16), upper)[0:1, :]
    offs = ends - padded

    nt = r * 128 // TM_MOE + N_EXPERTS_TOTAL
    ones = jnp.ones((128, 128), BF16)
    lanef = lane.astype(F32)
    diag = ri == ci
    ends_t = ends * (1.0 / TM_MOE)
    ecol = _dot(jnp.where(diag, jnp.broadcast_to(ends_t, (128, 128)), 0.0).astype(BF16), ones)
    passed = jnp.where((ecol <= ci.astype(F32)) & (ri < N_EXPERTS_TOTAL), 1.0, 0.0)
    te = jnp.minimum(jnp.sum(passed, axis=0, keepdims=True), N_EXPERTS_TOTAL - 1.0)
    tv = jnp.where(lanef < ecol[N_EXPERTS_TOTAL - 1:N_EXPERTS_TOTAL, :], 1.0, 0.0)
    shift = jnp.where(ri + 1 == ci, 1.0, 0.0).astype(BF16)
    te_prev = _dot(jnp.broadcast_to(te, (8, 128)).astype(BF16), shift)[0:1, :]
    first = jnp.where((lane == 0) | (te != te_prev), 1.0, 0.0)
    run = _dot(jnp.broadcast_to(first, (8, 128)).astype(BF16), upper)[0:1, :] - 1.0
    ws = run - 2.0 * jnp.floor(run * 0.5)
    tcol = _dot(jnp.where(diag, jnp.broadcast_to(te, (128, 128)), 0.0).astype(BF16), ones)
    later = jnp.where((tcol > te) & (ri < nt), tcol, float(N_EXPERTS_TOTAL))
    nx = jnp.min(later, axis=0, keepdims=True)
    nx = jnp.where(nx == N_EXPERTS_TOTAL, -1.0, nx)
    row8 = lax.broadcasted_iota(I32, (8, 128), 0)
    tab = jnp.where(row8 == 0, te, jnp.where(row8 == 1, tv, jnp.where(row8 == 2, ws, jnp.where(row8 == 3, nx, 0.0))))
    tab_ref[...] = tab.astype(I32)

    pos = rank - 1.0
    for x in range(N_EXPERTS_TOTAL):
        pos = pos + jnp.where(e == x, offs[:, x:x + 1], 0.0)

    hi = jnp.floor(pos * (1.0 / 256.0))
    lo = pos - 256.0 * hi
    jv = ((lane % 32) // 8).astype(F32)
    for c in range(4):
        sel = jnp.where(ri == 32 * c + 8 * (ci // 32) + ci % 8, 1.0, 0.0).astype(BF16)
        pc = 256.0 * _dot(hi.astype(BF16), sel) + _dot(lo.astype(BF16), sel)
        p8 = jnp.floor(pc * 0.125)
        piece = p8 * (8.0 * SUBROWS) + (pc - 8.0 * p8) + 8.0 * jv
        piece_ref[pl.ds(c, r, stride=4), :] = piece.astype(I32)


def _routing_tables(routet, n):
    nt = (2 * n) // TM_MOE + N_EXPERTS_TOTAL
    assert nt <= 128, "the tile tables hold one tile per lane"
    r = 2 * n // 128
    piece, tiles = pl.pallas_call(
        _route_tables_kernel,
        out_shape=[jax.ShapeDtypeStruct((4 * r, 128), I32), jax.ShapeDtypeStruct((8, 128), I32)],
        compiler_params=pltpu.CompilerParams(vmem_limit_bytes=VMEM_LIMIT),
        name="route_tables",
    )(routet[0:2].reshape(r, 128))
    return tiles, nt, piece.reshape(2, n * SUBROWS)


def kernel(x, p, w_in, a_ln_g, a_ln_b, a_ws, a_bs, w_a_proj, w_b_proj, w_o, ln1_g, ln1_b, w_group_router,
           b_group_router, w_expert_router, b_expert_router, w_gate, w_up, w_down, w_ple, w_ple_gate,
           ln2_g, ln2_b):
    bsz, s, d = x.shape
    n = bsz * s
    assert d == D_MODEL and s % (SPAN * max(B_DILATIONS)) == 0 and n % TM_PROJ == 0
    assert w_in.shape[0] == 1, "one layer"

    a_bias = jnp.repeat(a_bs[0].T, A_WIDTH // 8, axis=1)

    ga, gates, qkv1, qkv2, qkv3 = _proj(x, w_in[0], a_ln_g, a_ln_b, a_ws[0], a_bias)
    o1, l1 = _attn(qkv1.reshape(bsz, 1, s, 3 * COL))
    o2, l2 = _attn(qkv2)
    o3, l3 = _attn(qkv3)

    pad = 128 - N_GROUPS - N_EXPERTS_TOTAL
    wr = jnp.concatenate([w_group_router[0], w_expert_router[0].reshape(d, N_EXPERTS_TOTAL),
                          jnp.zeros((d, pad), F32)], axis=1).astype(BF16)
    br = jnp.concatenate([b_group_router[0], b_expert_router[0].reshape(-1), jnp.zeros((pad,), F32)])[None, :]
    x1, x1p, routet = _mix(
        ga, gates, o1.reshape(n, B_WIDTH), o2, o3, l1.reshape(n, 128), l2, l3, x.reshape(n, d),
        w_a_proj[0].astype(BF16), w_b_proj[0].astype(BF16), w_o[0].astype(BF16), wr, br, ln1_g, ln1_b)

    tiles, nt, piece = _routing_tables(routet, n)
    xs = _sc_scatter_rows(x1p.reshape(n * SUBROWS, 128), piece, nt * TM_MOE * SUBROWS)
    ys = _moe(tiles, nt, xs.reshape(nt * TM_MOE // 8, 32, 128),
              w_gate[0].reshape(N_EXPERTS_TOTAL, d, D_EXPERT), w_up[0].reshape(N_EXPERTS_TOTAL, d, D_EXPERT),
              w_down[0].reshape(N_EXPERTS_TOTAL, D_EXPERT, d)).reshape(nt * TM_MOE * SUBROWS, 128)
    yg = _sc_gather_rows(ys, piece.reshape(-1))
    out = _final(yg.reshape(n // 4, 32, 128), x1, p[0].reshape(n, PLE_DIM), routet,
                 (0.5 * w_ple[0]).astype(BF16), (0.5 * w_ple_gate[0]).astype(BF16), ln2_g, ln2_b)
    return out.reshape(bsz, s, d)
```

```python
import functools

import jax
import jax.numpy as jnp
from jax import lax
from jax.experimental import pallas as pl
from jax.experimental.pallas import tpu as pltpu
from jax.experimental.pallas import tpu_sc as plsc

F32 = jnp.float32
BF16 = jnp.bfloat16
U32 = jnp.uint32
I32 = jnp.int32

D_MODEL = 1024
PLE_DIM = 256
A_WIDTH = 512
A_CHUNK = 128
B_HEAD_DIM = 64
B_HEADS = 8
B_WIDTH = 512
B_DILATIONS = (1, 4, 16)
SPAN = 128
N_GROUPS = 4
N_EXPERTS = 8
N_EXPERTS_TOTAL = N_GROUPS * N_EXPERTS
D_EXPERT = 256
DEEPNORM_ALPHA = 2.0 ** 0.25
LN_EPS = 1e-5
COL = 512
NEG = -1e30

VMEM_LIMIT = 56 * 1024 * 1024

TM_PROJ = 512
TM_ATTN = 1024
TM_MIX = 512
TM_MOE = 512
TM_FIN = 1024


def _ln(x, g, b):
    mu = jnp.mean(x, axis=-1, keepdims=True)
    xc = x - mu
    var = jnp.mean(xc * xc, axis=-1, keepdims=True)
    return xc * lax.rsqrt(var + LN_EPS) * g + b


def _gelu_tanh(x):
    return 0.5 * x * (1.0 + jnp.tanh(0.7978845608028654 * (x + 0.044715 * (x * x * x))))


def _sigmoid(x):
    return 0.5 * jnp.tanh(0.5 * x) + 0.5


def _dot(a, b):
    return jnp.dot(a, b, preferred_element_type=F32)


PACK_W = D_MODEL // 2
SUBROWS = PACK_W // 128


def _store_packed_rows(ref, x):
    m = x.shape[0]
    xb = x.astype(BF16).astype(F32)
    lo = pltpu.bitcast(xb[:, :PACK_W], U32) >> 16
    hi = pltpu.bitcast(xb[:, PACK_W:], U32) & jnp.uint32(0xFFFF0000)
    w = hi | lo
    for j in range(SUBROWS):
        ref[:, 8 * j:8 * (j + 1), :] = w[:, 128 * j:128 * (j + 1)].reshape(m // 8, 8, 128)


def _load_packed_rows(ref):
    m = ref.shape[0] * 8
    ws = [ref[:, 8 * j:8 * (j + 1), :].reshape(m, 128) for j in range(SUBROWS)]
    lo = [pltpu.bitcast(w << 16, F32) for w in ws]
    hi = [pltpu.bitcast(w & jnp.uint32(0xFFFF0000), F32) for w in ws]
    return jnp.concatenate(lo + hi, axis=1)


W_IN_BLOCKS = 15
W_IN_SLOTS = 4


def _proj_kernel(x_ref, w_hbm, lng_ref, lnb_ref, ws_ref, bias_ref,
                 ga_ref, gates_ref, qkv1_ref, qkv2_ref, qkv3_ref, xc_ref, w, wstage_ref, wsem):
    tm = x_ref.shape[0]

    @pl.when(pl.program_id(0) == 0)
    def _():
        def fetch(j):
            return pltpu.make_async_copy(w_hbm.at[:, pl.ds(j * COL, COL)], wstage_ref.at[j % W_IN_SLOTS],
                                         wsem.at[j % W_IN_SLOTS])

        for j in range(W_IN_SLOTS):
            fetch(j).start()
        for j in range(W_IN_BLOCKS):
            fetch(j).wait()
            w[j] = wstage_ref[j % W_IN_SLOTS].astype(BF16)
            if j + W_IN_SLOTS < W_IN_BLOCKS:
                fetch(j + W_IN_SLOTS).start()

    xb = x_ref[...].astype(BF16)

    u_raw = _dot(xb, w[0])
    v_raw = _dot(xb, w[1])

    for i in range(4):
        gates_ref[:, i * COL:(i + 1) * COL] = _sigmoid(_dot(xb, w[11 + i])).astype(BF16)
    for j in range(3):
        qkv1_ref[:, j * COL:(j + 1) * COL] = _dot(xb, w[2 + 3 * j]).astype(BF16)

    for c in range(D_MODEL // 128):
        xc_ref[c] = x_ref[:, c * 128:(c + 1) * 128]
    for gi, out_ref in ((1, qkv2_ref), (2, qkv3_ref)):
        dl = B_DILATIONS[gi]
        per = tm // dl
        xp = jnp.concatenate(
            [jnp.concatenate([xc_ref[c, pl.ds(r, per, stride=dl), :] for c in range(D_MODEL // 128)], axis=1)
             for r in range(dl)], axis=0).astype(BF16)
        for j in range(3):
            res = _dot(xp, w[2 + 3 * j + gi]).astype(BF16)
            for r in range(dl):
                out_ref[r, :, j * COL:(j + 1) * COL] = res[r * per:(r + 1) * per]

    u = _gelu_tanh(u_raw)
    v = _gelu_tanh(v_raw)
    vn = _ln(v, lng_ref[...], lnb_ref[...]).astype(BF16)

    row = lax.broadcasted_iota(I32, (A_CHUNK, A_CHUNK), 0)
    colm = lax.broadcasted_iota(I32, (A_CHUNK, A_CHUNK), 1)
    causal = colm <= row
    lo = colm < 64
    zero = jnp.zeros((A_CHUNK, A_CHUNK), BF16)
    wcat = []
    for j in range(4):
        w0 = jnp.where(causal, ws_ref[2 * j], 0.0).astype(BF16)
        w1 = jnp.where(causal, ws_ref[2 * j + 1], 0.0).astype(BF16)
        wcat.append(jnp.concatenate([w0, w1], axis=1))
    for c in range(tm // A_CHUNK):
        r0 = c * A_CHUNK
        for j in range(4):
            c0 = j * 128
            vt = vn[r0:r0 + A_CHUNK, c0:c0 + 128]
            rhs = jnp.concatenate([jnp.where(lo, vt, zero), jnp.where(lo, zero, vt)], axis=0)
            mixed = _dot(wcat[j], rhs) + bias_ref[:, c0:c0 + 128]
            ga_ref[r0:r0 + A_CHUNK, c0:c0 + 128] = (u[r0:r0 + A_CHUNK, c0:c0 + 128] * mixed).astype(BF16)


def _proj(x, w_in, a_ln_g, a_ln_b, a_ws, a_bias):
    bsz, s, _ = x.shape
    n = bsz * s
    tm = TM_PROJ
    tiles = s // tm
    x2 = x.reshape(n, D_MODEL)
    full = lambda shape: pl.BlockSpec(shape, lambda i: (0,) * len(shape))
    rows = lambda width: pl.BlockSpec((tm, width), lambda i: (i, 0))
    dil = lambda dl: pl.BlockSpec((None, dl, tm // dl, 3 * COL), lambda i: (i // tiles, 0, i % tiles, 0))
    return pl.pallas_call(
        _proj_kernel,
        grid=(n // tm,),
        in_specs=[rows(D_MODEL), pl.BlockSpec(memory_space=pl.ANY)]
                 + [full((1, A_WIDTH)), full((1, A_WIDTH)), full((8, A_CHUNK, A_CHUNK)), full((A_CHUNK, A_WIDTH))],
        out_specs=[rows(A_WIDTH), rows(4 * COL), rows(3 * COL), dil(4), dil(16)],
        out_shape=[jax.ShapeDtypeStruct((n, A_WIDTH), BF16),
                   jax.ShapeDtypeStruct((n, 4 * COL), BF16),
                   jax.ShapeDtypeStruct((n, 3 * COL), BF16),
                   jax.ShapeDtypeStruct((bsz, 4, s // 4, 3 * COL), BF16),
                   jax.ShapeDtypeStruct((bsz, 16, s // 16, 3 * COL), BF16)],
        scratch_shapes=[pltpu.VMEM((D_MODEL // 128, tm, 128), F32),
                        pltpu.VMEM((W_IN_BLOCKS, D_MODEL, COL), BF16),
                        pltpu.VMEM((W_IN_SLOTS, D_MODEL, COL), F32),
                        pltpu.SemaphoreType.DMA((W_IN_SLOTS,))],
        compiler_params=pltpu.CompilerParams(dimension_semantics=("arbitrary",), vmem_limit_bytes=VMEM_LIMIT),
        name="proj",
    )(x2, w_in, a_ln_g, a_ln_b, a_ws, a_bias)


def _attn_kernel(qkv_ref, o_ref, lse_ref, *, ns, seq):
    nb = seq // SPAN
    lane = lax.broadcasted_iota(I32, (SPAN, 128), 1)
    lo = lane < 64
    lane16 = lane // 16
    qi = lax.broadcasted_iota(I32, (SPAN, 2 * SPAN), 0)
    ki = lax.broadcasted_iota(I32, (SPAN, 2 * SPAN), 1)
    causal = lax.broadcasted_iota(I32, (SPAN, SPAN), 1) <= lax.broadcasted_iota(I32, (SPAN, SPAN), 0)
    bias_first = jnp.where(causal, 0.0, NEG).astype(F32)
    bias_first = jnp.concatenate([bias_first, bias_first], axis=0)
    bias_main = jnp.where((ki >= qi) & (ki <= qi + SPAN), 0.0, NEG).astype(F32)
    bias_main = jnp.concatenate([bias_main, bias_main], axis=0)
    zero = jnp.zeros((SPAN, 128), BF16)

    for s in range(ns):
        def block(row0, start, bias, s=s):
            win = bias.shape[1]
            pairs = range(B_HEADS // 2)
            scores, values = [], []
            for jp in pairs:
                c0 = jp * 128
                q = qkv_ref[s, pl.ds(row0, SPAN), c0:c0 + 128] * jnp.asarray(0.125, BF16)
                k = qkv_ref[s, pl.ds(start, win), COL + c0:COL + c0 + 128]
                values.append(qkv_ref[s, pl.ds(start, win), 2 * COL + c0:2 * COL + c0 + 128])
                qs = jnp.concatenate([jnp.where(lo, q, zero), jnp.where(lo, zero, q)], axis=0)
                scores.append(lax.dot_general(qs, k, (((1,), (1,)), ((), ())), preferred_element_type=F32) + bias)
            probs, maxes, sums = [], [], []
            for jp in pairs:
                m = jnp.max(scores[jp], axis=-1, keepdims=True)
                p = jnp.exp(scores[jp] - m)
                maxes.append(m)
                sums.append(jnp.sum(p, axis=-1, keepdims=True))
                probs.append(p.astype(BF16))
            lse_tile = jnp.zeros((SPAN, 128), F32)
            for jp in pairs:
                c0 = jp * 128
                ov = _dot(probs[jp], values[jp])
                inv = 1.0 / sums[jp]
                o = jnp.where(lo, ov[:SPAN] * inv[:SPAN], ov[SPAN:] * inv[SPAN:])
                o_ref[pl.ds(row0, SPAN), s * B_WIDTH + c0:s * B_WIDTH + c0 + 128] = o.astype(BF16)
                lse = maxes[jp] + jnp.log(sums[jp])
                lse_tile = jnp.where(lane16 == 2 * jp, lse[:SPAN],
                                     jnp.where(lane16 == 2 * jp + 1, lse[SPAN:], lse_tile))
            lse_ref[pl.ds(row0, SPAN), s * 128:(s + 1) * 128] = lse_tile

        block(0, 0, bias_first)
        if nb > 1:
            def body(i, carry):
                block(pl.multiple_of(i * SPAN, SPAN), pl.multiple_of((i - 1) * SPAN, SPAN), bias_main)
                return carry
            lax.fori_loop(1, nb, body, 0, unroll=min(5, nb - 1))


def _attn(qkv_g):
    bsz, dl, seq, _ = qkv_g.shape
    ns = max(1, min(dl, TM_ATTN // seq))
    return pl.pallas_call(
        functools.partial(_attn_kernel, ns=ns, seq=seq),
        grid=(bsz, dl // ns),
        in_specs=[pl.BlockSpec((None, ns, seq, 3 * COL), lambda b, r: (b, r, 0, 0))],
        out_specs=[pl.BlockSpec((None, seq, ns * B_WIDTH), lambda b, r: (b, 0, r)),
                   pl.BlockSpec((None, seq, ns * 128), lambda b, r: (b, 0, r))],
        out_shape=[jax.ShapeDtypeStruct((bsz, seq, dl * B_WIDTH), BF16),
                   jax.ShapeDtypeStruct((bsz, seq, dl * 128), F32)],
        compiler_params=pltpu.CompilerParams(dimension_semantics=("parallel", "parallel"),
                                             vmem_limit_bytes=VMEM_LIMIT),
        name=f"attn{dl}",
    )(qkv_g)


def _natural_rows(ref, dl, scr):
    nchunk, tm, _ = scr.shape
    w = nchunk * 128
    per = tm // dl
    for r in range(dl):
        for c in range(nchunk):
            scr[c, pl.ds(r, per, stride=dl), :] = ref[:, r * w + c * 128:r * w + (c + 1) * 128].astype(F32)
    return jnp.concatenate([scr[c] for c in range(nchunk)], axis=1)


def _mix_kernel(ga_ref, gates_ref, o1_ref, o2_ref, o3_ref, l1_ref, l2_ref, l3_ref, x_ref,
                wa_ref, wb_ref, wo_ref, wr_ref, br_ref, g1_ref, b1_ref,
                x1_ref, x1p_ref, routet_ref, o2s_ref, o3s_ref, l2s_ref, l3s_ref):
    tm = x_ref.shape[0]
    o2 = _natural_rows(o2_ref, 4, o2s_ref)
    o3 = _natural_rows(o3_ref, 16, o3s_ref)
    l2 = _natural_rows(l2_ref, 4, l2s_ref)
    l3 = _natural_rows(l3_ref, 16, l3s_ref)
    er = lax.broadcasted_iota(I32, (256, B_WIDTH), 0)
    ec = lax.broadcasted_iota(I32, (256, B_WIDTH), 1)
    expand = jnp.where(er % 128 == (ec // B_HEAD_DIM) * 16, 1.0, 0.0).astype(BF16)

    def widen(w):
        hi = w.astype(BF16)
        lo = (w - hi.astype(F32)).astype(BF16)
        return _dot(jnp.concatenate([hi, lo], axis=1), expand)

    h = tm // 2
    halves = (slice(0, h), slice(h, tm))
    obs = []
    for r in halves:
        l1 = l1_ref[r, :]
        mx = jnp.maximum(l1, jnp.maximum(l2[r], l3[r]))
        e1, e2, e3 = jnp.exp(l1 - mx), jnp.exp(l2[r] - mx), jnp.exp(l3[r] - mx)
        inv = 1.0 / (e1 + e2 + e3)
        obs.append(widen(e1 * inv) * o1_ref[r, :].astype(F32) + widen(e2 * inv) * o2[r] + widen(e3 * inv) * o3[r])
    ybs = [_dot(ob.astype(BF16), wb_ref[...]) for ob in obs]
    yas = [_dot(ga_ref[r, :], wa_ref[...]) for r in halves]
    pres = [gates_ref[r, :D_MODEL].astype(F32) * ya + gates_ref[r, D_MODEL:].astype(F32) * yb
            for r, ya, yb in zip(halves, yas, ybs)]
    mixes = [_dot(pre.astype(BF16), wo_ref[...]) for pre in pres]
    x1s = [_ln(DEEPNORM_ALPHA * x_ref[r, :] + mix, g1_ref[...], b1_ref[...]) for r, mix in zip(halves, mixes)]
    logits = [_dot(x1.astype(BF16), wr_ref[...]) + br_ref[...] for x1 in x1s]

    nrow = 40
    row = lax.broadcasted_iota(I32, (nrow, h), 0).astype(F32)
    row8 = lax.broadcasted_iota(I32, (8, h), 0)
    big = 1e9
    for i, r in enumerate(halves):
        x1_ref[r, :] = x1s[i]
        _store_packed_rows(x1p_ref.at[i * h // 8:(i + 1) * h // 8], x1s[i])
        lg = logits[i].T[:nrow, :]
        gl = jnp.where(row < N_GROUPS, lg, NEG)
        gm = jnp.max(gl, axis=0, keepdims=True)
        gidx = jnp.min(jnp.where(gl == gm, row, big), axis=0, keepdims=True)
        gsum = jnp.sum(jnp.where(row < N_GROUPS, jnp.exp(gl - gm), 0.0), axis=0, keepdims=True)
        gprob = 1.0 / gsum
        lo_row = N_GROUPS + N_EXPERTS * gidx
        el = jnp.where((row >= lo_row) & (row < lo_row + N_EXPERTS), lg, NEG)
        v1 = jnp.max(el, axis=0, keepdims=True)
        i1 = jnp.min(jnp.where(el == v1, row, big), axis=0, keepdims=True)
        el2 = jnp.where(row == i1, NEG, el)
        v2 = jnp.max(el2, axis=0, keepdims=True)
        i2 = jnp.min(jnp.where(el2 == v2, row, big), axis=0, keepdims=True)
        t = jnp.exp(v2 - v1)
        w1 = 1.0 / (1.0 + t)
        w2 = t * w1
        routet_ref[:, r] = jnp.where(row8 == 0, i1 - N_GROUPS,
                                     jnp.where(row8 == 1, i2 - N_GROUPS,
                                               jnp.where(row8 == 2, gprob * w1,
                                                         jnp.where(row8 == 3, gprob * w2, 0.0))))


def _mix(ga, gates, o1, o2, o3, l1, l2, l3, x2, wa, wb, wo, wr, br, g1, b1):
    n = x2.shape[0]
    bsz = o2.shape[0]
    tm = TM_MIX
    tiles = n // bsz // tm
    rows = lambda w: pl.BlockSpec((tm, w), lambda i: (i, 0))
    grouped = lambda a, dl: pl.BlockSpec((None, tm // dl, a.shape[2]), lambda i: (i // tiles, i % tiles, 0))
    full = lambda a: pl.BlockSpec(a.shape, lambda i: (0,) * a.ndim)
    return pl.pallas_call(
        _mix_kernel,
        grid=(n // tm,),
        in_specs=[rows(A_WIDTH), rows(2 * D_MODEL), rows(B_WIDTH), grouped(o2, 4), grouped(o3, 16),
                  rows(128), grouped(l2, 4), grouped(l3, 16), rows(D_MODEL),
                  full(wa), full(wb), full(wo), full(wr), full(br), full(g1), full(b1)],
        out_specs=[rows(D_MODEL), pl.BlockSpec((tm // 8, 32, 128), lambda i: (i, 0, 0)),
                   pl.BlockSpec((8, tm), lambda i: (0, i))],
        out_shape=[jax.ShapeDtypeStruct((n, D_MODEL), F32),
                   jax.ShapeDtypeStruct((n // 8, 32, 128), U32),
                   jax.ShapeDtypeStruct((8, n), F32)],
        scratch_shapes=[pltpu.VMEM((B_WIDTH // 128, tm, 128), F32), pltpu.VMEM((B_WIDTH // 128, tm, 128), F32),
                        pltpu.VMEM((1, tm, 128), F32), pltpu.VMEM((1, tm, 128), F32)],
        compiler_params=pltpu.CompilerParams(dimension_semantics=("parallel",), vmem_limit_bytes=VMEM_LIMIT),
        name="mix",
    )(ga, gates, o1, o2, o3, l1, l2, l3, x2, wa, wb, wo, wr, br, g1, b1)


SC_WINDOW = 128


def _sc_mesh():
    return plsc.VectorSubcoreMesh(core_axis_name="core", subcore_axis_name="subcore")


def _sc_scatter_rows(rows, dst, n_out):
    r = rows.shape[0]
    k = 2

    @pl.kernel(out_type=jax.ShapeDtypeStruct((n_out, 128), rows.dtype), mesh=_sc_mesh(),
               scratch_types=[pltpu.SemaphoreType.DMA((2 * k,))])
    def scatter(rows_hbm, dst0_hbm, dst1_hbm, out_hbm, sems):
        def body(rows_vmem, dst0_vmem, dst1_vmem):
            copies = []
            for j in range(k):
                window = rows_vmem.at[pl.ds(j * SC_WINDOW, SC_WINDOW)]
                copies.append(pltpu.async_copy(window, out_hbm.at[dst0_vmem.at[j]], sems.at[2 * j]))
                copies.append(pltpu.async_copy(window, out_hbm.at[dst1_vmem.at[j]], sems.at[2 * j + 1]))
            for c in copies:
                c.wait()

        pltpu.emit_pipeline(
            body,
            grid=(r // (k * SC_WINDOW),),
            in_specs=[pl.BlockSpec((k * SC_WINDOW, 128), lambda i: (i, 0)),
                      pl.BlockSpec((k, SC_WINDOW), lambda i: (i, 0)),
                      pl.BlockSpec((k, SC_WINDOW), lambda i: (i, 0))],
            out_specs=[],
            core_axis_name=("core", "subcore"),
            dimension_semantics=(pltpu.PARALLEL,),
        )(rows_hbm, dst0_hbm, dst1_hbm)

    return scatter(rows, dst[0].reshape(r // SC_WINDOW, SC_WINDOW), dst[1].reshape(r // SC_WINDOW, SC_WINDOW))


def _sc_gather_rows(table, src):
    m = src.shape[0]
    k = 2

    @pl.kernel(out_type=jax.ShapeDtypeStruct((m, 128), table.dtype), mesh=_sc_mesh(),
               scratch_types=[pltpu.SemaphoreType.DMA((k,))])
    def gather(table_hbm, src_hbm, out_hbm, sems):
        def body(src_vmem, out_vmem):
            copies = [pltpu.async_copy(table_hbm.at[src_vmem.at[j]], out_vmem.at[pl.ds(j * SC_WINDOW, SC_WINDOW)],
                                       sems.at[j]) for j in range(k)]
            for c in copies:
                c.wait()

        pltpu.emit_pipeline(
            body,
            grid=(m // (k * SC_WINDOW),),
            in_specs=[pl.BlockSpec((k, SC_WINDOW), lambda i: (i, 0))],
            out_specs=[pl.BlockSpec((k * SC_WINDOW, 128), lambda i: (i, 0))],
            core_axis_name=("core", "subcore"),
            dimension_semantics=(pltpu.PARALLEL,),
        )(src_hbm, out_hbm)

    return gather(table, src.reshape(m // SC_WINDOW, SC_WINDOW))


MOE_X_SLOTS = 3
MOE_Y_SLOTS = 2


def _moe_kernel(tiles_ref, xs_hbm, wg_hbm, wu_hbm, wd_hbm, ys_hbm,
                xbuf, ybuf, wgf_ref, wuf_ref, wdf_ref, wgb_ref, wub_ref, wdb_ref, xsem, ysem, wsem, *, nt):
    te_ref, tv_ref, ws_ref, nx_ref = (tiles_ref.at[i] for i in range(4))
    t = pl.program_id(0)
    grp = TM_MOE // 8

    def fetch_x(j):
        return pltpu.make_async_copy(xs_hbm.at[pl.ds(j * grp, grp)], xbuf.at[j % MOE_X_SLOTS],
                                     xsem.at[j % MOE_X_SLOTS])

    def store_y(j):
        return pltpu.make_async_copy(ybuf.at[j % MOE_Y_SLOTS], ys_hbm.at[pl.ds(j * grp, grp)],
                                     ysem.at[j % MOE_Y_SLOTS])

    def fetch_w(e, slot):
        return [pltpu.make_async_copy(w.at[e], buf.at[slot], wsem.at[slot, i])
                for i, (w, buf) in enumerate(((wg_hbm, wgf_ref), (wu_hbm, wuf_ref), (wd_hbm, wdf_ref)))]

    @pl.when(t == 0)
    def _():
        for c in fetch_w(te_ref[0], 0):
            c.start()
        for j in range(MOE_X_SLOTS - 1):
            @pl.when(tv_ref[j] == 1)
            def _():
                fetch_x(j).start()

    ahead = t + (MOE_X_SLOTS - 1)

    @pl.when((ahead < nt) & (tv_ref[jnp.minimum(ahead, nt - 1)] == 1))
    def _():
        fetch_x(ahead).start()

    slot = ws_ref[t]

    @pl.when((t == 0) | (te_ref[t] != te_ref[jnp.maximum(t - 1, 0)]))
    def _():
        @pl.when(nx_ref[t] >= 0)
        def _():
            for c in fetch_w(nx_ref[t], 1 - slot):
                c.start()

        for c in fetch_w(te_ref[t], slot):
            c.wait()
        wgb_ref[...] = wgf_ref[slot].astype(BF16)
        wub_ref[...] = wuf_ref[slot].astype(BF16)
        wdb_ref[...] = wdf_ref[slot].astype(BF16)

    @pl.when((t >= MOE_Y_SLOTS) & (tv_ref[jnp.maximum(t - MOE_Y_SLOTS, 0)] == 1))
    def _():
        store_y(t - MOE_Y_SLOTS).wait()

    @pl.when(tv_ref[t] == 1)
    def _():
        fetch_x(t).wait()
        xb = _load_packed_rows(xbuf.at[t % MOE_X_SLOTS]).astype(BF16)
        g = _dot(xb, wgb_ref[...])
        u = _dot(xb, wub_ref[...])
        h = (g * _sigmoid(g) * u).astype(BF16)
        _store_packed_rows(ybuf.at[t % MOE_Y_SLOTS], _dot(h, wdb_ref[...]))
        store_y(t).start()

    @pl.when(t == nt - 1)
    def _():
        for j in range(nt - MOE_Y_SLOTS, nt):
            @pl.when(tv_ref[j] == 1)
            def _():
                store_y(j).wait()


def _moe(tiles, nt, xs, wg, wu, wd):
    tm = TM_MOE
    any_spec = pl.BlockSpec(memory_space=pl.ANY)
    return pl.pallas_call(
        functools.partial(_moe_kernel, nt=nt),
        grid_spec=pltpu.PrefetchScalarGridSpec(
            num_scalar_prefetch=1,
            grid=(nt,),
            in_specs=[any_spec] * 4,
            out_specs=any_spec,
            scratch_shapes=[pltpu.VMEM((MOE_X_SLOTS, tm // 8, 32, 128), U32),
                            pltpu.VMEM((MOE_Y_SLOTS, tm // 8, 32, 128), U32),
                            pltpu.VMEM((2, D_MODEL, D_EXPERT), F32), pltpu.VMEM((2, D_MODEL, D_EXPERT), F32),
                            pltpu.VMEM((2, D_EXPERT, D_MODEL), F32),
                            pltpu.VMEM((D_MODEL, D_EXPERT), BF16), pltpu.VMEM((D_MODEL, D_EXPERT), BF16),
                            pltpu.VMEM((D_EXPERT, D_MODEL), BF16),
                            pltpu.SemaphoreType.DMA((MOE_X_SLOTS,)), pltpu.SemaphoreType.DMA((MOE_Y_SLOTS,)),
                            pltpu.SemaphoreType.DMA((2, 3))]),
        out_shape=jax.ShapeDtypeStruct((nt * tm // 8, 32, 128), U32),
        compiler_params=pltpu.CompilerParams(dimension_semantics=("arbitrary",), vmem_limit_bytes=VMEM_LIMIT),
        name="moe",
    )(tiles, xs, wg, wu, wd)


def _final_kernel(y0_ref, y1_ref, x1_ref, p_ref, routet_ref, wple_ref, wpg_ref, g2_ref, b2_ref, out_ref):
    tm = x1_ref.shape[0]
    route = routet_ref[...].T
    h = tm // 4
    parts = tuple(slice(i * h, (i + 1) * h) for i in range(4))
    x1s = [x1_ref[r, :] for r in parts]
    plins = [_dot(p_ref[r, :].astype(BF16), wple_ref[...]) for r in parts]
    gates = [_dot(x1.astype(BF16), wpg_ref[...]) for x1 in x1s]
    sums = []
    for i, r in enumerate(parts):
        g8 = slice(i * h // 8, (i + 1) * h // 8)
        sums.append(DEEPNORM_ALPHA * x1s[i] + route[r, 2:3] * _load_packed_rows(y0_ref.at[g8])
                    + route[r, 3:4] * _load_packed_rows(y1_ref.at[g8]))
    for i, r in enumerate(parts):
        out_ref[r, :] = _ln((sums[i] + plins[i]) + plins[i] * jnp.tanh(gates[i]), g2_ref[...], b2_ref[...])


def _final(y0, y1, x1, p2, routet, wple, wpg, g2, b2):
    n = x1.shape[0]
    tm = TM_FIN
    nt = n // tm
    rows = lambda w: pl.BlockSpec((tm, w), lambda t: (t, 0))
    full = lambda a: pl.BlockSpec(a.shape, lambda t: (0,) * a.ndim)
    return pl.pallas_call(
        _final_kernel,
        grid=(nt,),
        in_specs=[pl.BlockSpec((tm // 8, 32, 128), lambda t: (t, 0, 0)),
                  pl.BlockSpec((tm // 8, 32, 128), lambda t: (t, 0, 0)),
                  rows(D_MODEL), rows(PLE_DIM), pl.BlockSpec((8, tm), lambda t: (0, t)),
                  full(wple), full(wpg), full(g2), full(b2)],
        out_specs=rows(D_MODEL),
        out_shape=jax.ShapeDtypeStruct((n, D_MODEL), F32),
        compiler_params=pltpu.CompilerParams(dimension_semantics=("parallel",), vmem_limit_bytes=VMEM_LIMIT),
        name="final",
    )(y0, y1, x1, p2, routet, wple, wpg, g2, b2)


def _route_tables_kernel(e_ref, piece_ref, tab_ref):
    r = e_ref.shape[0]
    e = e_ref[...]
    ri = lax.broadcasted_iota(I32, (128, 128), 0)
    ci = lax.broadcasted_iota(I32, (128, 128), 1)
    upper = jnp.where(ri <= ci, 1.0, 0.0).astype(BF16)
    rr = lax.broadcasted_iota(I32, (r, r), 0)
    rc = lax.broadcasted_iota(I32, (r, r), 1)
    below = jnp.where(rc < rr, 1.0, 0.0).astype(BF16)
    lane = lax.broadcasted_iota(I32, (1, 128), 1)

    experts = range(N_EXPERTS_TOTAL)
    ms = [jnp.where(e == x, 1.0, 0.0) for x in experts]
    pres = [_dot(m.astype(BF16), upper) for m in ms]
    tots = [jnp.broadcast_to(pre[:, 127:128], (r, 128)) for pre in pres]
    offs_rows = [_dot(below, tot.astype(BF16)) for tot in tots]
    rank = jnp.zeros((r, 128), F32)
    counts = jnp.zeros((1, 128), F32)
    for x in experts:
        rank = rank + ms[x] * (pres[x] + offs_rows[x])
        counts = jnp.where(lane == x, offs_rows[x][r - 1:r, :] + tots[x][r - 1:r, :], counts)
    padded = jnp.floor((counts + (TM_MOE - 1)) * (1.0 / TM_MOE)) * TM_MOE
    ends = _dot(jnp.broadcast_to(padded, (8, 128)).astype(BF16), upper)[0:1, :]
    offs = ends - padded

    nt = r * 128 // TM_MOE + N_EXPERTS_TOTAL
    ones = jnp.ones((128, 128), BF16)
    lanef = lane.astype(F32)
    diag = ri == ci
    ends_t = ends * (1.0 / TM_MOE)
    ecol = _dot(jnp.where(diag, jnp.broadcast_to(ends_t, (128, 128)), 0.0).astype(BF16), ones)
    passed = jnp.where((ecol <= ci.astype(F32)) & (ri < N_EXPERTS_TOTAL), 1.0, 0.0)
    te = jnp.minimum(jnp.sum(passed, axis=0, keepdims=True), N_EXPERTS_TOTAL - 1.0)
    tv = jnp.where(lanef < ecol[N_EXPERTS_TOTAL - 1:N_EXPERTS_TOTAL, :], 1.0, 0.0)
    shift = jnp.where(ri + 1 == ci, 1.0, 0.0).astype(BF16)
    te_prev = _dot(jnp.broadcast_to(te, (8, 128)).astype(BF16), shift)[0:1, :]
    first = jnp.where((lane == 0) | (te != te_prev), 1.0, 0.0)
    run = _dot(jnp.broadcast_to(first, (8, 128)).astype(BF16), upper)[0:1, :] - 1.0
    ws = run - 2.0 * jnp.floor(run * 0.5)
    tcol = _dot(jnp.where(diag, jnp.broadcast_to(te, (128, 128)), 0.0).astype(BF16), ones)
    later = jnp.where((tcol > te) & (ri < nt), tcol, float(N_EXPERTS_TOTAL))
    nx = jnp.min(later, axis=0, keepdims=True)
    nx = jnp.where(nx == N_EXPERTS_TOTAL, -1.0, nx)
    row8 = lax.broadcasted_iota(I32, (8, 128), 0)
    tab = jnp.where(row8 == 0, te, jnp.where(row8 == 1, tv, jnp.where(row8 == 2, ws, jnp.where(row8 == 3, nx, 0.0))))
    tab_ref[...] = tab.astype(I32)

    pos = rank - 1.0
    for x in range(N_EXPERTS_TOTAL):
        pos = pos + jnp.where(e == x, offs[:, x:x + 1], 0.0)

    hi = jnp.floor(pos * (1.0 / 256.0))
    lo = pos - 256.0 * hi
    jv = ((lane % 32) // 8).astype(F32)
    for c in range(4):
        sel = jnp.where(ri == 32 * c + 8 * (ci // 32) + ci % 8, 1.0, 0.0).astype(BF16)
        pc = 256.0 * _dot(hi.astype(BF16), sel) + _dot(lo.astype(BF16), sel)
        p8 = jnp.floor(pc * 0.125)
        piece = p8 * (8.0 * SUBROWS) + (pc - 8.0 * p8) + 8.0 * jv
        piece_ref[pl.ds(c, r, stride=4), :] = piece.astype(I32)


def _routing_tables(routet, n):
    nt = (2 * n) // TM_MOE + N_EXPERTS_TOTAL
    assert nt <= 128, "the tile tables hold one tile per lane"
    r = 2 * n // 128
    piece, tiles = pl.pallas_call(
        _route_tables_kernel,
        out_shape=[jax.ShapeDtypeStruct((4 * r, 128), I32), jax.ShapeDtypeStruct((8, 128), I32)],
        compiler_params=pltpu.CompilerParams(vmem_limit_bytes=VMEM_LIMIT),
        name="route_tables",
    )(routet[0:2].reshape(r, 128))
    return tiles, nt, piece.reshape(2, n * SUBROWS)


def kernel(x, p, w_in, a_ln_g, a_ln_b, a_ws, a_bs, w_a_proj, w_b_proj, w_o, ln1_g, ln1_b, w_group_router,
           b_group_router, w_expert_router, b_expert_router, w_gate, w_up, w_down, w_ple, w_ple_gate,
           ln2_g, ln2_b):
    bsz, s, d = x.shape
    n = bsz * s
    assert d == D_MODEL and s % (SPAN * max(B_DILATIONS)) == 0 and n % TM_PROJ == 0
    assert w_in.shape[0] == 1, "one layer"

    a_bias = jnp.repeat(a_bs[0].T, A_WIDTH // 8, axis=1)

    ga, gates, qkv1, qkv2, qkv3 = _proj(x, w_in[0], a_ln_g, a_ln_b, a_ws[0], a_bias)
    o1, l1 = _attn(qkv1.reshape(bsz, 1, s, 3 * COL))
    o2, l2 = _attn(qkv2)
    o3, l3 = _attn(qkv3)

    pad = 128 - N_GROUPS - N_EXPERTS_TOTAL
    wr = jnp.concatenate([w_group_router[0], w_expert_router[0].reshape(d, N_EXPERTS_TOTAL),
                          jnp.zeros((d, pad), F32)], axis=1).astype(BF16)
    br = jnp.concatenate([b_group_router[0], b_expert_router[0].reshape(-1), jnp.zeros((pad,), F32)])[None, :]
    x1, x1p, routet = _mix(
        ga, gates, o1.reshape(n, B_WIDTH), o2, o3, l1.reshape(n, 128), l2, l3, x.reshape(n, d),
        w_a_proj[0].astype(BF16), w_b_proj[0].astype(BF16), w_o[0].astype(BF16), wr, br, ln1_g, ln1_b)

    tiles, nt, piece = _routing_tables(routet, n)
    xs = _sc_scatter_rows(x1p.reshape(n * SUBROWS, 128), piece, nt * TM_MOE * SUBROWS)
    ys = _moe(tiles, nt, xs.reshape(nt * TM_MOE // 8, 32, 128),
              w_gate[0].reshape(N_EXPERTS_TOTAL, d, D_EXPERT), w_up[0].reshape(N_EXPERTS_TOTAL, d, D_EXPERT),
              w_down[0].reshape(N_EXPERTS_TOTAL, D_EXPERT, d)).reshape(nt * TM_MOE * SUBROWS, 128)
    yg = [_sc_gather_rows(ys, piece[k]).reshape(n // 8, 32, 128) for k in range(2)]
    out = _final(yg[0], yg[1], x1, p[0].reshape(n, PLE_DIM), routet,
                 (0.5 * w_ple[0]).astype(BF16), (0.5 * w_ple_gate[0]).astype(BF16), ln2_g, ln2_b)
    return out.reshape(bsz, s, d)
```
